```python
import jax, jax.numpy as jnp
from jax import lax
import numpy as np

D_MODEL = 1024
BATCH = 32
SEQ = 2048
DEPTH = 2

HEAD_DIM = 64
W_FOX = (D_MODEL * 3 // 8) // HEAD_DIM * HEAD_DIM
N_HEADS_FOX = W_FOX // HEAD_DIM
CONV_CH = D_MODEL // 4
W_DIL = D_MODEL - W_FOX - CONV_CH
N_HEADS_DIL = W_DIL // HEAD_DIM
DILATION_PAIRS = ((128, 1), (512, 4), (2048, 16))
CONV_K = 31
FFN_CONV_K = 3
D_FF = ((8 * D_MODEL // 3 + 127) // 128) * 128
Q_BLOCK = 128
EPS = 1e-6
OFF_QA = 0
OFF_KA = OFF_QA + W_FOX
OFF_VA = OFF_KA + W_FOX
OFF_FA = OFF_VA + W_FOX
OFF_QB = OFF_FA + N_HEADS_FOX
OFF_KB = OFF_QB + W_DIL
OFF_VB = OFF_KB + W_DIL
OFF_GV = OFF_VB + W_DIL
OFF_GG = OFF_GV + CONV_CH
N_IN = OFF_GG + CONV_CH

kernel_name = 'hybrid_fox_dilated_conformer_convffn'


def rmsnorm(x, g):
    xf = x.astype(jnp.float32)
    y = xf * lax.rsqrt(jnp.mean(xf * xf, axis=-1, keepdims=True) + EPS)
    return (y * g.astype(jnp.float32)).astype(x.dtype)


def layernorm(x, g, b):
    xf = x.astype(jnp.float32)
    mu = jnp.mean(xf, axis=-1, keepdims=True)
    xc = xf - mu
    y = xc * lax.rsqrt(jnp.mean(xc * xc, axis=-1, keepdims=True) + EPS)
    return (y * g.astype(jnp.float32) + b.astype(jnp.float32)).astype(x.dtype)


def causal_dwconv(x, w, b):
    k, c = w.shape
    y = lax.conv_general_dilated(x, w[:, None, :].astype(x.dtype), window_strides=(1,),
                                 padding=[(k - 1, 0)], dimension_numbers=('NWC', 'WIO', 'NWC'),
                                 feature_group_count=c)
    return y + b.astype(x.dtype)


def forgetting_attention(q, k, v, log_f):
    B, S, H, Dh = q.shape
    nb = S // Q_BLOCK
    c = jnp.cumsum(log_f, axis=1)
    c_k = jnp.transpose(c, (0, 2, 1))[:, :, None, :]
    qb = q.reshape(B, nb, Q_BLOCK, H, Dh).swapaxes(0, 1)
    cb = c.reshape(B, nb, Q_BLOCK, H).swapaxes(0, 1)
    kpos = jnp.arange(S)
    scale = Dh ** -0.5

    def block(args):
        qi, ci, i = args
        s = jnp.einsum('bqhd,bkhd->bhqk', qi, k, preferred_element_type=jnp.float32) * scale
        s = s + jnp.transpose(ci, (0, 2, 1))[..., None] - c_k
        qpos = i * Q_BLOCK + jnp.arange(Q_BLOCK)
        s = jnp.where(kpos[None, :] <= qpos[:, None], s, -jnp.inf)
        p = jax.nn.softmax(s, axis=-1)
        return jnp.einsum('bhqk,bkhd->bqhd', p.astype(v.dtype), v)

    o = lax.map(block, (qb, cb, jnp.arange(nb)))
    return o.swapaxes(0, 1).reshape(B, S, H, Dh)


def dilated_branch(q, k, v, window, dilation):
    B, S, H, Dh = q.shape
    blk = window // dilation
    span = blk * dilation
    sp = -(-S // span) * span
    nb = sp // span
    pad = ((0, 0), (0, sp - S), (0, 0), (0, 0))

    def strided(t):
        return jnp.pad(t, pad).reshape(B, nb, blk, dilation, H, Dh)

    qs, ks, vs = strided(q), strided(k), strided(v)
    def with_prev(t):
        prev = jnp.concatenate([jnp.zeros_like(t[:, :1]), t[:, :-1]], axis=1)
        return jnp.concatenate([prev, t], axis=2)
    kc, vc = with_prev(ks), with_prev(vs)
    s = jnp.einsum('bnqrhd,bnkrhd->bnrhqk', qs, kc, preferred_element_type=jnp.float32) * (Dh ** -0.5)
    qi = blk + jnp.arange(blk)
    ki = jnp.arange(2 * blk)
    delta = qi[:, None] - ki[None, :]
    band = (delta >= 0) & (delta <= blk)
    first = (jnp.arange(nb) == 0)[:, None, None] & (ki < blk)[None, None, :]
    valid = band[None] & ~first
    s = jnp.where(valid[None, :, None, None], s, -jnp.inf)
    m = jnp.max(s, axis=-1, keepdims=True)
    e = jnp.exp(s - m)
    l = jnp.sum(e, axis=-1)
    num = jnp.einsum('bnrhqk,bnkrhd->bnqrhd', e.astype(v.dtype), vc,
                     preferred_element_type=jnp.float32)
    num = num.reshape(B, sp, H, Dh)[:, :S]
    def to_seq(t):
        return jnp.transpose(t, (0, 1, 4, 2, 3)).reshape(B, sp, H)[:, :S]
    return num, to_seq(m[..., 0]), to_seq(l)


def dilated_mixture(q, k, v):
    parts = [dilated_branch(q, k, v, w, d) for (w, d) in DILATION_PAIRS]
    m_all = parts[0][1]
    for _, m, _ in parts[1:]:
        m_all = jnp.maximum(m_all, m)
    num = 0.0
    den = 0.0
    for n_p, m_p, l_p in parts:
        a = jnp.exp(m_p - m_all)
        num = num + a[..., None] * n_p
        den = den + a * l_p
    return (num / den[..., None]).astype(q.dtype)


def _fwd_setup_inputs(seed: int = 0) -> dict:
    key = jax.random.key(seed)
    ks = jax.random.split(key, 20)
    f32 = jnp.float32
    def nrm(k, shape, scale):
        return jax.random.normal(k, shape, f32) * scale
    return {
        'x': nrm(ks[0], (BATCH, SEQ, D_MODEL), 1.0),
        'ln1_g': 1.0 + nrm(ks[1], (DEPTH, D_MODEL), 0.02),
        'w_in': nrm(ks[2], (DEPTH, D_MODEL, N_IN), D_MODEL ** -0.5),
        'b_forget': jax.random.uniform(ks[3], (DEPTH, N_HEADS_FOX), f32, 1.0, 4.0),
        'g_out_fox': 1.0 + nrm(ks[4], (DEPTH, W_FOX), 0.02),
        'g_out_dil': 1.0 + nrm(ks[5], (DEPTH, W_DIL), 0.02),
        'conv_w': nrm(ks[6], (DEPTH, CONV_K, CONV_CH), CONV_K ** -0.5),
        'conv_b': nrm(ks[7], (DEPTH, CONV_CH), 0.02),
        'cnorm_g': 1.0 + nrm(ks[8], (DEPTH, CONV_CH), 0.02),
        'cnorm_b': nrm(ks[9], (DEPTH, CONV_CH), 0.02),
        'w_o': nrm(ks[10], (DEPTH, D_MODEL, D_MODEL), D_MODEL ** -0.5),
        'ln2_g': 1.0 + nrm(ks[11], (DEPTH, D_MODEL), 0.02),
        'w_up': nrm(ks[12], (DEPTH, D_MODEL, 2 * D_FF), D_MODEL ** -0.5),
        'ffn_conv_w': nrm(ks[13], (DEPTH, FFN_CONV_K, 2 * D_FF), FFN_CONV_K ** -0.5),
        'ffn_conv_b': nrm(ks[14], (DEPTH, 2 * D_FF), 0.02),
        'w_down': nrm(ks[15], (DEPTH, D_FF, D_MODEL), D_FF ** -0.5),
        'g_final': 1.0 + nrm(ks[16], (D_MODEL,), 0.02),
    }


def _fwd_reference(x, ln1_g, w_in, b_forget, g_out_fox, g_out_dil, conv_w, conv_b, cnorm_g, cnorm_b,
              w_o, ln2_g, w_up, ffn_conv_w, ffn_conv_b, w_down, g_final):
    B, S, _ = x.shape
    for l in range(DEPTH):
        h = rmsnorm(x, ln1_g[l])
        p = jnp.einsum('bsd,dn->bsn', h, w_in[l])
        qa = p[..., OFF_QA:OFF_KA].reshape(B, S, N_HEADS_FOX, HEAD_DIM)
        ka = p[..., OFF_KA:OFF_VA].reshape(B, S, N_HEADS_FOX, HEAD_DIM)
        va = p[..., OFF_VA:OFF_FA].reshape(B, S, N_HEADS_FOX, HEAD_DIM)
        log_f = jax.nn.log_sigmoid(p[..., OFF_FA:OFF_QB].astype(jnp.float32)
                                   + b_forget[l].astype(jnp.float32))
        ya = forgetting_attention(qa, ka, va, log_f).reshape(B, S, W_FOX)
        ya = rmsnorm(ya, g_out_fox[l])
        qb = p[..., OFF_QB:OFF_KB].reshape(B, S, N_HEADS_DIL, HEAD_DIM)
        kb = p[..., OFF_KB:OFF_VB].reshape(B, S, N_HEADS_DIL, HEAD_DIM)
        vb = p[..., OFF_VB:OFF_GV].reshape(B, S, N_HEADS_DIL, HEAD_DIM)
        yb = dilated_mixture(qb, kb, vb).reshape(B, S, W_DIL)
        yb = rmsnorm(yb, g_out_dil[l])
        yc = p[..., OFF_GV:OFF_GG] * jax.nn.sigmoid(p[..., OFF_GG:N_IN])
        yc = causal_dwconv(yc, conv_w[l], conv_b[l])
        yc = jax.nn.silu(layernorm(yc, cnorm_g[l], cnorm_b[l]))
        y = jnp.concatenate([ya, yb, yc], axis=-1)
        x = x + jnp.einsum('bsd,de->bse', y, w_o[l])
        h2 = rmsnorm(x, ln2_g[l])
        u = jnp.einsum('bsd,df->bsf', h2, w_up[l])
        u = causal_dwconv(u, ffn_conv_w[l], ffn_conv_b[l])
        hidden = jax.nn.silu(u[..., :D_FF]) * u[..., D_FF:]
        x = x + jnp.einsum('bsf,fd->bsd', hidden, w_down[l])
    return rmsnorm(x, g_final)


import jax as _jax
import jax.numpy as _jnp

TWIN_FORMAT = 'train_step'
FWD_PARAMS = ['x', 'ln1_g', 'w_in', 'b_forget', 'g_out_fox', 'g_out_dil', 'conv_w', 'conv_b', 'cnorm_g', 'cnorm_b', 'w_o', 'ln2_g', 'w_up', 'ffn_conv_w', 'ffn_conv_b', 'w_down', 'g_final']
TWIN_WEIGHTS = ['ln1_g', 'w_in', 'b_forget', 'g_out_fox', 'g_out_dil', 'conv_w', 'conv_b', 'cnorm_g', 'cnorm_b', 'w_o', 'ln2_g', 'w_up', 'ffn_conv_w', 'ffn_conv_b', 'w_down', 'g_final']
TWIN_DIFF_INPUT = 'x'
TWIN_INPUTS = ['x', 'ln1_g', 'w_in', 'b_forget', 'g_out_fox', 'g_out_dil', 'conv_w', 'conv_b', 'cnorm_g', 'cnorm_b', 'w_o', 'ln2_g', 'w_up', 'ffn_conv_w', 'ffn_conv_b', 'w_down', 'g_final', 'loss_target', 'm_ln1_g', 'm_w_in', 'm_b_forget', 'm_g_out_fox', 'm_g_out_dil', 'm_conv_w', 'm_conv_b', 'm_cnorm_g', 'm_cnorm_b', 'm_w_o', 'm_ln2_g', 'm_w_up', 'm_ffn_conv_w', 'm_ffn_conv_b', 'm_w_down', 'm_g_final', 'v_ln1_g', 'v_w_in', 'v_b_forget', 'v_g_out_fox', 'v_g_out_dil', 'v_conv_w', 'v_conv_b', 'v_cnorm_g', 'v_cnorm_b', 'v_w_o', 'v_ln2_g', 'v_w_up', 'v_ffn_conv_w', 'v_ffn_conv_b', 'v_w_down', 'v_g_final']
TWIN_OUTPUTS = ['loss', 'grad_x', 'grad_ln1_g', 'grad_w_in', 'grad_b_forget', 'grad_g_out_fox', 'grad_g_out_dil', 'grad_conv_w', 'grad_conv_b', 'grad_cnorm_g', 'grad_cnorm_b', 'grad_w_o', 'grad_ln2_g', 'grad_w_up', 'grad_ffn_conv_w', 'grad_ffn_conv_b', 'grad_w_down', 'grad_g_final', 'delta_ln1_g', 'delta_w_in', 'delta_b_forget', 'delta_g_out_fox', 'delta_g_out_dil', 'delta_conv_w', 'delta_conv_b', 'delta_cnorm_g', 'delta_cnorm_b', 'delta_w_o', 'delta_ln2_g', 'delta_w_up', 'delta_ffn_conv_w', 'delta_ffn_conv_b', 'delta_w_down', 'delta_g_final', 'new_m_ln1_g', 'new_m_w_in', 'new_m_b_forget', 'new_m_g_out_fox', 'new_m_g_out_dil', 'new_m_conv_w', 'new_m_conv_b', 'new_m_cnorm_g', 'new_m_cnorm_b', 'new_m_w_o', 'new_m_ln2_g', 'new_m_w_up', 'new_m_ffn_conv_w', 'new_m_ffn_conv_b', 'new_m_w_down', 'new_m_g_final', 'new_v_ln1_g', 'new_v_w_in', 'new_v_b_forget', 'new_v_g_out_fox', 'new_v_g_out_dil', 'new_v_conv_w', 'new_v_conv_b', 'new_v_cnorm_g', 'new_v_cnorm_b', 'new_v_w_o', 'new_v_ln2_g', 'new_v_w_up', 'new_v_ffn_conv_w', 'new_v_ffn_conv_b', 'new_v_w_down', 'new_v_g_final']
TWIN_LEAF_KINDS = {'loss': 'loss', 'grad_x': 'grad_x', 'grad_ln1_g': 'grad_w', 'grad_w_in': 'grad_w', 'grad_b_forget': 'grad_w', 'grad_g_out_fox': 'grad_w', 'grad_g_out_dil': 'grad_w', 'grad_conv_w': 'grad_w', 'grad_conv_b': 'grad_w', 'grad_cnorm_g': 'grad_w', 'grad_cnorm_b': 'grad_w', 'grad_w_o': 'grad_w', 'grad_ln2_g': 'grad_w', 'grad_w_up': 'grad_w', 'grad_ffn_conv_w': 'grad_w', 'grad_ffn_conv_b': 'grad_w', 'grad_w_down': 'grad_w', 'grad_g_final': 'grad_w', 'delta_ln1_g': 'delta_w', 'delta_w_in': 'delta_w', 'delta_b_forget': 'delta_w', 'delta_g_out_fox': 'delta_w', 'delta_g_out_dil': 'delta_w', 'delta_conv_w': 'delta_w', 'delta_conv_b': 'delta_w', 'delta_cnorm_g': 'delta_w', 'delta_cnorm_b': 'delta_w', 'delta_w_o': 'delta_w', 'delta_ln2_g': 'delta_w', 'delta_w_up': 'delta_w', 'delta_ffn_conv_w': 'delta_w', 'delta_ffn_conv_b': 'delta_w', 'delta_w_down': 'delta_w', 'delta_g_final': 'delta_w', 'new_m_ln1_g': 'new_m', 'new_m_w_in': 'new_m', 'new_m_b_forget': 'new_m', 'new_m_g_out_fox': 'new_m', 'new_m_g_out_dil': 'new_m', 'new_m_conv_w': 'new_m', 'new_m_conv_b': 'new_m', 'new_m_cnorm_g': 'new_m', 'new_m_cnorm_b': 'new_m', 'new_m_w_o': 'new_m', 'new_m_ln2_g': 'new_m', 'new_m_w_up': 'new_m', 'new_m_ffn_conv_w': 'new_m', 'new_m_ffn_conv_b': 'new_m', 'new_m_w_down': 'new_m', 'new_m_g_final': 'new_m', 'new_v_ln1_g': 'new_v', 'new_v_w_in': 'new_v', 'new_v_b_forget': 'new_v', 'new_v_g_out_fox': 'new_v', 'new_v_g_out_dil': 'new_v', 'new_v_conv_w': 'new_v', 'new_v_conv_b': 'new_v', 'new_v_cnorm_g': 'new_v', 'new_v_cnorm_b': 'new_v', 'new_v_w_o': 'new_v', 'new_v_ln2_g': 'new_v', 'new_v_w_up': 'new_v', 'new_v_ffn_conv_w': 'new_v', 'new_v_ffn_conv_b': 'new_v', 'new_v_w_down': 'new_v', 'new_v_g_final': 'new_v'}


def _forward(args):
    return _fwd_reference(*[args[k] for k in FWD_PARAMS])


def _output_shape():
    out = _jax.eval_shape(lambda: _forward(_fwd_setup_inputs(0)))
    return out.shape, out.dtype

N_MICROBATCH = 1
ADAM_LR = 0.001
ADAM_B1 = 0.9
ADAM_B2 = 0.999
ADAM_EPS = 1e-08
ADAM_WD = 0.01
ADAM_STEP = 10
PER_EXAMPLE_BATCH_AXIS = {'x': 0, 'loss_target': 0}
SHARED_INPUTS = []
_WEIGHT_DTYPES = {'ln1_g': _jnp.float32, 'w_in': _jnp.float32, 'b_forget': _jnp.float32, 'g_out_fox': _jnp.float32, 'g_out_dil': _jnp.float32, 'conv_w': _jnp.float32, 'conv_b': _jnp.float32, 'cnorm_g': _jnp.float32, 'cnorm_b': _jnp.float32, 'w_o': _jnp.float32, 'ln2_g': _jnp.float32, 'w_up': _jnp.float32, 'ffn_conv_w': _jnp.float32, 'ffn_conv_b': _jnp.float32, 'w_down': _jnp.float32, 'g_final': _jnp.float32}
MOMENT_SCALE = {'ln1_g': 2.557470e-01, 'w_in': 1.629231e-01, 'b_forget': 9.456687e-01, 'g_out_fox': 1.973380e-01, 'g_out_dil': 2.175357e-01, 'conv_w': 1.210774e-01, 'conv_b': 4.164469e-01, 'cnorm_g': 2.467787e-01, 'cnorm_b': 2.828953e-01, 'w_o': 1.912541e-01, 'ln2_g': 1.441256e-01, 'w_up': 6.226816e-02, 'ffn_conv_w': 6.124473e-02, 'ffn_conv_b': 7.204247e-02, 'w_down': 1.021729e-01, 'g_final': 6.376225e+01}


def _to_microbatches(a, axis):
    t = _jnp.moveaxis(a, axis, 0)
    t = t.reshape((N_MICROBATCH, t.shape[0] // N_MICROBATCH) + t.shape[1:])
    return _jnp.moveaxis(t, 1, axis + 1)


def setup_inputs(seed: int = 0) -> dict:
    inp = _fwd_setup_inputs(seed)
    key = _jax.random.fold_in(_jax.random.key(seed), 7919)
    shape, _ = _output_shape()
    out = dict(inp)
    out["loss_target"] = _jax.random.normal(_jax.random.fold_in(key, 0), shape, _jnp.float32)
    for i, name in enumerate(TWIN_WEIGHTS):
        w = inp[name].astype(_jnp.float32)
        if MOMENT_SCALE is None:
            s = _jnp.sqrt(_jnp.mean(_jnp.square(w)) + 1e-30)
        else:
            s = MOMENT_SCALE[name]
        km, kv = _jax.random.split(_jax.random.fold_in(key, i + 1))
        out[name] = w
        out["m_" + name] = s * _jax.random.normal(km, w.shape, _jnp.float32)
        out["v_" + name] = (s * s) * _jax.random.uniform(kv, w.shape, _jnp.float32, 0.5, 1.5)
    if N_MICROBATCH > 1:
        for name, axis in PER_EXAMPLE_BATCH_AXIS.items():
            out[name] = _to_microbatches(out[name], axis)
    return {'x': out['x'], 'ln1_g': out['ln1_g'], 'w_in': out['w_in'], 'b_forget': out['b_forget'], 'g_out_fox': out['g_out_fox'], 'g_out_dil': out['g_out_dil'], 'conv_w': out['conv_w'], 'conv_b': out['conv_b'], 'cnorm_g': out['cnorm_g'], 'cnorm_b': out['cnorm_b'], 'w_o': out['w_o'], 'ln2_g': out['ln2_g'], 'w_up': out['w_up'], 'ffn_conv_w': out['ffn_conv_w'], 'ffn_conv_b': out['ffn_conv_b'], 'w_down': out['w_down'], 'g_final': out['g_final'], 'loss_target': out['loss_target'], 'm_ln1_g': out['m_ln1_g'], 'm_w_in': out['m_w_in'], 'm_b_forget': out['m_b_forget'], 'm_g_out_fox': out['m_g_out_fox'], 'm_g_out_dil': out['m_g_out_dil'], 'm_conv_w': out['m_conv_w'], 'm_conv_b': out['m_conv_b'], 'm_cnorm_g': out['m_cnorm_g'], 'm_cnorm_b': out['m_cnorm_b'], 'm_w_o': out['m_w_o'], 'm_ln2_g': out['m_ln2_g'], 'm_w_up': out['m_w_up'], 'm_ffn_conv_w': out['m_ffn_conv_w'], 'm_ffn_conv_b': out['m_ffn_conv_b'], 'm_w_down': out['m_w_down'], 'm_g_final': out['m_g_final'], 'v_ln1_g': out['v_ln1_g'], 'v_w_in': out['v_w_in'], 'v_b_forget': out['v_b_forget'], 'v_g_out_fox': out['v_g_out_fox'], 'v_g_out_dil': out['v_g_out_dil'], 'v_conv_w': out['v_conv_w'], 'v_conv_b': out['v_conv_b'], 'v_cnorm_g': out['v_cnorm_g'], 'v_cnorm_b': out['v_cnorm_b'], 'v_w_o': out['v_w_o'], 'v_ln2_g': out['v_ln2_g'], 'v_w_up': out['v_w_up'], 'v_ffn_conv_w': out['v_ffn_conv_w'], 'v_ffn_conv_b': out['v_ffn_conv_b'], 'v_w_down': out['v_w_down'], 'v_g_final': out['v_g_final']}


def _loss(weights, diff, rest, loss_target):
    with _jax.named_scope("forward"):
        args = {**rest, TWIN_DIFF_INPUT: diff, **{k: w.astype(_WEIGHT_DTYPES[k]) for k, w in weights.items()}}
        y = _forward(args)
    with _jax.named_scope("loss_head"):
        err = _jnp.square(y.astype(_jnp.float32) - loss_target)
        return 0.5 * _jnp.sum(_jnp.mean(err, axis=-1)) if err.ndim else 0.5 * err


def _adamw(w, g, m, v):
    m = ADAM_B1 * m + (1.0 - ADAM_B1) * g
    v = ADAM_B2 * v + (1.0 - ADAM_B2) * _jnp.square(g)
    m_hat = m / (1.0 - ADAM_B1 ** ADAM_STEP)
    v_hat = v / (1.0 - ADAM_B2 ** ADAM_STEP)
    delta = -ADAM_LR * (m_hat / (_jnp.sqrt(v_hat) + ADAM_EPS) + ADAM_WD * w)
    return delta, m, v


def reference(x, ln1_g, w_in, b_forget, g_out_fox, g_out_dil, conv_w, conv_b, cnorm_g, cnorm_b, w_o, ln2_g, w_up, ffn_conv_w, ffn_conv_b, w_down, g_final, loss_target, m_ln1_g, m_w_in, m_b_forget, m_g_out_fox, m_g_out_dil, m_conv_w, m_conv_b, m_cnorm_g, m_cnorm_b, m_w_o, m_ln2_g, m_w_up, m_ffn_conv_w, m_ffn_conv_b, m_w_down, m_g_final, v_ln1_g, v_w_in, v_b_forget, v_g_out_fox, v_g_out_dil, v_conv_w, v_conv_b, v_cnorm_g, v_cnorm_b, v_w_o, v_ln2_g, v_w_up, v_ffn_conv_w, v_ffn_conv_b, v_w_down, v_g_final):
    given = dict(x=x, ln1_g=ln1_g, w_in=w_in, b_forget=b_forget, g_out_fox=g_out_fox, g_out_dil=g_out_dil, conv_w=conv_w, conv_b=conv_b, cnorm_g=cnorm_g, cnorm_b=cnorm_b, w_o=w_o, ln2_g=ln2_g, w_up=w_up, ffn_conv_w=ffn_conv_w, ffn_conv_b=ffn_conv_b, w_down=w_down, g_final=g_final, loss_target=loss_target, m_ln1_g=m_ln1_g, m_w_in=m_w_in, m_b_forget=m_b_forget, m_g_out_fox=m_g_out_fox, m_g_out_dil=m_g_out_dil, m_conv_w=m_conv_w, m_conv_b=m_conv_b, m_cnorm_g=m_cnorm_g, m_cnorm_b=m_cnorm_b, m_w_o=m_w_o, m_ln2_g=m_ln2_g, m_w_up=m_w_up, m_ffn_conv_w=m_ffn_conv_w, m_ffn_conv_b=m_ffn_conv_b, m_w_down=m_w_down, m_g_final=m_g_final, v_ln1_g=v_ln1_g, v_w_in=v_w_in, v_b_forget=v_b_forget, v_g_out_fox=v_g_out_fox, v_g_out_dil=v_g_out_dil, v_conv_w=v_conv_w, v_conv_b=v_conv_b, v_cnorm_g=v_cnorm_g, v_cnorm_b=v_cnorm_b, v_w_o=v_w_o, v_ln2_g=v_ln2_g, v_w_up=v_w_up, v_ffn_conv_w=v_ffn_conv_w, v_ffn_conv_b=v_ffn_conv_b, v_w_down=v_w_down, v_g_final=v_g_final)
    weights = {n: given[n] for n in TWIN_WEIGHTS}
    shared = {n: given[n] for n in SHARED_INPUTS}
    per_example = {n: given[n] for n in ['x']}
    grad_fn = _jax.value_and_grad(_loss, argnums=(0, 1))

    def one_microbatch(ex, loss_target):
        ex = dict(ex)
        diff = ex.pop(TWIN_DIFF_INPUT)
        return grad_fn(weights, diff, {**shared, **ex}, loss_target)

    if N_MICROBATCH == 1:
        loss, (grad_w, grad_x) = one_microbatch(per_example, given["loss_target"])
    else:
        def body(carry, xs):
            loss_sum, grad_sum = carry
            l_k, (gw_k, gx_k) = one_microbatch(xs[0], xs[1])
            with _jax.named_scope("update"):
                return (loss_sum + l_k, _jax.tree.map(_jnp.add, grad_sum, gw_k)), gx_k

        init = (_jnp.zeros((), _jnp.float32), _jax.tree.map(_jnp.zeros_like, weights))
        (loss, grad_w), grad_x = _jax.lax.scan(body, init, (per_example, given["loss_target"]))
    with _jax.named_scope("update"):
        delta_w, new_m, new_v = {}, {}, {}
        for n in TWIN_WEIGHTS:
            delta_w[n], new_m[n], new_v[n] = _adamw(weights[n], grad_w[n], given["m_" + n], given["v_" + n])
    return (loss, grad_x, *[grad_w[n] for n in TWIN_WEIGHTS], *[delta_w[n] for n in TWIN_WEIGHTS],
            *[new_m[n] for n in TWIN_WEIGHTS], *[new_v[n] for n in TWIN_WEIGHTS])
```

```python
import functools

import numpy as np
import jax
import jax.numpy as jnp
from jax import lax
from jax.experimental import pallas as pl
from jax.experimental.pallas import tpu as pltpu

F32 = jnp.float32
BF16 = jnp.bfloat16
HIGHEST = lax.Precision.HIGHEST

D = 1024
DEPTH = 2
HD = 64
WA = 384
NPAIR = 3
CC = 256
CONV_K = 31
DFF = 2816
FB = 704
NJ = 4
NDEV = 8
EPS = 1e-6
NQKV = 2304
NREST = 768
NPAD = NQKV + NREST
O_FA, O_QB, N_IN = 1152, 1158, 2822
DILATION_PAIRS = ((128, 1), (512, 4), (2048, 16))
NEG = -1e30

ADAM_LR, ADAM_B1, ADAM_B2, ADAM_EPS, ADAM_WD, ADAM_STEP = 0.001, 0.9, 0.999, 1e-08, 0.01, 10

VMEM_LIMIT = 48 * 1024 * 1024


def _cp(*sem):
    return pltpu.CompilerParams(dimension_semantics=sem, vmem_limit_bytes=VMEM_LIMIT)


def _sds(shape, dtype):
    return jax.ShapeDtypeStruct(shape, dtype)


def _dot(a, b, prec=None):
    return jnp.dot(a, b, preferred_element_type=F32, precision=prec)


def _dot_nt(a, b, prec=None):
    return lax.dot_general(a, b, (((1,), (1,)), ((), ())), preferred_element_type=F32, precision=prec)


def _dot_tn(a, b):
    return lax.dot_general(a, b, (((0,), (0,)), ((), ())), preferred_element_type=F32)


def _sigmoid(z):
    return 1.0 / (1.0 + jnp.exp(-z))


def _rms_bwd(dh, x, g):
    r = lax.rsqrt(jnp.mean(x * x, axis=-1, keepdims=True) + EPS)
    xh = x * r
    dg = jnp.sum(dh * xh, axis=0, keepdims=True)
    dxh = dh * g
    dx = r * (dxh - xh * jnp.mean(dxh * xh, axis=-1, keepdims=True))
    return dx, dg


def _exchange(groups, gather, name):
    flat = [a for g in groups for a in g]
    n = len(flat)
    out_shapes = [_sds((NDEV, len(g)) + (g[0].shape if gather else g[0].shape[1:]), g[0].dtype) for g in groups]

    def body(*refs):
        ins, outs = refs[:n], refs[n:n + len(groups)]
        send_sems, recv_sems, local_sems = refs[n + len(groups):]
        x, y, c = lax.axis_index("x"), lax.axis_index("y"), lax.axis_index("c")
        me = 4 * x + 2 * y + c
        copies = []
        a = 0
        for gi, g in enumerate(groups):
            for l in range(len(g)):
                own = pltpu.make_async_copy(ins[a] if gather else ins[a].at[me], outs[gi].at[me, l], local_sems.at[a])
                own.start()
                copies.append(own)
                for k in range(1, NDEV):
                    px, py, pc = x ^ ((k >> 2) & 1), y ^ ((k >> 1) & 1), c ^ (k & 1)
                    peer = 4 * px + 2 * py + pc
                    rc = pltpu.make_async_remote_copy(
                        src_ref=ins[a] if gather else ins[a].at[peer], dst_ref=outs[gi].at[me, l],
                        send_sem=send_sems.at[a, k - 1], recv_sem=recv_sems.at[a, k - 1],
                        device_id=(px, py, pc), device_id_type=pl.DeviceIdType.MESH)
                    rc.start()
                    copies.append(rc)
                a += 1
        for cp in copies:
            cp.wait()

    any_spec = pl.BlockSpec(memory_space=pl.ANY)
    return pl.pallas_call(
        body, name=name, out_shape=out_shapes,
        in_specs=[any_spec] * n, out_specs=[any_spec] * len(groups),
        scratch_shapes=[pltpu.SemaphoreType.DMA((n, NDEV - 1)), pltpu.SemaphoreType.DMA((n, NDEV - 1)),
                        pltpu.SemaphoreType.DMA((n,))],
        compiler_params=pltpu.CompilerParams(has_side_effects=True),
    )(*flat)


def _sum_parts(parts, name):
    npart, R, C = parts.shape

    def body(p_ref, o_ref):
        s = p_ref[0]
        for k in range(1, npart):
            s = s + p_ref[k]
        o_ref[...] = s

    return pl.pallas_call(body, name=name, out_shape=_sds((R, C), F32))(parts)


def _rms_fwd(x, g):
    T = x.shape[0]
    tm = min(512, T)

    def body(x_ref, g_ref, o_ref):
        xv = x_ref[...]
        r = lax.rsqrt(jnp.mean(xv * xv, axis=-1, keepdims=True) + EPS)
        o_ref[...] = (xv * r * g_ref[...]).astype(BF16)

    return pl.pallas_call(
        body, name="rms_fwd", grid=(T // tm,), out_shape=_sds((T, D), BF16),
        in_specs=[pl.BlockSpec((tm, D), lambda i: (i, 0)), pl.BlockSpec((1, D), lambda i: (0, 0))],
        out_specs=pl.BlockSpec((tm, D), lambda i: (i, 0)), compiler_params=_cp("parallel"))(x, g)


def _mm_in(h, w):
    T = h.shape[0]
    tm = min(512, T)

    def body(h_ref, w_ref, q_ref, r_ref):
        hv = h_ref[...]
        q_ref[...] = _dot(hv, w_ref[:, :NQKV]).astype(BF16)
        r_ref[...] = _dot(hv, w_ref[:, NQKV:])

    return pl.pallas_call(
        body, name="mm_in", grid=(T // tm,), out_shape=[_sds((T, NQKV), BF16), _sds((T, NREST), F32)],
        in_specs=[pl.BlockSpec((tm, D), lambda i: (i, 0)), pl.BlockSpec((D, NPAD), lambda i: (0, 0))],
        out_specs=[pl.BlockSpec((tm, NQKV), lambda i: (i, 0)), pl.BlockSpec((tm, NREST), lambda i: (i, 0))],
        compiler_params=_cp("parallel"))(h, w)


def _gate_fwd(p_rest, bf, S, TB):
    T = p_rest.shape[0]
    B, nb = T // S, S // TB

    def body(z_ref, b_ref, cc_ref, cr_ref):
        tri = (lax.broadcasted_iota(jnp.int32, (TB, TB), 0) >= lax.broadcasted_iota(jnp.int32, (TB, TB), 1)).astype(F32)
        carry = jnp.zeros((1, 128), F32)
        for j in range(nb):
            z = z_ref[j * TB:(j + 1) * TB, :] + b_ref[...]
            lf = jnp.minimum(z, 0.0) - jnp.log(1.0 + jnp.exp(-jnp.abs(z)))
            cb = _dot(tri, lf, HIGHEST) + carry
            cc_ref[j * TB:(j + 1) * TB, :] = cb
            ct = cb.T
            for hd in range(8):
                cr_ref[0, j, hd] = ct[hd:hd + 1, :]
            carry = cb[TB - 1:TB, :]

    return pl.pallas_call(
        body, name="gate_fwd", grid=(B,), out_shape=[_sds((T, 128), F32), _sds((B, nb, 8, 1, TB), F32)],
        in_specs=[pl.BlockSpec((S, 128), lambda b: (b, 4)), pl.BlockSpec((1, 128), lambda b: (0, 0))],
        out_specs=[pl.BlockSpec((S, 128), lambda b: (b, 0)), pl.BlockSpec((1, nb, 8, 1, TB), lambda b: (b, 0, 0, 0, 0))],
        compiler_params=_cp("parallel"))(p_rest, bf)


def _bias_table(S, TB, fox):
    nb = S // TB
    dj = np.arange(nb)[:, None, None]
    delta = dj * TB + np.arange(TB)[None, :, None] - np.arange(TB)[None, None, :]
    if fox:
        mult = (delta >= 0).astype(np.float64)
    else:
        mult = np.zeros(delta.shape)
        for w, d in DILATION_PAIRS:
            mult += (delta >= 0) & (delta % d == 0) & (delta <= w)
    with np.errstate(divide="ignore"):
        tab = np.where(mult > 0, np.log(np.maximum(mult, 1e-30)), NEG)
    return jnp.asarray(tab, F32)


def _head_masks():
    lane = lax.broadcasted_iota(jnp.int32, (1, 128), 1)
    return [lane < HD, lane >= HD]


def _attn_fwd(p_qkv, col0, tab, S, c_col=None, c_row=None, name="attn"):
    T = p_qkv.shape[0]
    nb, TB = tab.shape[0], tab.shape[1]
    B = T // S
    fox = c_col is not None

    def body(*refs):
        if fox:
            q_ref, k_ref, v_ref, tab_ref, cc_ref, cr_ref, o_ref, l_ref = refs
        else:
            q_ref, k_ref, v_ref, tab_ref, o_ref, l_ref = refs
        pair, i = pl.program_id(1), pl.program_id(2)
        hms = _head_masks()
        lane = lax.broadcasted_iota(jnp.int32, (1, 128), 1)
        q2 = q_ref[...] * 0.125
        qh = [jnp.where(hm, q2, jnp.zeros_like(q2)) for hm in hms]
        if fox:
            ccv = cc_ref[...]
            ccol = [jnp.sum(jnp.where(lane == 2 * pair + hh, ccv, 0.0), axis=1, keepdims=True) for hh in range(2)]

        def kv_step(j, carry):
            m0, l0, m1, l1, acc = carry
            rows = pl.ds(pl.multiple_of(j * TB, TB), TB)
            k2, v2 = k_ref[rows, :], v_ref[rows, :]
            bias = tab_ref[i - j]
            ms, ls, alphas, pvs = [m0, m1], [l0, l1], [], []
            for hh in range(2):
                s = _dot_nt(qh[hh], k2) + bias
                if fox:
                    s = s + (ccol[hh] - cr_ref[0, j, 2 * pair + hh])
                mn = jnp.maximum(ms[hh], jnp.max(s, axis=1, keepdims=True))
                p = jnp.exp(s - mn)
                alpha = jnp.exp(ms[hh] - mn)
                ls[hh] = alpha * ls[hh] + jnp.sum(p, axis=1, keepdims=True)
                ms[hh] = mn
                alphas.append(alpha)
                pvs.append(_dot(p.astype(BF16), jnp.where(hms[hh], v2, jnp.zeros_like(v2))))
            acc = acc * jnp.where(hms[0], alphas[0], alphas[1]) + pvs[0] + pvs[1]
            return ms[0], ls[0], ms[1], ls[1], acc

        col = lambda v: jnp.full((TB, 1), v, F32)
        m0, l0, m1, l1, acc = lax.fori_loop(0, i + 1, kv_step, (col(NEG), col(0.0), col(NEG), col(0.0), jnp.zeros((TB, 128), F32)))
        o_ref[...] = acc / jnp.where(hms[0], l0, l1)
        l_ref[...] = jnp.where(hms[0], m0 + jnp.log(l0), m1 + jnp.log(l1))

    in_specs = [pl.BlockSpec((TB, 128), lambda b, p, i: (b * nb + i, col0 + p)),
                pl.BlockSpec((S, 128), lambda b, p, i: (b, col0 + NPAIR + p)),
                pl.BlockSpec((S, 128), lambda b, p, i: (b, col0 + 2 * NPAIR + p)),
                pl.BlockSpec((nb, TB, TB), lambda b, p, i: (0, 0, 0))]
    args = [p_qkv, p_qkv, p_qkv, tab]
    if fox:
        in_specs += [pl.BlockSpec((TB, 128), lambda b, p, i: (b * nb + i, 0)),
                     pl.BlockSpec((1, nb, 8, 1, TB), lambda b, p, i: (b, 0, 0, 0, 0))]
        args += [c_col, c_row]
    blk = pl.BlockSpec((TB, 128), lambda b, p, i: (b * nb + i, p))
    return pl.pallas_call(
        body, name=name, grid=(B, NPAIR, nb), out_shape=[_sds((T, WA), F32), _sds((T, WA), F32)],
        in_specs=in_specs, out_specs=[blk, blk], compiler_params=_cp("parallel", "parallel", "parallel"))(*args)


def _conv_taps(buf, w_ref, first_row, rows):
    acc = w_ref[0:1, :] * buf[pl.ds(first_row, rows), :]
    for k in range(1, CONV_K):
        acc = acc + w_ref[k:k + 1, :] * buf[pl.ds(first_row + k, rows), :]
    return acc


def _mix_fwd(o_a, o_b, p_rest, g_a, g_b, cw, cb, ng, nbias, S):
    T = o_a.shape[0]
    tm = min(256, S)
    nps = S // tm
    HALO = 32

    def body(oa_ref, ob_ref, pt_ref, ph_ref, ga_ref, gb_ref, cw_ref, cb_ref, ng_ref, nb_ref, y_ref, buf):
        i = pl.program_id(0)
        first = (i % nps) == 0
        for o_ref, g_ref, c0 in ((oa_ref, ga_ref, 0), (ob_ref, gb_ref, WA)):
            o = o_ref[...]
            r = lax.rsqrt(jnp.mean(o * o, axis=-1, keepdims=True) + EPS)
            y_ref[:, c0:c0 + WA] = (o * r * g_ref[...]).astype(BF16)
        ph = jnp.where(first, 0.0, ph_ref[...])
        buf[0:HALO, :] = ph[:, :CC] * _sigmoid(ph[:, CC:])
        pt = pt_ref[...]
        buf[HALO:HALO + tm, :] = pt[:, :CC] * _sigmoid(pt[:, CC:])
        cv = _conv_taps(buf, cw_ref, HALO - CONV_K + 1, tm) + cb_ref[...]
        xc = cv - jnp.mean(cv, axis=-1, keepdims=True)
        lo = xc * lax.rsqrt(jnp.mean(xc * xc, axis=-1, keepdims=True) + EPS) * ng_ref[...] + nb_ref[...]
        y_ref[:, 2 * WA:] = (lo * _sigmoid(lo)).astype(BF16)

    row = lambda w: pl.BlockSpec((1, w), lambda i: (0, 0))
    return pl.pallas_call(
        body, name="mix_fwd", grid=(T // tm,), out_shape=_sds((T, D), BF16),
        in_specs=[pl.BlockSpec((tm, WA), lambda i: (i, 0)), pl.BlockSpec((tm, WA), lambda i: (i, 0)),
                  pl.BlockSpec((tm, 2 * CC), lambda i: (i, 0)),
                  pl.BlockSpec((HALO, 2 * CC), lambda i: (jnp.maximum(i * (tm // HALO) - 1, 0), 0)),
                  row(WA), row(WA), pl.BlockSpec((32, CC), lambda i: (0, 0)), row(CC), row(CC), row(CC)],
        out_specs=pl.BlockSpec((tm, D), lambda i: (i, 0)),
        scratch_shapes=[pltpu.VMEM((tm + HALO, CC), F32)],
        compiler_params=_cp("parallel"))(o_a, o_b, p_rest, p_rest, g_a, g_b, cw, cb, ng, nbias)


def _mm_res(a, w, resid, g, name):
    nk, T, kb = a.shape
    tm = min(512, T)

    def body(a_ref, w_ref, r_ref, g_ref, x_ref, h_ref, acc):
        k = pl.program_id(1)

        @pl.when(k == 0)
        def _():
            acc[...] = r_ref[...]

        acc[...] += _dot(a_ref[0], w_ref[0])

        @pl.when(k == nk - 1)
        def _():
            xv = acc[...]
            x_ref[...] = xv
            r = lax.rsqrt(jnp.mean(xv * xv, axis=-1, keepdims=True) + EPS)
            h_ref[...] = (xv * r * g_ref[...]).astype(BF16)

    return pl.pallas_call(
        body, name=name, grid=(T // tm, nk), out_shape=[_sds((T, D), F32), _sds((T, D), BF16)],
        in_specs=[pl.BlockSpec((1, tm, kb), lambda i, k: (k, i, 0)), pl.BlockSpec((1, kb, D), lambda i, k: (k, 0, 0)),
                  pl.BlockSpec((tm, D), lambda i, k: (i, 0)), pl.BlockSpec((1, D), lambda i, k: (0, 0))],
        out_specs=[pl.BlockSpec((tm, D), lambda i, k: (i, 0)), pl.BlockSpec((tm, D), lambda i, k: (i, 0))],
        scratch_shapes=[pltpu.VMEM((tm, D), F32)],
        compiler_params=_cp("parallel", "arbitrary"))(a, w, resid, g)


def _up_fwd(h2, w_up, fcw, fcb, S):
    T = h2.shape[0]
    tm = min(512, S)
    nps = S // tm

    def body(h_ref, hh_ref, w_ref, cw_ref, cb_ref, u_ref, hid_ref, buf):
        i = pl.program_id(1)
        first = (i % nps) == 0
        hv = h_ref[...]
        hh = jnp.where(first, jnp.zeros_like(hh_ref[...]), hh_ref[...])
        c = []
        for half in range(2):
            w = w_ref[half, 0]
            u = _dot(hv, w)
            u_ref[half, 0] = u
            buf[half, 0:8, :] = _dot(hh, w)
            buf[half, 8:8 + tm, :] = u
            cw = cw_ref[half, 0]
            c.append(cb_ref[half, 0] + cw[0:1] * buf[half, pl.ds(6, tm), :] + cw[1:2] * buf[half, pl.ds(7, tm), :] + cw[2:3] * u)
        hid_ref[0] = (c[0] * _sigmoid(c[0]) * c[1]).astype(BF16)

    return pl.pallas_call(
        body, name="up_fwd", grid=(NJ, T // tm), out_shape=[_sds((2, NJ, T, FB), F32), _sds((NJ, T, FB), BF16)],
        in_specs=[pl.BlockSpec((tm, D), lambda j, i: (i, 0)),
                  pl.BlockSpec((8, D), lambda j, i: (jnp.maximum(i * (tm // 8) - 1, 0), 0)),
                  pl.BlockSpec((2, 1, D, FB), lambda j, i: (0, j, 0, 0)),
                  pl.BlockSpec((2, 1, 3, FB), lambda j, i: (0, j, 0, 0)),
                  pl.BlockSpec((2, 1, 1, FB), lambda j, i: (0, j, 0, 0))],
        out_specs=[pl.BlockSpec((2, 1, tm, FB), lambda j, i: (0, j, i, 0)), pl.BlockSpec((1, tm, FB), lambda j, i: (j, i, 0))],
        scratch_shapes=[pltpu.VMEM((2, tm + 8, FB), F32)],
        compiler_params=_cp("parallel", "parallel"))(h2, h2, w_up, fcw, fcb)


def _final_loss(x, tgt, g):
    T = x.shape[0]
    tm = min(512, T)

    def body(x_ref, t_ref, g_ref, dx_ref, loss_ref, dg_ref):
        @pl.when(pl.program_id(0) == 0)
        def _():
            loss_ref[...] = jnp.zeros_like(loss_ref)
            dg_ref[...] = jnp.zeros_like(dg_ref)

        xv, gv = x_ref[...], g_ref[...]
        r = lax.rsqrt(jnp.mean(xv * xv, axis=-1, keepdims=True) + EPS)
        e = xv * r * gv - t_ref[...]
        loss_ref[...] += 0.5 * jnp.sum(jnp.mean(e * e, axis=-1, keepdims=True), axis=0, keepdims=True)
        dx, dg = _rms_bwd(e * (1.0 / D), xv, gv)
        dx_ref[...] = dx
        dg_ref[...] += dg

    return pl.pallas_call(
        body, name="final_loss", grid=(T // tm,), out_shape=[_sds((T, D), F32), _sds((1, 128), F32), _sds((1, D), F32)],
        in_specs=[pl.BlockSpec((tm, D), lambda i: (i, 0)), pl.BlockSpec((tm, D), lambda i: (i, 0)), pl.BlockSpec((1, D), lambda i: (0, 0))],
        out_specs=[pl.BlockSpec((tm, D), lambda i: (i, 0)), pl.BlockSpec((1, 128), lambda i: (0, 0)), pl.BlockSpec((1, D), lambda i: (0, 0))],
        compiler_params=_cp("arbitrary"))(x, tgt, g)


def _down_bwd(dx2, w_down, u, fcw, fcb, S):
    T = dx2.shape[0]
    tm = min(512, S)
    nps = S // tm
    nblk8 = T // 8

    def body(dx_ref, dxn_ref, wd_ref, ut_ref, up_ref, un_ref, cw_ref, cb_ref, du_ref, dcw_ref, dxb, ubuf, dcbuf):
        i = pl.program_id(1)
        first, last = (i % nps) == 0, (i % nps) == nps - 1

        @pl.when(i == 0)
        def _():
            dcw_ref[...] = jnp.zeros_like(dcw_ref)

        dxb[0:tm, :] = dx_ref[...].astype(BF16)
        dxb[tm:tm + 8, :] = jnp.where(last, 0.0, dxn_ref[...]).astype(BF16)
        dh = _dot_nt(dxb[...], wd_ref[0])
        ce = []
        for half in range(2):
            ubuf[half, 0:8, :] = jnp.where(first, 0.0, up_ref[half, 0])
            ubuf[half, 8:8 + tm, :] = ut_ref[half, 0]
            ubuf[half, 8 + tm:16 + tm, :] = un_ref[half, 0]
            cw = cw_ref[half, 0]
            ce.append(cb_ref[half, 0] + cw[0:1] * ubuf[half, pl.ds(6, tm + 8), :] + cw[1:2] * ubuf[half, pl.ds(7, tm + 8), :]
                      + cw[2:3] * ubuf[half, pl.ds(8, tm + 8), :])
        sg = _sigmoid(ce[0])
        dc = [dh * ce[1] * (sg * (1.0 + ce[0] * (1.0 - sg))), dh * (ce[0] * sg)]
        for half in range(2):
            cw = cw_ref[half, 0]
            dcbuf[...] = dc[half]
            dct = dc[half][0:tm]
            du_ref[half, 0] = (cw[2:3] * dct + cw[1:2] * dcbuf[pl.ds(1, tm), :] + cw[0:1] * dcbuf[pl.ds(2, tm), :]).astype(BF16)
            for k in range(3):
                dcw_ref[half, 0, k:k + 1, :] += jnp.sum(dct * ubuf[half, pl.ds(6 + k, tm), :], axis=0, keepdims=True)
            dcw_ref[half, 0, 3:4, :] += jnp.sum(dct, axis=0, keepdims=True)

    ublk = lambda rows, imap: pl.BlockSpec((2, 1, rows, FB), imap)
    return pl.pallas_call(
        body, name="down_bwd", grid=(NJ, T // tm), out_shape=[_sds((2, NJ, T, FB), BF16), _sds((2, NJ, 8, FB), F32)],
        in_specs=[pl.BlockSpec((tm, D), lambda j, i: (i, 0)),
                  pl.BlockSpec((8, D), lambda j, i: (jnp.minimum((i + 1) * (tm // 8), nblk8 - 1), 0)),
                  pl.BlockSpec((1, FB, D), lambda j, i: (j, 0, 0)),
                  ublk(tm, lambda j, i: (0, j, i, 0)),
                  ublk(8, lambda j, i: (0, j, jnp.maximum(i * (tm // 8) - 1, 0), 0)),
                  ublk(8, lambda j, i: (0, j, jnp.minimum((i + 1) * (tm // 8), nblk8 - 1), 0)),
                  pl.BlockSpec((2, 1, 3, FB), lambda j, i: (0, j, 0, 0)),
                  pl.BlockSpec((2, 1, 1, FB), lambda j, i: (0, j, 0, 0))],
        out_specs=[ublk(tm, lambda j, i: (0, j, i, 0)), pl.BlockSpec((2, 1, 8, FB), lambda j, i: (0, j, 0, 0))],
        scratch_shapes=[pltpu.VMEM((tm + 8, D), BF16), pltpu.VMEM((2, tm + 16, FB), F32), pltpu.VMEM((tm + 8, FB), F32)],
        compiler_params=_cp("parallel", "arbitrary"))(dx2, dx2, w_down, u, u, u, fcw, fcb)


def _wgrad(a, b, tn, name):
    na, T, ka = a.shape
    nb, _, kb = b.shape
    tk = min(512, T)

    def body(a_ref, b_ref, o_ref):
        @pl.when(pl.program_id(3) == 0)
        def _():
            o_ref[...] = jnp.zeros_like(o_ref)

        o_ref[0, 0] += _dot_tn(a_ref[0], b_ref[0].astype(BF16))

    return pl.pallas_call(
        body, name=name, grid=(na, nb, kb // tn, T // tk), out_shape=_sds((na, nb, ka, kb), F32),
        in_specs=[pl.BlockSpec((1, tk, ka), lambda ia, ib, n, t: (ia, t, 0)), pl.BlockSpec((1, tk, tn), lambda ia, ib, n, t: (ib, t, n))],
        out_specs=pl.BlockSpec((1, 1, ka, tn), lambda ia, ib, n, t: (ia, ib, 0, n)),
        compiler_params=_cp("parallel", "parallel", "parallel", "arbitrary"))(a, b)


def _dx_rms(dy, w, x, g, dres, blocked, name):
    T = x.shape[0]
    tm = min(512, T)
    nk = NJ if blocked else dy.shape[1] // D

    def body(dy_ref, w_ref, x_ref, g_ref, r_ref, dx_ref, dg_ref, acc):
        i, k = pl.program_id(0), pl.program_id(1)

        @pl.when((i == 0) & (k == 0))
        def _():
            dg_ref[...] = jnp.zeros_like(dg_ref)

        @pl.when(k == 0)
        def _():
            acc[...] = jnp.zeros_like(acc)

        if blocked:
            acc[...] += _dot_nt(dy_ref[0, 0], w_ref[0, 0]) + _dot_nt(dy_ref[1, 0], w_ref[1, 0])
        else:
            acc[...] += _dot_nt(dy_ref[...], w_ref[...])

        @pl.when(k == nk - 1)
        def _():
            dx, dg = _rms_bwd(acc[...], x_ref[...], g_ref[...])
            dx_ref[...] = r_ref[...] + dx
            dg_ref[...] += dg

    if blocked:
        dy_spec = pl.BlockSpec((2, 1, tm, FB), lambda i, k: (0, k, i, 0))
        w_spec = pl.BlockSpec((2, 1, D, FB), lambda i, k: (0, k, 0, 0))
    else:
        dy_spec = pl.BlockSpec((tm, D), lambda i, k: (i, k))
        w_spec = pl.BlockSpec((D, D), lambda i, k: (0, k))
    tile = pl.BlockSpec((tm, D), lambda i, k: (i, 0))
    return pl.pallas_call(
        body, name=name, grid=(T // tm, nk), out_shape=[_sds((T, D), F32), _sds((1, D), F32)],
        in_specs=[dy_spec, w_spec, tile, pl.BlockSpec((1, D), lambda i, k: (0, 0)), tile],
        out_specs=[tile, pl.BlockSpec((1, D), lambda i, k: (0, 0))],
        scratch_shapes=[pltpu.VMEM((tm, D), F32)],
        compiler_params=_cp("arbitrary", "arbitrary"))(dy, w, x, g, dres)


def _mm_nt(a, w, name):
    T = a.shape[0]
    tm = min(512, T)

    def body(a_ref, w_ref, o_ref):
        o_ref[...] = _dot_nt(a_ref[...].astype(BF16), w_ref[...])

    return pl.pallas_call(
        body, name=name, grid=(T // tm,), out_shape=_sds((T, D), F32),
        in_specs=[pl.BlockSpec((tm, D), lambda i: (i, 0)), pl.BlockSpec((D, D), lambda i: (0, 0))],
        out_specs=pl.BlockSpec((tm, D), lambda i: (i, 0)), compiler_params=_cp("parallel"))(a, w)


def _mix_bwd(dy, o_a, o_b, p_rest, g_a, g_b, cw, cb, ng, nbias, S):
    T = dy.shape[0]
    tm = min(256, S)
    nps = S // tm
    HALO = 32
    nblkh = T // HALO
    RE = tm + HALO

    def body(dy_ref, dyn_ref, oa_ref, ob_ref, pt_ref, pp_ref, pn_ref, ga_ref, gb_ref, cw_ref, cb_ref, ng_ref, nb_ref,
             doa_ref, dda_ref, dob_ref, ddb_ref, dg_ref, dgn_ref, dcw_ref, dcn_ref, gbuf, dvbuf):
        i = pl.program_id(0)
        first, last = (i % nps) == 0, (i % nps) == nps - 1

        @pl.when(i == 0)
        def _():
            dgn_ref[...] = jnp.zeros_like(dgn_ref)
            dcw_ref[...] = jnp.zeros_like(dcw_ref)
            dcn_ref[...] = jnp.zeros_like(dcn_ref)

        same_head = (lax.broadcasted_iota(jnp.int32, (WA, WA), 0) // HD == lax.broadcasted_iota(jnp.int32, (WA, WA), 1) // HD).astype(F32)
        for n, (o_ref, g_ref, do_ref, dd_ref) in enumerate(((oa_ref, ga_ref, doa_ref, dda_ref), (ob_ref, gb_ref, dob_ref, ddb_ref))):
            o = o_ref[...]
            do, dg = _rms_bwd(dy_ref[:, n * WA:(n + 1) * WA], o, g_ref[...])
            dob = do.astype(BF16)
            do_ref[...] = dob
            dd_ref[...] = _dot(dob.astype(F32) * o, same_head, HIGHEST)
            dgn_ref[n:n + 1, :] += dg

        pp = jnp.where(first, 0.0, pp_ref[...])
        gbuf[0:HALO, :] = pp[:, :CC] * _sigmoid(pp[:, CC:])
        pt = pt_ref[...]
        gv, sgg = pt[:, :CC], _sigmoid(pt[:, CC:])
        gbuf[HALO:HALO + tm, :] = gv * sgg
        pn = pn_ref[...]
        gbuf[HALO + tm:2 * HALO + tm, :] = pn[:, :CC] * _sigmoid(pn[:, CC:])
        cv = _conv_taps(gbuf, cw_ref, HALO - CONV_K + 1, RE) + cb_ref[...]
        xc = cv - jnp.mean(cv, axis=-1, keepdims=True)
        r = lax.rsqrt(jnp.mean(xc * xc, axis=-1, keepdims=True) + EPS)
        xh = xc * r
        lo = xh * ng_ref[...] + nb_ref[...]
        sg = _sigmoid(lo)
        dyc = jnp.concatenate([dy_ref[:, 2 * WA:], jnp.where(last, 0.0, dyn_ref[...])], axis=0)
        dlo = dyc * (sg * (1.0 + lo * (1.0 - sg)))
        dcn_ref[0:1, :] += jnp.sum(dlo[0:tm] * xh[0:tm], axis=0, keepdims=True)
        dcn_ref[1:2, :] += jnp.sum(dlo[0:tm], axis=0, keepdims=True)
        dxh = dlo * ng_ref[...]
        dcv = r * (dxh - jnp.mean(dxh, axis=-1, keepdims=True) - xh * jnp.mean(dxh * xh, axis=-1, keepdims=True))
        dvbuf[...] = dcv
        dct = dcv[0:tm]
        dcw_ref[CONV_K:CONV_K + 1, :] += jnp.sum(dct, axis=0, keepdims=True)
        dglu = jnp.zeros((tm, CC), F32)
        for k in range(CONV_K):
            dcw_ref[k:k + 1, :] += jnp.sum(dct * gbuf[pl.ds(HALO - CONV_K + 1 + k, tm), :], axis=0, keepdims=True)
            dglu = dglu + cw_ref[k:k + 1, :] * dvbuf[pl.ds(CONV_K - 1 - k, tm), :]
        dg_ref[:, :CC] = (dglu * sgg).astype(BF16)
        dg_ref[:, CC:] = (dglu * gv * sgg * (1.0 - sgg)).astype(BF16)

    row = lambda w: pl.BlockSpec((1, w), lambda i: (0, 0))
    tile = lambda w: pl.BlockSpec((tm, w), lambda i: (i, 0))
    nxt = lambda i: jnp.minimum((i + 1) * (tm // HALO), nblkh - 1)
    const = lambda r, w: pl.BlockSpec((r, w), lambda i: (0, 0))
    return pl.pallas_call(
        body, name="mix_bwd", grid=(T // tm,),
        out_shape=[_sds((T, WA), BF16), _sds((T, WA), F32), _sds((T, WA), BF16), _sds((T, WA), F32), _sds((T, 2 * CC), BF16),
                   _sds((8, WA), F32), _sds((32, CC), F32), _sds((8, CC), F32)],
        in_specs=[tile(D), pl.BlockSpec((HALO, CC), lambda i: (nxt(i), 3)), tile(WA), tile(WA), tile(2 * CC),
                  pl.BlockSpec((HALO, 2 * CC), lambda i: (jnp.maximum(i * (tm // HALO) - 1, 0), 0)),
                  pl.BlockSpec((HALO, 2 * CC), lambda i: (nxt(i), 0)),
                  row(WA), row(WA), const(32, CC), row(CC), row(CC), row(CC)],
        out_specs=[tile(WA), tile(WA), tile(WA), tile(WA), tile(2 * CC), const(8, WA), const(32, CC), const(8, CC)],
        scratch_shapes=[pltpu.VMEM((tm + 2 * HALO, CC), F32), pltpu.VMEM((RE, CC), F32)],
        compiler_params=_cp("arbitrary"))(dy, dy, o_a, o_b, p_rest, p_rest, p_rest, g_a, g_b, cw, cb, ng, nbias)


def _attn_bwd(p_qkv, col0, tab, do, lse, dd, S, c_col=None, c_row=None, name="attn_bwd"):
    T = p_qkv.shape[0]
    nb, TB = tab.shape[0], tab.shape[1]
    B = T // S
    fox = c_col is not None

    def body(*refs):
        if fox:
            q_ref, k_ref, v_ref, tab_ref, do_ref, l_ref, dd_ref, cc_ref, cr_ref, dq_ref, dk_ref, dv_ref, dc_ref, dcq_ref = refs
        else:
            q_ref, k_ref, v_ref, tab_ref, do_ref, l_ref, dd_ref, dq_ref, dk_ref, dv_ref = refs
        pair, j = pl.program_id(1), pl.program_id(2)
        hms = _head_masks()
        lane = lax.broadcasted_iota(jnp.int32, (1, 128), 1)

        @pl.when(j == 0)
        def _():
            dq_ref[...] = jnp.zeros_like(dq_ref)
            if fox:
                dcq_ref[...] = jnp.zeros_like(dcq_ref)

        k2, v2 = k_ref[...], v_ref[...]
        kh = [jnp.where(hm, k2, jnp.zeros_like(k2)) for hm in hms]

        def q_step(i, carry):
            dk, dv, dc0, dc1 = carry
            dcs = [dc0, dc1]
            rows = pl.ds(pl.multiple_of(i * TB, TB), TB)
            q2 = q_ref[rows, :] * 0.125
            do2, lse2, dd2 = do_ref[rows, :], l_ref[rows, :], dd_ref[rows, :]
            bias = tab_ref[i - j]
            if fox:
                ccv = cc_ref[rows, :]
            rsum = []
            for hh in range(2):
                qh = jnp.where(hms[hh], q2, jnp.zeros_like(q2))
                doh = jnp.where(hms[hh], do2, jnp.zeros_like(do2))
                s = _dot_nt(qh, k2) + bias
                if fox:
                    ccol = jnp.sum(jnp.where(lane == 2 * pair + hh, ccv, 0.0), axis=1, keepdims=True)
                    s = s + (ccol - cr_ref[0, 0, 2 * pair + hh])
                p = jnp.exp(s - lse2[:, HD * hh:HD * hh + 1])
                dv = dv + _dot_tn(p.astype(BF16), doh)
                dp = _dot_nt(doh, v2)
                ds = p * (dp - dd2[:, HD * hh:HD * hh + 1])
                dsb = ds.astype(BF16)
                dq_ref[rows, :] += _dot(dsb, kh[hh]) * 0.125
                dk = dk + _dot_tn(dsb, qh)
                if fox:
                    dcs[hh] = dcs[hh] - jnp.sum(ds, axis=0, keepdims=True)
                    rsum.append(jnp.sum(ds, axis=1, keepdims=True))
            if fox:
                dcq_ref[rows, :] += jnp.where(hms[0], rsum[0], rsum[1])
            return dk, dv, dcs[0], dcs[1]

        z = jnp.zeros((TB, 128), F32)
        zr = jnp.zeros((1, TB), F32)
        dk, dv, dc0, dc1 = lax.fori_loop(j, nb, q_step, (z, z, zr, zr))
        dk_ref[...] = dk.astype(BF16)
        dv_ref[...] = dv.astype(BF16)
        if fox:
            sub = lax.broadcasted_iota(jnp.int32, (8, TB), 0)
            dc_ref[0, 0, 0] = jnp.where(sub == 0, jnp.broadcast_to(dc0, (8, TB)),
                                        jnp.where(sub == 1, jnp.broadcast_to(dc1, (8, TB)), 0.0))

    seq = lambda c: pl.BlockSpec((S, 128), lambda b, p, j: (b, c + p))
    kvb = lambda c: pl.BlockSpec((TB, 128), lambda b, p, j: (b * nb + j, c + p))
    in_specs = [seq(col0), kvb(col0 + NPAIR), kvb(col0 + 2 * NPAIR), pl.BlockSpec((nb, TB, TB), lambda b, p, j: (0, 0, 0)),
                seq(0), seq(0), seq(0)]
    args = [p_qkv, p_qkv, p_qkv, tab, do, lse, dd]
    out_shape = [_sds((T, WA), F32), _sds((T, WA), BF16), _sds((T, WA), BF16)]
    out_specs = [seq(0), kvb(0), kvb(0)]
    if fox:
        in_specs += [pl.BlockSpec((S, 128), lambda b, p, j: (b, 0)), pl.BlockSpec((1, 1, 8, 1, TB), lambda b, p, j: (b, j, 0, 0, 0))]
        args += [c_col, c_row]
        out_shape.append(_sds((B, NPAIR, nb, 8, TB), F32))
        out_specs.append(pl.BlockSpec((1, 1, 1, 8, TB), lambda b, p, j: (b, p, j, 0, 0)))
        out_shape.append(_sds((T, WA), F32))
        out_specs.append(seq(0))
    return pl.pallas_call(
        body, name=name, grid=(B, NPAIR, nb), out_shape=out_shape, in_specs=in_specs, out_specs=out_specs,
        compiler_params=_cp("parallel", "parallel", "arbitrary"))(*args)


def _gate_bwd(dc, dcq, p_rest, bf, S, TB):
    T = p_rest.shape[0]
    B, nb = T // S, S // TB

    def body(dc_ref, dcq_ref, z_ref, b_ref, dz_ref, db_ref):
        @pl.when(pl.program_id(0) == 0)
        def _():
            db_ref[...] = jnp.zeros_like(db_ref)

        later = (lax.broadcasted_iota(jnp.int32, (TB, TB), 1) >= lax.broadcasted_iota(jnp.int32, (TB, TB), 0)).astype(F32)
        sub = lax.broadcasted_iota(jnp.int32, (128, TB), 0)
        lane = lax.broadcasted_iota(jnp.int32, (1, 128), 1)
        pick = (lax.broadcasted_iota(jnp.int32, (WA, 128), 0) == HD * lax.broadcasted_iota(jnp.int32, (WA, 128), 1)).astype(F32)
        carry = jnp.zeros((1, 128), F32)
        dbs = jnp.zeros((1, 128), F32)
        for j in reversed(range(nb)):
            xa = jnp.zeros((128, TB), F32)
            for p in range(NPAIR):
                blk = dc_ref[0, p, j]
                for hh in range(2):
                    xa = xa + jnp.where(sub == 2 * p + hh, jnp.broadcast_to(blk[hh:hh + 1, :], (128, TB)), 0.0)
            xq = _dot(dcq_ref[j * TB:(j + 1) * TB, :], pick, HIGHEST)
            part = _dot_nt(later, xa, HIGHEST) + _dot(later, xq, HIGHEST)
            tot = part + carry
            carry = carry + part[0:1, :]
            z = z_ref[j * TB:(j + 1) * TB, :] + b_ref[...]
            dz = jnp.where(lane < 6, tot * _sigmoid(-z), 0.0)
            dz_ref[j * TB:(j + 1) * TB, :] = dz.astype(BF16)
            dbs = dbs + jnp.sum(dz, axis=0, keepdims=True)
        db_ref[...] += dbs

    return pl.pallas_call(
        body, name="gate_bwd", grid=(B,), out_shape=[_sds((T, 128), BF16), _sds((1, 128), F32)],
        in_specs=[pl.BlockSpec((1, NPAIR, nb, 8, TB), lambda b: (b, 0, 0, 0, 0)), pl.BlockSpec((S, WA), lambda b: (b, 0)),
                  pl.BlockSpec((S, 128), lambda b: (b, 4)), pl.BlockSpec((1, 128), lambda b: (0, 0))],
        out_specs=[pl.BlockSpec((S, 128), lambda b: (b, 0)), pl.BlockSpec((1, 128), lambda b: (0, 0))],
        compiler_params=_cp("arbitrary"))(dc, dcq, p_rest, bf)


def _adamw(parts, w, m, v, name):
    npart, R, C = parts.shape
    tr = R
    for cand in (256, 128, 64, 32, 16, 8):
        if R % cand == 0 and cand * C * 4 <= (1 << 20):
            tr = cand
            break
    if R * C * 4 <= (1 << 20):
        tr = R

    def body(p_ref, w_ref, m_ref, v_ref, g_ref, d_ref, mo_ref, vo_ref):
        g = p_ref[0]
        for k in range(1, npart):
            g = g + p_ref[k]
        mn = ADAM_B1 * m_ref[...] + (1.0 - ADAM_B1) * g
        vn = ADAM_B2 * v_ref[...] + (1.0 - ADAM_B2) * (g * g)
        m_hat = mn / (1.0 - ADAM_B1 ** ADAM_STEP)
        v_hat = vn / (1.0 - ADAM_B2 ** ADAM_STEP)
        g_ref[...] = g
        d_ref[...] = -ADAM_LR * (m_hat / (jnp.sqrt(v_hat) + ADAM_EPS) + ADAM_WD * w_ref[...])
        mo_ref[...] = mn
        vo_ref[...] = vn

    blk = pl.BlockSpec((tr, C), lambda i: (i, 0))
    return pl.pallas_call(
        body, name=name, grid=(R // tr,), out_shape=[_sds((R, C), F32)] * 4,
        in_specs=[pl.BlockSpec((npart, tr, C), lambda i: (0, i, 0)), blk, blk, blk], out_specs=[blk] * 4,
        compiler_params=_cp("parallel"))(parts, w, m, v)


def _pad_w_in(w):
    z = jnp.zeros(w.shape[:-1] + (NPAD - N_IN,), w.dtype)
    return jnp.concatenate([w[..., :O_FA], w[..., O_QB:], w[..., O_FA:O_QB], z], axis=-1)


def _unpad_w_in(w):
    n_mid = N_IN - O_QB
    return jnp.concatenate([w[..., :O_FA], w[..., O_FA + n_mid:O_FA + n_mid + 6], w[..., O_FA:O_FA + n_mid]], axis=-1)


def _local_step(x, tgt, W, S):
    T = x.shape[0]
    TB = min(256, S)
    tab_fox, tab_dil = _bias_table(S, TB, True), _bias_table(S, TB, False)
    saved = []
    h = _rms_fwd(x, W["ln1_g"][0])
    for l in range(DEPTH):
        p_qkv, p_rest = _mm_in(h, W["w_in"][l])
        c_col, c_row = _gate_fwd(p_rest, W["b_forget"][l], S, TB)
        o_a, lse_a = _attn_fwd(p_qkv, 0, tab_fox, S, c_col, c_row, name="fox_fwd")
        o_b, lse_b = _attn_fwd(p_qkv, 3 * NPAIR, tab_dil, S, name="dil_fwd")
        y = _mix_fwd(o_a, o_b, p_rest, W["g_out_fox"][l], W["g_out_dil"][l], W["conv_w"][l], W["conv_b"][l],
                     W["cnorm_g"][l], W["cnorm_b"][l], S)
        x1, h2 = _mm_res(y[None], W["w_o"][l][None], x, W["ln2_g"][l], name="mm_o")
        u, hid = _up_fwd(h2, W["w_up"][l], W["ffn_conv_w"][l], W["ffn_conv_b"][l], S)
        g_next = W["ln1_g"][l + 1] if l + 1 < DEPTH else W["g_final"]
        x2, h_next = _mm_res(hid, W["w_down"][l], x1, g_next, name="mm_down")
        saved.append(dict(x=x, h=h, p_qkv=p_qkv, p_rest=p_rest, c_col=c_col, c_row=c_row, o_a=o_a, lse_a=lse_a, o_b=o_b,
                          lse_b=lse_b, y=y, x1=x1, h2=h2, u=u, hid=hid))
        x, h = x2, h_next
    dx, loss, dg_final = _final_loss(x, tgt, W["g_final"])
    G = {k: [None] * DEPTH for k in ("ln1_g", "w_in", "b_forget", "g_out_fox", "g_out_dil", "conv_w", "conv_b", "cnorm_g",
                                     "cnorm_b", "w_o", "ln2_g", "w_up", "ffn_conv_w", "ffn_conv_b", "w_down")}
    G["g_final"] = dg_final
    for l in reversed(range(DEPTH)):
        sv = saved[l]
        du, dcw = _down_bwd(dx, W["w_down"][l], sv["u"], W["ffn_conv_w"][l], W["ffn_conv_b"][l], S)
        G["w_down"][l] = _wgrad(sv["hid"], dx[None], D, "wg_down")[:, 0]
        G["ffn_conv_w"][l] = dcw[:, :, 0:3, :]
        G["ffn_conv_b"][l] = dcw[:, :, 3, :]
        G["w_up"][l] = _wgrad(sv["h2"][None], du.reshape(2 * NJ, T, FB), FB, "wg_up")[0]
        dx1, G["ln2_g"][l] = _dx_rms(du, W["w_up"][l], sv["x1"], W["ln2_g"][l], dx, True, "dx_up")
        G["w_o"][l] = _wgrad(sv["y"][None], dx1[None], D, "wg_o")[0, 0]
        dy = _mm_nt(dx1, W["w_o"][l], "mm_dy")
        do_a, dd_a, do_b, dd_b, dgvgg, dgn, dcvw, dcn = _mix_bwd(
            dy, sv["o_a"], sv["o_b"], sv["p_rest"], W["g_out_fox"][l], W["g_out_dil"][l], W["conv_w"][l], W["conv_b"][l],
            W["cnorm_g"][l], W["cnorm_b"][l], S)
        G["g_out_fox"][l], G["g_out_dil"][l] = dgn[0], dgn[1]
        G["conv_w"][l], G["conv_b"][l] = dcvw[:CONV_K], dcvw[CONV_K]
        G["cnorm_g"][l], G["cnorm_b"][l] = dcn[0], dcn[1]
        dq_a, dk_a, dv_a, dc, dcq = _attn_bwd(sv["p_qkv"], 0, tab_fox, do_a, sv["lse_a"], dd_a, S, sv["c_col"], sv["c_row"], name="fox_bwd")
        dq_b, dk_b, dv_b = _attn_bwd(sv["p_qkv"], 3 * NPAIR, tab_dil, do_b, sv["lse_b"], dd_b, S, name="dil_bwd")
        dfa, db = _gate_bwd(dc, dcq, sv["p_rest"], W["b_forget"][l], S, TB)
        G["b_forget"][l] = db[0, :6]
        dp = jnp.concatenate([dq_a.astype(BF16), dk_a, dv_a, dq_b.astype(BF16), dk_b, dv_b, dgvgg, dfa,
                              jnp.zeros((T, 128), BF16)], axis=1)
        G["w_in"][l] = _wgrad(sv["h"][None], dp[None], D, "wg_in")[0, 0]
        dx, G["ln1_g"][l] = _dx_rms(dp, W["w_in"][l], sv["x"], W["ln1_g"][l], dx1, False, "dx_in")
    return loss, dx, G


_SHARDED = ("w_in", "w_o", "w_up", "w_down", "conv_w", "ffn_conv_w")
_SMALL = ("ln1_g", "b_forget", "g_out_fox", "g_out_dil", "conv_b", "cnorm_g", "cnorm_b", "ln2_g", "ffn_conv_b", "g_final")
_ORDER = ("ln1_g", "w_in", "b_forget", "g_out_fox", "g_out_dil", "conv_w", "conv_b", "cnorm_g", "cnorm_b", "w_o", "ln2_g",
          "w_up", "ffn_conv_w", "ffn_conv_b", "w_down", "g_final")


def _full_weights(P, gathered):
    W = {}
    row = lambda a: [a[l][None, :] for l in range(DEPTH)]
    W["w_in"] = [_pad_w_in(gathered["w_in"][l].reshape(D, N_IN)) for l in range(DEPTH)]
    W["w_o"] = [gathered["w_o"][l].reshape(D, D) for l in range(DEPTH)]
    W["w_up"] = [gathered["w_up"][l].reshape(2, NJ, D, FB) for l in range(DEPTH)]
    W["w_down"] = [gathered["w_down"][l].reshape(NJ, FB, D) for l in range(DEPTH)]
    cw = [jnp.transpose(gathered["conv_w"][l].reshape(NDEV, CONV_K, CC // NDEV), (1, 0, 2)).reshape(CONV_K, CC) for l in range(DEPTH)]
    W["conv_w"] = [jnp.concatenate([c, jnp.zeros((1, CC), F32)], axis=0) for c in cw]
    W["ffn_conv_w"] = [gathered["ffn_conv_w"][l].reshape(2, NJ, 3, FB) for l in range(DEPTH)]
    W["ffn_conv_b"] = [P["ffn_conv_b"][l].reshape(2, NJ, 1, FB) for l in range(DEPTH)]
    W["b_forget"] = [jnp.concatenate([P["b_forget"][l], jnp.zeros((128 - 6,), F32)])[None, :] for l in range(DEPTH)]
    for k in ("ln1_g", "ln2_g", "g_out_fox", "g_out_dil", "conv_b", "cnorm_g", "cnorm_b"):
        W[k] = row(P[k])
    W["g_final"] = P["g_final"][None, :]
    return W


def kernel(x, ln1_g, w_in, b_forget, g_out_fox, g_out_dil, conv_w, conv_b, cnorm_g, cnorm_b, w_o, ln2_g, w_up, ffn_conv_w, ffn_conv_b, w_down, g_final, loss_target, m_ln1_g, m_w_in, m_b_forget, m_g_out_fox, m_g_out_dil, m_conv_w, m_conv_b, m_cnorm_g, m_cnorm_b, m_w_o, m_ln2_g, m_w_up, m_ffn_conv_w, m_ffn_conv_b, m_w_down, m_g_final, v_ln1_g, v_w_in, v_b_forget, v_g_out_fox, v_g_out_dil, v_conv_w, v_conv_b, v_cnorm_g, v_cnorm_b, v_w_o, v_ln2_g, v_w_up, v_ffn_conv_w, v_ffn_conv_b, v_w_down, v_g_final):
    P = dict(ln1_g=ln1_g, w_in=w_in, b_forget=b_forget, g_out_fox=g_out_fox, g_out_dil=g_out_dil, conv_w=conv_w, conv_b=conv_b,
             cnorm_g=cnorm_g, cnorm_b=cnorm_b, w_o=w_o, ln2_g=ln2_g, w_up=w_up, ffn_conv_w=ffn_conv_w, ffn_conv_b=ffn_conv_b,
             w_down=w_down, g_final=g_final)
    M = dict(ln1_g=m_ln1_g, w_in=m_w_in, b_forget=m_b_forget, g_out_fox=m_g_out_fox, g_out_dil=m_g_out_dil, conv_w=m_conv_w,
             conv_b=m_conv_b, cnorm_g=m_cnorm_g, cnorm_b=m_cnorm_b, w_o=m_w_o, ln2_g=m_ln2_g, w_up=m_w_up, ffn_conv_w=m_ffn_conv_w,
             ffn_conv_b=m_ffn_conv_b, w_down=m_w_down, g_final=m_g_final)
    V = dict(ln1_g=v_ln1_g, w_in=v_w_in, b_forget=v_b_forget, g_out_fox=v_g_out_fox, g_out_dil=v_g_out_dil, conv_w=v_conv_w,
             conv_b=v_conv_b, cnorm_g=v_cnorm_g, cnorm_b=v_cnorm_b, w_o=v_w_o, ln2_g=v_ln2_g, w_up=v_w_up, ffn_conv_w=v_ffn_conv_w,
             ffn_conv_b=v_ffn_conv_b, w_down=v_w_down, g_final=v_g_final)
    Bl, S, _ = x.shape
    T = Bl * S

    names = [(k, l) for l in range(DEPTH) for k in _SHARDED]
    send = [[P[k][l].astype(BF16) if k in ("w_in", "w_o", "w_up", "w_down") else P[k][l]] for k, l in names]
    got = _exchange(send, True, "gather_weights")
    gathered = {k: [None] * DEPTH for k in _SHARDED}
    for (k, l), arr in zip(names, got):
        gathered[k][l] = arr
    W = _full_weights(P, gathered)

    loss, dx, G = _local_step(x.reshape(T, D), loss_target.reshape(T, D), W, S)

    blocks = []
    for l in range(DEPTH):
        blocks.append(dict(
            w_in=_unpad_w_in(G["w_in"][l]).reshape(NDEV, D // NDEV, N_IN),
            w_o=G["w_o"][l].reshape(NDEV, D // NDEV, D),
            w_up=G["w_up"][l],
            w_down=G["w_down"][l].reshape(NDEV, DFF // NDEV, D),
            conv_w=jnp.transpose(G["conv_w"][l].reshape(CONV_K, NDEV, CC // NDEV), (1, 0, 2)),
            ffn_conv_w=G["ffn_conv_w"][l].reshape(NDEV, 3, FB)))
    a2a_in = [[blocks[l][k] for l in range(DEPTH)] for k in _SHARDED]
    parts = dict(zip(_SHARDED, _exchange(a2a_in, False, "scatter_grads")))

    small = [jnp.stack(G[k]).reshape(-1) for k in _SMALL[:-1]] + [G["g_final"].reshape(-1), loss[0, :1]]
    sizes = [int(s.shape[0]) for s in small]
    flat = jnp.concatenate(small)
    n_small = -(-flat.shape[0] // 1024) * 1024
    flat = jnp.concatenate([flat, jnp.zeros((n_small - flat.shape[0],), F32)]).reshape(n_small // 128, 128)
    small_parts = _exchange([[flat]], True, "gather_small")[0]

    out_g, out_d, out_m, out_v = {}, {}, {}, {}
    sg = _sum_parts(small_parts.reshape(NDEV, n_small // 128, 128), "sum_small").reshape(-1)
    off = 0
    summed = {}
    for k, n in zip(_SMALL + ("loss",), sizes):
        summed[k] = sg[off:off + n]
        off += n
    for k in _ORDER:
        shp = P[k].shape
        if k in _SHARDED:
            pr = parts[k].reshape((NDEV, -1, shp[-1]))
        else:
            pr = summed[k].reshape((1, -1, shp[-1]))
        R = pr.shape[1]
        g, d, mn, vn = _adamw(pr, P[k].reshape(R, -1), M[k].reshape(R, -1), V[k].reshape(R, -1), "adamw_" + k)
        out_g[k], out_d[k], out_m[k], out_v[k] = (t.reshape(shp) for t in (g, d, mn, vn))

    loss_out = summed["loss"].reshape(())
    grad_x = dx.reshape(Bl, S, D)
    return (loss_out, grad_x, *[out_g[k] for k in _ORDER], *[out_d[k] for k in _ORDER], *[out_m[k] for k in _ORDER],
            *[out_v[k] for k in _ORDER])
```

```python
import functools

import numpy as np
import jax
import jax.numpy as jnp
from jax import lax
from jax.experimental import pallas as pl
from jax.experimental.pallas import tpu as pltpu

F32 = jnp.float32
BF16 = jnp.bfloat16
GRAD_DTYPE = BF16
HIGHEST = lax.Precision.HIGHEST

D = 1024
DEPTH = 2
HD = 64
WA = 384
NPAIR = 3
CC = 256
CONV_K = 31
DFF = 2816
FB = 704
NJ = 4
NDEV = 8
EPS = 1e-6
NQKV = 2304
NREST = 768
NPAD = NQKV + NREST
O_FA, O_QB, N_IN = 1152, 1158, 2822
DILATION_PAIRS = ((128, 1), (512, 4), (2048, 16))
NEG = -1e30

ADAM_LR, ADAM_B1, ADAM_B2, ADAM_EPS, ADAM_WD, ADAM_STEP = 0.001, 0.9, 0.999, 1e-08, 0.01, 10

VMEM_LIMIT = 48 * 1024 * 1024


def _cp(*sem):
    return pltpu.CompilerParams(dimension_semantics=sem, vmem_limit_bytes=VMEM_LIMIT)


def _sds(shape, dtype):
    return jax.ShapeDtypeStruct(shape, dtype)


def _dot(a, b, prec=None):
    return jnp.dot(a, b, preferred_element_type=F32, precision=prec)


def _dot_nt(a, b, prec=None):
    return lax.dot_general(a, b, (((1,), (1,)), ((), ())), preferred_element_type=F32, precision=prec)


def _dot_tn(a, b):
    return lax.dot_general(a, b, (((0,), (0,)), ((), ())), preferred_element_type=F32)


def _sigmoid(z):
    return 1.0 / (1.0 + jnp.exp(-z))


def _rms_bwd(dh, x, g):
    r = lax.rsqrt(jnp.mean(x * x, axis=-1, keepdims=True) + EPS)
    xh = x * r
    dg = jnp.sum(dh * xh, axis=0, keepdims=True)
    dxh = dh * g
    dx = r * (dxh - xh * jnp.mean(dxh * xh, axis=-1, keepdims=True))
    return dx, dg


class _Plan:
    def __init__(self, groups, gather, slots=None, n_slots=None):
        self.groups, self.gather = groups, gather
        self.slots = slots or [list(range(len(g))) for g in groups]
        self.n_slots = n_slots or [len(g) for g in groups]
        self.flat = [a for g in groups for a in g]
        self.where = [(gi, s) for gi, g in enumerate(groups) for s in self.slots[gi]]
        self.zone_shapes = [_sds((NDEV, ns) + (g[0].shape if gather else g[0].shape[1:]), g[0].dtype)
                            for g, ns in zip(groups, self.n_slots)]

    def new_zones(self):
        return [lax.empty(z.shape, z.dtype) for z in self.zone_shapes]

    def me(self):
        x, y, c = lax.axis_index("x"), lax.axis_index("y"), lax.axis_index("c")
        return 4 * x + 2 * y + c

    def own_copies(self, ins, zones, sems):
        me = self.me()
        return [pltpu.make_async_copy(ins[a] if self.gather else ins[a].at[me], zones[gi].at[me, s], sems.at[a])
                for a, (gi, s) in enumerate(self.where)]

    def remote_copies(self, ins, zones, send_sems, recv_sems):
        x, y, c = lax.axis_index("x"), lax.axis_index("y"), lax.axis_index("c")
        me = 4 * x + 2 * y + c
        out = []
        for a, (gi, s) in enumerate(self.where):
            for k in range(1, NDEV):
                px, py, pc = x ^ ((k >> 2) & 1), y ^ ((k >> 1) & 1), c ^ (k & 1)
                out.append(pltpu.make_async_remote_copy(
                    src_ref=ins[a] if self.gather else ins[a].at[4 * px + 2 * py + pc], dst_ref=zones[gi].at[me, s],
                    send_sem=send_sems.at[a * (NDEV - 1) + k - 1], recv_sem=recv_sems.at[a * (NDEV - 1) + k - 1],
                    device_id=(px, py, pc), device_id_type=pl.DeviceIdType.MESH))
        return out


_ANY = pl.BlockSpec(memory_space=pl.ANY)
_HBM = pl.BlockSpec(memory_space=pltpu.HBM)
_SEM = pl.BlockSpec(memory_space=pltpu.SEMAPHORE)


def _exchange(plan, name, zones=None):
    n, nz = len(plan.flat), len(plan.groups)
    zones = zones or plan.new_zones()

    def body(*refs):
        ins, zin = refs[:n], refs[n:n + nz]
        send_sems, recv_sems, local_sems = refs[n + 2 * nz:]
        copies = plan.own_copies(ins, zin, local_sems) + plan.remote_copies(ins, zin, send_sems, recv_sems)
        for cp in copies:
            cp.start()
        for cp in copies:
            cp.wait()

    return pl.pallas_call(
        body, name=name, out_shape=plan.zone_shapes, in_specs=[_ANY] * (n + nz), out_specs=[_ANY] * nz,
        input_output_aliases={n + g: g for g in range(nz)},
        scratch_shapes=[pltpu.SemaphoreType.DMA((n * (NDEV - 1),)), pltpu.SemaphoreType.DMA((n * (NDEV - 1),)),
                        pltpu.SemaphoreType.DMA((n,))],
        compiler_params=pltpu.CompilerParams(has_side_effects=True),
    )(*plan.flat, *zones)


def _place_own(plan, name, zones=None):
    n, nz = len(plan.flat), len(plan.groups)
    zones = zones or plan.new_zones()

    def body(*refs):
        copies = plan.own_copies(refs[:n], refs[n:n + nz], refs[n + 2 * nz])
        for cp in copies:
            cp.start()
        for cp in copies:
            cp.wait()

    return pl.pallas_call(
        body, name=name, out_shape=plan.zone_shapes, in_specs=[_ANY] * (n + nz), out_specs=[_ANY] * nz,
        input_output_aliases={n + g: g for g in range(nz)}, scratch_shapes=[pltpu.SemaphoreType.DMA((n,))],
        compiler_params=pltpu.CompilerParams(has_side_effects=True),
    )(*plan.flat, *zones)


def _exchange_start(plan, name, zones):
    n, nz = len(plan.flat), len(plan.groups)
    hbm = lambda a: pltpu.HBM(a.shape, a.dtype)

    def body(*refs):
        ins, zin = refs[:n], refs[n:n + nz]
        send_sems, recv_sems = refs[n + nz], refs[n + nz + 1]
        token = refs[-1]
        for cp in plan.remote_copies(ins, zin, send_sems, recv_sems):
            cp.start()
        token[...] = jnp.zeros_like(token)

    outs = pl.pallas_call(
        body, name=name,
        out_shape=(pltpu.SemaphoreType.DMA((n * (NDEV - 1),)), pltpu.SemaphoreType.DMA((n * (NDEV - 1),)),
                   *[hbm(a) for a in plan.flat], *[hbm(z) for z in plan.zone_shapes], _sds((8, 128), F32)),
        in_specs=[_HBM] * (n + nz),
        out_specs=(_SEM, _SEM, *[_HBM] * (n + nz), pl.BlockSpec(memory_space=pltpu.VMEM)),
        input_output_aliases={i: 2 + i for i in range(n + nz)},
        compiler_params=pltpu.CompilerParams(has_side_effects=pltpu.SideEffectType.DATAFLOW_SIDE_EFFECTING),
    )(*[pltpu.with_memory_space_constraint(a, pltpu.HBM) for a in plan.flat],
      *[pltpu.with_memory_space_constraint(z, pltpu.HBM) for z in zones])
    return outs[:-1], outs[-1]


def _exchange_wait(plan, name, handle, after):
    n, nz = len(plan.flat), len(plan.groups)
    send_sems, recv_sems = handle[0], handle[1]
    thru = handle[2:]
    hbm = lambda a: pltpu.HBM(a.shape, a.dtype)

    def body(*refs):
        ins, zin = refs[:n], refs[n:n + nz]
        ssem, rsem = refs[n + nz], refs[n + nz + 1]
        for cp in plan.remote_copies(ins, zin, ssem, rsem):
            cp.wait_send()
            cp.wait_recv()

    outs = pl.pallas_call(
        body, name=name, out_shape=tuple(hbm(a) for a in thru),
        in_specs=[_HBM] * (n + nz) + [_SEM, _SEM, _ANY], out_specs=tuple([_HBM] * (n + nz)),
        input_output_aliases={i: i for i in range(n + nz)},
        compiler_params=pltpu.CompilerParams(has_side_effects=pltpu.SideEffectType.DATAFLOW_SIDE_EFFECTING),
    )(*thru, send_sems, recv_sems, after)
    return list(outs[n:])


def _sum_parts(parts, name):
    npart, R, C = parts.shape

    def body(p_ref, o_ref):
        s = p_ref[0]
        for k in range(1, npart):
            s = s + p_ref[k]
        o_ref[...] = s

    return pl.pallas_call(body, name=name, out_shape=_sds((R, C), F32))(parts)


def _rms_fwd(x, g):
    T = x.shape[0]
    tm = min(512, T)

    def body(x_ref, g_ref, o_ref):
        xv = x_ref[...]
        r = lax.rsqrt(jnp.mean(xv * xv, axis=-1, keepdims=True) + EPS)
        o_ref[...] = (xv * r * g_ref[...]).astype(BF16)

    return pl.pallas_call(
        body, name="rms_fwd", grid=(T // tm,), out_shape=_sds((T, D), BF16),
        in_specs=[pl.BlockSpec((tm, D), lambda i: (i, 0)), pl.BlockSpec((1, D), lambda i: (0, 0))],
        out_specs=pl.BlockSpec((tm, D), lambda i: (i, 0)), compiler_params=_cp("parallel"))(x, g)


def _mm_in(h, w):
    T = h.shape[0]
    tm = min(512, T)

    def body(h_ref, w_ref, q_ref, r_ref):
        hv = h_ref[...]
        q_ref[...] = _dot(hv, w_ref[:, :NQKV]).astype(BF16)
        r_ref[...] = _dot(hv, w_ref[:, NQKV:])

    return pl.pallas_call(
        body, name="mm_in", grid=(T // tm,), out_shape=[_sds((T, NQKV), BF16), _sds((T, NREST), F32)],
        in_specs=[pl.BlockSpec((tm, D), lambda i: (i, 0)), pl.BlockSpec((D, NPAD), lambda i: (0, 0))],
        out_specs=[pl.BlockSpec((tm, NQKV), lambda i: (i, 0)), pl.BlockSpec((tm, NREST), lambda i: (i, 0))],
        compiler_params=_cp("parallel"))(h, w)


AUG_Q1, AUG_K1 = 3, 0


def _aug_lane0(h):
    return HD * (1 - h % 2)


def _aug_tables():
    selq = np.zeros((NPAIR, 3 * 128, 128), np.float32)
    selk = np.zeros((NPAIR, 3 * 128, 128), np.float32)
    oneq = np.zeros((NPAIR, 1, 128), np.float32)
    onek = np.zeros((NPAIR, 1, 128), np.float32)
    for h in range(2 * NPAIR):
        p, l0 = h // 2, _aug_lane0(h)
        for piece in range(3):
            selq[p, 128 * piece + h, l0 + piece] = 1.0
            selk[p, 128 * piece + h, l0 + 3 + piece] = -1.0
            oneq[p, 0, l0 + AUG_Q1 + piece] = 1.0
            onek[p, 0, l0 + AUG_K1 + piece] = 1.0
    return [jnp.asarray(a, BF16) for a in (selq, selk, oneq, onek)]


def _gate_fwd(p_rest, bf, S, TB):
    T = p_rest.shape[0]
    B, nb = T // S, S // TB
    selq, selk, oneq, onek = _aug_tables()

    def body(z_ref, b_ref, sq_ref, sk_ref, oq_ref, ok_ref, aq_ref, ak_ref):
        tri = (lax.broadcasted_iota(jnp.int32, (TB, TB), 0) >= lax.broadcasted_iota(jnp.int32, (TB, TB), 1)).astype(F32)
        carry = jnp.zeros((1, 128), F32)
        for j in range(nb):
            rows = slice(j * TB, (j + 1) * TB)
            z = z_ref[rows, :] + b_ref[...]
            lf = jnp.minimum(z, 0.0) - jnp.log(1.0 + jnp.exp(-jnp.abs(z)))
            cb = _dot(tri, lf, HIGHEST) + carry
            carry = cb[TB - 1:TB, :]
            hi = cb.astype(BF16)
            mid = (cb - hi.astype(F32)).astype(BF16)
            lo = (cb - hi.astype(F32) - mid.astype(F32)).astype(BF16)
            pieces = jnp.concatenate([hi, mid, lo], axis=1)
            for p in range(NPAIR):
                aq_ref[p, rows, :] = (_dot(pieces, sq_ref[p]) + oq_ref[p].astype(F32)).astype(BF16)
                ak_ref[p, rows, :] = (_dot(pieces, sk_ref[p]) + ok_ref[p].astype(F32)).astype(BF16)

    full = lambda a: pl.BlockSpec(a.shape, lambda b: (0,) * a.ndim)
    return pl.pallas_call(
        body, name="gate_fwd", grid=(B,), out_shape=[_sds((NPAIR, T, 128), BF16), _sds((NPAIR, T, 128), BF16)],
        in_specs=[pl.BlockSpec((S, 128), lambda b: (b, 4)), pl.BlockSpec((1, 128), lambda b: (0, 0)),
                  full(selq), full(selk), full(oneq), full(onek)],
        out_specs=[pl.BlockSpec((NPAIR, S, 128), lambda b: (0, b, 0)), pl.BlockSpec((NPAIR, S, 128), lambda b: (0, b, 0))],
        compiler_params=_cp("parallel"))(p_rest, bf, selq, selk, oneq, onek)


def _bias_table(S, TB, fox):
    nb = S // TB
    dj = np.arange(nb)[:, None, None]
    delta = dj * TB + np.arange(TB)[None, None, :] - np.arange(TB)[None, :, None]
    if fox:
        mult = (delta >= 0).astype(np.float64)
    else:
        mult = np.zeros(delta.shape)
        for w, d in DILATION_PAIRS:
            mult += (delta >= 0) & (delta % d == 0) & (delta <= w)
    with np.errstate(divide="ignore"):
        tab = np.where(mult > 0, np.log(np.maximum(mult, 1e-30)), NEG)
    tab = np.concatenate([tab, np.full((1, TB, TB), NEG)], axis=0)
    return jnp.asarray(tab, F32)


def _head_masks():
    lane = lax.broadcasted_iota(jnp.int32, (1, 128), 1)
    return [lane < HD, lane >= HD]


def _transpose_bf16(a):
    return a.astype(F32).T.astype(BF16)


def _col_reduce(x, op):
    while x.shape[0] > 8:
        half = x.shape[0] // 2
        x = op(x[:half], x[half:])
    return jnp.max(x, axis=0, keepdims=True) if op is jnp.maximum else jnp.sum(x, axis=0, keepdims=True)


def _attn_fwd(p_qkv, col0, tab, S, aq=None, ak=None, name="attn"):
    T = p_qkv.shape[0]
    nb, TB = tab.shape[0] - 1, tab.shape[1]
    B = T // S
    fox = aq is not None

    def body(*refs):
        if fox:
            q_ref, k_ref, v_ref, tab_ref, aq_ref, ak_ref, o_ref, l_ref, vt_scr, acc_scr, st_scr = refs
        else:
            q_ref, k_ref, v_ref, tab_ref, o_ref, l_ref, vt_scr, acc_scr, st_scr = refs
        i = pl.program_id(2)
        hms = _head_masks()

        @pl.when(i == 0)
        def _():
            for jj in range(nb):
                vt_scr[jj] = _transpose_bf16(v_ref[jj * TB:(jj + 1) * TB, :])

        q2 = q_ref[...] * 0.125
        spare_q = aq_ref[0] if fox else jnp.zeros_like(q2)
        qa = [jnp.where(hm, q2, spare_q) for hm in hms]
        acc_scr[...] = jnp.zeros_like(acc_scr)

        def scores(j):
            jc = jnp.minimum(j, nb - 1)
            rows = pl.ds(pl.multiple_of(jc * TB, TB), TB)
            k2 = k_ref[rows, :]
            bias = tab_ref[jnp.where(j <= i, i - j, nb)]
            return [_dot_nt(jnp.where(hms[hh], k2, ak_ref[0, rows, :]) if fox else k2, qa[hh]) + bias for hh in range(2)]

        def absorb(j, sts, stats):
            jc = jnp.minimum(j, nb - 1)
            alphas, pts = [], []
            for hh in range(2):
                m_old, l_old = stats[2 * hh], stats[2 * hh + 1]
                m_new = jnp.maximum(m_old, _col_reduce(sts[hh], jnp.maximum))
                pt = jnp.exp(sts[hh] - m_new)
                alphas.append(jnp.exp(m_old - m_new))
                stats[2 * hh] = m_new
                stats[2 * hh + 1] = alphas[hh] * l_old + _col_reduce(pt, jnp.add)
                pts.append(pt.astype(BF16))
            pvs = [_dot(vt_scr[jc, HD * hh:HD * (hh + 1), :], pts[hh]) for hh in range(2)]
            for hh in range(2):
                half = slice(HD * hh, HD * (hh + 1))
                acc_scr[half, :] = alphas[hh] * acc_scr[half, :] + pvs[hh]

        def kv_pair(jj, carry):
            stats = list(carry)
            j0 = 2 * jj
            st0 = [st_scr[hh] for hh in range(2)]
            st1 = scores(j0 + 1)
            absorb(j0, st0, stats)
            nxt = scores(j0 + 2)
            for hh in range(2):
                st_scr[hh] = nxt[hh]
            absorb(j0 + 1, st1, stats)
            return tuple(stats)

        first = scores(0)
        for hh in range(2):
            st_scr[hh] = first[hh]
        row = lambda v: jnp.full((1, TB), v, F32)
        m0, l0, m1, l1 = lax.fori_loop(0, i // 2 + 1, kv_pair, (row(NEG), row(0.0), row(NEG), row(0.0)))
        top = lax.broadcasted_iota(jnp.int32, (128, 1), 0) < HD
        o_ref[...] = (acc_scr[...] * jnp.where(top, 1.0 / l0, 1.0 / l1)).T
        l_ref[0, 0] = m0 + jnp.log(l0)
        l_ref[0, 1] = m1 + jnp.log(l1)

    in_specs = [pl.BlockSpec((TB, 128), lambda b, p, i: (b * nb + i, col0 + p)),
                pl.BlockSpec((S, 128), lambda b, p, i: (b, col0 + NPAIR + p)),
                pl.BlockSpec((S, 128), lambda b, p, i: (b, col0 + 2 * NPAIR + p)),
                pl.BlockSpec((nb + 1, TB, TB), lambda b, p, i: (0, 0, 0))]
    args = [p_qkv, p_qkv, p_qkv, tab]
    if fox:
        in_specs += [pl.BlockSpec((1, TB, 128), lambda b, p, i: (p, b * nb + i, 0)),
                     pl.BlockSpec((1, S, 128), lambda b, p, i: (p, b, 0))]
        args += [aq, ak]
    return pl.pallas_call(
        body, name=name, grid=(B, NPAIR, nb), out_shape=[_sds((T, WA), F32), _sds((T // TB, 2 * NPAIR, 1, TB), F32)],
        in_specs=in_specs,
        out_specs=[pl.BlockSpec((TB, 128), lambda b, p, i: (b * nb + i, p)),
                   pl.BlockSpec((1, 2, 1, TB), lambda b, p, i: (b * nb + i, p, 0, 0))],
        scratch_shapes=[pltpu.VMEM((nb, 128, TB), BF16), pltpu.VMEM((128, TB), F32), pltpu.VMEM((2, TB, TB), F32)],
        compiler_params=_cp("parallel", "parallel", "arbitrary"))(*args)


def _conv_taps(buf, w_ref, first_row, rows):
    acc = w_ref[0:1, :] * buf[pl.ds(first_row, rows), :]
    for k in range(1, CONV_K):
        acc = acc + w_ref[k:k + 1, :] * buf[pl.ds(first_row + k, rows), :]
    return acc


def _mix_fwd(o_a, o_b, p_rest, g_a, g_b, cw, cb, ng, nbias, S):
    T = o_a.shape[0]
    tm = min(256, S)
    nps = S // tm
    HALO = 32

    def body(oa_ref, ob_ref, pt_ref, ph_ref, ga_ref, gb_ref, cw_ref, cb_ref, ng_ref, nb_ref, y_ref, buf):
        i = pl.program_id(0)
        first = (i % nps) == 0
        for o_ref, g_ref, c0 in ((oa_ref, ga_ref, 0), (ob_ref, gb_ref, WA)):
            o = o_ref[...]
            r = lax.rsqrt(jnp.mean(o * o, axis=-1, keepdims=True) + EPS)
            y_ref[:, c0:c0 + WA] = (o * r * g_ref[...]).astype(BF16)
        ph = jnp.where(first, 0.0, ph_ref[...])
        buf[0:HALO, :] = ph[:, :CC] * _sigmoid(ph[:, CC:])
        pt = pt_ref[...]
        buf[HALO:HALO + tm, :] = pt[:, :CC] * _sigmoid(pt[:, CC:])
        cv = _conv_taps(buf, cw_ref, HALO - CONV_K + 1, tm) + cb_ref[...]
        xc = cv - jnp.mean(cv, axis=-1, keepdims=True)
        lo = xc * lax.rsqrt(jnp.mean(xc * xc, axis=-1, keepdims=True) + EPS) * ng_ref[...] + nb_ref[...]
        y_ref[:, 2 * WA:] = (lo * _sigmoid(lo)).astype(BF16)

    row = lambda w: pl.BlockSpec((1, w), lambda i: (0, 0))
    return pl.pallas_call(
        body, name="mix_fwd", grid=(T // tm,), out_shape=_sds((T, D), BF16),
        in_specs=[pl.BlockSpec((tm, WA), lambda i: (i, 0)), pl.BlockSpec((tm, WA), lambda i: (i, 0)),
                  pl.BlockSpec((tm, 2 * CC), lambda i: (i, 0)),
                  pl.BlockSpec((HALO, 2 * CC), lambda i: (jnp.maximum(i * (tm // HALO) - 1, 0), 0)),
                  row(WA), row(WA), pl.BlockSpec((32, CC), lambda i: (0, 0)), row(CC), row(CC), row(CC)],
        out_specs=pl.BlockSpec((tm, D), lambda i: (i, 0)),
        scratch_shapes=[pltpu.VMEM((tm + HALO, CC), F32)],
        compiler_params=_cp("parallel"))(o_a, o_b, p_rest, p_rest, g_a, g_b, cw, cb, ng, nbias)


def _mm_res(a, w, resid, g, name):
    nk, T, kb = a.shape
    tm = min(512, T)

    def body(a_ref, w_ref, r_ref, g_ref, x_ref, h_ref, acc):
        k = pl.program_id(1)

        @pl.when(k == 0)
        def _():
            acc[...] = r_ref[...]

        acc[...] += _dot(a_ref[0], w_ref[0])

        @pl.when(k == nk - 1)
        def _():
            xv = acc[...]
            x_ref[...] = xv
            r = lax.rsqrt(jnp.mean(xv * xv, axis=-1, keepdims=True) + EPS)
            h_ref[...] = (xv * r * g_ref[...]).astype(BF16)

    return pl.pallas_call(
        body, name=name, grid=(T // tm, nk), out_shape=[_sds((T, D), F32), _sds((T, D), BF16)],
        in_specs=[pl.BlockSpec((1, tm, kb), lambda i, k: (k, i, 0)), pl.BlockSpec((1, kb, D), lambda i, k: (k, 0, 0)),
                  pl.BlockSpec((tm, D), lambda i, k: (i, 0)), pl.BlockSpec((1, D), lambda i, k: (0, 0))],
        out_specs=[pl.BlockSpec((tm, D), lambda i, k: (i, 0)), pl.BlockSpec((tm, D), lambda i, k: (i, 0))],
        scratch_shapes=[pltpu.VMEM((tm, D), F32)],
        compiler_params=_cp("parallel", "arbitrary"))(a, w, resid, g)


def _up_fwd(h2, w_up, fcw, fcb, S):
    T = h2.shape[0]
    tm = min(512, S)
    nps = S // tm

    def body(h_ref, hh_ref, w_ref, cw_ref, cb_ref, u_ref, hid_ref, buf):
        i = pl.program_id(1)
        first = (i % nps) == 0
        hv = h_ref[...]
        hh = jnp.where(first, jnp.zeros_like(hh_ref[...]), hh_ref[...])
        c = []
        for half in range(2):
            w = w_ref[half, 0]
            u = _dot(hv, w)
            u_ref[half, 0] = u
            buf[half, 0:8, :] = _dot(hh, w)
            buf[half, 8:8 + tm, :] = u
            cw = cw_ref[half, 0]
            c.append(cb_ref[half, 0] + cw[0:1] * buf[half, pl.ds(6, tm), :] + cw[1:2] * buf[half, pl.ds(7, tm), :] + cw[2:3] * u)
        hid_ref[0] = (c[0] * _sigmoid(c[0]) * c[1]).astype(BF16)

    return pl.pallas_call(
        body, name="up_fwd", grid=(NJ, T // tm), out_shape=[_sds((2, NJ, T, FB), F32), _sds((NJ, T, FB), BF16)],
        in_specs=[pl.BlockSpec((tm, D), lambda j, i: (i, 0)),
                  pl.BlockSpec((8, D), lambda j, i: (jnp.maximum(i * (tm // 8) - 1, 0), 0)),
                  pl.BlockSpec((2, 1, D, FB), lambda j, i: (0, j, 0, 0)),
                  pl.BlockSpec((2, 1, 3, FB), lambda j, i: (0, j, 0, 0)),
                  pl.BlockSpec((2, 1, 1, FB), lambda j, i: (0, j, 0, 0))],
        out_specs=[pl.BlockSpec((2, 1, tm, FB), lambda j, i: (0, j, i, 0)), pl.BlockSpec((1, tm, FB), lambda j, i: (j, i, 0))],
        scratch_shapes=[pltpu.VMEM((2, tm + 8, FB), F32)],
        compiler_params=_cp("parallel", "parallel"))(h2, h2, w_up, fcw, fcb)


def _final_loss(x, tgt, g):
    T = x.shape[0]
    tm = min(512, T)

    def body(x_ref, t_ref, g_ref, dx_ref, loss_ref, dg_ref):
        @pl.when(pl.program_id(0) == 0)
        def _():
            loss_ref[...] = jnp.zeros_like(loss_ref)
            dg_ref[...] = jnp.zeros_like(dg_ref)

        xv, gv = x_ref[...], g_ref[...]
        r = lax.rsqrt(jnp.mean(xv * xv, axis=-1, keepdims=True) + EPS)
        e = xv * r * gv - t_ref[...]
        loss_ref[...] += 0.5 * jnp.sum(jnp.mean(e * e, axis=-1, keepdims=True), axis=0, keepdims=True)
        dx, dg = _rms_bwd(e * (1.0 / D), xv, gv)
        dx_ref[...] = dx
        dg_ref[...] += dg

    return pl.pallas_call(
        body, name="final_loss", grid=(T // tm,), out_shape=[_sds((T, D), F32), _sds((1, 128), F32), _sds((1, D), F32)],
        in_specs=[pl.BlockSpec((tm, D), lambda i: (i, 0)), pl.BlockSpec((tm, D), lambda i: (i, 0)), pl.BlockSpec((1, D), lambda i: (0, 0))],
        out_specs=[pl.BlockSpec((tm, D), lambda i: (i, 0)), pl.BlockSpec((1, 128), lambda i: (0, 0)), pl.BlockSpec((1, D), lambda i: (0, 0))],
        compiler_params=_cp("arbitrary"))(x, tgt, g)


def _down_bwd(dx2, w_down, u, fcw, fcb, S):
    T = dx2.shape[0]
    tm = min(512, S)
    nps = S // tm
    nblk8 = T // 8

    def body(dx_ref, dxn_ref, wd_ref, ut_ref, up_ref, un_ref, cw_ref, cb_ref, du_ref, dcw_ref, dxb, ubuf, dcbuf):
        i = pl.program_id(1)
        first, last = (i % nps) == 0, (i % nps) == nps - 1

        @pl.when(i == 0)
        def _():
            dcw_ref[...] = jnp.zeros_like(dcw_ref)

        dxb[0:tm, :] = dx_ref[...].astype(BF16)
        dxb[tm:tm + 8, :] = jnp.where(last, 0.0, dxn_ref[...]).astype(BF16)
        dh = _dot_nt(dxb[...], wd_ref[0])
        ce = []
        for half in range(2):
            ubuf[half, 0:8, :] = jnp.where(first, 0.0, up_ref[half, 0])
            ubuf[half, 8:8 + tm, :] = ut_ref[half, 0]
            ubuf[half, 8 + tm:16 + tm, :] = un_ref[half, 0]
            cw = cw_ref[half, 0]
            ce.append(cb_ref[half, 0] + cw[0:1] * ubuf[half, pl.ds(6, tm + 8), :] + cw[1:2] * ubuf[half, pl.ds(7, tm + 8), :]
                      + cw[2:3] * ubuf[half, pl.ds(8, tm + 8), :])
        sg = _sigmoid(ce[0])
        dc = [dh * ce[1] * (sg * (1.0 + ce[0] * (1.0 - sg))), dh * (ce[0] * sg)]
        for half in range(2):
            cw = cw_ref[half, 0]
            dcbuf[...] = dc[half]
            dct = dc[half][0:tm]
            du_ref[half, 0] = (cw[2:3] * dct + cw[1:2] * dcbuf[pl.ds(1, tm), :] + cw[0:1] * dcbuf[pl.ds(2, tm), :]).astype(BF16)
            for k in range(3):
                dcw_ref[half, 0, k:k + 1, :] += jnp.sum(dct * ubuf[half, pl.ds(6 + k, tm), :], axis=0, keepdims=True)
            dcw_ref[half, 0, 3:4, :] += jnp.sum(dct, axis=0, keepdims=True)

    ublk = lambda rows, imap: pl.BlockSpec((2, 1, rows, FB), imap)
    return pl.pallas_call(
        body, name="down_bwd", grid=(NJ, T // tm), out_shape=[_sds((2, NJ, T, FB), BF16), _sds((2, NJ, 8, FB), F32)],
        in_specs=[pl.BlockSpec((tm, D), lambda j, i: (i, 0)),
                  pl.BlockSpec((8, D), lambda j, i: (jnp.minimum((i + 1) * (tm // 8), nblk8 - 1), 0)),
                  pl.BlockSpec((1, FB, D), lambda j, i: (j, 0, 0)),
                  ublk(tm, lambda j, i: (0, j, i, 0)),
                  ublk(8, lambda j, i: (0, j, jnp.maximum(i * (tm // 8) - 1, 0), 0)),
                  ublk(8, lambda j, i: (0, j, jnp.minimum((i + 1) * (tm // 8), nblk8 - 1), 0)),
                  pl.BlockSpec((2, 1, 3, FB), lambda j, i: (0, j, 0, 0)),
                  pl.BlockSpec((2, 1, 1, FB), lambda j, i: (0, j, 0, 0))],
        out_specs=[ublk(tm, lambda j, i: (0, j, i, 0)), pl.BlockSpec((2, 1, 8, FB), lambda j, i: (0, j, 0, 0))],
        scratch_shapes=[pltpu.VMEM((tm + 8, D), BF16), pltpu.VMEM((2, tm + 16, FB), F32), pltpu.VMEM((tm + 8, FB), F32)],
        compiler_params=_cp("parallel", "arbitrary"))(dx2, dx2, w_down, u, u, u, fcw, fcb)


def _wgrad(a, b, tn, name):
    na, T, ka = a.shape
    nb, _, kb = b.shape
    tk = min(512, T)
    nt = T // tk

    def body(a_ref, b_ref, o_ref, acc):
        t = pl.program_id(3)

        @pl.when(t == 0)
        def _():
            acc[...] = jnp.zeros_like(acc)

        acc[...] += _dot_tn(a_ref[0], b_ref[0].astype(BF16))

        @pl.when(t == nt - 1)
        def _():
            o_ref[0, 0] = acc[...].astype(GRAD_DTYPE)

    return pl.pallas_call(
        body, name=name, grid=(na, nb, kb // tn, nt), out_shape=_sds((na, nb, ka, kb), GRAD_DTYPE),
        in_specs=[pl.BlockSpec((1, tk, ka), lambda ia, ib, n, t: (ia, t, 0)), pl.BlockSpec((1, tk, tn), lambda ia, ib, n, t: (ib, t, n))],
        out_specs=pl.BlockSpec((1, 1, ka, tn), lambda ia, ib, n, t: (ia, ib, 0, n)),
        scratch_shapes=[pltpu.VMEM((ka, tn), F32)],
        compiler_params=_cp("parallel", "parallel", "parallel", "arbitrary"))(a, b)


def _dx_rms(dy, w, x, g, dres, blocked, name):
    T = x.shape[0]
    tm = min(512, T)
    nk = NJ if blocked else dy.shape[1] // D

    def body(dy_ref, w_ref, x_ref, g_ref, r_ref, dx_ref, dg_ref, acc):
        i, k = pl.program_id(0), pl.program_id(1)

        @pl.when((i == 0) & (k == 0))
        def _():
            dg_ref[...] = jnp.zeros_like(dg_ref)

        @pl.when(k == 0)
        def _():
            acc[...] = jnp.zeros_like(acc)

        if blocked:
            acc[...] += _dot_nt(dy_ref[0, 0], w_ref[0, 0]) + _dot_nt(dy_ref[1, 0], w_ref[1, 0])
        else:
            acc[...] += _dot_nt(dy_ref[...], w_ref[...])

        @pl.when(k == nk - 1)
        def _():
            dx, dg = _rms_bwd(acc[...], x_ref[...], g_ref[...])
            dx_ref[...] = r_ref[...] + dx
            dg_ref[...] += dg

    if blocked:
        dy_spec = pl.BlockSpec((2, 1, tm, FB), lambda i, k: (0, k, i, 0))
        w_spec = pl.BlockSpec((2, 1, D, FB), lambda i, k: (0, k, 0, 0))
    else:
        dy_spec = pl.BlockSpec((tm, D), lambda i, k: (i, k))
        w_spec = pl.BlockSpec((D, D), lambda i, k: (0, k))
    tile = pl.BlockSpec((tm, D), lambda i, k: (i, 0))
    return pl.pallas_call(
        body, name=name, grid=(T // tm, nk), out_shape=[_sds((T, D), F32), _sds((1, D), F32)],
        in_specs=[dy_spec, w_spec, tile, pl.BlockSpec((1, D), lambda i, k: (0, 0)), tile],
        out_specs=[tile, pl.BlockSpec((1, D), lambda i, k: (0, 0))],
        scratch_shapes=[pltpu.VMEM((tm, D), F32)],
        compiler_params=_cp("arbitrary", "arbitrary"))(dy, w, x, g, dres)


def _mm_nt(a, w, name):
    T = a.shape[0]
    tm = min(512, T)

    def body(a_ref, w_ref, o_ref):
        o_ref[...] = _dot_nt(a_ref[...].astype(BF16), w_ref[...])

    return pl.pallas_call(
        body, name=name, grid=(T // tm,), out_shape=_sds((T, D), F32),
        in_specs=[pl.BlockSpec((tm, D), lambda i: (i, 0)), pl.BlockSpec((D, D), lambda i: (0, 0))],
        out_specs=pl.BlockSpec((tm, D), lambda i: (i, 0)), compiler_params=_cp("parallel"))(a, w)


def _mix_bwd(dy, o_a, o_b, p_rest, g_a, g_b, cw, cb, ng, nbias, S):
    T = dy.shape[0]
    tm = min(256, S)
    nps = S // tm
    HALO = 32
    nblkh = T // HALO
    RE = tm + HALO

    def body(dy_ref, dyn_ref, oa_ref, ob_ref, pt_ref, pp_ref, pn_ref, ga_ref, gb_ref, cw_ref, cb_ref, ng_ref, nb_ref,
             doa_ref, dda_ref, dob_ref, ddb_ref, dg_ref, dgn_ref, dcw_ref, dcn_ref, gbuf, dvbuf):
        i = pl.program_id(0)
        first, last = (i % nps) == 0, (i % nps) == nps - 1

        @pl.when(i == 0)
        def _():
            dgn_ref[...] = jnp.zeros_like(dgn_ref)
            dcw_ref[...] = jnp.zeros_like(dcw_ref)
            dcn_ref[...] = jnp.zeros_like(dcn_ref)

        head_of = (lax.broadcasted_iota(jnp.int32, (8, WA), 1) // HD == lax.broadcasted_iota(jnp.int32, (8, WA), 0)).astype(F32)
        for n, (o_ref, g_ref, do_ref, dd_ref) in enumerate(((oa_ref, ga_ref, doa_ref, dda_ref), (ob_ref, gb_ref, dob_ref, ddb_ref))):
            o = o_ref[...]
            do, dg = _rms_bwd(dy_ref[:, n * WA:(n + 1) * WA], o, g_ref[...])
            dob = do.astype(BF16)
            do_ref[...] = dob
            ddt = _dot_nt(head_of, dob.astype(F32) * o, HIGHEST)
            for hd in range(8):
                dd_ref[0, hd] = ddt[hd:hd + 1, :]
            dgn_ref[n:n + 1, :] += dg

        pp = jnp.where(first, 0.0, pp_ref[...])
        gbuf[0:HALO, :] = pp[:, :CC] * _sigmoid(pp[:, CC:])
        pt = pt_ref[...]
        gv, sgg = pt[:, :CC], _sigmoid(pt[:, CC:])
        gbuf[HALO:HALO + tm, :] = gv * sgg
        pn = pn_ref[...]
        gbuf[HALO + tm:2 * HALO + tm, :] = pn[:, :CC] * _sigmoid(pn[:, CC:])
        cv = _conv_taps(gbuf, cw_ref, HALO - CONV_K + 1, RE) + cb_ref[...]
        xc = cv - jnp.mean(cv, axis=-1, keepdims=True)
        r = lax.rsqrt(jnp.mean(xc * xc, axis=-1, keepdims=True) + EPS)
        xh = xc * r
        lo = xh * ng_ref[...] + nb_ref[...]
        sg = _sigmoid(lo)
        dyc = jnp.concatenate([dy_ref[:, 2 * WA:], jnp.where(last, 0.0, dyn_ref[...])], axis=0)
        dlo = dyc * (sg * (1.0 + lo * (1.0 - sg)))
        dcn_ref[0:1, :] += jnp.sum(dlo[0:tm] * xh[0:tm], axis=0, keepdims=True)
        dcn_ref[1:2, :] += jnp.sum(dlo[0:tm], axis=0, keepdims=True)
        dxh = dlo * ng_ref[...]
        dcv = r * (dxh - jnp.mean(dxh, axis=-1, keepdims=True) - xh * jnp.mean(dxh * xh, axis=-1, keepdims=True))
        dvbuf[...] = dcv
        dct = dcv[0:tm]
        dcw_ref[CONV_K:CONV_K + 1, :] += jnp.sum(dct, axis=0, keepdims=True)
        dglu = jnp.zeros((tm, CC), F32)
        for k in range(CONV_K):
            dcw_ref[k:k + 1, :] += jnp.sum(dct * gbuf[pl.ds(HALO - CONV_K + 1 + k, tm), :], axis=0, keepdims=True)
            dglu = dglu + cw_ref[k:k + 1, :] * dvbuf[pl.ds(CONV_K - 1 - k, tm), :]
        dg_ref[:, :CC] = (dglu * sgg).astype(BF16)
        dg_ref[:, CC:] = (dglu * gv * sgg * (1.0 - sgg)).astype(BF16)

    row = lambda w: pl.BlockSpec((1, w), lambda i: (0, 0))
    tile = lambda w: pl.BlockSpec((tm, w), lambda i: (i, 0))
    nxt = lambda i: jnp.minimum((i + 1) * (tm // HALO), nblkh - 1)
    const = lambda r, w: pl.BlockSpec((r, w), lambda i: (0, 0))
    ddrow = pl.BlockSpec((1, 8, 1, tm), lambda i: (i, 0, 0, 0))
    return pl.pallas_call(
        body, name="mix_bwd", grid=(T // tm,),
        out_shape=[_sds((T, WA), BF16), _sds((T // tm, 8, 1, tm), F32), _sds((T, WA), BF16), _sds((T // tm, 8, 1, tm), F32),
                   _sds((T, 2 * CC), BF16), _sds((8, WA), F32), _sds((32, CC), F32), _sds((8, CC), F32)],
        in_specs=[tile(D), pl.BlockSpec((HALO, CC), lambda i: (nxt(i), 3)), tile(WA), tile(WA), tile(2 * CC),
                  pl.BlockSpec((HALO, 2 * CC), lambda i: (jnp.maximum(i * (tm // HALO) - 1, 0), 0)),
                  pl.BlockSpec((HALO, 2 * CC), lambda i: (nxt(i), 0)),
                  row(WA), row(WA), const(32, CC), row(CC), row(CC), row(CC)],
        out_specs=[tile(WA), ddrow, tile(WA), ddrow, tile(2 * CC), const(8, WA), const(32, CC), const(8, CC)],
        scratch_shapes=[pltpu.VMEM((tm + 2 * HALO, CC), F32), pltpu.VMEM((RE, CC), F32)],
        compiler_params=_cp("arbitrary"))(dy, dy, o_a, o_b, p_rest, p_rest, p_rest, g_a, g_b, cw, cb, ng, nbias)


def _attn_bwd(p_qkv, col0, tab, do, lse, dd, S, aq=None, ak=None, name="attn_bwd"):
    T = p_qkv.shape[0]
    nb, TB = tab.shape[0] - 1, tab.shape[1]
    B = T // S
    fox = aq is not None

    def body(*refs):
        if fox:
            (q_ref, k_ref, v_ref, tab_ref, do_ref, l_ref, dd_ref, aq_ref, ak_ref,
             dq_ref, dk_ref, dv_ref, dqa_ref, dka_ref, dqt_scr, dk_scr, dv_scr, sd_scr) = refs
        else:
            q_ref, k_ref, v_ref, tab_ref, do_ref, l_ref, dd_ref, dq_ref, dk_ref, dv_ref, dqt_scr, dk_scr, dv_scr, sd_scr = refs
        j = pl.program_id(2)
        hms = _head_masks()

        @pl.when(j == 0)
        def _():
            dqt_scr[...] = jnp.zeros_like(dqt_scr)

        dk_scr[...] = jnp.zeros_like(dk_scr)
        dv_scr[...] = jnp.zeros_like(dv_scr)
        k2, v2 = k_ref[...], v_ref[...]
        ka = [jnp.where(hm, k2, ak_ref[0] if fox else jnp.zeros_like(k2)) for hm in hms]
        kat = [_transpose_bf16(a) for a in ka]

        def operands(i):
            ic = jnp.minimum(i, nb - 1)
            rows = pl.ds(pl.multiple_of(ic * TB, TB), TB)
            q2 = q_ref[rows, :] * 0.125
            do2 = do_ref[rows, :]
            qas = [jnp.where(hms[hh], q2, aq_ref[0, rows, :] if fox else jnp.zeros_like(q2)) for hh in range(2)]
            dohs = [jnp.where(hms[hh], do2, jnp.zeros_like(do2)) for hh in range(2)]
            return ic, qas, dohs

        def scores(i):
            ic, qas, dohs = operands(i)
            bias = tab_ref[jnp.where(i < nb, i - j, nb)]
            return [(_dot_nt(ka[hh], qas[hh]) + bias, _dot_nt(v2, dohs[hh])) for hh in range(2)]

        def absorb(i, sd):
            ic, qas, dohs = operands(i)
            for hh in range(2):
                st, dpt = sd[hh]
                pt = jnp.exp(st - l_ref[ic, hh])
                dsb = (pt * (dpt - dd_ref[ic, hh])).astype(BF16)
                dv_scr[...] += _dot(pt.astype(BF16), dohs[hh])
                dk_scr[hh] += _dot(dsb, qas[hh])
                dqt_scr[hh, ic] += _dot(kat[hh], dsb)

        def q_pair(ii, carry):
            i0 = j + 2 * ii
            sd0 = [(sd_scr[hh, 0], sd_scr[hh, 1]) for hh in range(2)]
            sd1 = scores(i0 + 1)
            absorb(i0, sd0)
            nxt = scores(i0 + 2)
            for hh in range(2):
                sd_scr[hh, 0], sd_scr[hh, 1] = nxt[hh]
            absorb(i0 + 1, sd1)
            return carry

        first = scores(j)
        for hh in range(2):
            sd_scr[hh, 0], sd_scr[hh, 1] = first[hh]
        lax.fori_loop(0, (nb - j + 1) // 2, q_pair, 0)
        dk_ref[...] = jnp.where(hms[0], dk_scr[0], dk_scr[1]).astype(BF16)
        dv_ref[...] = dv_scr[...].astype(BF16)
        if fox:
            dka_ref[...] = jnp.where(hms[0], dk_scr[1], dk_scr[0])

        @pl.when(j == nb - 1)
        def _():
            for ii in range(nb):
                t0, t1 = dqt_scr[0, ii].T, dqt_scr[1, ii].T
                dq_ref[ii * TB:(ii + 1) * TB, :] = jnp.where(hms[0], t0, t1) * 0.125
                if fox:
                    dqa_ref[ii * TB:(ii + 1) * TB, :] = jnp.where(hms[0], t1, t0)

    seq = lambda c: pl.BlockSpec((S, 128), lambda b, p, j: (b, c + p))
    kvb = lambda c: pl.BlockSpec((TB, 128), lambda b, p, j: (b * nb + j, c + p))
    stat = pl.BlockSpec((nb, 2, 1, TB), lambda b, p, j: (b, p, 0, 0))
    in_specs = [seq(col0), kvb(col0 + NPAIR), kvb(col0 + 2 * NPAIR), pl.BlockSpec((nb + 1, TB, TB), lambda b, p, j: (0, 0, 0)),
                seq(0), stat, stat]
    args = [p_qkv, p_qkv, p_qkv, tab, do, lse, dd]
    out_shape = [_sds((T, WA), F32), _sds((T, WA), BF16), _sds((T, WA), BF16)]
    out_specs = [seq(0), kvb(0), kvb(0)]
    if fox:
        in_specs += [pl.BlockSpec((1, S, 128), lambda b, p, j: (p, b, 0)), pl.BlockSpec((1, TB, 128), lambda b, p, j: (p, b * nb + j, 0))]
        args += [aq, ak]
        out_shape += [_sds((T, WA), F32), _sds((T, WA), F32)]
        out_specs += [seq(0), kvb(0)]
    return pl.pallas_call(
        body, name=name, grid=(B, NPAIR, nb), out_shape=out_shape, in_specs=in_specs, out_specs=out_specs,
        scratch_shapes=[pltpu.VMEM((2, nb, 128, TB), F32), pltpu.VMEM((2, TB, 128), F32), pltpu.VMEM((TB, 128), F32),
                        pltpu.VMEM((2, 2, TB, TB), F32)],
        compiler_params=_cp("parallel", "parallel", "arbitrary"))(*args)


def _gate_bwd(dqa, dka, p_rest, bf, S, TB):
    T = p_rest.shape[0]
    B, nb = T // S, S // TB

    def body(dqa_ref, dka_ref, z_ref, b_ref, dz_ref, db_ref):
        @pl.when(pl.program_id(0) == 0)
        def _():
            db_ref[...] = jnp.zeros_like(db_ref)

        later = (lax.broadcasted_iota(jnp.int32, (TB, TB), 1) >= lax.broadcasted_iota(jnp.int32, (TB, TB), 0)).astype(F32)
        lane = lax.broadcasted_iota(jnp.int32, (1, 128), 1)
        src, head = lax.broadcasted_iota(jnp.int32, (WA, 128), 0), lax.broadcasted_iota(jnp.int32, (WA, 128), 1)
        spare0 = 128 * (head // 2) + HD * (1 - head % 2)
        pick_q = (src == spare0 + AUG_K1).astype(F32)
        pick_k = (src == spare0 + AUG_Q1).astype(F32)
        carry = jnp.zeros((1, 128), F32)
        dbs = jnp.zeros((1, 128), F32)
        for j in reversed(range(nb)):
            rows = slice(j * TB, (j + 1) * TB)
            dc = _dot(dqa_ref[rows, :], pick_q, HIGHEST) - _dot(dka_ref[rows, :], pick_k, HIGHEST)
            part = _dot(later, dc, HIGHEST)
            tot = part + carry
            carry = carry + part[0:1, :]
            z = z_ref[j * TB:(j + 1) * TB, :] + b_ref[...]
            dz = jnp.where(lane < 6, tot * _sigmoid(-z), 0.0)
            dz_ref[j * TB:(j + 1) * TB, :] = dz.astype(BF16)
            dbs = dbs + jnp.sum(dz, axis=0, keepdims=True)
        db_ref[...] += dbs

    return pl.pallas_call(
        body, name="gate_bwd", grid=(B,), out_shape=[_sds((T, 128), BF16), _sds((1, 128), F32)],
        in_specs=[pl.BlockSpec((S, WA), lambda b: (b, 0)), pl.BlockSpec((S, WA), lambda b: (b, 0)),
                  pl.BlockSpec((S, 128), lambda b: (b, 4)), pl.BlockSpec((1, 128), lambda b: (0, 0))],
        out_specs=[pl.BlockSpec((S, 128), lambda b: (b, 0)), pl.BlockSpec((1, 128), lambda b: (0, 0))],
        compiler_params=_cp("arbitrary"))(dqa, dka, p_rest, bf)


def _adamw(parts, w, m, v, name):
    npart, R, C = parts.shape
    tr = R
    for cand in (256, 128, 64, 32, 16, 8):
        if R % cand == 0 and cand * C * 4 <= (1 << 20):
            tr = cand
            break
    if R * C * 4 <= (1 << 20):
        tr = R

    def body(p_ref, w_ref, m_ref, v_ref, g_ref, d_ref, mo_ref, vo_ref):
        g = p_ref[0].astype(F32)
        for k in range(1, npart):
            g = g + p_ref[k].astype(F32)
        mn = ADAM_B1 * m_ref[...] + (1.0 - ADAM_B1) * g
        vn = ADAM_B2 * v_ref[...] + (1.0 - ADAM_B2) * (g * g)
        m_hat = mn / (1.0 - ADAM_B1 ** ADAM_STEP)
        v_hat = vn / (1.0 - ADAM_B2 ** ADAM_STEP)
        g_ref[...] = g
        d_ref[...] = -ADAM_LR * (m_hat / (jnp.sqrt(v_hat) + ADAM_EPS) + ADAM_WD * w_ref[...])
        mo_ref[...] = mn
        vo_ref[...] = vn

    blk = pl.BlockSpec((tr, C), lambda i: (i, 0))
    return pl.pallas_call(
        body, name=name, grid=(R // tr,), out_shape=[_sds((R, C), F32)] * 4,
        in_specs=[pl.BlockSpec((npart, tr, C), lambda i: (0, i, 0)), blk, blk, blk], out_specs=[blk] * 4,
        compiler_params=_cp("parallel"))(parts, w, m, v)


def _pad_w_in(w):
    z = jnp.zeros(w.shape[:-1] + (NPAD - N_IN,), w.dtype)
    return jnp.concatenate([w[..., :O_FA], w[..., O_QB:], w[..., O_FA:O_QB], z], axis=-1)


def _unpad_w_in(w):
    n_mid = N_IN - O_QB
    return jnp.concatenate([w[..., :O_FA], w[..., O_FA + n_mid:O_FA + n_mid + 6], w[..., O_FA:O_FA + n_mid]], axis=-1)


def _local_step(x, tgt, W, S, late_weights=None, on_layer_grads=None):
    T = x.shape[0]
    TB = min(256, S)
    tab_fox, tab_dil = _bias_table(S, TB, True), _bias_table(S, TB, False)
    saved = []
    h = _rms_fwd(x, W["ln1_g"][0])
    for l in range(DEPTH):
        p_qkv, p_rest = _mm_in(h, W["w_in"][l])
        aq, ak = _gate_fwd(p_rest, W["b_forget"][l], S, TB)
        o_a, lse_a = _attn_fwd(p_qkv, 0, tab_fox, S, aq, ak, name="fox_fwd")
        o_b, lse_b = _attn_fwd(p_qkv, 3 * NPAIR, tab_dil, S, name="dil_fwd")
        y = _mix_fwd(o_a, o_b, p_rest, W["g_out_fox"][l], W["g_out_dil"][l], W["conv_w"][l], W["conv_b"][l],
                     W["cnorm_g"][l], W["cnorm_b"][l], S)
        if l == 0 and late_weights is not None:
            W = late_weights(W, y)
        x1, h2 = _mm_res(y[None], W["w_o"][l][None], x, W["ln2_g"][l], name="mm_o")
        u, hid = _up_fwd(h2, W["w_up"][l], W["ffn_conv_w"][l], W["ffn_conv_b"][l], S)
        g_next = W["ln1_g"][l + 1] if l + 1 < DEPTH else W["g_final"]
        x2, h_next = _mm_res(hid, W["w_down"][l], x1, g_next, name="mm_down")
        saved.append(dict(x=x, h=h, p_qkv=p_qkv, p_rest=p_rest, aq=aq, ak=ak, o_a=o_a, lse_a=lse_a, o_b=o_b,
                          lse_b=lse_b, y=y, x1=x1, h2=h2, u=u, hid=hid))
        x, h = x2, h_next
    dx, loss, dg_final = _final_loss(x, tgt, W["g_final"])
    G = {k: [None] * DEPTH for k in ("ln1_g", "w_in", "b_forget", "g_out_fox", "g_out_dil", "conv_w", "conv_b", "cnorm_g",
                                     "cnorm_b", "w_o", "ln2_g", "w_up", "ffn_conv_w", "ffn_conv_b", "w_down")}
    G["g_final"] = dg_final
    for l in reversed(range(DEPTH)):
        sv = saved[l]
        du, dcw = _down_bwd(dx, W["w_down"][l], sv["u"], W["ffn_conv_w"][l], W["ffn_conv_b"][l], S)
        G["w_down"][l] = _wgrad(sv["hid"], dx[None], D, "wg_down")[:, 0]
        G["ffn_conv_w"][l] = dcw[:, :, 0:3, :]
        G["ffn_conv_b"][l] = dcw[:, :, 3, :]
        G["w_up"][l] = _wgrad(sv["h2"][None], du.reshape(2 * NJ, T, FB), FB, "wg_up")[0]
        dx1, G["ln2_g"][l] = _dx_rms(du, W["w_up"][l], sv["x1"], W["ln2_g"][l], dx, True, "dx_up")
        G["w_o"][l] = _wgrad(sv["y"][None], dx1[None], D, "wg_o")[0, 0]
        dy = _mm_nt(dx1, W["w_o"][l], "mm_dy")
        do_a, dd_a, do_b, dd_b, dgvgg, dgn, dcvw, dcn = _mix_bwd(
            dy, sv["o_a"], sv["o_b"], sv["p_rest"], W["g_out_fox"][l], W["g_out_dil"][l], W["conv_w"][l], W["conv_b"][l],
            W["cnorm_g"][l], W["cnorm_b"][l], S)
        G["g_out_fox"][l], G["g_out_dil"][l] = dgn[0], dgn[1]
        G["conv_w"][l], G["conv_b"][l] = dcvw[:CONV_K], dcvw[CONV_K]
        G["cnorm_g"][l], G["cnorm_b"][l] = dcn[0], dcn[1]
        dq_a, dk_a, dv_a, dqa, dka = _attn_bwd(sv["p_qkv"], 0, tab_fox, do_a, sv["lse_a"], dd_a, S, sv["aq"], sv["ak"], name="fox_bwd")
        dq_b, dk_b, dv_b = _attn_bwd(sv["p_qkv"], 3 * NPAIR, tab_dil, do_b, sv["lse_b"], dd_b, S, name="dil_bwd")
        dfa, db = _gate_bwd(dqa, dka, sv["p_rest"], W["b_forget"][l], S, TB)
        G["b_forget"][l] = db[0, :6]
        dp = jnp.concatenate([dq_a.astype(BF16), dk_a, dv_a, dq_b.astype(BF16), dk_b, dv_b, dgvgg, dfa,
                              jnp.zeros((T, 128), BF16)], axis=1)
        G["w_in"][l] = _wgrad(sv["h"][None], dp[None], D, "wg_in")[0, 0]
        dx, G["ln1_g"][l] = _dx_rms(dp, W["w_in"][l], sv["x"], W["ln1_g"][l], dx1, False, "dx_in")
        token = on_layer_grads(l, G) if on_layer_grads is not None else None
        if token is not None and l > 0:
            W = dict(W, ffn_conv_b=[b + token[0, 0] for b in W["ffn_conv_b"]])
    return loss, dx, G


_SHARDED = ("w_in", "w_o", "w_up", "w_down", "conv_w", "ffn_conv_w")
_SMALL = ("ln1_g", "b_forget", "g_out_fox", "g_out_dil", "conv_b", "cnorm_g", "cnorm_b", "ln2_g", "ffn_conv_b", "g_final")
_ORDER = ("ln1_g", "w_in", "b_forget", "g_out_fox", "g_out_dil", "conv_w", "conv_b", "cnorm_g", "cnorm_b", "w_o", "ln2_g",
          "w_up", "ffn_conv_w", "ffn_conv_b", "w_down", "g_final")


def _kernel_ready(k, zone):
    if k == "w_in":
        return _pad_w_in(zone.reshape(D, N_IN))
    if k == "w_o":
        return zone.reshape(D, D)
    if k == "w_up":
        return zone.reshape(2, NJ, D, FB)
    if k == "w_down":
        return zone.reshape(NJ, FB, D)
    if k == "conv_w":
        cw = jnp.transpose(zone.reshape(NDEV, CONV_K, CC // NDEV), (1, 0, 2)).reshape(CONV_K, CC)
        return jnp.concatenate([cw, jnp.zeros((1, CC), F32)], axis=0)
    return zone.reshape(2, NJ, 3, FB)


def _full_weights(P, gathered):
    W = {}
    row = lambda a: [a[l][None, :] for l in range(DEPTH)]
    for k in _SHARDED:
        W[k] = [None if gathered[k][l] is None else _kernel_ready(k, gathered[k][l]) for l in range(DEPTH)]
    W["ffn_conv_b"] = [P["ffn_conv_b"][l].reshape(2, NJ, 1, FB) for l in range(DEPTH)]
    W["b_forget"] = [jnp.concatenate([P["b_forget"][l], jnp.zeros((128 - 6,), F32)])[None, :] for l in range(DEPTH)]
    for k in ("ln1_g", "ln2_g", "g_out_fox", "g_out_dil", "conv_b", "cnorm_g", "cnorm_b"):
        W[k] = row(P[k])
    W["g_final"] = P["g_final"][None, :]
    return W


def kernel(x, ln1_g, w_in, b_forget, g_out_fox, g_out_dil, conv_w, conv_b, cnorm_g, cnorm_b, w_o, ln2_g, w_up, ffn_conv_w, ffn_conv_b, w_down, g_final, loss_target, m_ln1_g, m_w_in, m_b_forget, m_g_out_fox, m_g_out_dil, m_conv_w, m_conv_b, m_cnorm_g, m_cnorm_b, m_w_o, m_ln2_g, m_w_up, m_ffn_conv_w, m_ffn_conv_b, m_w_down, m_g_final, v_ln1_g, v_w_in, v_b_forget, v_g_out_fox, v_g_out_dil, v_conv_w, v_conv_b, v_cnorm_g, v_cnorm_b, v_w_o, v_ln2_g, v_w_up, v_ffn_conv_w, v_ffn_conv_b, v_w_down, v_g_final):
    P = dict(ln1_g=ln1_g, w_in=w_in, b_forget=b_forget, g_out_fox=g_out_fox, g_out_dil=g_out_dil, conv_w=conv_w, conv_b=conv_b,
             cnorm_g=cnorm_g, cnorm_b=cnorm_b, w_o=w_o, ln2_g=ln2_g, w_up=w_up, ffn_conv_w=ffn_conv_w, ffn_conv_b=ffn_conv_b,
             w_down=w_down, g_final=g_final)
    M = dict(ln1_g=m_ln1_g, w_in=m_w_in, b_forget=m_b_forget, g_out_fox=m_g_out_fox, g_out_dil=m_g_out_dil, conv_w=m_conv_w,
             conv_b=m_conv_b, cnorm_g=m_cnorm_g, cnorm_b=m_cnorm_b, w_o=m_w_o, ln2_g=m_ln2_g, w_up=m_w_up, ffn_conv_w=m_ffn_conv_w,
             ffn_conv_b=m_ffn_conv_b, w_down=m_w_down, g_final=m_g_final)
    V = dict(ln1_g=v_ln1_g, w_in=v_w_in, b_forget=v_b_forget, g_out_fox=v_g_out_fox, g_out_dil=v_g_out_dil, conv_w=v_conv_w,
             conv_b=v_conv_b, cnorm_g=v_cnorm_g, cnorm_b=v_cnorm_b, w_o=v_w_o, ln2_g=v_ln2_g, w_up=v_w_up, ffn_conv_w=v_ffn_conv_w,
             ffn_conv_b=v_ffn_conv_b, w_down=v_w_down, g_final=v_g_final)
    Bl, S, _ = x.shape
    T = Bl * S

    shard = lambda k, l: P[k][l].astype(BF16) if k in ("w_in", "w_o", "w_up", "w_down") else P[k][l]
    early = [("w_in", 0)] + [(k, l) for k in ("conv_w", "ffn_conv_w") for l in range(DEPTH)]
    late = [(k, 0) for k in ("w_o", "w_up", "w_down")] + [(k, 1) for k in ("w_in", "w_o", "w_up", "w_down")]
    gathered = {k: [None] * DEPTH for k in _SHARDED}
    plan_early = _Plan([[shard(k, l)] for k, l in early], True)
    for (k, l), zone in zip(early, _exchange(plan_early, "gather_early")):
        gathered[k][l] = zone
    plan_late = _Plan([[shard(k, l)] for k, l in late], True)
    late_handle, late_token = _exchange_start(plan_late, "gather_late_start", _place_own(plan_late, "gather_late_own"))
    W = _full_weights(P, gathered)
    W["ln1_g"][0] = W["ln1_g"][0] + late_token[0, 0]

    def late_weights(W, after):
        zones = _exchange_wait(plan_late, "gather_late_wait", late_handle, after)
        W = dict(W)
        for (k, l), zone in zip(late, zones):
            W[k] = list(W[k])
            W[k][l] = _kernel_ready(k, zone)
        return W

    def owner_blocks(G, l):
        blk = dict(
            w_in=_unpad_w_in(G["w_in"][l]).reshape(NDEV, D // NDEV, N_IN),
            w_o=G["w_o"][l].reshape(NDEV, D // NDEV, D),
            w_up=G["w_up"][l],
            w_down=G["w_down"][l].reshape(NDEV, DFF // NDEV, D),
            conv_w=jnp.transpose(G["conv_w"][l].reshape(CONV_K, NDEV, CC // NDEV), (1, 0, 2)),
            ffn_conv_w=G["ffn_conv_w"][l].reshape(NDEV, 3, FB))
        return _Plan([[blk[k].astype(GRAD_DTYPE)] for k in _SHARDED], False, slots=[[l]] * len(_SHARDED),
                     n_slots=[DEPTH] * len(_SHARDED))

    flying = {}

    def on_layer_grads(l, G):
        if l == 0:
            return None
        flying["plan"] = owner_blocks(G, l)
        flying["handle"], token = _exchange_start(flying["plan"], "scatter_l1_start", _place_own(flying["plan"], "scatter_l1_own"))
        return token

    loss, dx, G = _local_step(x.reshape(T, D), loss_target.reshape(T, D), W, S, late_weights, on_layer_grads)

    zones = _exchange_wait(flying["plan"], "scatter_l1_wait", flying["handle"], dx)
    parts = dict(zip(_SHARDED, _exchange(owner_blocks(G, 0), "scatter_l0", zones)))

    small = [jnp.stack(G[k]).reshape(-1) for k in _SMALL[:-1]] + [G["g_final"].reshape(-1), loss[0, :1]]
    sizes = [int(s.shape[0]) for s in small]
    flat = jnp.concatenate(small)
    n_small = -(-flat.shape[0] // 1024) * 1024
    flat = jnp.concatenate([flat, jnp.zeros((n_small - flat.shape[0],), F32)]).reshape(n_small // 128, 128)
    small_parts = _exchange(_Plan([[flat]], True), "gather_small")[0]

    out_g, out_d, out_m, out_v = {}, {}, {}, {}
    sg = _sum_parts(small_parts.reshape(NDEV, n_small // 128, 128), "sum_small").reshape(-1)
    off = 0
    summed = {}
    for k, n in zip(_SMALL + ("loss",), sizes):
        summed[k] = sg[off:off + n]
        off += n
    for k in _ORDER:
        shp = P[k].shape
        if k in _SHARDED:
            pr = parts[k].reshape((NDEV, -1, shp[-1]))
        else:
            pr = summed[k].reshape((1, -1, shp[-1]))
        R = pr.shape[1]
        g, d, mn, vn = _adamw(pr, P[k].reshape(R, -1), M[k].reshape(R, -1), V[k].reshape(R, -1), "adamw_" + k)
        out_g[k], out_d[k], out_m[k], out_v[k] = (t.reshape(shp) for t in (g, d, mn, vn))

    loss_out = summed["loss"].reshape(())
    grad_x = dx.reshape(Bl, S, D)
    return (loss_out, grad_x, *[out_g[k] for k in _ORDER], *[out_d[k] for k in _ORDER], *[out_m[k] for k in _ORDER],
            *[out_v[k] for k in _ORDER])
```

```python
import functools

import numpy as np
import jax
import jax.numpy as jnp
from jax import lax
from jax.experimental import pallas as pl
from jax.experimental.pallas import tpu as pltpu

F32 = jnp.float32
BF16 = jnp.bfloat16
GRAD_DTYPE = BF16
HIGHEST = lax.Precision.HIGHEST

D = 1024
DEPTH = 2
HD = 64
WA = 384
NPAIR = 3
CC = 256
CONV_K = 31
DFF = 2816
FB = 704
NJ = 4
NDEV = 8
EPS = 1e-6
NQKV = 2304
NREST = 768
NPAD = NQKV + NREST
O_FA, O_QB, N_IN = 1152, 1158, 2822
DILATION_PAIRS = ((128, 1), (512, 4), (2048, 16))
NEG = -1e30

ADAM_LR, ADAM_B1, ADAM_B2, ADAM_EPS, ADAM_WD, ADAM_STEP = 0.001, 0.9, 0.999, 1e-08, 0.01, 10

VMEM_LIMIT = 48 * 1024 * 1024


def _cp(*sem):
    return pltpu.CompilerParams(dimension_semantics=sem, vmem_limit_bytes=VMEM_LIMIT)


def _sds(shape, dtype):
    return jax.ShapeDtypeStruct(shape, dtype)


def _dot(a, b, prec=None):
    return jnp.dot(a, b, preferred_element_type=F32, precision=prec)


def _dot_nt(a, b, prec=None):
    return lax.dot_general(a, b, (((1,), (1,)), ((), ())), preferred_element_type=F32, precision=prec)


def _dot_tn(a, b):
    return lax.dot_general(a, b, (((0,), (0,)), ((), ())), preferred_element_type=F32)


def _sigmoid(z):
    return 0.5 * jnp.tanh(0.5 * z) + 0.5


def _rms_bwd(dh, x, g):
    r = lax.rsqrt(jnp.mean(x * x, axis=-1, keepdims=True) + EPS)
    xh = x * r
    dg = jnp.sum(dh * xh, axis=0, keepdims=True)
    dxh = dh * g
    dx = r * (dxh - xh * jnp.mean(dxh * xh, axis=-1, keepdims=True))
    return dx, dg


class _Plan:
    def __init__(self, groups, gather, slots=None, n_slots=None):
        self.groups, self.gather = groups, gather
        self.slots = slots or [list(range(len(g))) for g in groups]
        self.n_slots = n_slots or [len(g) for g in groups]
        self.flat = [a for g in groups for a in g]
        self.where = [(gi, s) for gi, g in enumerate(groups) for s in self.slots[gi]]
        self.zone_shapes = [_sds((NDEV, ns) + (g[0].shape if gather else g[0].shape[1:]), g[0].dtype)
                            for g, ns in zip(groups, self.n_slots)]

    def new_zones(self):
        return [lax.empty(z.shape, z.dtype) for z in self.zone_shapes]

    def me(self):
        x, y, c = lax.axis_index("x"), lax.axis_index("y"), lax.axis_index("c")
        return 4 * x + 2 * y + c

    def zones_with_own(self):
        me = self.me()
        zones = self.new_zones()
        for a, (gi, s) in enumerate(self.where):
            src = self.flat[a] if self.gather else lax.dynamic_index_in_dim(self.flat[a], me, 0, keepdims=False)
            zones[gi] = lax.dynamic_update_slice(zones[gi], src[None, None], (me, s) + (0,) * src.ndim)
        return zones

    def own_copies(self, ins, zones, sems):
        me = self.me()
        return [pltpu.make_async_copy(ins[a] if self.gather else ins[a].at[me], zones[gi].at[me, s], sems.at[a])
                for a, (gi, s) in enumerate(self.where)]

    def remote_copies(self, ins, zones, send_sems, recv_sems):
        x, y, c = lax.axis_index("x"), lax.axis_index("y"), lax.axis_index("c")
        me = 4 * x + 2 * y + c
        out = []
        for a, (gi, s) in enumerate(self.where):
            for k in range(1, NDEV):
                px, py, pc = x ^ ((k >> 2) & 1), y ^ ((k >> 1) & 1), c ^ (k & 1)
                out.append(pltpu.make_async_remote_copy(
                    src_ref=ins[a] if self.gather else ins[a].at[4 * px + 2 * py + pc], dst_ref=zones[gi].at[me, s],
                    send_sem=send_sems.at[a * (NDEV - 1) + k - 1], recv_sem=recv_sems.at[a * (NDEV - 1) + k - 1],
                    device_id=(px, py, pc), device_id_type=pl.DeviceIdType.MESH))
        return out


_ANY = pl.BlockSpec(memory_space=pl.ANY)
_HBM = pl.BlockSpec(memory_space=pltpu.HBM)
_SEM = pl.BlockSpec(memory_space=pltpu.SEMAPHORE)


def _exchange(plan, name, zones=None):
    n, nz = len(plan.flat), len(plan.groups)
    zones = zones or plan.new_zones()

    def body(*refs):
        ins, zin = refs[:n], refs[n:n + nz]
        send_sems, recv_sems, local_sems = refs[n + 2 * nz:]
        copies = plan.own_copies(ins, zin, local_sems) + plan.remote_copies(ins, zin, send_sems, recv_sems)
        for cp in copies:
            cp.start()
        for cp in copies:
            cp.wait()

    return pl.pallas_call(
        body, name=name, out_shape=plan.zone_shapes, in_specs=[_ANY] * (n + nz), out_specs=[_ANY] * nz,
        input_output_aliases={n + g: g for g in range(nz)},
        scratch_shapes=[pltpu.SemaphoreType.DMA((n * (NDEV - 1),)), pltpu.SemaphoreType.DMA((n * (NDEV - 1),)),
                        pltpu.SemaphoreType.DMA((n,))],
        compiler_params=pltpu.CompilerParams(has_side_effects=True),
    )(*plan.flat, *zones)


def _exchange_start(plan, name, zones):
    n, nz = len(plan.flat), len(plan.groups)
    hbm = lambda a: pltpu.HBM(a.shape, a.dtype)

    def body(*refs):
        ins, zin = refs[:n], refs[n:n + nz]
        send_sems, recv_sems = refs[n + nz], refs[n + nz + 1]
        token = refs[-1]
        for cp in plan.remote_copies(ins, zin, send_sems, recv_sems):
            cp.start()
        token[...] = jnp.zeros_like(token)

    outs = pl.pallas_call(
        body, name=name,
        out_shape=(pltpu.SemaphoreType.DMA((n * (NDEV - 1),)), pltpu.SemaphoreType.DMA((n * (NDEV - 1),)),
                   *[hbm(a) for a in plan.flat], *[hbm(z) for z in plan.zone_shapes], _sds((8, 128), F32)),
        in_specs=[_HBM] * (n + nz),
        out_specs=(_SEM, _SEM, *[_HBM] * (n + nz), pl.BlockSpec(memory_space=pltpu.VMEM)),
        input_output_aliases={i: 2 + i for i in range(n + nz)},
        compiler_params=pltpu.CompilerParams(has_side_effects=pltpu.SideEffectType.DATAFLOW_SIDE_EFFECTING),
    )(*[pltpu.with_memory_space_constraint(a, pltpu.HBM) for a in plan.flat],
      *[pltpu.with_memory_space_constraint(z, pltpu.HBM) for z in zones])
    return outs[:-1], outs[-1]


def _exchange_wait(plan, name, handle, after):
    n, nz = len(plan.flat), len(plan.groups)
    send_sems, recv_sems = handle[0], handle[1]
    thru = handle[2:]
    hbm = lambda a: pltpu.HBM(a.shape, a.dtype)

    def body(*refs):
        ins, zin = refs[:n], refs[n:n + nz]
        ssem, rsem = refs[n + nz], refs[n + nz + 1]
        for cp in plan.remote_copies(ins, zin, ssem, rsem):
            cp.wait_send()
            cp.wait_recv()

    outs = pl.pallas_call(
        body, name=name, out_shape=tuple(hbm(a) for a in thru),
        in_specs=[_HBM] * (n + nz) + [_SEM, _SEM, _ANY], out_specs=tuple([_HBM] * (n + nz)),
        input_output_aliases={i: i for i in range(n + nz)},
        compiler_params=pltpu.CompilerParams(has_side_effects=pltpu.SideEffectType.DATAFLOW_SIDE_EFFECTING),
    )(*thru, send_sems, recv_sems, after)
    return list(outs[n:])


def _sum_parts(parts, name):
    npart, R, C = parts.shape

    def body(p_ref, o_ref):
        s = p_ref[0]
        for k in range(1, npart):
            s = s + p_ref[k]
        o_ref[...] = s

    return pl.pallas_call(body, name=name, out_shape=_sds((R, C), F32))(parts)


def _rms_fwd(x, g):
    T = x.shape[0]
    tm = min(512, T)

    def body(x_ref, g_ref, o_ref):
        xv = x_ref[...]
        r = lax.rsqrt(jnp.mean(xv * xv, axis=-1, keepdims=True) + EPS)
        o_ref[...] = (xv * r * g_ref[...]).astype(BF16)

    return pl.pallas_call(
        body, name="rms_fwd", grid=(T // tm,), out_shape=_sds((T, D), BF16),
        in_specs=[pl.BlockSpec((tm, D), lambda i: (i, 0)), pl.BlockSpec((1, D), lambda i: (0, 0))],
        out_specs=pl.BlockSpec((tm, D), lambda i: (i, 0)), compiler_params=_cp("parallel"))(x, g)


def _mm_in(h, w):
    T = h.shape[0]
    tm = min(512, T)

    def body(h_ref, w_ref, q_ref, r_ref):
        hv = h_ref[...]
        q_ref[...] = _dot(hv, w_ref[:, :NQKV]).astype(BF16)
        r_ref[...] = _dot(hv, w_ref[:, NQKV:])

    return pl.pallas_call(
        body, name="mm_in", grid=(T // tm,), out_shape=[_sds((T, NQKV), BF16), _sds((T, NREST), F32)],
        in_specs=[pl.BlockSpec((tm, D), lambda i: (i, 0)), pl.BlockSpec((D, NPAD), lambda i: (0, 0))],
        out_specs=[pl.BlockSpec((tm, NQKV), lambda i: (i, 0)), pl.BlockSpec((tm, NREST), lambda i: (i, 0))],
        compiler_params=_cp("parallel"))(h, w)


AUG_Q1, AUG_K1 = 3, 0


def _aug_lane0(h):
    return HD * (1 - h % 2)


def _aug_tables():
    selq = np.zeros((NPAIR, 3 * 128, 128), np.float32)
    selk = np.zeros((NPAIR, 3 * 128, 128), np.float32)
    oneq = np.zeros((NPAIR, 1, 128), np.float32)
    onek = np.zeros((NPAIR, 1, 128), np.float32)
    for h in range(2 * NPAIR):
        p, l0 = h // 2, _aug_lane0(h)
        for piece in range(3):
            selq[p, 128 * piece + h, l0 + piece] = 1.0
            selk[p, 128 * piece + h, l0 + 3 + piece] = -1.0
            oneq[p, 0, l0 + AUG_Q1 + piece] = 1.0
            onek[p, 0, l0 + AUG_K1 + piece] = 1.0
    return [jnp.asarray(a, BF16) for a in (selq, selk, oneq, onek)]


def _gate_fwd(p_rest, bf, S, TB):
    T = p_rest.shape[0]
    B, nb = T // S, S // TB
    selq, selk, oneq, onek = _aug_tables()

    def body(z_ref, b_ref, sq_ref, sk_ref, oq_ref, ok_ref, aq_ref, ak_ref):
        tri = (lax.broadcasted_iota(jnp.int32, (TB, TB), 0) >= lax.broadcasted_iota(jnp.int32, (TB, TB), 1)).astype(F32)
        carry = jnp.zeros((1, 128), F32)
        for j in range(nb):
            rows = slice(j * TB, (j + 1) * TB)
            z = z_ref[rows, :] + b_ref[...]
            lf = jnp.minimum(z, 0.0) - jnp.log(1.0 + jnp.exp(-jnp.abs(z)))
            cb = _dot(tri, lf, HIGHEST) + carry
            carry = cb[TB - 1:TB, :]
            hi = cb.astype(BF16)
            mid = (cb - hi.astype(F32)).astype(BF16)
            lo = (cb - hi.astype(F32) - mid.astype(F32)).astype(BF16)
            pieces = jnp.concatenate([hi, mid, lo], axis=1)
            for p in range(NPAIR):
                aq_ref[p, rows, :] = (_dot(pieces, sq_ref[p]) + oq_ref[p].astype(F32)).astype(BF16)
                ak_ref[p, rows, :] = (_dot(pieces, sk_ref[p]) + ok_ref[p].astype(F32)).astype(BF16)

    full = lambda a: pl.BlockSpec(a.shape, lambda b: (0,) * a.ndim)
    return pl.pallas_call(
        body, name="gate_fwd", grid=(B,), out_shape=[_sds((NPAIR, T, 128), BF16), _sds((NPAIR, T, 128), BF16)],
        in_specs=[pl.BlockSpec((S, 128), lambda b: (b, 4)), pl.BlockSpec((1, 128), lambda b: (0, 0)),
                  full(selq), full(selk), full(oneq), full(onek)],
        out_specs=[pl.BlockSpec((NPAIR, S, 128), lambda b: (0, b, 0)), pl.BlockSpec((NPAIR, S, 128), lambda b: (0, b, 0))],
        compiler_params=_cp("parallel"))(p_rest, bf, selq, selk, oneq, onek)


def _bias_table(S, TB, fox):
    nb = S // TB
    dj = np.arange(nb)[:, None, None]
    delta = dj * TB + np.arange(TB)[None, None, :] - np.arange(TB)[None, :, None]
    if fox:
        mult = (delta >= 0).astype(np.float64)
    else:
        mult = np.zeros(delta.shape)
        for w, d in DILATION_PAIRS:
            mult += (delta >= 0) & (delta % d == 0) & (delta <= w)
    with np.errstate(divide="ignore"):
        tab = np.where(mult > 0, np.log(np.maximum(mult, 1e-30)), NEG)
    tab = np.concatenate([tab, np.full((1, TB, TB), NEG)], axis=0)
    return jnp.asarray(tab, F32)


def _head_masks():
    lane = lax.broadcasted_iota(jnp.int32, (1, 128), 1)
    return [lane < HD, lane >= HD]


def _transpose_bf16(a):
    return a.astype(F32).T.astype(BF16)


def _col_reduce(x, op):
    while x.shape[0] > 8:
        half = x.shape[0] // 2
        x = op(x[:half], x[half:])
    return jnp.max(x, axis=0, keepdims=True) if op is jnp.maximum else jnp.sum(x, axis=0, keepdims=True)


def _attn_fwd(p_qkv, col0, tab, S, aq=None, ak=None, name="attn"):
    T = p_qkv.shape[0]
    nb, TB = tab.shape[0] - 1, tab.shape[1]
    B = T // S
    fox = aq is not None

    def body(*refs):
        if fox:
            q_ref, k_ref, v_ref, tab_ref, aq_ref, ak_ref, o_ref, l_ref, vt_scr, acc_scr, st_scr = refs
        else:
            q_ref, k_ref, v_ref, tab_ref, o_ref, l_ref, vt_scr, acc_scr, st_scr = refs
        i = pl.program_id(2)
        hms = _head_masks()

        @pl.when(i == 0)
        def _():
            for jj in range(nb):
                vt_scr[jj] = _transpose_bf16(v_ref[jj * TB:(jj + 1) * TB, :])

        q2 = q_ref[...] * 0.125
        spare_q = aq_ref[0] if fox else jnp.zeros_like(q2)
        qa = [jnp.where(hm, q2, spare_q) for hm in hms]
        acc_scr[...] = jnp.zeros_like(acc_scr)

        def scores(j):
            jc = jnp.minimum(j, nb - 1)
            rows = pl.ds(pl.multiple_of(jc * TB, TB), TB)
            k2 = k_ref[rows, :]
            bias = tab_ref[jnp.where(j <= i, i - j, nb)]
            return [_dot_nt(jnp.where(hms[hh], k2, ak_ref[0, rows, :]) if fox else k2, qa[hh]) + bias for hh in range(2)]

        def absorb(j, sts, stats):
            jc = jnp.minimum(j, nb - 1)
            alphas, pts = [], []
            for hh in range(2):
                m_old, l_old = stats[2 * hh], stats[2 * hh + 1]
                m_new = jnp.maximum(m_old, _col_reduce(sts[hh], jnp.maximum))
                pt = jnp.exp(sts[hh] - m_new)
                alphas.append(jnp.exp(m_old - m_new))
                stats[2 * hh] = m_new
                stats[2 * hh + 1] = alphas[hh] * l_old + _col_reduce(pt, jnp.add)
                pts.append(pt.astype(BF16))
            pvs = [_dot(vt_scr[jc, HD * hh:HD * (hh + 1), :], pts[hh]) for hh in range(2)]
            for hh in range(2):
                half = slice(HD * hh, HD * (hh + 1))
                acc_scr[half, :] = alphas[hh] * acc_scr[half, :] + pvs[hh]

        def kv_pair(jj, carry):
            stats = list(carry)
            j0 = 2 * jj
            st0 = [st_scr[hh] for hh in range(2)]
            st1 = scores(j0 + 1)
            absorb(j0, st0, stats)
            nxt = scores(j0 + 2)
            for hh in range(2):
                st_scr[hh] = nxt[hh]
            absorb(j0 + 1, st1, stats)
            return tuple(stats)

        first = scores(0)
        for hh in range(2):
            st_scr[hh] = first[hh]
        row = lambda v: jnp.full((1, TB), v, F32)
        m0, l0, m1, l1 = lax.fori_loop(0, i // 2 + 1, kv_pair, (row(NEG), row(0.0), row(NEG), row(0.0)))
        top = lax.broadcasted_iota(jnp.int32, (128, 1), 0) < HD
        o_ref[...] = (acc_scr[...] * jnp.where(top, 1.0 / l0, 1.0 / l1)).T
        l_ref[0, 0] = m0 + jnp.log(l0)
        l_ref[0, 1] = m1 + jnp.log(l1)

    in_specs = [pl.BlockSpec((TB, 128), lambda b, p, i: (b * nb + i, col0 + p)),
                pl.BlockSpec((S, 128), lambda b, p, i: (b, col0 + NPAIR + p)),
                pl.BlockSpec((S, 128), lambda b, p, i: (b, col0 + 2 * NPAIR + p)),
                pl.BlockSpec((nb + 1, TB, TB), lambda b, p, i: (0, 0, 0))]
    args = [p_qkv, p_qkv, p_qkv, tab]
    if fox:
        in_specs += [pl.BlockSpec((1, TB, 128), lambda b, p, i: (p, b * nb + i, 0)),
                     pl.BlockSpec((1, S, 128), lambda b, p, i: (p, b, 0))]
        args += [aq, ak]
    return pl.pallas_call(
        body, name=name, grid=(B, NPAIR, nb), out_shape=[_sds((T, WA), F32), _sds((T // TB, 2 * NPAIR, 1, TB), F32)],
        in_specs=in_specs,
        out_specs=[pl.BlockSpec((TB, 128), lambda b, p, i: (b * nb + i, p)),
                   pl.BlockSpec((1, 2, 1, TB), lambda b, p, i: (b * nb + i, p, 0, 0))],
        scratch_shapes=[pltpu.VMEM((nb, 128, TB), BF16), pltpu.VMEM((128, TB), F32), pltpu.VMEM((2, TB, TB), F32)],
        compiler_params=_cp("parallel", "parallel", "arbitrary"))(*args)


def _phase_copies(src, phases, length):
    for r in range(1, 8):
        phases[r, 0:length - 8, :] = src[pl.ds(r, length - 8), :]


def _tap(src, phases, offset, rows):
    r = offset % 8
    return src[pl.ds(offset, rows), :] if r == 0 else phases[r, pl.ds(offset - r, rows), :]


def _conv_taps(src, phases, w_ref, first_row, rows):
    acc = w_ref[0:1, :] * _tap(src, phases, first_row, rows)
    for k in range(1, CONV_K):
        acc = acc + w_ref[k:k + 1, :] * _tap(src, phases, first_row + k, rows)
    return acc


def _mix_fwd(o_a, o_b, p_rest, g_a, g_b, cw, cb, ng, nbias, S):
    T = o_a.shape[0]
    tm = min(256, S)
    nps = S // tm
    HALO = 32

    def body(oa_ref, ob_ref, pt_ref, ph_ref, ga_ref, gb_ref, cw_ref, cb_ref, ng_ref, nb_ref, y_ref, buf, phases):
        i = pl.program_id(0)
        first = (i % nps) == 0
        for o_ref, g_ref, c0 in ((oa_ref, ga_ref, 0), (ob_ref, gb_ref, WA)):
            o = o_ref[...]
            r = lax.rsqrt(jnp.mean(o * o, axis=-1, keepdims=True) + EPS)
            y_ref[:, c0:c0 + WA] = (o * r * g_ref[...]).astype(BF16)
        ph = jnp.where(first, 0.0, ph_ref[...])
        buf[0:HALO, :] = ph[:, :CC] * _sigmoid(ph[:, CC:])
        pt = pt_ref[...]
        buf[HALO:HALO + tm, :] = pt[:, :CC] * _sigmoid(pt[:, CC:])
        _phase_copies(buf, phases, tm + HALO)
        cv = _conv_taps(buf, phases, cw_ref, HALO - CONV_K + 1, tm) + cb_ref[...]
        xc = cv - jnp.mean(cv, axis=-1, keepdims=True)
        lo = xc * lax.rsqrt(jnp.mean(xc * xc, axis=-1, keepdims=True) + EPS) * ng_ref[...] + nb_ref[...]
        y_ref[:, 2 * WA:] = (lo * _sigmoid(lo)).astype(BF16)

    row = lambda w: pl.BlockSpec((1, w), lambda i: (0, 0))
    return pl.pallas_call(
        body, name="mix_fwd", grid=(T // tm,), out_shape=_sds((T, D), BF16),
        in_specs=[pl.BlockSpec((tm, WA), lambda i: (i, 0)), pl.BlockSpec((tm, WA), lambda i: (i, 0)),
                  pl.BlockSpec((tm, 2 * CC), lambda i: (i, 0)),
                  pl.BlockSpec((HALO, 2 * CC), lambda i: (jnp.maximum(i * (tm // HALO) - 1, 0), 0)),
                  row(WA), row(WA), pl.BlockSpec((32, CC), lambda i: (0, 0)), row(CC), row(CC), row(CC)],
        out_specs=pl.BlockSpec((tm, D), lambda i: (i, 0)),
        scratch_shapes=[pltpu.VMEM((tm + HALO, CC), F32), pltpu.VMEM((8, tm + HALO, CC), F32)],
        compiler_params=_cp("parallel"))(o_a, o_b, p_rest, p_rest, g_a, g_b, cw, cb, ng, nbias)


def _mm_res(a, w, resid, g, name):
    nk, T, kb = a.shape
    tm = min(512, T)

    def body(a_ref, w_ref, r_ref, g_ref, x_ref, h_ref, acc):
        k = pl.program_id(1)

        @pl.when(k == 0)
        def _():
            acc[...] = r_ref[...]

        acc[...] += _dot(a_ref[0], w_ref[0])

        @pl.when(k == nk - 1)
        def _():
            xv = acc[...]
            x_ref[...] = xv
            r = lax.rsqrt(jnp.mean(xv * xv, axis=-1, keepdims=True) + EPS)
            h_ref[...] = (xv * r * g_ref[...]).astype(BF16)

    return pl.pallas_call(
        body, name=name, grid=(T // tm, nk), out_shape=[_sds((T, D), F32), _sds((T, D), BF16)],
        in_specs=[pl.BlockSpec((1, tm, kb), lambda i, k: (k, i, 0)), pl.BlockSpec((1, kb, D), lambda i, k: (k, 0, 0)),
                  pl.BlockSpec((tm, D), lambda i, k: (i, 0)), pl.BlockSpec((1, D), lambda i, k: (0, 0))],
        out_specs=[pl.BlockSpec((tm, D), lambda i, k: (i, 0)), pl.BlockSpec((tm, D), lambda i, k: (i, 0))],
        scratch_shapes=[pltpu.VMEM((tm, D), F32)],
        compiler_params=_cp("parallel", "arbitrary"))(a, w, resid, g)


def _up_fwd(h2, w_up, fcw, fcb, S):
    T = h2.shape[0]
    tm = min(512, S)
    nps = S // tm

    def body(h_ref, hh_ref, w_ref, cw_ref, cb_ref, u_ref, hid_ref, buf):
        i = pl.program_id(1)
        first = (i % nps) == 0
        hv = h_ref[...]
        hh = jnp.where(first, jnp.zeros_like(hh_ref[...]), hh_ref[...])
        c = []
        for half in range(2):
            w = w_ref[half, 0]
            u = _dot(hv, w)
            u_ref[half, 0] = u
            buf[half, 0:8, :] = _dot(hh, w)
            buf[half, 8:8 + tm, :] = u
            cw = cw_ref[half, 0]
            c.append(cb_ref[half, 0] + cw[0:1] * buf[half, pl.ds(6, tm), :] + cw[1:2] * buf[half, pl.ds(7, tm), :] + cw[2:3] * u)
        hid_ref[0] = (c[0] * _sigmoid(c[0]) * c[1]).astype(BF16)

    return pl.pallas_call(
        body, name="up_fwd", grid=(NJ, T // tm), out_shape=[_sds((2, NJ, T, FB), F32), _sds((NJ, T, FB), BF16)],
        in_specs=[pl.BlockSpec((tm, D), lambda j, i: (i, 0)),
                  pl.BlockSpec((8, D), lambda j, i: (jnp.maximum(i * (tm // 8) - 1, 0), 0)),
                  pl.BlockSpec((2, 1, D, FB), lambda j, i: (0, j, 0, 0)),
                  pl.BlockSpec((2, 1, 3, FB), lambda j, i: (0, j, 0, 0)),
                  pl.BlockSpec((2, 1, 1, FB), lambda j, i: (0, j, 0, 0))],
        out_specs=[pl.BlockSpec((2, 1, tm, FB), lambda j, i: (0, j, i, 0)), pl.BlockSpec((1, tm, FB), lambda j, i: (j, i, 0))],
        scratch_shapes=[pltpu.VMEM((2, tm + 8, FB), F32)],
        compiler_params=_cp("parallel", "parallel"))(h2, h2, w_up, fcw, fcb)


def _final_loss(x, tgt, g):
    T = x.shape[0]
    tm = min(512, T)

    def body(x_ref, t_ref, g_ref, dx_ref, loss_ref, dg_ref):
        @pl.when(pl.program_id(0) == 0)
        def _():
            loss_ref[...] = jnp.zeros_like(loss_ref)
            dg_ref[...] = jnp.zeros_like(dg_ref)

        xv, gv = x_ref[...], g_ref[...]
        r = lax.rsqrt(jnp.mean(xv * xv, axis=-1, keepdims=True) + EPS)
        e = xv * r * gv - t_ref[...]
        loss_ref[...] += 0.5 * jnp.sum(jnp.mean(e * e, axis=-1, keepdims=True), axis=0, keepdims=True)
        dx, dg = _rms_bwd(e * (1.0 / D), xv, gv)
        dx_ref[...] = dx
        dg_ref[...] += dg

    return pl.pallas_call(
        body, name="final_loss", grid=(T // tm,), out_shape=[_sds((T, D), F32), _sds((1, 128), F32), _sds((1, D), F32)],
        in_specs=[pl.BlockSpec((tm, D), lambda i: (i, 0)), pl.BlockSpec((tm, D), lambda i: (i, 0)), pl.BlockSpec((1, D), lambda i: (0, 0))],
        out_specs=[pl.BlockSpec((tm, D), lambda i: (i, 0)), pl.BlockSpec((1, 128), lambda i: (0, 0)), pl.BlockSpec((1, D), lambda i: (0, 0))],
        compiler_params=_cp("arbitrary"))(x, tgt, g)


def _down_bwd(dx2, w_down, u, fcw, fcb, S):
    T = dx2.shape[0]
    tm = min(512, S)
    nps = S // tm
    nblk8 = T // 8

    def body(dx_ref, dxn_ref, wd_ref, ut_ref, up_ref, un_ref, cw_ref, cb_ref, du_ref, dcw_ref, dxb, ubuf, dcbuf, ush):
        i = pl.program_id(1)
        first, last = (i % nps) == 0, (i % nps) == nps - 1

        @pl.when(i == 0)
        def _():
            dcw_ref[...] = jnp.zeros_like(dcw_ref)

        dxb[0:tm, :] = dx_ref[...].astype(BF16)
        dxb[tm:tm + 8, :] = jnp.where(last, 0.0, dxn_ref[...]).astype(BF16)
        dh = _dot_nt(dxb[...], wd_ref[0])
        ce = []
        for half in range(2):
            ubuf[half, 0:8, :] = jnp.where(first, 0.0, up_ref[half, 0])
            ubuf[half, 8:8 + tm, :] = ut_ref[half, 0]
            ubuf[half, 8 + tm:16 + tm, :] = un_ref[half, 0]
            cw = cw_ref[half, 0]
            for k in range(2):
                ush[half, k] = ubuf[half, pl.ds(6 + k, tm + 8), :]
            ce.append(cb_ref[half, 0] + cw[0:1] * ush[half, 0] + cw[1:2] * ush[half, 1] + cw[2:3] * ubuf[half, pl.ds(8, tm + 8), :])
        sg = _sigmoid(ce[0])
        dc = [dh * ce[1] * (sg * (1.0 + ce[0] * (1.0 - sg))), dh * (ce[0] * sg)]
        for half in range(2):
            cw = cw_ref[half, 0]
            dcbuf[...] = dc[half]
            dct = dc[half][0:tm]
            du_ref[half, 0] = (cw[2:3] * dct + cw[1:2] * dcbuf[pl.ds(1, tm), :] + cw[0:1] * dcbuf[pl.ds(2, tm), :]).astype(BF16)
            for k in range(3):
                u_back = ush[half, k, 0:tm, :] if k < 2 else ubuf[half, pl.ds(8, tm), :]
                dcw_ref[half, 0, k:k + 1, :] += jnp.sum(dct * u_back, axis=0, keepdims=True)
            dcw_ref[half, 0, 3:4, :] += jnp.sum(dct, axis=0, keepdims=True)

    ublk = lambda rows, imap: pl.BlockSpec((2, 1, rows, FB), imap)
    return pl.pallas_call(
        body, name="down_bwd", grid=(NJ, T // tm), out_shape=[_sds((2, NJ, T, FB), BF16), _sds((2, NJ, 8, FB), F32)],
        in_specs=[pl.BlockSpec((tm, D), lambda j, i: (i, 0)),
                  pl.BlockSpec((8, D), lambda j, i: (jnp.minimum((i + 1) * (tm // 8), nblk8 - 1), 0)),
                  pl.BlockSpec((1, FB, D), lambda j, i: (j, 0, 0)),
                  ublk(tm, lambda j, i: (0, j, i, 0)),
                  ublk(8, lambda j, i: (0, j, jnp.maximum(i * (tm // 8) - 1, 0), 0)),
                  ublk(8, lambda j, i: (0, j, jnp.minimum((i + 1) * (tm // 8), nblk8 - 1), 0)),
                  pl.BlockSpec((2, 1, 3, FB), lambda j, i: (0, j, 0, 0)),
                  pl.BlockSpec((2, 1, 1, FB), lambda j, i: (0, j, 0, 0))],
        out_specs=[ublk(tm, lambda j, i: (0, j, i, 0)), pl.BlockSpec((2, 1, 8, FB), lambda j, i: (0, j, 0, 0))],
        scratch_shapes=[pltpu.VMEM((tm + 8, D), BF16), pltpu.VMEM((2, tm + 16, FB), F32), pltpu.VMEM((tm + 8, FB), F32),
                        pltpu.VMEM((2, 2, tm + 8, FB), F32)],
        compiler_params=_cp("parallel", "arbitrary"))(dx2, dx2, w_down, u, u, u, fcw, fcb)


def _wgrad(a, b, tn, name):
    na, T, ka = a.shape
    nb, _, kb = b.shape
    tk = min(512, T)
    nt = T // tk

    def body(a_ref, b_ref, o_ref, acc):
        t = pl.program_id(3)

        @pl.when(t == 0)
        def _():
            acc[...] = jnp.zeros_like(acc)

        acc[...] += _dot_tn(a_ref[0], b_ref[0].astype(BF16))

        @pl.when(t == nt - 1)
        def _():
            o_ref[0, 0] = acc[...].astype(GRAD_DTYPE)

    return pl.pallas_call(
        body, name=name, grid=(na, nb, kb // tn, nt), out_shape=_sds((na, nb, ka, kb), GRAD_DTYPE),
        in_specs=[pl.BlockSpec((1, tk, ka), lambda ia, ib, n, t: (ia, t, 0)), pl.BlockSpec((1, tk, tn), lambda ia, ib, n, t: (ib, t, n))],
        out_specs=pl.BlockSpec((1, 1, ka, tn), lambda ia, ib, n, t: (ia, ib, 0, n)),
        scratch_shapes=[pltpu.VMEM((ka, tn), F32)],
        compiler_params=_cp("parallel", "parallel", "parallel", "arbitrary"))(a, b)


def _dx_rms(dy, w, x, g, dres, blocked, name):
    T = x.shape[0]
    tm = min(512, T)
    nk = NJ if blocked else dy.shape[1] // D

    def body(dy_ref, w_ref, x_ref, g_ref, r_ref, dx_ref, dg_ref, acc):
        i, k = pl.program_id(0), pl.program_id(1)

        @pl.when((i == 0) & (k == 0))
        def _():
            dg_ref[...] = jnp.zeros_like(dg_ref)

        @pl.when(k == 0)
        def _():
            acc[...] = jnp.zeros_like(acc)

        if blocked:
            acc[...] += _dot_nt(dy_ref[0, 0], w_ref[0, 0]) + _dot_nt(dy_ref[1, 0], w_ref[1, 0])
        else:
            acc[...] += _dot_nt(dy_ref[...], w_ref[...])

        @pl.when(k == nk - 1)
        def _():
            dx, dg = _rms_bwd(acc[...], x_ref[...], g_ref[...])
            dx_ref[...] = r_ref[...] + dx
            dg_ref[...] += dg

    if blocked:
        dy_spec = pl.BlockSpec((2, 1, tm, FB), lambda i, k: (0, k, i, 0))
        w_spec = pl.BlockSpec((2, 1, D, FB), lambda i, k: (0, k, 0, 0))
    else:
        dy_spec = pl.BlockSpec((tm, D), lambda i, k: (i, k))
        w_spec = pl.BlockSpec((D, D), lambda i, k: (0, k))
    tile = pl.BlockSpec((tm, D), lambda i, k: (i, 0))
    return pl.pallas_call(
        body, name=name, grid=(T // tm, nk), out_shape=[_sds((T, D), F32), _sds((1, D), F32)],
        in_specs=[dy_spec, w_spec, tile, pl.BlockSpec((1, D), lambda i, k: (0, 0)), tile],
        out_specs=[tile, pl.BlockSpec((1, D), lambda i, k: (0, 0))],
        scratch_shapes=[pltpu.VMEM((tm, D), F32)],
        compiler_params=_cp("arbitrary", "arbitrary"))(dy, w, x, g, dres)


def _mm_nt(a, w, name):
    T = a.shape[0]
    tm = min(512, T)

    def body(a_ref, w_ref, o_ref):
        o_ref[...] = _dot_nt(a_ref[...].astype(BF16), w_ref[...])

    return pl.pallas_call(
        body, name=name, grid=(T // tm,), out_shape=_sds((T, D), F32),
        in_specs=[pl.BlockSpec((tm, D), lambda i: (i, 0)), pl.BlockSpec((D, D), lambda i: (0, 0))],
        out_specs=pl.BlockSpec((tm, D), lambda i: (i, 0)), compiler_params=_cp("parallel"))(a, w)


def _mix_bwd(dy, o_a, o_b, p_rest, g_a, g_b, cw, cb, ng, nbias, S):
    T = dy.shape[0]
    tm = min(256, S)
    nps = S // tm
    HALO = 32
    nblkh = T // HALO
    RE = tm + HALO

    def body(dy_ref, dyn_ref, oa_ref, ob_ref, pt_ref, pp_ref, pn_ref, ga_ref, gb_ref, cw_ref, cb_ref, ng_ref, nb_ref,
             doa_ref, dda_ref, dob_ref, ddb_ref, dg_ref, dgn_ref, dcw_ref, dcn_ref, gbuf, dvbuf, gph, dph):
        i = pl.program_id(0)
        first, last = (i % nps) == 0, (i % nps) == nps - 1

        @pl.when(i == 0)
        def _():
            dgn_ref[...] = jnp.zeros_like(dgn_ref)
            dcw_ref[...] = jnp.zeros_like(dcw_ref)
            dcn_ref[...] = jnp.zeros_like(dcn_ref)

        head_of = (lax.broadcasted_iota(jnp.int32, (8, WA), 1) // HD == lax.broadcasted_iota(jnp.int32, (8, WA), 0)).astype(F32)
        for n, (o_ref, g_ref, do_ref, dd_ref) in enumerate(((oa_ref, ga_ref, doa_ref, dda_ref), (ob_ref, gb_ref, dob_ref, ddb_ref))):
            o = o_ref[...]
            do, dg = _rms_bwd(dy_ref[:, n * WA:(n + 1) * WA], o, g_ref[...])
            dob = do.astype(BF16)
            do_ref[...] = dob
            ddt = _dot_nt(head_of, dob.astype(F32) * o, HIGHEST)
            for hd in range(8):
                dd_ref[0, hd] = ddt[hd:hd + 1, :]
            dgn_ref[n:n + 1, :] += dg

        pp = jnp.where(first, 0.0, pp_ref[...])
        gbuf[0:HALO, :] = pp[:, :CC] * _sigmoid(pp[:, CC:])
        pt = pt_ref[...]
        gv, sgg = pt[:, :CC], _sigmoid(pt[:, CC:])
        gbuf[HALO:HALO + tm, :] = gv * sgg
        pn = pn_ref[...]
        gbuf[HALO + tm:2 * HALO + tm, :] = pn[:, :CC] * _sigmoid(pn[:, CC:])
        _phase_copies(gbuf, gph, tm + 2 * HALO)
        cv = _conv_taps(gbuf, gph, cw_ref, HALO - CONV_K + 1, RE) + cb_ref[...]
        xc = cv - jnp.mean(cv, axis=-1, keepdims=True)
        r = lax.rsqrt(jnp.mean(xc * xc, axis=-1, keepdims=True) + EPS)
        xh = xc * r
        lo = xh * ng_ref[...] + nb_ref[...]
        sg = _sigmoid(lo)
        dyc = jnp.concatenate([dy_ref[:, 2 * WA:], jnp.where(last, 0.0, dyn_ref[...])], axis=0)
        dlo = dyc * (sg * (1.0 + lo * (1.0 - sg)))
        dcn_ref[0:1, :] += jnp.sum(dlo[0:tm] * xh[0:tm], axis=0, keepdims=True)
        dcn_ref[1:2, :] += jnp.sum(dlo[0:tm], axis=0, keepdims=True)
        dxh = dlo * ng_ref[...]
        dcv = r * (dxh - jnp.mean(dxh, axis=-1, keepdims=True) - xh * jnp.mean(dxh * xh, axis=-1, keepdims=True))
        dvbuf[...] = dcv
        _phase_copies(dvbuf, dph, RE)
        dct = dcv[0:tm]
        dcw_ref[CONV_K:CONV_K + 1, :] += jnp.sum(dct, axis=0, keepdims=True)
        dglu = jnp.zeros((tm, CC), F32)
        for k in range(CONV_K):
            dcw_ref[k:k + 1, :] += jnp.sum(dct * _tap(gbuf, gph, HALO - CONV_K + 1 + k, tm), axis=0, keepdims=True)
            dglu = dglu + cw_ref[k:k + 1, :] * _tap(dvbuf, dph, CONV_K - 1 - k, tm)
        dg_ref[:, :CC] = (dglu * sgg).astype(BF16)
        dg_ref[:, CC:] = (dglu * gv * sgg * (1.0 - sgg)).astype(BF16)

    row = lambda w: pl.BlockSpec((1, w), lambda i: (0, 0))
    tile = lambda w: pl.BlockSpec((tm, w), lambda i: (i, 0))
    nxt = lambda i: jnp.minimum((i + 1) * (tm // HALO), nblkh - 1)
    const = lambda r, w: pl.BlockSpec((r, w), lambda i: (0, 0))
    ddrow = pl.BlockSpec((1, 8, 1, tm), lambda i: (i, 0, 0, 0))
    return pl.pallas_call(
        body, name="mix_bwd", grid=(T // tm,),
        out_shape=[_sds((T, WA), BF16), _sds((T // tm, 8, 1, tm), F32), _sds((T, WA), BF16), _sds((T // tm, 8, 1, tm), F32),
                   _sds((T, 2 * CC), BF16), _sds((8, WA), F32), _sds((32, CC), F32), _sds((8, CC), F32)],
        in_specs=[tile(D), pl.BlockSpec((HALO, CC), lambda i: (nxt(i), 3)), tile(WA), tile(WA), tile(2 * CC),
                  pl.BlockSpec((HALO, 2 * CC), lambda i: (jnp.maximum(i * (tm // HALO) - 1, 0), 0)),
                  pl.BlockSpec((HALO, 2 * CC), lambda i: (nxt(i), 0)),
                  row(WA), row(WA), const(32, CC), row(CC), row(CC), row(CC)],
        out_specs=[tile(WA), ddrow, tile(WA), ddrow, tile(2 * CC), const(8, WA), const(32, CC), const(8, CC)],
        scratch_shapes=[pltpu.VMEM((tm + 2 * HALO, CC), F32), pltpu.VMEM((RE, CC), F32),
                        pltpu.VMEM((8, tm + 2 * HALO, CC), F32), pltpu.VMEM((8, RE, CC), F32)],
        compiler_params=_cp("arbitrary"))(dy, dy, o_a, o_b, p_rest, p_rest, p_rest, g_a, g_b, cw, cb, ng, nbias)


def _attn_bwd(p_qkv, col0, tab, do, lse, dd, S, aq=None, ak=None, name="attn_bwd"):
    T = p_qkv.shape[0]
    nb, TB = tab.shape[0] - 1, tab.shape[1]
    B = T // S
    fox = aq is not None

    def body(*refs):
        if fox:
            (q_ref, k_ref, v_ref, tab_ref, do_ref, l_ref, dd_ref, aq_ref, ak_ref,
             dq_ref, dk_ref, dv_ref, dqa_ref, dka_ref, dqt_scr, dk_scr, dv_scr, sd_scr) = refs
        else:
            q_ref, k_ref, v_ref, tab_ref, do_ref, l_ref, dd_ref, dq_ref, dk_ref, dv_ref, dqt_scr, dk_scr, dv_scr, sd_scr = refs
        j = pl.program_id(2)
        hms = _head_masks()

        @pl.when(j == 0)
        def _():
            dqt_scr[...] = jnp.zeros_like(dqt_scr)

        dk_scr[...] = jnp.zeros_like(dk_scr)
        dv_scr[...] = jnp.zeros_like(dv_scr)
        k2, v2 = k_ref[...], v_ref[...]
        ka = [jnp.where(hm, k2, ak_ref[0] if fox else jnp.zeros_like(k2)) for hm in hms]
        kat = [_transpose_bf16(a) for a in ka]

        def operands(i):
            ic = jnp.minimum(i, nb - 1)
            rows = pl.ds(pl.multiple_of(ic * TB, TB), TB)
            q2 = q_ref[rows, :] * 0.125
            do2 = do_ref[rows, :]
            qas = [jnp.where(hms[hh], q2, aq_ref[0, rows, :] if fox else jnp.zeros_like(q2)) for hh in range(2)]
            dohs = [jnp.where(hms[hh], do2, jnp.zeros_like(do2)) for hh in range(2)]
            return ic, qas, dohs

        def scores(i):
            ic, qas, dohs = operands(i)
            bias = tab_ref[jnp.where(i < nb, i - j, nb)]
            return [(_dot_nt(ka[hh], qas[hh]) + bias, _dot_nt(v2, dohs[hh])) for hh in range(2)]

        def absorb(i, sd):
            ic, qas, dohs = operands(i)
            for hh in range(2):
                st, dpt = sd[hh]
                pt = jnp.exp(st - l_ref[ic, hh])
                dsb = (pt * (dpt - dd_ref[ic, hh])).astype(BF16)
                dv_scr[...] += _dot(pt.astype(BF16), dohs[hh])
                dk_scr[hh] += _dot(dsb, qas[hh])
                dqt_scr[hh, ic] += _dot(kat[hh], dsb)

        def q_pair(ii, carry):
            i0 = j + 2 * ii
            sd0 = [(sd_scr[hh, 0], sd_scr[hh, 1]) for hh in range(2)]
            sd1 = scores(i0 + 1)
            absorb(i0, sd0)
            nxt = scores(i0 + 2)
            for hh in range(2):
                sd_scr[hh, 0], sd_scr[hh, 1] = nxt[hh]
            absorb(i0 + 1, sd1)
            return carry

        first = scores(j)
        for hh in range(2):
            sd_scr[hh, 0], sd_scr[hh, 1] = first[hh]
        lax.fori_loop(0, (nb - j + 1) // 2, q_pair, 0)
        dk_ref[...] = jnp.where(hms[0], dk_scr[0], dk_scr[1]).astype(BF16)
        dv_ref[...] = dv_scr[...].astype(BF16)
        if fox:
            dka_ref[...] = jnp.where(hms[0], dk_scr[1], dk_scr[0])

        @pl.when(j == nb - 1)
        def _():
            for ii in range(nb):
                t0, t1 = dqt_scr[0, ii].T, dqt_scr[1, ii].T
                dq_ref[ii * TB:(ii + 1) * TB, :] = jnp.where(hms[0], t0, t1) * 0.125
                if fox:
                    dqa_ref[ii * TB:(ii + 1) * TB, :] = jnp.where(hms[0], t1, t0)

    seq = lambda c: pl.BlockSpec((S, 128), lambda b, p, j: (b, c + p))
    kvb = lambda c: pl.BlockSpec((TB, 128), lambda b, p, j: (b * nb + j, c + p))
    stat = pl.BlockSpec((nb, 2, 1, TB), lambda b, p, j: (b, p, 0, 0))
    in_specs = [seq(col0), kvb(col0 + NPAIR), kvb(col0 + 2 * NPAIR), pl.BlockSpec((nb + 1, TB, TB), lambda b, p, j: (0, 0, 0)),
                seq(0), stat, stat]
    args = [p_qkv, p_qkv, p_qkv, tab, do, lse, dd]
    out_shape = [_sds((T, WA), F32), _sds((T, WA), BF16), _sds((T, WA), BF16)]
    out_specs = [seq(0), kvb(0), kvb(0)]
    if fox:
        in_specs += [pl.BlockSpec((1, S, 128), lambda b, p, j: (p, b, 0)), pl.BlockSpec((1, TB, 128), lambda b, p, j: (p, b * nb + j, 0))]
        args += [aq, ak]
        out_shape += [_sds((T, WA), F32), _sds((T, WA), F32)]
        out_specs += [seq(0), kvb(0)]
    return pl.pallas_call(
        body, name=name, grid=(B, NPAIR, nb), out_shape=out_shape, in_specs=in_specs, out_specs=out_specs,
        scratch_shapes=[pltpu.VMEM((2, nb, 128, TB), F32), pltpu.VMEM((2, TB, 128), F32), pltpu.VMEM((TB, 128), F32),
                        pltpu.VMEM((2, 2, TB, TB), F32)],
        compiler_params=_cp("parallel", "parallel", "arbitrary"))(*args)


def _gate_bwd(dqa, dka, p_rest, bf, S, TB):
    T = p_rest.shape[0]
    B, nb = T // S, S // TB

    def body(dqa_ref, dka_ref, z_ref, b_ref, dz_ref, db_ref):
        @pl.when(pl.program_id(0) == 0)
        def _():
            db_ref[...] = jnp.zeros_like(db_ref)

        later = (lax.broadcasted_iota(jnp.int32, (TB, TB), 1) >= lax.broadcasted_iota(jnp.int32, (TB, TB), 0)).astype(F32)
        lane = lax.broadcasted_iota(jnp.int32, (1, 128), 1)
        src, head = lax.broadcasted_iota(jnp.int32, (WA, 128), 0), lax.broadcasted_iota(jnp.int32, (WA, 128), 1)
        spare0 = 128 * (head // 2) + HD * (1 - head % 2)
        pick_q = (src == spare0 + AUG_K1).astype(F32)
        pick_k = (src == spare0 + AUG_Q1).astype(F32)
        carry = jnp.zeros((1, 128), F32)
        dbs = jnp.zeros((1, 128), F32)
        for j in reversed(range(nb)):
            rows = slice(j * TB, (j + 1) * TB)
            dc = _dot(dqa_ref[rows, :], pick_q, HIGHEST) - _dot(dka_ref[rows, :], pick_k, HIGHEST)
            part = _dot(later, dc, HIGHEST)
            tot = part + carry
            carry = carry + part[0:1, :]
            z = z_ref[j * TB:(j + 1) * TB, :] + b_ref[...]
            dz = jnp.where(lane < 6, tot * _sigmoid(-z), 0.0)
            dz_ref[j * TB:(j + 1) * TB, :] = dz.astype(BF16)
            dbs = dbs + jnp.sum(dz, axis=0, keepdims=True)
        db_ref[...] += dbs

    return pl.pallas_call(
        body, name="gate_bwd", grid=(B,), out_shape=[_sds((T, 128), BF16), _sds((1, 128), F32)],
        in_specs=[pl.BlockSpec((S, WA), lambda b: (b, 0)), pl.BlockSpec((S, WA), lambda b: (b, 0)),
                  pl.BlockSpec((S, 128), lambda b: (b, 4)), pl.BlockSpec((1, 128), lambda b: (0, 0))],
        out_specs=[pl.BlockSpec((S, 128), lambda b: (b, 0)), pl.BlockSpec((1, 128), lambda b: (0, 0))],
        compiler_params=_cp("arbitrary"))(dqa, dka, p_rest, bf)


def _adamw(parts, w, m, v, name, row0=0, prev=None):
    npart, Rp, C = parts.shape
    R = w.shape[0]
    tr = Rp
    for cand in (256, 128, 64, 32, 16, 8):
        if Rp % cand == 0 and row0 % cand == 0 and cand * C * 4 <= (1 << 20):
            tr = cand
            break
    if Rp == R and R * C * 4 <= (1 << 20):
        tr = R
    assert Rp % tr == 0 and row0 % tr == 0, (name, Rp, row0, tr)
    b0 = row0 // tr

    def body(p_ref, w_ref, m_ref, v_ref, *rest):
        g_ref, d_ref, mo_ref, vo_ref = rest[-4:]
        g = p_ref[0].astype(F32)
        for k in range(1, npart):
            g = g + p_ref[k].astype(F32)
        mn = ADAM_B1 * m_ref[...] + (1.0 - ADAM_B1) * g
        vn = ADAM_B2 * v_ref[...] + (1.0 - ADAM_B2) * (g * g)
        m_hat = mn / (1.0 - ADAM_B1 ** ADAM_STEP)
        v_hat = vn / (1.0 - ADAM_B2 ** ADAM_STEP)
        g_ref[...] = g
        d_ref[...] = -ADAM_LR * (m_hat / (jnp.sqrt(v_hat) + ADAM_EPS) + ADAM_WD * w_ref[...])
        mo_ref[...] = mn
        vo_ref[...] = vn

    blk = pl.BlockSpec((tr, C), lambda i: (b0 + i, 0))
    prev = list(prev) if prev is not None else []
    return pl.pallas_call(
        body, name=name, grid=(Rp // tr,), out_shape=[_sds((R, C), F32)] * 4,
        in_specs=[pl.BlockSpec((npart, tr, C), lambda i: (0, i, 0)), blk, blk, blk] + [_ANY] * len(prev), out_specs=[blk] * 4,
        input_output_aliases={4 + i: i for i in range(len(prev))},
        compiler_params=_cp("parallel"))(parts, w, m, v, *prev)


def _pad_w_in(w):
    z = jnp.zeros(w.shape[:-1] + (NPAD - N_IN,), w.dtype)
    return jnp.concatenate([w[..., :O_FA], w[..., O_QB:], w[..., O_FA:O_QB], z], axis=-1)


def _unpad_w_in(w):
    n_mid = N_IN - O_QB
    return jnp.concatenate([w[..., :O_FA], w[..., O_FA + n_mid:O_FA + n_mid + 6], w[..., O_FA:O_FA + n_mid]], axis=-1)


def _local_step(x, tgt, W, S, late_weights=None, on_layer_grads=None, on_ffn_grads=None):
    T = x.shape[0]
    TB = min(256, S)
    tab_fox, tab_dil = _bias_table(S, TB, True), _bias_table(S, TB, False)
    saved = []
    h = _rms_fwd(x, W["ln1_g"][0])
    for l in range(DEPTH):
        p_qkv, p_rest = _mm_in(h, W["w_in"][l])
        aq, ak = _gate_fwd(p_rest, W["b_forget"][l], S, TB)
        o_a, lse_a = _attn_fwd(p_qkv, 0, tab_fox, S, aq, ak, name="fox_fwd")
        o_b, lse_b = _attn_fwd(p_qkv, 3 * NPAIR, tab_dil, S, name="dil_fwd")
        y = _mix_fwd(o_a, o_b, p_rest, W["g_out_fox"][l], W["g_out_dil"][l], W["conv_w"][l], W["conv_b"][l],
                     W["cnorm_g"][l], W["cnorm_b"][l], S)
        if l == 0 and late_weights is not None:
            W = late_weights(W, y)
        x1, h2 = _mm_res(y[None], W["w_o"][l][None], x, W["ln2_g"][l], name="mm_o")
        u, hid = _up_fwd(h2, W["w_up"][l], W["ffn_conv_w"][l], W["ffn_conv_b"][l], S)
        g_next = W["ln1_g"][l + 1] if l + 1 < DEPTH else W["g_final"]
        x2, h_next = _mm_res(hid, W["w_down"][l], x1, g_next, name="mm_down")
        saved.append(dict(x=x, h=h, p_qkv=p_qkv, p_rest=p_rest, aq=aq, ak=ak, o_a=o_a, lse_a=lse_a, o_b=o_b,
                          lse_b=lse_b, y=y, x1=x1, h2=h2, u=u, hid=hid))
        x, h = x2, h_next
    dx, loss, dg_final = _final_loss(x, tgt, W["g_final"])
    G = {k: [None] * DEPTH for k in ("ln1_g", "w_in", "b_forget", "g_out_fox", "g_out_dil", "conv_w", "conv_b", "cnorm_g",
                                     "cnorm_b", "w_o", "ln2_g", "w_up", "ffn_conv_w", "ffn_conv_b", "w_down")}
    G["g_final"] = dg_final
    for l in reversed(range(DEPTH)):
        sv = saved[l]
        du, dcw = _down_bwd(dx, W["w_down"][l], sv["u"], W["ffn_conv_w"][l], W["ffn_conv_b"][l], S)
        G["w_down"][l] = _wgrad(sv["hid"], dx[None], D, "wg_down")[:, 0]
        G["ffn_conv_w"][l] = dcw[:, :, 0:3, :]
        G["ffn_conv_b"][l] = dcw[:, :, 3, :]
        G["w_up"][l] = _wgrad(sv["h2"][None], du.reshape(2 * NJ, T, FB), FB, "wg_up")[0]
        token = on_ffn_grads(l, G) if on_ffn_grads is not None else None
        ln2 = W["ln2_g"][l] if token is None else W["ln2_g"][l] + token[0, 0]
        dx1, G["ln2_g"][l] = _dx_rms(du, W["w_up"][l], sv["x1"], ln2, dx, True, "dx_up")
        G["w_o"][l] = _wgrad(sv["y"][None], dx1[None], D, "wg_o")[0, 0]
        dy = _mm_nt(dx1, W["w_o"][l], "mm_dy")
        do_a, dd_a, do_b, dd_b, dgvgg, dgn, dcvw, dcn = _mix_bwd(
            dy, sv["o_a"], sv["o_b"], sv["p_rest"], W["g_out_fox"][l], W["g_out_dil"][l], W["conv_w"][l], W["conv_b"][l],
            W["cnorm_g"][l], W["cnorm_b"][l], S)
        G["g_out_fox"][l], G["g_out_dil"][l] = dgn[0], dgn[1]
        G["conv_w"][l], G["conv_b"][l] = dcvw[:CONV_K], dcvw[CONV_K]
        G["cnorm_g"][l], G["cnorm_b"][l] = dcn[0], dcn[1]
        dq_a, dk_a, dv_a, dqa, dka = _attn_bwd(sv["p_qkv"], 0, tab_fox, do_a, sv["lse_a"], dd_a, S, sv["aq"], sv["ak"], name="fox_bwd")
        dq_b, dk_b, dv_b = _attn_bwd(sv["p_qkv"], 3 * NPAIR, tab_dil, do_b, sv["lse_b"], dd_b, S, name="dil_bwd")
        dfa, db = _gate_bwd(dqa, dka, sv["p_rest"], W["b_forget"][l], S, TB)
        G["b_forget"][l] = db[0, :6]
        dp = jnp.concatenate([dq_a.astype(BF16), dk_a, dv_a, dq_b.astype(BF16), dk_b, dv_b, dgvgg, dfa,
                              jnp.zeros((T, 128), BF16)], axis=1)
        G["w_in"][l] = _wgrad(sv["h"][None], dp[None], D, "wg_in")[0, 0]
        dx, G["ln1_g"][l] = _dx_rms(dp, W["w_in"][l], sv["x"], W["ln1_g"][l], dx1, False, "dx_in")
        token = on_layer_grads(l, G) if on_layer_grads is not None else None
        if token is not None and l > 0:
            W = dict(W, ffn_conv_b=[b + token[0, 0] for b in W["ffn_conv_b"]])
    return loss, dx, G


_MATMUL_W = ("w_in", "w_o", "w_up", "w_down")
_SHARDED = _MATMUL_W + ("conv_w", "ffn_conv_w")
_SMALL = ("ln1_g", "b_forget", "g_out_fox", "g_out_dil", "conv_b", "cnorm_g", "cnorm_b", "ln2_g", "ffn_conv_b", "g_final")
_ORDER = ("ln1_g", "w_in", "b_forget", "g_out_fox", "g_out_dil", "conv_w", "conv_b", "cnorm_g", "cnorm_b", "w_o", "ln2_g",
          "w_up", "ffn_conv_w", "ffn_conv_b", "w_down", "g_final")


def _kernel_ready(k, zone):
    if k == "w_in":
        return _pad_w_in(zone.reshape(D, N_IN))
    if k == "w_o":
        return zone.reshape(D, D)
    if k == "w_up":
        return zone.reshape(2, NJ, D, FB)
    if k == "w_down":
        return zone.reshape(NJ, FB, D)
    if k == "conv_w":
        cw = jnp.transpose(zone.reshape(NDEV, CONV_K, CC // NDEV), (1, 0, 2)).reshape(CONV_K, CC)
        return jnp.concatenate([cw, jnp.zeros((1, CC), F32)], axis=0)
    return zone.reshape(2, NJ, 3, FB)


def _full_weights(P, gathered):
    W = {}
    row = lambda a: [a[l][None, :] for l in range(DEPTH)]
    for k in _SHARDED:
        W[k] = [None if gathered[k][l] is None else _kernel_ready(k, gathered[k][l]) for l in range(DEPTH)]
    W["ffn_conv_b"] = [P["ffn_conv_b"][l].reshape(2, NJ, 1, FB) for l in range(DEPTH)]
    W["b_forget"] = [jnp.concatenate([P["b_forget"][l], jnp.zeros((128 - 6,), F32)])[None, :] for l in range(DEPTH)]
    for k in ("ln1_g", "ln2_g", "g_out_fox", "g_out_dil", "conv_b", "cnorm_g", "cnorm_b"):
        W[k] = row(P[k])
    W["g_final"] = P["g_final"][None, :]
    return W


def kernel(x, ln1_g, w_in, b_forget, g_out_fox, g_out_dil, conv_w, conv_b, cnorm_g, cnorm_b, w_o, ln2_g, w_up, ffn_conv_w, ffn_conv_b, w_down, g_final, loss_target, m_ln1_g, m_w_in, m_b_forget, m_g_out_fox, m_g_out_dil, m_conv_w, m_conv_b, m_cnorm_g, m_cnorm_b, m_w_o, m_ln2_g, m_w_up, m_ffn_conv_w, m_ffn_conv_b, m_w_down, m_g_final, v_ln1_g, v_w_in, v_b_forget, v_g_out_fox, v_g_out_dil, v_conv_w, v_conv_b, v_cnorm_g, v_cnorm_b, v_w_o, v_ln2_g, v_w_up, v_ffn_conv_w, v_ffn_conv_b, v_w_down, v_g_final):
    P = dict(ln1_g=ln1_g, w_in=w_in, b_forget=b_forget, g_out_fox=g_out_fox, g_out_dil=g_out_dil, conv_w=conv_w, conv_b=conv_b,
             cnorm_g=cnorm_g, cnorm_b=cnorm_b, w_o=w_o, ln2_g=ln2_g, w_up=w_up, ffn_conv_w=ffn_conv_w, ffn_conv_b=ffn_conv_b,
             w_down=w_down, g_final=g_final)
    M = dict(ln1_g=m_ln1_g, w_in=m_w_in, b_forget=m_b_forget, g_out_fox=m_g_out_fox, g_out_dil=m_g_out_dil, conv_w=m_conv_w,
             conv_b=m_conv_b, cnorm_g=m_cnorm_g, cnorm_b=m_cnorm_b, w_o=m_w_o, ln2_g=m_ln2_g, w_up=m_w_up, ffn_conv_w=m_ffn_conv_w,
             ffn_conv_b=m_ffn_conv_b, w_down=m_w_down, g_final=m_g_final)
    V = dict(ln1_g=v_ln1_g, w_in=v_w_in, b_forget=v_b_forget, g_out_fox=v_g_out_fox, g_out_dil=v_g_out_dil, conv_w=v_conv_w,
             conv_b=v_conv_b, cnorm_g=v_cnorm_g, cnorm_b=v_cnorm_b, w_o=v_w_o, ln2_g=v_ln2_g, w_up=v_w_up, ffn_conv_w=v_ffn_conv_w,
             ffn_conv_b=v_ffn_conv_b, w_down=v_w_down, g_final=v_g_final)
    Bl, S, _ = x.shape
    T = Bl * S

    shard = lambda k, l: P[k][l].astype(BF16) if k in _MATMUL_W else P[k][l]
    early = [("w_in", 0)] + [(k, l) for k in ("conv_w", "ffn_conv_w") for l in range(DEPTH)]
    late = [(k, 0) for k in ("w_o", "w_up", "w_down")] + [(k, 1) for k in _MATMUL_W]
    gathered = {k: [None] * DEPTH for k in _SHARDED}
    plan_early = _Plan([[shard(k, l)] for k, l in early], True)
    for (k, l), zone in zip(early, _exchange(plan_early, "gather_early")):
        gathered[k][l] = zone
    plan_late = _Plan([[shard(k, l)] for k, l in late], True)
    late_handle, late_token = _exchange_start(plan_late, "gather_late_start", plan_late.zones_with_own())
    W = _full_weights(P, gathered)
    W["ln1_g"][0] = W["ln1_g"][0] + late_token[0, 0]

    def late_weights(W, after):
        zones = _exchange_wait(plan_late, "gather_late_wait", late_handle, after)
        W = dict(W)
        for (k, l), zone in zip(late, zones):
            W[k] = list(W[k])
            W[k][l] = _kernel_ready(k, zone)
        return W

    def owner_block(G, l, k):
        if k == "w_in":
            blk = _unpad_w_in(G[k][l]).reshape(NDEV, D // NDEV, N_IN)
        elif k == "w_o":
            blk = G[k][l].reshape(NDEV, D // NDEV, D)
        elif k == "w_up":
            blk = G[k][l]
        elif k == "w_down":
            blk = G[k][l].reshape(NDEV, DFF // NDEV, D)
        elif k == "conv_w":
            blk = jnp.transpose(G[k][l].reshape(CONV_K, NDEV, CC // NDEV), (1, 0, 2))
        else:
            blk = G[k][l].reshape(NDEV, 3, FB)
        return blk.astype(GRAD_DTYPE)

    flying = []

    def start_scatter(tag, G, l, keys):
        plan = _Plan([[owner_block(G, l, k)] for k in keys], False)
        handle, token = _exchange_start(plan, "scatter_%s_start" % tag, plan.zones_with_own())
        flying.append((tag, l, keys, plan, handle))
        return token

    def on_layer_grads(l, G):
        return start_scatter("l1", G, l, _MATMUL_W) if l == 1 else None

    def on_ffn_grads(l, G):
        return start_scatter("l0_ffn", G, l, ("w_up", "w_down")) if l == 0 else None

    loss, dx, G = _local_step(x.reshape(T, D), loss_target.reshape(T, D), W, S, late_weights, on_layer_grads, on_ffn_grads)

    parts = {}
    for tag, l, keys, plan, handle in flying:
        for k, zone in zip(keys, _exchange_wait(plan, "scatter_%s_wait" % tag, handle, dx)):
            parts[(k, l)] = zone
    plan_last = _Plan([[owner_block(G, 0, "w_in")], [owner_block(G, 0, "w_o")]]
                      + [[owner_block(G, l, k) for l in range(DEPTH)] for k in ("conv_w", "ffn_conv_w")], False)
    last = _exchange(plan_last, "scatter_last")
    parts[("w_in", 0)], parts[("w_o", 0)] = last[0], last[1]
    small_parts_sharded = {"conv_w": last[2], "ffn_conv_w": last[3]}

    small = [jnp.stack(G[k]).reshape(-1) for k in _SMALL[:-1]] + [G["g_final"].reshape(-1), loss[0, :1]]
    sizes = [int(s.shape[0]) for s in small]
    flat = jnp.concatenate(small)
    n_small = -(-flat.shape[0] // 1024) * 1024
    flat = jnp.concatenate([flat, jnp.zeros((n_small - flat.shape[0],), F32)]).reshape(n_small // 128, 128)
    small_parts = _exchange(_Plan([[flat]], True), "gather_small")[0]

    out_g, out_d, out_m, out_v = {}, {}, {}, {}
    sg = _sum_parts(small_parts.reshape(NDEV, n_small // 128, 128), "sum_small").reshape(-1)
    off = 0
    summed = {}
    for k, n in zip(_SMALL + ("loss",), sizes):
        summed[k] = sg[off:off + n]
        off += n
    for k in _ORDER:
        shp = P[k].shape
        C = shp[-1]
        if k in _MATMUL_W:
            R0 = shp[1]
            flat2 = lambda t: t.reshape(DEPTH * R0, C)
            res = None
            for l in reversed(range(DEPTH)):
                res = _adamw(parts[(k, l)].reshape(NDEV, R0, C), flat2(P[k]), flat2(M[k]), flat2(V[k]),
                             "adamw_%s_l%d" % (k, l), row0=l * R0, prev=res)
        else:
            pr = small_parts_sharded[k].reshape((NDEV, -1, C)) if k in _SHARDED else summed[k].reshape((1, -1, C))
            R = pr.shape[1]
            res = _adamw(pr, P[k].reshape(R, -1), M[k].reshape(R, -1), V[k].reshape(R, -1), "adamw_" + k)
        out_g[k], out_d[k], out_m[k], out_v[k] = (t.reshape(shp) for t in res)

    loss_out = summed["loss"].reshape(())
    grad_x = dx.reshape(Bl, S, D)
    return (loss_out, grad_x, *[out_g[k] for k in _ORDER], *[out_d[k] for k in _ORDER], *[out_m[k] for k in _ORDER],
            *[out_v[k] for k in _ORDER])
```

```python
import functools

import numpy as np
import jax
import jax.numpy as jnp
from jax import lax
from jax.experimental import pallas as pl
from jax.experimental.pallas import tpu as pltpu

F32 = jnp.float32
BF16 = jnp.bfloat16
GRAD_DTYPE = BF16
HIGHEST = lax.Precision.HIGHEST

D = 1024
DEPTH = 2
HD = 64
WA = 384
NPAIR = 3
CC = 256
CONV_K = 31
DFF = 2816
FB = 704
NJ = 4
NDEV = 8
EPS = 1e-6
NQKV = 2304
NREST = 768
NPAD = NQKV + NREST
O_FA, O_QB, N_IN = 1152, 1158, 2822
DILATION_PAIRS = ((128, 1), (512, 4), (2048, 16))
NEG = -1e30

ADAM_LR, ADAM_B1, ADAM_B2, ADAM_EPS, ADAM_WD, ADAM_STEP = 0.001, 0.9, 0.999, 1e-08, 0.01, 10

VMEM_LIMIT = 48 * 1024 * 1024
VMEM_LIMIT_BIG = 56 * 1024 * 1024


def _cp(*sem):
    return pltpu.CompilerParams(dimension_semantics=sem, vmem_limit_bytes=VMEM_LIMIT)


def _sds(shape, dtype):
    return jax.ShapeDtypeStruct(shape, dtype)


def _dot(a, b, prec=None):
    return jnp.dot(a, b, preferred_element_type=F32, precision=prec)


def _dot_nt(a, b, prec=None):
    return lax.dot_general(a, b, (((1,), (1,)), ((), ())), preferred_element_type=F32, precision=prec)


def _dot_tn(a, b):
    return lax.dot_general(a, b, (((0,), (0,)), ((), ())), preferred_element_type=F32)


def _sigmoid(z):
    return 0.5 * jnp.tanh(0.5 * z) + 0.5


def _rms_bwd(dh, x, g):
    r = lax.rsqrt(jnp.mean(x * x, axis=-1, keepdims=True) + EPS)
    xh = x * r
    dg = jnp.sum(dh * xh, axis=0, keepdims=True)
    dxh = dh * g
    dx = r * (dxh - xh * jnp.mean(dxh * xh, axis=-1, keepdims=True))
    return dx, dg


class _Plan:
    def __init__(self, groups, gather, slots=None, n_slots=None):
        self.groups, self.gather = groups, gather
        self.slots = slots or [list(range(len(g))) for g in groups]
        self.n_slots = n_slots or [len(g) for g in groups]
        self.flat = [a for g in groups for a in g]
        self.where = [(gi, s) for gi, g in enumerate(groups) for s in self.slots[gi]]
        self.zone_shapes = [_sds((NDEV, ns) + (g[0].shape if gather else g[0].shape[1:]), g[0].dtype)
                            for g, ns in zip(groups, self.n_slots)]

    def new_zones(self):
        return [lax.empty(z.shape, z.dtype) for z in self.zone_shapes]

    def me(self):
        x, y, c = lax.axis_index("x"), lax.axis_index("y"), lax.axis_index("c")
        return 4 * x + 2 * y + c

    def zones_with_own(self):
        me = self.me()
        zones = self.new_zones()
        for a, (gi, s) in enumerate(self.where):
            src = self.flat[a] if self.gather else lax.dynamic_index_in_dim(self.flat[a], me, 0, keepdims=False)
            zones[gi] = lax.dynamic_update_slice(zones[gi], src[None, None], (me, s) + (0,) * src.ndim)
        return zones

    def own_copies(self, ins, zones, sems):
        me = self.me()
        return [pltpu.make_async_copy(ins[a] if self.gather else ins[a].at[me], zones[gi].at[me, s], sems.at[a])
                for a, (gi, s) in enumerate(self.where)]

    def remote_copies(self, ins, zones, send_sems, recv_sems):
        x, y, c = lax.axis_index("x"), lax.axis_index("y"), lax.axis_index("c")
        me = 4 * x + 2 * y + c
        out = []
        for a, (gi, s) in enumerate(self.where):
            for k in range(1, NDEV):
                px, py, pc = x ^ ((k >> 2) & 1), y ^ ((k >> 1) & 1), c ^ (k & 1)
                out.append(pltpu.make_async_remote_copy(
                    src_ref=ins[a] if self.gather else ins[a].at[4 * px + 2 * py + pc], dst_ref=zones[gi].at[me, s],
                    send_sem=send_sems.at[a * (NDEV - 1) + k - 1], recv_sem=recv_sems.at[a * (NDEV - 1) + k - 1],
                    device_id=(px, py, pc), device_id_type=pl.DeviceIdType.MESH))
        return out


_ANY = pl.BlockSpec(memory_space=pl.ANY)
_HBM = pl.BlockSpec(memory_space=pltpu.HBM)
_SEM = pl.BlockSpec(memory_space=pltpu.SEMAPHORE)


def _exchange(plan, name, zones=None):
    n, nz = len(plan.flat), len(plan.groups)
    zones = zones or plan.new_zones()

    def body(*refs):
        ins, zin = refs[:n], refs[n:n + nz]
        send_sems, recv_sems, local_sems = refs[n + 2 * nz:]
        copies = plan.own_copies(ins, zin, local_sems) + plan.remote_copies(ins, zin, send_sems, recv_sems)
        for cp in copies:
            cp.start()
        for cp in copies:
            cp.wait()

    return pl.pallas_call(
        body, name=name, out_shape=plan.zone_shapes, in_specs=[_ANY] * (n + nz), out_specs=[_ANY] * nz,
        input_output_aliases={n + g: g for g in range(nz)},
        scratch_shapes=[pltpu.SemaphoreType.DMA((n * (NDEV - 1),)), pltpu.SemaphoreType.DMA((n * (NDEV - 1),)),
                        pltpu.SemaphoreType.DMA((n,))],
        compiler_params=pltpu.CompilerParams(has_side_effects=True),
    )(*plan.flat, *zones)


def _exchange_start(plan, name, zones):
    n, nz = len(plan.flat), len(plan.groups)
    hbm = lambda a: pltpu.HBM(a.shape, a.dtype)

    def body(*refs):
        ins, zin = refs[:n], refs[n:n + nz]
        send_sems, recv_sems = refs[n + nz], refs[n + nz + 1]
        token = refs[-1]
        for cp in plan.remote_copies(ins, zin, send_sems, recv_sems):
            cp.start()
        token[...] = jnp.zeros_like(token)

    outs = pl.pallas_call(
        body, name=name,
        out_shape=(pltpu.SemaphoreType.DMA((n * (NDEV - 1),)), pltpu.SemaphoreType.DMA((n * (NDEV - 1),)),
                   *[hbm(a) for a in plan.flat], *[hbm(z) for z in plan.zone_shapes], _sds((8, 128), F32)),
        in_specs=[_HBM] * (n + nz),
        out_specs=(_SEM, _SEM, *[_HBM] * (n + nz), pl.BlockSpec(memory_space=pltpu.VMEM)),
        input_output_aliases={i: 2 + i for i in range(n + nz)},
        compiler_params=pltpu.CompilerParams(has_side_effects=pltpu.SideEffectType.DATAFLOW_SIDE_EFFECTING),
    )(*[pltpu.with_memory_space_constraint(a, pltpu.HBM) for a in plan.flat],
      *[pltpu.with_memory_space_constraint(z, pltpu.HBM) for z in zones])
    return outs[:-1], outs[-1]


def _exchange_wait(plan, name, handle, after):
    n, nz = len(plan.flat), len(plan.groups)
    send_sems, recv_sems = handle[0], handle[1]
    thru = handle[2:]
    hbm = lambda a: pltpu.HBM(a.shape, a.dtype)

    def body(*refs):
        ins, zin = refs[:n], refs[n:n + nz]
        ssem, rsem = refs[n + nz], refs[n + nz + 1]
        for cp in plan.remote_copies(ins, zin, ssem, rsem):
            cp.wait_send()
            cp.wait_recv()

    outs = pl.pallas_call(
        body, name=name, out_shape=tuple(hbm(a) for a in thru),
        in_specs=[_HBM] * (n + nz) + [_SEM, _SEM, _ANY], out_specs=tuple([_HBM] * (n + nz)),
        input_output_aliases={i: i for i in range(n + nz)},
        compiler_params=pltpu.CompilerParams(has_side_effects=pltpu.SideEffectType.DATAFLOW_SIDE_EFFECTING),
    )(*thru, send_sems, recv_sems, after)
    return list(outs[n:])


def _sum_parts(parts, name):
    npart, R, C = parts.shape

    def body(p_ref, o_ref):
        s = p_ref[0]
        for k in range(1, npart):
            s = s + p_ref[k]
        o_ref[...] = s

    return pl.pallas_call(body, name=name, out_shape=_sds((R, C), F32))(parts)


def _rms_fwd(x, g):
    T = x.shape[0]
    tm = min(512, T)

    def body(x_ref, g_ref, o_ref):
        xv = x_ref[...]
        r = lax.rsqrt(jnp.mean(xv * xv, axis=-1, keepdims=True) + EPS)
        o_ref[...] = (xv * r * g_ref[...]).astype(BF16)

    return pl.pallas_call(
        body, name="rms_fwd", grid=(T // tm,), out_shape=_sds((T, D), BF16),
        in_specs=[pl.BlockSpec((tm, D), lambda i: (i, 0)), pl.BlockSpec((1, D), lambda i: (0, 0))],
        out_specs=pl.BlockSpec((tm, D), lambda i: (i, 0)), compiler_params=_cp("parallel"))(x, g)


def _mm_in(h, w):
    T = h.shape[0]
    tm = min(512, T)

    def body(h_ref, w_ref, q_ref, r_ref):
        hv = h_ref[...]
        q_ref[...] = _dot(hv, w_ref[:, :NQKV]).astype(BF16)
        r_ref[...] = _dot(hv, w_ref[:, NQKV:])

    return pl.pallas_call(
        body, name="mm_in", grid=(T // tm,), out_shape=[_sds((T, NQKV), BF16), _sds((T, NREST), F32)],
        in_specs=[pl.BlockSpec((tm, D), lambda i: (i, 0)), pl.BlockSpec((D, NPAD), lambda i: (0, 0))],
        out_specs=[pl.BlockSpec((tm, NQKV), lambda i: (i, 0)), pl.BlockSpec((tm, NREST), lambda i: (i, 0))],
        compiler_params=_cp("parallel"))(h, w)


AUG_Q1, AUG_K1 = 3, 0


def _aug_lane0(h):
    return HD * (1 - h % 2)


def _aug_tables():
    selq = np.zeros((NPAIR, 3 * 128, 128), np.float32)
    selk = np.zeros((NPAIR, 3 * 128, 128), np.float32)
    oneq = np.zeros((NPAIR, 1, 128), np.float32)
    onek = np.zeros((NPAIR, 1, 128), np.float32)
    for h in range(2 * NPAIR):
        p, l0 = h // 2, _aug_lane0(h)
        for piece in range(3):
            selq[p, 128 * piece + h, l0 + piece] = 1.0
            selk[p, 128 * piece + h, l0 + 3 + piece] = -1.0
            oneq[p, 0, l0 + AUG_Q1 + piece] = 1.0
            onek[p, 0, l0 + AUG_K1 + piece] = 1.0
    return [jnp.asarray(a, BF16) for a in (selq, selk, oneq, onek)]


def _gate_fwd(p_rest, bf, S, TB):
    T = p_rest.shape[0]
    B, nb = T // S, S // TB
    selq, selk, oneq, onek = _aug_tables()

    def body(z_ref, b_ref, sq_ref, sk_ref, oq_ref, ok_ref, aq_ref, ak_ref):
        tri = (lax.broadcasted_iota(jnp.int32, (TB, TB), 0) >= lax.broadcasted_iota(jnp.int32, (TB, TB), 1)).astype(F32)
        carry = jnp.zeros((1, 128), F32)
        for j in range(nb):
            rows = slice(j * TB, (j + 1) * TB)
            z = z_ref[rows, :] + b_ref[...]
            lf = jnp.minimum(z, 0.0) - jnp.log(1.0 + jnp.exp(-jnp.abs(z)))
            cb = _dot(tri, lf, HIGHEST) + carry
            carry = cb[TB - 1:TB, :]
            hi = cb.astype(BF16)
            mid = (cb - hi.astype(F32)).astype(BF16)
            lo = (cb - hi.astype(F32) - mid.astype(F32)).astype(BF16)
            pieces = jnp.concatenate([hi, mid, lo], axis=1)
            for p in range(NPAIR):
                aq_ref[p, rows, :] = (_dot(pieces, sq_ref[p]) + oq_ref[p].astype(F32)).astype(BF16)
                ak_ref[p, rows, :] = (_dot(pieces, sk_ref[p]) + ok_ref[p].astype(F32)).astype(BF16)

    full = lambda a: pl.BlockSpec(a.shape, lambda b: (0,) * a.ndim)
    return pl.pallas_call(
        body, name="gate_fwd", grid=(B,), out_shape=[_sds((NPAIR, T, 128), BF16), _sds((NPAIR, T, 128), BF16)],
        in_specs=[pl.BlockSpec((S, 128), lambda b: (b, 4)), pl.BlockSpec((1, 128), lambda b: (0, 0)),
                  full(selq), full(selk), full(oneq), full(onek)],
        out_specs=[pl.BlockSpec((NPAIR, S, 128), lambda b: (0, b, 0)), pl.BlockSpec((NPAIR, S, 128), lambda b: (0, b, 0))],
        compiler_params=_cp("parallel"))(p_rest, bf, selq, selk, oneq, onek)


def _bias_table(S, TB, fox):
    nb = S // TB
    dj = np.arange(nb)[:, None, None]
    delta = dj * TB + np.arange(TB)[None, None, :] - np.arange(TB)[None, :, None]
    if fox:
        mult = (delta >= 0).astype(np.float64)
    else:
        mult = np.zeros(delta.shape)
        for w, d in DILATION_PAIRS:
            mult += (delta >= 0) & (delta % d == 0) & (delta <= w)
    with np.errstate(divide="ignore"):
        tab = np.where(mult > 0, np.log(np.maximum(mult, 1e-30)), NEG)
    tab = np.concatenate([tab, np.full((1, TB, TB), NEG)], axis=0)
    return jnp.asarray(tab, F32)


def _head_masks():
    lane = lax.broadcasted_iota(jnp.int32, (1, 128), 1)
    return [lane < HD, lane >= HD]


def _transpose_bf16(a):
    return a.astype(F32).T.astype(BF16)


def _col_reduce(x, op):
    while x.shape[0] > 8:
        half = x.shape[0] // 2
        x = op(x[:half], x[half:])
    return jnp.max(x, axis=0, keepdims=True) if op is jnp.maximum else jnp.sum(x, axis=0, keepdims=True)


def _attn_fwd(p_qkv, col0, tab, S, aq=None, ak=None, name="attn"):
    T = p_qkv.shape[0]
    nb, TB = tab.shape[0] - 1, tab.shape[1]
    B = T // S
    fox = aq is not None

    def body(*refs):
        if fox:
            q_ref, k_ref, v_ref, tab_ref, aq_ref, ak_ref, o_ref, l_ref, kat_ref, vt_scr, acc_scr, st_scr = refs
        else:
            q_ref, k_ref, v_ref, tab_ref, o_ref, l_ref, kat_ref, vt_scr, acc_scr, st_scr = refs
        i = pl.program_id(2)
        hms = _head_masks()

        @pl.when(i == 0)
        def _():
            for jj in range(nb):
                rows = slice(jj * TB, (jj + 1) * TB)
                vt_scr[jj] = _transpose_bf16(v_ref[rows, :])
                for hh in range(2):
                    spare_k = ak_ref[0, rows, :] if fox else jnp.zeros((TB, 128), BF16)
                    kat_ref[0, 0, hh, jj] = _transpose_bf16(jnp.where(hms[hh], k_ref[rows, :], spare_k))

        q2 = q_ref[...] * 0.125
        spare_q = aq_ref[0] if fox else jnp.zeros_like(q2)
        qa = [jnp.where(hm, q2, spare_q) for hm in hms]
        acc_scr[...] = jnp.zeros_like(acc_scr)

        def scores(j):
            jc = jnp.minimum(j, nb - 1)
            rows = pl.ds(pl.multiple_of(jc * TB, TB), TB)
            k2 = k_ref[rows, :]
            bias = tab_ref[jnp.where(j <= i, i - j, nb)]
            return [_dot_nt(jnp.where(hms[hh], k2, ak_ref[0, rows, :]) if fox else k2, qa[hh]) + bias for hh in range(2)]

        def absorb(j, sts, stats):
            jc = jnp.minimum(j, nb - 1)
            alphas, pts = [], []
            for hh in range(2):
                m_old, l_old = stats[2 * hh], stats[2 * hh + 1]
                m_new = jnp.maximum(m_old, _col_reduce(sts[hh], jnp.maximum))
                pt = jnp.exp(sts[hh] - m_new)
                alphas.append(jnp.exp(m_old - m_new))
                stats[2 * hh] = m_new
                stats[2 * hh + 1] = alphas[hh] * l_old + _col_reduce(pt, jnp.add)
                pts.append(pt.astype(BF16))
            pvs = [_dot(vt_scr[jc, HD * hh:HD * (hh + 1), :], pts[hh]) for hh in range(2)]
            for hh in range(2):
                half = slice(HD * hh, HD * (hh + 1))
                acc_scr[half, :] = alphas[hh] * acc_scr[half, :] + pvs[hh]

        def kv_pair(jj, carry):
            stats = list(carry)
            j0 = 2 * jj
            st0 = [st_scr[hh] for hh in range(2)]
            st1 = scores(j0 + 1)
            absorb(j0, st0, stats)
            nxt = scores(j0 + 2)
            for hh in range(2):
                st_scr[hh] = nxt[hh]
            absorb(j0 + 1, st1, stats)
            return tuple(stats)

        first = scores(0)
        for hh in range(2):
            st_scr[hh] = first[hh]
        row = lambda v: jnp.full((1, TB), v, F32)
        m0, l0, m1, l1 = lax.fori_loop(0, i // 2 + 1, kv_pair, (row(NEG), row(0.0), row(NEG), row(0.0)))
        top = lax.broadcasted_iota(jnp.int32, (128, 1), 0) < HD
        o_ref[...] = (acc_scr[...] * jnp.where(top, 1.0 / l0, 1.0 / l1)).T
        l_ref[0, 0] = m0 + jnp.log(l0)
        l_ref[0, 1] = m1 + jnp.log(l1)

    in_specs = [pl.BlockSpec((TB, 128), lambda b, p, i: (b * nb + i, col0 + p)),
                pl.BlockSpec((S, 128), lambda b, p, i: (b, col0 + NPAIR + p)),
                pl.BlockSpec((S, 128), lambda b, p, i: (b, col0 + 2 * NPAIR + p)),
                pl.BlockSpec((nb + 1, TB, TB), lambda b, p, i: (0, 0, 0))]
    args = [p_qkv, p_qkv, p_qkv, tab]
    if fox:
        in_specs += [pl.BlockSpec((1, TB, 128), lambda b, p, i: (p, b * nb + i, 0)),
                     pl.BlockSpec((1, S, 128), lambda b, p, i: (p, b, 0))]
        args += [aq, ak]
    return pl.pallas_call(
        body, name=name, grid=(B, NPAIR, nb),
        out_shape=[_sds((T, WA), F32), _sds((T // TB, 2 * NPAIR, 1, TB), F32), _sds((B, NPAIR, 2, nb, 128, TB), BF16)],
        in_specs=in_specs,
        out_specs=[pl.BlockSpec((TB, 128), lambda b, p, i: (b * nb + i, p)),
                   pl.BlockSpec((1, 2, 1, TB), lambda b, p, i: (b * nb + i, p, 0, 0)),
                   pl.BlockSpec((1, 1, 2, nb, 128, TB), lambda b, p, i: (b, p, 0, 0, 0, 0))],
        scratch_shapes=[pltpu.VMEM((nb, 128, TB), BF16), pltpu.VMEM((128, TB), F32), pltpu.VMEM((2, TB, TB), F32)],
        compiler_params=_cp("parallel", "parallel", "arbitrary"))(*args)


def _phase_copies(src, phases, length):
    for r in range(1, 8):
        phases[r, 0:length - 8, :] = src[pl.ds(r, length - 8), :]


def _tap(src, phases, offset, rows):
    r = offset % 8
    return src[pl.ds(offset, rows), :] if r == 0 else phases[r, pl.ds(offset - r, rows), :]


def _conv_taps(src, phases, w_ref, first_row, rows):
    acc = w_ref[0:1, :] * _tap(src, phases, first_row, rows)
    for k in range(1, CONV_K):
        acc = acc + w_ref[k:k + 1, :] * _tap(src, phases, first_row + k, rows)
    return acc


def _mix_fwd(o_a, o_b, p_rest, g_a, g_b, cw, cb, ng, nbias, S):
    T = o_a.shape[0]
    tm = min(256, S)
    nps = S // tm
    HALO = 32

    def body(oa_ref, ob_ref, pt_ref, ph_ref, ga_ref, gb_ref, cw_ref, cb_ref, ng_ref, nb_ref, y_ref, buf, phases):
        i = pl.program_id(0)
        first = (i % nps) == 0
        for o_ref, g_ref, c0 in ((oa_ref, ga_ref, 0), (ob_ref, gb_ref, WA)):
            o = o_ref[...]
            r = lax.rsqrt(jnp.mean(o * o, axis=-1, keepdims=True) + EPS)
            y_ref[:, c0:c0 + WA] = (o * r * g_ref[...]).astype(BF16)
        ph = jnp.where(first, 0.0, ph_ref[...])
        buf[0:HALO, :] = ph[:, :CC] * _sigmoid(ph[:, CC:])
        pt = pt_ref[...]
        buf[HALO:HALO + tm, :] = pt[:, :CC] * _sigmoid(pt[:, CC:])
        _phase_copies(buf, phases, tm + HALO)
        cv = _conv_taps(buf, phases, cw_ref, HALO - CONV_K + 1, tm) + cb_ref[...]
        xc = cv - jnp.mean(cv, axis=-1, keepdims=True)
        lo = xc * lax.rsqrt(jnp.mean(xc * xc, axis=-1, keepdims=True) + EPS) * ng_ref[...] + nb_ref[...]
        y_ref[:, 2 * WA:] = (lo * _sigmoid(lo)).astype(BF16)

    row = lambda w: pl.BlockSpec((1, w), lambda i: (0, 0))
    return pl.pallas_call(
        body, name="mix_fwd", grid=(T // tm,), out_shape=_sds((T, D), BF16),
        in_specs=[pl.BlockSpec((tm, WA), lambda i: (i, 0)), pl.BlockSpec((tm, WA), lambda i: (i, 0)),
                  pl.BlockSpec((tm, 2 * CC), lambda i: (i, 0)),
                  pl.BlockSpec((HALO, 2 * CC), lambda i: (jnp.maximum(i * (tm // HALO) - 1, 0), 0)),
                  row(WA), row(WA), pl.BlockSpec((32, CC), lambda i: (0, 0)), row(CC), row(CC), row(CC)],
        out_specs=pl.BlockSpec((tm, D), lambda i: (i, 0)),
        scratch_shapes=[pltpu.VMEM((tm + HALO, CC), F32), pltpu.VMEM((8, tm + HALO, CC), F32)],
        compiler_params=_cp("parallel"))(o_a, o_b, p_rest, p_rest, g_a, g_b, cw, cb, ng, nbias)


def _mm_res(a, w, resid, g, name):
    nk, T, kb = a.shape
    tm = min(512, T)

    def body(a_ref, w_ref, r_ref, g_ref, x_ref, h_ref):
        xv = r_ref[...]
        for k in range(nk):
            xv = xv + _dot(a_ref[k], w_ref[k])
        x_ref[...] = xv
        r = lax.rsqrt(jnp.mean(xv * xv, axis=-1, keepdims=True) + EPS)
        h_ref[...] = (xv * r * g_ref[...]).astype(BF16)

    return pl.pallas_call(
        body, name=name, grid=(T // tm,), out_shape=[_sds((T, D), F32), _sds((T, D), BF16)],
        in_specs=[pl.BlockSpec((nk, tm, kb), lambda i: (0, i, 0)), pl.BlockSpec((nk, kb, D), lambda i: (0, 0, 0)),
                  pl.BlockSpec((tm, D), lambda i: (i, 0)), pl.BlockSpec((1, D), lambda i: (0, 0))],
        out_specs=[pl.BlockSpec((tm, D), lambda i: (i, 0)), pl.BlockSpec((tm, D), lambda i: (i, 0))],
        compiler_params=_cp("parallel"))(a, w, resid, g)


def _up_fwd(h2, w_up, fcw, fcb, S):
    T = h2.shape[0]
    tm = min(512, S)
    nps = S // tm

    def body(h_ref, hh_ref, w_ref, cw_ref, cb_ref, u_ref, hid_ref, buf):
        i = pl.program_id(1)
        first = (i % nps) == 0
        hv = h_ref[...]
        hh = jnp.where(first, jnp.zeros_like(hh_ref[...]), hh_ref[...])
        c = []
        for half in range(2):
            w = w_ref[half, 0]
            u = _dot(hv, w)
            u_ref[half, 0] = u
            buf[half, 0:8, :] = _dot(hh, w)
            buf[half, 8:8 + tm, :] = u
            cw = cw_ref[half, 0]
            c.append(cb_ref[half, 0] + cw[0:1] * buf[half, pl.ds(6, tm), :] + cw[1:2] * buf[half, pl.ds(7, tm), :] + cw[2:3] * u)
        hid_ref[0] = (c[0] * _sigmoid(c[0]) * c[1]).astype(BF16)

    return pl.pallas_call(
        body, name="up_fwd", grid=(NJ, T // tm), out_shape=[_sds((2, NJ, T, FB), F32), _sds((NJ, T, FB), BF16)],
        in_specs=[pl.BlockSpec((tm, D), lambda j, i: (i, 0)),
                  pl.BlockSpec((8, D), lambda j, i: (jnp.maximum(i * (tm // 8) - 1, 0), 0)),
                  pl.BlockSpec((2, 1, D, FB), lambda j, i: (0, j, 0, 0)),
                  pl.BlockSpec((2, 1, 3, FB), lambda j, i: (0, j, 0, 0)),
                  pl.BlockSpec((2, 1, 1, FB), lambda j, i: (0, j, 0, 0))],
        out_specs=[pl.BlockSpec((2, 1, tm, FB), lambda j, i: (0, j, i, 0)), pl.BlockSpec((1, tm, FB), lambda j, i: (j, i, 0))],
        scratch_shapes=[pltpu.VMEM((2, tm + 8, FB), F32)],
        compiler_params=_cp("parallel", "parallel"))(h2, h2, w_up, fcw, fcb)


def _final_loss(x, tgt, g):
    T = x.shape[0]
    tm = min(512, T)

    def body(x_ref, t_ref, g_ref, dx_ref, loss_ref, dg_ref):
        @pl.when(pl.program_id(0) == 0)
        def _():
            loss_ref[...] = jnp.zeros_like(loss_ref)
            dg_ref[...] = jnp.zeros_like(dg_ref)

        xv, gv = x_ref[...], g_ref[...]
        r = lax.rsqrt(jnp.mean(xv * xv, axis=-1, keepdims=True) + EPS)
        e = xv * r * gv - t_ref[...]
        loss_ref[...] += 0.5 * jnp.sum(jnp.mean(e * e, axis=-1, keepdims=True), axis=0, keepdims=True)
        dx, dg = _rms_bwd(e * (1.0 / D), xv, gv)
        dx_ref[...] = dx
        dg_ref[...] += dg

    return pl.pallas_call(
        body, name="final_loss", grid=(T // tm,), out_shape=[_sds((T, D), F32), _sds((1, 128), F32), _sds((1, D), F32)],
        in_specs=[pl.BlockSpec((tm, D), lambda i: (i, 0)), pl.BlockSpec((tm, D), lambda i: (i, 0)), pl.BlockSpec((1, D), lambda i: (0, 0))],
        out_specs=[pl.BlockSpec((tm, D), lambda i: (i, 0)), pl.BlockSpec((1, 128), lambda i: (0, 0)), pl.BlockSpec((1, D), lambda i: (0, 0))],
        compiler_params=_cp("arbitrary"))(x, tgt, g)


def _down_bwd(dx2, w_down, u, fcw, fcb, S):
    T = dx2.shape[0]
    tm = min(512, S)
    nps = S // tm
    nblk8 = T // 8

    def body(dx_ref, dxn_ref, wd_ref, ut_ref, up_ref, un_ref, cw_ref, cb_ref, du_ref, dcw_ref, dxb, ubuf, dcbuf, ush):
        i = pl.program_id(1)
        first, last = (i % nps) == 0, (i % nps) == nps - 1

        @pl.when(i == 0)
        def _():
            dcw_ref[...] = jnp.zeros_like(dcw_ref)

        dxb[0:tm, :] = dx_ref[...].astype(BF16)
        dxb[tm:tm + 8, :] = jnp.where(last, 0.0, dxn_ref[...]).astype(BF16)
        dh = _dot_nt(dxb[...], wd_ref[0])
        ce = []
        for half in range(2):
            ubuf[half, 0:8, :] = jnp.where(first, 0.0, up_ref[half, 0])
            ubuf[half, 8:8 + tm, :] = ut_ref[half, 0]
            ubuf[half, 8 + tm:16 + tm, :] = un_ref[half, 0]
            cw = cw_ref[half, 0]
            for k in range(2):
                ush[half, k] = ubuf[half, pl.ds(6 + k, tm + 8), :]
            ce.append(cb_ref[half, 0] + cw[0:1] * ush[half, 0] + cw[1:2] * ush[half, 1] + cw[2:3] * ubuf[half, pl.ds(8, tm + 8), :])
        sg = _sigmoid(ce[0])
        dc = [dh * ce[1] * (sg * (1.0 + ce[0] * (1.0 - sg))), dh * (ce[0] * sg)]
        for half in range(2):
            cw = cw_ref[half, 0]
            dcbuf[...] = dc[half]
            dct = dc[half][0:tm]
            du_ref[half, 0] = (cw[2:3] * dct + cw[1:2] * dcbuf[pl.ds(1, tm), :] + cw[0:1] * dcbuf[pl.ds(2, tm), :]).astype(BF16)
            for k in range(3):
                u_back = ush[half, k, 0:tm, :] if k < 2 else ubuf[half, pl.ds(8, tm), :]
                dcw_ref[half, 0, k:k + 1, :] += jnp.sum(dct * u_back, axis=0, keepdims=True)
            dcw_ref[half, 0, 3:4, :] += jnp.sum(dct, axis=0, keepdims=True)

    ublk = lambda rows, imap: pl.BlockSpec((2, 1, rows, FB), imap)
    return pl.pallas_call(
        body, name="down_bwd", grid=(NJ, T // tm), out_shape=[_sds((2, NJ, T, FB), BF16), _sds((2, NJ, 8, FB), F32)],
        in_specs=[pl.BlockSpec((tm, D), lambda j, i: (i, 0)),
                  pl.BlockSpec((8, D), lambda j, i: (jnp.minimum((i + 1) * (tm // 8), nblk8 - 1), 0)),
                  pl.BlockSpec((1, FB, D), lambda j, i: (j, 0, 0)),
                  ublk(tm, lambda j, i: (0, j, i, 0)),
                  ublk(8, lambda j, i: (0, j, jnp.maximum(i * (tm // 8) - 1, 0), 0)),
                  ublk(8, lambda j, i: (0, j, jnp.minimum((i + 1) * (tm // 8), nblk8 - 1), 0)),
                  pl.BlockSpec((2, 1, 3, FB), lambda j, i: (0, j, 0, 0)),
                  pl.BlockSpec((2, 1, 1, FB), lambda j, i: (0, j, 0, 0))],
        out_specs=[ublk(tm, lambda j, i: (0, j, i, 0)), pl.BlockSpec((2, 1, 8, FB), lambda j, i: (0, j, 0, 0))],
        scratch_shapes=[pltpu.VMEM((tm + 8, D), BF16), pltpu.VMEM((2, tm + 16, FB), F32), pltpu.VMEM((tm + 8, FB), F32),
                        pltpu.VMEM((2, 2, tm + 8, FB), F32)],
        compiler_params=_cp("parallel", "arbitrary"))(dx2, dx2, w_down, u, u, u, fcw, fcb)


def _wgrad(a, b, tn, name):
    na, T, ka = a.shape
    nb, _, kb = b.shape
    tk = min(1024, T)
    nt = T // tk

    def body(a_ref, b_ref, o_ref, acc):
        t = pl.program_id(3)

        @pl.when(t == 0)
        def _():
            acc[...] = jnp.zeros_like(acc)

        acc[...] += _dot_tn(a_ref[0], b_ref[0].astype(BF16))

        @pl.when(t == nt - 1)
        def _():
            o_ref[0, 0] = acc[...].astype(GRAD_DTYPE)

    return pl.pallas_call(
        body, name=name, grid=(na, nb, kb // tn, nt), out_shape=_sds((na, nb, ka, kb), GRAD_DTYPE),
        in_specs=[pl.BlockSpec((1, tk, ka), lambda ia, ib, n, t: (ia, t, 0)), pl.BlockSpec((1, tk, tn), lambda ia, ib, n, t: (ib, t, n))],
        out_specs=pl.BlockSpec((1, 1, ka, tn), lambda ia, ib, n, t: (ia, ib, 0, n)),
        scratch_shapes=[pltpu.VMEM((ka, tn), F32)],
        compiler_params=_cp("parallel", "parallel", "parallel", "arbitrary"))(a, b)


def _dx_rms(dy, w, x, g, dres, blocked, name):
    T = x.shape[0]
    tm = min(256 if blocked else 512, T)

    def body(dy_ref, w_ref, x_ref, g_ref, r_ref, dx_ref, dg_ref):
        @pl.when(pl.program_id(0) == 0)
        def _():
            dg_ref[...] = jnp.zeros_like(dg_ref)

        if blocked:
            dh = _dot_nt(dy_ref[0, 0], w_ref[0, 0])
            for half, k in [(h, k) for k in range(NJ) for h in range(2)][1:]:
                dh = dh + _dot_nt(dy_ref[half, k], w_ref[half, k])
        else:
            dh = _dot_nt(dy_ref[...], w_ref[...])
        dx, dg = _rms_bwd(dh, x_ref[...], g_ref[...])
        dx_ref[...] = r_ref[...] + dx
        dg_ref[...] += dg

    if blocked:
        dy_spec = pl.BlockSpec((2, NJ, tm, FB), lambda i: (0, 0, i, 0))
        w_spec = pl.BlockSpec((2, NJ, D, FB), lambda i: (0, 0, 0, 0))
    else:
        dy_spec = pl.BlockSpec((tm, dy.shape[1]), lambda i: (i, 0))
        w_spec = pl.BlockSpec(w.shape, lambda i: (0, 0))
    tile = pl.BlockSpec((tm, D), lambda i: (i, 0))
    return pl.pallas_call(
        body, name=name, grid=(T // tm,), out_shape=[_sds((T, D), F32), _sds((1, D), F32)],
        in_specs=[dy_spec, w_spec, tile, pl.BlockSpec((1, D), lambda i: (0, 0)), tile],
        out_specs=[tile, pl.BlockSpec((1, D), lambda i: (0, 0))],
        compiler_params=pltpu.CompilerParams(dimension_semantics=("arbitrary",), vmem_limit_bytes=VMEM_LIMIT_BIG))(dy, w, x, g, dres)


def _mm_nt(a, w, name):
    T = a.shape[0]
    tm = min(512, T)

    def body(a_ref, w_ref, o_ref):
        o_ref[...] = _dot_nt(a_ref[...].astype(BF16), w_ref[...])

    return pl.pallas_call(
        body, name=name, grid=(T // tm,), out_shape=_sds((T, D), F32),
        in_specs=[pl.BlockSpec((tm, D), lambda i: (i, 0)), pl.BlockSpec((D, D), lambda i: (0, 0))],
        out_specs=pl.BlockSpec((tm, D), lambda i: (i, 0)), compiler_params=_cp("parallel"))(a, w)


def _mix_bwd(dy, o_a, o_b, p_rest, g_a, g_b, cw, cb, ng, nbias, S):
    T = dy.shape[0]
    tm = min(256, S)
    nps = S // tm
    HALO = 32
    nblkh = T // HALO
    RE = tm + HALO

    def body(dy_ref, dyn_ref, oa_ref, ob_ref, pt_ref, pp_ref, pn_ref, ga_ref, gb_ref, cw_ref, cb_ref, ng_ref, nb_ref,
             doa_ref, dda_ref, dob_ref, ddb_ref, dg_ref, dgn_ref, dcw_ref, dcn_ref, gbuf, dvbuf, gph, dph):
        i = pl.program_id(0)
        first, last = (i % nps) == 0, (i % nps) == nps - 1

        @pl.when(i == 0)
        def _():
            dgn_ref[...] = jnp.zeros_like(dgn_ref)
            dcw_ref[...] = jnp.zeros_like(dcw_ref)
            dcn_ref[...] = jnp.zeros_like(dcn_ref)

        head_of = (lax.broadcasted_iota(jnp.int32, (8, WA), 1) // HD == lax.broadcasted_iota(jnp.int32, (8, WA), 0)).astype(F32)
        for n, (o_ref, g_ref, do_ref, dd_ref) in enumerate(((oa_ref, ga_ref, doa_ref, dda_ref), (ob_ref, gb_ref, dob_ref, ddb_ref))):
            o = o_ref[...]
            do, dg = _rms_bwd(dy_ref[:, n * WA:(n + 1) * WA], o, g_ref[...])
            dob = do.astype(BF16)
            do_ref[...] = dob
            ddt = _dot_nt(head_of, dob.astype(F32) * o, HIGHEST)
            for hd in range(8):
                dd_ref[0, hd] = ddt[hd:hd + 1, :]
            dgn_ref[n:n + 1, :] += dg

        pp = jnp.where(first, 0.0, pp_ref[...])
        gbuf[0:HALO, :] = pp[:, :CC] * _sigmoid(pp[:, CC:])
        pt = pt_ref[...]
        gv, sgg = pt[:, :CC], _sigmoid(pt[:, CC:])
        gbuf[HALO:HALO + tm, :] = gv * sgg
        pn = pn_ref[...]
        gbuf[HALO + tm:2 * HALO + tm, :] = pn[:, :CC] * _sigmoid(pn[:, CC:])
        _phase_copies(gbuf, gph, tm + 2 * HALO)
        cv = _conv_taps(gbuf, gph, cw_ref, HALO - CONV_K + 1, RE) + cb_ref[...]
        xc = cv - jnp.mean(cv, axis=-1, keepdims=True)
        r = lax.rsqrt(jnp.mean(xc * xc, axis=-1, keepdims=True) + EPS)
        xh = xc * r
        lo = xh * ng_ref[...] + nb_ref[...]
        sg = _sigmoid(lo)
        dyc = jnp.concatenate([dy_ref[:, 2 * WA:], jnp.where(last, 0.0, dyn_ref[...])], axis=0)
        dlo = dyc * (sg * (1.0 + lo * (1.0 - sg)))
        dcn_ref[0:1, :] += jnp.sum(dlo[0:tm] * xh[0:tm], axis=0, keepdims=True)
        dcn_ref[1:2, :] += jnp.sum(dlo[0:tm], axis=0, keepdims=True)
        dxh = dlo * ng_ref[...]
        dcv = r * (dxh - jnp.mean(dxh, axis=-1, keepdims=True) - xh * jnp.mean(dxh * xh, axis=-1, keepdims=True))
        dvbuf[...] = dcv
        _phase_copies(dvbuf, dph, RE)
        dct = dcv[0:tm]
        dcw_ref[CONV_K:CONV_K + 1, :] += jnp.sum(dct, axis=0, keepdims=True)
        dglu = jnp.zeros((tm, CC), F32)
        for k in range(CONV_K):
            dcw_ref[k:k + 1, :] += jnp.sum(dct * _tap(gbuf, gph, HALO - CONV_K + 1 + k, tm), axis=0, keepdims=True)
            dglu = dglu + cw_ref[k:k + 1, :] * _tap(dvbuf, dph, CONV_K - 1 - k, tm)
        dg_ref[:, :CC] = (dglu * sgg).astype(BF16)
        dg_ref[:, CC:] = (dglu * gv * sgg * (1.0 - sgg)).astype(BF16)

    row = lambda w: pl.BlockSpec((1, w), lambda i: (0, 0))
    tile = lambda w: pl.BlockSpec((tm, w), lambda i: (i, 0))
    nxt = lambda i: jnp.minimum((i + 1) * (tm // HALO), nblkh - 1)
    const = lambda r, w: pl.BlockSpec((r, w), lambda i: (0, 0))
    ddrow = pl.BlockSpec((1, 8, 1, tm), lambda i: (i, 0, 0, 0))
    return pl.pallas_call(
        body, name="mix_bwd", grid=(T // tm,),
        out_shape=[_sds((T, WA), BF16), _sds((T // tm, 8, 1, tm), F32), _sds((T, WA), BF16), _sds((T // tm, 8, 1, tm), F32),
                   _sds((T, 2 * CC), BF16), _sds((8, WA), F32), _sds((32, CC), F32), _sds((8, CC), F32)],
        in_specs=[tile(D), pl.BlockSpec((HALO, CC), lambda i: (nxt(i), 3)), tile(WA), tile(WA), tile(2 * CC),
                  pl.BlockSpec((HALO, 2 * CC), lambda i: (jnp.maximum(i * (tm // HALO) - 1, 0), 0)),
                  pl.BlockSpec((HALO, 2 * CC), lambda i: (nxt(i), 0)),
                  row(WA), row(WA), const(32, CC), row(CC), row(CC), row(CC)],
        out_specs=[tile(WA), ddrow, tile(WA), ddrow, tile(2 * CC), const(8, WA), const(32, CC), const(8, CC)],
        scratch_shapes=[pltpu.VMEM((tm + 2 * HALO, CC), F32), pltpu.VMEM((RE, CC), F32),
                        pltpu.VMEM((8, tm + 2 * HALO, CC), F32), pltpu.VMEM((8, RE, CC), F32)],
        compiler_params=_cp("arbitrary"))(dy, dy, o_a, o_b, p_rest, p_rest, p_rest, g_a, g_b, cw, cb, ng, nbias)


def _attn_bwd(p_qkv, col0, tab, do, lse, dd, kat, S, aq=None, ak=None, name="attn_bwd"):
    T = p_qkv.shape[0]
    nb, TB = tab.shape[0] - 1, tab.shape[1]
    B = T // S
    fox = aq is not None

    def body(*refs):
        if fox:
            (q_ref, k_ref, v_ref, tab_ref, do_ref, l_ref, dd_ref, kat_ref, aq_ref, ak_ref,
             dq_ref, dk_ref, dv_ref, dqa_ref, dka_ref, dqt_scr, dk_scr, dv_scr, sd_scr) = refs
        else:
            (q_ref, k_ref, v_ref, tab_ref, do_ref, l_ref, dd_ref, kat_ref,
             dq_ref, dk_ref, dv_ref, dqt_scr, dk_scr, dv_scr, sd_scr) = refs
        j = pl.program_id(2)
        hms = _head_masks()

        @pl.when(j == 0)
        def _():
            dqt_scr[...] = jnp.zeros_like(dqt_scr)

        dk_scr[...] = jnp.zeros_like(dk_scr)
        dv_scr[...] = jnp.zeros_like(dv_scr)
        k2, v2 = k_ref[...], v_ref[...]
        ka = [jnp.where(hm, k2, ak_ref[0] if fox else jnp.zeros_like(k2)) for hm in hms]
        kat = [kat_ref[0, 0, hh, 0] for hh in range(2)]

        def operands(i):
            ic = jnp.minimum(i, nb - 1)
            rows = pl.ds(pl.multiple_of(ic * TB, TB), TB)
            q2 = q_ref[rows, :] * 0.125
            do2 = do_ref[rows, :]
            qas = [jnp.where(hms[hh], q2, aq_ref[0, rows, :] if fox else jnp.zeros_like(q2)) for hh in range(2)]
            dohs = [jnp.where(hms[hh], do2, jnp.zeros_like(do2)) for hh in range(2)]
            return ic, qas, dohs

        def scores(i):
            ic, qas, dohs = operands(i)
            bias = tab_ref[jnp.where(i < nb, i - j, nb)]
            return [(_dot_nt(ka[hh], qas[hh]) + bias, _dot_nt(v2, dohs[hh])) for hh in range(2)]

        def absorb(i, sd):
            ic, qas, dohs = operands(i)
            for hh in range(2):
                st, dpt = sd[hh]
                pt = jnp.exp(st - l_ref[ic, hh])
                dsb = (pt * (dpt - dd_ref[ic, hh])).astype(BF16)
                dv_scr[...] += _dot(pt.astype(BF16), dohs[hh])
                dk_scr[hh] += _dot(dsb, qas[hh])
                dqt_scr[hh, ic] += _dot(kat[hh], dsb)

        def q_pair(ii, carry):
            i0 = j + 2 * ii
            sd0 = [(sd_scr[hh, 0], sd_scr[hh, 1]) for hh in range(2)]
            sd1 = scores(i0 + 1)
            absorb(i0, sd0)
            nxt = scores(i0 + 2)
            for hh in range(2):
                sd_scr[hh, 0], sd_scr[hh, 1] = nxt[hh]
            absorb(i0 + 1, sd1)
            return carry

        first = scores(j)
        for hh in range(2):
            sd_scr[hh, 0], sd_scr[hh, 1] = first[hh]
        lax.fori_loop(0, (nb - j + 1) // 2, q_pair, 0)
        dk_ref[...] = jnp.where(hms[0], dk_scr[0], dk_scr[1]).astype(BF16)
        dv_ref[...] = dv_scr[...].astype(BF16)
        if fox:
            dka_ref[...] = jnp.where(hms[0], dk_scr[1], dk_scr[0])

        @pl.when(j == nb - 1)
        def _():
            for ii in range(nb):
                t0, t1 = dqt_scr[0, ii].T, dqt_scr[1, ii].T
                dq_ref[ii * TB:(ii + 1) * TB, :] = jnp.where(hms[0], t0, t1) * 0.125
                if fox:
                    dqa_ref[ii * TB:(ii + 1) * TB, :] = jnp.where(hms[0], t1, t0)

    seq = lambda c: pl.BlockSpec((S, 128), lambda b, p, j: (b, c + p))
    kvb = lambda c: pl.BlockSpec((TB, 128), lambda b, p, j: (b * nb + j, c + p))
    stat = pl.BlockSpec((nb, 2, 1, TB), lambda b, p, j: (b, p, 0, 0))
    in_specs = [seq(col0), kvb(col0 + NPAIR), kvb(col0 + 2 * NPAIR), pl.BlockSpec((nb + 1, TB, TB), lambda b, p, j: (0, 0, 0)),
                seq(0), stat, stat, pl.BlockSpec((1, 1, 2, 1, 128, TB), lambda b, p, j: (b, p, 0, j, 0, 0))]
    args = [p_qkv, p_qkv, p_qkv, tab, do, lse, dd, kat]
    out_shape = [_sds((T, WA), F32), _sds((T, WA), BF16), _sds((T, WA), BF16)]
    out_specs = [seq(0), kvb(0), kvb(0)]
    if fox:
        in_specs += [pl.BlockSpec((1, S, 128), lambda b, p, j: (p, b, 0)), pl.BlockSpec((1, TB, 128), lambda b, p, j: (p, b * nb + j, 0))]
        args += [aq, ak]
        out_shape += [_sds((T, WA), F32), _sds((T, WA), F32)]
        out_specs += [seq(0), kvb(0)]
    return pl.pallas_call(
        body, name=name, grid=(B, NPAIR, nb), out_shape=out_shape, in_specs=in_specs, out_specs=out_specs,
        scratch_shapes=[pltpu.VMEM((2, nb, 128, TB), F32), pltpu.VMEM((2, TB, 128), F32), pltpu.VMEM((TB, 128), F32),
                        pltpu.VMEM((2, 2, TB, TB), F32)],
        compiler_params=_cp("parallel", "parallel", "arbitrary"))(*args)


def _gate_bwd(dqa, dka, p_rest, bf, S, TB):
    T = p_rest.shape[0]
    B, nb = T // S, S // TB

    def body(dqa_ref, dka_ref, z_ref, b_ref, dz_ref, db_ref):
        @pl.when(pl.program_id(0) == 0)
        def _():
            db_ref[...] = jnp.zeros_like(db_ref)

        later = (lax.broadcasted_iota(jnp.int32, (TB, TB), 1) >= lax.broadcasted_iota(jnp.int32, (TB, TB), 0)).astype(F32)
        lane = lax.broadcasted_iota(jnp.int32, (1, 128), 1)
        src, head = lax.broadcasted_iota(jnp.int32, (WA, 128), 0), lax.broadcasted_iota(jnp.int32, (WA, 128), 1)
        spare0 = 128 * (head // 2) + HD * (1 - head % 2)
        pick_q = (src == spare0 + AUG_K1).astype(F32)
        pick_k = (src == spare0 + AUG_Q1).astype(F32)
        carry = jnp.zeros((1, 128), F32)
        dbs = jnp.zeros((1, 128), F32)
        for j in reversed(range(nb)):
            rows = slice(j * TB, (j + 1) * TB)
            dc = _dot(dqa_ref[rows, :], pick_q, HIGHEST) - _dot(dka_ref[rows, :], pick_k, HIGHEST)
            part = _dot(later, dc, HIGHEST)
            tot = part + carry
            carry = carry + part[0:1, :]
            z = z_ref[j * TB:(j + 1) * TB, :] + b_ref[...]
            dz = jnp.where(lane < 6, tot * _sigmoid(-z), 0.0)
            dz_ref[j * TB:(j + 1) * TB, :] = dz.astype(BF16)
            dbs = dbs + jnp.sum(dz, axis=0, keepdims=True)
        db_ref[...] += dbs

    return pl.pallas_call(
        body, name="gate_bwd", grid=(B,), out_shape=[_sds((T, 128), BF16), _sds((1, 128), F32)],
        in_specs=[pl.BlockSpec((S, WA), lambda b: (b, 0)), pl.BlockSpec((S, WA), lambda b: (b, 0)),
                  pl.BlockSpec((S, 128), lambda b: (b, 4)), pl.BlockSpec((1, 128), lambda b: (0, 0))],
        out_specs=[pl.BlockSpec((S, 128), lambda b: (b, 0)), pl.BlockSpec((1, 128), lambda b: (0, 0))],
        compiler_params=_cp("arbitrary"))(dqa, dka, p_rest, bf)


def _adamw(parts, w, m, v, name, row0=0, prev=None):
    npart, Rp, C = parts.shape
    R = w.shape[0]
    tr = Rp
    for cand in (256, 128, 64, 32, 16, 8):
        if Rp % cand == 0 and row0 % cand == 0 and cand * C * 4 <= (1 << 20):
            tr = cand
            break
    if Rp == R and R * C * 4 <= (1 << 20):
        tr = R
    assert Rp % tr == 0 and row0 % tr == 0, (name, Rp, row0, tr)
    b0 = row0 // tr

    def body(p_ref, w_ref, m_ref, v_ref, *rest):
        g_ref, d_ref, mo_ref, vo_ref = rest[-4:]
        g = p_ref[0].astype(F32)
        for k in range(1, npart):
            g = g + p_ref[k].astype(F32)
        mn = ADAM_B1 * m_ref[...] + (1.0 - ADAM_B1) * g
        vn = ADAM_B2 * v_ref[...] + (1.0 - ADAM_B2) * (g * g)
        m_hat = mn / (1.0 - ADAM_B1 ** ADAM_STEP)
        v_hat = vn / (1.0 - ADAM_B2 ** ADAM_STEP)
        g_ref[...] = g
        d_ref[...] = -ADAM_LR * (m_hat / (jnp.sqrt(v_hat) + ADAM_EPS) + ADAM_WD * w_ref[...])
        mo_ref[...] = mn
        vo_ref[...] = vn

    blk = pl.BlockSpec((tr, C), lambda i: (b0 + i, 0))
    prev = list(prev) if prev is not None else []
    return pl.pallas_call(
        body, name=name, grid=(Rp // tr,), out_shape=[_sds((R, C), F32)] * 4,
        in_specs=[pl.BlockSpec((npart, tr, C), lambda i: (0, i, 0)), blk, blk, blk] + [_ANY] * len(prev), out_specs=[blk] * 4,
        input_output_aliases={4 + i: i for i in range(len(prev))},
        compiler_params=_cp("parallel"))(parts, w, m, v, *prev)


def _pad_w_in(w):
    z = jnp.zeros(w.shape[:-1] + (NPAD - N_IN,), w.dtype)
    return jnp.concatenate([w[..., :O_FA], w[..., O_QB:], w[..., O_FA:O_QB], z], axis=-1)


def _unpad_w_in(w):
    n_mid = N_IN - O_QB
    return jnp.concatenate([w[..., :O_FA], w[..., O_FA + n_mid:O_FA + n_mid + 6], w[..., O_FA:O_FA + n_mid]], axis=-1)


def _local_step(x, tgt, W, S, late_weights=None, on_layer_grads=None, on_ffn_grads=None):
    T = x.shape[0]
    TB = min(256, S)
    tab_fox, tab_dil = _bias_table(S, TB, True), _bias_table(S, TB, False)
    saved = []
    h = _rms_fwd(x, W["ln1_g"][0])
    for l in range(DEPTH):
        p_qkv, p_rest = _mm_in(h, W["w_in"][l])
        aq, ak = _gate_fwd(p_rest, W["b_forget"][l], S, TB)
        o_a, lse_a, kat_a = _attn_fwd(p_qkv, 0, tab_fox, S, aq, ak, name="fox_fwd")
        o_b, lse_b, kat_b = _attn_fwd(p_qkv, 3 * NPAIR, tab_dil, S, name="dil_fwd")
        y = _mix_fwd(o_a, o_b, p_rest, W["g_out_fox"][l], W["g_out_dil"][l], W["conv_w"][l], W["conv_b"][l],
                     W["cnorm_g"][l], W["cnorm_b"][l], S)
        if l == 0 and late_weights is not None:
            W = late_weights(W, y)
        x1, h2 = _mm_res(y[None], W["w_o"][l][None], x, W["ln2_g"][l], name="mm_o")
        u, hid = _up_fwd(h2, W["w_up"][l], W["ffn_conv_w"][l], W["ffn_conv_b"][l], S)
        g_next = W["ln1_g"][l + 1] if l + 1 < DEPTH else W["g_final"]
        x2, h_next = _mm_res(hid, W["w_down"][l], x1, g_next, name="mm_down")
        saved.append(dict(x=x, h=h, p_qkv=p_qkv, p_rest=p_rest, aq=aq, ak=ak, o_a=o_a, lse_a=lse_a, o_b=o_b,
                          lse_b=lse_b, kat_a=kat_a, kat_b=kat_b, y=y, x1=x1, h2=h2, u=u, hid=hid))
        x, h = x2, h_next
    dx, loss, dg_final = _final_loss(x, tgt, W["g_final"])
    G = {k: [None] * DEPTH for k in ("ln1_g", "w_in", "b_forget", "g_out_fox", "g_out_dil", "conv_w", "conv_b", "cnorm_g",
                                     "cnorm_b", "w_o", "ln2_g", "w_up", "ffn_conv_w", "ffn_conv_b", "w_down")}
    G["g_final"] = dg_final
    for l in reversed(range(DEPTH)):
        sv = saved[l]
        du, dcw = _down_bwd(dx, W["w_down"][l], sv["u"], W["ffn_conv_w"][l], W["ffn_conv_b"][l], S)
        G["w_down"][l] = _wgrad(sv["hid"], dx[None], D, "wg_down")[:, 0]
        G["ffn_conv_w"][l] = dcw[:, :, 0:3, :]
        G["ffn_conv_b"][l] = dcw[:, :, 3, :]
        G["w_up"][l] = _wgrad(sv["h2"][None], du.reshape(2 * NJ, T, FB), FB, "wg_up")[0]
        token = on_ffn_grads(l, G) if on_ffn_grads is not None else None
        ln2 = W["ln2_g"][l] if token is None else W["ln2_g"][l] + token[0, 0]
        dx1, G["ln2_g"][l] = _dx_rms(du, W["w_up"][l], sv["x1"], ln2, dx, True, "dx_up")
        G["w_o"][l] = _wgrad(sv["y"][None], dx1[None], D, "wg_o")[0, 0]
        dy = _mm_nt(dx1, W["w_o"][l], "mm_dy")
        do_a, dd_a, do_b, dd_b, dgvgg, dgn, dcvw, dcn = _mix_bwd(
            dy, sv["o_a"], sv["o_b"], sv["p_rest"], W["g_out_fox"][l], W["g_out_dil"][l], W["conv_w"][l], W["conv_b"][l],
            W["cnorm_g"][l], W["cnorm_b"][l], S)
        G["g_out_fox"][l], G["g_out_dil"][l] = dgn[0], dgn[1]
        G["conv_w"][l], G["conv_b"][l] = dcvw[:CONV_K], dcvw[CONV_K]
        G["cnorm_g"][l], G["cnorm_b"][l] = dcn[0], dcn[1]
        dq_a, dk_a, dv_a, dqa, dka = _attn_bwd(sv["p_qkv"], 0, tab_fox, do_a, sv["lse_a"], dd_a, sv["kat_a"], S, sv["aq"], sv["ak"],
                                               name="fox_bwd")
        dq_b, dk_b, dv_b = _attn_bwd(sv["p_qkv"], 3 * NPAIR, tab_dil, do_b, sv["lse_b"], dd_b, sv["kat_b"], S, name="dil_bwd")
        dfa, db = _gate_bwd(dqa, dka, sv["p_rest"], W["b_forget"][l], S, TB)
        G["b_forget"][l] = db[0, :6]
        dp = jnp.concatenate([dq_a.astype(BF16), dk_a, dv_a, dq_b.astype(BF16), dk_b, dv_b, dgvgg, dfa,
                              jnp.zeros((T, 128), BF16)], axis=1)
        G["w_in"][l] = _wgrad(sv["h"][None], dp[None], D, "wg_in")[0, 0]
        dx, G["ln1_g"][l] = _dx_rms(dp, W["w_in"][l], sv["x"], W["ln1_g"][l], dx1, False, "dx_in")
        token = on_layer_grads(l, G) if on_layer_grads is not None else None
        if token is not None and l > 0:
            W = dict(W, ffn_conv_b=[b + token[0, 0] for b in W["ffn_conv_b"]])
    return loss, dx, G


_MATMUL_W = ("w_in", "w_o", "w_up", "w_down")
_SHARDED = _MATMUL_W + ("conv_w", "ffn_conv_w")
_SMALL = ("ln1_g", "b_forget", "g_out_fox", "g_out_dil", "conv_b", "cnorm_g", "cnorm_b", "ln2_g", "ffn_conv_b", "g_final")
_ORDER = ("ln1_g", "w_in", "b_forget", "g_out_fox", "g_out_dil", "conv_w", "conv_b", "cnorm_g", "cnorm_b", "w_o", "ln2_g",
          "w_up", "ffn_conv_w", "ffn_conv_b", "w_down", "g_final")


def _kernel_ready(k, zone):
    if k == "w_in":
        return _pad_w_in(zone.reshape(D, N_IN))
    if k == "w_o":
        return zone.reshape(D, D)
    if k == "w_up":
        return zone.reshape(2, NJ, D, FB)
    if k == "w_down":
        return zone.reshape(NJ, FB, D)
    if k == "conv_w":
        cw = jnp.transpose(zone.reshape(NDEV, CONV_K, CC // NDEV), (1, 0, 2)).reshape(CONV_K, CC)
        return jnp.concatenate([cw, jnp.zeros((1, CC), F32)], axis=0)
    return zone.reshape(2, NJ, 3, FB)


def _full_weights(P, gathered):
    W = {}
    row = lambda a: [a[l][None, :] for l in range(DEPTH)]
    for k in _SHARDED:
        W[k] = [None if gathered[k][l] is None else _kernel_ready(k, gathered[k][l]) for l in range(DEPTH)]
    W["ffn_conv_b"] = [P["ffn_conv_b"][l].reshape(2, NJ, 1, FB) for l in range(DEPTH)]
    W["b_forget"] = [jnp.concatenate([P["b_forget"][l], jnp.zeros((128 - 6,), F32)])[None, :] for l in range(DEPTH)]
    for k in ("ln1_g", "ln2_g", "g_out_fox", "g_out_dil", "conv_b", "cnorm_g", "cnorm_b"):
        W[k] = row(P[k])
    W["g_final"] = P["g_final"][None, :]
    return W


def kernel(x, ln1_g, w_in, b_forget, g_out_fox, g_out_dil, conv_w, conv_b, cnorm_g, cnorm_b, w_o, ln2_g, w_up, ffn_conv_w, ffn_conv_b, w_down, g_final, loss_target, m_ln1_g, m_w_in, m_b_forget, m_g_out_fox, m_g_out_dil, m_conv_w, m_conv_b, m_cnorm_g, m_cnorm_b, m_w_o, m_ln2_g, m_w_up, m_ffn_conv_w, m_ffn_conv_b, m_w_down, m_g_final, v_ln1_g, v_w_in, v_b_forget, v_g_out_fox, v_g_out_dil, v_conv_w, v_conv_b, v_cnorm_g, v_cnorm_b, v_w_o, v_ln2_g, v_w_up, v_ffn_conv_w, v_ffn_conv_b, v_w_down, v_g_final):
    P = dict(ln1_g=ln1_g, w_in=w_in, b_forget=b_forget, g_out_fox=g_out_fox, g_out_dil=g_out_dil, conv_w=conv_w, conv_b=conv_b,
             cnorm_g=cnorm_g, cnorm_b=cnorm_b, w_o=w_o, ln2_g=ln2_g, w_up=w_up, ffn_conv_w=ffn_conv_w, ffn_conv_b=ffn_conv_b,
             w_down=w_down, g_final=g_final)
    M = dict(ln1_g=m_ln1_g, w_in=m_w_in, b_forget=m_b_forget, g_out_fox=m_g_out_fox, g_out_dil=m_g_out_dil, conv_w=m_conv_w,
             conv_b=m_conv_b, cnorm_g=m_cnorm_g, cnorm_b=m_cnorm_b, w_o=m_w_o, ln2_g=m_ln2_g, w_up=m_w_up, ffn_conv_w=m_ffn_conv_w,
             ffn_conv_b=m_ffn_conv_b, w_down=m_w_down, g_final=m_g_final)
    V = dict(ln1_g=v_ln1_g, w_in=v_w_in, b_forget=v_b_forget, g_out_fox=v_g_out_fox, g_out_dil=v_g_out_dil, conv_w=v_conv_w,
             conv_b=v_conv_b, cnorm_g=v_cnorm_g, cnorm_b=v_cnorm_b, w_o=v_w_o, ln2_g=v_ln2_g, w_up=v_w_up, ffn_conv_w=v_ffn_conv_w,
             ffn_conv_b=v_ffn_conv_b, w_down=v_w_down, g_final=v_g_final)
    Bl, S, _ = x.shape
    T = Bl * S

    shard = lambda k, l: P[k][l].astype(BF16) if k in _MATMUL_W else P[k][l]
    early = [("w_in", 0)] + [(k, l) for k in ("conv_w", "ffn_conv_w") for l in range(DEPTH)]
    late = [(k, 0) for k in ("w_o", "w_up", "w_down")] + [(k, 1) for k in _MATMUL_W]
    gathered = {k: [None] * DEPTH for k in _SHARDED}
    plan_early = _Plan([[shard(k, l)] for k, l in early], True)
    for (k, l), zone in zip(early, _exchange(plan_early, "gather_early")):
        gathered[k][l] = zone
    plan_late = _Plan([[shard(k, l)] for k, l in late], True)
    late_handle, late_token = _exchange_start(plan_late, "gather_late_start", plan_late.zones_with_own())
    W = _full_weights(P, gathered)
    W["ln1_g"][0] = W["ln1_g"][0] + late_token[0, 0]

    def late_weights(W, after):
        zones = _exchange_wait(plan_late, "gather_late_wait", late_handle, after)
        W = dict(W)
        for (k, l), zone in zip(late, zones):
            W[k] = list(W[k])
            W[k][l] = _kernel_ready(k, zone)
        return W

    def owner_block(G, l, k):
        if k == "w_in":
            blk = _unpad_w_in(G[k][l]).reshape(NDEV, D // NDEV, N_IN)
        elif k == "w_o":
            blk = G[k][l].reshape(NDEV, D // NDEV, D)
        elif k == "w_up":
            blk = G[k][l]
        elif k == "w_down":
            blk = G[k][l].reshape(NDEV, DFF // NDEV, D)
        elif k == "conv_w":
            blk = jnp.transpose(G[k][l].reshape(CONV_K, NDEV, CC // NDEV), (1, 0, 2))
        else:
            blk = G[k][l].reshape(NDEV, 3, FB)
        return blk.astype(GRAD_DTYPE)

    flying = []

    def start_scatter(tag, G, l, keys):
        plan = _Plan([[owner_block(G, l, k)] for k in keys], False)
        handle, token = _exchange_start(plan, "scatter_%s_start" % tag, plan.zones_with_own())
        flying.append((tag, l, keys, plan, handle))
        return token

    def on_layer_grads(l, G):
        return start_scatter("l1", G, l, _MATMUL_W) if l == 1 else None

    def on_ffn_grads(l, G):
        return start_scatter("l0_ffn", G, l, ("w_up", "w_down")) if l == 0 else None

    loss, dx, G = _local_step(x.reshape(T, D), loss_target.reshape(T, D), W, S, late_weights, on_layer_grads, on_ffn_grads)

    parts = {}
    for tag, l, keys, plan, handle in flying:
        for k, zone in zip(keys, _exchange_wait(plan, "scatter_%s_wait" % tag, handle, dx)):
            parts[(k, l)] = zone
    plan_last = _Plan([[owner_block(G, 0, "w_in")], [owner_block(G, 0, "w_o")]]
                      + [[owner_block(G, l, k) for l in range(DEPTH)] for k in ("conv_w", "ffn_conv_w")], False)
    last = _exchange(plan_last, "scatter_last")
    parts[("w_in", 0)], parts[("w_o", 0)] = last[0], last[1]
    small_parts_sharded = {"conv_w": last[2], "ffn_conv_w": last[3]}

    small = [jnp.stack(G[k]).reshape(-1) for k in _SMALL[:-1]] + [G["g_final"].reshape(-1), loss[0, :1]]
    sizes = [int(s.shape[0]) for s in small]
    flat = jnp.concatenate(small)
    n_small = -(-flat.shape[0] // 1024) * 1024
    flat = jnp.concatenate([flat, jnp.zeros((n_small - flat.shape[0],), F32)]).reshape(n_small // 128, 128)
    small_parts = _exchange(_Plan([[flat]], True), "gather_small")[0]

    out_g, out_d, out_m, out_v = {}, {}, {}, {}
    sg = _sum_parts(small_parts.reshape(NDEV, n_small // 128, 128), "sum_small").reshape(-1)
    off = 0
    summed = {}
    for k, n in zip(_SMALL + ("loss",), sizes):
        summed[k] = sg[off:off + n]
        off += n
    for k in _ORDER:
        shp = P[k].shape
        C = shp[-1]
        if k in _MATMUL_W:
            R0 = shp[1]
            flat2 = lambda t: t.reshape(DEPTH * R0, C)
            res = None
            for l in reversed(range(DEPTH)):
                res = _adamw(parts[(k, l)].reshape(NDEV, R0, C), flat2(P[k]), flat2(M[k]), flat2(V[k]),
                             "adamw_%s_l%d" % (k, l), row0=l * R0, prev=res)
        else:
            pr = small_parts_sharded[k].reshape((NDEV, -1, C)) if k in _SHARDED else summed[k].reshape((1, -1, C))
            R = pr.shape[1]
            res = _adamw(pr, P[k].reshape(R, -1), M[k].reshape(R, -1), V[k].reshape(R, -1), "adamw_" + k)
        out_g[k], out_d[k], out_m[k], out_v[k] = (t.reshape(shp) for t in res)

    loss_out = summed["loss"].reshape(())
    grad_x = dx.reshape(Bl, S, D)
    return (loss_out, grad_x, *[out_g[k] for k in _ORDER], *[out_d[k] for k in _ORDER], *[out_m[k] for k in _ORDER],
            *[out_v[k] for k in _ORDER])
```

```python
import functools

import numpy as np
import jax
import jax.numpy as jnp
from jax import lax
from jax.experimental import pallas as pl
from jax.experimental.pallas import tpu as pltpu

F32 = jnp.float32
BF16 = jnp.bfloat16
GRAD_DTYPE = BF16
HIGHEST = lax.Precision.HIGHEST

D = 1024
DEPTH = 2
HD = 64
WA = 384
NPAIR = 3
CC = 256
CONV_K = 31
DFF = 2816
FB = 704
NJ = 4
NDEV = 8
EPS = 1e-6
NQKV = 2304
NREST = 768
NPAD = NQKV + NREST
O_FA, O_QB, N_IN = 1152, 1158, 2822
DILATION_PAIRS = ((128, 1), (512, 4), (2048, 16))
NEG = -1e30

ADAM_LR, ADAM_B1, ADAM_B2, ADAM_EPS, ADAM_WD, ADAM_STEP = 0.001, 0.9, 0.999, 1e-08, 0.01, 10

VMEM_LIMIT = 48 * 1024 * 1024
VMEM_LIMIT_BIG = 56 * 1024 * 1024


def _cp(*sem):
    return pltpu.CompilerParams(dimension_semantics=sem, vmem_limit_bytes=VMEM_LIMIT)


def _sds(shape, dtype):
    return jax.ShapeDtypeStruct(shape, dtype)


def _dot(a, b, prec=None):
    return jnp.dot(a, b, preferred_element_type=F32, precision=prec)


def _dot_nt(a, b, prec=None):
    return lax.dot_general(a, b, (((1,), (1,)), ((), ())), preferred_element_type=F32, precision=prec)


def _dot_tn(a, b):
    return lax.dot_general(a, b, (((0,), (0,)), ((), ())), preferred_element_type=F32)


def _sigmoid(z):
    return 0.5 * jnp.tanh(0.5 * z) + 0.5


def _rms_bwd(dh, x, g):
    r = lax.rsqrt(jnp.mean(x * x, axis=-1, keepdims=True) + EPS)
    xh = x * r
    dg = jnp.sum(dh * xh, axis=0, keepdims=True)
    dxh = dh * g
    dx = r * (dxh - xh * jnp.mean(dxh * xh, axis=-1, keepdims=True))
    return dx, dg


class _Plan:
    def __init__(self, groups, gather, slots=None, n_slots=None):
        self.groups, self.gather = groups, gather
        self.slots = slots or [list(range(len(g))) for g in groups]
        self.n_slots = n_slots or [len(g) for g in groups]
        self.flat = [a for g in groups for a in g]
        self.where = [(gi, s) for gi, g in enumerate(groups) for s in self.slots[gi]]
        self.zone_shapes = [_sds((NDEV, ns) + (g[0].shape if gather else g[0].shape[1:]), g[0].dtype)
                            for g, ns in zip(groups, self.n_slots)]

    def new_zones(self):
        return [lax.empty(z.shape, z.dtype) for z in self.zone_shapes]

    def me(self):
        x, y, c = lax.axis_index("x"), lax.axis_index("y"), lax.axis_index("c")
        return 4 * x + 2 * y + c

    def zones_with_own(self):
        me = self.me()
        zones = self.new_zones()
        for a, (gi, s) in enumerate(self.where):
            src = self.flat[a] if self.gather else lax.dynamic_index_in_dim(self.flat[a], me, 0, keepdims=False)
            zones[gi] = lax.dynamic_update_slice(zones[gi], src[None, None], (me, s) + (0,) * src.ndim)
        return zones

    def own_copies(self, ins, zones, sems):
        me = self.me()
        return [pltpu.make_async_copy(ins[a] if self.gather else ins[a].at[me], zones[gi].at[me, s], sems.at[a])
                for a, (gi, s) in enumerate(self.where)]

    def remote_copies(self, ins, zones, send_sems, recv_sems):
        x, y, c = lax.axis_index("x"), lax.axis_index("y"), lax.axis_index("c")
        me = 4 * x + 2 * y + c
        out = []
        for a, (gi, s) in enumerate(self.where):
            for k in range(1, NDEV):
                px, py, pc = x ^ ((k >> 2) & 1), y ^ ((k >> 1) & 1), c ^ (k & 1)
                out.append(pltpu.make_async_remote_copy(
                    src_ref=ins[a] if self.gather else ins[a].at[4 * px + 2 * py + pc], dst_ref=zones[gi].at[me, s],
                    send_sem=send_sems.at[a * (NDEV - 1) + k - 1], recv_sem=recv_sems.at[a * (NDEV - 1) + k - 1],
                    device_id=(px, py, pc), device_id_type=pl.DeviceIdType.MESH))
        return out


_ANY = pl.BlockSpec(memory_space=pl.ANY)
_HBM = pl.BlockSpec(memory_space=pltpu.HBM)
_SEM = pl.BlockSpec(memory_space=pltpu.SEMAPHORE)


def _exchange(plan, name, zones=None):
    n, nz = len(plan.flat), len(plan.groups)
    zones = zones or plan.new_zones()

    def body(*refs):
        ins, zin = refs[:n], refs[n:n + nz]
        send_sems, recv_sems, local_sems = refs[n + 2 * nz:]
        copies = plan.own_copies(ins, zin, local_sems) + plan.remote_copies(ins, zin, send_sems, recv_sems)
        for cp in copies:
            cp.start()
        for cp in copies:
            cp.wait()

    return pl.pallas_call(
        body, name=name, out_shape=plan.zone_shapes, in_specs=[_ANY] * (n + nz), out_specs=[_ANY] * nz,
        input_output_aliases={n + g: g for g in range(nz)},
        scratch_shapes=[pltpu.SemaphoreType.DMA((n * (NDEV - 1),)), pltpu.SemaphoreType.DMA((n * (NDEV - 1),)),
                        pltpu.SemaphoreType.DMA((n,))],
        compiler_params=pltpu.CompilerParams(has_side_effects=True),
    )(*plan.flat, *zones)


def _exchange_start(plan, name, zones):
    n, nz = len(plan.flat), len(plan.groups)
    hbm = lambda a: pltpu.HBM(a.shape, a.dtype)

    def body(*refs):
        ins, zin = refs[:n], refs[n:n + nz]
        send_sems, recv_sems = refs[n + nz], refs[n + nz + 1]
        token = refs[-1]
        for cp in plan.remote_copies(ins, zin, send_sems, recv_sems):
            cp.start()
        token[...] = jnp.zeros_like(token)

    outs = pl.pallas_call(
        body, name=name,
        out_shape=(pltpu.SemaphoreType.DMA((n * (NDEV - 1),)), pltpu.SemaphoreType.DMA((n * (NDEV - 1),)),
                   *[hbm(a) for a in plan.flat], *[hbm(z) for z in plan.zone_shapes], _sds((8, 128), F32)),
        in_specs=[_HBM] * (n + nz),
        out_specs=(_SEM, _SEM, *[_HBM] * (n + nz), pl.BlockSpec(memory_space=pltpu.VMEM)),
        input_output_aliases={i: 2 + i for i in range(n + nz)},
        compiler_params=pltpu.CompilerParams(has_side_effects=pltpu.SideEffectType.DATAFLOW_SIDE_EFFECTING),
    )(*[pltpu.with_memory_space_constraint(a, pltpu.HBM) for a in plan.flat],
      *[pltpu.with_memory_space_constraint(z, pltpu.HBM) for z in zones])
    return outs[:-1], outs[-1]


def _exchange_wait(plan, name, handle, after):
    n, nz = len(plan.flat), len(plan.groups)
    send_sems, recv_sems = handle[0], handle[1]
    thru = handle[2:]
    hbm = lambda a: pltpu.HBM(a.shape, a.dtype)

    def body(*refs):
        ins, zin = refs[:n], refs[n:n + nz]
        ssem, rsem = refs[n + nz], refs[n + nz + 1]
        for cp in plan.remote_copies(ins, zin, ssem, rsem):
            cp.wait_send()
            cp.wait_recv()

    outs = pl.pallas_call(
        body, name=name, out_shape=tuple(hbm(a) for a in thru),
        in_specs=[_HBM] * (n + nz) + [_SEM, _SEM, _ANY], out_specs=tuple([_HBM] * (n + nz)),
        input_output_aliases={i: i for i in range(n + nz)},
        compiler_params=pltpu.CompilerParams(has_side_effects=pltpu.SideEffectType.DATAFLOW_SIDE_EFFECTING),
    )(*thru, send_sems, recv_sems, after)
    return list(outs[n:])


def _sum_parts(parts, name):
    npart, R, C = parts.shape

    def body(p_ref, o_ref):
        s = p_ref[0]
        for k in range(1, npart):
            s = s + p_ref[k]
        o_ref[...] = s

    return pl.pallas_call(body, name=name, out_shape=_sds((R, C), F32))(parts)


def _rms_fwd(x, g):
    T = x.shape[0]
    tm = min(512, T)

    def body(x_ref, g_ref, o_ref):
        xv = x_ref[...]
        r = lax.rsqrt(jnp.mean(xv * xv, axis=-1, keepdims=True) + EPS)
        o_ref[...] = (xv * r * g_ref[...]).astype(BF16)

    return pl.pallas_call(
        body, name="rms_fwd", grid=(T // tm,), out_shape=_sds((T, D), BF16),
        in_specs=[pl.BlockSpec((tm, D), lambda i: (i, 0)), pl.BlockSpec((1, D), lambda i: (0, 0))],
        out_specs=pl.BlockSpec((tm, D), lambda i: (i, 0)), compiler_params=_cp("parallel"))(x, g)


def _mm_in(h, w):
    T = h.shape[0]
    tm = min(512, T)

    def body(h_ref, w_ref, q_ref, r_ref):
        hv = h_ref[...]
        q_ref[...] = _dot(hv, w_ref[:, :NQKV]).astype(BF16)
        r_ref[...] = _dot(hv, w_ref[:, NQKV:])

    return pl.pallas_call(
        body, name="mm_in", grid=(T // tm,), out_shape=[_sds((T, NQKV), BF16), _sds((T, NREST), F32)],
        in_specs=[pl.BlockSpec((tm, D), lambda i: (i, 0)), pl.BlockSpec((D, NPAD), lambda i: (0, 0))],
        out_specs=[pl.BlockSpec((tm, NQKV), lambda i: (i, 0)), pl.BlockSpec((tm, NREST), lambda i: (i, 0))],
        compiler_params=_cp("parallel"))(h, w)


AUG_Q1, AUG_K1 = 3, 0


def _aug_lane0(h):
    return HD * (1 - h % 2)


def _aug_tables():
    selq = np.zeros((NPAIR, 3 * 128, 128), np.float32)
    selk = np.zeros((NPAIR, 3 * 128, 128), np.float32)
    oneq = np.zeros((NPAIR, 1, 128), np.float32)
    onek = np.zeros((NPAIR, 1, 128), np.float32)
    for h in range(2 * NPAIR):
        p, l0 = h // 2, _aug_lane0(h)
        for piece in range(3):
            selq[p, 128 * piece + h, l0 + piece] = 1.0
            selk[p, 128 * piece + h, l0 + 3 + piece] = -1.0
            oneq[p, 0, l0 + AUG_Q1 + piece] = 1.0
            onek[p, 0, l0 + AUG_K1 + piece] = 1.0
    return [jnp.asarray(a, BF16) for a in (selq, selk, oneq, onek)]


def _gate_fwd(p_rest, bf, S, TB):
    T = p_rest.shape[0]
    B, nb = T // S, S // TB
    selq, selk, oneq, onek = _aug_tables()

    def body(z_ref, b_ref, sq_ref, sk_ref, oq_ref, ok_ref, aq_ref, ak_ref):
        tri = (lax.broadcasted_iota(jnp.int32, (TB, TB), 0) >= lax.broadcasted_iota(jnp.int32, (TB, TB), 1)).astype(F32)
        carry = jnp.zeros((1, 128), F32)
        for j in range(nb):
            rows = slice(j * TB, (j + 1) * TB)
            z = z_ref[rows, :] + b_ref[...]
            lf = jnp.minimum(z, 0.0) - jnp.log(1.0 + jnp.exp(-jnp.abs(z)))
            cb = _dot(tri, lf, HIGHEST) + carry
            carry = cb[TB - 1:TB, :]
            hi = cb.astype(BF16)
            mid = (cb - hi.astype(F32)).astype(BF16)
            lo = (cb - hi.astype(F32) - mid.astype(F32)).astype(BF16)
            pieces = jnp.concatenate([hi, mid, lo], axis=1)
            for p in range(NPAIR):
                aq_ref[p, rows, :] = (_dot(pieces, sq_ref[p]) + oq_ref[p].astype(F32)).astype(BF16)
                ak_ref[p, rows, :] = (_dot(pieces, sk_ref[p]) + ok_ref[p].astype(F32)).astype(BF16)

    full = lambda a: pl.BlockSpec(a.shape, lambda b: (0,) * a.ndim)
    return pl.pallas_call(
        body, name="gate_fwd", grid=(B,), out_shape=[_sds((NPAIR, T, 128), BF16), _sds((NPAIR, T, 128), BF16)],
        in_specs=[pl.BlockSpec((S, 128), lambda b: (b, 4)), pl.BlockSpec((1, 128), lambda b: (0, 0)),
                  full(selq), full(selk), full(oneq), full(onek)],
        out_specs=[pl.BlockSpec((NPAIR, S, 128), lambda b: (0, b, 0)), pl.BlockSpec((NPAIR, S, 128), lambda b: (0, b, 0))],
        compiler_params=_cp("parallel"))(p_rest, bf, selq, selk, oneq, onek)


def _bias_table(S, TB, TQ, fox):
    r = TQ // TB
    d = np.arange(-(r - 1), S // TB)[:, None, None]
    delta = d * TB + np.arange(TQ)[None, None, :] - np.arange(TB)[None, :, None]
    if fox:
        mult = (delta >= 0).astype(np.float64)
    else:
        mult = np.zeros(delta.shape)
        for w, dil in DILATION_PAIRS:
            mult += (delta >= 0) & (delta % dil == 0) & (delta <= w)
    with np.errstate(divide="ignore"):
        tab = np.where(mult > 0, np.log(np.maximum(mult, 1e-30)), NEG)
    tab = np.concatenate([tab, np.full((1, TB, TQ), NEG)], axis=0)
    return jnp.asarray(tab, F32)


def _head_masks():
    lane = lax.broadcasted_iota(jnp.int32, (1, 128), 1)
    return [lane < HD, lane >= HD]


def _transpose_bf16(a):
    return a.astype(F32).T.astype(BF16)


def _col_reduce(x, op):
    while x.shape[0] > 8:
        half = x.shape[0] // 2
        x = op(x[:half], x[half:])
    return jnp.max(x, axis=0, keepdims=True) if op is jnp.maximum else jnp.sum(x, axis=0, keepdims=True)


def _attn_fwd(p_qkv, col0, tab, S, aq=None, ak=None, name="attn"):
    T = p_qkv.shape[0]
    n_tab, TB, TQ = tab.shape
    nb, nq, r = S // TB, S // TQ, TQ // TB
    B = T // S
    fox = aq is not None

    def body(*refs):
        if fox:
            q_ref, k_ref, v_ref, tab_ref, aq_ref, ak_ref, o_ref, l_ref, kat_ref, vta_scr, acc_scr, st_scr = refs
        else:
            q_ref, k_ref, v_ref, tab_ref, o_ref, l_ref, kat_ref, vta_scr, acc_scr, st_scr = refs
        i = pl.program_id(2)
        hms = _head_masks()
        top = lax.broadcasted_iota(jnp.int32, (128, 1), 0) < HD
        last_key_block = r * i + r - 1

        @pl.when(i == 0)
        def _():
            for jj in range(nb):
                rows = slice(jj * TB, (jj + 1) * TB)
                vt = _transpose_bf16(v_ref[rows, :])
                for hh in range(2):
                    vta_scr[hh, jj] = jnp.where(top == (hh == 0), vt, jnp.ones_like(vt))
                    spare_k = ak_ref[0, rows, :] if fox else jnp.zeros((TB, 128), BF16)
                    kat_ref[0, 0, hh, jj] = _transpose_bf16(jnp.where(hms[hh], k_ref[rows, :], spare_k))

        q2 = q_ref[...] * 0.125
        spare_q = aq_ref[0] if fox else jnp.zeros_like(q2)
        qa = [jnp.where(hm, q2, spare_q) for hm in hms]
        acc_scr[...] = jnp.zeros_like(acc_scr)

        def scores(j):
            jc = jnp.minimum(j, nb - 1)
            rows = pl.ds(pl.multiple_of(jc * TB, TB), TB)
            k2 = k_ref[rows, :]
            bias = tab_ref[jnp.where(j <= last_key_block, r * i - j + (r - 1), n_tab - 1)]
            return [_dot_nt(jnp.where(hms[hh], k2, ak_ref[0, rows, :]) if fox else k2, qa[hh]) + bias for hh in range(2)]

        def absorb(j, sts, stats):
            jc = jnp.minimum(j, nb - 1)
            alphas, pts = [], []
            for hh in range(2):
                m_old = stats[2 * hh]
                m_new = jnp.maximum(m_old, _col_reduce(sts[hh], jnp.maximum))
                pts.append(jnp.exp(sts[hh] - m_new).astype(BF16))
                alphas.append(jnp.exp(m_old - m_new))
                stats[2 * hh] = m_new
            pvs = [_dot(vta_scr[hh, jc], pts[hh]) for hh in range(2)]
            for hh in range(2):
                half = slice(HD * hh, HD * (hh + 1))
                ones_row = HD * (1 - hh)
                acc_scr[half, :] = alphas[hh] * acc_scr[half, :] + pvs[hh][half]
                stats[2 * hh + 1] = alphas[hh] * stats[2 * hh + 1] + pvs[hh][ones_row:ones_row + 1]

        def kv_pair(jj, carry):
            stats = list(carry)
            j0 = 2 * jj
            st0 = [st_scr[hh] for hh in range(2)]
            st1 = scores(j0 + 1)
            absorb(j0, st0, stats)
            nxt = scores(j0 + 2)
            for hh in range(2):
                st_scr[hh] = nxt[hh]
            absorb(j0 + 1, st1, stats)
            return tuple(stats)

        first = scores(0)
        for hh in range(2):
            st_scr[hh] = first[hh]
        row = lambda v: jnp.full((1, TQ), v, F32)
        m0, l0, m1, l1 = lax.fori_loop(0, (last_key_block + 2) // 2, kv_pair, (row(NEG), row(0.0), row(NEG), row(0.0)))
        o_ref[...] = (acc_scr[...] * jnp.where(top, 1.0 / l0, 1.0 / l1)).T
        for hh, lse in enumerate((m0 + jnp.log(l0), m1 + jnp.log(l1))):
            for t in range(r):
                l_ref[t, hh] = lse[:, t * TB:(t + 1) * TB]

    in_specs = [pl.BlockSpec((TQ, 128), lambda b, p, i: (b * nq + i, col0 + p)),
                pl.BlockSpec((S, 128), lambda b, p, i: (b, col0 + NPAIR + p)),
                pl.BlockSpec((S, 128), lambda b, p, i: (b, col0 + 2 * NPAIR + p)),
                pl.BlockSpec(tab.shape, lambda b, p, i: (0, 0, 0))]
    args = [p_qkv, p_qkv, p_qkv, tab]
    if fox:
        in_specs += [pl.BlockSpec((1, TQ, 128), lambda b, p, i: (p, b * nq + i, 0)),
                     pl.BlockSpec((1, S, 128), lambda b, p, i: (p, b, 0))]
        args += [aq, ak]
    return pl.pallas_call(
        body, name=name, grid=(B, NPAIR, nq),
        out_shape=[_sds((T, WA), F32), _sds((T // TB, 2 * NPAIR, 1, TB), F32), _sds((B, NPAIR, 2, nb, 128, TB), BF16)],
        in_specs=in_specs,
        out_specs=[pl.BlockSpec((TQ, 128), lambda b, p, i: (b * nq + i, p)),
                   pl.BlockSpec((r, 2, 1, TB), lambda b, p, i: (b * nq + i, p, 0, 0)),
                   pl.BlockSpec((1, 1, 2, nb, 128, TB), lambda b, p, i: (b, p, 0, 0, 0, 0))],
        scratch_shapes=[pltpu.VMEM((2, nb, 128, TB), BF16), pltpu.VMEM((128, TQ), F32), pltpu.VMEM((2, TB, TQ), F32)],
        compiler_params=_cp("parallel", "parallel", "arbitrary"))(*args)


def _phase_copies(src, phases, length):
    for r in range(1, 8):
        phases[r, 0:length - 8, :] = src[pl.ds(r, length - 8), :]


def _tap(src, phases, offset, rows):
    r = offset % 8
    return src[pl.ds(offset, rows), :] if r == 0 else phases[r, pl.ds(offset - r, rows), :]


def _conv_taps(src, phases, w_ref, first_row, rows):
    acc = w_ref[0:1, :] * _tap(src, phases, first_row, rows)
    for k in range(1, CONV_K):
        acc = acc + w_ref[k:k + 1, :] * _tap(src, phases, first_row + k, rows)
    return acc


def _mix_fwd(o_a, o_b, p_rest, g_a, g_b, cw, cb, ng, nbias, S):
    T = o_a.shape[0]
    tm = min(256, S)
    nps = S // tm
    HALO = 32

    def body(oa_ref, ob_ref, pt_ref, ph_ref, ga_ref, gb_ref, cw_ref, cb_ref, ng_ref, nb_ref, y_ref, buf, phases):
        i = pl.program_id(0)
        first = (i % nps) == 0
        for o_ref, g_ref, c0 in ((oa_ref, ga_ref, 0), (ob_ref, gb_ref, WA)):
            o = o_ref[...]
            r = lax.rsqrt(jnp.mean(o * o, axis=-1, keepdims=True) + EPS)
            y_ref[:, c0:c0 + WA] = (o * r * g_ref[...]).astype(BF16)
        ph = jnp.where(first, 0.0, ph_ref[...])
        buf[0:HALO, :] = ph[:, :CC] * _sigmoid(ph[:, CC:])
        pt = pt_ref[...]
        buf[HALO:HALO + tm, :] = pt[:, :CC] * _sigmoid(pt[:, CC:])
        _phase_copies(buf, phases, tm + HALO)
        cv = _conv_taps(buf, phases, cw_ref, HALO - CONV_K + 1, tm) + cb_ref[...]
        xc = cv - jnp.mean(cv, axis=-1, keepdims=True)
        lo = xc * lax.rsqrt(jnp.mean(xc * xc, axis=-1, keepdims=True) + EPS) * ng_ref[...] + nb_ref[...]
        y_ref[:, 2 * WA:] = (lo * _sigmoid(lo)).astype(BF16)

    row = lambda w: pl.BlockSpec((1, w), lambda i: (0, 0))
    return pl.pallas_call(
        body, name="mix_fwd", grid=(T // tm,), out_shape=_sds((T, D), BF16),
        in_specs=[pl.BlockSpec((tm, WA), lambda i: (i, 0)), pl.BlockSpec((tm, WA), lambda i: (i, 0)),
                  pl.BlockSpec((tm, 2 * CC), lambda i: (i, 0)),
                  pl.BlockSpec((HALO, 2 * CC), lambda i: (jnp.maximum(i * (tm // HALO) - 1, 0), 0)),
                  row(WA), row(WA), pl.BlockSpec((32, CC), lambda i: (0, 0)), row(CC), row(CC), row(CC)],
        out_specs=pl.BlockSpec((tm, D), lambda i: (i, 0)),
        scratch_shapes=[pltpu.VMEM((tm + HALO, CC), F32), pltpu.VMEM((8, tm + HALO, CC), F32)],
        compiler_params=_cp("parallel"))(o_a, o_b, p_rest, p_rest, g_a, g_b, cw, cb, ng, nbias)


def _mm_res(a, w, resid, g, name):
    nk, T, kb = a.shape
    tm = min(512, T)

    def body(a_ref, w_ref, r_ref, g_ref, x_ref, h_ref):
        xv = r_ref[...]
        for k in range(nk):
            xv = xv + _dot(a_ref[k], w_ref[k])
        x_ref[...] = xv
        r = lax.rsqrt(jnp.mean(xv * xv, axis=-1, keepdims=True) + EPS)
        h_ref[...] = (xv * r * g_ref[...]).astype(BF16)

    return pl.pallas_call(
        body, name=name, grid=(T // tm,), out_shape=[_sds((T, D), F32), _sds((T, D), BF16)],
        in_specs=[pl.BlockSpec((nk, tm, kb), lambda i: (0, i, 0)), pl.BlockSpec((nk, kb, D), lambda i: (0, 0, 0)),
                  pl.BlockSpec((tm, D), lambda i: (i, 0)), pl.BlockSpec((1, D), lambda i: (0, 0))],
        out_specs=[pl.BlockSpec((tm, D), lambda i: (i, 0)), pl.BlockSpec((tm, D), lambda i: (i, 0))],
        compiler_params=_cp("parallel"))(a, w, resid, g)


def _up_fwd(h2, w_up, fcw, fcb, S):
    T = h2.shape[0]
    tm = min(512, S)
    nps = S // tm

    def body(h_ref, hh_ref, w_ref, cw_ref, cb_ref, u_ref, hid_ref, buf):
        i = pl.program_id(1)
        first = (i % nps) == 0
        hv = h_ref[...]
        hh = jnp.where(first, jnp.zeros_like(hh_ref[...]), hh_ref[...])
        c = []
        for half in range(2):
            w = w_ref[half, 0]
            u = _dot(hv, w)
            u_ref[half, 0] = u
            buf[half, 0:8, :] = _dot(hh, w)
            buf[half, 8:8 + tm, :] = u
            cw = cw_ref[half, 0]
            c.append(cb_ref[half, 0] + cw[0:1] * buf[half, pl.ds(6, tm), :] + cw[1:2] * buf[half, pl.ds(7, tm), :] + cw[2:3] * u)
        hid_ref[0] = (c[0] * _sigmoid(c[0]) * c[1]).astype(BF16)

    return pl.pallas_call(
        body, name="up_fwd", grid=(NJ, T // tm), out_shape=[_sds((2, NJ, T, FB), F32), _sds((NJ, T, FB), BF16)],
        in_specs=[pl.BlockSpec((tm, D), lambda j, i: (i, 0)),
                  pl.BlockSpec((8, D), lambda j, i: (jnp.maximum(i * (tm // 8) - 1, 0), 0)),
                  pl.BlockSpec((2, 1, D, FB), lambda j, i: (0, j, 0, 0)),
                  pl.BlockSpec((2, 1, 3, FB), lambda j, i: (0, j, 0, 0)),
                  pl.BlockSpec((2, 1, 1, FB), lambda j, i: (0, j, 0, 0))],
        out_specs=[pl.BlockSpec((2, 1, tm, FB), lambda j, i: (0, j, i, 0)), pl.BlockSpec((1, tm, FB), lambda j, i: (j, i, 0))],
        scratch_shapes=[pltpu.VMEM((2, tm + 8, FB), F32)],
        compiler_params=_cp("parallel", "parallel"))(h2, h2, w_up, fcw, fcb)


def _final_loss(x, tgt, g):
    T = x.shape[0]
    tm = min(512, T)

    def body(x_ref, t_ref, g_ref, dx_ref, loss_ref, dg_ref):
        @pl.when(pl.program_id(0) == 0)
        def _():
            loss_ref[...] = jnp.zeros_like(loss_ref)
            dg_ref[...] = jnp.zeros_like(dg_ref)

        xv, gv = x_ref[...], g_ref[...]
        r = lax.rsqrt(jnp.mean(xv * xv, axis=-1, keepdims=True) + EPS)
        e = xv * r * gv - t_ref[...]
        loss_ref[...] += 0.5 * jnp.sum(jnp.mean(e * e, axis=-1, keepdims=True), axis=0, keepdims=True)
        dx, dg = _rms_bwd(e * (1.0 / D), xv, gv)
        dx_ref[...] = dx
        dg_ref[...] += dg

    return pl.pallas_call(
        body, name="final_loss", grid=(T // tm,), out_shape=[_sds((T, D), F32), _sds((1, 128), F32), _sds((1, D), F32)],
        in_specs=[pl.BlockSpec((tm, D), lambda i: (i, 0)), pl.BlockSpec((tm, D), lambda i: (i, 0)), pl.BlockSpec((1, D), lambda i: (0, 0))],
        out_specs=[pl.BlockSpec((tm, D), lambda i: (i, 0)), pl.BlockSpec((1, 128), lambda i: (0, 0)), pl.BlockSpec((1, D), lambda i: (0, 0))],
        compiler_params=_cp("arbitrary"))(x, tgt, g)


def _down_bwd(dx2, w_down, u, fcw, fcb, S):
    T = dx2.shape[0]
    tm = min(512, S)
    nps = S // tm
    nblk8 = T // 8

    def body(dx_ref, dxn_ref, wd_ref, ut_ref, up_ref, un_ref, cw_ref, cb_ref, du_ref, dcw_ref, dxb, ubuf, dcbuf, ush):
        i = pl.program_id(1)
        first, last = (i % nps) == 0, (i % nps) == nps - 1

        @pl.when(i == 0)
        def _():
            dcw_ref[...] = jnp.zeros_like(dcw_ref)

        dxb[0:tm, :] = dx_ref[...].astype(BF16)
        dxb[tm:tm + 8, :] = jnp.where(last, 0.0, dxn_ref[...]).astype(BF16)
        dh = _dot_nt(dxb[...], wd_ref[0])
        ce = []
        for half in range(2):
            ubuf[half, 0:8, :] = jnp.where(first, 0.0, up_ref[half, 0])
            ubuf[half, 8:8 + tm, :] = ut_ref[half, 0]
            ubuf[half, 8 + tm:16 + tm, :] = un_ref[half, 0]
            cw = cw_ref[half, 0]
            for k in range(2):
                ush[half, k] = ubuf[half, pl.ds(6 + k, tm + 8), :]
            ce.append(cb_ref[half, 0] + cw[0:1] * ush[half, 0] + cw[1:2] * ush[half, 1] + cw[2:3] * ubuf[half, pl.ds(8, tm + 8), :])
        sg = _sigmoid(ce[0])
        dc = [dh * ce[1] * (sg * (1.0 + ce[0] * (1.0 - sg))), dh * (ce[0] * sg)]
        for half in range(2):
            cw = cw_ref[half, 0]
            dcbuf[...] = dc[half]
            dct = dc[half][0:tm]
            du_ref[half, 0] = (cw[2:3] * dct + cw[1:2] * dcbuf[pl.ds(1, tm), :] + cw[0:1] * dcbuf[pl.ds(2, tm), :]).astype(BF16)
            for k in range(3):
                u_back = ush[half, k, 0:tm, :] if k < 2 else ubuf[half, pl.ds(8, tm), :]
                dcw_ref[half, 0, k:k + 1, :] += jnp.sum(dct * u_back, axis=0, keepdims=True)
            dcw_ref[half, 0, 3:4, :] += jnp.sum(dct, axis=0, keepdims=True)

    ublk = lambda rows, imap: pl.BlockSpec((2, 1, rows, FB), imap)
    return pl.pallas_call(
        body, name="down_bwd", grid=(NJ, T // tm), out_shape=[_sds((2, NJ, T, FB), BF16), _sds((2, NJ, 8, FB), F32)],
        in_specs=[pl.BlockSpec((tm, D), lambda j, i: (i, 0)),
                  pl.BlockSpec((8, D), lambda j, i: (jnp.minimum((i + 1) * (tm // 8), nblk8 - 1), 0)),
                  pl.BlockSpec((1, FB, D), lambda j, i: (j, 0, 0)),
                  ublk(tm, lambda j, i: (0, j, i, 0)),
                  ublk(8, lambda j, i: (0, j, jnp.maximum(i * (tm // 8) - 1, 0), 0)),
                  ublk(8, lambda j, i: (0, j, jnp.minimum((i + 1) * (tm // 8), nblk8 - 1), 0)),
                  pl.BlockSpec((2, 1, 3, FB), lambda j, i: (0, j, 0, 0)),
                  pl.BlockSpec((2, 1, 1, FB), lambda j, i: (0, j, 0, 0))],
        out_specs=[ublk(tm, lambda j, i: (0, j, i, 0)), pl.BlockSpec((2, 1, 8, FB), lambda j, i: (0, j, 0, 0))],
        scratch_shapes=[pltpu.VMEM((tm + 8, D), BF16), pltpu.VMEM((2, tm + 16, FB), F32), pltpu.VMEM((tm + 8, FB), F32),
                        pltpu.VMEM((2, 2, tm + 8, FB), F32)],
        compiler_params=_cp("parallel", "arbitrary"))(dx2, dx2, w_down, u, u, u, fcw, fcb)


def _wgrad(a, b, tn, name, ga=1, gb=1):
    na, T, ka = a.shape
    nb, _, kb = b.shape
    tk = min(1024, T)
    nt = T // tk

    def body(a_ref, b_ref, o_ref, acc):
        t = pl.program_id(3)

        @pl.when(t == 0)
        def _():
            acc[...] = jnp.zeros_like(acc)

        bs = [b_ref[ib].astype(BF16) for ib in range(gb)]
        for ia in range(ga):
            av = a_ref[ia]
            for ib in range(gb):
                acc[ia, ib] += _dot_tn(av, bs[ib])

        @pl.when(t == nt - 1)
        def _():
            o_ref[...] = acc[...].astype(GRAD_DTYPE)

    return pl.pallas_call(
        body, name=name, grid=(na // ga, nb // gb, kb // tn, nt), out_shape=_sds((na, nb, ka, kb), GRAD_DTYPE),
        in_specs=[pl.BlockSpec((ga, tk, ka), lambda ia, ib, n, t: (ia, t, 0)), pl.BlockSpec((gb, tk, tn), lambda ia, ib, n, t: (ib, t, n))],
        out_specs=pl.BlockSpec((ga, gb, ka, tn), lambda ia, ib, n, t: (ia, ib, 0, n)),
        scratch_shapes=[pltpu.VMEM((ga, gb, ka, tn), F32)],
        compiler_params=_cp("parallel", "parallel", "parallel", "arbitrary"))(a, b)


def _dx_rms(dy, w, x, g, dres, blocked, name):
    T = x.shape[0]
    tm = min(256 if blocked else 512, T)

    def body(dy_ref, w_ref, x_ref, g_ref, r_ref, dx_ref, dg_ref):
        @pl.when(pl.program_id(0) == 0)
        def _():
            dg_ref[...] = jnp.zeros_like(dg_ref)

        if blocked:
            dh = _dot_nt(dy_ref[0, 0], w_ref[0, 0])
            for half, k in [(h, k) for k in range(NJ) for h in range(2)][1:]:
                dh = dh + _dot_nt(dy_ref[half, k], w_ref[half, k])
        else:
            dh = _dot_nt(dy_ref[...], w_ref[...])
        dx, dg = _rms_bwd(dh, x_ref[...], g_ref[...])
        dx_ref[...] = r_ref[...] + dx
        dg_ref[...] += dg

    if blocked:
        dy_spec = pl.BlockSpec((2, NJ, tm, FB), lambda i: (0, 0, i, 0))
        w_spec = pl.BlockSpec((2, NJ, D, FB), lambda i: (0, 0, 0, 0))
    else:
        dy_spec = pl.BlockSpec((tm, dy.shape[1]), lambda i: (i, 0))
        w_spec = pl.BlockSpec(w.shape, lambda i: (0, 0))
    tile = pl.BlockSpec((tm, D), lambda i: (i, 0))
    return pl.pallas_call(
        body, name=name, grid=(T // tm,), out_shape=[_sds((T, D), F32), _sds((1, D), F32)],
        in_specs=[dy_spec, w_spec, tile, pl.BlockSpec((1, D), lambda i: (0, 0)), tile],
        out_specs=[tile, pl.BlockSpec((1, D), lambda i: (0, 0))],
        compiler_params=pltpu.CompilerParams(dimension_semantics=("arbitrary",), vmem_limit_bytes=VMEM_LIMIT_BIG))(dy, w, x, g, dres)


def _mm_nt(a, w, name):
    T = a.shape[0]
    tm = min(512, T)

    def body(a_ref, w_ref, o_ref):
        o_ref[...] = _dot_nt(a_ref[...].astype(BF16), w_ref[...])

    return pl.pallas_call(
        body, name=name, grid=(T // tm,), out_shape=_sds((T, D), F32),
        in_specs=[pl.BlockSpec((tm, D), lambda i: (i, 0)), pl.BlockSpec((D, D), lambda i: (0, 0))],
        out_specs=pl.BlockSpec((tm, D), lambda i: (i, 0)), compiler_params=_cp("parallel"))(a, w)


def _mix_bwd(dy, o_a, o_b, p_rest, g_a, g_b, cw, cb, ng, nbias, S):
    T = dy.shape[0]
    tm = min(256, S)
    nps = S // tm
    HALO = 32
    nblkh = T // HALO
    RE = tm + HALO

    def body(dy_ref, dyn_ref, oa_ref, ob_ref, pt_ref, pp_ref, pn_ref, ga_ref, gb_ref, cw_ref, cb_ref, ng_ref, nb_ref,
             doa_ref, dda_ref, dob_ref, ddb_ref, dg_ref, dgn_ref, dcw_ref, dcn_ref, gbuf, dvbuf, gph, dph):
        i = pl.program_id(0)
        first, last = (i % nps) == 0, (i % nps) == nps - 1

        @pl.when(i == 0)
        def _():
            dgn_ref[...] = jnp.zeros_like(dgn_ref)
            dcw_ref[...] = jnp.zeros_like(dcw_ref)
            dcn_ref[...] = jnp.zeros_like(dcn_ref)

        head_of = (lax.broadcasted_iota(jnp.int32, (8, WA), 1) // HD == lax.broadcasted_iota(jnp.int32, (8, WA), 0)).astype(F32)
        for n, (o_ref, g_ref, do_ref, dd_ref) in enumerate(((oa_ref, ga_ref, doa_ref, dda_ref), (ob_ref, gb_ref, dob_ref, ddb_ref))):
            o = o_ref[...]
            do, dg = _rms_bwd(dy_ref[:, n * WA:(n + 1) * WA], o, g_ref[...])
            dob = do.astype(BF16)
            do_ref[...] = dob
            ddt = _dot_nt(head_of, dob.astype(F32) * o, HIGHEST)
            for hd in range(8):
                dd_ref[0, hd] = ddt[hd:hd + 1, :]
            dgn_ref[n:n + 1, :] += dg

        pp = jnp.where(first, 0.0, pp_ref[...])
        gbuf[0:HALO, :] = pp[:, :CC] * _sigmoid(pp[:, CC:])
        pt = pt_ref[...]
        gv, sgg = pt[:, :CC], _sigmoid(pt[:, CC:])
        gbuf[HALO:HALO + tm, :] = gv * sgg
        pn = pn_ref[...]
        gbuf[HALO + tm:2 * HALO + tm, :] = pn[:, :CC] * _sigmoid(pn[:, CC:])
        _phase_copies(gbuf, gph, tm + 2 * HALO)
        cv = _conv_taps(gbuf, gph, cw_ref, HALO - CONV_K + 1, RE) + cb_ref[...]
        xc = cv - jnp.mean(cv, axis=-1, keepdims=True)
        r = lax.rsqrt(jnp.mean(xc * xc, axis=-1, keepdims=True) + EPS)
        xh = xc * r
        lo = xh * ng_ref[...] + nb_ref[...]
        sg = _sigmoid(lo)
        dyc = jnp.concatenate([dy_ref[:, 2 * WA:], jnp.where(last, 0.0, dyn_ref[...])], axis=0)
        dlo = dyc * (sg * (1.0 + lo * (1.0 - sg)))
        dcn_ref[0:1, :] += jnp.sum(dlo[0:tm] * xh[0:tm], axis=0, keepdims=True)
        dcn_ref[1:2, :] += jnp.sum(dlo[0:tm], axis=0, keepdims=True)
        dxh = dlo * ng_ref[...]
        dcv = r * (dxh - jnp.mean(dxh, axis=-1, keepdims=True) - xh * jnp.mean(dxh * xh, axis=-1, keepdims=True))
        dvbuf[...] = dcv
        _phase_copies(dvbuf, dph, RE)
        dct = dcv[0:tm]
        dcw_ref[CONV_K:CONV_K + 1, :] += jnp.sum(dct, axis=0, keepdims=True)
        dglu = jnp.zeros((tm, CC), F32)
        for k in range(CONV_K):
            dcw_ref[k:k + 1, :] += jnp.sum(dct * _tap(gbuf, gph, HALO - CONV_K + 1 + k, tm), axis=0, keepdims=True)
            dglu = dglu + cw_ref[k:k + 1, :] * _tap(dvbuf, dph, CONV_K - 1 - k, tm)
        dg_ref[:, :CC] = (dglu * sgg).astype(BF16)
        dg_ref[:, CC:] = (dglu * gv * sgg * (1.0 - sgg)).astype(BF16)

    row = lambda w: pl.BlockSpec((1, w), lambda i: (0, 0))
    tile = lambda w: pl.BlockSpec((tm, w), lambda i: (i, 0))
    nxt = lambda i: jnp.minimum((i + 1) * (tm // HALO), nblkh - 1)
    const = lambda r, w: pl.BlockSpec((r, w), lambda i: (0, 0))
    ddrow = pl.BlockSpec((1, 8, 1, tm), lambda i: (i, 0, 0, 0))
    return pl.pallas_call(
        body, name="mix_bwd", grid=(T // tm,),
        out_shape=[_sds((T, WA), BF16), _sds((T // tm, 8, 1, tm), F32), _sds((T, WA), BF16), _sds((T // tm, 8, 1, tm), F32),
                   _sds((T, 2 * CC), BF16), _sds((8, WA), F32), _sds((32, CC), F32), _sds((8, CC), F32)],
        in_specs=[tile(D), pl.BlockSpec((HALO, CC), lambda i: (nxt(i), 3)), tile(WA), tile(WA), tile(2 * CC),
                  pl.BlockSpec((HALO, 2 * CC), lambda i: (jnp.maximum(i * (tm // HALO) - 1, 0), 0)),
                  pl.BlockSpec((HALO, 2 * CC), lambda i: (nxt(i), 0)),
                  row(WA), row(WA), const(32, CC), row(CC), row(CC), row(CC)],
        out_specs=[tile(WA), ddrow, tile(WA), ddrow, tile(2 * CC), const(8, WA), const(32, CC), const(8, CC)],
        scratch_shapes=[pltpu.VMEM((tm + 2 * HALO, CC), F32), pltpu.VMEM((RE, CC), F32),
                        pltpu.VMEM((8, tm + 2 * HALO, CC), F32), pltpu.VMEM((8, RE, CC), F32)],
        compiler_params=_cp("arbitrary"))(dy, dy, o_a, o_b, p_rest, p_rest, p_rest, g_a, g_b, cw, cb, ng, nbias)


def _attn_bwd(p_qkv, col0, tab, do, lse, dd, kat, S, aq=None, ak=None, name="attn_bwd"):
    T = p_qkv.shape[0]
    nb, TB = tab.shape[0] - 1, tab.shape[1]
    B = T // S
    fox = aq is not None

    def body(*refs):
        if fox:
            (q_ref, k_ref, v_ref, tab_ref, do_ref, l_ref, dd_ref, kat_ref, aq_ref, ak_ref,
             dq_ref, dk_ref, dv_ref, dqa_ref, dka_ref, dqt_scr, dk_scr, dv_scr, sd_scr) = refs
        else:
            (q_ref, k_ref, v_ref, tab_ref, do_ref, l_ref, dd_ref, kat_ref,
             dq_ref, dk_ref, dv_ref, dqt_scr, dk_scr, dv_scr, sd_scr) = refs
        j = pl.program_id(2)
        hms = _head_masks()

        @pl.when(j == 0)
        def _():
            dqt_scr[...] = jnp.zeros_like(dqt_scr)

        dk_scr[...] = jnp.zeros_like(dk_scr)
        dv_scr[...] = jnp.zeros_like(dv_scr)
        k2, v2 = k_ref[...], v_ref[...]
        ka = [jnp.where(hm, k2, ak_ref[0] if fox else jnp.zeros_like(k2)) for hm in hms]
        kat = [kat_ref[0, 0, hh, 0] for hh in range(2)]

        def operands(i):
            ic = jnp.minimum(i, nb - 1)
            rows = pl.ds(pl.multiple_of(ic * TB, TB), TB)
            q2 = q_ref[rows, :] * 0.125
            do2 = do_ref[rows, :]
            qas = [jnp.where(hms[hh], q2, aq_ref[0, rows, :] if fox else jnp.zeros_like(q2)) for hh in range(2)]
            dohs = [jnp.where(hms[hh], do2, jnp.zeros_like(do2)) for hh in range(2)]
            return ic, qas, dohs

        def scores(i):
            ic, qas, dohs = operands(i)
            bias = tab_ref[jnp.where(i < nb, i - j, nb)]
            return [(_dot_nt(ka[hh], qas[hh]) + bias, _dot_nt(v2, dohs[hh])) for hh in range(2)]

        def absorb(i, sd):
            ic, qas, dohs = operands(i)
            for hh in range(2):
                st, dpt = sd[hh]
                pt = jnp.exp(st - l_ref[ic, hh])
                dsb = (pt * (dpt - dd_ref[ic, hh])).astype(BF16)
                dv_scr[...] += _dot(pt.astype(BF16), dohs[hh])
                dk_scr[hh] += _dot(dsb, qas[hh])
                dqt_scr[hh, ic] += _dot(kat[hh], dsb)

        def q_pair(ii, carry):
            i0 = j + 2 * ii
            sd0 = [(sd_scr[hh, 0], sd_scr[hh, 1]) for hh in range(2)]
            sd1 = scores(i0 + 1)
            absorb(i0, sd0)
            nxt = scores(i0 + 2)
            for hh in range(2):
                sd_scr[hh, 0], sd_scr[hh, 1] = nxt[hh]
            absorb(i0 + 1, sd1)
            return carry

        first = scores(j)
        for hh in range(2):
            sd_scr[hh, 0], sd_scr[hh, 1] = first[hh]
        lax.fori_loop(0, (nb - j + 1) // 2, q_pair, 0)
        dk_ref[...] = jnp.where(hms[0], dk_scr[0], dk_scr[1]).astype(BF16)
        dv_ref[...] = dv_scr[...].astype(BF16)
        if fox:
            dka_ref[...] = jnp.where(hms[0], dk_scr[1], dk_scr[0])

        @pl.when(j == nb - 1)
        def _():
            for ii in range(nb):
                t0, t1 = dqt_scr[0, ii].T, dqt_scr[1, ii].T
                dq_ref[ii * TB:(ii + 1) * TB, :] = jnp.where(hms[0], t0, t1) * 0.125
                if fox:
                    dqa_ref[ii * TB:(ii + 1) * TB, :] = jnp.where(hms[0], t1, t0)

    seq = lambda c: pl.BlockSpec((S, 128), lambda b, p, j: (b, c + p))
    kvb = lambda c: pl.BlockSpec((TB, 128), lambda b, p, j: (b * nb + j, c + p))
    stat = pl.BlockSpec((nb, 2, 1, TB), lambda b, p, j: (b, p, 0, 0))
    in_specs = [seq(col0), kvb(col0 + NPAIR), kvb(col0 + 2 * NPAIR), pl.BlockSpec((nb + 1, TB, TB), lambda b, p, j: (0, 0, 0)),
                seq(0), stat, stat, pl.BlockSpec((1, 1, 2, 1, 128, TB), lambda b, p, j: (b, p, 0, j, 0, 0))]
    args = [p_qkv, p_qkv, p_qkv, tab, do, lse, dd, kat]
    out_shape = [_sds((T, WA), F32), _sds((T, WA), BF16), _sds((T, WA), BF16)]
    out_specs = [seq(0), kvb(0), kvb(0)]
    if fox:
        in_specs += [pl.BlockSpec((1, S, 128), lambda b, p, j: (p, b, 0)), pl.BlockSpec((1, TB, 128), lambda b, p, j: (p, b * nb + j, 0))]
        args += [aq, ak]
        out_shape += [_sds((T, WA), F32), _sds((T, WA), F32)]
        out_specs += [seq(0), kvb(0)]
    return pl.pallas_call(
        body, name=name, grid=(B, NPAIR, nb), out_shape=out_shape, in_specs=in_specs, out_specs=out_specs,
        scratch_shapes=[pltpu.VMEM((2, nb, 128, TB), F32), pltpu.VMEM((2, TB, 128), F32), pltpu.VMEM((TB, 128), F32),
                        pltpu.VMEM((2, 2, TB, TB), F32)],
        compiler_params=_cp("parallel", "parallel", "arbitrary"))(*args)


def _gate_bwd(dqa, dka, p_rest, bf, S, TB):
    T = p_rest.shape[0]
    B, nb = T // S, S // TB

    def body(dqa_ref, dka_ref, z_ref, b_ref, dz_ref, db_ref):
        @pl.when(pl.program_id(0) == 0)
        def _():
            db_ref[...] = jnp.zeros_like(db_ref)

        later = (lax.broadcasted_iota(jnp.int32, (TB, TB), 1) >= lax.broadcasted_iota(jnp.int32, (TB, TB), 0)).astype(F32)
        lane = lax.broadcasted_iota(jnp.int32, (1, 128), 1)
        src, head = lax.broadcasted_iota(jnp.int32, (WA, 128), 0), lax.broadcasted_iota(jnp.int32, (WA, 128), 1)
        spare0 = 128 * (head // 2) + HD * (1 - head % 2)
        pick_q = (src == spare0 + AUG_K1).astype(F32)
        pick_k = (src == spare0 + AUG_Q1).astype(F32)
        carry = jnp.zeros((1, 128), F32)
        dbs = jnp.zeros((1, 128), F32)
        for j in reversed(range(nb)):
            rows = slice(j * TB, (j + 1) * TB)
            dc = _dot(dqa_ref[rows, :], pick_q, HIGHEST) - _dot(dka_ref[rows, :], pick_k, HIGHEST)
            part = _dot(later, dc, HIGHEST)
            tot = part + carry
            carry = carry + part[0:1, :]
            z = z_ref[j * TB:(j + 1) * TB, :] + b_ref[...]
            dz = jnp.where(lane < 6, tot * _sigmoid(-z), 0.0)
            dz_ref[j * TB:(j + 1) * TB, :] = dz.astype(BF16)
            dbs = dbs + jnp.sum(dz, axis=0, keepdims=True)
        db_ref[...] += dbs

    return pl.pallas_call(
        body, name="gate_bwd", grid=(B,), out_shape=[_sds((T, 128), BF16), _sds((1, 128), F32)],
        in_specs=[pl.BlockSpec((S, WA), lambda b: (b, 0)), pl.BlockSpec((S, WA), lambda b: (b, 0)),
                  pl.BlockSpec((S, 128), lambda b: (b, 4)), pl.BlockSpec((1, 128), lambda b: (0, 0))],
        out_specs=[pl.BlockSpec((S, 128), lambda b: (b, 0)), pl.BlockSpec((1, 128), lambda b: (0, 0))],
        compiler_params=_cp("arbitrary"))(dqa, dka, p_rest, bf)


def _adamw(parts, w, m, v, name, row0=0, prev=None):
    npart, Rp, C = parts.shape
    R = w.shape[0]
    tr = Rp
    for cand in (256, 128, 64, 32, 16, 8):
        if Rp % cand == 0 and row0 % cand == 0 and cand * C * 4 <= (1 << 20):
            tr = cand
            break
    if Rp == R and R * C * 4 <= (1 << 20):
        tr = R
    assert Rp % tr == 0 and row0 % tr == 0, (name, Rp, row0, tr)
    b0 = row0 // tr

    def body(p_ref, w_ref, m_ref, v_ref, *rest):
        g_ref, d_ref, mo_ref, vo_ref = rest[-4:]
        g = p_ref[0].astype(F32)
        for k in range(1, npart):
            g = g + p_ref[k].astype(F32)
        mn = ADAM_B1 * m_ref[...] + (1.0 - ADAM_B1) * g
        vn = ADAM_B2 * v_ref[...] + (1.0 - ADAM_B2) * (g * g)
        m_hat = mn / (1.0 - ADAM_B1 ** ADAM_STEP)
        v_hat = vn / (1.0 - ADAM_B2 ** ADAM_STEP)
        g_ref[...] = g
        d_ref[...] = -ADAM_LR * (m_hat / (jnp.sqrt(v_hat) + ADAM_EPS) + ADAM_WD * w_ref[...])
        mo_ref[...] = mn
        vo_ref[...] = vn

    blk = pl.BlockSpec((tr, C), lambda i: (b0 + i, 0))
    prev = list(prev) if prev is not None else []
    return pl.pallas_call(
        body, name=name, grid=(Rp // tr,), out_shape=[_sds((R, C), F32)] * 4,
        in_specs=[pl.BlockSpec((npart, tr, C), lambda i: (0, i, 0)), blk, blk, blk] + [_ANY] * len(prev), out_specs=[blk] * 4,
        input_output_aliases={4 + i: i for i in range(len(prev))},
        compiler_params=_cp("parallel"))(parts, w, m, v, *prev)


def _pad_w_in(w):
    z = jnp.zeros(w.shape[:-1] + (NPAD - N_IN,), w.dtype)
    return jnp.concatenate([w[..., :O_FA], w[..., O_QB:], w[..., O_FA:O_QB], z], axis=-1)


def _unpad_w_in(w):
    n_mid = N_IN - O_QB
    return jnp.concatenate([w[..., :O_FA], w[..., O_FA + n_mid:O_FA + n_mid + 6], w[..., O_FA:O_FA + n_mid]], axis=-1)


def _local_step(x, tgt, W, S, late_weights=None, on_layer_grads=None, on_ffn_grads=None):
    T = x.shape[0]
    TB = min(256, S)
    TQ = min(2 * TB, S)
    tab_fox, tab_dil = _bias_table(S, TB, TB, True), _bias_table(S, TB, TB, False)
    tabq_fox, tabq_dil = _bias_table(S, TB, TQ, True), _bias_table(S, TB, TQ, False)
    saved = []
    h = _rms_fwd(x, W["ln1_g"][0])
    for l in range(DEPTH):
        p_qkv, p_rest = _mm_in(h, W["w_in"][l])
        aq, ak = _gate_fwd(p_rest, W["b_forget"][l], S, TB)
        o_a, lse_a, kat_a = _attn_fwd(p_qkv, 0, tabq_fox, S, aq, ak, name="fox_fwd")
        o_b, lse_b, kat_b = _attn_fwd(p_qkv, 3 * NPAIR, tabq_dil, S, name="dil_fwd")
        y = _mix_fwd(o_a, o_b, p_rest, W["g_out_fox"][l], W["g_out_dil"][l], W["conv_w"][l], W["conv_b"][l],
                     W["cnorm_g"][l], W["cnorm_b"][l], S)
        if l == 0 and late_weights is not None:
            W = late_weights(W, y)
        x1, h2 = _mm_res(y[None], W["w_o"][l][None], x, W["ln2_g"][l], name="mm_o")
        u, hid = _up_fwd(h2, W["w_up"][l], W["ffn_conv_w"][l], W["ffn_conv_b"][l], S)
        g_next = W["ln1_g"][l + 1] if l + 1 < DEPTH else W["g_final"]
        x2, h_next = _mm_res(hid, W["w_down"][l], x1, g_next, name="mm_down")
        saved.append(dict(x=x, h=h, p_qkv=p_qkv, p_rest=p_rest, aq=aq, ak=ak, o_a=o_a, lse_a=lse_a, o_b=o_b,
                          lse_b=lse_b, kat_a=kat_a, kat_b=kat_b, y=y, x1=x1, h2=h2, u=u, hid=hid))
        x, h = x2, h_next
    dx, loss, dg_final = _final_loss(x, tgt, W["g_final"])
    G = {k: [None] * DEPTH for k in ("ln1_g", "w_in", "b_forget", "g_out_fox", "g_out_dil", "conv_w", "conv_b", "cnorm_g",
                                     "cnorm_b", "w_o", "ln2_g", "w_up", "ffn_conv_w", "ffn_conv_b", "w_down")}
    G["g_final"] = dg_final
    for l in reversed(range(DEPTH)):
        sv = saved[l]
        du, dcw = _down_bwd(dx, W["w_down"][l], sv["u"], W["ffn_conv_w"][l], W["ffn_conv_b"][l], S)
        G["w_down"][l] = _wgrad(sv["hid"], dx[None], D, "wg_down", ga=NJ)[:, 0]
        G["ffn_conv_w"][l] = dcw[:, :, 0:3, :]
        G["ffn_conv_b"][l] = dcw[:, :, 3, :]
        G["w_up"][l] = _wgrad(sv["h2"][None], du.reshape(2 * NJ, T, FB), FB, "wg_up", gb=NJ)[0]
        token = on_ffn_grads(l, G) if on_ffn_grads is not None else None
        ln2 = W["ln2_g"][l] if token is None else W["ln2_g"][l] + token[0, 0]
        dx1, G["ln2_g"][l] = _dx_rms(du, W["w_up"][l], sv["x1"], ln2, dx, True, "dx_up")
        G["w_o"][l] = _wgrad(sv["y"][None], dx1[None], D, "wg_o")[0, 0]
        dy = _mm_nt(dx1, W["w_o"][l], "mm_dy")
        do_a, dd_a, do_b, dd_b, dgvgg, dgn, dcvw, dcn = _mix_bwd(
            dy, sv["o_a"], sv["o_b"], sv["p_rest"], W["g_out_fox"][l], W["g_out_dil"][l], W["conv_w"][l], W["conv_b"][l],
            W["cnorm_g"][l], W["cnorm_b"][l], S)
        G["g_out_fox"][l], G["g_out_dil"][l] = dgn[0], dgn[1]
        G["conv_w"][l], G["conv_b"][l] = dcvw[:CONV_K], dcvw[CONV_K]
        G["cnorm_g"][l], G["cnorm_b"][l] = dcn[0], dcn[1]
        dq_a, dk_a, dv_a, dqa, dka = _attn_bwd(sv["p_qkv"], 0, tab_fox, do_a, sv["lse_a"], dd_a, sv["kat_a"], S, sv["aq"], sv["ak"],
                                               name="fox_bwd")
        dq_b, dk_b, dv_b = _attn_bwd(sv["p_qkv"], 3 * NPAIR, tab_dil, do_b, sv["lse_b"], dd_b, sv["kat_b"], S, name="dil_bwd")
        dfa, db = _gate_bwd(dqa, dka, sv["p_rest"], W["b_forget"][l], S, TB)
        G["b_forget"][l] = db[0, :6]
        dp = jnp.concatenate([dq_a.astype(BF16), dk_a, dv_a, dq_b.astype(BF16), dk_b, dv_b, dgvgg, dfa,
                              jnp.zeros((T, 128), BF16)], axis=1)
        G["w_in"][l] = _wgrad(sv["h"][None], dp[None], NPAD, "wg_in")[0, 0]
        token = on_layer_grads(l, G) if on_layer_grads is not None else None
        ln1 = W["ln1_g"][l] if token is None else W["ln1_g"][l] + token[0, 0]
        dx, G["ln1_g"][l] = _dx_rms(dp, W["w_in"][l], sv["x"], ln1, dx1, False, "dx_in")
    return loss, dx, G


_MATMUL_W = ("w_in", "w_o", "w_up", "w_down")
_SHARDED = _MATMUL_W + ("conv_w", "ffn_conv_w")
_SMALL = ("ln1_g", "b_forget", "g_out_fox", "g_out_dil", "conv_b", "cnorm_g", "cnorm_b", "ln2_g", "ffn_conv_b", "g_final")
_ORDER = ("ln1_g", "w_in", "b_forget", "g_out_fox", "g_out_dil", "conv_w", "conv_b", "cnorm_g", "cnorm_b", "w_o", "ln2_g",
          "w_up", "ffn_conv_w", "ffn_conv_b", "w_down", "g_final")


def _kernel_ready(k, zone):
    if k == "w_in":
        return _pad_w_in(zone.reshape(D, N_IN))
    if k == "w_o":
        return zone.reshape(D, D)
    if k == "w_up":
        return zone.reshape(2, NJ, D, FB)
    if k == "w_down":
        return zone.reshape(NJ, FB, D)
    if k == "conv_w":
        cw = jnp.transpose(zone.reshape(NDEV, CONV_K, CC // NDEV), (1, 0, 2)).reshape(CONV_K, CC)
        return jnp.concatenate([cw, jnp.zeros((1, CC), F32)], axis=0)
    return zone.reshape(2, NJ, 3, FB)


def _full_weights(P, gathered):
    W = {}
    row = lambda a: [a[l][None, :] for l in range(DEPTH)]
    for k in _SHARDED:
        W[k] = [None if gathered[k][l] is None else _kernel_ready(k, gathered[k][l]) for l in range(DEPTH)]
    W["ffn_conv_b"] = [P["ffn_conv_b"][l].reshape(2, NJ, 1, FB) for l in range(DEPTH)]
    W["b_forget"] = [jnp.concatenate([P["b_forget"][l], jnp.zeros((128 - 6,), F32)])[None, :] for l in range(DEPTH)]
    for k in ("ln1_g", "ln2_g", "g_out_fox", "g_out_dil", "conv_b", "cnorm_g", "cnorm_b"):
        W[k] = row(P[k])
    W["g_final"] = P["g_final"][None, :]
    return W


def kernel(x, ln1_g, w_in, b_forget, g_out_fox, g_out_dil, conv_w, conv_b, cnorm_g, cnorm_b, w_o, ln2_g, w_up, ffn_conv_w, ffn_conv_b, w_down, g_final, loss_target, m_ln1_g, m_w_in, m_b_forget, m_g_out_fox, m_g_out_dil, m_conv_w, m_conv_b, m_cnorm_g, m_cnorm_b, m_w_o, m_ln2_g, m_w_up, m_ffn_conv_w, m_ffn_conv_b, m_w_down, m_g_final, v_ln1_g, v_w_in, v_b_forget, v_g_out_fox, v_g_out_dil, v_conv_w, v_conv_b, v_cnorm_g, v_cnorm_b, v_w_o, v_ln2_g, v_w_up, v_ffn_conv_w, v_ffn_conv_b, v_w_down, v_g_final):
    P = dict(ln1_g=ln1_g, w_in=w_in, b_forget=b_forget, g_out_fox=g_out_fox, g_out_dil=g_out_dil, conv_w=conv_w, conv_b=conv_b,
             cnorm_g=cnorm_g, cnorm_b=cnorm_b, w_o=w_o, ln2_g=ln2_g, w_up=w_up, ffn_conv_w=ffn_conv_w, ffn_conv_b=ffn_conv_b,
             w_down=w_down, g_final=g_final)
    M = dict(ln1_g=m_ln1_g, w_in=m_w_in, b_forget=m_b_forget, g_out_fox=m_g_out_fox, g_out_dil=m_g_out_dil, conv_w=m_conv_w,
             conv_b=m_conv_b, cnorm_g=m_cnorm_g, cnorm_b=m_cnorm_b, w_o=m_w_o, ln2_g=m_ln2_g, w_up=m_w_up, ffn_conv_w=m_ffn_conv_w,
             ffn_conv_b=m_ffn_conv_b, w_down=m_w_down, g_final=m_g_final)
    V = dict(ln1_g=v_ln1_g, w_in=v_w_in, b_forget=v_b_forget, g_out_fox=v_g_out_fox, g_out_dil=v_g_out_dil, conv_w=v_conv_w,
             conv_b=v_conv_b, cnorm_g=v_cnorm_g, cnorm_b=v_cnorm_b, w_o=v_w_o, ln2_g=v_ln2_g, w_up=v_w_up, ffn_conv_w=v_ffn_conv_w,
             ffn_conv_b=v_ffn_conv_b, w_down=v_w_down, g_final=v_g_final)
    Bl, S, _ = x.shape
    T = Bl * S

    shard = lambda k, l: P[k][l].astype(BF16) if k in _MATMUL_W else P[k][l]
    early = [("w_in", 0)] + [(k, l) for k in ("conv_w", "ffn_conv_w") for l in range(DEPTH)]
    late = [(k, 0) for k in ("w_o", "w_up", "w_down")] + [(k, 1) for k in _MATMUL_W]
    gathered = {k: [None] * DEPTH for k in _SHARDED}
    plan_early = _Plan([[shard(k, l)] for k, l in early], True)
    for (k, l), zone in zip(early, _exchange(plan_early, "gather_early")):
        gathered[k][l] = zone
    plan_late = _Plan([[shard(k, l)] for k, l in late], True)
    late_handle, late_token = _exchange_start(plan_late, "gather_late_start", plan_late.zones_with_own())
    W = _full_weights(P, gathered)
    W["ln1_g"][0] = W["ln1_g"][0] + late_token[0, 0]

    def late_weights(W, after):
        zones = _exchange_wait(plan_late, "gather_late_wait", late_handle, after)
        W = dict(W)
        for (k, l), zone in zip(late, zones):
            W[k] = list(W[k])
            W[k][l] = _kernel_ready(k, zone)
        return W

    def owner_block(G, l, k):
        if k == "w_in":
            blk = _unpad_w_in(G[k][l]).reshape(NDEV, D // NDEV, N_IN)
        elif k == "w_o":
            blk = G[k][l].reshape(NDEV, D // NDEV, D)
        elif k == "w_up":
            blk = G[k][l]
        elif k == "w_down":
            blk = G[k][l].reshape(NDEV, DFF // NDEV, D)
        elif k == "conv_w":
            blk = jnp.transpose(G[k][l].reshape(CONV_K, NDEV, CC // NDEV), (1, 0, 2))
        else:
            blk = G[k][l].reshape(NDEV, 3, FB)
        return blk.astype(GRAD_DTYPE)

    flying = []

    def start_scatter(tag, G, l, keys):
        plan = _Plan([[owner_block(G, l, k)] for k in keys], False)
        handle, token = _exchange_start(plan, "scatter_%s_start" % tag, plan.zones_with_own())
        flying.append((tag, l, keys, plan, handle))
        return token

    def on_layer_grads(l, G):
        return start_scatter("l1", G, l, _MATMUL_W) if l == 1 else start_scatter("l0_rest", G, l, ("w_in", "w_o"))

    def on_ffn_grads(l, G):
        return start_scatter("l0_ffn", G, l, ("w_up", "w_down")) if l == 0 else None

    loss, dx, G = _local_step(x.reshape(T, D), loss_target.reshape(T, D), W, S, late_weights, on_layer_grads, on_ffn_grads)

    parts = {}
    for tag, l, keys, plan, handle in flying:
        for k, zone in zip(keys, _exchange_wait(plan, "scatter_%s_wait" % tag, handle, dx)):
            parts[(k, l)] = zone
    plan_last = _Plan([[owner_block(G, l, k) for l in range(DEPTH)] for k in ("conv_w", "ffn_conv_w")], False)
    last = _exchange(plan_last, "scatter_last")
    small_parts_sharded = {"conv_w": last[0], "ffn_conv_w": last[1]}

    small = [jnp.stack(G[k]).reshape(-1) for k in _SMALL[:-1]] + [G["g_final"].reshape(-1), loss[0, :1]]
    sizes = [int(s.shape[0]) for s in small]
    flat = jnp.concatenate(small)
    n_small = -(-flat.shape[0] // 1024) * 1024
    flat = jnp.concatenate([flat, jnp.zeros((n_small - flat.shape[0],), F32)]).reshape(n_small // 128, 128)
    small_parts = _exchange(_Plan([[flat]], True), "gather_small")[0]

    out_g, out_d, out_m, out_v = {}, {}, {}, {}
    sg = _sum_parts(small_parts.reshape(NDEV, n_small // 128, 128), "sum_small").reshape(-1)
    off = 0
    summed = {}
    for k, n in zip(_SMALL + ("loss",), sizes):
        summed[k] = sg[off:off + n]
        off += n
    for k in _ORDER:
        shp = P[k].shape
        C = shp[-1]
        if k in _MATMUL_W:
            R0 = shp[1]
            flat2 = lambda t: t.reshape(DEPTH * R0, C)
            res = None
            for l in reversed(range(DEPTH)):
                res = _adamw(parts[(k, l)].reshape(NDEV, R0, C), flat2(P[k]), flat2(M[k]), flat2(V[k]),
                             "adamw_%s_l%d" % (k, l), row0=l * R0, prev=res)
        else:
            pr = small_parts_sharded[k].reshape((NDEV, -1, C)) if k in _SHARDED else summed[k].reshape((1, -1, C))
            R = pr.shape[1]
            res = _adamw(pr, P[k].reshape(R, -1), M[k].reshape(R, -1), V[k].reshape(R, -1), "adamw_" + k)
        out_g[k], out_d[k], out_m[k], out_v[k] = (t.reshape(shp) for t in res)

    loss_out = summed["loss"].reshape(())
    grad_x = dx.reshape(Bl, S, D)
    return (loss_out, grad_x, *[out_g[k] for k in _ORDER], *[out_d[k] for k in _ORDER], *[out_m[k] for k in _ORDER],
            *[out_v[k] for k in _ORDER])
```

```python
import functools

import numpy as np
import jax
import jax.numpy as jnp
from jax import lax
from jax.experimental import pallas as pl
from jax.experimental.pallas import tpu as pltpu

F32 = jnp.float32
BF16 = jnp.bfloat16
GRAD_DTYPE = BF16
HIGHEST = lax.Precision.HIGHEST

D = 1024
DEPTH = 2
HD = 64
WA = 384
NPAIR = 3
CC = 256
CONV_K = 31
DFF = 2816
FB = 704
NJ = 4
NDEV = 8
EPS = 1e-6
NQKV = 2304
NREST = 768
NPAD = NQKV + NREST
O_FA, O_QB, N_IN = 1152, 1158, 2822
DILATION_PAIRS = ((128, 1), (512, 4), (2048, 16))
NEG = -1e30

ADAM_LR, ADAM_B1, ADAM_B2, ADAM_EPS, ADAM_WD, ADAM_STEP = 0.001, 0.9, 0.999, 1e-08, 0.01, 10

VMEM_LIMIT = 48 * 1024 * 1024
VMEM_LIMIT_BIG = 56 * 1024 * 1024


def _cp(*sem):
    return pltpu.CompilerParams(dimension_semantics=sem, vmem_limit_bytes=VMEM_LIMIT)


def _sds(shape, dtype):
    return jax.ShapeDtypeStruct(shape, dtype)


def _dot(a, b, prec=None):
    return jnp.dot(a, b, preferred_element_type=F32, precision=prec)


def _dot_nt(a, b, prec=None):
    return lax.dot_general(a, b, (((1,), (1,)), ((), ())), preferred_element_type=F32, precision=prec)


def _dot_tn(a, b):
    return lax.dot_general(a, b, (((0,), (0,)), ((), ())), preferred_element_type=F32)


def _sigmoid(z):
    return 0.5 * jnp.tanh(0.5 * z) + 0.5


def _rms_bwd(dh, x, g):
    r = lax.rsqrt(jnp.mean(x * x, axis=-1, keepdims=True) + EPS)
    xh = x * r
    dg = jnp.sum(dh * xh, axis=0, keepdims=True)
    dxh = dh * g
    dx = r * (dxh - xh * jnp.mean(dxh * xh, axis=-1, keepdims=True))
    return dx, dg


class _Plan:
    def __init__(self, groups, gather, slots=None, n_slots=None):
        self.groups, self.gather = groups, gather
        self.slots = slots or [list(range(len(g))) for g in groups]
        self.n_slots = n_slots or [len(g) for g in groups]
        self.flat = [a for g in groups for a in g]
        self.where = [(gi, s) for gi, g in enumerate(groups) for s in self.slots[gi]]
        self.zone_shapes = [_sds((NDEV, ns) + (g[0].shape if gather else g[0].shape[1:]), g[0].dtype)
                            for g, ns in zip(groups, self.n_slots)]

    def new_zones(self):
        return [lax.empty(z.shape, z.dtype) for z in self.zone_shapes]

    def me(self):
        x, y, c = lax.axis_index("x"), lax.axis_index("y"), lax.axis_index("c")
        return 4 * x + 2 * y + c

    def zones_with_own(self):
        me = self.me()
        zones = self.new_zones()
        for a, (gi, s) in enumerate(self.where):
            src = self.flat[a] if self.gather else lax.dynamic_index_in_dim(self.flat[a], me, 0, keepdims=False)
            zones[gi] = lax.dynamic_update_slice(zones[gi], src[None, None], (me, s) + (0,) * src.ndim)
        return zones

    def own_copies(self, ins, zones, sems):
        me = self.me()
        return [pltpu.make_async_copy(ins[a] if self.gather else ins[a].at[me], zones[gi].at[me, s], sems.at[a])
                for a, (gi, s) in enumerate(self.where)]

    def remote_copies(self, ins, zones, send_sems, recv_sems):
        x, y, c = lax.axis_index("x"), lax.axis_index("y"), lax.axis_index("c")
        me = 4 * x + 2 * y + c
        out = []
        for a, (gi, s) in enumerate(self.where):
            for k in range(1, NDEV):
                px, py, pc = x ^ ((k >> 2) & 1), y ^ ((k >> 1) & 1), c ^ (k & 1)
                out.append(pltpu.make_async_remote_copy(
                    src_ref=ins[a] if self.gather else ins[a].at[4 * px + 2 * py + pc], dst_ref=zones[gi].at[me, s],
                    send_sem=send_sems.at[a * (NDEV - 1) + k - 1], recv_sem=recv_sems.at[a * (NDEV - 1) + k - 1],
                    device_id=(px, py, pc), device_id_type=pl.DeviceIdType.MESH))
        return out


_ANY = pl.BlockSpec(memory_space=pl.ANY)
_HBM = pl.BlockSpec(memory_space=pltpu.HBM)
_SEM = pl.BlockSpec(memory_space=pltpu.SEMAPHORE)


def _exchange(plan, name, zones=None):
    n, nz = len(plan.flat), len(plan.groups)
    zones = zones or plan.new_zones()

    def body(*refs):
        ins, zin = refs[:n], refs[n:n + nz]
        send_sems, recv_sems, local_sems = refs[n + 2 * nz:]
        copies = plan.own_copies(ins, zin, local_sems) + plan.remote_copies(ins, zin, send_sems, recv_sems)
        for cp in copies:
            cp.start()
        for cp in copies:
            cp.wait()

    return pl.pallas_call(
        body, name=name, out_shape=plan.zone_shapes, in_specs=[_ANY] * (n + nz), out_specs=[_ANY] * nz,
        input_output_aliases={n + g: g for g in range(nz)},
        scratch_shapes=[pltpu.SemaphoreType.DMA((n * (NDEV - 1),)), pltpu.SemaphoreType.DMA((n * (NDEV - 1),)),
                        pltpu.SemaphoreType.DMA((n,))],
        compiler_params=pltpu.CompilerParams(has_side_effects=True),
    )(*plan.flat, *zones)


def _exchange_start(plan, name, zones):
    n, nz = len(plan.flat), len(plan.groups)
    hbm = lambda a: pltpu.HBM(a.shape, a.dtype)

    def body(*refs):
        ins, zin = refs[:n], refs[n:n + nz]
        send_sems, recv_sems = refs[n + nz], refs[n + nz + 1]
        token = refs[-1]
        for cp in plan.remote_copies(ins, zin, send_sems, recv_sems):
            cp.start()
        token[...] = jnp.zeros_like(token)

    outs = pl.pallas_call(
        body, name=name,
        out_shape=(pltpu.SemaphoreType.DMA((n * (NDEV - 1),)), pltpu.SemaphoreType.DMA((n * (NDEV - 1),)),
                   *[hbm(a) for a in plan.flat], *[hbm(z) for z in plan.zone_shapes], _sds((8, 128), F32)),
        in_specs=[_HBM] * (n + nz),
        out_specs=(_SEM, _SEM, *[_HBM] * (n + nz), pl.BlockSpec(memory_space=pltpu.VMEM)),
        input_output_aliases={i: 2 + i for i in range(n + nz)},
        compiler_params=pltpu.CompilerParams(has_side_effects=pltpu.SideEffectType.DATAFLOW_SIDE_EFFECTING),
    )(*[pltpu.with_memory_space_constraint(a, pltpu.HBM) for a in plan.flat],
      *[pltpu.with_memory_space_constraint(z, pltpu.HBM) for z in zones])
    return outs[:-1], outs[-1]


def _exchange_wait(plan, name, handle, after):
    n, nz = len(plan.flat), len(plan.groups)
    send_sems, recv_sems = handle[0], handle[1]
    thru = handle[2:]
    hbm = lambda a: pltpu.HBM(a.shape, a.dtype)

    def body(*refs):
        ins, zin = refs[:n], refs[n:n + nz]
        ssem, rsem = refs[n + nz], refs[n + nz + 1]
        for cp in plan.remote_copies(ins, zin, ssem, rsem):
            cp.wait_send()
            cp.wait_recv()

    outs = pl.pallas_call(
        body, name=name, out_shape=tuple(hbm(a) for a in thru),
        in_specs=[_HBM] * (n + nz) + [_SEM, _SEM, _ANY], out_specs=tuple([_HBM] * (n + nz)),
        input_output_aliases={i: i for i in range(n + nz)},
        compiler_params=pltpu.CompilerParams(has_side_effects=pltpu.SideEffectType.DATAFLOW_SIDE_EFFECTING),
    )(*thru, send_sems, recv_sems, after)
    return list(outs[n:])


def _sum_parts(parts, name):
    npart, R, C = parts.shape

    def body(p_ref, o_ref):
        s = p_ref[0]
        for k in range(1, npart):
            s = s + p_ref[k]
        o_ref[...] = s

    return pl.pallas_call(body, name=name, out_shape=_sds((R, C), F32))(parts)


def _rms_fwd(x, g):
    T = x.shape[0]
    tm = min(512, T)

    def body(x_ref, g_ref, o_ref):
        xv = x_ref[...]
        r = lax.rsqrt(jnp.mean(xv * xv, axis=-1, keepdims=True) + EPS)
        o_ref[...] = (xv * r * g_ref[...]).astype(BF16)

    return pl.pallas_call(
        body, name="rms_fwd", grid=(T // tm,), out_shape=_sds((T, D), BF16),
        in_specs=[pl.BlockSpec((tm, D), lambda i: (i, 0)), pl.BlockSpec((1, D), lambda i: (0, 0))],
        out_specs=pl.BlockSpec((tm, D), lambda i: (i, 0)), compiler_params=_cp("parallel"))(x, g)


def _mm_in(h, w):
    T = h.shape[0]
    tm = min(512, T)

    def body(h_ref, w_ref, q_ref, r_ref):
        hv = h_ref[...]
        q_ref[...] = _dot(hv, w_ref[:, :NQKV]).astype(BF16)
        r_ref[...] = _dot(hv, w_ref[:, NQKV:])

    return pl.pallas_call(
        body, name="mm_in", grid=(T // tm,), out_shape=[_sds((T, NQKV), BF16), _sds((T, NREST), F32)],
        in_specs=[pl.BlockSpec((tm, D), lambda i: (i, 0)), pl.BlockSpec((D, NPAD), lambda i: (0, 0))],
        out_specs=[pl.BlockSpec((tm, NQKV), lambda i: (i, 0)), pl.BlockSpec((tm, NREST), lambda i: (i, 0))],
        compiler_params=_cp("parallel"))(h, w)


AUG_Q1, AUG_K1 = 3, 0


def _aug_lane0(h):
    return HD * (1 - h % 2)


def _aug_tables():
    selq = np.zeros((NPAIR, 3 * 128, 128), np.float32)
    selk = np.zeros((NPAIR, 3 * 128, 128), np.float32)
    oneq = np.zeros((NPAIR, 1, 128), np.float32)
    onek = np.zeros((NPAIR, 1, 128), np.float32)
    for h in range(2 * NPAIR):
        p, l0 = h // 2, _aug_lane0(h)
        for piece in range(3):
            selq[p, 128 * piece + h, l0 + piece] = 1.0
            selk[p, 128 * piece + h, l0 + 3 + piece] = -1.0
            oneq[p, 0, l0 + AUG_Q1 + piece] = 1.0
            onek[p, 0, l0 + AUG_K1 + piece] = 1.0
    return [jnp.asarray(a, BF16) for a in (selq, selk, oneq, onek)]


def _gate_fwd(p_rest, bf, S, TB):
    T = p_rest.shape[0]
    B, nb = T // S, S // TB
    selq, selk, oneq, onek = _aug_tables()

    def body(z_ref, b_ref, sq_ref, sk_ref, oq_ref, ok_ref, aq_ref, ak_ref):
        tri = (lax.broadcasted_iota(jnp.int32, (TB, TB), 0) >= lax.broadcasted_iota(jnp.int32, (TB, TB), 1)).astype(F32)
        carry = jnp.zeros((1, 128), F32)
        for j in range(nb):
            rows = slice(j * TB, (j + 1) * TB)
            z = z_ref[rows, :] + b_ref[...]
            lf = jnp.minimum(z, 0.0) - jnp.log(1.0 + jnp.exp(-jnp.abs(z)))
            cb = _dot(tri, lf, HIGHEST) + carry
            carry = cb[TB - 1:TB, :]
            hi = cb.astype(BF16)
            mid = (cb - hi.astype(F32)).astype(BF16)
            lo = (cb - hi.astype(F32) - mid.astype(F32)).astype(BF16)
            pieces = jnp.concatenate([hi, mid, lo], axis=1)
            for p in range(NPAIR):
                aq_ref[p, rows, :] = (_dot(pieces, sq_ref[p]) + oq_ref[p].astype(F32)).astype(BF16)
                ak_ref[p, rows, :] = (_dot(pieces, sk_ref[p]) + ok_ref[p].astype(F32)).astype(BF16)

    full = lambda a: pl.BlockSpec(a.shape, lambda b: (0,) * a.ndim)
    return pl.pallas_call(
        body, name="gate_fwd", grid=(B,), out_shape=[_sds((NPAIR, T, 128), BF16), _sds((NPAIR, T, 128), BF16)],
        in_specs=[pl.BlockSpec((S, 128), lambda b: (b, 4)), pl.BlockSpec((1, 128), lambda b: (0, 0)),
                  full(selq), full(selk), full(oneq), full(onek)],
        out_specs=[pl.BlockSpec((NPAIR, S, 128), lambda b: (0, b, 0)), pl.BlockSpec((NPAIR, S, 128), lambda b: (0, b, 0))],
        compiler_params=_cp("parallel"))(p_rest, bf, selq, selk, oneq, onek)


def _bias_table(S, n_key, n_query, fox):
    unit = min(n_key, n_query)
    d_min = -(n_query // unit - 1)
    d = np.arange(d_min, S // unit)[:, None, None]
    delta = d * unit + np.arange(n_query)[None, None, :] - np.arange(n_key)[None, :, None]
    if fox:
        mult = (delta >= 0).astype(np.float64)
    else:
        mult = np.zeros(delta.shape)
        for w, dil in DILATION_PAIRS:
            mult += (delta >= 0) & (delta % dil == 0) & (delta <= w)
    with np.errstate(divide="ignore"):
        tab = np.where(mult > 0, np.log(np.maximum(mult, 1e-30)), NEG)
    tab = np.concatenate([tab, np.full((1, n_key, n_query), NEG)], axis=0)
    return jnp.asarray(tab, F32)


def _head_masks():
    lane = lax.broadcasted_iota(jnp.int32, (1, 128), 1)
    return [lane < HD, lane >= HD]


def _transpose_bf16(a):
    return a.astype(F32).T.astype(BF16)


def _col_reduce(x, op):
    while x.shape[0] > 8:
        half = x.shape[0] // 2
        x = op(x[:half], x[half:])
    return jnp.max(x, axis=0, keepdims=True) if op is jnp.maximum else jnp.sum(x, axis=0, keepdims=True)


def _attn_fwd(p_qkv, col0, tab, S, aq=None, ak=None, name="attn"):
    T = p_qkv.shape[0]
    n_tab, TB, TQ = tab.shape
    nb, nq, r = S // TB, S // TQ, TQ // TB
    B = T // S
    fox = aq is not None

    def body(*refs):
        if fox:
            q_ref, k_ref, v_ref, tab_ref, aq_ref, ak_ref, o_ref, l_ref, kat_ref, vta_scr, acc_scr, st_scr = refs
        else:
            q_ref, k_ref, v_ref, tab_ref, o_ref, l_ref, kat_ref, vta_scr, acc_scr, st_scr = refs
        i = pl.program_id(2)
        hms = _head_masks()
        top = lax.broadcasted_iota(jnp.int32, (128, 1), 0) < HD
        last_key_block = r * i + r - 1

        @pl.when(i == 0)
        def _():
            for jj in range(nb):
                rows = slice(jj * TB, (jj + 1) * TB)
                vt = _transpose_bf16(v_ref[rows, :])
                for hh in range(2):
                    vta_scr[hh, jj] = jnp.where(top == (hh == 0), vt, jnp.ones_like(vt))
                    spare_k = ak_ref[0, rows, :] if fox else jnp.zeros((TB, 128), BF16)
                    kat_ref[0, 0, hh, jj] = _transpose_bf16(jnp.where(hms[hh], k_ref[rows, :], spare_k))

        q2 = q_ref[...] * 0.125
        spare_q = aq_ref[0] if fox else jnp.zeros_like(q2)
        qa = [jnp.where(hm, q2, spare_q) for hm in hms]
        acc_scr[...] = jnp.zeros_like(acc_scr)

        def scores(j):
            jc = jnp.minimum(j, nb - 1)
            rows = pl.ds(pl.multiple_of(jc * TB, TB), TB)
            k2 = k_ref[rows, :]
            bias = tab_ref[jnp.where(j <= last_key_block, r * i - j + (r - 1), n_tab - 1)]
            return [_dot_nt(jnp.where(hms[hh], k2, ak_ref[0, rows, :]) if fox else k2, qa[hh]) + bias for hh in range(2)]

        def absorb(j, sts, stats):
            jc = jnp.minimum(j, nb - 1)
            alphas, pts = [], []
            for hh in range(2):
                m_old = stats[2 * hh]
                m_new = jnp.maximum(m_old, _col_reduce(sts[hh], jnp.maximum))
                pts.append(jnp.exp(sts[hh] - m_new).astype(BF16))
                alphas.append(jnp.exp(m_old - m_new))
                stats[2 * hh] = m_new
            pvs = [_dot(vta_scr[hh, jc], pts[hh]) for hh in range(2)]
            for hh in range(2):
                half = slice(HD * hh, HD * (hh + 1))
                ones_row = HD * (1 - hh)
                acc_scr[half, :] = alphas[hh] * acc_scr[half, :] + pvs[hh][half]
                stats[2 * hh + 1] = alphas[hh] * stats[2 * hh + 1] + pvs[hh][ones_row:ones_row + 1]

        def kv_pair(jj, carry):
            stats = list(carry)
            j0 = 2 * jj
            st0 = [st_scr[hh] for hh in range(2)]
            st1 = scores(j0 + 1)
            absorb(j0, st0, stats)
            nxt = scores(j0 + 2)
            for hh in range(2):
                st_scr[hh] = nxt[hh]
            absorb(j0 + 1, st1, stats)
            return tuple(stats)

        first = scores(0)
        for hh in range(2):
            st_scr[hh] = first[hh]
        row = lambda v: jnp.full((1, TQ), v, F32)
        m0, l0, m1, l1 = lax.fori_loop(0, (last_key_block + 2) // 2, kv_pair, (row(NEG), row(0.0), row(NEG), row(0.0)))
        o_ref[...] = (acc_scr[...] * jnp.where(top, 1.0 / l0, 1.0 / l1)).T
        for hh, lse in enumerate((m0 + jnp.log(l0), m1 + jnp.log(l1))):
            for t in range(r):
                l_ref[t, hh] = lse[:, t * TB:(t + 1) * TB]

    in_specs = [pl.BlockSpec((TQ, 128), lambda b, p, i: (b * nq + i, col0 + p)),
                pl.BlockSpec((S, 128), lambda b, p, i: (b, col0 + NPAIR + p)),
                pl.BlockSpec((S, 128), lambda b, p, i: (b, col0 + 2 * NPAIR + p)),
                pl.BlockSpec(tab.shape, lambda b, p, i: (0, 0, 0))]
    args = [p_qkv, p_qkv, p_qkv, tab]
    if fox:
        in_specs += [pl.BlockSpec((1, TQ, 128), lambda b, p, i: (p, b * nq + i, 0)),
                     pl.BlockSpec((1, S, 128), lambda b, p, i: (p, b, 0))]
        args += [aq, ak]
    return pl.pallas_call(
        body, name=name, grid=(B, NPAIR, nq),
        out_shape=[_sds((T, WA), F32), _sds((T // TB, 2 * NPAIR, 1, TB), F32), _sds((B, NPAIR, 2, nb, 128, TB), BF16)],
        in_specs=in_specs,
        out_specs=[pl.BlockSpec((TQ, 128), lambda b, p, i: (b * nq + i, p)),
                   pl.BlockSpec((r, 2, 1, TB), lambda b, p, i: (b * nq + i, p, 0, 0)),
                   pl.BlockSpec((1, 1, 2, nb, 128, TB), lambda b, p, i: (b, p, 0, 0, 0, 0))],
        scratch_shapes=[pltpu.VMEM((2, nb, 128, TB), BF16), pltpu.VMEM((128, TQ), F32), pltpu.VMEM((2, TB, TQ), F32)],
        compiler_params=_cp("parallel", "parallel", "arbitrary"))(*args)


def _phase_copies(src, phases, length):
    for r in range(1, 8):
        phases[r, 0:length - 8, :] = src[pl.ds(r, length - 8), :]


def _tap(src, phases, offset, rows):
    r = offset % 8
    return src[pl.ds(offset, rows), :] if r == 0 else phases[r, pl.ds(offset - r, rows), :]


def _conv_taps(src, phases, w_ref, first_row, rows):
    acc = w_ref[0:1, :] * _tap(src, phases, first_row, rows)
    for k in range(1, CONV_K):
        acc = acc + w_ref[k:k + 1, :] * _tap(src, phases, first_row + k, rows)
    return acc


def _mix_fwd(o_a, o_b, p_rest, g_a, g_b, cw, cb, ng, nbias, S):
    T = o_a.shape[0]
    tm = min(256, S)
    nps = S // tm
    HALO = 32

    def body(oa_ref, ob_ref, pt_ref, ph_ref, ga_ref, gb_ref, cw_ref, cb_ref, ng_ref, nb_ref, y_ref, buf, phases):
        i = pl.program_id(0)
        first = (i % nps) == 0
        for o_ref, g_ref, c0 in ((oa_ref, ga_ref, 0), (ob_ref, gb_ref, WA)):
            o = o_ref[...]
            r = lax.rsqrt(jnp.mean(o * o, axis=-1, keepdims=True) + EPS)
            y_ref[:, c0:c0 + WA] = (o * r * g_ref[...]).astype(BF16)
        ph = jnp.where(first, 0.0, ph_ref[...])
        buf[0:HALO, :] = ph[:, :CC] * _sigmoid(ph[:, CC:])
        pt = pt_ref[...]
        buf[HALO:HALO + tm, :] = pt[:, :CC] * _sigmoid(pt[:, CC:])
        _phase_copies(buf, phases, tm + HALO)
        cv = _conv_taps(buf, phases, cw_ref, HALO - CONV_K + 1, tm) + cb_ref[...]
        xc = cv - jnp.mean(cv, axis=-1, keepdims=True)
        lo = xc * lax.rsqrt(jnp.mean(xc * xc, axis=-1, keepdims=True) + EPS) * ng_ref[...] + nb_ref[...]
        y_ref[:, 2 * WA:] = (lo * _sigmoid(lo)).astype(BF16)

    row = lambda w: pl.BlockSpec((1, w), lambda i: (0, 0))
    return pl.pallas_call(
        body, name="mix_fwd", grid=(T // tm,), out_shape=_sds((T, D), BF16),
        in_specs=[pl.BlockSpec((tm, WA), lambda i: (i, 0)), pl.BlockSpec((tm, WA), lambda i: (i, 0)),
                  pl.BlockSpec((tm, 2 * CC), lambda i: (i, 0)),
                  pl.BlockSpec((HALO, 2 * CC), lambda i: (jnp.maximum(i * (tm // HALO) - 1, 0), 0)),
                  row(WA), row(WA), pl.BlockSpec((32, CC), lambda i: (0, 0)), row(CC), row(CC), row(CC)],
        out_specs=pl.BlockSpec((tm, D), lambda i: (i, 0)),
        scratch_shapes=[pltpu.VMEM((tm + HALO, CC), F32), pltpu.VMEM((8, tm + HALO, CC), F32)],
        compiler_params=_cp("parallel"))(o_a, o_b, p_rest, p_rest, g_a, g_b, cw, cb, ng, nbias)


def _mm_res(a, w, resid, g, name):
    nk, T, kb = a.shape
    tm = min(512, T)

    def body(a_ref, w_ref, r_ref, g_ref, x_ref, h_ref):
        xv = r_ref[...]
        for k in range(nk):
            xv = xv + _dot(a_ref[k], w_ref[k])
        x_ref[...] = xv
        r = lax.rsqrt(jnp.mean(xv * xv, axis=-1, keepdims=True) + EPS)
        h_ref[...] = (xv * r * g_ref[...]).astype(BF16)

    return pl.pallas_call(
        body, name=name, grid=(T // tm,), out_shape=[_sds((T, D), F32), _sds((T, D), BF16)],
        in_specs=[pl.BlockSpec((nk, tm, kb), lambda i: (0, i, 0)), pl.BlockSpec((nk, kb, D), lambda i: (0, 0, 0)),
                  pl.BlockSpec((tm, D), lambda i: (i, 0)), pl.BlockSpec((1, D), lambda i: (0, 0))],
        out_specs=[pl.BlockSpec((tm, D), lambda i: (i, 0)), pl.BlockSpec((tm, D), lambda i: (i, 0))],
        compiler_params=_cp("parallel"))(a, w, resid, g)


def _up_fwd(h2, w_up, fcw, fcb, S):
    T = h2.shape[0]
    tm = min(512, S)
    nps = S // tm

    def body(h_ref, hh_ref, w_ref, cw_ref, cb_ref, u_ref, hid_ref, buf, hbuf):
        i = pl.program_id(1)
        first = (i % nps) == 0
        hbuf[0:tm, :] = h_ref[...]
        hbuf[tm:tm + 8, :] = jnp.where(first, jnp.zeros_like(hh_ref[...]), hh_ref[...])
        c = []
        for half in range(2):
            ua = _dot(hbuf[...], w_ref[half, 0])
            u = ua[0:tm]
            u_ref[half, 0] = u
            buf[half, 0:8, :] = ua[tm:tm + 8]
            buf[half, 8:8 + tm, :] = u
            cw = cw_ref[half, 0]
            c.append(cb_ref[half, 0] + cw[0:1] * buf[half, pl.ds(6, tm), :] + cw[1:2] * buf[half, pl.ds(7, tm), :] + cw[2:3] * u)
        hid_ref[0] = (c[0] * _sigmoid(c[0]) * c[1]).astype(BF16)

    return pl.pallas_call(
        body, name="up_fwd", grid=(NJ, T // tm), out_shape=[_sds((2, NJ, T, FB), F32), _sds((NJ, T, FB), BF16)],
        in_specs=[pl.BlockSpec((tm, D), lambda j, i: (i, 0)),
                  pl.BlockSpec((8, D), lambda j, i: (jnp.maximum(i * (tm // 8) - 1, 0), 0)),
                  pl.BlockSpec((2, 1, D, FB), lambda j, i: (0, j, 0, 0)),
                  pl.BlockSpec((2, 1, 3, FB), lambda j, i: (0, j, 0, 0)),
                  pl.BlockSpec((2, 1, 1, FB), lambda j, i: (0, j, 0, 0))],
        out_specs=[pl.BlockSpec((2, 1, tm, FB), lambda j, i: (0, j, i, 0)), pl.BlockSpec((1, tm, FB), lambda j, i: (j, i, 0))],
        scratch_shapes=[pltpu.VMEM((2, tm + 8, FB), F32), pltpu.VMEM((tm + 8, D), BF16)],
        compiler_params=_cp("parallel", "parallel"))(h2, h2, w_up, fcw, fcb)


def _final_loss(x, tgt, g):
    T = x.shape[0]
    tm = min(512, T)

    def body(x_ref, t_ref, g_ref, dx_ref, loss_ref, dg_ref):
        @pl.when(pl.program_id(0) == 0)
        def _():
            loss_ref[...] = jnp.zeros_like(loss_ref)
            dg_ref[...] = jnp.zeros_like(dg_ref)

        xv, gv = x_ref[...], g_ref[...]
        r = lax.rsqrt(jnp.mean(xv * xv, axis=-1, keepdims=True) + EPS)
        e = xv * r * gv - t_ref[...]
        loss_ref[...] += 0.5 * jnp.sum(jnp.mean(e * e, axis=-1, keepdims=True), axis=0, keepdims=True)
        dx, dg = _rms_bwd(e * (1.0 / D), xv, gv)
        dx_ref[...] = dx
        dg_ref[...] += dg

    return pl.pallas_call(
        body, name="final_loss", grid=(T // tm,), out_shape=[_sds((T, D), F32), _sds((1, 128), F32), _sds((1, D), F32)],
        in_specs=[pl.BlockSpec((tm, D), lambda i: (i, 0)), pl.BlockSpec((tm, D), lambda i: (i, 0)), pl.BlockSpec((1, D), lambda i: (0, 0))],
        out_specs=[pl.BlockSpec((tm, D), lambda i: (i, 0)), pl.BlockSpec((1, 128), lambda i: (0, 0)), pl.BlockSpec((1, D), lambda i: (0, 0))],
        compiler_params=_cp("arbitrary"))(x, tgt, g)


def _down_bwd(dx2, w_down, u, fcw, fcb, S):
    T = dx2.shape[0]
    tm = min(512, S)
    nps = S // tm
    nblk8 = T // 8

    def body(dx_ref, dxn_ref, wd_ref, ut_ref, up_ref, un_ref, cw_ref, cb_ref, du_ref, dcw_ref, dxb, ubuf, dcbuf, ush):
        i = pl.program_id(1)
        first, last = (i % nps) == 0, (i % nps) == nps - 1

        @pl.when(i == 0)
        def _():
            dcw_ref[...] = jnp.zeros_like(dcw_ref)

        dxb[0:tm, :] = dx_ref[...].astype(BF16)
        dxb[tm:tm + 8, :] = jnp.where(last, 0.0, dxn_ref[...]).astype(BF16)
        dh = _dot_nt(dxb[...], wd_ref[0])
        ce = []
        for half in range(2):
            ubuf[half, 0:8, :] = jnp.where(first, 0.0, up_ref[half, 0])
            ubuf[half, 8:8 + tm, :] = ut_ref[half, 0]
            ubuf[half, 8 + tm:16 + tm, :] = un_ref[half, 0]
            cw = cw_ref[half, 0]
            for k in range(2):
                ush[half, k] = ubuf[half, pl.ds(6 + k, tm + 8), :]
            ce.append(cb_ref[half, 0] + cw[0:1] * ush[half, 0] + cw[1:2] * ush[half, 1] + cw[2:3] * ubuf[half, pl.ds(8, tm + 8), :])
        sg = _sigmoid(ce[0])
        dc = [dh * ce[1] * (sg * (1.0 + ce[0] * (1.0 - sg))), dh * (ce[0] * sg)]
        for half in range(2):
            cw = cw_ref[half, 0]
            dcbuf[...] = dc[half]
            dct = dc[half][0:tm]
            du_ref[half, 0] = (cw[2:3] * dct + cw[1:2] * dcbuf[pl.ds(1, tm), :] + cw[0:1] * dcbuf[pl.ds(2, tm), :]).astype(BF16)
            for k in range(3):
                u_back = ush[half, k, 0:tm, :] if k < 2 else ubuf[half, pl.ds(8, tm), :]
                dcw_ref[half, 0, k:k + 1, :] += jnp.sum(dct * u_back, axis=0, keepdims=True)
            dcw_ref[half, 0, 3:4, :] += jnp.sum(dct, axis=0, keepdims=True)

    ublk = lambda rows, imap: pl.BlockSpec((2, 1, rows, FB), imap)
    return pl.pallas_call(
        body, name="down_bwd", grid=(NJ, T // tm), out_shape=[_sds((2, NJ, T, FB), BF16), _sds((2, NJ, 8, FB), F32)],
        in_specs=[pl.BlockSpec((tm, D), lambda j, i: (i, 0)),
                  pl.BlockSpec((8, D), lambda j, i: (jnp.minimum((i + 1) * (tm // 8), nblk8 - 1), 0)),
                  pl.BlockSpec((1, FB, D), lambda j, i: (j, 0, 0)),
                  ublk(tm, lambda j, i: (0, j, i, 0)),
                  ublk(8, lambda j, i: (0, j, jnp.maximum(i * (tm // 8) - 1, 0), 0)),
                  ublk(8, lambda j, i: (0, j, jnp.minimum((i + 1) * (tm // 8), nblk8 - 1), 0)),
                  pl.BlockSpec((2, 1, 3, FB), lambda j, i: (0, j, 0, 0)),
                  pl.BlockSpec((2, 1, 1, FB), lambda j, i: (0, j, 0, 0))],
        out_specs=[ublk(tm, lambda j, i: (0, j, i, 0)), pl.BlockSpec((2, 1, 8, FB), lambda j, i: (0, j, 0, 0))],
        scratch_shapes=[pltpu.VMEM((tm + 8, D), BF16), pltpu.VMEM((2, tm + 16, FB), F32), pltpu.VMEM((tm + 8, FB), F32),
                        pltpu.VMEM((2, 2, tm + 8, FB), F32)],
        compiler_params=_cp("parallel", "arbitrary"))(dx2, dx2, w_down, u, u, u, fcw, fcb)


def _wgrad(a, b, tn, name, ga=1, gb=1):
    na, T, ka = a.shape
    nb, _, kb = b.shape
    tk = min(1024, T)
    nt = T // tk

    def body(a_ref, b_ref, o_ref, acc):
        t = pl.program_id(3)

        @pl.when(t == 0)
        def _():
            acc[...] = jnp.zeros_like(acc)

        bs = [b_ref[ib].astype(BF16) for ib in range(gb)]
        for ia in range(ga):
            av = a_ref[ia]
            for ib in range(gb):
                acc[ia, ib] += _dot_tn(av, bs[ib])

        @pl.when(t == nt - 1)
        def _():
            o_ref[...] = acc[...].astype(GRAD_DTYPE)

    return pl.pallas_call(
        body, name=name, grid=(na // ga, nb // gb, kb // tn, nt), out_shape=_sds((na, nb, ka, kb), GRAD_DTYPE),
        in_specs=[pl.BlockSpec((ga, tk, ka), lambda ia, ib, n, t: (ia, t, 0)), pl.BlockSpec((gb, tk, tn), lambda ia, ib, n, t: (ib, t, n))],
        out_specs=pl.BlockSpec((ga, gb, ka, tn), lambda ia, ib, n, t: (ia, ib, 0, n)),
        scratch_shapes=[pltpu.VMEM((ga, gb, ka, tn), F32)],
        compiler_params=_cp("parallel", "parallel", "parallel", "arbitrary"))(a, b)


def _dx_rms(dy, w, x, g, dres, blocked, name):
    T = x.shape[0]
    tm = min(256 if blocked else 512, T)

    def body(dy_ref, w_ref, x_ref, g_ref, r_ref, dx_ref, dg_ref):
        @pl.when(pl.program_id(0) == 0)
        def _():
            dg_ref[...] = jnp.zeros_like(dg_ref)

        if blocked:
            dh = _dot_nt(dy_ref[0, 0], w_ref[0, 0])
            for half, k in [(h, k) for k in range(NJ) for h in range(2)][1:]:
                dh = dh + _dot_nt(dy_ref[half, k], w_ref[half, k])
        else:
            dh = _dot_nt(dy_ref[...], w_ref[...])
        dx, dg = _rms_bwd(dh, x_ref[...], g_ref[...])
        dx_ref[...] = r_ref[...] + dx
        dg_ref[...] += dg

    if blocked:
        dy_spec = pl.BlockSpec((2, NJ, tm, FB), lambda i: (0, 0, i, 0))
        w_spec = pl.BlockSpec((2, NJ, D, FB), lambda i: (0, 0, 0, 0))
    else:
        dy_spec = pl.BlockSpec((tm, dy.shape[1]), lambda i: (i, 0))
        w_spec = pl.BlockSpec(w.shape, lambda i: (0, 0))
    tile = pl.BlockSpec((tm, D), lambda i: (i, 0))
    return pl.pallas_call(
        body, name=name, grid=(T // tm,), out_shape=[_sds((T, D), F32), _sds((1, D), F32)],
        in_specs=[dy_spec, w_spec, tile, pl.BlockSpec((1, D), lambda i: (0, 0)), tile],
        out_specs=[tile, pl.BlockSpec((1, D), lambda i: (0, 0))],
        compiler_params=pltpu.CompilerParams(dimension_semantics=("arbitrary",), vmem_limit_bytes=VMEM_LIMIT_BIG))(dy, w, x, g, dres)


def _mm_nt(a, w, name):
    T = a.shape[0]
    tm = min(512, T)

    def body(a_ref, w_ref, o_ref):
        o_ref[...] = _dot_nt(a_ref[...].astype(BF16), w_ref[...])

    return pl.pallas_call(
        body, name=name, grid=(T // tm,), out_shape=_sds((T, D), F32),
        in_specs=[pl.BlockSpec((tm, D), lambda i: (i, 0)), pl.BlockSpec((D, D), lambda i: (0, 0))],
        out_specs=pl.BlockSpec((tm, D), lambda i: (i, 0)), compiler_params=_cp("parallel"))(a, w)


def _mix_bwd(dy, o_a, o_b, p_rest, g_a, g_b, cw, cb, ng, nbias, S):
    T = dy.shape[0]
    tm = min(256, S)
    nps = S // tm
    HALO = 32
    nblkh = T // HALO
    RE = tm + HALO

    def body(dy_ref, dyn_ref, oa_ref, ob_ref, pt_ref, pp_ref, pn_ref, ga_ref, gb_ref, cw_ref, cb_ref, ng_ref, nb_ref,
             doa_ref, dda_ref, dob_ref, ddb_ref, dg_ref, dgn_ref, dcw_ref, dcn_ref, gbuf, dvbuf, gph, dph):
        i = pl.program_id(0)
        first, last = (i % nps) == 0, (i % nps) == nps - 1

        @pl.when(i == 0)
        def _():
            dgn_ref[...] = jnp.zeros_like(dgn_ref)
            dcw_ref[...] = jnp.zeros_like(dcw_ref)
            dcn_ref[...] = jnp.zeros_like(dcn_ref)

        head_of = (lax.broadcasted_iota(jnp.int32, (8, WA), 1) // HD == lax.broadcasted_iota(jnp.int32, (8, WA), 0)).astype(F32)
        for n, (o_ref, g_ref, do_ref, dd_ref) in enumerate(((oa_ref, ga_ref, doa_ref, dda_ref), (ob_ref, gb_ref, dob_ref, ddb_ref))):
            o = o_ref[...]
            do, dg = _rms_bwd(dy_ref[:, n * WA:(n + 1) * WA], o, g_ref[...])
            dob = do.astype(BF16)
            do_ref[...] = dob
            ddt = _dot_nt(head_of, dob.astype(F32) * o, HIGHEST)
            for hd in range(8):
                dd_ref[0, hd] = ddt[hd:hd + 1, :]
            dgn_ref[n:n + 1, :] += dg

        pp = jnp.where(first, 0.0, pp_ref[...])
        gbuf[0:HALO, :] = pp[:, :CC] * _sigmoid(pp[:, CC:])
        pt = pt_ref[...]
        gv, sgg = pt[:, :CC], _sigmoid(pt[:, CC:])
        gbuf[HALO:HALO + tm, :] = gv * sgg
        pn = pn_ref[...]
        gbuf[HALO + tm:2 * HALO + tm, :] = pn[:, :CC] * _sigmoid(pn[:, CC:])
        _phase_copies(gbuf, gph, tm + 2 * HALO)
        cv = _conv_taps(gbuf, gph, cw_ref, HALO - CONV_K + 1, RE) + cb_ref[...]
        xc = cv - jnp.mean(cv, axis=-1, keepdims=True)
        r = lax.rsqrt(jnp.mean(xc * xc, axis=-1, keepdims=True) + EPS)
        xh = xc * r
        lo = xh * ng_ref[...] + nb_ref[...]
        sg = _sigmoid(lo)
        dyc = jnp.concatenate([dy_ref[:, 2 * WA:], jnp.where(last, 0.0, dyn_ref[...])], axis=0)
        dlo = dyc * (sg * (1.0 + lo * (1.0 - sg)))
        dcn_ref[0:1, :] += jnp.sum(dlo[0:tm] * xh[0:tm], axis=0, keepdims=True)
        dcn_ref[1:2, :] += jnp.sum(dlo[0:tm], axis=0, keepdims=True)
        dxh = dlo * ng_ref[...]
        dcv = r * (dxh - jnp.mean(dxh, axis=-1, keepdims=True) - xh * jnp.mean(dxh * xh, axis=-1, keepdims=True))
        dvbuf[...] = dcv
        _phase_copies(dvbuf, dph, RE)
        dct = dcv[0:tm]
        dcw_ref[CONV_K:CONV_K + 1, :] += jnp.sum(dct, axis=0, keepdims=True)
        dglu = jnp.zeros((tm, CC), F32)
        for k in range(CONV_K):
            dcw_ref[k:k + 1, :] += jnp.sum(dct * _tap(gbuf, gph, HALO - CONV_K + 1 + k, tm), axis=0, keepdims=True)
            dglu = dglu + cw_ref[k:k + 1, :] * _tap(dvbuf, dph, CONV_K - 1 - k, tm)
        dg_ref[:, :CC] = (dglu * sgg).astype(BF16)
        dg_ref[:, CC:] = (dglu * gv * sgg * (1.0 - sgg)).astype(BF16)

    row = lambda w: pl.BlockSpec((1, w), lambda i: (0, 0))
    tile = lambda w: pl.BlockSpec((tm, w), lambda i: (i, 0))
    nxt = lambda i: jnp.minimum((i + 1) * (tm // HALO), nblkh - 1)
    const = lambda r, w: pl.BlockSpec((r, w), lambda i: (0, 0))
    ddrow = pl.BlockSpec((1, 8, 1, tm), lambda i: (i, 0, 0, 0))
    return pl.pallas_call(
        body, name="mix_bwd", grid=(T // tm,),
        out_shape=[_sds((T, WA), BF16), _sds((T // tm, 8, 1, tm), F32), _sds((T, WA), BF16), _sds((T // tm, 8, 1, tm), F32),
                   _sds((T, 2 * CC), BF16), _sds((8, WA), F32), _sds((32, CC), F32), _sds((8, CC), F32)],
        in_specs=[tile(D), pl.BlockSpec((HALO, CC), lambda i: (nxt(i), 3)), tile(WA), tile(WA), tile(2 * CC),
                  pl.BlockSpec((HALO, 2 * CC), lambda i: (jnp.maximum(i * (tm // HALO) - 1, 0), 0)),
                  pl.BlockSpec((HALO, 2 * CC), lambda i: (nxt(i), 0)),
                  row(WA), row(WA), const(32, CC), row(CC), row(CC), row(CC)],
        out_specs=[tile(WA), ddrow, tile(WA), ddrow, tile(2 * CC), const(8, WA), const(32, CC), const(8, CC)],
        scratch_shapes=[pltpu.VMEM((tm + 2 * HALO, CC), F32), pltpu.VMEM((RE, CC), F32),
                        pltpu.VMEM((8, tm + 2 * HALO, CC), F32), pltpu.VMEM((8, RE, CC), F32)],
        compiler_params=_cp("arbitrary"))(dy, dy, o_a, o_b, p_rest, p_rest, p_rest, g_a, g_b, cw, cb, ng, nbias)


def _attn_bwd(p_qkv, col0, tab, do, lse, dd, kat, S, aq=None, ak=None, name="attn_bwd"):
    T = p_qkv.shape[0]
    n_tab, TK, TB = tab.shape
    nb, nkb, r = S // TB, S // TK, TK // TB
    B = T // S
    fox = aq is not None

    def body(*refs):
        if fox:
            (q_ref, k_ref, v_ref, tab_ref, do_ref, l_ref, dd_ref, kat_ref, aq_ref, ak_ref,
             dq_ref, dk_ref, dv_ref, dqa_ref, dka_ref, dqt_scr, dk_scr, dv_scr, sd_scr) = refs
        else:
            (q_ref, k_ref, v_ref, tab_ref, do_ref, l_ref, dd_ref, kat_ref,
             dq_ref, dk_ref, dv_ref, dqt_scr, dk_scr, dv_scr, sd_scr) = refs
        j = pl.program_id(2)
        hms = _head_masks()
        first_q_block = r * j

        @pl.when(j == 0)
        def _():
            dqt_scr[...] = jnp.zeros_like(dqt_scr)

        dk_scr[...] = jnp.zeros_like(dk_scr)
        dv_scr[...] = jnp.zeros_like(dv_scr)
        k2, v2 = k_ref[...], v_ref[...]
        ka = [jnp.where(hm, k2, ak_ref[0] if fox else jnp.zeros_like(k2)) for hm in hms]
        kat = [jnp.concatenate([kat_ref[0, 0, hh, t] for t in range(r)], axis=1) if r > 1 else kat_ref[0, 0, hh, 0]
               for hh in range(2)]

        def operands(i):
            ic = jnp.minimum(i, nb - 1)
            rows = pl.ds(pl.multiple_of(ic * TB, TB), TB)
            q2 = q_ref[rows, :] * 0.125
            do2 = do_ref[rows, :]
            qas = [jnp.where(hms[hh], q2, aq_ref[0, rows, :] if fox else jnp.zeros_like(q2)) for hh in range(2)]
            dohs = [jnp.where(hms[hh], do2, jnp.zeros_like(do2)) for hh in range(2)]
            return ic, qas, dohs

        def scores(i):
            ic, qas, dohs = operands(i)
            bias = tab_ref[jnp.where(i < nb, i - first_q_block, n_tab - 1)]
            return [(_dot_nt(ka[hh], qas[hh]) + bias, _dot_nt(v2, dohs[hh])) for hh in range(2)]

        def absorb(i, sd):
            ic, qas, dohs = operands(i)
            for hh in range(2):
                st, dpt = sd[hh]
                pt = jnp.exp(st - l_ref[ic, hh])
                dsb = (pt * (dpt - dd_ref[ic, hh])).astype(BF16)
                dv_scr[...] += _dot(pt.astype(BF16), dohs[hh])
                dk_scr[hh] += _dot(dsb, qas[hh])
                dqt_scr[hh, ic] += _dot(kat[hh], dsb)

        def q_pair(ii, carry):
            i0 = first_q_block + 2 * ii
            sd0 = [(sd_scr[hh, 0], sd_scr[hh, 1]) for hh in range(2)]
            sd1 = scores(i0 + 1)
            absorb(i0, sd0)
            nxt = scores(i0 + 2)
            for hh in range(2):
                sd_scr[hh, 0], sd_scr[hh, 1] = nxt[hh]
            absorb(i0 + 1, sd1)
            return carry

        first = scores(first_q_block)
        for hh in range(2):
            sd_scr[hh, 0], sd_scr[hh, 1] = first[hh]
        lax.fori_loop(0, (nb - first_q_block + 1) // 2, q_pair, 0)
        dk_ref[...] = jnp.where(hms[0], dk_scr[0], dk_scr[1]).astype(BF16)
        dv_ref[...] = dv_scr[...].astype(BF16)
        if fox:
            dka_ref[...] = jnp.where(hms[0], dk_scr[1], dk_scr[0])

        @pl.when(j == nkb - 1)
        def _():
            for ii in range(nb):
                t0, t1 = dqt_scr[0, ii].T, dqt_scr[1, ii].T
                dq_ref[ii * TB:(ii + 1) * TB, :] = jnp.where(hms[0], t0, t1) * 0.125
                if fox:
                    dqa_ref[ii * TB:(ii + 1) * TB, :] = jnp.where(hms[0], t1, t0)

    seq = lambda c: pl.BlockSpec((S, 128), lambda b, p, j: (b, c + p))
    kvb = lambda c: pl.BlockSpec((TK, 128), lambda b, p, j: (b * nkb + j, c + p))
    stat = pl.BlockSpec((nb, 2, 1, TB), lambda b, p, j: (b, p, 0, 0))
    in_specs = [seq(col0), kvb(col0 + NPAIR), kvb(col0 + 2 * NPAIR), pl.BlockSpec(tab.shape, lambda b, p, j: (0, 0, 0)),
                seq(0), stat, stat, pl.BlockSpec((1, 1, 2, r, 128, TB), lambda b, p, j: (b, p, 0, j, 0, 0))]
    args = [p_qkv, p_qkv, p_qkv, tab, do, lse, dd, kat]
    out_shape = [_sds((T, WA), F32), _sds((T, WA), BF16), _sds((T, WA), BF16)]
    out_specs = [seq(0), kvb(0), kvb(0)]
    if fox:
        in_specs += [pl.BlockSpec((1, S, 128), lambda b, p, j: (p, b, 0)), pl.BlockSpec((1, TK, 128), lambda b, p, j: (p, b * nkb + j, 0))]
        args += [aq, ak]
        out_shape += [_sds((T, WA), F32), _sds((T, WA), F32)]
        out_specs += [seq(0), kvb(0)]
    return pl.pallas_call(
        body, name=name, grid=(B, NPAIR, nkb), out_shape=out_shape, in_specs=in_specs, out_specs=out_specs,
        scratch_shapes=[pltpu.VMEM((2, nb, 128, TB), F32), pltpu.VMEM((2, TK, 128), F32), pltpu.VMEM((TK, 128), F32),
                        pltpu.VMEM((2, 2, TK, TB), F32)],
        compiler_params=_cp("parallel", "parallel", "arbitrary"))(*args)


def _gate_bwd(dqa, dka, p_rest, bf, S, TB):
    T = p_rest.shape[0]
    B, nb = T // S, S // TB

    def body(dqa_ref, dka_ref, z_ref, b_ref, dz_ref, db_ref):
        @pl.when(pl.program_id(0) == 0)
        def _():
            db_ref[...] = jnp.zeros_like(db_ref)

        later = (lax.broadcasted_iota(jnp.int32, (TB, TB), 1) >= lax.broadcasted_iota(jnp.int32, (TB, TB), 0)).astype(F32)
        lane = lax.broadcasted_iota(jnp.int32, (1, 128), 1)
        src, head = lax.broadcasted_iota(jnp.int32, (WA, 128), 0), lax.broadcasted_iota(jnp.int32, (WA, 128), 1)
        spare0 = 128 * (head // 2) + HD * (1 - head % 2)
        pick_q = (src == spare0 + AUG_K1).astype(F32)
        pick_k = (src == spare0 + AUG_Q1).astype(F32)
        carry = jnp.zeros((1, 128), F32)
        dbs = jnp.zeros((1, 128), F32)
        for j in reversed(range(nb)):
            rows = slice(j * TB, (j + 1) * TB)
            dc = _dot(dqa_ref[rows, :], pick_q, HIGHEST) - _dot(dka_ref[rows, :], pick_k, HIGHEST)
            part = _dot(later, dc, HIGHEST)
            tot = part + carry
            carry = carry + part[0:1, :]
            z = z_ref[j * TB:(j + 1) * TB, :] + b_ref[...]
            dz = jnp.where(lane < 6, tot * _sigmoid(-z), 0.0)
            dz_ref[j * TB:(j + 1) * TB, :] = dz.astype(BF16)
            dbs = dbs + jnp.sum(dz, axis=0, keepdims=True)
        db_ref[...] += dbs

    return pl.pallas_call(
        body, name="gate_bwd", grid=(B,), out_shape=[_sds((T, 128), BF16), _sds((1, 128), F32)],
        in_specs=[pl.BlockSpec((S, WA), lambda b: (b, 0)), pl.BlockSpec((S, WA), lambda b: (b, 0)),
                  pl.BlockSpec((S, 128), lambda b: (b, 4)), pl.BlockSpec((1, 128), lambda b: (0, 0))],
        out_specs=[pl.BlockSpec((S, 128), lambda b: (b, 0)), pl.BlockSpec((1, 128), lambda b: (0, 0))],
        compiler_params=_cp("arbitrary"))(dqa, dka, p_rest, bf)


def _adamw(parts, w, m, v, name, row0=0, prev=None):
    npart, Rp, C = parts.shape
    R = w.shape[0]
    tr = Rp
    for cand in (256, 128, 64, 32, 16, 8):
        if Rp % cand == 0 and row0 % cand == 0 and cand * C * 4 <= (1 << 20):
            tr = cand
            break
    if Rp == R and R * C * 4 <= (1 << 20):
        tr = R
    assert Rp % tr == 0 and row0 % tr == 0, (name, Rp, row0, tr)
    b0 = row0 // tr

    def body(p_ref, w_ref, m_ref, v_ref, *rest):
        g_ref, d_ref, mo_ref, vo_ref = rest[-4:]
        g = p_ref[0].astype(F32)
        for k in range(1, npart):
            g = g + p_ref[k].astype(F32)
        mn = ADAM_B1 * m_ref[...] + (1.0 - ADAM_B1) * g
        vn = ADAM_B2 * v_ref[...] + (1.0 - ADAM_B2) * (g * g)
        m_hat = mn / (1.0 - ADAM_B1 ** ADAM_STEP)
        v_hat = vn / (1.0 - ADAM_B2 ** ADAM_STEP)
        g_ref[...] = g
        d_ref[...] = -ADAM_LR * (m_hat / (jnp.sqrt(v_hat) + ADAM_EPS) + ADAM_WD * w_ref[...])
        mo_ref[...] = mn
        vo_ref[...] = vn

    blk = pl.BlockSpec((tr, C), lambda i: (b0 + i, 0))
    prev = list(prev) if prev is not None else []
    return pl.pallas_call(
        body, name=name, grid=(Rp // tr,), out_shape=[_sds((R, C), F32)] * 4,
        in_specs=[pl.BlockSpec((npart, tr, C), lambda i: (0, i, 0)), blk, blk, blk] + [_ANY] * len(prev), out_specs=[blk] * 4,
        input_output_aliases={4 + i: i for i in range(len(prev))},
        compiler_params=_cp("parallel"))(parts, w, m, v, *prev)


def _pad_w_in(w):
    z = jnp.zeros(w.shape[:-1] + (NPAD - N_IN,), w.dtype)
    return jnp.concatenate([w[..., :O_FA], w[..., O_QB:], w[..., O_FA:O_QB], z], axis=-1)


def _unpad_w_in(w):
    n_mid = N_IN - O_QB
    return jnp.concatenate([w[..., :O_FA], w[..., O_FA + n_mid:O_FA + n_mid + 6], w[..., O_FA:O_FA + n_mid]], axis=-1)


def _local_step(x, tgt, W, S, late_weights=None, on_layer_grads=None, on_ffn_grads=None):
    T = x.shape[0]
    TB = min(256, S)
    TW = min(2 * TB, S)
    tab_fox, tab_dil = _bias_table(S, TW, TB, True), _bias_table(S, TW, TB, False)
    tabq_fox, tabq_dil = _bias_table(S, TB, TW, True), _bias_table(S, TB, TW, False)
    saved = []
    h = _rms_fwd(x, W["ln1_g"][0])
    for l in range(DEPTH):
        p_qkv, p_rest = _mm_in(h, W["w_in"][l])
        aq, ak = _gate_fwd(p_rest, W["b_forget"][l], S, TB)
        o_a, lse_a, kat_a = _attn_fwd(p_qkv, 0, tabq_fox, S, aq, ak, name="fox_fwd")
        o_b, lse_b, kat_b = _attn_fwd(p_qkv, 3 * NPAIR, tabq_dil, S, name="dil_fwd")
        y = _mix_fwd(o_a, o_b, p_rest, W["g_out_fox"][l], W["g_out_dil"][l], W["conv_w"][l], W["conv_b"][l],
                     W["cnorm_g"][l], W["cnorm_b"][l], S)
        if l == 0 and late_weights is not None:
            W = late_weights(W, y)
        x1, h2 = _mm_res(y[None], W["w_o"][l][None], x, W["ln2_g"][l], name="mm_o")
        u, hid = _up_fwd(h2, W["w_up"][l], W["ffn_conv_w"][l], W["ffn_conv_b"][l], S)
        g_next = W["ln1_g"][l + 1] if l + 1 < DEPTH else W["g_final"]
        x2, h_next = _mm_res(hid, W["w_down"][l], x1, g_next, name="mm_down")
        saved.append(dict(x=x, h=h, p_qkv=p_qkv, p_rest=p_rest, aq=aq, ak=ak, o_a=o_a, lse_a=lse_a, o_b=o_b,
                          lse_b=lse_b, kat_a=kat_a, kat_b=kat_b, y=y, x1=x1, h2=h2, u=u, hid=hid))
        x, h = x2, h_next
    dx, loss, dg_final = _final_loss(x, tgt, W["g_final"])
    G = {k: [None] * DEPTH for k in ("ln1_g", "w_in", "b_forget", "g_out_fox", "g_out_dil", "conv_w", "conv_b", "cnorm_g",
                                     "cnorm_b", "w_o", "ln2_g", "w_up", "ffn_conv_w", "ffn_conv_b", "w_down")}
    G["g_final"] = dg_final
    for l in reversed(range(DEPTH)):
        sv = saved[l]
        du, dcw = _down_bwd(dx, W["w_down"][l], sv["u"], W["ffn_conv_w"][l], W["ffn_conv_b"][l], S)
        G["w_down"][l] = _wgrad(sv["hid"], dx[None], D, "wg_down", ga=NJ)[:, 0]
        G["ffn_conv_w"][l] = dcw[:, :, 0:3, :]
        G["ffn_conv_b"][l] = dcw[:, :, 3, :]
        G["w_up"][l] = _wgrad(sv["h2"][None], du.reshape(2 * NJ, T, FB), FB, "wg_up", gb=NJ)[0]
        token = on_ffn_grads(l, G) if on_ffn_grads is not None else None
        ln2 = W["ln2_g"][l] if token is None else W["ln2_g"][l] + token[0, 0]
        dx1, G["ln2_g"][l] = _dx_rms(du, W["w_up"][l], sv["x1"], ln2, dx, True, "dx_up")
        G["w_o"][l] = _wgrad(sv["y"][None], dx1[None], D, "wg_o")[0, 0]
        dy = _mm_nt(dx1, W["w_o"][l], "mm_dy")
        do_a, dd_a, do_b, dd_b, dgvgg, dgn, dcvw, dcn = _mix_bwd(
            dy, sv["o_a"], sv["o_b"], sv["p_rest"], W["g_out_fox"][l], W["g_out_dil"][l], W["conv_w"][l], W["conv_b"][l],
            W["cnorm_g"][l], W["cnorm_b"][l], S)
        G["g_out_fox"][l], G["g_out_dil"][l] = dgn[0], dgn[1]
        G["conv_w"][l], G["conv_b"][l] = dcvw[:CONV_K], dcvw[CONV_K]
        G["cnorm_g"][l], G["cnorm_b"][l] = dcn[0], dcn[1]
        dq_a, dk_a, dv_a, dqa, dka = _attn_bwd(sv["p_qkv"], 0, tab_fox, do_a, sv["lse_a"], dd_a, sv["kat_a"], S, sv["aq"], sv["ak"],
                                               name="fox_bwd")
        dq_b, dk_b, dv_b = _attn_bwd(sv["p_qkv"], 3 * NPAIR, tab_dil, do_b, sv["lse_b"], dd_b, sv["kat_b"], S, name="dil_bwd")
        dfa, db = _gate_bwd(dqa, dka, sv["p_rest"], W["b_forget"][l], S, TB)
        G["b_forget"][l] = db[0, :6]
        dp = jnp.concatenate([dq_a.astype(BF16), dk_a, dv_a, dq_b.astype(BF16), dk_b, dv_b, dgvgg, dfa,
                              jnp.zeros((T, 128), BF16)], axis=1)
        G["w_in"][l] = _wgrad(sv["h"][None], dp[None], NPAD, "wg_in")[0, 0]
        token = on_layer_grads(l, G) if on_layer_grads is not None else None
        ln1 = W["ln1_g"][l] if token is None else W["ln1_g"][l] + token[0, 0]
        dx, G["ln1_g"][l] = _dx_rms(dp, W["w_in"][l], sv["x"], ln1, dx1, False, "dx_in")
    return loss, dx, G


_MATMUL_W = ("w_in", "w_o", "w_up", "w_down")
_SHARDED = _MATMUL_W + ("conv_w", "ffn_conv_w")
_SMALL = ("ln1_g", "b_forget", "g_out_fox", "g_out_dil", "conv_b", "cnorm_g", "cnorm_b", "ln2_g", "ffn_conv_b", "g_final")
_ORDER = ("ln1_g", "w_in", "b_forget", "g_out_fox", "g_out_dil", "conv_w", "conv_b", "cnorm_g", "cnorm_b", "w_o", "ln2_g",
          "w_up", "ffn_conv_w", "ffn_conv_b", "w_down", "g_final")


def _kernel_ready(k, zone):
    if k == "w_in":
        return _pad_w_in(zone.reshape(D, N_IN))
    if k == "w_o":
        return zone.reshape(D, D)
    if k == "w_up":
        return zone.reshape(2, NJ, D, FB)
    if k == "w_down":
        return zone.reshape(NJ, FB, D)
    if k == "conv_w":
        cw = jnp.transpose(zone.reshape(NDEV, CONV_K, CC // NDEV), (1, 0, 2)).reshape(CONV_K, CC)
        return jnp.concatenate([cw, jnp.zeros((1, CC), F32)], axis=0)
    return zone.reshape(2, NJ, 3, FB)


def _full_weights(P, gathered):
    W = {}
    row = lambda a: [a[l][None, :] for l in range(DEPTH)]
    for k in _SHARDED:
        W[k] = [None if gathered[k][l] is None else _kernel_ready(k, gathered[k][l]) for l in range(DEPTH)]
    W["ffn_conv_b"] = [P["ffn_conv_b"][l].reshape(2, NJ, 1, FB) for l in range(DEPTH)]
    W["b_forget"] = [jnp.concatenate([P["b_forget"][l], jnp.zeros((128 - 6,), F32)])[None, :] for l in range(DEPTH)]
    for k in ("ln1_g", "ln2_g", "g_out_fox", "g_out_dil", "conv_b", "cnorm_g", "cnorm_b"):
        W[k] = row(P[k])
    W["g_final"] = P["g_final"][None, :]
    return W


def kernel(x, ln1_g, w_in, b_forget, g_out_fox, g_out_dil, conv_w, conv_b, cnorm_g, cnorm_b, w_o, ln2_g, w_up, ffn_conv_w, ffn_conv_b, w_down, g_final, loss_target, m_ln1_g, m_w_in, m_b_forget, m_g_out_fox, m_g_out_dil, m_conv_w, m_conv_b, m_cnorm_g, m_cnorm_b, m_w_o, m_ln2_g, m_w_up, m_ffn_conv_w, m_ffn_conv_b, m_w_down, m_g_final, v_ln1_g, v_w_in, v_b_forget, v_g_out_fox, v_g_out_dil, v_conv_w, v_conv_b, v_cnorm_g, v_cnorm_b, v_w_o, v_ln2_g, v_w_up, v_ffn_conv_w, v_ffn_conv_b, v_w_down, v_g_final):
    P = dict(ln1_g=ln1_g, w_in=w_in, b_forget=b_forget, g_out_fox=g_out_fox, g_out_dil=g_out_dil, conv_w=conv_w, conv_b=conv_b,
             cnorm_g=cnorm_g, cnorm_b=cnorm_b, w_o=w_o, ln2_g=ln2_g, w_up=w_up, ffn_conv_w=ffn_conv_w, ffn_conv_b=ffn_conv_b,
             w_down=w_down, g_final=g_final)
    M = dict(ln1_g=m_ln1_g, w_in=m_w_in, b_forget=m_b_forget, g_out_fox=m_g_out_fox, g_out_dil=m_g_out_dil, conv_w=m_conv_w,
             conv_b=m_conv_b, cnorm_g=m_cnorm_g, cnorm_b=m_cnorm_b, w_o=m_w_o, ln2_g=m_ln2_g, w_up=m_w_up, ffn_conv_w=m_ffn_conv_w,
             ffn_conv_b=m_ffn_conv_b, w_down=m_w_down, g_final=m_g_final)
    V = dict(ln1_g=v_ln1_g, w_in=v_w_in, b_forget=v_b_forget, g_out_fox=v_g_out_fox, g_out_dil=v_g_out_dil, conv_w=v_conv_w,
             conv_b=v_conv_b, cnorm_g=v_cnorm_g, cnorm_b=v_cnorm_b, w_o=v_w_o, ln2_g=v_ln2_g, w_up=v_w_up, ffn_conv_w=v_ffn_conv_w,
             ffn_conv_b=v_ffn_conv_b, w_down=v_w_down, g_final=v_g_final)
    Bl, S, _ = x.shape
    T = Bl * S

    shard = lambda k, l: P[k][l].astype(BF16) if k in _MATMUL_W else P[k][l]
    early = [("w_in", 0)] + [(k, l) for k in ("conv_w", "ffn_conv_w") for l in range(DEPTH)]
    late = [(k, 0) for k in ("w_o", "w_up", "w_down")] + [(k, 1) for k in _MATMUL_W]
    gathered = {k: [None] * DEPTH for k in _SHARDED}
    plan_early = _Plan([[shard(k, l)] for k, l in early], True)
    for (k, l), zone in zip(early, _exchange(plan_early, "gather_early")):
        gathered[k][l] = zone
    plan_late = _Plan([[shard(k, l)] for k, l in late], True)
    late_handle, late_token = _exchange_start(plan_late, "gather_late_start", plan_late.zones_with_own())
    W = _full_weights(P, gathered)
    W["ln1_g"][0] = W["ln1_g"][0] + late_token[0, 0]

    def late_weights(W, after):
        zones = _exchange_wait(plan_late, "gather_late_wait", late_handle, after)
        W = dict(W)
        for (k, l), zone in zip(late, zones):
            W[k] = list(W[k])
            W[k][l] = _kernel_ready(k, zone)
        return W

    def owner_block(G, l, k):
        if k == "w_in":
            blk = _unpad_w_in(G[k][l]).reshape(NDEV, D // NDEV, N_IN)
        elif k == "w_o":
            blk = G[k][l].reshape(NDEV, D // NDEV, D)
        elif k == "w_up":
            blk = G[k][l]
        elif k == "w_down":
            blk = G[k][l].reshape(NDEV, DFF // NDEV, D)
        elif k == "conv_w":
            blk = jnp.transpose(G[k][l].reshape(CONV_K, NDEV, CC // NDEV), (1, 0, 2))
        else:
            blk = G[k][l].reshape(NDEV, 3, FB)
        return blk.astype(GRAD_DTYPE)

    flying = []

    def start_scatter(tag, G, l, keys):
        plan = _Plan([[owner_block(G, l, k)] for k in keys], False)
        handle, token = _exchange_start(plan, "scatter_%s_start" % tag, plan.zones_with_own())
        flying.append((tag, l, keys, plan, handle))
        return token

    def on_layer_grads(l, G):
        return start_scatter("l1", G, l, _MATMUL_W) if l == 1 else start_scatter("l0_rest", G, l, ("w_in", "w_o"))

    def on_ffn_grads(l, G):
        return start_scatter("l0_ffn", G, l, ("w_up", "w_down")) if l == 0 else None

    loss, dx, G = _local_step(x.reshape(T, D), loss_target.reshape(T, D), W, S, late_weights, on_layer_grads, on_ffn_grads)

    parts = {}
    for tag, l, keys, plan, handle in flying:
        for k, zone in zip(keys, _exchange_wait(plan, "scatter_%s_wait" % tag, handle, dx)):
            parts[(k, l)] = zone
    plan_last = _Plan([[owner_block(G, l, k) for l in range(DEPTH)] for k in ("conv_w", "ffn_conv_w")], False)
    last = _exchange(plan_last, "scatter_last")
    small_parts_sharded = {"conv_w": last[0], "ffn_conv_w": last[1]}

    small = [jnp.stack(G[k]).reshape(-1) for k in _SMALL[:-1]] + [G["g_final"].reshape(-1), loss[0, :1]]
    sizes = [int(s.shape[0]) for s in small]
    flat = jnp.concatenate(small)
    n_small = -(-flat.shape[0] // 1024) * 1024
    flat = jnp.concatenate([flat, jnp.zeros((n_small - flat.shape[0],), F32)]).reshape(n_small // 128, 128)
    small_parts = _exchange(_Plan([[flat]], True), "gather_small")[0]

    out_g, out_d, out_m, out_v = {}, {}, {}, {}
    sg = _sum_parts(small_parts.reshape(NDEV, n_small // 128, 128), "sum_small").reshape(-1)
    off = 0
    summed = {}
    for k, n in zip(_SMALL + ("loss",), sizes):
        summed[k] = sg[off:off + n]
        off += n
    for k in _ORDER:
        shp = P[k].shape
        C = shp[-1]
        if k in _MATMUL_W:
            R0 = shp[1]
            flat2 = lambda t: t.reshape(DEPTH * R0, C)
            res = None
            for l in reversed(range(DEPTH)):
                res = _adamw(parts[(k, l)].reshape(NDEV, R0, C), flat2(P[k]), flat2(M[k]), flat2(V[k]),
                             "adamw_%s_l%d" % (k, l), row0=l * R0, prev=res)
        else:
            pr = small_parts_sharded[k].reshape((NDEV, -1, C)) if k in _SHARDED else summed[k].reshape((1, -1, C))
            R = pr.shape[1]
            res = _adamw(pr, P[k].reshape(R, -1), M[k].reshape(R, -1), V[k].reshape(R, -1), "adamw_" + k)
        out_g[k], out_d[k], out_m[k], out_v[k] = (t.reshape(shp) for t in res)

    loss_out = summed["loss"].reshape(())
    grad_x = dx.reshape(Bl, S, D)
    return (loss_out, grad_x, *[out_g[k] for k in _ORDER], *[out_d[k] for k in _ORDER], *[out_m[k] for k in _ORDER],
            *[out_v[k] for k in _ORDER])
```

```python
import functools

import numpy as np
import jax
import jax.numpy as jnp
from jax import lax
from jax.experimental import pallas as pl
from jax.experimental.pallas import tpu as pltpu

F32 = jnp.float32
BF16 = jnp.bfloat16
GRAD_DTYPE = BF16
HIGHEST = lax.Precision.HIGHEST

D = 1024
DEPTH = 2
HD = 64
WA = 384
NPAIR = 3
CC = 256
CONV_K = 31
DFF = 2816
FB = 704
NJ = 4
NDEV = 8
EPS = 1e-6
NQKV = 2304
NREST = 768
NPAD = NQKV + NREST
O_FA, O_QB, N_IN = 1152, 1158, 2822
DILATION_PAIRS = ((128, 1), (512, 4), (2048, 16))
NEG = -1e30

ADAM_LR, ADAM_B1, ADAM_B2, ADAM_EPS, ADAM_WD, ADAM_STEP = 0.001, 0.9, 0.999, 1e-08, 0.01, 10

VMEM_LIMIT = 48 * 1024 * 1024
VMEM_LIMIT_BIG = 56 * 1024 * 1024


def _cp(*sem):
    return pltpu.CompilerParams(dimension_semantics=sem, vmem_limit_bytes=VMEM_LIMIT)


def _sds(shape, dtype):
    return jax.ShapeDtypeStruct(shape, dtype)


def _dot(a, b, prec=None):
    return jnp.dot(a, b, preferred_element_type=F32, precision=prec)


def _dot_nt(a, b, prec=None):
    return lax.dot_general(a, b, (((1,), (1,)), ((), ())), preferred_element_type=F32, precision=prec)


def _dot_tn(a, b):
    return lax.dot_general(a, b, (((0,), (0,)), ((), ())), preferred_element_type=F32)


def _sigmoid(z):
    return 0.5 * jnp.tanh(0.5 * z) + 0.5


def _rms_bwd(dh, x, g):
    r = lax.rsqrt(jnp.mean(x * x, axis=-1, keepdims=True) + EPS)
    xh = x * r
    dg = jnp.sum(dh * xh, axis=0, keepdims=True)
    dxh = dh * g
    dx = r * (dxh - xh * jnp.mean(dxh * xh, axis=-1, keepdims=True))
    return dx, dg


class _Plan:
    def __init__(self, groups, gather, slots=None, n_slots=None):
        self.groups, self.gather = groups, gather
        self.slots = slots or [list(range(len(g))) for g in groups]
        self.n_slots = n_slots or [len(g) for g in groups]
        self.flat = [a for g in groups for a in g]
        self.where = [(gi, s) for gi, g in enumerate(groups) for s in self.slots[gi]]
        self.zone_shapes = [_sds((NDEV, ns) + (g[0].shape if gather else g[0].shape[1:]), g[0].dtype)
                            for g, ns in zip(groups, self.n_slots)]

    def new_zones(self):
        return [lax.empty(z.shape, z.dtype) for z in self.zone_shapes]

    def me(self):
        x, y, c = lax.axis_index("x"), lax.axis_index("y"), lax.axis_index("c")
        return 4 * x + 2 * y + c

    def zones_with_own(self):
        me = self.me()
        zones = self.new_zones()
        for a, (gi, s) in enumerate(self.where):
            src = self.flat[a] if self.gather else lax.dynamic_index_in_dim(self.flat[a], me, 0, keepdims=False)
            zones[gi] = lax.dynamic_update_slice(zones[gi], src[None, None], (me, s) + (0,) * src.ndim)
        return zones

    def own_copies(self, ins, zones, sems):
        me = self.me()
        return [pltpu.make_async_copy(ins[a] if self.gather else ins[a].at[me], zones[gi].at[me, s], sems.at[a])
                for a, (gi, s) in enumerate(self.where)]

    def remote_copies(self, ins, zones, send_sems, recv_sems):
        x, y, c = lax.axis_index("x"), lax.axis_index("y"), lax.axis_index("c")
        me = 4 * x + 2 * y + c
        out = []
        for a, (gi, s) in enumerate(self.where):
            for k in range(1, NDEV):
                px, py, pc = x ^ ((k >> 2) & 1), y ^ ((k >> 1) & 1), c ^ (k & 1)
                out.append(pltpu.make_async_remote_copy(
                    src_ref=ins[a] if self.gather else ins[a].at[4 * px + 2 * py + pc], dst_ref=zones[gi].at[me, s],
                    send_sem=send_sems.at[a * (NDEV - 1) + k - 1], recv_sem=recv_sems.at[a * (NDEV - 1) + k - 1],
                    device_id=(px, py, pc), device_id_type=pl.DeviceIdType.MESH))
        return out


_ANY = pl.BlockSpec(memory_space=pl.ANY)
_HBM = pl.BlockSpec(memory_space=pltpu.HBM)
_SEM = pl.BlockSpec(memory_space=pltpu.SEMAPHORE)


def _exchange(plan, name, zones=None):
    n, nz = len(plan.flat), len(plan.groups)
    zones = zones or plan.new_zones()

    def body(*refs):
        ins, zin = refs[:n], refs[n:n + nz]
        send_sems, recv_sems, local_sems = refs[n + 2 * nz:]
        copies = plan.own_copies(ins, zin, local_sems) + plan.remote_copies(ins, zin, send_sems, recv_sems)
        for cp in copies:
            cp.start()
        for cp in copies:
            cp.wait()

    return pl.pallas_call(
        body, name=name, out_shape=plan.zone_shapes, in_specs=[_ANY] * (n + nz), out_specs=[_ANY] * nz,
        input_output_aliases={n + g: g for g in range(nz)},
        scratch_shapes=[pltpu.SemaphoreType.DMA((n * (NDEV - 1),)), pltpu.SemaphoreType.DMA((n * (NDEV - 1),)),
                        pltpu.SemaphoreType.DMA((n,))],
        compiler_params=pltpu.CompilerParams(has_side_effects=True),
    )(*plan.flat, *zones)


def _exchange_start(plan, name, zones):
    n, nz = len(plan.flat), len(plan.groups)
    hbm = lambda a: pltpu.HBM(a.shape, a.dtype)

    def body(*refs):
        ins, zin = refs[:n], refs[n:n + nz]
        send_sems, recv_sems = refs[n + nz], refs[n + nz + 1]
        token = refs[-1]
        for cp in plan.remote_copies(ins, zin, send_sems, recv_sems):
            cp.start()
        token[...] = jnp.zeros_like(token)

    outs = pl.pallas_call(
        body, name=name,
        out_shape=(pltpu.SemaphoreType.DMA((n * (NDEV - 1),)), pltpu.SemaphoreType.DMA((n * (NDEV - 1),)),
                   *[hbm(a) for a in plan.flat], *[hbm(z) for z in plan.zone_shapes], _sds((8, 128), F32)),
        in_specs=[_HBM] * (n + nz),
        out_specs=(_SEM, _SEM, *[_HBM] * (n + nz), pl.BlockSpec(memory_space=pltpu.VMEM)),
        input_output_aliases={i: 2 + i for i in range(n + nz)},
        compiler_params=pltpu.CompilerParams(has_side_effects=pltpu.SideEffectType.DATAFLOW_SIDE_EFFECTING),
    )(*[pltpu.with_memory_space_constraint(a, pltpu.HBM) for a in plan.flat],
      *[pltpu.with_memory_space_constraint(z, pltpu.HBM) for z in zones])
    return outs[:-1], outs[-1]


def _exchange_wait(plan, name, handle, after):
    n, nz = len(plan.flat), len(plan.groups)
    send_sems, recv_sems = handle[0], handle[1]
    thru = handle[2:]
    hbm = lambda a: pltpu.HBM(a.shape, a.dtype)

    def body(*refs):
        ins, zin = refs[:n], refs[n:n + nz]
        ssem, rsem = refs[n + nz], refs[n + nz + 1]
        for cp in plan.remote_copies(ins, zin, ssem, rsem):
            cp.wait_send()
            cp.wait_recv()

    outs = pl.pallas_call(
        body, name=name, out_shape=tuple(hbm(a) for a in thru),
        in_specs=[_HBM] * (n + nz) + [_SEM, _SEM, _ANY], out_specs=tuple([_HBM] * (n + nz)),
        input_output_aliases={i: i for i in range(n + nz)},
        compiler_params=pltpu.CompilerParams(has_side_effects=pltpu.SideEffectType.DATAFLOW_SIDE_EFFECTING),
    )(*thru, send_sems, recv_sems, after)
    return list(outs[n:])


def _sum_parts(parts, name):
    npart, R, C = parts.shape

    def body(p_ref, o_ref):
        s = p_ref[0]
        for k in range(1, npart):
            s = s + p_ref[k]
        o_ref[...] = s

    return pl.pallas_call(body, name=name, out_shape=_sds((R, C), F32))(parts)


def _rms_fwd(x, g):
    T = x.shape[0]
    tm = min(512, T)

    def body(x_ref, g_ref, o_ref):
        xv = x_ref[...]
        r = lax.rsqrt(jnp.mean(xv * xv, axis=-1, keepdims=True) + EPS)
        o_ref[...] = (xv * r * g_ref[...]).astype(BF16)

    return pl.pallas_call(
        body, name="rms_fwd", grid=(T // tm,), out_shape=_sds((T, D), BF16),
        in_specs=[pl.BlockSpec((tm, D), lambda i: (i, 0)), pl.BlockSpec((1, D), lambda i: (0, 0))],
        out_specs=pl.BlockSpec((tm, D), lambda i: (i, 0)), compiler_params=_cp("parallel"))(x, g)


def _mm_in(h, w):
    T = h.shape[0]
    tm = min(512, T)

    def body(h_ref, w_ref, q_ref, r_ref):
        hv = h_ref[...]
        q_ref[...] = _dot(hv, w_ref[:, :NQKV]).astype(BF16)
        r_ref[...] = _dot(hv, w_ref[:, NQKV:])

    return pl.pallas_call(
        body, name="mm_in", grid=(T // tm,), out_shape=[_sds((T, NQKV), BF16), _sds((T, NREST), F32)],
        in_specs=[pl.BlockSpec((tm, D), lambda i: (i, 0)), pl.BlockSpec((D, NPAD), lambda i: (0, 0))],
        out_specs=[pl.BlockSpec((tm, NQKV), lambda i: (i, 0)), pl.BlockSpec((tm, NREST), lambda i: (i, 0))],
        compiler_params=_cp("parallel"))(h, w)


AUG_Q1, AUG_K1 = 3, 0


def _aug_lane0(h):
    return HD * (1 - h % 2)


def _aug_tables():
    selq = np.zeros((NPAIR, 3 * 128, 128), np.float32)
    selk = np.zeros((NPAIR, 3 * 128, 128), np.float32)
    oneq = np.zeros((NPAIR, 1, 128), np.float32)
    onek = np.zeros((NPAIR, 1, 128), np.float32)
    for h in range(2 * NPAIR):
        p, l0 = h // 2, _aug_lane0(h)
        for piece in range(3):
            selq[p, 128 * piece + h, l0 + piece] = 1.0
            selk[p, 128 * piece + h, l0 + 3 + piece] = -1.0
            oneq[p, 0, l0 + AUG_Q1 + piece] = 1.0
            onek[p, 0, l0 + AUG_K1 + piece] = 1.0
    return [jnp.asarray(a, BF16) for a in (selq, selk, oneq, onek)]


def _gate_fwd(p_rest, bf, S, TB):
    T = p_rest.shape[0]
    B, nb = T // S, S // TB
    selq, selk, oneq, onek = _aug_tables()

    def body(z_ref, b_ref, sq_ref, sk_ref, oq_ref, ok_ref, aq_ref, ak_ref):
        tri = (lax.broadcasted_iota(jnp.int32, (TB, TB), 0) >= lax.broadcasted_iota(jnp.int32, (TB, TB), 1)).astype(F32)
        carry = jnp.zeros((1, 128), F32)
        for j in range(nb):
            rows = slice(j * TB, (j + 1) * TB)
            z = z_ref[rows, :] + b_ref[...]
            lf = jnp.minimum(z, 0.0) - jnp.log(1.0 + jnp.exp(-jnp.abs(z)))
            cb = _dot(tri, lf, HIGHEST) + carry
            carry = cb[TB - 1:TB, :]
            hi = cb.astype(BF16)
            mid = (cb - hi.astype(F32)).astype(BF16)
            lo = (cb - hi.astype(F32) - mid.astype(F32)).astype(BF16)
            pieces = jnp.concatenate([hi, mid, lo], axis=1)
            for p in range(NPAIR):
                aq_ref[p, rows, :] = (_dot(pieces, sq_ref[p]) + oq_ref[p].astype(F32)).astype(BF16)
                ak_ref[p, rows, :] = (_dot(pieces, sk_ref[p]) + ok_ref[p].astype(F32)).astype(BF16)

    full = lambda a: pl.BlockSpec(a.shape, lambda b: (0,) * a.ndim)
    return pl.pallas_call(
        body, name="gate_fwd", grid=(B,), out_shape=[_sds((NPAIR, T, 128), BF16), _sds((NPAIR, T, 128), BF16)],
        in_specs=[pl.BlockSpec((S, 128), lambda b: (b, 4)), pl.BlockSpec((1, 128), lambda b: (0, 0)),
                  full(selq), full(selk), full(oneq), full(onek)],
        out_specs=[pl.BlockSpec((NPAIR, S, 128), lambda b: (0, b, 0)), pl.BlockSpec((NPAIR, S, 128), lambda b: (0, b, 0))],
        compiler_params=_cp("parallel"))(p_rest, bf, selq, selk, oneq, onek)


def _bias_table(S, n_key, n_query, fox):
    unit = min(n_key, n_query)
    d_min = -(n_query // unit - 1)
    d = np.arange(d_min, S // unit)[:, None, None]
    delta = d * unit + np.arange(n_query)[None, None, :] - np.arange(n_key)[None, :, None]
    if fox:
        mult = (delta >= 0).astype(np.float64)
    else:
        mult = np.zeros(delta.shape)
        for w, dil in DILATION_PAIRS:
            mult += (delta >= 0) & (delta % dil == 0) & (delta <= w)
    with np.errstate(divide="ignore"):
        tab = np.where(mult > 0, np.log(np.maximum(mult, 1e-30)), NEG)
    tab = np.concatenate([tab, np.full((1, n_key, n_query), NEG)], axis=0)
    return jnp.asarray(tab, F32)


def _head_masks():
    lane = lax.broadcasted_iota(jnp.int32, (1, 128), 1)
    return [lane < HD, lane >= HD]


def _transpose_bf16(a):
    return a.astype(F32).T.astype(BF16)


def _col_reduce(x, op):
    while x.shape[0] > 8:
        half = x.shape[0] // 2
        x = op(x[:half], x[half:])
    return jnp.max(x, axis=0, keepdims=True) if op is jnp.maximum else jnp.sum(x, axis=0, keepdims=True)


def _attn_fwd(p_qkv, col0, tab, S, aq=None, ak=None, name="attn"):
    T = p_qkv.shape[0]
    n_tab, TB, TQ = tab.shape
    nb, nq, r = S // TB, S // TQ, TQ // TB
    B = T // S
    fox = aq is not None

    def body(*refs):
        if fox:
            q_ref, k_ref, v_ref, tab_ref, aq_ref, ak_ref, o_ref, l_ref, kat_ref, vta_scr, acc_scr, st_scr = refs
        else:
            q_ref, k_ref, v_ref, tab_ref, o_ref, l_ref, kat_ref, vta_scr, acc_scr, st_scr = refs
        i = pl.program_id(2)
        hms = _head_masks()
        top = lax.broadcasted_iota(jnp.int32, (128, 1), 0) < HD
        last_key_block = r * i + r - 1

        @pl.when(i == 0)
        def _():
            for jj in range(nb):
                rows = slice(jj * TB, (jj + 1) * TB)
                vt = _transpose_bf16(v_ref[rows, :])
                for hh in range(2):
                    vta_scr[hh, jj] = jnp.where(top == (hh == 0), vt, jnp.ones_like(vt))
                    spare_k = ak_ref[0, rows, :] if fox else jnp.zeros((TB, 128), BF16)
                    kat_ref[0, 0, hh, jj] = _transpose_bf16(jnp.where(hms[hh], k_ref[rows, :], spare_k))

        q2 = q_ref[...] * 0.125
        spare_q = aq_ref[0] if fox else jnp.zeros_like(q2)
        qa = [jnp.where(hm, q2, spare_q) for hm in hms]
        acc_scr[...] = jnp.zeros_like(acc_scr)

        def scores(j):
            jc = jnp.minimum(j, nb - 1)
            rows = pl.ds(pl.multiple_of(jc * TB, TB), TB)
            k2 = k_ref[rows, :]
            bias = tab_ref[jnp.where(j <= last_key_block, r * i - j + (r - 1), n_tab - 1)]
            return [_dot_nt(jnp.where(hms[hh], k2, ak_ref[0, rows, :]) if fox else k2, qa[hh]) + bias for hh in range(2)]

        def absorb(j, sts, stats):
            jc = jnp.minimum(j, nb - 1)
            alphas, pts = [], []
            for hh in range(2):
                m_old = stats[2 * hh]
                m_new = jnp.maximum(m_old, _col_reduce(sts[hh], jnp.maximum))
                pts.append(jnp.exp(sts[hh] - m_new).astype(BF16))
                alphas.append(jnp.exp(m_old - m_new))
                stats[2 * hh] = m_new
            pvs = [_dot(vta_scr[hh, jc], pts[hh]) for hh in range(2)]
            for hh in range(2):
                half = slice(HD * hh, HD * (hh + 1))
                ones_row = HD * (1 - hh)
                acc_scr[half, :] = alphas[hh] * acc_scr[half, :] + pvs[hh][half]
                stats[2 * hh + 1] = alphas[hh] * stats[2 * hh + 1] + pvs[hh][ones_row:ones_row + 1]

        def kv_pair(jj, carry):
            stats = list(carry)
            j0 = 2 * jj
            st0 = [st_scr[hh] for hh in range(2)]
            st1 = scores(j0 + 1)
            absorb(j0, st0, stats)
            nxt = scores(j0 + 2)
            for hh in range(2):
                st_scr[hh] = nxt[hh]
            absorb(j0 + 1, st1, stats)
            return tuple(stats)

        first = scores(0)
        for hh in range(2):
            st_scr[hh] = first[hh]
        row = lambda v: jnp.full((1, TQ), v, F32)
        m0, l0, m1, l1 = lax.fori_loop(0, (last_key_block + 2) // 2, kv_pair, (row(NEG), row(0.0), row(NEG), row(0.0)))
        o_ref[...] = (acc_scr[...] * jnp.where(top, 1.0 / l0, 1.0 / l1)).T
        for hh, lse in enumerate((m0 + jnp.log(l0), m1 + jnp.log(l1))):
            for t in range(r):
                l_ref[t, hh] = lse[:, t * TB:(t + 1) * TB]

    in_specs = [pl.BlockSpec((TQ, 128), lambda b, p, i: (b * nq + i, col0 + p)),
                pl.BlockSpec((S, 128), lambda b, p, i: (b, col0 + NPAIR + p)),
                pl.BlockSpec((S, 128), lambda b, p, i: (b, col0 + 2 * NPAIR + p)),
                pl.BlockSpec(tab.shape, lambda b, p, i: (0, 0, 0))]
    args = [p_qkv, p_qkv, p_qkv, tab]
    if fox:
        in_specs += [pl.BlockSpec((1, TQ, 128), lambda b, p, i: (p, b * nq + i, 0)),
                     pl.BlockSpec((1, S, 128), lambda b, p, i: (p, b, 0))]
        args += [aq, ak]
    return pl.pallas_call(
        body, name=name, grid=(B, NPAIR, nq),
        out_shape=[_sds((T, WA), F32), _sds((T // TB, 2 * NPAIR, 1, TB), F32), _sds((B, NPAIR, 2, nb, 128, TB), BF16)],
        in_specs=in_specs,
        out_specs=[pl.BlockSpec((TQ, 128), lambda b, p, i: (b * nq + i, p)),
                   pl.BlockSpec((r, 2, 1, TB), lambda b, p, i: (b * nq + i, p, 0, 0)),
                   pl.BlockSpec((1, 1, 2, nb, 128, TB), lambda b, p, i: (b, p, 0, 0, 0, 0))],
        scratch_shapes=[pltpu.VMEM((2, nb, 128, TB), BF16), pltpu.VMEM((128, TQ), F32), pltpu.VMEM((2, TB, TQ), F32)],
        compiler_params=_cp("parallel", "parallel", "arbitrary"))(*args)


def _phase_copies(src, phases, length):
    for r in range(1, 8):
        phases[r, 0:length - 8, :] = src[pl.ds(r, length - 8), :]


def _tap(src, phases, offset, rows):
    r = offset % 8
    return src[pl.ds(offset, rows), :] if r == 0 else phases[r, pl.ds(offset - r, rows), :]


def _conv_taps(src, phases, w_ref, first_row, rows):
    acc = w_ref[0:1, :] * _tap(src, phases, first_row, rows)
    for k in range(1, CONV_K):
        acc = acc + w_ref[k:k + 1, :] * _tap(src, phases, first_row + k, rows)
    return acc


def _mix_fwd(o_a, o_b, p_rest, g_a, g_b, cw, cb, ng, nbias, S):
    T = o_a.shape[0]
    tm = min(256, S)
    nps = S // tm
    HALO = 32

    def body(oa_ref, ob_ref, pt_ref, ph_ref, ga_ref, gb_ref, cw_ref, cb_ref, ng_ref, nb_ref, y_ref, cv_ref, buf, phases):
        i = pl.program_id(0)
        first = (i % nps) == 0
        for o_ref, g_ref, c0 in ((oa_ref, ga_ref, 0), (ob_ref, gb_ref, WA)):
            o = o_ref[...]
            r = lax.rsqrt(jnp.mean(o * o, axis=-1, keepdims=True) + EPS)
            y_ref[:, c0:c0 + WA] = (o * r * g_ref[...]).astype(BF16)
        ph = jnp.where(first, 0.0, ph_ref[...])
        buf[0:HALO, :] = ph[:, :CC] * _sigmoid(ph[:, CC:])
        pt = pt_ref[...]
        buf[HALO:HALO + tm, :] = pt[:, :CC] * _sigmoid(pt[:, CC:])
        _phase_copies(buf, phases, tm + HALO)
        cv = _conv_taps(buf, phases, cw_ref, HALO - CONV_K + 1, tm) + cb_ref[...]
        cv_ref[...] = cv
        xc = cv - jnp.mean(cv, axis=-1, keepdims=True)
        lo = xc * lax.rsqrt(jnp.mean(xc * xc, axis=-1, keepdims=True) + EPS) * ng_ref[...] + nb_ref[...]
        y_ref[:, 2 * WA:] = (lo * _sigmoid(lo)).astype(BF16)

    row = lambda w: pl.BlockSpec((1, w), lambda i: (0, 0))
    return pl.pallas_call(
        body, name="mix_fwd", grid=(T // tm,), out_shape=[_sds((T, D), BF16), _sds((T, CC), F32)],
        in_specs=[pl.BlockSpec((tm, WA), lambda i: (i, 0)), pl.BlockSpec((tm, WA), lambda i: (i, 0)),
                  pl.BlockSpec((tm, 2 * CC), lambda i: (i, 0)),
                  pl.BlockSpec((HALO, 2 * CC), lambda i: (jnp.maximum(i * (tm // HALO) - 1, 0), 0)),
                  row(WA), row(WA), pl.BlockSpec((32, CC), lambda i: (0, 0)), row(CC), row(CC), row(CC)],
        out_specs=[pl.BlockSpec((tm, D), lambda i: (i, 0)), pl.BlockSpec((tm, CC), lambda i: (i, 0))],
        scratch_shapes=[pltpu.VMEM((tm + HALO, CC), F32), pltpu.VMEM((8, tm + HALO, CC), F32)],
        compiler_params=_cp("parallel"))(o_a, o_b, p_rest, p_rest, g_a, g_b, cw, cb, ng, nbias)


def _mm_res(a, w, resid, g, name):
    nk, T, kb = a.shape
    tm = min(512, T)

    def body(a_ref, w_ref, r_ref, g_ref, x_ref, h_ref):
        xv = r_ref[...]
        for k in range(nk):
            xv = xv + _dot(a_ref[k], w_ref[k])
        x_ref[...] = xv
        r = lax.rsqrt(jnp.mean(xv * xv, axis=-1, keepdims=True) + EPS)
        h_ref[...] = (xv * r * g_ref[...]).astype(BF16)

    return pl.pallas_call(
        body, name=name, grid=(T // tm,), out_shape=[_sds((T, D), F32), _sds((T, D), BF16)],
        in_specs=[pl.BlockSpec((nk, tm, kb), lambda i: (0, i, 0)), pl.BlockSpec((nk, kb, D), lambda i: (0, 0, 0)),
                  pl.BlockSpec((tm, D), lambda i: (i, 0)), pl.BlockSpec((1, D), lambda i: (0, 0))],
        out_specs=[pl.BlockSpec((tm, D), lambda i: (i, 0)), pl.BlockSpec((tm, D), lambda i: (i, 0))],
        compiler_params=_cp("parallel"))(a, w, resid, g)


def _up_fwd(h2, w_up, fcw, fcb, S):
    T = h2.shape[0]
    tm = min(512, S)
    nps = S // tm

    def body(h_ref, hh_ref, w_ref, cw_ref, cb_ref, u_ref, c_ref, hid_ref, buf, hbuf):
        i = pl.program_id(1)
        first = (i % nps) == 0
        hbuf[0:tm, :] = h_ref[...]
        hbuf[tm:tm + 8, :] = jnp.where(first, jnp.zeros_like(hh_ref[...]), hh_ref[...])
        c = []
        for half in range(2):
            ua = _dot(hbuf[...], w_ref[half, 0])
            u = ua[0:tm]
            u_ref[half, 0] = u
            buf[half, 0:8, :] = ua[tm:tm + 8]
            buf[half, 8:8 + tm, :] = u
            cw = cw_ref[half, 0]
            c.append(cb_ref[half, 0] + cw[0:1] * buf[half, pl.ds(6, tm), :] + cw[1:2] * buf[half, pl.ds(7, tm), :] + cw[2:3] * u)
        for half in range(2):
            c_ref[half, 0] = c[half]
        hid_ref[0] = (c[0] * _sigmoid(c[0]) * c[1]).astype(BF16)

    return pl.pallas_call(
        body, name="up_fwd", grid=(NJ, T // tm),
        out_shape=[_sds((2, NJ, T, FB), F32), _sds((2, NJ, T, FB), F32), _sds((NJ, T, FB), BF16)],
        in_specs=[pl.BlockSpec((tm, D), lambda j, i: (i, 0)),
                  pl.BlockSpec((8, D), lambda j, i: (jnp.maximum(i * (tm // 8) - 1, 0), 0)),
                  pl.BlockSpec((2, 1, D, FB), lambda j, i: (0, j, 0, 0)),
                  pl.BlockSpec((2, 1, 3, FB), lambda j, i: (0, j, 0, 0)),
                  pl.BlockSpec((2, 1, 1, FB), lambda j, i: (0, j, 0, 0))],
        out_specs=[pl.BlockSpec((2, 1, tm, FB), lambda j, i: (0, j, i, 0)), pl.BlockSpec((2, 1, tm, FB), lambda j, i: (0, j, i, 0)),
                   pl.BlockSpec((1, tm, FB), lambda j, i: (j, i, 0))],
        scratch_shapes=[pltpu.VMEM((2, tm + 8, FB), F32), pltpu.VMEM((tm + 8, D), BF16)],
        compiler_params=_cp("parallel", "parallel"))(h2, h2, w_up, fcw, fcb)


def _final_loss(x, tgt, g):
    T = x.shape[0]
    tm = min(512, T)

    def body(x_ref, t_ref, g_ref, dx_ref, loss_ref, dg_ref):
        @pl.when(pl.program_id(0) == 0)
        def _():
            loss_ref[...] = jnp.zeros_like(loss_ref)
            dg_ref[...] = jnp.zeros_like(dg_ref)

        xv, gv = x_ref[...], g_ref[...]
        r = lax.rsqrt(jnp.mean(xv * xv, axis=-1, keepdims=True) + EPS)
        e = xv * r * gv - t_ref[...]
        loss_ref[...] += 0.5 * jnp.sum(jnp.mean(e * e, axis=-1, keepdims=True), axis=0, keepdims=True)
        dx, dg = _rms_bwd(e * (1.0 / D), xv, gv)
        dx_ref[...] = dx
        dg_ref[...] += dg

    return pl.pallas_call(
        body, name="final_loss", grid=(T // tm,), out_shape=[_sds((T, D), F32), _sds((1, 128), F32), _sds((1, D), F32)],
        in_specs=[pl.BlockSpec((tm, D), lambda i: (i, 0)), pl.BlockSpec((tm, D), lambda i: (i, 0)), pl.BlockSpec((1, D), lambda i: (0, 0))],
        out_specs=[pl.BlockSpec((tm, D), lambda i: (i, 0)), pl.BlockSpec((1, 128), lambda i: (0, 0)), pl.BlockSpec((1, D), lambda i: (0, 0))],
        compiler_params=_cp("arbitrary"))(x, tgt, g)


def _down_bwd(dx2, w_down, u, c, fcw, S):
    T = dx2.shape[0]
    tm = min(512, S)
    nps = S // tm
    nblk8 = T // 8

    def body(dx_ref, dxn_ref, wd_ref, ut_ref, ct_ref, cn_ref, cw_ref, du_ref, dcw_ref, dxb, dcbuf, dsh):
        i = pl.program_id(1)
        last = (i % nps) == nps - 1

        @pl.when(i == 0)
        def _():
            dcw_ref[...] = jnp.zeros_like(dcw_ref)

        dxb[0:tm, :] = dx_ref[...].astype(BF16)
        dxb[tm:tm + 8, :] = jnp.where(last, 0.0, dxn_ref[...]).astype(BF16)
        dh = _dot_nt(dxb[...], wd_ref[0])
        ce = [jnp.concatenate([ct_ref[half, 0], cn_ref[half, 0]], axis=0) for half in range(2)]
        sg = _sigmoid(ce[0])
        dc = [dh * ce[1] * (sg * (1.0 + ce[0] * (1.0 - sg))), dh * (ce[0] * sg)]
        for half in range(2):
            cw = cw_ref[half, 0]
            dcbuf[...] = dc[half]
            for k in range(2):
                dsh[k] = dcbuf[pl.ds(k + 1, tm), :]
            ahead = [dc[half][0:tm], dsh[0], dsh[1]]
            du_ref[half, 0] = (cw[2:3] * ahead[0] + cw[1:2] * ahead[1] + cw[0:1] * ahead[2]).astype(BF16)
            uv = ut_ref[half, 0]
            for k in range(3):
                dcw_ref[half, 0, k:k + 1, :] += jnp.sum(ahead[2 - k] * uv, axis=0, keepdims=True)
            dcw_ref[half, 0, 3:4, :] += jnp.sum(ahead[0], axis=0, keepdims=True)

    ublk = lambda rows, imap: pl.BlockSpec((2, 1, rows, FB), imap)
    return pl.pallas_call(
        body, name="down_bwd", grid=(NJ, T // tm), out_shape=[_sds((2, NJ, T, FB), BF16), _sds((2, NJ, 8, FB), F32)],
        in_specs=[pl.BlockSpec((tm, D), lambda j, i: (i, 0)),
                  pl.BlockSpec((8, D), lambda j, i: (jnp.minimum((i + 1) * (tm // 8), nblk8 - 1), 0)),
                  pl.BlockSpec((1, FB, D), lambda j, i: (j, 0, 0)),
                  ublk(tm, lambda j, i: (0, j, i, 0)),
                  ublk(tm, lambda j, i: (0, j, i, 0)),
                  ublk(8, lambda j, i: (0, j, jnp.minimum((i + 1) * (tm // 8), nblk8 - 1), 0)),
                  pl.BlockSpec((2, 1, 3, FB), lambda j, i: (0, j, 0, 0))],
        out_specs=[ublk(tm, lambda j, i: (0, j, i, 0)), pl.BlockSpec((2, 1, 8, FB), lambda j, i: (0, j, 0, 0))],
        scratch_shapes=[pltpu.VMEM((tm + 8, D), BF16), pltpu.VMEM((tm + 8, FB), F32), pltpu.VMEM((2, tm, FB), F32)],
        compiler_params=_cp("parallel", "arbitrary"))(dx2, dx2, w_down, u, c, c, fcw)


def _wgrad(a, b, tn, name, ga=1, gb=1):
    na, T, ka = a.shape
    nb, _, kb = b.shape
    tk = min(1024, T)
    nt = T // tk

    def body(a_ref, b_ref, o_ref, acc):
        t = pl.program_id(3)

        @pl.when(t == 0)
        def _():
            acc[...] = jnp.zeros_like(acc)

        bs = [b_ref[ib].astype(BF16) for ib in range(gb)]
        for ia in range(ga):
            av = a_ref[ia]
            for ib in range(gb):
                acc[ia, ib] += _dot_tn(av, bs[ib])

        @pl.when(t == nt - 1)
        def _():
            o_ref[...] = acc[...].astype(GRAD_DTYPE)

    return pl.pallas_call(
        body, name=name, grid=(na // ga, nb // gb, kb // tn, nt), out_shape=_sds((na, nb, ka, kb), GRAD_DTYPE),
        in_specs=[pl.BlockSpec((ga, tk, ka), lambda ia, ib, n, t: (ia, t, 0)), pl.BlockSpec((gb, tk, tn), lambda ia, ib, n, t: (ib, t, n))],
        out_specs=pl.BlockSpec((ga, gb, ka, tn), lambda ia, ib, n, t: (ia, ib, 0, n)),
        scratch_shapes=[pltpu.VMEM((ga, gb, ka, tn), F32)],
        compiler_params=_cp("parallel", "parallel", "parallel", "arbitrary"))(a, b)


def _dx_rms(dy, w, x, g, dres, blocked, name):
    T = x.shape[0]
    tm = min(256 if blocked else 512, T)

    def body(dy_ref, w_ref, x_ref, g_ref, r_ref, dx_ref, dg_ref):
        @pl.when(pl.program_id(0) == 0)
        def _():
            dg_ref[...] = jnp.zeros_like(dg_ref)

        if blocked:
            dh = _dot_nt(dy_ref[0, 0], w_ref[0, 0])
            for half, k in [(h, k) for k in range(NJ) for h in range(2)][1:]:
                dh = dh + _dot_nt(dy_ref[half, k], w_ref[half, k])
        else:
            dh = _dot_nt(dy_ref[...], w_ref[...])
        dx, dg = _rms_bwd(dh, x_ref[...], g_ref[...])
        dx_ref[...] = r_ref[...] + dx
        dg_ref[...] += dg

    if blocked:
        dy_spec = pl.BlockSpec((2, NJ, tm, FB), lambda i: (0, 0, i, 0))
        w_spec = pl.BlockSpec((2, NJ, D, FB), lambda i: (0, 0, 0, 0))
    else:
        dy_spec = pl.BlockSpec((tm, dy.shape[1]), lambda i: (i, 0))
        w_spec = pl.BlockSpec(w.shape, lambda i: (0, 0))
    tile = pl.BlockSpec((tm, D), lambda i: (i, 0))
    return pl.pallas_call(
        body, name=name, grid=(T // tm,), out_shape=[_sds((T, D), F32), _sds((1, D), F32)],
        in_specs=[dy_spec, w_spec, tile, pl.BlockSpec((1, D), lambda i: (0, 0)), tile],
        out_specs=[tile, pl.BlockSpec((1, D), lambda i: (0, 0))],
        compiler_params=pltpu.CompilerParams(dimension_semantics=("arbitrary",), vmem_limit_bytes=VMEM_LIMIT_BIG))(dy, w, x, g, dres)


def _mm_nt(a, w, name):
    T = a.shape[0]
    tm = min(512, T)

    def body(a_ref, w_ref, o_ref):
        o_ref[...] = _dot_nt(a_ref[...].astype(BF16), w_ref[...])

    return pl.pallas_call(
        body, name=name, grid=(T // tm,), out_shape=_sds((T, D), F32),
        in_specs=[pl.BlockSpec((tm, D), lambda i: (i, 0)), pl.BlockSpec((D, D), lambda i: (0, 0))],
        out_specs=pl.BlockSpec((tm, D), lambda i: (i, 0)), compiler_params=_cp("parallel"))(a, w)


def _mix_bwd(dy, o_a, o_b, p_rest, cv, g_a, g_b, cw, ng, nbias, S):
    T = dy.shape[0]
    tm = min(256, S)
    nps = S // tm
    HALO = 32
    nblkh = T // HALO
    RE = tm + HALO

    def body(dy_ref, dyn_ref, oa_ref, ob_ref, pt_ref, cvt_ref, cvn_ref, ga_ref, gb_ref, cw_ref, ng_ref, nb_ref,
             doa_ref, dda_ref, dob_ref, ddb_ref, dg_ref, dgn_ref, dcw_ref, dcn_ref, dvbuf, dph):
        i = pl.program_id(0)
        last = (i % nps) == nps - 1

        @pl.when(i == 0)
        def _():
            dgn_ref[...] = jnp.zeros_like(dgn_ref)
            dcw_ref[...] = jnp.zeros_like(dcw_ref)
            dcn_ref[...] = jnp.zeros_like(dcn_ref)

        head_of = (lax.broadcasted_iota(jnp.int32, (8, WA), 1) // HD == lax.broadcasted_iota(jnp.int32, (8, WA), 0)).astype(F32)
        for n, (o_ref, g_ref, do_ref, dd_ref) in enumerate(((oa_ref, ga_ref, doa_ref, dda_ref), (ob_ref, gb_ref, dob_ref, ddb_ref))):
            o = o_ref[...]
            do, dg = _rms_bwd(dy_ref[:, n * WA:(n + 1) * WA], o, g_ref[...])
            dob = do.astype(BF16)
            do_ref[...] = dob
            ddt = _dot_nt(head_of, dob.astype(F32) * o, HIGHEST)
            for hd in range(8):
                dd_ref[0, hd] = ddt[hd:hd + 1, :]
            dgn_ref[n:n + 1, :] += dg

        pt = pt_ref[...]
        gv, sgg = pt[:, :CC], _sigmoid(pt[:, CC:])
        glu = gv * sgg
        cv = jnp.concatenate([cvt_ref[...], cvn_ref[...]], axis=0)
        xc = cv - jnp.mean(cv, axis=-1, keepdims=True)
        r = lax.rsqrt(jnp.mean(xc * xc, axis=-1, keepdims=True) + EPS)
        xh = xc * r
        lo = xh * ng_ref[...] + nb_ref[...]
        sg = _sigmoid(lo)
        dyc = jnp.concatenate([dy_ref[:, 2 * WA:], jnp.where(last, 0.0, dyn_ref[...])], axis=0)
        dlo = dyc * (sg * (1.0 + lo * (1.0 - sg)))
        dcn_ref[0:1, :] += jnp.sum(dlo[0:tm] * xh[0:tm], axis=0, keepdims=True)
        dcn_ref[1:2, :] += jnp.sum(dlo[0:tm], axis=0, keepdims=True)
        dxh = dlo * ng_ref[...]
        dcv = r * (dxh - jnp.mean(dxh, axis=-1, keepdims=True) - xh * jnp.mean(dxh * xh, axis=-1, keepdims=True))
        dvbuf[...] = dcv
        _phase_copies(dvbuf, dph, RE)
        dct = dcv[0:tm]
        dcw_ref[CONV_K:CONV_K + 1, :] += jnp.sum(dct, axis=0, keepdims=True)
        dglu = jnp.zeros((tm, CC), F32)
        for k in range(CONV_K):
            ahead = _tap(dvbuf, dph, CONV_K - 1 - k, tm)
            dcw_ref[k:k + 1, :] += jnp.sum(ahead * glu, axis=0, keepdims=True)
            dglu = dglu + cw_ref[k:k + 1, :] * ahead
        dg_ref[:, :CC] = (dglu * sgg).astype(BF16)
        dg_ref[:, CC:] = (dglu * gv * sgg * (1.0 - sgg)).astype(BF16)

    row = lambda w: pl.BlockSpec((1, w), lambda i: (0, 0))
    tile = lambda w: pl.BlockSpec((tm, w), lambda i: (i, 0))
    nxt = lambda i: jnp.minimum((i + 1) * (tm // HALO), nblkh - 1)
    const = lambda r, w: pl.BlockSpec((r, w), lambda i: (0, 0))
    ddrow = pl.BlockSpec((1, 8, 1, tm), lambda i: (i, 0, 0, 0))
    return pl.pallas_call(
        body, name="mix_bwd", grid=(T // tm,),
        out_shape=[_sds((T, WA), BF16), _sds((T // tm, 8, 1, tm), F32), _sds((T, WA), BF16), _sds((T // tm, 8, 1, tm), F32),
                   _sds((T, 2 * CC), BF16), _sds((8, WA), F32), _sds((32, CC), F32), _sds((8, CC), F32)],
        in_specs=[tile(D), pl.BlockSpec((HALO, CC), lambda i: (nxt(i), 3)), tile(WA), tile(WA), tile(2 * CC),
                  tile(CC), pl.BlockSpec((HALO, CC), lambda i: (nxt(i), 0)),
                  row(WA), row(WA), const(32, CC), row(CC), row(CC)],
        out_specs=[tile(WA), ddrow, tile(WA), ddrow, tile(2 * CC), const(8, WA), const(32, CC), const(8, CC)],
        scratch_shapes=[pltpu.VMEM((RE, CC), F32), pltpu.VMEM((8, RE, CC), F32)],
        compiler_params=_cp("arbitrary"))(dy, dy, o_a, o_b, p_rest, cv, cv, g_a, g_b, cw, ng, nbias)


def _attn_bwd(p_qkv, col0, tab, do, lse, dd, kat, S, aq=None, ak=None, name="attn_bwd"):
    T = p_qkv.shape[0]
    n_tab, TK, TB = tab.shape
    nb, nkb, r = S // TB, S // TK, TK // TB
    B = T // S
    fox = aq is not None

    def body(*refs):
        if fox:
            (q_ref, k_ref, v_ref, tab_ref, do_ref, l_ref, dd_ref, kat_ref, aq_ref, ak_ref,
             dq_ref, dk_ref, dv_ref, dqa_ref, dka_ref, dqt_scr, dk_scr, dv_scr, sd_scr) = refs
        else:
            (q_ref, k_ref, v_ref, tab_ref, do_ref, l_ref, dd_ref, kat_ref,
             dq_ref, dk_ref, dv_ref, dqt_scr, dk_scr, dv_scr, sd_scr) = refs
        j = pl.program_id(2)
        hms = _head_masks()
        first_q_block = r * j

        @pl.when(j == 0)
        def _():
            dqt_scr[...] = jnp.zeros_like(dqt_scr)

        dk_scr[...] = jnp.zeros_like(dk_scr)
        dv_scr[...] = jnp.zeros_like(dv_scr)
        k2, v2 = k_ref[...], v_ref[...]
        ka = [jnp.where(hm, k2, ak_ref[0] if fox else jnp.zeros_like(k2)) for hm in hms]
        kat = [jnp.concatenate([kat_ref[0, 0, hh, t] for t in range(r)], axis=1) if r > 1 else kat_ref[0, 0, hh, 0]
               for hh in range(2)]

        def operands(i):
            ic = jnp.minimum(i, nb - 1)
            rows = pl.ds(pl.multiple_of(ic * TB, TB), TB)
            q2 = q_ref[rows, :] * 0.125
            do2 = do_ref[rows, :]
            qas = [jnp.where(hms[hh], q2, aq_ref[0, rows, :] if fox else jnp.zeros_like(q2)) for hh in range(2)]
            dohs = [jnp.where(hms[hh], do2, jnp.zeros_like(do2)) for hh in range(2)]
            return ic, qas, dohs

        def scores(i):
            ic, qas, dohs = operands(i)
            bias = tab_ref[jnp.where(i < nb, i - first_q_block, n_tab - 1)]
            return [(_dot_nt(ka[hh], qas[hh]) + bias, _dot_nt(v2, dohs[hh])) for hh in range(2)]

        def absorb(i, sd):
            ic, qas, dohs = operands(i)
            for hh in range(2):
                st, dpt = sd[hh]
                pt = jnp.exp(st - l_ref[ic, hh])
                dsb = (pt * (dpt - dd_ref[ic, hh])).astype(BF16)
                dv_scr[...] += _dot(pt.astype(BF16), dohs[hh])
                dk_scr[hh] += _dot(dsb, qas[hh])
                dqt_scr[hh, ic] += _dot(kat[hh], dsb)

        def q_pair(ii, carry):
            i0 = first_q_block + 2 * ii
            sd0 = [(sd_scr[hh, 0], sd_scr[hh, 1]) for hh in range(2)]
            sd1 = scores(i0 + 1)
            absorb(i0, sd0)
            nxt = scores(i0 + 2)
            for hh in range(2):
                sd_scr[hh, 0], sd_scr[hh, 1] = nxt[hh]
            absorb(i0 + 1, sd1)
            return carry

        first = scores(first_q_block)
        for hh in range(2):
            sd_scr[hh, 0], sd_scr[hh, 1] = first[hh]
        lax.fori_loop(0, (nb - first_q_block + 1) // 2, q_pair, 0)
        dk_ref[...] = jnp.where(hms[0], dk_scr[0], dk_scr[1]).astype(BF16)
        dv_ref[...] = dv_scr[...].astype(BF16)
        if fox:
            dka_ref[...] = jnp.where(hms[0], dk_scr[1], dk_scr[0])

        @pl.when(j == nkb - 1)
        def _():
            for ii in range(nb):
                t0, t1 = dqt_scr[0, ii].T, dqt_scr[1, ii].T
                dq_ref[ii * TB:(ii + 1) * TB, :] = jnp.where(hms[0], t0, t1) * 0.125
                if fox:
                    dqa_ref[ii * TB:(ii + 1) * TB, :] = jnp.where(hms[0], t1, t0)

    seq = lambda c: pl.BlockSpec((S, 128), lambda b, p, j: (b, c + p))
    kvb = lambda c: pl.BlockSpec((TK, 128), lambda b, p, j: (b * nkb + j, c + p))
    stat = pl.BlockSpec((nb, 2, 1, TB), lambda b, p, j: (b, p, 0, 0))
    in_specs = [seq(col0), kvb(col0 + NPAIR), kvb(col0 + 2 * NPAIR), pl.BlockSpec(tab.shape, lambda b, p, j: (0, 0, 0)),
                seq(0), stat, stat, pl.BlockSpec((1, 1, 2, r, 128, TB), lambda b, p, j: (b, p, 0, j, 0, 0))]
    args = [p_qkv, p_qkv, p_qkv, tab, do, lse, dd, kat]
    out_shape = [_sds((T, WA), F32), _sds((T, WA), BF16), _sds((T, WA), BF16)]
    out_specs = [seq(0), kvb(0), kvb(0)]
    if fox:
        in_specs += [pl.BlockSpec((1, S, 128), lambda b, p, j: (p, b, 0)), pl.BlockSpec((1, TK, 128), lambda b, p, j: (p, b * nkb + j, 0))]
        args += [aq, ak]
        out_shape += [_sds((T, WA), F32), _sds((T, WA), F32)]
        out_specs += [seq(0), kvb(0)]
    return pl.pallas_call(
        body, name=name, grid=(B, NPAIR, nkb), out_shape=out_shape, in_specs=in_specs, out_specs=out_specs,
        scratch_shapes=[pltpu.VMEM((2, nb, 128, TB), F32), pltpu.VMEM((2, TK, 128), F32), pltpu.VMEM((TK, 128), F32),
                        pltpu.VMEM((2, 2, TK, TB), F32)],
        compiler_params=_cp("parallel", "parallel", "arbitrary"))(*args)


def _gate_bwd(dqa, dka, p_rest, bf, S, TB):
    T = p_rest.shape[0]
    B, nb = T // S, S // TB

    def body(dqa_ref, dka_ref, z_ref, b_ref, dz_ref, db_ref):
        @pl.when(pl.program_id(0) == 0)
        def _():
            db_ref[...] = jnp.zeros_like(db_ref)

        later = (lax.broadcasted_iota(jnp.int32, (TB, TB), 1) >= lax.broadcasted_iota(jnp.int32, (TB, TB), 0)).astype(F32)
        lane = lax.broadcasted_iota(jnp.int32, (1, 128), 1)
        src, head = lax.broadcasted_iota(jnp.int32, (WA, 128), 0), lax.broadcasted_iota(jnp.int32, (WA, 128), 1)
        spare0 = 128 * (head // 2) + HD * (1 - head % 2)
        pick_q = (src == spare0 + AUG_K1).astype(F32)
        pick_k = (src == spare0 + AUG_Q1).astype(F32)
        carry = jnp.zeros((1, 128), F32)
        dbs = jnp.zeros((1, 128), F32)
        for j in reversed(range(nb)):
            rows = slice(j * TB, (j + 1) * TB)
            dc = _dot(dqa_ref[rows, :], pick_q, HIGHEST) - _dot(dka_ref[rows, :], pick_k, HIGHEST)
            part = _dot(later, dc, HIGHEST)
            tot = part + carry
            carry = carry + part[0:1, :]
            z = z_ref[j * TB:(j + 1) * TB, :] + b_ref[...]
            dz = jnp.where(lane < 6, tot * _sigmoid(-z), 0.0)
            dz_ref[j * TB:(j + 1) * TB, :] = dz.astype(BF16)
            dbs = dbs + jnp.sum(dz, axis=0, keepdims=True)
        db_ref[...] += dbs

    return pl.pallas_call(
        body, name="gate_bwd", grid=(B,), out_shape=[_sds((T, 128), BF16), _sds((1, 128), F32)],
        in_specs=[pl.BlockSpec((S, WA), lambda b: (b, 0)), pl.BlockSpec((S, WA), lambda b: (b, 0)),
                  pl.BlockSpec((S, 128), lambda b: (b, 4)), pl.BlockSpec((1, 128), lambda b: (0, 0))],
        out_specs=[pl.BlockSpec((S, 128), lambda b: (b, 0)), pl.BlockSpec((1, 128), lambda b: (0, 0))],
        compiler_params=_cp("arbitrary"))(dqa, dka, p_rest, bf)


def _adamw(parts, w, m, v, name, row0=0, prev=None):
    npart, Rp, C = parts.shape
    R = w.shape[0]
    tr = Rp
    for cand in (256, 128, 64, 32, 16, 8):
        if Rp % cand == 0 and row0 % cand == 0 and cand * C * 4 <= (1 << 20):
            tr = cand
            break
    if Rp == R and R * C * 4 <= (1 << 20):
        tr = R
    assert Rp % tr == 0 and row0 % tr == 0, (name, Rp, row0, tr)
    b0 = row0 // tr

    def body(p_ref, w_ref, m_ref, v_ref, *rest):
        g_ref, d_ref, mo_ref, vo_ref = rest[-4:]
        g = p_ref[0].astype(F32)
        for k in range(1, npart):
            g = g + p_ref[k].astype(F32)
        mn = ADAM_B1 * m_ref[...] + (1.0 - ADAM_B1) * g
        vn = ADAM_B2 * v_ref[...] + (1.0 - ADAM_B2) * (g * g)
        m_hat = mn / (1.0 - ADAM_B1 ** ADAM_STEP)
        v_hat = vn / (1.0 - ADAM_B2 ** ADAM_STEP)
        g_ref[...] = g
        d_ref[...] = -ADAM_LR * (m_hat / (jnp.sqrt(v_hat) + ADAM_EPS) + ADAM_WD * w_ref[...])
        mo_ref[...] = mn
        vo_ref[...] = vn

    blk = pl.BlockSpec((tr, C), lambda i: (b0 + i, 0))
    prev = list(prev) if prev is not None else []
    return pl.pallas_call(
        body, name=name, grid=(Rp // tr,), out_shape=[_sds((R, C), F32)] * 4,
        in_specs=[pl.BlockSpec((npart, tr, C), lambda i: (0, i, 0)), blk, blk, blk] + [_ANY] * len(prev), out_specs=[blk] * 4,
        input_output_aliases={4 + i: i for i in range(len(prev))},
        compiler_params=_cp("parallel"))(parts, w, m, v, *prev)


def _pad_w_in(w):
    z = jnp.zeros(w.shape[:-1] + (NPAD - N_IN,), w.dtype)
    return jnp.concatenate([w[..., :O_FA], w[..., O_QB:], w[..., O_FA:O_QB], z], axis=-1)


def _unpad_w_in(w):
    n_mid = N_IN - O_QB
    return jnp.concatenate([w[..., :O_FA], w[..., O_FA + n_mid:O_FA + n_mid + 6], w[..., O_FA:O_FA + n_mid]], axis=-1)


def _local_step(x, tgt, W, S, late_weights=None, on_layer_grads=None, on_ffn_grads=None):
    T = x.shape[0]
    TB = min(256, S)
    TW = min(2 * TB, S)
    tab_fox, tab_dil = _bias_table(S, TW, TB, True), _bias_table(S, TW, TB, False)
    tabq_fox, tabq_dil = _bias_table(S, TB, TW, True), _bias_table(S, TB, TW, False)
    saved = []
    h = _rms_fwd(x, W["ln1_g"][0])
    for l in range(DEPTH):
        p_qkv, p_rest = _mm_in(h, W["w_in"][l])
        aq, ak = _gate_fwd(p_rest, W["b_forget"][l], S, TB)
        o_a, lse_a, kat_a = _attn_fwd(p_qkv, 0, tabq_fox, S, aq, ak, name="fox_fwd")
        o_b, lse_b, kat_b = _attn_fwd(p_qkv, 3 * NPAIR, tabq_dil, S, name="dil_fwd")
        y, cv = _mix_fwd(o_a, o_b, p_rest, W["g_out_fox"][l], W["g_out_dil"][l], W["conv_w"][l], W["conv_b"][l],
                     W["cnorm_g"][l], W["cnorm_b"][l], S)
        if l == 0 and late_weights is not None:
            W = late_weights(W, y)
        x1, h2 = _mm_res(y[None], W["w_o"][l][None], x, W["ln2_g"][l], name="mm_o")
        u, c, hid = _up_fwd(h2, W["w_up"][l], W["ffn_conv_w"][l], W["ffn_conv_b"][l], S)
        g_next = W["ln1_g"][l + 1] if l + 1 < DEPTH else W["g_final"]
        x2, h_next = _mm_res(hid, W["w_down"][l], x1, g_next, name="mm_down")
        saved.append(dict(x=x, h=h, p_qkv=p_qkv, p_rest=p_rest, aq=aq, ak=ak, o_a=o_a, lse_a=lse_a, o_b=o_b,
                          lse_b=lse_b, kat_a=kat_a, kat_b=kat_b, y=y, cv=cv, x1=x1, h2=h2, u=u, c=c, hid=hid))
        x, h = x2, h_next
    dx, loss, dg_final = _final_loss(x, tgt, W["g_final"])
    G = {k: [None] * DEPTH for k in ("ln1_g", "w_in", "b_forget", "g_out_fox", "g_out_dil", "conv_w", "conv_b", "cnorm_g",
                                     "cnorm_b", "w_o", "ln2_g", "w_up", "ffn_conv_w", "ffn_conv_b", "w_down")}
    G["g_final"] = dg_final
    for l in reversed(range(DEPTH)):
        sv = saved[l]
        du, dcw = _down_bwd(dx, W["w_down"][l], sv["u"], sv["c"], W["ffn_conv_w"][l], S)
        G["w_down"][l] = _wgrad(sv["hid"], dx[None], D, "wg_down", ga=NJ)[:, 0]
        G["ffn_conv_w"][l] = dcw[:, :, 0:3, :]
        G["ffn_conv_b"][l] = dcw[:, :, 3, :]
        G["w_up"][l] = _wgrad(sv["h2"][None], du.reshape(2 * NJ, T, FB), FB, "wg_up", gb=NJ)[0]
        token = on_ffn_grads(l, G) if on_ffn_grads is not None else None
        ln2 = W["ln2_g"][l] if token is None else W["ln2_g"][l] + token[0, 0]
        dx1, G["ln2_g"][l] = _dx_rms(du, W["w_up"][l], sv["x1"], ln2, dx, True, "dx_up")
        G["w_o"][l] = _wgrad(sv["y"][None], dx1[None], D, "wg_o")[0, 0]
        dy = _mm_nt(dx1, W["w_o"][l], "mm_dy")
        do_a, dd_a, do_b, dd_b, dgvgg, dgn, dcvw, dcn = _mix_bwd(
            dy, sv["o_a"], sv["o_b"], sv["p_rest"], sv["cv"], W["g_out_fox"][l], W["g_out_dil"][l], W["conv_w"][l],
            W["cnorm_g"][l], W["cnorm_b"][l], S)
        G["g_out_fox"][l], G["g_out_dil"][l] = dgn[0], dgn[1]
        G["conv_w"][l], G["conv_b"][l] = dcvw[:CONV_K], dcvw[CONV_K]
        G["cnorm_g"][l], G["cnorm_b"][l] = dcn[0], dcn[1]
        dq_a, dk_a, dv_a, dqa, dka = _attn_bwd(sv["p_qkv"], 0, tab_fox, do_a, sv["lse_a"], dd_a, sv["kat_a"], S, sv["aq"], sv["ak"],
                                               name="fox_bwd")
        dq_b, dk_b, dv_b = _attn_bwd(sv["p_qkv"], 3 * NPAIR, tab_dil, do_b, sv["lse_b"], dd_b, sv["kat_b"], S, name="dil_bwd")
        dfa, db = _gate_bwd(dqa, dka, sv["p_rest"], W["b_forget"][l], S, TB)
        G["b_forget"][l] = db[0, :6]
        dp = jnp.concatenate([dq_a.astype(BF16), dk_a, dv_a, dq_b.astype(BF16), dk_b, dv_b, dgvgg, dfa,
                              jnp.zeros((T, 128), BF16)], axis=1)
        G["w_in"][l] = _wgrad(sv["h"][None], dp[None], NPAD, "wg_in")[0, 0]
        token = on_layer_grads(l, G) if on_layer_grads is not None else None
        ln1 = W["ln1_g"][l] if token is None else W["ln1_g"][l] + token[0, 0]
        dx, G["ln1_g"][l] = _dx_rms(dp, W["w_in"][l], sv["x"], ln1, dx1, False, "dx_in")
    return loss, dx, G


_MATMUL_W = ("w_in", "w_o", "w_up", "w_down")
_SHARDED = _MATMUL_W + ("conv_w", "ffn_conv_w")
_SMALL = ("ln1_g", "b_forget", "g_out_fox", "g_out_dil", "conv_b", "cnorm_g", "cnorm_b", "ln2_g", "ffn_conv_b", "g_final")
_ORDER = ("ln1_g", "w_in", "b_forget", "g_out_fox", "g_out_dil", "conv_w", "conv_b", "cnorm_g", "cnorm_b", "w_o", "ln2_g",
          "w_up", "ffn_conv_w", "ffn_conv_b", "w_down", "g_final")


def _kernel_ready(k, zone):
    if k == "w_in":
        return _pad_w_in(zone.reshape(D, N_IN))
    if k == "w_o":
        return zone.reshape(D, D)
    if k == "w_up":
        return zone.reshape(2, NJ, D, FB)
    if k == "w_down":
        return zone.reshape(NJ, FB, D)
    if k == "conv_w":
        cw = jnp.transpose(zone.reshape(NDEV, CONV_K, CC // NDEV), (1, 0, 2)).reshape(CONV_K, CC)
        return jnp.concatenate([cw, jnp.zeros((1, CC), F32)], axis=0)
    return zone.reshape(2, NJ, 3, FB)


def _full_weights(P, gathered):
    W = {}
    row = lambda a: [a[l][None, :] for l in range(DEPTH)]
    for k in _SHARDED:
        W[k] = [None if gathered[k][l] is None else _kernel_ready(k, gathered[k][l]) for l in range(DEPTH)]
    W["ffn_conv_b"] = [P["ffn_conv_b"][l].reshape(2, NJ, 1, FB) for l in range(DEPTH)]
    W["b_forget"] = [jnp.concatenate([P["b_forget"][l], jnp.zeros((128 - 6,), F32)])[None, :] for l in range(DEPTH)]
    for k in ("ln1_g", "ln2_g", "g_out_fox", "g_out_dil", "conv_b", "cnorm_g", "cnorm_b"):
        W[k] = row(P[k])
    W["g_final"] = P["g_final"][None, :]
    return W


def kernel(x, ln1_g, w_in, b_forget, g_out_fox, g_out_dil, conv_w, conv_b, cnorm_g, cnorm_b, w_o, ln2_g, w_up, ffn_conv_w, ffn_conv_b, w_down, g_final, loss_target, m_ln1_g, m_w_in, m_b_forget, m_g_out_fox, m_g_out_dil, m_conv_w, m_conv_b, m_cnorm_g, m_cnorm_b, m_w_o, m_ln2_g, m_w_up, m_ffn_conv_w, m_ffn_conv_b, m_w_down, m_g_final, v_ln1_g, v_w_in, v_b_forget, v_g_out_fox, v_g_out_dil, v_conv_w, v_conv_b, v_cnorm_g, v_cnorm_b, v_w_o, v_ln2_g, v_w_up, v_ffn_conv_w, v_ffn_conv_b, v_w_down, v_g_final):
    P = dict(ln1_g=ln1_g, w_in=w_in, b_forget=b_forget, g_out_fox=g_out_fox, g_out_dil=g_out_dil, conv_w=conv_w, conv_b=conv_b,
             cnorm_g=cnorm_g, cnorm_b=cnorm_b, w_o=w_o, ln2_g=ln2_g, w_up=w_up, ffn_conv_w=ffn_conv_w, ffn_conv_b=ffn_conv_b,
             w_down=w_down, g_final=g_final)
    M = dict(ln1_g=m_ln1_g, w_in=m_w_in, b_forget=m_b_forget, g_out_fox=m_g_out_fox, g_out_dil=m_g_out_dil, conv_w=m_conv_w,
             conv_b=m_conv_b, cnorm_g=m_cnorm_g, cnorm_b=m_cnorm_b, w_o=m_w_o, ln2_g=m_ln2_g, w_up=m_w_up, ffn_conv_w=m_ffn_conv_w,
             ffn_conv_b=m_ffn_conv_b, w_down=m_w_down, g_final=m_g_final)
    V = dict(ln1_g=v_ln1_g, w_in=v_w_in, b_forget=v_b_forget, g_out_fox=v_g_out_fox, g_out_dil=v_g_out_dil, conv_w=v_conv_w,
             conv_b=v_conv_b, cnorm_g=v_cnorm_g, cnorm_b=v_cnorm_b, w_o=v_w_o, ln2_g=v_ln2_g, w_up=v_w_up, ffn_conv_w=v_ffn_conv_w,
             ffn_conv_b=v_ffn_conv_b, w_down=v_w_down, g_final=v_g_final)
    Bl, S, _ = x.shape
    T = Bl * S

    shard = lambda k, l: P[k][l].astype(BF16) if k in _MATMUL_W else P[k][l]
    early = [("w_in", 0)] + [(k, l) for k in ("conv_w", "ffn_conv_w") for l in range(DEPTH)]
    late = [(k, 0) for k in ("w_o", "w_up", "w_down")] + [(k, 1) for k in _MATMUL_W]
    gathered = {k: [None] * DEPTH for k in _SHARDED}
    plan_early = _Plan([[shard(k, l)] for k, l in early], True)
    for (k, l), zone in zip(early, _exchange(plan_early, "gather_early")):
        gathered[k][l] = zone
    plan_late = _Plan([[shard(k, l)] for k, l in late], True)
    late_handle, late_token = _exchange_start(plan_late, "gather_late_start", plan_late.zones_with_own())
    W = _full_weights(P, gathered)
    W["ln1_g"][0] = W["ln1_g"][0] + late_token[0, 0]

    def late_weights(W, after):
        zones = _exchange_wait(plan_late, "gather_late_wait", late_handle, after)
        W = dict(W)
        for (k, l), zone in zip(late, zones):
            W[k] = list(W[k])
            W[k][l] = _kernel_ready(k, zone)
        return W

    def owner_block(G, l, k):
        if k == "w_in":
            blk = _unpad_w_in(G[k][l]).reshape(NDEV, D // NDEV, N_IN)
        elif k == "w_o":
            blk = G[k][l].reshape(NDEV, D // NDEV, D)
        elif k == "w_up":
            blk = G[k][l]
        elif k == "w_down":
            blk = G[k][l].reshape(NDEV, DFF // NDEV, D)
        elif k == "conv_w":
            blk = jnp.transpose(G[k][l].reshape(CONV_K, NDEV, CC // NDEV), (1, 0, 2))
        else:
            blk = G[k][l].reshape(NDEV, 3, FB)
        return blk.astype(GRAD_DTYPE)

    flying = []

    def start_scatter(tag, G, l, keys):
        plan = _Plan([[owner_block(G, l, k)] for k in keys], False)
        handle, token = _exchange_start(plan, "scatter_%s_start" % tag, plan.zones_with_own())
        flying.append((tag, l, keys, plan, handle))
        return token

    def on_layer_grads(l, G):
        return start_scatter("l1", G, l, _MATMUL_W) if l == 1 else start_scatter("l0_rest", G, l, ("w_in", "w_o"))

    def on_ffn_grads(l, G):
        return start_scatter("l0_ffn", G, l, ("w_up", "w_down")) if l == 0 else None

    loss, dx, G = _local_step(x.reshape(T, D), loss_target.reshape(T, D), W, S, late_weights, on_layer_grads, on_ffn_grads)

    parts = {}
    for tag, l, keys, plan, handle in flying:
        for k, zone in zip(keys, _exchange_wait(plan, "scatter_%s_wait" % tag, handle, dx)):
            parts[(k, l)] = zone
    plan_last = _Plan([[owner_block(G, l, k) for l in range(DEPTH)] for k in ("conv_w", "ffn_conv_w")], False)
    last = _exchange(plan_last, "scatter_last")
    small_parts_sharded = {"conv_w": last[0], "ffn_conv_w": last[1]}

    small = [jnp.stack(G[k]).reshape(-1) for k in _SMALL[:-1]] + [G["g_final"].reshape(-1), loss[0, :1]]
    sizes = [int(s.shape[0]) for s in small]
    flat = jnp.concatenate(small)
    n_small = -(-flat.shape[0] // 1024) * 1024
    flat = jnp.concatenate([flat, jnp.zeros((n_small - flat.shape[0],), F32)]).reshape(n_small // 128, 128)
    small_parts = _exchange(_Plan([[flat]], True), "gather_small")[0]

    out_g, out_d, out_m, out_v = {}, {}, {}, {}
    sg = _sum_parts(small_parts.reshape(NDEV, n_small // 128, 128), "sum_small").reshape(-1)
    off = 0
    summed = {}
    for k, n in zip(_SMALL + ("loss",), sizes):
        summed[k] = sg[off:off + n]
        off += n
    for k in _ORDER:
        shp = P[k].shape
        C = shp[-1]
        if k in _MATMUL_W:
            R0 = shp[1]
            flat2 = lambda t: t.reshape(DEPTH * R0, C)
            res = None
            for l in reversed(range(DEPTH)):
                res = _adamw(parts[(k, l)].reshape(NDEV, R0, C), flat2(P[k]), flat2(M[k]), flat2(V[k]),
                             "adamw_%s_l%d" % (k, l), row0=l * R0, prev=res)
        else:
            pr = small_parts_sharded[k].reshape((NDEV, -1, C)) if k in _SHARDED else summed[k].reshape((1, -1, C))
            R = pr.shape[1]
            res = _adamw(pr, P[k].reshape(R, -1), M[k].reshape(R, -1), V[k].reshape(R, -1), "adamw_" + k)
        out_g[k], out_d[k], out_m[k], out_v[k] = (t.reshape(shp) for t in res)

    loss_out = summed["loss"].reshape(())
    grad_x = dx.reshape(Bl, S, D)
    return (loss_out, grad_x, *[out_g[k] for k in _ORDER], *[out_d[k] for k in _ORDER], *[out_m[k] for k in _ORDER],
            *[out_v[k] for k in _ORDER])
```

```python
import functools

import numpy as np
import jax
import jax.numpy as jnp
from jax import lax
from jax.experimental import pallas as pl
from jax.experimental.pallas import tpu as pltpu

F32 = jnp.float32
BF16 = jnp.bfloat16
GRAD_DTYPE = BF16
HIGHEST = lax.Precision.HIGHEST

D = 1024
DEPTH = 2
HD = 64
WA = 384
NPAIR = 3
CC = 256
CONV_K = 31
DFF = 2816
FB = 704
NJ = 4
NDEV = 8
EPS = 1e-6
NQKV = 2304
NREST = 768
NPAD = NQKV + NREST
O_FA, O_QB, N_IN = 1152, 1158, 2822
DILATION_PAIRS = ((128, 1), (512, 4), (2048, 16))
NEG = -1e30

ADAM_LR, ADAM_B1, ADAM_B2, ADAM_EPS, ADAM_WD, ADAM_STEP = 0.001, 0.9, 0.999, 1e-08, 0.01, 10

VMEM_LIMIT = 48 * 1024 * 1024
VMEM_LIMIT_BIG = 56 * 1024 * 1024


def _cp(*sem):
    return pltpu.CompilerParams(dimension_semantics=sem, vmem_limit_bytes=VMEM_LIMIT)


def _sds(shape, dtype):
    return jax.ShapeDtypeStruct(shape, dtype)


def _dot(a, b, prec=None):
    return jnp.dot(a, b, preferred_element_type=F32, precision=prec)


def _dot_nt(a, b, prec=None):
    return lax.dot_general(a, b, (((1,), (1,)), ((), ())), preferred_element_type=F32, precision=prec)


def _dot_tn(a, b):
    return lax.dot_general(a, b, (((0,), (0,)), ((), ())), preferred_element_type=F32)


def _sigmoid(z):
    return 0.5 * jnp.tanh(0.5 * z) + 0.5


def _rms_bwd(dh, x, g):
    r = lax.rsqrt(jnp.mean(x * x, axis=-1, keepdims=True) + EPS)
    xh = x * r
    dg = jnp.sum(dh * xh, axis=0, keepdims=True)
    dxh = dh * g
    dx = r * (dxh - xh * jnp.mean(dxh * xh, axis=-1, keepdims=True))
    return dx, dg


def _blank(shape, dtype):
    def body(o_ref):
        pass

    return pl.pallas_call(body, name="blank", out_shape=_sds(shape, dtype), out_specs=pl.BlockSpec(memory_space=pl.ANY))()


class _Plan:
    def __init__(self, groups, gather, slots=None, n_slots=None):
        self.groups, self.gather = groups, gather
        self.slots = slots or [list(range(len(g))) for g in groups]
        self.n_slots = n_slots or [len(g) for g in groups]
        self.flat = [a for g in groups for a in g]
        self.where = [(gi, s) for gi, g in enumerate(groups) for s in self.slots[gi]]
        self.zone_shapes = [_sds((NDEV, ns) + (g[0].shape if gather else g[0].shape[1:]), g[0].dtype)
                            for g, ns in zip(groups, self.n_slots)]

    def new_zones(self):
        return [_blank(z.shape, z.dtype) for z in self.zone_shapes]

    def me(self):
        x, y, c = lax.axis_index("x"), lax.axis_index("y"), lax.axis_index("c")
        return 4 * x + 2 * y + c

    def zones_with_own(self):
        me = self.me()
        zones = self.new_zones()
        for a, (gi, s) in enumerate(self.where):
            src = self.flat[a] if self.gather else lax.dynamic_index_in_dim(self.flat[a], me, 0, keepdims=False)
            zones[gi] = lax.dynamic_update_slice(zones[gi], src[None, None], (me, s) + (0,) * src.ndim)
        return zones

    def own_copies(self, ins, zones, sems):
        me = self.me()
        return [pltpu.make_async_copy(ins[a] if self.gather else ins[a].at[me], zones[gi].at[me, s], sems.at[a])
                for a, (gi, s) in enumerate(self.where)]

    def remote_copies(self, ins, zones, send_sems, recv_sems):
        x, y, c = lax.axis_index("x"), lax.axis_index("y"), lax.axis_index("c")
        me = 4 * x + 2 * y + c
        out = []
        for a, (gi, s) in enumerate(self.where):
            for k in range(1, NDEV):
                px, py, pc = x ^ ((k >> 2) & 1), y ^ ((k >> 1) & 1), c ^ (k & 1)
                out.append(pltpu.make_async_remote_copy(
                    src_ref=ins[a] if self.gather else ins[a].at[4 * px + 2 * py + pc], dst_ref=zones[gi].at[me, s],
                    send_sem=send_sems.at[a * (NDEV - 1) + k - 1], recv_sem=recv_sems.at[a * (NDEV - 1) + k - 1],
                    device_id=(px, py, pc), device_id_type=pl.DeviceIdType.MESH))
        return out


_ANY = pl.BlockSpec(memory_space=pl.ANY)
_HBM = pl.BlockSpec(memory_space=pltpu.HBM)
_SEM = pl.BlockSpec(memory_space=pltpu.SEMAPHORE)


def _exchange(plan, name, zones=None):
    n, nz = len(plan.flat), len(plan.groups)
    zones = zones or plan.new_zones()

    def body(*refs):
        ins, zin = refs[:n], refs[n:n + nz]
        send_sems, recv_sems, local_sems = refs[n + 2 * nz:]
        copies = plan.own_copies(ins, zin, local_sems) + plan.remote_copies(ins, zin, send_sems, recv_sems)
        for cp in copies:
            cp.start()
        for cp in copies:
            cp.wait()

    return pl.pallas_call(
        body, name=name, out_shape=plan.zone_shapes, in_specs=[_ANY] * (n + nz), out_specs=[_ANY] * nz,
        input_output_aliases={n + g: g for g in range(nz)},
        scratch_shapes=[pltpu.SemaphoreType.DMA((n * (NDEV - 1),)), pltpu.SemaphoreType.DMA((n * (NDEV - 1),)),
                        pltpu.SemaphoreType.DMA((n,))],
        compiler_params=pltpu.CompilerParams(has_side_effects=True),
    )(*plan.flat, *zones)


def _exchange_start(plan, name, zones):
    n, nz = len(plan.flat), len(plan.groups)
    hbm = lambda a: pltpu.HBM(a.shape, a.dtype)

    def body(*refs):
        ins, zin = refs[:n], refs[n:n + nz]
        send_sems, recv_sems = refs[n + nz], refs[n + nz + 1]
        token = refs[-1]
        for cp in plan.remote_copies(ins, zin, send_sems, recv_sems):
            cp.start()
        token[...] = jnp.zeros_like(token)

    outs = pl.pallas_call(
        body, name=name,
        out_shape=(pltpu.SemaphoreType.DMA((n * (NDEV - 1),)), pltpu.SemaphoreType.DMA((n * (NDEV - 1),)),
                   *[hbm(a) for a in plan.flat], *[hbm(z) for z in plan.zone_shapes], _sds((8, 128), F32)),
        in_specs=[_HBM] * (n + nz),
        out_specs=(_SEM, _SEM, *[_HBM] * (n + nz), pl.BlockSpec(memory_space=pltpu.VMEM)),
        input_output_aliases={i: 2 + i for i in range(n + nz)},
        compiler_params=pltpu.CompilerParams(has_side_effects=pltpu.SideEffectType.DATAFLOW_SIDE_EFFECTING),
    )(*[pltpu.with_memory_space_constraint(a, pltpu.HBM) for a in plan.flat],
      *[pltpu.with_memory_space_constraint(z, pltpu.HBM) for z in zones])
    return outs[:-1], outs[-1]


def _exchange_wait(plan, name, handle, after):
    n, nz = len(plan.flat), len(plan.groups)
    send_sems, recv_sems = handle[0], handle[1]
    thru = handle[2:]
    hbm = lambda a: pltpu.HBM(a.shape, a.dtype)

    def body(*refs):
        ins, zin = refs[:n], refs[n:n + nz]
        ssem, rsem = refs[n + nz], refs[n + nz + 1]
        for cp in plan.remote_copies(ins, zin, ssem, rsem):
            cp.wait_send()
            cp.wait_recv()

    outs = pl.pallas_call(
        body, name=name, out_shape=tuple(hbm(a) for a in thru),
        in_specs=[_HBM] * (n + nz) + [_SEM, _SEM, _ANY], out_specs=tuple([_HBM] * (n + nz)),
        input_output_aliases={i: i for i in range(n + nz)},
        compiler_params=pltpu.CompilerParams(has_side_effects=pltpu.SideEffectType.DATAFLOW_SIDE_EFFECTING),
    )(*thru, send_sems, recv_sems, after)
    return list(outs[n:])


def _sum_parts(parts, name):
    npart, R, C = parts.shape

    def body(p_ref, o_ref):
        s = p_ref[0]
        for k in range(1, npart):
            s = s + p_ref[k]
        o_ref[...] = s

    return pl.pallas_call(body, name=name, out_shape=_sds((R, C), F32))(parts)


def _rms_fwd(x, g):
    T = x.shape[0]
    tm = min(512, T)

    def body(x_ref, g_ref, o_ref):
        xv = x_ref[...]
        r = lax.rsqrt(jnp.mean(xv * xv, axis=-1, keepdims=True) + EPS)
        o_ref[...] = (xv * r * g_ref[...]).astype(BF16)

    return pl.pallas_call(
        body, name="rms_fwd", grid=(T // tm,), out_shape=_sds((T, D), BF16),
        in_specs=[pl.BlockSpec((tm, D), lambda i: (i, 0)), pl.BlockSpec((1, D), lambda i: (0, 0))],
        out_specs=pl.BlockSpec((tm, D), lambda i: (i, 0)), compiler_params=_cp("parallel"))(x, g)


def _mm_in(h, w):
    T = h.shape[0]
    tm = min(512, T)

    def body(h_ref, w_ref, q_ref, r_ref):
        hv = h_ref[...]
        q_ref[...] = _dot(hv, w_ref[:, :NQKV]).astype(BF16)
        r_ref[...] = _dot(hv, w_ref[:, NQKV:])

    return pl.pallas_call(
        body, name="mm_in", grid=(T // tm,), out_shape=[_sds((T, NQKV), BF16), _sds((T, NREST), F32)],
        in_specs=[pl.BlockSpec((tm, D), lambda i: (i, 0)), pl.BlockSpec((D, NPAD), lambda i: (0, 0))],
        out_specs=[pl.BlockSpec((tm, NQKV), lambda i: (i, 0)), pl.BlockSpec((tm, NREST), lambda i: (i, 0))],
        compiler_params=_cp("parallel"))(h, w)


AUG_Q1, AUG_K1 = 3, 0


def _aug_lane0(h):
    return HD * (1 - h % 2)


def _aug_tables():
    selq = np.zeros((NPAIR, 3 * 128, 128), np.float32)
    selk = np.zeros((NPAIR, 3 * 128, 128), np.float32)
    oneq = np.zeros((NPAIR, 1, 128), np.float32)
    onek = np.zeros((NPAIR, 1, 128), np.float32)
    for h in range(2 * NPAIR):
        p, l0 = h // 2, _aug_lane0(h)
        for piece in range(3):
            selq[p, 128 * piece + h, l0 + piece] = 1.0
            selk[p, 128 * piece + h, l0 + 3 + piece] = -1.0
            oneq[p, 0, l0 + AUG_Q1 + piece] = 1.0
            onek[p, 0, l0 + AUG_K1 + piece] = 1.0
    return [jnp.asarray(a, BF16) for a in (selq, selk, oneq, onek)]


def _gate_fwd(p_rest, bf, S, TB):
    T = p_rest.shape[0]
    B, nb = T // S, S // TB
    selq, selk, oneq, onek = _aug_tables()

    def body(z_ref, b_ref, sq_ref, sk_ref, oq_ref, ok_ref, aq_ref, ak_ref):
        tri = (lax.broadcasted_iota(jnp.int32, (TB, TB), 0) >= lax.broadcasted_iota(jnp.int32, (TB, TB), 1)).astype(F32)
        carry = jnp.zeros((1, 128), F32)
        for j in range(nb):
            rows = slice(j * TB, (j + 1) * TB)
            z = z_ref[rows, :] + b_ref[...]
            lf = jnp.minimum(z, 0.0) - jnp.log(1.0 + jnp.exp(-jnp.abs(z)))
            cb = _dot(tri, lf, HIGHEST) + carry
            carry = cb[TB - 1:TB, :]
            hi = cb.astype(BF16)
            mid = (cb - hi.astype(F32)).astype(BF16)
            lo = (cb - hi.astype(F32) - mid.astype(F32)).astype(BF16)
            pieces = jnp.concatenate([hi, mid, lo], axis=1)
            for p in range(NPAIR):
                aq_ref[p, rows, :] = (_dot(pieces, sq_ref[p]) + oq_ref[p].astype(F32)).astype(BF16)
                ak_ref[p, rows, :] = (_dot(pieces, sk_ref[p]) + ok_ref[p].astype(F32)).astype(BF16)

    full = lambda a: pl.BlockSpec(a.shape, lambda b: (0,) * a.ndim)
    return pl.pallas_call(
        body, name="gate_fwd", grid=(B,), out_shape=[_sds((NPAIR, T, 128), BF16), _sds((NPAIR, T, 128), BF16)],
        in_specs=[pl.BlockSpec((S, 128), lambda b: (b, 4)), pl.BlockSpec((1, 128), lambda b: (0, 0)),
                  full(selq), full(selk), full(oneq), full(onek)],
        out_specs=[pl.BlockSpec((NPAIR, S, 128), lambda b: (0, b, 0)), pl.BlockSpec((NPAIR, S, 128), lambda b: (0, b, 0))],
        compiler_params=_cp("parallel"))(p_rest, bf, selq, selk, oneq, onek)


def _bias_table(S, n_key, n_query, fox):
    unit = min(n_key, n_query)
    d_min = -(n_query // unit - 1)
    d = np.arange(d_min, S // unit)[:, None, None]
    delta = d * unit + np.arange(n_query)[None, None, :] - np.arange(n_key)[None, :, None]
    if fox:
        mult = (delta >= 0).astype(np.float64)
    else:
        mult = np.zeros(delta.shape)
        for w, dil in DILATION_PAIRS:
            mult += (delta >= 0) & (delta % dil == 0) & (delta <= w)
    with np.errstate(divide="ignore"):
        tab = np.where(mult > 0, np.log(np.maximum(mult, 1e-30)), NEG)
    tab = np.concatenate([tab, np.full((1, n_key, n_query), NEG)], axis=0)
    return jnp.asarray(tab, F32)


def _head_masks():
    lane = lax.broadcasted_iota(jnp.int32, (1, 128), 1)
    return [lane < HD, lane >= HD]


def _transpose_bf16(a):
    return a.astype(F32).T.astype(BF16)


def _col_reduce(x, op):
    while x.shape[0] > 8:
        half = x.shape[0] // 2
        x = op(x[:half], x[half:])
    return jnp.max(x, axis=0, keepdims=True) if op is jnp.maximum else jnp.sum(x, axis=0, keepdims=True)


def _attn_fwd(p_qkv, col0, tab, S, aq=None, ak=None, name="attn"):
    T = p_qkv.shape[0]
    n_tab, TB, TQ = tab.shape
    nb, nq, r = S // TB, S // TQ, TQ // TB
    B = T // S
    fox = aq is not None

    def body(*refs):
        if fox:
            q_ref, k_ref, v_ref, tab_ref, aq_ref, ak_ref, o_ref, l_ref, kat_ref, vta_scr, acc_scr, st_scr = refs
        else:
            q_ref, k_ref, v_ref, tab_ref, o_ref, l_ref, kat_ref, vta_scr, acc_scr, st_scr = refs
        i = pl.program_id(2)
        hms = _head_masks()
        top = lax.broadcasted_iota(jnp.int32, (128, 1), 0) < HD
        last_key_block = r * i + r - 1

        @pl.when(i == 0)
        def _():
            for jj in range(nb):
                rows = slice(jj * TB, (jj + 1) * TB)
                vt = _transpose_bf16(v_ref[rows, :])
                for hh in range(2):
                    vta_scr[hh, jj] = jnp.where(top == (hh == 0), vt, jnp.ones_like(vt))
                    spare_k = ak_ref[0, rows, :] if fox else jnp.zeros((TB, 128), BF16)
                    kat_ref[0, 0, hh, jj] = _transpose_bf16(jnp.where(hms[hh], k_ref[rows, :], spare_k))

        q2 = q_ref[...] * 0.125
        spare_q = aq_ref[0] if fox else jnp.zeros_like(q2)
        qa = [jnp.where(hm, q2, spare_q) for hm in hms]
        acc_scr[...] = jnp.zeros_like(acc_scr)

        def scores(j):
            jc = jnp.minimum(j, nb - 1)
            rows = pl.ds(pl.multiple_of(jc * TB, TB), TB)
            k2 = k_ref[rows, :]
            bias = tab_ref[jnp.where(j <= last_key_block, r * i - j + (r - 1), n_tab - 1)]
            return [_dot_nt(jnp.where(hms[hh], k2, ak_ref[0, rows, :]) if fox else k2, qa[hh]) + bias for hh in range(2)]

        def absorb(j, sts, stats):
            jc = jnp.minimum(j, nb - 1)
            alphas, pts = [], []
            for hh in range(2):
                m_old = stats[2 * hh]
                m_new = jnp.maximum(m_old, _col_reduce(sts[hh], jnp.maximum))
                pts.append(jnp.exp(sts[hh] - m_new).astype(BF16))
                alphas.append(jnp.exp(m_old - m_new))
                stats[2 * hh] = m_new
            pvs = [_dot(vta_scr[hh, jc], pts[hh]) for hh in range(2)]
            for hh in range(2):
                half = slice(HD * hh, HD * (hh + 1))
                ones_row = HD * (1 - hh)
                acc_scr[half, :] = alphas[hh] * acc_scr[half, :] + pvs[hh][half]
                stats[2 * hh + 1] = alphas[hh] * stats[2 * hh + 1] + pvs[hh][ones_row:ones_row + 1]

        def kv_pair(jj, carry):
            stats = list(carry)
            j0 = 2 * jj
            st0 = [st_scr[hh] for hh in range(2)]
            st1 = scores(j0 + 1)
            absorb(j0, st0, stats)
            nxt = scores(j0 + 2)
            for hh in range(2):
                st_scr[hh] = nxt[hh]
            absorb(j0 + 1, st1, stats)
            return tuple(stats)

        first = scores(0)
        for hh in range(2):
            st_scr[hh] = first[hh]
        row = lambda v: jnp.full((1, TQ), v, F32)
        m0, l0, m1, l1 = lax.fori_loop(0, (last_key_block + 2) // 2, kv_pair, (row(NEG), row(0.0), row(NEG), row(0.0)))
        o_ref[...] = (acc_scr[...] * jnp.where(top, 1.0 / l0, 1.0 / l1)).T
        for hh, lse in enumerate((m0 + jnp.log(l0), m1 + jnp.log(l1))):
            for t in range(r):
                l_ref[t, hh] = lse[:, t * TB:(t + 1) * TB]

    in_specs = [pl.BlockSpec((TQ, 128), lambda b, p, i: (b * nq + i, col0 + p)),
                pl.BlockSpec((S, 128), lambda b, p, i: (b, col0 + NPAIR + p)),
                pl.BlockSpec((S, 128), lambda b, p, i: (b, col0 + 2 * NPAIR + p)),
                pl.BlockSpec(tab.shape, lambda b, p, i: (0, 0, 0))]
    args = [p_qkv, p_qkv, p_qkv, tab]
    if fox:
        in_specs += [pl.BlockSpec((1, TQ, 128), lambda b, p, i: (p, b * nq + i, 0)),
                     pl.BlockSpec((1, S, 128), lambda b, p, i: (p, b, 0))]
        args += [aq, ak]
    return pl.pallas_call(
        body, name=name, grid=(B, NPAIR, nq),
        out_shape=[_sds((T, WA), F32), _sds((T // TB, 2 * NPAIR, 1, TB), F32), _sds((B, NPAIR, 2, nb, 128, TB), BF16)],
        in_specs=in_specs,
        out_specs=[pl.BlockSpec((TQ, 128), lambda b, p, i: (b * nq + i, p)),
                   pl.BlockSpec((r, 2, 1, TB), lambda b, p, i: (b * nq + i, p, 0, 0)),
                   pl.BlockSpec((1, 1, 2, nb, 128, TB), lambda b, p, i: (b, p, 0, 0, 0, 0))],
        scratch_shapes=[pltpu.VMEM((2, nb, 128, TB), BF16), pltpu.VMEM((128, TQ), F32), pltpu.VMEM((2, TB, TQ), F32)],
        compiler_params=_cp("parallel", "parallel", "arbitrary"))(*args)


def _phase_copies(src, phases, length):
    for r in range(1, 8):
        phases[r, 0:length - 8, :] = src[pl.ds(r, length - 8), :]


def _tap(src, phases, offset, rows):
    r = offset % 8
    return src[pl.ds(offset, rows), :] if r == 0 else phases[r, pl.ds(offset - r, rows), :]


def _conv_taps(src, phases, w_ref, first_row, rows):
    acc = w_ref[0:1, :] * _tap(src, phases, first_row, rows)
    for k in range(1, CONV_K):
        acc = acc + w_ref[k:k + 1, :] * _tap(src, phases, first_row + k, rows)
    return acc


def _mix_fwd(o_a, o_b, p_rest, g_a, g_b, cw, cb, ng, nbias, S):
    T = o_a.shape[0]
    tm = min(256, S)
    nps = S // tm
    HALO = 32

    def body(oa_ref, ob_ref, pt_ref, ph_ref, ga_ref, gb_ref, cw_ref, cb_ref, ng_ref, nb_ref, y_ref, cv_ref, buf, phases):
        i = pl.program_id(0)
        first = (i % nps) == 0
        for o_ref, g_ref, c0 in ((oa_ref, ga_ref, 0), (ob_ref, gb_ref, WA)):
            o = o_ref[...]
            r = lax.rsqrt(jnp.mean(o * o, axis=-1, keepdims=True) + EPS)
            y_ref[:, c0:c0 + WA] = (o * r * g_ref[...]).astype(BF16)
        ph = jnp.where(first, 0.0, ph_ref[...])
        buf[0:HALO, :] = ph[:, :CC] * _sigmoid(ph[:, CC:])
        pt = pt_ref[...]
        buf[HALO:HALO + tm, :] = pt[:, :CC] * _sigmoid(pt[:, CC:])
        _phase_copies(buf, phases, tm + HALO)
        cv = _conv_taps(buf, phases, cw_ref, HALO - CONV_K + 1, tm) + cb_ref[...]
        cv_ref[...] = cv
        xc = cv - jnp.mean(cv, axis=-1, keepdims=True)
        lo = xc * lax.rsqrt(jnp.mean(xc * xc, axis=-1, keepdims=True) + EPS) * ng_ref[...] + nb_ref[...]
        y_ref[:, 2 * WA:] = (lo * _sigmoid(lo)).astype(BF16)

    row = lambda w: pl.BlockSpec((1, w), lambda i: (0, 0))
    return pl.pallas_call(
        body, name="mix_fwd", grid=(T // tm,), out_shape=[_sds((T, D), BF16), _sds((T, CC), F32)],
        in_specs=[pl.BlockSpec((tm, WA), lambda i: (i, 0)), pl.BlockSpec((tm, WA), lambda i: (i, 0)),
                  pl.BlockSpec((tm, 2 * CC), lambda i: (i, 0)),
                  pl.BlockSpec((HALO, 2 * CC), lambda i: (jnp.maximum(i * (tm // HALO) - 1, 0), 0)),
                  row(WA), row(WA), pl.BlockSpec((32, CC), lambda i: (0, 0)), row(CC), row(CC), row(CC)],
        out_specs=[pl.BlockSpec((tm, D), lambda i: (i, 0)), pl.BlockSpec((tm, CC), lambda i: (i, 0))],
        scratch_shapes=[pltpu.VMEM((tm + HALO, CC), F32), pltpu.VMEM((8, tm + HALO, CC), F32)],
        compiler_params=_cp("parallel"))(o_a, o_b, p_rest, p_rest, g_a, g_b, cw, cb, ng, nbias)


def _mm_res(a, w, resid, g, name):
    nk, T, kb = a.shape
    tm = min(512, T)

    def body(a_ref, w_ref, r_ref, g_ref, x_ref, h_ref):
        xv = r_ref[...]
        for k in range(nk):
            xv = xv + _dot(a_ref[k], w_ref[k])
        x_ref[...] = xv
        r = lax.rsqrt(jnp.mean(xv * xv, axis=-1, keepdims=True) + EPS)
        h_ref[...] = (xv * r * g_ref[...]).astype(BF16)

    return pl.pallas_call(
        body, name=name, grid=(T // tm,), out_shape=[_sds((T, D), F32), _sds((T, D), BF16)],
        in_specs=[pl.BlockSpec((nk, tm, kb), lambda i: (0, i, 0)), pl.BlockSpec((nk, kb, D), lambda i: (0, 0, 0)),
                  pl.BlockSpec((tm, D), lambda i: (i, 0)), pl.BlockSpec((1, D), lambda i: (0, 0))],
        out_specs=[pl.BlockSpec((tm, D), lambda i: (i, 0)), pl.BlockSpec((tm, D), lambda i: (i, 0))],
        compiler_params=_cp("parallel"))(a, w, resid, g)


def _up_fwd(h2, w_up, fcw, fcb, S):
    T = h2.shape[0]
    tm = min(512, S)
    nps = S // tm

    def body(h_ref, hh_ref, w_ref, cw_ref, cb_ref, u_ref, c_ref, hid_ref, buf, hbuf):
        i = pl.program_id(1)
        first = (i % nps) == 0
        hbuf[0:tm, :] = h_ref[...]
        hbuf[tm:tm + 8, :] = jnp.where(first, jnp.zeros_like(hh_ref[...]), hh_ref[...])
        c = []
        for half in range(2):
            ua = _dot(hbuf[...], w_ref[half, 0])
            u = ua[0:tm]
            u_ref[half, 0] = u
            buf[half, 0:8, :] = ua[tm:tm + 8]
            buf[half, 8:8 + tm, :] = u
            cw = cw_ref[half, 0]
            c.append(cb_ref[half, 0] + cw[0:1] * buf[half, pl.ds(6, tm), :] + cw[1:2] * buf[half, pl.ds(7, tm), :] + cw[2:3] * u)
        for half in range(2):
            c_ref[half, 0] = c[half]
        hid_ref[0] = (c[0] * _sigmoid(c[0]) * c[1]).astype(BF16)

    return pl.pallas_call(
        body, name="up_fwd", grid=(NJ, T // tm),
        out_shape=[_sds((2, NJ, T, FB), F32), _sds((2, NJ, T, FB), F32), _sds((NJ, T, FB), BF16)],
        in_specs=[pl.BlockSpec((tm, D), lambda j, i: (i, 0)),
                  pl.BlockSpec((8, D), lambda j, i: (jnp.maximum(i * (tm // 8) - 1, 0), 0)),
                  pl.BlockSpec((2, 1, D, FB), lambda j, i: (0, j, 0, 0)),
                  pl.BlockSpec((2, 1, 3, FB), lambda j, i: (0, j, 0, 0)),
                  pl.BlockSpec((2, 1, 1, FB), lambda j, i: (0, j, 0, 0))],
        out_specs=[pl.BlockSpec((2, 1, tm, FB), lambda j, i: (0, j, i, 0)), pl.BlockSpec((2, 1, tm, FB), lambda j, i: (0, j, i, 0)),
                   pl.BlockSpec((1, tm, FB), lambda j, i: (j, i, 0))],
        scratch_shapes=[pltpu.VMEM((2, tm + 8, FB), F32), pltpu.VMEM((tm + 8, D), BF16)],
        compiler_params=_cp("parallel", "parallel"))(h2, h2, w_up, fcw, fcb)


def _final_loss(x, tgt, g):
    T = x.shape[0]
    tm = min(512, T)

    def body(x_ref, t_ref, g_ref, dx_ref, loss_ref, dg_ref):
        @pl.when(pl.program_id(0) == 0)
        def _():
            loss_ref[...] = jnp.zeros_like(loss_ref)
            dg_ref[...] = jnp.zeros_like(dg_ref)

        xv, gv = x_ref[...], g_ref[...]
        r = lax.rsqrt(jnp.mean(xv * xv, axis=-1, keepdims=True) + EPS)
        e = xv * r * gv - t_ref[...]
        loss_ref[...] += 0.5 * jnp.sum(jnp.mean(e * e, axis=-1, keepdims=True), axis=0, keepdims=True)
        dx, dg = _rms_bwd(e * (1.0 / D), xv, gv)
        dx_ref[...] = dx
        dg_ref[...] += dg

    return pl.pallas_call(
        body, name="final_loss", grid=(T // tm,), out_shape=[_sds((T, D), F32), _sds((1, 128), F32), _sds((1, D), F32)],
        in_specs=[pl.BlockSpec((tm, D), lambda i: (i, 0)), pl.BlockSpec((tm, D), lambda i: (i, 0)), pl.BlockSpec((1, D), lambda i: (0, 0))],
        out_specs=[pl.BlockSpec((tm, D), lambda i: (i, 0)), pl.BlockSpec((1, 128), lambda i: (0, 0)), pl.BlockSpec((1, D), lambda i: (0, 0))],
        compiler_params=_cp("arbitrary"))(x, tgt, g)


def _down_bwd(dx2, w_down, u, c, fcw, S):
    T = dx2.shape[0]
    tm = min(512, S)
    nps = S // tm
    nblk8 = T // 8

    def body(dx_ref, dxn_ref, wd_ref, ut_ref, ct_ref, cn_ref, cw_ref, du_ref, dcw_ref, dxb, dcbuf, dsh):
        i = pl.program_id(1)
        last = (i % nps) == nps - 1

        @pl.when(i == 0)
        def _():
            dcw_ref[...] = jnp.zeros_like(dcw_ref)

        dxb[0:tm, :] = dx_ref[...].astype(BF16)
        dxb[tm:tm + 8, :] = jnp.where(last, 0.0, dxn_ref[...]).astype(BF16)
        dh = _dot_nt(dxb[...], wd_ref[0])
        ce = [jnp.concatenate([ct_ref[half, 0], cn_ref[half, 0]], axis=0) for half in range(2)]
        sg = _sigmoid(ce[0])
        dc = [dh * ce[1] * (sg * (1.0 + ce[0] * (1.0 - sg))), dh * (ce[0] * sg)]
        for half in range(2):
            cw = cw_ref[half, 0]
            dcbuf[...] = dc[half]
            for k in range(2):
                dsh[k] = dcbuf[pl.ds(k + 1, tm), :]
            ahead = [dc[half][0:tm], dsh[0], dsh[1]]
            du_ref[half, 0] = (cw[2:3] * ahead[0] + cw[1:2] * ahead[1] + cw[0:1] * ahead[2]).astype(BF16)
            uv = ut_ref[half, 0]
            for k in range(3):
                dcw_ref[half, 0, k:k + 1, :] += jnp.sum(ahead[2 - k] * uv, axis=0, keepdims=True)
            dcw_ref[half, 0, 3:4, :] += jnp.sum(ahead[0], axis=0, keepdims=True)

    ublk = lambda rows, imap: pl.BlockSpec((2, 1, rows, FB), imap)
    return pl.pallas_call(
        body, name="down_bwd", grid=(NJ, T // tm), out_shape=[_sds((2, NJ, T, FB), BF16), _sds((2, NJ, 8, FB), F32)],
        in_specs=[pl.BlockSpec((tm, D), lambda j, i: (i, 0)),
                  pl.BlockSpec((8, D), lambda j, i: (jnp.minimum((i + 1) * (tm // 8), nblk8 - 1), 0)),
                  pl.BlockSpec((1, FB, D), lambda j, i: (j, 0, 0)),
                  ublk(tm, lambda j, i: (0, j, i, 0)),
                  ublk(tm, lambda j, i: (0, j, i, 0)),
                  ublk(8, lambda j, i: (0, j, jnp.minimum((i + 1) * (tm // 8), nblk8 - 1), 0)),
                  pl.BlockSpec((2, 1, 3, FB), lambda j, i: (0, j, 0, 0))],
        out_specs=[ublk(tm, lambda j, i: (0, j, i, 0)), pl.BlockSpec((2, 1, 8, FB), lambda j, i: (0, j, 0, 0))],
        scratch_shapes=[pltpu.VMEM((tm + 8, D), BF16), pltpu.VMEM((tm + 8, FB), F32), pltpu.VMEM((2, tm, FB), F32)],
        compiler_params=_cp("parallel", "arbitrary"))(dx2, dx2, w_down, u, c, c, fcw)


def _wgrad(a, b, tn, name, ga=1, gb=1):
    na, T, ka = a.shape
    nb, _, kb = b.shape
    tk = min(1024, T)
    nt = T // tk

    def body(a_ref, b_ref, o_ref, acc):
        t = pl.program_id(3)

        @pl.when(t == 0)
        def _():
            acc[...] = jnp.zeros_like(acc)

        bs = [b_ref[ib].astype(BF16) for ib in range(gb)]
        for ia in range(ga):
            av = a_ref[ia]
            for ib in range(gb):
                acc[ia, ib] += _dot_tn(av, bs[ib])

        @pl.when(t == nt - 1)
        def _():
            o_ref[...] = acc[...].astype(GRAD_DTYPE)

    return pl.pallas_call(
        body, name=name, grid=(na // ga, nb // gb, kb // tn, nt), out_shape=_sds((na, nb, ka, kb), GRAD_DTYPE),
        in_specs=[pl.BlockSpec((ga, tk, ka), lambda ia, ib, n, t: (ia, t, 0)), pl.BlockSpec((gb, tk, tn), lambda ia, ib, n, t: (ib, t, n))],
        out_specs=pl.BlockSpec((ga, gb, ka, tn), lambda ia, ib, n, t: (ia, ib, 0, n)),
        scratch_shapes=[pltpu.VMEM((ga, gb, ka, tn), F32)],
        compiler_params=_cp("parallel", "parallel", "parallel", "arbitrary"))(a, b)


def _dx_rms(dy, w, x, g, dres, blocked, name):
    T = x.shape[0]
    tm = min(256 if blocked else 512, T)

    def body(dy_ref, w_ref, x_ref, g_ref, r_ref, dx_ref, dg_ref):
        @pl.when(pl.program_id(0) == 0)
        def _():
            dg_ref[...] = jnp.zeros_like(dg_ref)

        if blocked:
            dh = _dot_nt(dy_ref[0, 0], w_ref[0, 0])
            for half, k in [(h, k) for k in range(NJ) for h in range(2)][1:]:
                dh = dh + _dot_nt(dy_ref[half, k], w_ref[half, k])
        else:
            dh = _dot_nt(dy_ref[...], w_ref[...])
        dx, dg = _rms_bwd(dh, x_ref[...], g_ref[...])
        dx_ref[...] = r_ref[...] + dx
        dg_ref[...] += dg

    if blocked:
        dy_spec = pl.BlockSpec((2, NJ, tm, FB), lambda i: (0, 0, i, 0))
        w_spec = pl.BlockSpec((2, NJ, D, FB), lambda i: (0, 0, 0, 0))
    else:
        dy_spec = pl.BlockSpec((tm, dy.shape[1]), lambda i: (i, 0))
        w_spec = pl.BlockSpec(w.shape, lambda i: (0, 0))
    tile = pl.BlockSpec((tm, D), lambda i: (i, 0))
    return pl.pallas_call(
        body, name=name, grid=(T // tm,), out_shape=[_sds((T, D), F32), _sds((1, D), F32)],
        in_specs=[dy_spec, w_spec, tile, pl.BlockSpec((1, D), lambda i: (0, 0)), tile],
        out_specs=[tile, pl.BlockSpec((1, D), lambda i: (0, 0))],
        compiler_params=pltpu.CompilerParams(dimension_semantics=("arbitrary",), vmem_limit_bytes=VMEM_LIMIT_BIG))(dy, w, x, g, dres)


def _mm_nt(a, w, name):
    T = a.shape[0]
    tm = min(512, T)

    def body(a_ref, w_ref, o_ref):
        o_ref[...] = _dot_nt(a_ref[...].astype(BF16), w_ref[...])

    return pl.pallas_call(
        body, name=name, grid=(T // tm,), out_shape=_sds((T, D), F32),
        in_specs=[pl.BlockSpec((tm, D), lambda i: (i, 0)), pl.BlockSpec((D, D), lambda i: (0, 0))],
        out_specs=pl.BlockSpec((tm, D), lambda i: (i, 0)), compiler_params=_cp("parallel"))(a, w)


def _mix_bwd(dy, o_a, o_b, p_rest, cv, g_a, g_b, cw, ng, nbias, S):
    T = dy.shape[0]
    tm = min(256, S)
    nps = S // tm
    HALO = 32
    nblkh = T // HALO
    RE = tm + HALO

    def body(dy_ref, dyn_ref, oa_ref, ob_ref, pt_ref, cvt_ref, cvn_ref, ga_ref, gb_ref, cw_ref, ng_ref, nb_ref,
             doa_ref, dda_ref, dob_ref, ddb_ref, dg_ref, dgn_ref, dcw_ref, dcn_ref, dvbuf, dph):
        i = pl.program_id(0)
        last = (i % nps) == nps - 1

        @pl.when(i == 0)
        def _():
            dgn_ref[...] = jnp.zeros_like(dgn_ref)
            dcw_ref[...] = jnp.zeros_like(dcw_ref)
            dcn_ref[...] = jnp.zeros_like(dcn_ref)

        head_of = (lax.broadcasted_iota(jnp.int32, (8, WA), 1) // HD == lax.broadcasted_iota(jnp.int32, (8, WA), 0)).astype(F32)
        for n, (o_ref, g_ref, do_ref, dd_ref) in enumerate(((oa_ref, ga_ref, doa_ref, dda_ref), (ob_ref, gb_ref, dob_ref, ddb_ref))):
            o = o_ref[...]
            do, dg = _rms_bwd(dy_ref[:, n * WA:(n + 1) * WA], o, g_ref[...])
            dob = do.astype(BF16)
            do_ref[...] = dob
            ddt = _dot_nt(head_of, dob.astype(F32) * o, HIGHEST)
            for hd in range(8):
                dd_ref[0, hd] = ddt[hd:hd + 1, :]
            dgn_ref[n:n + 1, :] += dg

        pt = pt_ref[...]
        gv, sgg = pt[:, :CC], _sigmoid(pt[:, CC:])
        glu = gv * sgg
        cv = jnp.concatenate([cvt_ref[...], cvn_ref[...]], axis=0)
        xc = cv - jnp.mean(cv, axis=-1, keepdims=True)
        r = lax.rsqrt(jnp.mean(xc * xc, axis=-1, keepdims=True) + EPS)
        xh = xc * r
        lo = xh * ng_ref[...] + nb_ref[...]
        sg = _sigmoid(lo)
        dyc = jnp.concatenate([dy_ref[:, 2 * WA:], jnp.where(last, 0.0, dyn_ref[...])], axis=0)
        dlo = dyc * (sg * (1.0 + lo * (1.0 - sg)))
        dcn_ref[0:1, :] += jnp.sum(dlo[0:tm] * xh[0:tm], axis=0, keepdims=True)
        dcn_ref[1:2, :] += jnp.sum(dlo[0:tm], axis=0, keepdims=True)
        dxh = dlo * ng_ref[...]
        dcv = r * (dxh - jnp.mean(dxh, axis=-1, keepdims=True) - xh * jnp.mean(dxh * xh, axis=-1, keepdims=True))
        dvbuf[...] = dcv
        _phase_copies(dvbuf, dph, RE)
        dct = dcv[0:tm]
        dcw_ref[CONV_K:CONV_K + 1, :] += jnp.sum(dct, axis=0, keepdims=True)
        dglu = jnp.zeros((tm, CC), F32)
        for k in range(CONV_K):
            ahead = _tap(dvbuf, dph, CONV_K - 1 - k, tm)
            dcw_ref[k:k + 1, :] += jnp.sum(ahead * glu, axis=0, keepdims=True)
            dglu = dglu + cw_ref[k:k + 1, :] * ahead
        dg_ref[:, :CC] = (dglu * sgg).astype(BF16)
        dg_ref[:, CC:] = (dglu * gv * sgg * (1.0 - sgg)).astype(BF16)

    row = lambda w: pl.BlockSpec((1, w), lambda i: (0, 0))
    tile = lambda w: pl.BlockSpec((tm, w), lambda i: (i, 0))
    nxt = lambda i: jnp.minimum((i + 1) * (tm // HALO), nblkh - 1)
    const = lambda r, w: pl.BlockSpec((r, w), lambda i: (0, 0))
    ddrow = pl.BlockSpec((1, 8, 1, tm), lambda i: (i, 0, 0, 0))
    return pl.pallas_call(
        body, name="mix_bwd", grid=(T // tm,),
        out_shape=[_sds((T, WA), BF16), _sds((T // tm, 8, 1, tm), F32), _sds((T, WA), BF16), _sds((T // tm, 8, 1, tm), F32),
                   _sds((T, 2 * CC), BF16), _sds((8, WA), F32), _sds((32, CC), F32), _sds((8, CC), F32)],
        in_specs=[tile(D), pl.BlockSpec((HALO, CC), lambda i: (nxt(i), 3)), tile(WA), tile(WA), tile(2 * CC),
                  tile(CC), pl.BlockSpec((HALO, CC), lambda i: (nxt(i), 0)),
                  row(WA), row(WA), const(32, CC), row(CC), row(CC)],
        out_specs=[tile(WA), ddrow, tile(WA), ddrow, tile(2 * CC), const(8, WA), const(32, CC), const(8, CC)],
        scratch_shapes=[pltpu.VMEM((RE, CC), F32), pltpu.VMEM((8, RE, CC), F32)],
        compiler_params=_cp("arbitrary"))(dy, dy, o_a, o_b, p_rest, cv, cv, g_a, g_b, cw, ng, nbias)


def _attn_bwd(p_qkv, col0, tab, do, lse, dd, kat, S, aq=None, ak=None, name="attn_bwd"):
    T = p_qkv.shape[0]
    n_tab, TK, TB = tab.shape
    nb, nkb, r = S // TB, S // TK, TK // TB
    B = T // S
    fox = aq is not None

    def body(*refs):
        if fox:
            (q_ref, k_ref, v_ref, tab_ref, do_ref, l_ref, dd_ref, kat_ref, aq_ref, ak_ref,
             dq_ref, dk_ref, dv_ref, dqa_ref, dka_ref, dqt_scr, dk_scr, dv_scr, sd_scr) = refs
        else:
            (q_ref, k_ref, v_ref, tab_ref, do_ref, l_ref, dd_ref, kat_ref,
             dq_ref, dk_ref, dv_ref, dqt_scr, dk_scr, dv_scr, sd_scr) = refs
        j = pl.program_id(2)
        hms = _head_masks()
        first_q_block = r * j

        @pl.when(j == 0)
        def _():
            dqt_scr[...] = jnp.zeros_like(dqt_scr)

        dk_scr[...] = jnp.zeros_like(dk_scr)
        dv_scr[...] = jnp.zeros_like(dv_scr)
        k2, v2 = k_ref[...], v_ref[...]
        ka = [jnp.where(hm, k2, ak_ref[0] if fox else jnp.zeros_like(k2)) for hm in hms]
        kat = [jnp.concatenate([kat_ref[0, 0, hh, t] for t in range(r)], axis=1) if r > 1 else kat_ref[0, 0, hh, 0]
               for hh in range(2)]

        def operands(i):
            ic = jnp.minimum(i, nb - 1)
            rows = pl.ds(pl.multiple_of(ic * TB, TB), TB)
            q2 = q_ref[rows, :] * 0.125
            do2 = do_ref[rows, :]
            qas = [jnp.where(hms[hh], q2, aq_ref[0, rows, :] if fox else jnp.zeros_like(q2)) for hh in range(2)]
            dohs = [jnp.where(hms[hh], do2, jnp.zeros_like(do2)) for hh in range(2)]
            return ic, qas, dohs

        def scores(i):
            ic, qas, dohs = operands(i)
            bias = tab_ref[jnp.where(i < nb, i - first_q_block, n_tab - 1)]
            return [(_dot_nt(ka[hh], qas[hh]) + bias, _dot_nt(v2, dohs[hh])) for hh in range(2)]

        def absorb(i, sd):
            ic, qas, dohs = operands(i)
            for hh in range(2):
                st, dpt = sd[hh]
                pt = jnp.exp(st - l_ref[ic, hh])
                dsb = (pt * (dpt - dd_ref[ic, hh])).astype(BF16)
                dv_scr[...] += _dot(pt.astype(BF16), dohs[hh])
                dk_scr[hh] += _dot(dsb, qas[hh])
                dqt_scr[hh, ic] += _dot(kat[hh], dsb)

        def q_pair(ii, carry):
            i0 = first_q_block + 2 * ii
            sd0 = [(sd_scr[hh, 0], sd_scr[hh, 1]) for hh in range(2)]
            sd1 = scores(i0 + 1)
            absorb(i0, sd0)
            nxt = scores(i0 + 2)
            for hh in range(2):
                sd_scr[hh, 0], sd_scr[hh, 1] = nxt[hh]
            absorb(i0 + 1, sd1)
            return carry

        first = scores(first_q_block)
        for hh in range(2):
            sd_scr[hh, 0], sd_scr[hh, 1] = first[hh]
        lax.fori_loop(0, (nb - first_q_block + 1) // 2, q_pair, 0)
        dk_ref[...] = jnp.where(hms[0], dk_scr[0], dk_scr[1]).astype(BF16)
        dv_ref[...] = dv_scr[...].astype(BF16)
        if fox:
            dka_ref[...] = jnp.where(hms[0], dk_scr[1], dk_scr[0])

        @pl.when(j == nkb - 1)
        def _():
            for ii in range(nb):
                t0, t1 = dqt_scr[0, ii].T, dqt_scr[1, ii].T
                dq_ref[ii * TB:(ii + 1) * TB, :] = jnp.where(hms[0], t0, t1) * 0.125
                if fox:
                    dqa_ref[ii * TB:(ii + 1) * TB, :] = jnp.where(hms[0], t1, t0)

    seq = lambda c: pl.BlockSpec((S, 128), lambda b, p, j: (b, c + p))
    kvb = lambda c: pl.BlockSpec((TK, 128), lambda b, p, j: (b * nkb + j, c + p))
    stat = pl.BlockSpec((nb, 2, 1, TB), lambda b, p, j: (b, p, 0, 0))
    in_specs = [seq(col0), kvb(col0 + NPAIR), kvb(col0 + 2 * NPAIR), pl.BlockSpec(tab.shape, lambda b, p, j: (0, 0, 0)),
                seq(0), stat, stat, pl.BlockSpec((1, 1, 2, r, 128, TB), lambda b, p, j: (b, p, 0, j, 0, 0))]
    args = [p_qkv, p_qkv, p_qkv, tab, do, lse, dd, kat]
    out_shape = [_sds((T, WA), F32), _sds((T, WA), BF16), _sds((T, WA), BF16)]
    out_specs = [seq(0), kvb(0), kvb(0)]
    if fox:
        in_specs += [pl.BlockSpec((1, S, 128), lambda b, p, j: (p, b, 0)), pl.BlockSpec((1, TK, 128), lambda b, p, j: (p, b * nkb + j, 0))]
        args += [aq, ak]
        out_shape += [_sds((T, WA), F32), _sds((T, WA), F32)]
        out_specs += [seq(0), kvb(0)]
    return pl.pallas_call(
        body, name=name, grid=(B, NPAIR, nkb), out_shape=out_shape, in_specs=in_specs, out_specs=out_specs,
        scratch_shapes=[pltpu.VMEM((2, nb, 128, TB), F32), pltpu.VMEM((2, TK, 128), F32), pltpu.VMEM((TK, 128), F32),
                        pltpu.VMEM((2, 2, TK, TB), F32)],
        compiler_params=_cp("parallel", "parallel", "arbitrary"))(*args)


def _gate_bwd(dqa, dka, p_rest, bf, S, TB):
    T = p_rest.shape[0]
    B, nb = T // S, S // TB

    def body(dqa_ref, dka_ref, z_ref, b_ref, dz_ref, db_ref):
        @pl.when(pl.program_id(0) == 0)
        def _():
            db_ref[...] = jnp.zeros_like(db_ref)

        later = (lax.broadcasted_iota(jnp.int32, (TB, TB), 1) >= lax.broadcasted_iota(jnp.int32, (TB, TB), 0)).astype(F32)
        lane = lax.broadcasted_iota(jnp.int32, (1, 128), 1)
        src, head = lax.broadcasted_iota(jnp.int32, (WA, 128), 0), lax.broadcasted_iota(jnp.int32, (WA, 128), 1)
        spare0 = 128 * (head // 2) + HD * (1 - head % 2)
        pick_q = (src == spare0 + AUG_K1).astype(F32)
        pick_k = (src == spare0 + AUG_Q1).astype(F32)
        carry = jnp.zeros((1, 128), F32)
        dbs = jnp.zeros((1, 128), F32)
        for j in reversed(range(nb)):
            rows = slice(j * TB, (j + 1) * TB)
            dc = _dot(dqa_ref[rows, :], pick_q, HIGHEST) - _dot(dka_ref[rows, :], pick_k, HIGHEST)
            part = _dot(later, dc, HIGHEST)
            tot = part + carry
            carry = carry + part[0:1, :]
            z = z_ref[j * TB:(j + 1) * TB, :] + b_ref[...]
            dz = jnp.where(lane < 6, tot * _sigmoid(-z), 0.0)
            dz_ref[j * TB:(j + 1) * TB, :] = dz.astype(BF16)
            dbs = dbs + jnp.sum(dz, axis=0, keepdims=True)
        db_ref[...] += dbs

    return pl.pallas_call(
        body, name="gate_bwd", grid=(B,), out_shape=[_sds((T, 128), BF16), _sds((1, 128), F32)],
        in_specs=[pl.BlockSpec((S, WA), lambda b: (b, 0)), pl.BlockSpec((S, WA), lambda b: (b, 0)),
                  pl.BlockSpec((S, 128), lambda b: (b, 4)), pl.BlockSpec((1, 128), lambda b: (0, 0))],
        out_specs=[pl.BlockSpec((S, 128), lambda b: (b, 0)), pl.BlockSpec((1, 128), lambda b: (0, 0))],
        compiler_params=_cp("arbitrary"))(dqa, dka, p_rest, bf)


def _adamw(parts, w, m, v, name, row0=0, prev=None):
    npart, Rp, C = parts.shape
    R = w.shape[0]
    tr = Rp
    for cand in (256, 128, 64, 32, 16, 8):
        if Rp % cand == 0 and row0 % cand == 0 and cand * C * 4 <= (1 << 20):
            tr = cand
            break
    if Rp == R and R * C * 4 <= (1 << 20):
        tr = R
    assert Rp % tr == 0 and row0 % tr == 0, (name, Rp, row0, tr)
    b0 = row0 // tr

    def body(p_ref, w_ref, m_ref, v_ref, *rest):
        g_ref, d_ref, mo_ref, vo_ref = rest[-4:]
        g = p_ref[0].astype(F32)
        for k in range(1, npart):
            g = g + p_ref[k].astype(F32)
        mn = ADAM_B1 * m_ref[...] + (1.0 - ADAM_B1) * g
        vn = ADAM_B2 * v_ref[...] + (1.0 - ADAM_B2) * (g * g)
        m_hat = mn / (1.0 - ADAM_B1 ** ADAM_STEP)
        v_hat = vn / (1.0 - ADAM_B2 ** ADAM_STEP)
        g_ref[...] = g
        d_ref[...] = -ADAM_LR * (m_hat / (jnp.sqrt(v_hat) + ADAM_EPS) + ADAM_WD * w_ref[...])
        mo_ref[...] = mn
        vo_ref[...] = vn

    blk = pl.BlockSpec((tr, C), lambda i: (b0 + i, 0))
    prev = list(prev) if prev is not None else []
    return pl.pallas_call(
        body, name=name, grid=(Rp // tr,), out_shape=[_sds((R, C), F32)] * 4,
        in_specs=[pl.BlockSpec((npart, tr, C), lambda i: (0, i, 0)), blk, blk, blk] + [_ANY] * len(prev), out_specs=[blk] * 4,
        input_output_aliases={4 + i: i for i in range(len(prev))},
        compiler_params=_cp("parallel"))(parts, w, m, v, *prev)


def _pad_w_in(w):
    z = jnp.zeros(w.shape[:-1] + (NPAD - N_IN,), w.dtype)
    return jnp.concatenate([w[..., :O_FA], w[..., O_QB:], w[..., O_FA:O_QB], z], axis=-1)


def _unpad_w_in(w):
    n_mid = N_IN - O_QB
    return jnp.concatenate([w[..., :O_FA], w[..., O_FA + n_mid:O_FA + n_mid + 6], w[..., O_FA:O_FA + n_mid]], axis=-1)


def _local_step(x, tgt, W, S, late_weights=None, on_layer_grads=None, on_ffn_grads=None):
    T = x.shape[0]
    TB = min(256, S)
    TW = min(2 * TB, S)
    tab_fox, tab_dil = _bias_table(S, TW, TB, True), _bias_table(S, TW, TB, False)
    tabq_fox, tabq_dil = _bias_table(S, TB, TW, True), _bias_table(S, TB, TW, False)
    saved = []
    h = _rms_fwd(x, W["ln1_g"][0])
    for l in range(DEPTH):
        p_qkv, p_rest = _mm_in(h, W["w_in"][l])
        aq, ak = _gate_fwd(p_rest, W["b_forget"][l], S, TB)
        o_a, lse_a, kat_a = _attn_fwd(p_qkv, 0, tabq_fox, S, aq, ak, name="fox_fwd")
        o_b, lse_b, kat_b = _attn_fwd(p_qkv, 3 * NPAIR, tabq_dil, S, name="dil_fwd")
        y, cv = _mix_fwd(o_a, o_b, p_rest, W["g_out_fox"][l], W["g_out_dil"][l], W["conv_w"][l], W["conv_b"][l],
                     W["cnorm_g"][l], W["cnorm_b"][l], S)
        if l == 0 and late_weights is not None:
            W = late_weights(W, y)
        x1, h2 = _mm_res(y[None], W["w_o"][l][None], x, W["ln2_g"][l], name="mm_o")
        u, c, hid = _up_fwd(h2, W["w_up"][l], W["ffn_conv_w"][l], W["ffn_conv_b"][l], S)
        g_next = W["ln1_g"][l + 1] if l + 1 < DEPTH else W["g_final"]
        x2, h_next = _mm_res(hid, W["w_down"][l], x1, g_next, name="mm_down")
        saved.append(dict(x=x, h=h, p_qkv=p_qkv, p_rest=p_rest, aq=aq, ak=ak, o_a=o_a, lse_a=lse_a, o_b=o_b,
                          lse_b=lse_b, kat_a=kat_a, kat_b=kat_b, y=y, cv=cv, x1=x1, h2=h2, u=u, c=c, hid=hid))
        x, h = x2, h_next
    dx, loss, dg_final = _final_loss(x, tgt, W["g_final"])
    G = {k: [None] * DEPTH for k in ("ln1_g", "w_in", "b_forget", "g_out_fox", "g_out_dil", "conv_w", "conv_b", "cnorm_g",
                                     "cnorm_b", "w_o", "ln2_g", "w_up", "ffn_conv_w", "ffn_conv_b", "w_down")}
    G["g_final"] = dg_final
    for l in reversed(range(DEPTH)):
        sv = saved[l]
        du, dcw = _down_bwd(dx, W["w_down"][l], sv["u"], sv["c"], W["ffn_conv_w"][l], S)
        G["w_down"][l] = _wgrad(sv["hid"], dx[None], D, "wg_down", ga=NJ)[:, 0]
        G["ffn_conv_w"][l] = dcw[:, :, 0:3, :]
        G["ffn_conv_b"][l] = dcw[:, :, 3, :]
        G["w_up"][l] = _wgrad(du.reshape(2 * NJ, T, FB), sv["h2"][None], D, "wg_up", ga=NJ)[:, 0]
        token = on_ffn_grads(l, G) if on_ffn_grads is not None else None
        ln2 = W["ln2_g"][l] if token is None else W["ln2_g"][l] + token[0, 0]
        dx1, G["ln2_g"][l] = _dx_rms(du, W["w_up"][l], sv["x1"], ln2, dx, True, "dx_up")
        G["w_o"][l] = _wgrad(sv["y"][None], dx1[None], D, "wg_o")[0, 0]
        dy = _mm_nt(dx1, W["w_o"][l], "mm_dy")
        do_a, dd_a, do_b, dd_b, dgvgg, dgn, dcvw, dcn = _mix_bwd(
            dy, sv["o_a"], sv["o_b"], sv["p_rest"], sv["cv"], W["g_out_fox"][l], W["g_out_dil"][l], W["conv_w"][l],
            W["cnorm_g"][l], W["cnorm_b"][l], S)
        G["g_out_fox"][l], G["g_out_dil"][l] = dgn[0], dgn[1]
        G["conv_w"][l], G["conv_b"][l] = dcvw[:CONV_K], dcvw[CONV_K]
        G["cnorm_g"][l], G["cnorm_b"][l] = dcn[0], dcn[1]
        dq_a, dk_a, dv_a, dqa, dka = _attn_bwd(sv["p_qkv"], 0, tab_fox, do_a, sv["lse_a"], dd_a, sv["kat_a"], S, sv["aq"], sv["ak"],
                                               name="fox_bwd")
        dq_b, dk_b, dv_b = _attn_bwd(sv["p_qkv"], 3 * NPAIR, tab_dil, do_b, sv["lse_b"], dd_b, sv["kat_b"], S, name="dil_bwd")
        dfa, db = _gate_bwd(dqa, dka, sv["p_rest"], W["b_forget"][l], S, TB)
        G["b_forget"][l] = db[0, :6]
        dp = jnp.concatenate([dq_a.astype(BF16), dk_a, dv_a, dq_b.astype(BF16), dk_b, dv_b, dgvgg, dfa,
                              jnp.zeros((T, 128), BF16)], axis=1)
        G["w_in"][l] = _wgrad(sv["h"][None], dp[None], NPAD, "wg_in")[0, 0]
        token = on_layer_grads(l, G) if on_layer_grads is not None else None
        ln1 = W["ln1_g"][l] if token is None else W["ln1_g"][l] + token[0, 0]
        dx, G["ln1_g"][l] = _dx_rms(dp, W["w_in"][l], sv["x"], ln1, dx1, False, "dx_in")
    return loss, dx, G


_MATMUL_W = ("w_in", "w_o", "w_up", "w_down")
_SHARDED = _MATMUL_W + ("conv_w", "ffn_conv_w")
_SMALL = ("ln1_g", "b_forget", "g_out_fox", "g_out_dil", "conv_b", "cnorm_g", "cnorm_b", "ln2_g", "ffn_conv_b", "g_final")
_ORDER = ("ln1_g", "w_in", "b_forget", "g_out_fox", "g_out_dil", "conv_w", "conv_b", "cnorm_g", "cnorm_b", "w_o", "ln2_g",
          "w_up", "ffn_conv_w", "ffn_conv_b", "w_down", "g_final")


def _kernel_ready(k, zone):
    if k == "w_in":
        return zone.reshape(D, NPAD)
    if k == "w_o":
        return zone.reshape(D, D)
    if k == "w_up":
        return zone.reshape(2, NJ, D, FB)
    if k == "w_down":
        return zone.reshape(NJ, FB, D)
    if k == "conv_w":
        cw = jnp.transpose(zone.reshape(NDEV, CONV_K, CC // NDEV), (1, 0, 2)).reshape(CONV_K, CC)
        return jnp.concatenate([cw, jnp.zeros((1, CC), F32)], axis=0)
    return zone.reshape(2, NJ, 3, FB)


def _full_weights(P, gathered):
    W = {}
    row = lambda a: [a[l][None, :] for l in range(DEPTH)]
    for k in _SHARDED:
        W[k] = [None if gathered[k][l] is None else _kernel_ready(k, gathered[k][l]) for l in range(DEPTH)]
    W["ffn_conv_b"] = [P["ffn_conv_b"][l].reshape(2, NJ, 1, FB) for l in range(DEPTH)]
    W["b_forget"] = [jnp.concatenate([P["b_forget"][l], jnp.zeros((128 - 6,), F32)])[None, :] for l in range(DEPTH)]
    for k in ("ln1_g", "ln2_g", "g_out_fox", "g_out_dil", "conv_b", "cnorm_g", "cnorm_b"):
        W[k] = row(P[k])
    W["g_final"] = P["g_final"][None, :]
    return W


def kernel(x, ln1_g, w_in, b_forget, g_out_fox, g_out_dil, conv_w, conv_b, cnorm_g, cnorm_b, w_o, ln2_g, w_up, ffn_conv_w, ffn_conv_b, w_down, g_final, loss_target, m_ln1_g, m_w_in, m_b_forget, m_g_out_fox, m_g_out_dil, m_conv_w, m_conv_b, m_cnorm_g, m_cnorm_b, m_w_o, m_ln2_g, m_w_up, m_ffn_conv_w, m_ffn_conv_b, m_w_down, m_g_final, v_ln1_g, v_w_in, v_b_forget, v_g_out_fox, v_g_out_dil, v_conv_w, v_conv_b, v_cnorm_g, v_cnorm_b, v_w_o, v_ln2_g, v_w_up, v_ffn_conv_w, v_ffn_conv_b, v_w_down, v_g_final):
    P = dict(ln1_g=ln1_g, w_in=w_in, b_forget=b_forget, g_out_fox=g_out_fox, g_out_dil=g_out_dil, conv_w=conv_w, conv_b=conv_b,
             cnorm_g=cnorm_g, cnorm_b=cnorm_b, w_o=w_o, ln2_g=ln2_g, w_up=w_up, ffn_conv_w=ffn_conv_w, ffn_conv_b=ffn_conv_b,
             w_down=w_down, g_final=g_final)
    M = dict(ln1_g=m_ln1_g, w_in=m_w_in, b_forget=m_b_forget, g_out_fox=m_g_out_fox, g_out_dil=m_g_out_dil, conv_w=m_conv_w,
             conv_b=m_conv_b, cnorm_g=m_cnorm_g, cnorm_b=m_cnorm_b, w_o=m_w_o, ln2_g=m_ln2_g, w_up=m_w_up, ffn_conv_w=m_ffn_conv_w,
             ffn_conv_b=m_ffn_conv_b, w_down=m_w_down, g_final=m_g_final)
    V = dict(ln1_g=v_ln1_g, w_in=v_w_in, b_forget=v_b_forget, g_out_fox=v_g_out_fox, g_out_dil=v_g_out_dil, conv_w=v_conv_w,
             conv_b=v_conv_b, cnorm_g=v_cnorm_g, cnorm_b=v_cnorm_b, w_o=v_w_o, ln2_g=v_ln2_g, w_up=v_w_up, ffn_conv_w=v_ffn_conv_w,
             ffn_conv_b=v_ffn_conv_b, w_down=v_w_down, g_final=v_g_final)
    Bl, S, _ = x.shape
    T = Bl * S

    def shard(k, l):
        if k == "w_in":
            return _pad_w_in(P[k][l].astype(BF16))
        return P[k][l].astype(BF16) if k in _MATMUL_W else P[k][l]

    early = [("w_in", 0)] + [(k, l) for k in ("conv_w", "ffn_conv_w") for l in range(DEPTH)]
    late = [(k, 0) for k in ("w_o", "w_up", "w_down")] + [(k, 1) for k in _MATMUL_W]
    gathered = {k: [None] * DEPTH for k in _SHARDED}
    plan_early = _Plan([[shard(k, l)] for k, l in early], True)
    for (k, l), zone in zip(early, _exchange(plan_early, "gather_early")):
        gathered[k][l] = zone
    plan_late = _Plan([[shard(k, l)] for k, l in late], True)
    late_handle, late_token = _exchange_start(plan_late, "gather_late_start", plan_late.zones_with_own())
    W = _full_weights(P, gathered)
    W["ln1_g"][0] = W["ln1_g"][0] + late_token[0, 0]

    def late_weights(W, after):
        zones = _exchange_wait(plan_late, "gather_late_wait", late_handle, after)
        W = dict(W)
        for (k, l), zone in zip(late, zones):
            W[k] = list(W[k])
            W[k][l] = _kernel_ready(k, zone)
        return W

    def owner_block(G, l, k):
        if k == "w_in":
            blk = _unpad_w_in(G[k][l]).reshape(NDEV, D // NDEV, N_IN)
        elif k == "w_o":
            blk = G[k][l].reshape(NDEV, D // NDEV, D)
        elif k == "w_up":
            blk = G[k][l]
        elif k == "w_down":
            blk = G[k][l].reshape(NDEV, DFF // NDEV, D)
        elif k == "conv_w":
            blk = jnp.transpose(G[k][l].reshape(CONV_K, NDEV, CC // NDEV), (1, 0, 2))
        else:
            blk = G[k][l].reshape(NDEV, 3, FB)
        return blk.astype(GRAD_DTYPE)

    flying = []

    def start_scatter(tag, G, l, keys):
        plan = _Plan([[owner_block(G, l, k)] for k in keys], False)
        handle, token = _exchange_start(plan, "scatter_%s_start" % tag, plan.zones_with_own())
        flying.append((tag, l, keys, plan, handle))
        return token

    def on_layer_grads(l, G):
        return start_scatter("l1", G, l, _MATMUL_W) if l == 1 else start_scatter("l0_rest", G, l, ("w_in", "w_o"))

    def on_ffn_grads(l, G):
        return start_scatter("l0_ffn", G, l, ("w_up", "w_down")) if l == 0 else None

    loss, dx, G = _local_step(x.reshape(T, D), loss_target.reshape(T, D), W, S, late_weights, on_layer_grads, on_ffn_grads)

    parts = {}
    for tag, l, keys, plan, handle in flying:
        for k, zone in zip(keys, _exchange_wait(plan, "scatter_%s_wait" % tag, handle, dx)):
            parts[(k, l)] = zone
    plan_last = _Plan([[owner_block(G, l, k) for l in range(DEPTH)] for k in ("conv_w", "ffn_conv_w")], False)
    last = _exchange(plan_last, "scatter_last")
    small_parts_sharded = {"conv_w": last[0], "ffn_conv_w": last[1]}

    small = [jnp.stack(G[k]).reshape(-1) for k in _SMALL[:-1]] + [G["g_final"].reshape(-1), loss[0, :1]]
    sizes = [int(s.shape[0]) for s in small]
    flat = jnp.concatenate(small)
    n_small = -(-flat.shape[0] // 1024) * 1024
    flat = jnp.concatenate([flat, jnp.zeros((n_small - flat.shape[0],), F32)]).reshape(n_small // 128, 128)
    small_parts = _exchange(_Plan([[flat]], True), "gather_small")[0]

    out_g, out_d, out_m, out_v = {}, {}, {}, {}
    sg = _sum_parts(small_parts.reshape(NDEV, n_small // 128, 128), "sum_small").reshape(-1)
    off = 0
    summed = {}
    for k, n in zip(_SMALL + ("loss",), sizes):
        summed[k] = sg[off:off + n]
        off += n
    for k in _ORDER:
        shp = P[k].shape
        C = shp[-1]
        if k in _MATMUL_W:
            swap = k == "w_up"
            R0, Cw = (shp[2], shp[1]) if swap else (shp[1], shp[2])
            flat2 = lambda t: (jnp.swapaxes(t, 1, 2) if swap else t).reshape(DEPTH * R0, Cw)
            res = None
            for l in reversed(range(DEPTH)):
                res = _adamw(parts[(k, l)].reshape(NDEV, R0, Cw), flat2(P[k]), flat2(M[k]), flat2(V[k]),
                             "adamw_%s_l%d" % (k, l), row0=l * R0, prev=res)
            if swap:
                res = [jnp.swapaxes(t.reshape(DEPTH, R0, Cw), 1, 2) for t in res]
        else:
            pr = small_parts_sharded[k].reshape((NDEV, -1, C)) if k in _SHARDED else summed[k].reshape((1, -1, C))
            R = pr.shape[1]
            res = _adamw(pr, P[k].reshape(R, -1), M[k].reshape(R, -1), V[k].reshape(R, -1), "adamw_" + k)
        out_g[k], out_d[k], out_m[k], out_v[k] = (t.reshape(shp) for t in res)

    loss_out = summed["loss"].reshape(())
    grad_x = dx.reshape(Bl, S, D)
    return (loss_out, grad_x, *[out_g[k] for k in _ORDER], *[out_d[k] for k in _ORDER], *[out_m[k] for k in _ORDER],
            *[out_v[k] for k in _ORDER])
```

```python
import functools

import numpy as np
import jax
import jax.numpy as jnp
from jax import lax
from jax.experimental import pallas as pl
from jax.experimental.pallas import tpu as pltpu

F32 = jnp.float32
BF16 = jnp.bfloat16
GRAD_DTYPE = BF16
HIGHEST = lax.Precision.HIGHEST

D = 1024
DEPTH = 2
HD = 64
WA = 384
NPAIR = 3
CC = 256
CONV_K = 31
DFF = 2816
FB = 704
NJ = 4
NDEV = 8
EPS = 1e-6
NQKV = 2304
NREST = 768
NPAD = NQKV + NREST
O_FA, O_QB, N_IN = 1152, 1158, 2822
DILATION_PAIRS = ((128, 1), (512, 4), (2048, 16))
NEG = -1e30

ADAM_LR, ADAM_B1, ADAM_B2, ADAM_EPS, ADAM_WD, ADAM_STEP = 0.001, 0.9, 0.999, 1e-08, 0.01, 10

VMEM_LIMIT = 48 * 1024 * 1024
VMEM_LIMIT_BIG = 56 * 1024 * 1024


def _cp(*sem):
    return pltpu.CompilerParams(dimension_semantics=sem, vmem_limit_bytes=VMEM_LIMIT)


def _sds(shape, dtype):
    return jax.ShapeDtypeStruct(shape, dtype)


def _dot(a, b, prec=None):
    return jnp.dot(a, b, preferred_element_type=F32, precision=prec)


def _dot_nt(a, b, prec=None):
    return lax.dot_general(a, b, (((1,), (1,)), ((), ())), preferred_element_type=F32, precision=prec)


def _dot_tn(a, b):
    return lax.dot_general(a, b, (((0,), (0,)), ((), ())), preferred_element_type=F32)


def _sigmoid(z):
    return 0.5 * jnp.tanh(0.5 * z) + 0.5


def _rms_bwd(dh, x, g):
    r = lax.rsqrt(jnp.mean(x * x, axis=-1, keepdims=True) + EPS)
    xh = x * r
    dg = jnp.sum(dh * xh, axis=0, keepdims=True)
    dxh = dh * g
    dx = r * (dxh - xh * jnp.mean(dxh * xh, axis=-1, keepdims=True))
    return dx, dg


class _Plan:
    def __init__(self, groups, gather, slots=None, n_slots=None):
        self.groups, self.gather = groups, gather
        self.slots = slots or [list(range(len(g))) for g in groups]
        self.n_slots = n_slots or [len(g) for g in groups]
        self.flat = [a for g in groups for a in g]
        self.where = [(gi, s) for gi, g in enumerate(groups) for s in self.slots[gi]]
        self.zone_shapes = [_sds((NDEV, ns) + (g[0].shape if gather else g[0].shape[1:]), g[0].dtype)
                            for g, ns in zip(groups, self.n_slots)]

    def new_zones(self):
        return [lax.empty(z.shape, z.dtype) for z in self.zone_shapes]

    def me(self):
        x, y, c = lax.axis_index("x"), lax.axis_index("y"), lax.axis_index("c")
        return 4 * x + 2 * y + c

    def zones_with_own(self):
        me = self.me()
        zones = self.new_zones()
        for a, (gi, s) in enumerate(self.where):
            src = self.flat[a] if self.gather else lax.dynamic_index_in_dim(self.flat[a], me, 0, keepdims=False)
            zones[gi] = lax.dynamic_update_slice(zones[gi], src[None, None], (me, s) + (0,) * src.ndim)
        return zones

    def own_copies(self, ins, zones, sems):
        me = self.me()
        return [pltpu.make_async_copy(ins[a] if self.gather else ins[a].at[me], zones[gi].at[me, s], sems.at[a])
                for a, (gi, s) in enumerate(self.where)]

    def remote_copies(self, ins, zones, send_sems, recv_sems):
        x, y, c = lax.axis_index("x"), lax.axis_index("y"), lax.axis_index("c")
        me = 4 * x + 2 * y + c
        out = []
        for a, (gi, s) in enumerate(self.where):
            for k in range(1, NDEV):
                px, py, pc = x ^ ((k >> 2) & 1), y ^ ((k >> 1) & 1), c ^ (k & 1)
                out.append(pltpu.make_async_remote_copy(
                    src_ref=ins[a] if self.gather else ins[a].at[4 * px + 2 * py + pc], dst_ref=zones[gi].at[me, s],
                    send_sem=send_sems.at[a * (NDEV - 1) + k - 1], recv_sem=recv_sems.at[a * (NDEV - 1) + k - 1],
                    device_id=(px, py, pc), device_id_type=pl.DeviceIdType.MESH))
        return out


_ANY = pl.BlockSpec(memory_space=pl.ANY)
_HBM = pl.BlockSpec(memory_space=pltpu.HBM)
_SEM = pl.BlockSpec(memory_space=pltpu.SEMAPHORE)


def _exchange(plan, name, zones=None):
    n, nz = len(plan.flat), len(plan.groups)
    zones = zones or plan.new_zones()

    def body(*refs):
        ins, zin = refs[:n], refs[n:n + nz]
        send_sems, recv_sems, local_sems = refs[n + 2 * nz:]
        copies = plan.own_copies(ins, zin, local_sems) + plan.remote_copies(ins, zin, send_sems, recv_sems)
        for cp in copies:
            cp.start()
        for cp in copies:
            cp.wait()

    return pl.pallas_call(
        body, name=name, out_shape=plan.zone_shapes, in_specs=[_ANY] * (n + nz), out_specs=[_ANY] * nz,
        input_output_aliases={n + g: g for g in range(nz)},
        scratch_shapes=[pltpu.SemaphoreType.DMA((n * (NDEV - 1),)), pltpu.SemaphoreType.DMA((n * (NDEV - 1),)),
                        pltpu.SemaphoreType.DMA((n,))],
        compiler_params=pltpu.CompilerParams(has_side_effects=True),
    )(*plan.flat, *zones)


def _exchange_start(plan, name, zones):
    n, nz = len(plan.flat), len(plan.groups)
    hbm = lambda a: pltpu.HBM(a.shape, a.dtype)

    def body(*refs):
        ins, zin = refs[:n], refs[n:n + nz]
        send_sems, recv_sems = refs[n + nz], refs[n + nz + 1]
        token = refs[-1]
        for cp in plan.remote_copies(ins, zin, send_sems, recv_sems):
            cp.start()
        token[...] = jnp.zeros_like(token)

    outs = pl.pallas_call(
        body, name=name,
        out_shape=(pltpu.SemaphoreType.DMA((n * (NDEV - 1),)), pltpu.SemaphoreType.DMA((n * (NDEV - 1),)),
                   *[hbm(a) for a in plan.flat], *[hbm(z) for z in plan.zone_shapes], _sds((8, 128), F32)),
        in_specs=[_HBM] * (n + nz),
        out_specs=(_SEM, _SEM, *[_HBM] * (n + nz), pl.BlockSpec(memory_space=pltpu.VMEM)),
        input_output_aliases={i: 2 + i for i in range(n + nz)},
        compiler_params=pltpu.CompilerParams(has_side_effects=pltpu.SideEffectType.DATAFLOW_SIDE_EFFECTING),
    )(*[pltpu.with_memory_space_constraint(a, pltpu.HBM) for a in plan.flat],
      *[pltpu.with_memory_space_constraint(z, pltpu.HBM) for z in zones])
    return outs[:-1], outs[-1]


def _exchange_wait(plan, name, handle, after):
    n, nz = len(plan.flat), len(plan.groups)
    send_sems, recv_sems = handle[0], handle[1]
    thru = handle[2:]
    hbm = lambda a: pltpu.HBM(a.shape, a.dtype)

    def body(*refs):
        ins, zin = refs[:n], refs[n:n + nz]
        ssem, rsem = refs[n + nz], refs[n + nz + 1]
        for cp in plan.remote_copies(ins, zin, ssem, rsem):
            cp.wait_send()
            cp.wait_recv()

    outs = pl.pallas_call(
        body, name=name, out_shape=tuple(hbm(a) for a in thru),
        in_specs=[_HBM] * (n + nz) + [_SEM, _SEM, _ANY], out_specs=tuple([_HBM] * (n + nz)),
        input_output_aliases={i: i for i in range(n + nz)},
        compiler_params=pltpu.CompilerParams(has_side_effects=pltpu.SideEffectType.DATAFLOW_SIDE_EFFECTING),
    )(*thru, send_sems, recv_sems, after)
    return list(outs[n:])


def _sum_parts(parts, name):
    npart, R, C = parts.shape

    def body(p_ref, o_ref):
        s = p_ref[0]
        for k in range(1, npart):
            s = s + p_ref[k]
        o_ref[...] = s

    return pl.pallas_call(body, name=name, out_shape=_sds((R, C), F32))(parts)


def _rms_fwd(x, g):
    T = x.shape[0]
    tm = min(512, T)

    def body(x_ref, g_ref, o_ref):
        xv = x_ref[...]
        r = lax.rsqrt(jnp.mean(xv * xv, axis=-1, keepdims=True) + EPS)
        o_ref[...] = (xv * r * g_ref[...]).astype(BF16)

    return pl.pallas_call(
        body, name="rms_fwd", grid=(T // tm,), out_shape=_sds((T, D), BF16),
        in_specs=[pl.BlockSpec((tm, D), lambda i: (i, 0)), pl.BlockSpec((1, D), lambda i: (0, 0))],
        out_specs=pl.BlockSpec((tm, D), lambda i: (i, 0)), compiler_params=_cp("parallel"))(x, g)


def _mm_in(h, w):
    T = h.shape[0]
    tm = min(512, T)

    def body(h_ref, w_ref, q_ref, r_ref):
        hv = h_ref[...]
        q_ref[...] = _dot(hv, w_ref[:, :NQKV]).astype(BF16)
        r_ref[...] = _dot(hv, w_ref[:, NQKV:])

    return pl.pallas_call(
        body, name="mm_in", grid=(T // tm,), out_shape=[_sds((T, NQKV), BF16), _sds((T, NREST), F32)],
        in_specs=[pl.BlockSpec((tm, D), lambda i: (i, 0)), pl.BlockSpec((D, NPAD), lambda i: (0, 0))],
        out_specs=[pl.BlockSpec((tm, NQKV), lambda i: (i, 0)), pl.BlockSpec((tm, NREST), lambda i: (i, 0))],
        compiler_params=_cp("parallel"))(h, w)


AUG_Q1, AUG_K1 = 3, 0


def _aug_lane0(h):
    return HD * (1 - h % 2)


def _aug_tables():
    selq = np.zeros((NPAIR, 3 * 128, 128), np.float32)
    selk = np.zeros((NPAIR, 3 * 128, 128), np.float32)
    oneq = np.zeros((NPAIR, 1, 128), np.float32)
    onek = np.zeros((NPAIR, 1, 128), np.float32)
    for h in range(2 * NPAIR):
        p, l0 = h // 2, _aug_lane0(h)
        for piece in range(3):
            selq[p, 128 * piece + h, l0 + piece] = 1.0
            selk[p, 128 * piece + h, l0 + 3 + piece] = -1.0
            oneq[p, 0, l0 + AUG_Q1 + piece] = 1.0
            onek[p, 0, l0 + AUG_K1 + piece] = 1.0
    return [jnp.asarray(a, BF16) for a in (selq, selk, oneq, onek)]


def _gate_fwd(p_rest, bf, S, TB):
    T = p_rest.shape[0]
    B, nb = T // S, S // TB
    selq, selk, oneq, onek = _aug_tables()

    def body(z_ref, b_ref, sq_ref, sk_ref, oq_ref, ok_ref, aq_ref, ak_ref):
        tri = (lax.broadcasted_iota(jnp.int32, (TB, TB), 0) >= lax.broadcasted_iota(jnp.int32, (TB, TB), 1)).astype(F32)
        carry = jnp.zeros((1, 128), F32)
        for j in range(nb):
            rows = slice(j * TB, (j + 1) * TB)
            z = z_ref[rows, :] + b_ref[...]
            lf = jnp.minimum(z, 0.0) - jnp.log(1.0 + jnp.exp(-jnp.abs(z)))
            cb = _dot(tri, lf, HIGHEST) + carry
            carry = cb[TB - 1:TB, :]
            hi = cb.astype(BF16)
            mid = (cb - hi.astype(F32)).astype(BF16)
            lo = (cb - hi.astype(F32) - mid.astype(F32)).astype(BF16)
            pieces = jnp.concatenate([hi, mid, lo], axis=1)
            for p in range(NPAIR):
                aq_ref[p, rows, :] = (_dot(pieces, sq_ref[p]) + oq_ref[p].astype(F32)).astype(BF16)
                ak_ref[p, rows, :] = (_dot(pieces, sk_ref[p]) + ok_ref[p].astype(F32)).astype(BF16)

    full = lambda a: pl.BlockSpec(a.shape, lambda b: (0,) * a.ndim)
    return pl.pallas_call(
        body, name="gate_fwd", grid=(B,), out_shape=[_sds((NPAIR, T, 128), BF16), _sds((NPAIR, T, 128), BF16)],
        in_specs=[pl.BlockSpec((S, 128), lambda b: (b, 4)), pl.BlockSpec((1, 128), lambda b: (0, 0)),
                  full(selq), full(selk), full(oneq), full(onek)],
        out_specs=[pl.BlockSpec((NPAIR, S, 128), lambda b: (0, b, 0)), pl.BlockSpec((NPAIR, S, 128), lambda b: (0, b, 0))],
        compiler_params=_cp("parallel"))(p_rest, bf, selq, selk, oneq, onek)


def _bias_table(S, n_key, n_query, fox):
    unit = min(n_key, n_query)
    d_min = -(n_query // unit - 1)
    d = np.arange(d_min, S // unit)[:, None, None]
    delta = d * unit + np.arange(n_query)[None, None, :] - np.arange(n_key)[None, :, None]
    if fox:
        mult = (delta >= 0).astype(np.float64)
    else:
        mult = np.zeros(delta.shape)
        for w, dil in DILATION_PAIRS:
            mult += (delta >= 0) & (delta % dil == 0) & (delta <= w)
    with np.errstate(divide="ignore"):
        tab = np.where(mult > 0, np.log(np.maximum(mult, 1e-30)), NEG)
    tab = np.concatenate([tab, np.full((1, n_key, n_query), NEG)], axis=0)
    return jnp.asarray(tab, F32)


def _head_masks():
    lane = lax.broadcasted_iota(jnp.int32, (1, 128), 1)
    return [lane < HD, lane >= HD]


def _transpose_bf16(a):
    return a.astype(F32).T.astype(BF16)


def _col_reduce(x, op):
    while x.shape[0] > 8:
        half = x.shape[0] // 2
        x = op(x[:half], x[half:])
    return jnp.max(x, axis=0, keepdims=True) if op is jnp.maximum else jnp.sum(x, axis=0, keepdims=True)


def _attn_fwd(p_qkv, col0, tab, S, aq=None, ak=None, name="attn"):
    T = p_qkv.shape[0]
    n_tab, TB, TQ = tab.shape
    nb, nq, r = S // TB, S // TQ, TQ // TB
    B = T // S
    fox = aq is not None

    def body(*refs):
        if fox:
            q_ref, k_ref, v_ref, tab_ref, aq_ref, ak_ref, o_ref, l_ref, kat_ref, vta_scr, acc_scr, st_scr = refs
        else:
            q_ref, k_ref, v_ref, tab_ref, o_ref, l_ref, kat_ref, vta_scr, acc_scr, st_scr = refs
        i = pl.program_id(2)
        hms = _head_masks()
        top = lax.broadcasted_iota(jnp.int32, (128, 1), 0) < HD
        last_key_block = r * i + r - 1

        @pl.when(i == 0)
        def _():
            for jj in range(nb):
                rows = slice(jj * TB, (jj + 1) * TB)
                vt = _transpose_bf16(v_ref[rows, :])
                for hh in range(2):
                    vta_scr[hh, jj] = jnp.where(top == (hh == 0), vt, jnp.ones_like(vt))
                    spare_k = ak_ref[0, rows, :] if fox else jnp.zeros((TB, 128), BF16)
                    kat_ref[0, 0, hh, jj] = _transpose_bf16(jnp.where(hms[hh], k_ref[rows, :], spare_k))

        q2 = q_ref[...] * 0.125
        spare_q = aq_ref[0] if fox else jnp.zeros_like(q2)
        qa = [jnp.where(hm, q2, spare_q) for hm in hms]
        acc_scr[...] = jnp.zeros_like(acc_scr)

        def scores(j):
            jc = jnp.minimum(j, nb - 1)
            rows = pl.ds(pl.multiple_of(jc * TB, TB), TB)
            k2 = k_ref[rows, :]
            bias = tab_ref[jnp.where(j <= last_key_block, r * i - j + (r - 1), n_tab - 1)]
            return [_dot_nt(jnp.where(hms[hh], k2, ak_ref[0, rows, :]) if fox else k2, qa[hh]) + bias for hh in range(2)]

        def absorb(j, sts, stats):
            jc = jnp.minimum(j, nb - 1)
            alphas, pts = [], []
            for hh in range(2):
                m_old = stats[2 * hh]
                m_new = jnp.maximum(m_old, _col_reduce(sts[hh], jnp.maximum))
                pts.append(jnp.exp(sts[hh] - m_new).astype(BF16))
                alphas.append(jnp.exp(m_old - m_new))
                stats[2 * hh] = m_new
            pvs = [_dot(vta_scr[hh, jc], pts[hh]) for hh in range(2)]
            for hh in range(2):
                half = slice(HD * hh, HD * (hh + 1))
                ones_row = HD * (1 - hh)
                acc_scr[half, :] = alphas[hh] * acc_scr[half, :] + pvs[hh][half]
                stats[2 * hh + 1] = alphas[hh] * stats[2 * hh + 1] + pvs[hh][ones_row:ones_row + 1]

        def kv_pair(jj, carry):
            stats = list(carry)
            j0 = 2 * jj
            st0 = [st_scr[hh] for hh in range(2)]
            st1 = scores(j0 + 1)
            absorb(j0, st0, stats)
            nxt = scores(j0 + 2)
            for hh in range(2):
                st_scr[hh] = nxt[hh]
            absorb(j0 + 1, st1, stats)
            return tuple(stats)

        first = scores(0)
        for hh in range(2):
            st_scr[hh] = first[hh]
        row = lambda v: jnp.full((1, TQ), v, F32)
        m0, l0, m1, l1 = lax.fori_loop(0, (last_key_block + 2) // 2, kv_pair, (row(NEG), row(0.0), row(NEG), row(0.0)))
        o_ref[...] = (acc_scr[...] * jnp.where(top, 1.0 / l0, 1.0 / l1)).T
        for hh, lse in enumerate((m0 + jnp.log(l0), m1 + jnp.log(l1))):
            for t in range(r):
                l_ref[t, hh] = lse[:, t * TB:(t + 1) * TB]

    in_specs = [pl.BlockSpec((TQ, 128), lambda b, p, i: (b * nq + i, col0 + p)),
                pl.BlockSpec((S, 128), lambda b, p, i: (b, col0 + NPAIR + p)),
                pl.BlockSpec((S, 128), lambda b, p, i: (b, col0 + 2 * NPAIR + p)),
                pl.BlockSpec(tab.shape, lambda b, p, i: (0, 0, 0))]
    args = [p_qkv, p_qkv, p_qkv, tab]
    if fox:
        in_specs += [pl.BlockSpec((1, TQ, 128), lambda b, p, i: (p, b * nq + i, 0)),
                     pl.BlockSpec((1, S, 128), lambda b, p, i: (p, b, 0))]
        args += [aq, ak]
    return pl.pallas_call(
        body, name=name, grid=(B, NPAIR, nq),
        out_shape=[_sds((T, WA), F32), _sds((T // TB, 2 * NPAIR, 1, TB), F32), _sds((B, NPAIR, 2, nb, 128, TB), BF16)],
        in_specs=in_specs,
        out_specs=[pl.BlockSpec((TQ, 128), lambda b, p, i: (b * nq + i, p)),
                   pl.BlockSpec((r, 2, 1, TB), lambda b, p, i: (b * nq + i, p, 0, 0)),
                   pl.BlockSpec((1, 1, 2, nb, 128, TB), lambda b, p, i: (b, p, 0, 0, 0, 0))],
        scratch_shapes=[pltpu.VMEM((2, nb, 128, TB), BF16), pltpu.VMEM((128, TQ), F32), pltpu.VMEM((2, TB, TQ), F32)],
        compiler_params=_cp("parallel", "parallel", "arbitrary"))(*args)


def _phase_copies(src, phases, length):
    for r in range(1, 8):
        phases[r, 0:length - 8, :] = src[pl.ds(r, length - 8), :]


def _tap(src, phases, offset, rows):
    r = offset % 8
    return src[pl.ds(offset, rows), :] if r == 0 else phases[r, pl.ds(offset - r, rows), :]


def _conv_taps(src, phases, w_ref, first_row, rows):
    acc = w_ref[0:1, :] * _tap(src, phases, first_row, rows)
    for k in range(1, CONV_K):
        acc = acc + w_ref[k:k + 1, :] * _tap(src, phases, first_row + k, rows)
    return acc


def _mix_fwd(o_a, o_b, p_rest, g_a, g_b, cw, cb, ng, nbias, S):
    T = o_a.shape[0]
    tm = min(256, S)
    nps = S // tm
    HALO = 32

    def body(oa_ref, ob_ref, pt_ref, ph_ref, ga_ref, gb_ref, cw_ref, cb_ref, ng_ref, nb_ref, y_ref, cv_ref, buf, phases):
        i = pl.program_id(0)
        first = (i % nps) == 0
        for o_ref, g_ref, c0 in ((oa_ref, ga_ref, 0), (ob_ref, gb_ref, WA)):
            o = o_ref[...]
            r = lax.rsqrt(jnp.mean(o * o, axis=-1, keepdims=True) + EPS)
            y_ref[:, c0:c0 + WA] = (o * r * g_ref[...]).astype(BF16)
        ph = jnp.where(first, 0.0, ph_ref[...])
        buf[0:HALO, :] = ph[:, :CC] * _sigmoid(ph[:, CC:])
        pt = pt_ref[...]
        buf[HALO:HALO + tm, :] = pt[:, :CC] * _sigmoid(pt[:, CC:])
        _phase_copies(buf, phases, tm + HALO)
        cv = _conv_taps(buf, phases, cw_ref, HALO - CONV_K + 1, tm) + cb_ref[...]
        cv_ref[...] = cv
        xc = cv - jnp.mean(cv, axis=-1, keepdims=True)
        lo = xc * lax.rsqrt(jnp.mean(xc * xc, axis=-1, keepdims=True) + EPS) * ng_ref[...] + nb_ref[...]
        y_ref[:, 2 * WA:] = (lo * _sigmoid(lo)).astype(BF16)

    row = lambda w: pl.BlockSpec((1, w), lambda i: (0, 0))
    return pl.pallas_call(
        body, name="mix_fwd", grid=(T // tm,), out_shape=[_sds((T, D), BF16), _sds((T, CC), F32)],
        in_specs=[pl.BlockSpec((tm, WA), lambda i: (i, 0)), pl.BlockSpec((tm, WA), lambda i: (i, 0)),
                  pl.BlockSpec((tm, 2 * CC), lambda i: (i, 0)),
                  pl.BlockSpec((HALO, 2 * CC), lambda i: (jnp.maximum(i * (tm // HALO) - 1, 0), 0)),
                  row(WA), row(WA), pl.BlockSpec((32, CC), lambda i: (0, 0)), row(CC), row(CC), row(CC)],
        out_specs=[pl.BlockSpec((tm, D), lambda i: (i, 0)), pl.BlockSpec((tm, CC), lambda i: (i, 0))],
        scratch_shapes=[pltpu.VMEM((tm + HALO, CC), F32), pltpu.VMEM((8, tm + HALO, CC), F32)],
        compiler_params=_cp("parallel"))(o_a, o_b, p_rest, p_rest, g_a, g_b, cw, cb, ng, nbias)


def _mm_res(a, w, resid, g, name):
    nk, T, kb = a.shape
    tm = min(512, T)

    def body(a_ref, w_ref, r_ref, g_ref, x_ref, h_ref):
        xv = r_ref[...]
        for k in range(nk):
            xv = xv + _dot(a_ref[k], w_ref[k])
        x_ref[...] = xv
        r = lax.rsqrt(jnp.mean(xv * xv, axis=-1, keepdims=True) + EPS)
        h_ref[...] = (xv * r * g_ref[...]).astype(BF16)

    return pl.pallas_call(
        body, name=name, grid=(T // tm,), out_shape=[_sds((T, D), F32), _sds((T, D), BF16)],
        in_specs=[pl.BlockSpec((nk, tm, kb), lambda i: (0, i, 0)), pl.BlockSpec((nk, kb, D), lambda i: (0, 0, 0)),
                  pl.BlockSpec((tm, D), lambda i: (i, 0)), pl.BlockSpec((1, D), lambda i: (0, 0))],
        out_specs=[pl.BlockSpec((tm, D), lambda i: (i, 0)), pl.BlockSpec((tm, D), lambda i: (i, 0))],
        compiler_params=_cp("parallel"))(a, w, resid, g)


def _up_fwd(h2, w_up, fcw, fcb, S):
    T = h2.shape[0]
    tm = min(512, S)
    nps = S // tm

    def body(h_ref, hh_ref, w_ref, cw_ref, cb_ref, u_ref, c_ref, hid_ref, buf, hbuf):
        i = pl.program_id(1)
        first = (i % nps) == 0
        hbuf[0:tm, :] = h_ref[...]
        hbuf[tm:tm + 8, :] = jnp.where(first, jnp.zeros_like(hh_ref[...]), hh_ref[...])
        c = []
        for half in range(2):
            ua = _dot(hbuf[...], w_ref[half, 0])
            u = ua[0:tm]
            u_ref[half, 0] = u
            buf[half, 0:8, :] = ua[tm:tm + 8]
            buf[half, 8:8 + tm, :] = u
            cw = cw_ref[half, 0]
            c.append(cb_ref[half, 0] + cw[0:1] * buf[half, pl.ds(6, tm), :] + cw[1:2] * buf[half, pl.ds(7, tm), :] + cw[2:3] * u)
        for half in range(2):
            c_ref[half, 0] = c[half]
        hid_ref[0] = (c[0] * _sigmoid(c[0]) * c[1]).astype(BF16)

    return pl.pallas_call(
        body, name="up_fwd", grid=(NJ, T // tm),
        out_shape=[_sds((2, NJ, T, FB), F32), _sds((2, NJ, T, FB), F32), _sds((NJ, T, FB), BF16)],
        in_specs=[pl.BlockSpec((tm, D), lambda j, i: (i, 0)),
                  pl.BlockSpec((8, D), lambda j, i: (jnp.maximum(i * (tm // 8) - 1, 0), 0)),
                  pl.BlockSpec((2, 1, D, FB), lambda j, i: (0, j, 0, 0)),
                  pl.BlockSpec((2, 1, 3, FB), lambda j, i: (0, j, 0, 0)),
                  pl.BlockSpec((2, 1, 1, FB), lambda j, i: (0, j, 0, 0))],
        out_specs=[pl.BlockSpec((2, 1, tm, FB), lambda j, i: (0, j, i, 0)), pl.BlockSpec((2, 1, tm, FB), lambda j, i: (0, j, i, 0)),
                   pl.BlockSpec((1, tm, FB), lambda j, i: (j, i, 0))],
        scratch_shapes=[pltpu.VMEM((2, tm + 8, FB), F32), pltpu.VMEM((tm + 8, D), BF16)],
        compiler_params=_cp("parallel", "parallel"))(h2, h2, w_up, fcw, fcb)


def _final_loss(x, tgt, g):
    T = x.shape[0]
    tm = min(512, T)

    def body(x_ref, t_ref, g_ref, dx_ref, loss_ref, dg_ref):
        @pl.when(pl.program_id(0) == 0)
        def _():
            loss_ref[...] = jnp.zeros_like(loss_ref)
            dg_ref[...] = jnp.zeros_like(dg_ref)

        xv, gv = x_ref[...], g_ref[...]
        r = lax.rsqrt(jnp.mean(xv * xv, axis=-1, keepdims=True) + EPS)
        e = xv * r * gv - t_ref[...]
        loss_ref[...] += 0.5 * jnp.sum(jnp.mean(e * e, axis=-1, keepdims=True), axis=0, keepdims=True)
        dx, dg = _rms_bwd(e * (1.0 / D), xv, gv)
        dx_ref[...] = dx
        dg_ref[...] += dg

    return pl.pallas_call(
        body, name="final_loss", grid=(T // tm,), out_shape=[_sds((T, D), F32), _sds((1, 128), F32), _sds((1, D), F32)],
        in_specs=[pl.BlockSpec((tm, D), lambda i: (i, 0)), pl.BlockSpec((tm, D), lambda i: (i, 0)), pl.BlockSpec((1, D), lambda i: (0, 0))],
        out_specs=[pl.BlockSpec((tm, D), lambda i: (i, 0)), pl.BlockSpec((1, 128), lambda i: (0, 0)), pl.BlockSpec((1, D), lambda i: (0, 0))],
        compiler_params=_cp("arbitrary"))(x, tgt, g)


def _down_bwd(dx2, w_down, u, c, fcw, S):
    T = dx2.shape[0]
    tm = min(512, S)
    nps = S // tm
    nblk8 = T // 8

    def body(dx_ref, dxn_ref, wd_ref, ut_ref, ct_ref, cn_ref, cw_ref, du_ref, dcw_ref, dxb, dcbuf, dsh):
        i = pl.program_id(1)
        last = (i % nps) == nps - 1

        @pl.when(i == 0)
        def _():
            dcw_ref[...] = jnp.zeros_like(dcw_ref)

        dxb[0:tm, :] = dx_ref[...].astype(BF16)
        dxb[tm:tm + 8, :] = jnp.where(last, 0.0, dxn_ref[...]).astype(BF16)
        dh = _dot_nt(dxb[...], wd_ref[0])
        ce = [jnp.concatenate([ct_ref[half, 0], cn_ref[half, 0]], axis=0) for half in range(2)]
        sg = _sigmoid(ce[0])
        dc = [dh * ce[1] * (sg * (1.0 + ce[0] * (1.0 - sg))), dh * (ce[0] * sg)]
        for half in range(2):
            cw = cw_ref[half, 0]
            dcbuf[...] = dc[half]
            for k in range(2):
                dsh[k] = dcbuf[pl.ds(k + 1, tm), :]
            ahead = [dc[half][0:tm], dsh[0], dsh[1]]
            du_ref[half, 0] = (cw[2:3] * ahead[0] + cw[1:2] * ahead[1] + cw[0:1] * ahead[2]).astype(BF16)
            uv = ut_ref[half, 0]
            for k in range(3):
                dcw_ref[half, 0, k:k + 1, :] += jnp.sum(ahead[2 - k] * uv, axis=0, keepdims=True)
            dcw_ref[half, 0, 3:4, :] += jnp.sum(ahead[0], axis=0, keepdims=True)

    ublk = lambda rows, imap: pl.BlockSpec((2, 1, rows, FB), imap)
    return pl.pallas_call(
        body, name="down_bwd", grid=(NJ, T // tm), out_shape=[_sds((2, NJ, T, FB), BF16), _sds((2, NJ, 8, FB), F32)],
        in_specs=[pl.BlockSpec((tm, D), lambda j, i: (i, 0)),
                  pl.BlockSpec((8, D), lambda j, i: (jnp.minimum((i + 1) * (tm // 8), nblk8 - 1), 0)),
                  pl.BlockSpec((1, FB, D), lambda j, i: (j, 0, 0)),
                  ublk(tm, lambda j, i: (0, j, i, 0)),
                  ublk(tm, lambda j, i: (0, j, i, 0)),
                  ublk(8, lambda j, i: (0, j, jnp.minimum((i + 1) * (tm // 8), nblk8 - 1), 0)),
                  pl.BlockSpec((2, 1, 3, FB), lambda j, i: (0, j, 0, 0))],
        out_specs=[ublk(tm, lambda j, i: (0, j, i, 0)), pl.BlockSpec((2, 1, 8, FB), lambda j, i: (0, j, 0, 0))],
        scratch_shapes=[pltpu.VMEM((tm + 8, D), BF16), pltpu.VMEM((tm + 8, FB), F32), pltpu.VMEM((2, tm, FB), F32)],
        compiler_params=_cp("parallel", "arbitrary"))(dx2, dx2, w_down, u, c, c, fcw)


def _wgrad(a, b, tn, name, ga=1, gb=1):
    na, T, ka = a.shape
    nb, _, kb = b.shape
    tk = min(1024, T)
    nt = T // tk

    def body(a_ref, b_ref, o_ref, acc):
        t = pl.program_id(3)

        @pl.when(t == 0)
        def _():
            acc[...] = jnp.zeros_like(acc)

        bs = [b_ref[ib].astype(BF16) for ib in range(gb)]
        for ia in range(ga):
            av = a_ref[ia]
            for ib in range(gb):
                acc[ia, ib] += _dot_tn(av, bs[ib])

        @pl.when(t == nt - 1)
        def _():
            o_ref[...] = acc[...].astype(GRAD_DTYPE)

    return pl.pallas_call(
        body, name=name, grid=(na // ga, nb // gb, kb // tn, nt), out_shape=_sds((na, nb, ka, kb), GRAD_DTYPE),
        in_specs=[pl.BlockSpec((ga, tk, ka), lambda ia, ib, n, t: (ia, t, 0)), pl.BlockSpec((gb, tk, tn), lambda ia, ib, n, t: (ib, t, n))],
        out_specs=pl.BlockSpec((ga, gb, ka, tn), lambda ia, ib, n, t: (ia, ib, 0, n)),
        scratch_shapes=[pltpu.VMEM((ga, gb, ka, tn), F32)],
        compiler_params=_cp("parallel", "parallel", "parallel", "arbitrary"))(a, b)


def _dx_rms(dy, w, x, g, dres, blocked, name):
    T = x.shape[0]
    tm = min(256 if blocked else 512, T)

    def body(dy_ref, w_ref, x_ref, g_ref, r_ref, dx_ref, dg_ref):
        @pl.when(pl.program_id(0) == 0)
        def _():
            dg_ref[...] = jnp.zeros_like(dg_ref)

        if blocked:
            dh = _dot_nt(dy_ref[0, 0], w_ref[0, 0])
            for half, k in [(h, k) for k in range(NJ) for h in range(2)][1:]:
                dh = dh + _dot_nt(dy_ref[half, k], w_ref[half, k])
        else:
            dh = _dot_nt(dy_ref[...], w_ref[...])
        dx, dg = _rms_bwd(dh, x_ref[...], g_ref[...])
        dx_ref[...] = r_ref[...] + dx
        dg_ref[...] += dg

    if blocked:
        dy_spec = pl.BlockSpec((2, NJ, tm, FB), lambda i: (0, 0, i, 0))
        w_spec = pl.BlockSpec((2, NJ, D, FB), lambda i: (0, 0, 0, 0))
    else:
        dy_spec = pl.BlockSpec((tm, dy.shape[1]), lambda i: (i, 0))
        w_spec = pl.BlockSpec(w.shape, lambda i: (0, 0))
    tile = pl.BlockSpec((tm, D), lambda i: (i, 0))
    return pl.pallas_call(
        body, name=name, grid=(T // tm,), out_shape=[_sds((T, D), F32), _sds((1, D), F32)],
        in_specs=[dy_spec, w_spec, tile, pl.BlockSpec((1, D), lambda i: (0, 0)), tile],
        out_specs=[tile, pl.BlockSpec((1, D), lambda i: (0, 0))],
        compiler_params=pltpu.CompilerParams(dimension_semantics=("arbitrary",), vmem_limit_bytes=VMEM_LIMIT_BIG))(dy, w, x, g, dres)


def _mm_nt(a, w, name):
    T = a.shape[0]
    tm = min(512, T)

    def body(a_ref, w_ref, o_ref):
        o_ref[...] = _dot_nt(a_ref[...].astype(BF16), w_ref[...])

    return pl.pallas_call(
        body, name=name, grid=(T // tm,), out_shape=_sds((T, D), F32),
        in_specs=[pl.BlockSpec((tm, D), lambda i: (i, 0)), pl.BlockSpec((D, D), lambda i: (0, 0))],
        out_specs=pl.BlockSpec((tm, D), lambda i: (i, 0)), compiler_params=_cp("parallel"))(a, w)


def _mix_bwd(dy, o_a, o_b, p_rest, cv, g_a, g_b, cw, ng, nbias, S):
    T = dy.shape[0]
    tm = min(256, S)
    nps = S // tm
    HALO = 32
    nblkh = T // HALO
    RE = tm + HALO

    def body(dy_ref, dyn_ref, oa_ref, ob_ref, pt_ref, cvt_ref, cvn_ref, ga_ref, gb_ref, cw_ref, ng_ref, nb_ref,
             doa_ref, dda_ref, dob_ref, ddb_ref, dg_ref, dgn_ref, dcw_ref, dcn_ref, dvbuf, dph):
        i = pl.program_id(0)
        last = (i % nps) == nps - 1

        @pl.when(i == 0)
        def _():
            dgn_ref[...] = jnp.zeros_like(dgn_ref)
            dcw_ref[...] = jnp.zeros_like(dcw_ref)
            dcn_ref[...] = jnp.zeros_like(dcn_ref)

        head_of = (lax.broadcasted_iota(jnp.int32, (8, WA), 1) // HD == lax.broadcasted_iota(jnp.int32, (8, WA), 0)).astype(F32)
        for n, (o_ref, g_ref, do_ref, dd_ref) in enumerate(((oa_ref, ga_ref, doa_ref, dda_ref), (ob_ref, gb_ref, dob_ref, ddb_ref))):
            o = o_ref[...]
            do, dg = _rms_bwd(dy_ref[:, n * WA:(n + 1) * WA], o, g_ref[...])
            dob = do.astype(BF16)
            do_ref[...] = dob
            ddt = _dot_nt(head_of, dob.astype(F32) * o, HIGHEST)
            for hd in range(8):
                dd_ref[0, hd] = ddt[hd:hd + 1, :]
            dgn_ref[n:n + 1, :] += dg

        pt = pt_ref[...]
        gv, sgg = pt[:, :CC], _sigmoid(pt[:, CC:])
        glu = gv * sgg
        cv = jnp.concatenate([cvt_ref[...], cvn_ref[...]], axis=0)
        xc = cv - jnp.mean(cv, axis=-1, keepdims=True)
        r = lax.rsqrt(jnp.mean(xc * xc, axis=-1, keepdims=True) + EPS)
        xh = xc * r
        lo = xh * ng_ref[...] + nb_ref[...]
        sg = _sigmoid(lo)
        dyc = jnp.concatenate([dy_ref[:, 2 * WA:], jnp.where(last, 0.0, dyn_ref[...])], axis=0)
        dlo = dyc * (sg * (1.0 + lo * (1.0 - sg)))
        dcn_ref[0:1, :] += jnp.sum(dlo[0:tm] * xh[0:tm], axis=0, keepdims=True)
        dcn_ref[1:2, :] += jnp.sum(dlo[0:tm], axis=0, keepdims=True)
        dxh = dlo * ng_ref[...]
        dcv = r * (dxh - jnp.mean(dxh, axis=-1, keepdims=True) - xh * jnp.mean(dxh * xh, axis=-1, keepdims=True))
        dvbuf[...] = dcv
        _phase_copies(dvbuf, dph, RE)
        dct = dcv[0:tm]
        dcw_ref[CONV_K:CONV_K + 1, :] += jnp.sum(dct, axis=0, keepdims=True)
        dglu = jnp.zeros((tm, CC), F32)
        for k in range(CONV_K):
            ahead = _tap(dvbuf, dph, CONV_K - 1 - k, tm)
            dcw_ref[k:k + 1, :] += jnp.sum(ahead * glu, axis=0, keepdims=True)
            dglu = dglu + cw_ref[k:k + 1, :] * ahead
        dg_ref[:, :CC] = (dglu * sgg).astype(BF16)
        dg_ref[:, CC:] = (dglu * gv * sgg * (1.0 - sgg)).astype(BF16)

    row = lambda w: pl.BlockSpec((1, w), lambda i: (0, 0))
    tile = lambda w: pl.BlockSpec((tm, w), lambda i: (i, 0))
    nxt = lambda i: jnp.minimum((i + 1) * (tm // HALO), nblkh - 1)
    const = lambda r, w: pl.BlockSpec((r, w), lambda i: (0, 0))
    ddrow = pl.BlockSpec((1, 8, 1, tm), lambda i: (i, 0, 0, 0))
    return pl.pallas_call(
        body, name="mix_bwd", grid=(T // tm,),
        out_shape=[_sds((T, WA), BF16), _sds((T // tm, 8, 1, tm), F32), _sds((T, WA), BF16), _sds((T // tm, 8, 1, tm), F32),
                   _sds((T, 2 * CC), BF16), _sds((8, WA), F32), _sds((32, CC), F32), _sds((8, CC), F32)],
        in_specs=[tile(D), pl.BlockSpec((HALO, CC), lambda i: (nxt(i), 3)), tile(WA), tile(WA), tile(2 * CC),
                  tile(CC), pl.BlockSpec((HALO, CC), lambda i: (nxt(i), 0)),
                  row(WA), row(WA), const(32, CC), row(CC), row(CC)],
        out_specs=[tile(WA), ddrow, tile(WA), ddrow, tile(2 * CC), const(8, WA), const(32, CC), const(8, CC)],
        scratch_shapes=[pltpu.VMEM((RE, CC), F32), pltpu.VMEM((8, RE, CC), F32)],
        compiler_params=_cp("arbitrary"))(dy, dy, o_a, o_b, p_rest, cv, cv, g_a, g_b, cw, ng, nbias)


def _attn_bwd(p_qkv, col0, tab, do, lse, dd, kat, S, aq=None, ak=None, name="attn_bwd"):
    T = p_qkv.shape[0]
    n_tab, TK, TB = tab.shape
    nb, nkb, r = S // TB, S // TK, TK // TB
    B = T // S
    fox = aq is not None

    def body(*refs):
        if fox:
            (q_ref, k_ref, v_ref, tab_ref, do_ref, l_ref, dd_ref, kat_ref, aq_ref, ak_ref,
             dq_ref, dk_ref, dv_ref, dqa_ref, dka_ref, dqt_scr, dk_scr, dv_scr, sd_scr) = refs
        else:
            (q_ref, k_ref, v_ref, tab_ref, do_ref, l_ref, dd_ref, kat_ref,
             dq_ref, dk_ref, dv_ref, dqt_scr, dk_scr, dv_scr, sd_scr) = refs
        j = pl.program_id(2)
        hms = _head_masks()
        first_q_block = r * j

        @pl.when(j == 0)
        def _():
            dqt_scr[...] = jnp.zeros_like(dqt_scr)

        dk_scr[...] = jnp.zeros_like(dk_scr)
        dv_scr[...] = jnp.zeros_like(dv_scr)
        k2, v2 = k_ref[...], v_ref[...]
        ka = [jnp.where(hm, k2, ak_ref[0] if fox else jnp.zeros_like(k2)) for hm in hms]
        kat = [jnp.concatenate([kat_ref[0, 0, hh, t] for t in range(r)], axis=1) if r > 1 else kat_ref[0, 0, hh, 0]
               for hh in range(2)]

        def operands(i):
            ic = jnp.minimum(i, nb - 1)
            rows = pl.ds(pl.multiple_of(ic * TB, TB), TB)
            q2 = q_ref[rows, :] * 0.125
            do2 = do_ref[rows, :]
            qas = [jnp.where(hms[hh], q2, aq_ref[0, rows, :] if fox else jnp.zeros_like(q2)) for hh in range(2)]
            dohs = [jnp.where(hms[hh], do2, jnp.zeros_like(do2)) for hh in range(2)]
            return ic, qas, dohs

        def scores(i):
            ic, qas, dohs = operands(i)
            bias = tab_ref[jnp.where(i < nb, i - first_q_block, n_tab - 1)]
            return [(_dot_nt(ka[hh], qas[hh]) + bias, _dot_nt(v2, dohs[hh])) for hh in range(2)]

        def absorb(i, sd):
            ic, qas, dohs = operands(i)
            for hh in range(2):
                st, dpt = sd[hh]
                pt = jnp.exp(st - l_ref[ic, hh])
                dsb = (pt * (dpt - dd_ref[ic, hh])).astype(BF16)
                dv_scr[...] += _dot(pt.astype(BF16), dohs[hh])
                dk_scr[hh] += _dot(dsb, qas[hh])
                dqt_scr[hh, ic] += _dot(kat[hh], dsb)

        def q_pair(ii, carry):
            i0 = first_q_block + 2 * ii
            sd0 = [(sd_scr[hh, 0], sd_scr[hh, 1]) for hh in range(2)]
            sd1 = scores(i0 + 1)
            absorb(i0, sd0)
            nxt = scores(i0 + 2)
            for hh in range(2):
                sd_scr[hh, 0], sd_scr[hh, 1] = nxt[hh]
            absorb(i0 + 1, sd1)
            return carry

        first = scores(first_q_block)
        for hh in range(2):
            sd_scr[hh, 0], sd_scr[hh, 1] = first[hh]
        lax.fori_loop(0, (nb - first_q_block + 1) // 2, q_pair, 0)
        dk_ref[...] = jnp.where(hms[0], dk_scr[0], dk_scr[1]).astype(BF16)
        dv_ref[...] = dv_scr[...].astype(BF16)
        if fox:
            dka_ref[...] = jnp.where(hms[0], dk_scr[1], dk_scr[0])

        @pl.when(j == nkb - 1)
        def _():
            for ii in range(nb):
                t0, t1 = dqt_scr[0, ii].T, dqt_scr[1, ii].T
                dq_ref[ii * TB:(ii + 1) * TB, :] = jnp.where(hms[0], t0, t1) * 0.125
                if fox:
                    dqa_ref[ii * TB:(ii + 1) * TB, :] = jnp.where(hms[0], t1, t0)

    seq = lambda c: pl.BlockSpec((S, 128), lambda b, p, j: (b, c + p))
    kvb = lambda c: pl.BlockSpec((TK, 128), lambda b, p, j: (b * nkb + j, c + p))
    stat = pl.BlockSpec((nb, 2, 1, TB), lambda b, p, j: (b, p, 0, 0))
    in_specs = [seq(col0), kvb(col0 + NPAIR), kvb(col0 + 2 * NPAIR), pl.BlockSpec(tab.shape, lambda b, p, j: (0, 0, 0)),
                seq(0), stat, stat, pl.BlockSpec((1, 1, 2, r, 128, TB), lambda b, p, j: (b, p, 0, j, 0, 0))]
    args = [p_qkv, p_qkv, p_qkv, tab, do, lse, dd, kat]
    out_shape = [_sds((T, WA), F32), _sds((T, WA), BF16), _sds((T, WA), BF16)]
    out_specs = [seq(0), kvb(0), kvb(0)]
    if fox:
        in_specs += [pl.BlockSpec((1, S, 128), lambda b, p, j: (p, b, 0)), pl.BlockSpec((1, TK, 128), lambda b, p, j: (p, b * nkb + j, 0))]
        args += [aq, ak]
        out_shape += [_sds((T, WA), F32), _sds((T, WA), F32)]
        out_specs += [seq(0), kvb(0)]
    return pl.pallas_call(
        body, name=name, grid=(B, NPAIR, nkb), out_shape=out_shape, in_specs=in_specs, out_specs=out_specs,
        scratch_shapes=[pltpu.VMEM((2, nb, 128, TB), F32), pltpu.VMEM((2, TK, 128), F32), pltpu.VMEM((TK, 128), F32),
                        pltpu.VMEM((2, 2, TK, TB), F32)],
        compiler_params=_cp("parallel", "parallel", "arbitrary"))(*args)


def _gate_bwd(dqa, dka, p_rest, bf, S, TB):
    T = p_rest.shape[0]
    B, nb = T // S, S // TB

    def body(dqa_ref, dka_ref, z_ref, b_ref, dz_ref, db_ref):
        @pl.when(pl.program_id(0) == 0)
        def _():
            db_ref[...] = jnp.zeros_like(db_ref)

        later = (lax.broadcasted_iota(jnp.int32, (TB, TB), 1) >= lax.broadcasted_iota(jnp.int32, (TB, TB), 0)).astype(F32)
        lane = lax.broadcasted_iota(jnp.int32, (1, 128), 1)
        src, head = lax.broadcasted_iota(jnp.int32, (WA, 128), 0), lax.broadcasted_iota(jnp.int32, (WA, 128), 1)
        spare0 = 128 * (head // 2) + HD * (1 - head % 2)
        pick_q = (src == spare0 + AUG_K1).astype(F32)
        pick_k = (src == spare0 + AUG_Q1).astype(F32)
        carry = jnp.zeros((1, 128), F32)
        dbs = jnp.zeros((1, 128), F32)
        for j in reversed(range(nb)):
            rows = slice(j * TB, (j + 1) * TB)
            dc = _dot(dqa_ref[rows, :], pick_q, HIGHEST) - _dot(dka_ref[rows, :], pick_k, HIGHEST)
            part = _dot(later, dc, HIGHEST)
            tot = part + carry
            carry = carry + part[0:1, :]
            z = z_ref[j * TB:(j + 1) * TB, :] + b_ref[...]
            dz = jnp.where(lane < 6, tot * _sigmoid(-z), 0.0)
            dz_ref[j * TB:(j + 1) * TB, :] = dz.astype(BF16)
            dbs = dbs + jnp.sum(dz, axis=0, keepdims=True)
        db_ref[...] += dbs

    return pl.pallas_call(
        body, name="gate_bwd", grid=(B,), out_shape=[_sds((T, 128), BF16), _sds((1, 128), F32)],
        in_specs=[pl.BlockSpec((S, WA), lambda b: (b, 0)), pl.BlockSpec((S, WA), lambda b: (b, 0)),
                  pl.BlockSpec((S, 128), lambda b: (b, 4)), pl.BlockSpec((1, 128), lambda b: (0, 0))],
        out_specs=[pl.BlockSpec((S, 128), lambda b: (b, 0)), pl.BlockSpec((1, 128), lambda b: (0, 0))],
        compiler_params=_cp("arbitrary"))(dqa, dka, p_rest, bf)


def _adamw(parts, w, m, v, name, row0=0, prev=None):
    npart, Rp, C = parts.shape
    R = w.shape[0]
    tr = Rp
    for cand in (256, 128, 64, 32, 16, 8):
        if Rp % cand == 0 and row0 % cand == 0 and cand * C * 4 <= (1 << 20):
            tr = cand
            break
    if Rp == R and R * C * 4 <= (1 << 20):
        tr = R
    assert Rp % tr == 0 and row0 % tr == 0, (name, Rp, row0, tr)
    b0 = row0 // tr

    def body(p_ref, w_ref, m_ref, v_ref, *rest):
        g_ref, d_ref, mo_ref, vo_ref = rest[-4:]
        g = p_ref[0].astype(F32)
        for k in range(1, npart):
            g = g + p_ref[k].astype(F32)
        mn = ADAM_B1 * m_ref[...] + (1.0 - ADAM_B1) * g
        vn = ADAM_B2 * v_ref[...] + (1.0 - ADAM_B2) * (g * g)
        m_hat = mn / (1.0 - ADAM_B1 ** ADAM_STEP)
        v_hat = vn / (1.0 - ADAM_B2 ** ADAM_STEP)
        g_ref[...] = g
        d_ref[...] = -ADAM_LR * (m_hat / (jnp.sqrt(v_hat) + ADAM_EPS) + ADAM_WD * w_ref[...])
        mo_ref[...] = mn
        vo_ref[...] = vn

    blk = pl.BlockSpec((tr, C), lambda i: (b0 + i, 0))
    prev = list(prev) if prev is not None else []
    return pl.pallas_call(
        body, name=name, grid=(Rp // tr,), out_shape=[_sds((R, C), F32)] * 4,
        in_specs=[pl.BlockSpec((npart, tr, C), lambda i: (0, i, 0)), blk, blk, blk] + [_ANY] * len(prev), out_specs=[blk] * 4,
        input_output_aliases={4 + i: i for i in range(len(prev))},
        compiler_params=_cp("parallel"))(parts, w, m, v, *prev)


def _pad_w_in(w):
    z = jnp.zeros(w.shape[:-1] + (NPAD - N_IN,), w.dtype)
    return jnp.concatenate([w[..., :O_FA], w[..., O_QB:], w[..., O_FA:O_QB], z], axis=-1)


def _unpad_w_in(w):
    n_mid = N_IN - O_QB
    return jnp.concatenate([w[..., :O_FA], w[..., O_FA + n_mid:O_FA + n_mid + 6], w[..., O_FA:O_FA + n_mid]], axis=-1)


def _local_step(x, tgt, W, S, late_weights=None, on_layer_grads=None, on_ffn_grads=None):
    T = x.shape[0]
    TB = min(256, S)
    TW = min(2 * TB, S)
    tab_fox, tab_dil = _bias_table(S, TW, TB, True), _bias_table(S, TW, TB, False)
    tabq_fox, tabq_dil = _bias_table(S, TB, TW, True), _bias_table(S, TB, TW, False)
    saved = []
    h = _rms_fwd(x, W["ln1_g"][0])
    for l in range(DEPTH):
        p_qkv, p_rest = _mm_in(h, W["w_in"][l])
        aq, ak = _gate_fwd(p_rest, W["b_forget"][l], S, TB)
        o_a, lse_a, kat_a = _attn_fwd(p_qkv, 0, tabq_fox, S, aq, ak, name="fox_fwd")
        o_b, lse_b, kat_b = _attn_fwd(p_qkv, 3 * NPAIR, tabq_dil, S, name="dil_fwd")
        y, cv = _mix_fwd(o_a, o_b, p_rest, W["g_out_fox"][l], W["g_out_dil"][l], W["conv_w"][l], W["conv_b"][l],
                     W["cnorm_g"][l], W["cnorm_b"][l], S)
        if l == 0 and late_weights is not None:
            W = late_weights(W, y)
        x1, h2 = _mm_res(y[None], W["w_o"][l][None], x, W["ln2_g"][l], name="mm_o")
        u, c, hid = _up_fwd(h2, W["w_up"][l], W["ffn_conv_w"][l], W["ffn_conv_b"][l], S)
        g_next = W["ln1_g"][l + 1] if l + 1 < DEPTH else W["g_final"]
        x2, h_next = _mm_res(hid, W["w_down"][l], x1, g_next, name="mm_down")
        saved.append(dict(x=x, h=h, p_qkv=p_qkv, p_rest=p_rest, aq=aq, ak=ak, o_a=o_a, lse_a=lse_a, o_b=o_b,
                          lse_b=lse_b, kat_a=kat_a, kat_b=kat_b, y=y, cv=cv, x1=x1, h2=h2, u=u, c=c, hid=hid))
        x, h = x2, h_next
    dx, loss, dg_final = _final_loss(x, tgt, W["g_final"])
    G = {k: [None] * DEPTH for k in ("ln1_g", "w_in", "b_forget", "g_out_fox", "g_out_dil", "conv_w", "conv_b", "cnorm_g",
                                     "cnorm_b", "w_o", "ln2_g", "w_up", "ffn_conv_w", "ffn_conv_b", "w_down")}
    G["g_final"] = dg_final
    for l in reversed(range(DEPTH)):
        sv = saved[l]
        du, dcw = _down_bwd(dx, W["w_down"][l], sv["u"], sv["c"], W["ffn_conv_w"][l], S)
        G["w_down"][l] = _wgrad(sv["hid"], dx[None], D, "wg_down", ga=NJ)[:, 0]
        G["ffn_conv_w"][l] = dcw[:, :, 0:3, :]
        G["ffn_conv_b"][l] = dcw[:, :, 3, :]
        G["w_up"][l] = _wgrad(du.reshape(2 * NJ, T, FB), sv["h2"][None], D, "wg_up", ga=NJ)[:, 0]
        token = on_ffn_grads(l, G) if on_ffn_grads is not None else None
        ln2 = W["ln2_g"][l] if token is None else W["ln2_g"][l] + token[0, 0]
        dx1, G["ln2_g"][l] = _dx_rms(du, W["w_up"][l], sv["x1"], ln2, dx, True, "dx_up")
        G["w_o"][l] = _wgrad(sv["y"][None], dx1[None], D, "wg_o")[0, 0]
        dy = _mm_nt(dx1, W["w_o"][l], "mm_dy")
        do_a, dd_a, do_b, dd_b, dgvgg, dgn, dcvw, dcn = _mix_bwd(
            dy, sv["o_a"], sv["o_b"], sv["p_rest"], sv["cv"], W["g_out_fox"][l], W["g_out_dil"][l], W["conv_w"][l],
            W["cnorm_g"][l], W["cnorm_b"][l], S)
        G["g_out_fox"][l], G["g_out_dil"][l] = dgn[0], dgn[1]
        G["conv_w"][l], G["conv_b"][l] = dcvw[:CONV_K], dcvw[CONV_K]
        G["cnorm_g"][l], G["cnorm_b"][l] = dcn[0], dcn[1]
        dq_a, dk_a, dv_a, dqa, dka = _attn_bwd(sv["p_qkv"], 0, tab_fox, do_a, sv["lse_a"], dd_a, sv["kat_a"], S, sv["aq"], sv["ak"],
                                               name="fox_bwd")
        dq_b, dk_b, dv_b = _attn_bwd(sv["p_qkv"], 3 * NPAIR, tab_dil, do_b, sv["lse_b"], dd_b, sv["kat_b"], S, name="dil_bwd")
        dfa, db = _gate_bwd(dqa, dka, sv["p_rest"], W["b_forget"][l], S, TB)
        G["b_forget"][l] = db[0, :6]
        dp = jnp.concatenate([dq_a.astype(BF16), dk_a, dv_a, dq_b.astype(BF16), dk_b, dv_b, dgvgg, dfa,
                              jnp.zeros((T, 128), BF16)], axis=1)
        G["w_in"][l] = _wgrad(sv["h"][None], dp[None], NPAD, "wg_in")[0, 0]
        token = on_layer_grads(l, G) if on_layer_grads is not None else None
        ln1 = W["ln1_g"][l] if token is None else W["ln1_g"][l] + token[0, 0]
        dx, G["ln1_g"][l] = _dx_rms(dp, W["w_in"][l], sv["x"], ln1, dx1, False, "dx_in")
    return loss, dx, G


_MATMUL_W = ("w_in", "w_o", "w_up", "w_down")
_SHARDED = _MATMUL_W + ("conv_w", "ffn_conv_w")
_SMALL = ("ln1_g", "b_forget", "g_out_fox", "g_out_dil", "conv_b", "cnorm_g", "cnorm_b", "ln2_g", "ffn_conv_b", "g_final")
_ORDER = ("ln1_g", "w_in", "b_forget", "g_out_fox", "g_out_dil", "conv_w", "conv_b", "cnorm_g", "cnorm_b", "w_o", "ln2_g",
          "w_up", "ffn_conv_w", "ffn_conv_b", "w_down", "g_final")


def _kernel_ready(k, zone):
    if k == "w_in":
        return zone.reshape(D, NPAD)
    if k == "w_o":
        return zone.reshape(D, D)
    if k == "w_up":
        return zone.reshape(2, NJ, D, FB)
    if k == "w_down":
        return zone.reshape(NJ, FB, D)
    if k == "conv_w":
        cw = jnp.transpose(zone.reshape(NDEV, CONV_K, CC // NDEV), (1, 0, 2)).reshape(CONV_K, CC)
        return jnp.concatenate([cw, jnp.zeros((1, CC), F32)], axis=0)
    return zone.reshape(2, NJ, 3, FB)


def _full_weights(P, gathered):
    W = {}
    row = lambda a: [a[l][None, :] for l in range(DEPTH)]
    for k in _SHARDED:
        W[k] = [None if gathered[k][l] is None else _kernel_ready(k, gathered[k][l]) for l in range(DEPTH)]
    W["ffn_conv_b"] = [P["ffn_conv_b"][l].reshape(2, NJ, 1, FB) for l in range(DEPTH)]
    W["b_forget"] = [jnp.concatenate([P["b_forget"][l], jnp.zeros((128 - 6,), F32)])[None, :] for l in range(DEPTH)]
    for k in ("ln1_g", "ln2_g", "g_out_fox", "g_out_dil", "conv_b", "cnorm_g", "cnorm_b"):
        W[k] = row(P[k])
    W["g_final"] = P["g_final"][None, :]
    return W


def kernel(x, ln1_g, w_in, b_forget, g_out_fox, g_out_dil, conv_w, conv_b, cnorm_g, cnorm_b, w_o, ln2_g, w_up, ffn_conv_w, ffn_conv_b, w_down, g_final, loss_target, m_ln1_g, m_w_in, m_b_forget, m_g_out_fox, m_g_out_dil, m_conv_w, m_conv_b, m_cnorm_g, m_cnorm_b, m_w_o, m_ln2_g, m_w_up, m_ffn_conv_w, m_ffn_conv_b, m_w_down, m_g_final, v_ln1_g, v_w_in, v_b_forget, v_g_out_fox, v_g_out_dil, v_conv_w, v_conv_b, v_cnorm_g, v_cnorm_b, v_w_o, v_ln2_g, v_w_up, v_ffn_conv_w, v_ffn_conv_b, v_w_down, v_g_final):
    P = dict(ln1_g=ln1_g, w_in=w_in, b_forget=b_forget, g_out_fox=g_out_fox, g_out_dil=g_out_dil, conv_w=conv_w, conv_b=conv_b,
             cnorm_g=cnorm_g, cnorm_b=cnorm_b, w_o=w_o, ln2_g=ln2_g, w_up=w_up, ffn_conv_w=ffn_conv_w, ffn_conv_b=ffn_conv_b,
             w_down=w_down, g_final=g_final)
    M = dict(ln1_g=m_ln1_g, w_in=m_w_in, b_forget=m_b_forget, g_out_fox=m_g_out_fox, g_out_dil=m_g_out_dil, conv_w=m_conv_w,
             conv_b=m_conv_b, cnorm_g=m_cnorm_g, cnorm_b=m_cnorm_b, w_o=m_w_o, ln2_g=m_ln2_g, w_up=m_w_up, ffn_conv_w=m_ffn_conv_w,
             ffn_conv_b=m_ffn_conv_b, w_down=m_w_down, g_final=m_g_final)
    V = dict(ln1_g=v_ln1_g, w_in=v_w_in, b_forget=v_b_forget, g_out_fox=v_g_out_fox, g_out_dil=v_g_out_dil, conv_w=v_conv_w,
             conv_b=v_conv_b, cnorm_g=v_cnorm_g, cnorm_b=v_cnorm_b, w_o=v_w_o, ln2_g=v_ln2_g, w_up=v_w_up, ffn_conv_w=v_ffn_conv_w,
             ffn_conv_b=v_ffn_conv_b, w_down=v_w_down, g_final=v_g_final)
    Bl, S, _ = x.shape
    T = Bl * S

    def shard(k, l):
        if k == "w_in":
            return _pad_w_in(P[k][l].astype(BF16))
        return P[k][l].astype(BF16) if k in _MATMUL_W else P[k][l]

    early = [("w_in", 0)] + [(k, l) for k in ("conv_w", "ffn_conv_w") for l in range(DEPTH)]
    late = [(k, 0) for k in ("w_o", "w_up", "w_down")] + [(k, 1) for k in _MATMUL_W]
    gathered = {k: [None] * DEPTH for k in _SHARDED}
    plan_early = _Plan([[shard(k, l)] for k, l in early], True)
    for (k, l), zone in zip(early, _exchange(plan_early, "gather_early")):
        gathered[k][l] = zone
    plan_late = _Plan([[shard(k, l)] for k, l in late], True)
    late_handle, late_token = _exchange_start(plan_late, "gather_late_start", plan_late.zones_with_own())
    W = _full_weights(P, gathered)
    W["ln1_g"][0] = W["ln1_g"][0] + late_token[0, 0]

    def late_weights(W, after):
        zones = _exchange_wait(plan_late, "gather_late_wait", late_handle, after)
        W = dict(W)
        for (k, l), zone in zip(late, zones):
            W[k] = list(W[k])
            W[k][l] = _kernel_ready(k, zone)
        return W

    def owner_block(G, l, k):
        if k == "w_in":
            blk = _unpad_w_in(G[k][l]).reshape(NDEV, D // NDEV, N_IN)
        elif k == "w_o":
            blk = G[k][l].reshape(NDEV, D // NDEV, D)
        elif k == "w_up":
            blk = G[k][l]
        elif k == "w_down":
            blk = G[k][l].reshape(NDEV, DFF // NDEV, D)
        elif k == "conv_w":
            blk = jnp.transpose(G[k][l].reshape(CONV_K, NDEV, CC // NDEV), (1, 0, 2))
        else:
            blk = G[k][l].reshape(NDEV, 3, FB)
        return blk.astype(GRAD_DTYPE)

    flying = []

    def start_scatter(tag, G, l, keys):
        plan = _Plan([[owner_block(G, l, k)] for k in keys], False)
        handle, token = _exchange_start(plan, "scatter_%s_start" % tag, plan.zones_with_own())
        flying.append((tag, l, keys, plan, handle))
        return token

    def on_layer_grads(l, G):
        return start_scatter("l1", G, l, _MATMUL_W) if l == 1 else start_scatter("l0_rest", G, l, ("w_in", "w_o"))

    def on_ffn_grads(l, G):
        return start_scatter("l0_ffn", G, l, ("w_up", "w_down")) if l == 0 else None

    loss, dx, G = _local_step(x.reshape(T, D), loss_target.reshape(T, D), W, S, late_weights, on_layer_grads, on_ffn_grads)

    parts = {}
    for tag, l, keys, plan, handle in flying:
        for k, zone in zip(keys, _exchange_wait(plan, "scatter_%s_wait" % tag, handle, dx)):
            parts[(k, l)] = zone
    plan_last = _Plan([[owner_block(G, l, k) for l in range(DEPTH)] for k in ("conv_w", "ffn_conv_w")], False)
    last = _exchange(plan_last, "scatter_last")
    small_parts_sharded = {"conv_w": last[0], "ffn_conv_w": last[1]}

    small = [jnp.stack(G[k]).reshape(-1) for k in _SMALL[:-1]] + [G["g_final"].reshape(-1), loss[0, :1]]
    sizes = [int(s.shape[0]) for s in small]
    flat = jnp.concatenate(small)
    n_small = -(-flat.shape[0] // 1024) * 1024
    flat = jnp.concatenate([flat, jnp.zeros((n_small - flat.shape[0],), F32)]).reshape(n_small // 128, 128)
    small_parts = _exchange(_Plan([[flat]], True), "gather_small")[0]

    out_g, out_d, out_m, out_v = {}, {}, {}, {}
    sg = _sum_parts(small_parts.reshape(NDEV, n_small // 128, 128), "sum_small").reshape(-1)
    off = 0
    summed = {}
    for k, n in zip(_SMALL + ("loss",), sizes):
        summed[k] = sg[off:off + n]
        off += n
    for k in _ORDER:
        shp = P[k].shape
        C = shp[-1]
        if k in _MATMUL_W:
            swap = k == "w_up"
            R0, Cw = (shp[2], shp[1]) if swap else (shp[1], shp[2])
            flat2 = lambda t: (jnp.swapaxes(t, 1, 2) if swap else t).reshape(DEPTH * R0, Cw)
            res = None
            for l in reversed(range(DEPTH)):
                res = _adamw(parts[(k, l)].reshape(NDEV, R0, Cw), flat2(P[k]), flat2(M[k]), flat2(V[k]),
                             "adamw_%s_l%d" % (k, l), row0=l * R0, prev=res)
            if swap:
                res = [jnp.swapaxes(t.reshape(DEPTH, R0, Cw), 1, 2) for t in res]
        else:
            pr = small_parts_sharded[k].reshape((NDEV, -1, C)) if k in _SHARDED else summed[k].reshape((1, -1, C))
            R = pr.shape[1]
            res = _adamw(pr, P[k].reshape(R, -1), M[k].reshape(R, -1), V[k].reshape(R, -1), "adamw_" + k)
        out_g[k], out_d[k], out_m[k], out_v[k] = (t.reshape(shp) for t in res)

    loss_out = summed["loss"].reshape(())
    grad_x = dx.reshape(Bl, S, D)
    return (loss_out, grad_x, *[out_g[k] for k in _ORDER], *[out_d[k] for k in _ORDER], *[out_m[k] for k in _ORDER],
            *[out_v[k] for k in _ORDER])
```

```python
import functools

import numpy as np
import jax
import jax.numpy as jnp
from jax import lax
from jax.experimental import pallas as pl
from jax.experimental.pallas import tpu as pltpu

F32 = jnp.float32
BF16 = jnp.bfloat16
GRAD_DTYPE = BF16
HIGHEST = lax.Precision.HIGHEST

D = 1024
DEPTH = 2
HD = 64
WA = 384
NPAIR = 3
CC = 256
CONV_K = 31
DFF = 2816
FB = 704
NJ = 4
NDEV = 8
EPS = 1e-6
NQKV = 2304
NREST = 768
NPAD = NQKV + NREST
O_FA, O_QB, N_IN = 1152, 1158, 2822
DILATION_PAIRS = ((128, 1), (512, 4), (2048, 16))
NEG = -1e30

ADAM_LR, ADAM_B1, ADAM_B2, ADAM_EPS, ADAM_WD, ADAM_STEP = 0.001, 0.9, 0.999, 1e-08, 0.01, 10

VMEM_LIMIT = 48 * 1024 * 1024
VMEM_LIMIT_BIG = 56 * 1024 * 1024


def _cp(*sem):
    return pltpu.CompilerParams(dimension_semantics=sem, vmem_limit_bytes=VMEM_LIMIT)


def _sds(shape, dtype):
    return jax.ShapeDtypeStruct(shape, dtype)


def _dot(a, b, prec=None):
    return jnp.dot(a, b, preferred_element_type=F32, precision=prec)


def _dot_nt(a, b, prec=None):
    return lax.dot_general(a, b, (((1,), (1,)), ((), ())), preferred_element_type=F32, precision=prec)


def _dot_tn(a, b):
    return lax.dot_general(a, b, (((0,), (0,)), ((), ())), preferred_element_type=F32)


def _sigmoid(z):
    return 0.5 * jnp.tanh(0.5 * z) + 0.5


def _rms_bwd(dh, x, g):
    r = lax.rsqrt(jnp.mean(x * x, axis=-1, keepdims=True) + EPS)
    xh = x * r
    dg = jnp.sum(dh * xh, axis=0, keepdims=True)
    dxh = dh * g
    dx = r * (dxh - xh * jnp.mean(dxh * xh, axis=-1, keepdims=True))
    return dx, dg


class _Plan:
    def __init__(self, groups, gather, slots=None, n_slots=None):
        self.groups, self.gather = groups, gather
        self.slots = slots or [list(range(len(g))) for g in groups]
        self.n_slots = n_slots or [len(g) for g in groups]
        self.flat = [a for g in groups for a in g]
        self.where = [(gi, s) for gi, g in enumerate(groups) for s in self.slots[gi]]
        self.zone_shapes = [_sds((NDEV, ns) + (g[0].shape if gather else g[0].shape[1:]), g[0].dtype)
                            for g, ns in zip(groups, self.n_slots)]

    def new_zones(self):
        return [lax.empty(z.shape, z.dtype) for z in self.zone_shapes]

    def me(self):
        x, y, c = lax.axis_index("x"), lax.axis_index("y"), lax.axis_index("c")
        return 4 * x + 2 * y + c

    def zones_with_own(self):
        me = self.me()
        zones = self.new_zones()
        for a, (gi, s) in enumerate(self.where):
            src = self.flat[a] if self.gather else lax.dynamic_index_in_dim(self.flat[a], me, 0, keepdims=False)
            zones[gi] = lax.dynamic_update_slice(zones[gi], src[None, None], (me, s) + (0,) * src.ndim)
        return zones

    def own_copies(self, ins, zones, sems):
        me = self.me()
        return [pltpu.make_async_copy(ins[a] if self.gather else ins[a].at[me], zones[gi].at[me, s], sems.at[a])
                for a, (gi, s) in enumerate(self.where)]

    def remote_copies(self, ins, zones, send_sems, recv_sems):
        x, y, c = lax.axis_index("x"), lax.axis_index("y"), lax.axis_index("c")
        me = 4 * x + 2 * y + c
        out = []
        for a, (gi, s) in enumerate(self.where):
            for k in range(1, NDEV):
                px, py, pc = x ^ ((k >> 2) & 1), y ^ ((k >> 1) & 1), c ^ (k & 1)
                out.append(pltpu.make_async_remote_copy(
                    src_ref=ins[a] if self.gather else ins[a].at[4 * px + 2 * py + pc], dst_ref=zones[gi].at[me, s],
                    send_sem=send_sems.at[a * (NDEV - 1) + k - 1], recv_sem=recv_sems.at[a * (NDEV - 1) + k - 1],
                    device_id=(px, py, pc), device_id_type=pl.DeviceIdType.MESH))
        return out


_ANY = pl.BlockSpec(memory_space=pl.ANY)
_HBM = pl.BlockSpec(memory_space=pltpu.HBM)
_SEM = pl.BlockSpec(memory_space=pltpu.SEMAPHORE)


def _exchange(plan, name, zones=None):
    n, nz = len(plan.flat), len(plan.groups)
    zones = zones or plan.new_zones()

    def body(*refs):
        ins, zin = refs[:n], refs[n:n + nz]
        send_sems, recv_sems, local_sems = refs[n + 2 * nz:]
        copies = plan.own_copies(ins, zin, local_sems) + plan.remote_copies(ins, zin, send_sems, recv_sems)
        for cp in copies:
            cp.start()
        for cp in copies:
            cp.wait()

    return pl.pallas_call(
        body, name=name, out_shape=plan.zone_shapes, in_specs=[_ANY] * (n + nz), out_specs=[_ANY] * nz,
        input_output_aliases={n + g: g for g in range(nz)},
        scratch_shapes=[pltpu.SemaphoreType.DMA((n * (NDEV - 1),)), pltpu.SemaphoreType.DMA((n * (NDEV - 1),)),
                        pltpu.SemaphoreType.DMA((n,))],
        compiler_params=pltpu.CompilerParams(has_side_effects=True),
    )(*plan.flat, *zones)


def _exchange_start(plan, name, zones):
    n, nz = len(plan.flat), len(plan.groups)
    hbm = lambda a: pltpu.HBM(a.shape, a.dtype)

    def body(*refs):
        ins, zin = refs[:n], refs[n:n + nz]
        send_sems, recv_sems = refs[n + nz], refs[n + nz + 1]
        token = refs[-1]
        for cp in plan.remote_copies(ins, zin, send_sems, recv_sems):
            cp.start()
        token[...] = jnp.zeros_like(token)

    outs = pl.pallas_call(
        body, name=name,
        out_shape=(pltpu.SemaphoreType.DMA((n * (NDEV - 1),)), pltpu.SemaphoreType.DMA((n * (NDEV - 1),)),
                   *[hbm(a) for a in plan.flat], *[hbm(z) for z in plan.zone_shapes], _sds((8, 128), F32)),
        in_specs=[_HBM] * (n + nz),
        out_specs=(_SEM, _SEM, *[_HBM] * (n + nz), pl.BlockSpec(memory_space=pltpu.VMEM)),
        input_output_aliases={i: 2 + i for i in range(n + nz)},
        compiler_params=pltpu.CompilerParams(has_side_effects=pltpu.SideEffectType.DATAFLOW_SIDE_EFFECTING),
    )(*[pltpu.with_memory_space_constraint(a, pltpu.HBM) for a in plan.flat],
      *[pltpu.with_memory_space_constraint(z, pltpu.HBM) for z in zones])
    return outs[:-1], outs[-1]


def _exchange_wait(plan, name, handle, after):
    n, nz = len(plan.flat), len(plan.groups)
    send_sems, recv_sems = handle[0], handle[1]
    thru = handle[2:]
    hbm = lambda a: pltpu.HBM(a.shape, a.dtype)

    def body(*refs):
        ins, zin = refs[:n], refs[n:n + nz]
        ssem, rsem = refs[n + nz], refs[n + nz + 1]
        for cp in plan.remote_copies(ins, zin, ssem, rsem):
            cp.wait_send()
            cp.wait_recv()

    outs = pl.pallas_call(
        body, name=name, out_shape=tuple(hbm(a) for a in thru),
        in_specs=[_HBM] * (n + nz) + [_SEM, _SEM, _ANY], out_specs=tuple([_HBM] * (n + nz)),
        input_output_aliases={i: i for i in range(n + nz)},
        compiler_params=pltpu.CompilerParams(has_side_effects=pltpu.SideEffectType.DATAFLOW_SIDE_EFFECTING),
    )(*thru, send_sems, recv_sems, after)
    return list(outs[n:])


def _sum_parts(parts, name):
    npart, R, C = parts.shape

    def body(p_ref, o_ref):
        s = p_ref[0]
        for k in range(1, npart):
            s = s + p_ref[k]
        o_ref[...] = s

    return pl.pallas_call(body, name=name, out_shape=_sds((R, C), F32))(parts)


def _rms_fwd(x, g):
    T = x.shape[0]
    tm = min(512, T)

    def body(x_ref, g_ref, o_ref):
        xv = x_ref[...]
        r = lax.rsqrt(jnp.mean(xv * xv, axis=-1, keepdims=True) + EPS)
        o_ref[...] = (xv * r * g_ref[...]).astype(BF16)

    return pl.pallas_call(
        body, name="rms_fwd", grid=(T // tm,), out_shape=_sds((T, D), BF16),
        in_specs=[pl.BlockSpec((tm, D), lambda i: (i, 0)), pl.BlockSpec((1, D), lambda i: (0, 0))],
        out_specs=pl.BlockSpec((tm, D), lambda i: (i, 0)), compiler_params=_cp("parallel"))(x, g)


def _mm_in(h, w):
    T = h.shape[0]
    tm = min(512, T)

    def body(h_ref, w_ref, q_ref, r_ref):
        hv = h_ref[...]
        q_ref[...] = _dot(hv, w_ref[:, :NQKV]).astype(BF16)
        r_ref[...] = _dot(hv, w_ref[:, NQKV:])

    return pl.pallas_call(
        body, name="mm_in", grid=(T // tm,), out_shape=[_sds((T, NQKV), BF16), _sds((T, NREST), F32)],
        in_specs=[pl.BlockSpec((tm, D), lambda i: (i, 0)), pl.BlockSpec((D, NPAD), lambda i: (0, 0))],
        out_specs=[pl.BlockSpec((tm, NQKV), lambda i: (i, 0)), pl.BlockSpec((tm, NREST), lambda i: (i, 0))],
        compiler_params=_cp("parallel"))(h, w)


AUG_Q1, AUG_K1 = 3, 0


def _aug_lane0(h):
    return HD * (1 - h % 2)


def _aug_tables():
    selq = np.zeros((NPAIR, 3 * 128, 128), np.float32)
    selk = np.zeros((NPAIR, 3 * 128, 128), np.float32)
    oneq = np.zeros((NPAIR, 1, 128), np.float32)
    onek = np.zeros((NPAIR, 1, 128), np.float32)
    for h in range(2 * NPAIR):
        p, l0 = h // 2, _aug_lane0(h)
        for piece in range(3):
            selq[p, 128 * piece + h, l0 + piece] = 1.0
            selk[p, 128 * piece + h, l0 + 3 + piece] = -1.0
            oneq[p, 0, l0 + AUG_Q1 + piece] = 1.0
            onek[p, 0, l0 + AUG_K1 + piece] = 1.0
    return [jnp.asarray(a, BF16) for a in (selq, selk, oneq, onek)]


def _gate_fwd(p_rest, bf, S, TB):
    T = p_rest.shape[0]
    B, nb = T // S, S // TB
    selq, selk, oneq, onek = _aug_tables()

    def body(z_ref, b_ref, sq_ref, sk_ref, oq_ref, ok_ref, aq_ref, ak_ref):
        tri = (lax.broadcasted_iota(jnp.int32, (TB, TB), 0) >= lax.broadcasted_iota(jnp.int32, (TB, TB), 1)).astype(F32)
        carry = jnp.zeros((1, 128), F32)
        for j in range(nb):
            rows = slice(j * TB, (j + 1) * TB)
            z = z_ref[rows, :] + b_ref[...]
            lf = jnp.minimum(z, 0.0) - jnp.log(1.0 + jnp.exp(-jnp.abs(z)))
            cb = _dot(tri, lf, HIGHEST) + carry
            carry = cb[TB - 1:TB, :]
            hi = cb.astype(BF16)
            mid = (cb - hi.astype(F32)).astype(BF16)
            lo = (cb - hi.astype(F32) - mid.astype(F32)).astype(BF16)
            pieces = jnp.concatenate([hi, mid, lo], axis=1)
            for p in range(NPAIR):
                aq_ref[p, rows, :] = (_dot(pieces, sq_ref[p]) + oq_ref[p].astype(F32)).astype(BF16)
                ak_ref[p, rows, :] = (_dot(pieces, sk_ref[p]) + ok_ref[p].astype(F32)).astype(BF16)

    full = lambda a: pl.BlockSpec(a.shape, lambda b: (0,) * a.ndim)
    return pl.pallas_call(
        body, name="gate_fwd", grid=(B,), out_shape=[_sds((NPAIR, T, 128), BF16), _sds((NPAIR, T, 128), BF16)],
        in_specs=[pl.BlockSpec((S, 128), lambda b: (b, 4)), pl.BlockSpec((1, 128), lambda b: (0, 0)),
                  full(selq), full(selk), full(oneq), full(onek)],
        out_specs=[pl.BlockSpec((NPAIR, S, 128), lambda b: (0, b, 0)), pl.BlockSpec((NPAIR, S, 128), lambda b: (0, b, 0))],
        compiler_params=_cp("parallel"))(p_rest, bf, selq, selk, oneq, onek)


def _bias_table(S, n_key, n_query, fox):
    unit = min(n_key, n_query)
    d_min = -(n_query // unit - 1)
    d = np.arange(d_min, S // unit)[:, None, None]
    delta = d * unit + np.arange(n_query)[None, None, :] - np.arange(n_key)[None, :, None]
    if fox:
        mult = (delta >= 0).astype(np.float64)
    else:
        mult = np.zeros(delta.shape)
        for w, dil in DILATION_PAIRS:
            mult += (delta >= 0) & (delta % dil == 0) & (delta <= w)
    with np.errstate(divide="ignore"):
        tab = np.where(mult > 0, np.log(np.maximum(mult, 1e-30)), NEG)
    tab = np.concatenate([tab, np.full((1, n_key, n_query), NEG)], axis=0)
    return jnp.asarray(tab, F32)


def _head_masks():
    lane = lax.broadcasted_iota(jnp.int32, (1, 128), 1)
    return [lane < HD, lane >= HD]


def _transpose_bf16(a):
    return a.astype(F32).T.astype(BF16)


def _col_reduce(x, op):
    while x.shape[0] > 8:
        half = x.shape[0] // 2
        x = op(x[:half], x[half:])
    return jnp.max(x, axis=0, keepdims=True) if op is jnp.maximum else jnp.sum(x, axis=0, keepdims=True)


def _attn_fwd(p_qkv, col0, tab, S, aq=None, ak=None, name="attn"):
    T = p_qkv.shape[0]
    n_tab, TB, TQ = tab.shape
    nb, nq, r = S // TB, S // TQ, TQ // TB
    B = T // S
    fox = aq is not None

    def body(*refs):
        if fox:
            q_ref, k_ref, v_ref, tab_ref, aq_ref, ak_ref, o_ref, l_ref, kat_ref, vta_scr, acc_scr, st_scr = refs
        else:
            q_ref, k_ref, v_ref, tab_ref, o_ref, l_ref, kat_ref, vta_scr, acc_scr, st_scr = refs
        i = pl.program_id(2)
        hms = _head_masks()
        top = lax.broadcasted_iota(jnp.int32, (128, 1), 0) < HD
        last_key_block = r * i + r - 1

        @pl.when(i == 0)
        def _():
            for jj in range(nb):
                rows = slice(jj * TB, (jj + 1) * TB)
                vt = _transpose_bf16(v_ref[rows, :])
                for hh in range(2):
                    vta_scr[hh, jj] = jnp.where(top == (hh == 0), vt, jnp.ones_like(vt))
                    spare_k = ak_ref[0, rows, :] if fox else jnp.zeros((TB, 128), BF16)
                    kat_ref[0, 0, hh, jj] = _transpose_bf16(jnp.where(hms[hh], k_ref[rows, :], spare_k))

        q2 = q_ref[...] * 0.125
        spare_q = aq_ref[0] if fox else jnp.zeros_like(q2)
        qa = [jnp.where(hm, q2, spare_q) for hm in hms]
        acc_scr[...] = jnp.zeros_like(acc_scr)

        def scores(j):
            jc = jnp.minimum(j, nb - 1)
            rows = pl.ds(pl.multiple_of(jc * TB, TB), TB)
            k2 = k_ref[rows, :]
            bias = tab_ref[jnp.where(j <= last_key_block, r * i - j + (r - 1), n_tab - 1)]
            return [_dot_nt(jnp.where(hms[hh], k2, ak_ref[0, rows, :]) if fox else k2, qa[hh]) + bias for hh in range(2)]

        def absorb(j, sts, stats):
            jc = jnp.minimum(j, nb - 1)
            alphas, pts = [], []
            for hh in range(2):
                m_old = stats[2 * hh]
                m_new = jnp.maximum(m_old, _col_reduce(sts[hh], jnp.maximum))
                pts.append(jnp.exp(sts[hh] - m_new).astype(BF16))
                alphas.append(jnp.exp(m_old - m_new))
                stats[2 * hh] = m_new
            pvs = [_dot(vta_scr[hh, jc], pts[hh]) for hh in range(2)]
            for hh in range(2):
                half = slice(HD * hh, HD * (hh + 1))
                ones_row = HD * (1 - hh)
                acc_scr[half, :] = alphas[hh] * acc_scr[half, :] + pvs[hh][half]
                stats[2 * hh + 1] = alphas[hh] * stats[2 * hh + 1] + pvs[hh][ones_row:ones_row + 1]

        def kv_pair(jj, carry):
            stats = list(carry)
            j0 = 2 * jj
            st0 = [st_scr[hh] for hh in range(2)]
            st1 = scores(j0 + 1)
            absorb(j0, st0, stats)
            nxt = scores(j0 + 2)
            for hh in range(2):
                st_scr[hh] = nxt[hh]
            absorb(j0 + 1, st1, stats)
            return tuple(stats)

        first = scores(0)
        for hh in range(2):
            st_scr[hh] = first[hh]
        row = lambda v: jnp.full((1, TQ), v, F32)
        m0, l0, m1, l1 = lax.fori_loop(0, (last_key_block + 2) // 2, kv_pair, (row(NEG), row(0.0), row(NEG), row(0.0)))
        o_ref[...] = (acc_scr[...] * jnp.where(top, 1.0 / l0, 1.0 / l1)).T
        for hh, lse in enumerate((m0 + jnp.log(l0), m1 + jnp.log(l1))):
            for t in range(r):
                l_ref[t, hh] = lse[:, t * TB:(t + 1) * TB]

    in_specs = [pl.BlockSpec((TQ, 128), lambda b, p, i: (b * nq + i, col0 + p)),
                pl.BlockSpec((S, 128), lambda b, p, i: (b, col0 + NPAIR + p)),
                pl.BlockSpec((S, 128), lambda b, p, i: (b, col0 + 2 * NPAIR + p)),
                pl.BlockSpec(tab.shape, lambda b, p, i: (0, 0, 0))]
    args = [p_qkv, p_qkv, p_qkv, tab]
    if fox:
        in_specs += [pl.BlockSpec((1, TQ, 128), lambda b, p, i: (p, b * nq + i, 0)),
                     pl.BlockSpec((1, S, 128), lambda b, p, i: (p, b, 0))]
        args += [aq, ak]
    return pl.pallas_call(
        body, name=name, grid=(B, NPAIR, nq),
        out_shape=[_sds((T, WA), F32), _sds((T // TB, 2 * NPAIR, 1, TB), F32), _sds((B, NPAIR, 2, nb, 128, TB), BF16)],
        in_specs=in_specs,
        out_specs=[pl.BlockSpec((TQ, 128), lambda b, p, i: (b * nq + i, p)),
                   pl.BlockSpec((r, 2, 1, TB), lambda b, p, i: (b * nq + i, p, 0, 0)),
                   pl.BlockSpec((1, 1, 2, nb, 128, TB), lambda b, p, i: (b, p, 0, 0, 0, 0))],
        scratch_shapes=[pltpu.VMEM((2, nb, 128, TB), BF16), pltpu.VMEM((128, TQ), F32), pltpu.VMEM((2, TB, TQ), F32)],
        compiler_params=_cp("parallel", "parallel", "arbitrary"))(*args)


def _phase_copies(src, phases, length):
    for r in range(1, 8):
        phases[r, 0:length - 8, :] = src[pl.ds(r, length - 8), :]


def _tap(src, phases, offset, rows):
    r = offset % 8
    return src[pl.ds(offset, rows), :] if r == 0 else phases[r, pl.ds(offset - r, rows), :]


def _conv_taps(src, phases, w_ref, first_row, rows):
    acc = w_ref[0:1, :] * _tap(src, phases, first_row, rows)
    for k in range(1, CONV_K):
        acc = acc + w_ref[k:k + 1, :] * _tap(src, phases, first_row + k, rows)
    return acc


def _mix_fwd(o_a, o_b, p_rest, g_a, g_b, cw, cb, ng, nbias, S):
    T = o_a.shape[0]
    tm = min(256, S)
    nps = S // tm
    HALO = 32

    def body(oa_ref, ob_ref, pt_ref, ph_ref, ga_ref, gb_ref, cw_ref, cb_ref, ng_ref, nb_ref, y_ref, cv_ref, buf, phases):
        i = pl.program_id(0)
        first = (i % nps) == 0
        for o_ref, g_ref, c0 in ((oa_ref, ga_ref, 0), (ob_ref, gb_ref, WA)):
            o = o_ref[...]
            r = lax.rsqrt(jnp.mean(o * o, axis=-1, keepdims=True) + EPS)
            y_ref[:, c0:c0 + WA] = (o * r * g_ref[...]).astype(BF16)
        ph = jnp.where(first, 0.0, ph_ref[...])
        buf[0:HALO, :] = ph[:, :CC] * _sigmoid(ph[:, CC:])
        pt = pt_ref[...]
        buf[HALO:HALO + tm, :] = pt[:, :CC] * _sigmoid(pt[:, CC:])
        _phase_copies(buf, phases, tm + HALO)
        cv = _conv_taps(buf, phases, cw_ref, HALO - CONV_K + 1, tm) + cb_ref[...]
        cv_ref[...] = cv
        xc = cv - jnp.mean(cv, axis=-1, keepdims=True)
        lo = xc * lax.rsqrt(jnp.mean(xc * xc, axis=-1, keepdims=True) + EPS) * ng_ref[...] + nb_ref[...]
        y_ref[:, 2 * WA:] = (lo * _sigmoid(lo)).astype(BF16)

    row = lambda w: pl.BlockSpec((1, w), lambda i: (0, 0))
    return pl.pallas_call(
        body, name="mix_fwd", grid=(T // tm,), out_shape=[_sds((T, D), BF16), _sds((T, CC), F32)],
        in_specs=[pl.BlockSpec((tm, WA), lambda i: (i, 0)), pl.BlockSpec((tm, WA), lambda i: (i, 0)),
                  pl.BlockSpec((tm, 2 * CC), lambda i: (i, 0)),
                  pl.BlockSpec((HALO, 2 * CC), lambda i: (jnp.maximum(i * (tm // HALO) - 1, 0), 0)),
                  row(WA), row(WA), pl.BlockSpec((32, CC), lambda i: (0, 0)), row(CC), row(CC), row(CC)],
        out_specs=[pl.BlockSpec((tm, D), lambda i: (i, 0)), pl.BlockSpec((tm, CC), lambda i: (i, 0))],
        scratch_shapes=[pltpu.VMEM((tm + HALO, CC), F32), pltpu.VMEM((8, tm + HALO, CC), F32)],
        compiler_params=_cp("parallel"))(o_a, o_b, p_rest, p_rest, g_a, g_b, cw, cb, ng, nbias)


def _mm_res(a, w, resid, g, name):
    nk, T, kb = a.shape
    tm = min(512, T)

    def body(a_ref, w_ref, r_ref, g_ref, x_ref, h_ref):
        xv = r_ref[...]
        for k in range(nk):
            xv = xv + _dot(a_ref[k], w_ref[k])
        x_ref[...] = xv
        r = lax.rsqrt(jnp.mean(xv * xv, axis=-1, keepdims=True) + EPS)
        h_ref[...] = (xv * r * g_ref[...]).astype(BF16)

    return pl.pallas_call(
        body, name=name, grid=(T // tm,), out_shape=[_sds((T, D), F32), _sds((T, D), BF16)],
        in_specs=[pl.BlockSpec((nk, tm, kb), lambda i: (0, i, 0)), pl.BlockSpec((nk, kb, D), lambda i: (0, 0, 0)),
                  pl.BlockSpec((tm, D), lambda i: (i, 0)), pl.BlockSpec((1, D), lambda i: (0, 0))],
        out_specs=[pl.BlockSpec((tm, D), lambda i: (i, 0)), pl.BlockSpec((tm, D), lambda i: (i, 0))],
        compiler_params=_cp("parallel"))(a, w, resid, g)


def _up_fwd(h2, w_up, fcw, fcb, S):
    T = h2.shape[0]
    tm = min(512, S)
    nps = S // tm

    def body(h_ref, hh_ref, w_ref, cw_ref, cb_ref, u_ref, c_ref, hid_ref, buf, hbuf):
        i = pl.program_id(1)
        first = (i % nps) == 0
        hbuf[0:tm, :] = h_ref[...]
        hbuf[tm:tm + 8, :] = jnp.where(first, jnp.zeros_like(hh_ref[...]), hh_ref[...])
        c = []
        for half in range(2):
            ua = _dot(hbuf[...], w_ref[half, 0])
            u = ua[0:tm]
            u_ref[half, 0] = u
            buf[half, 0:8, :] = ua[tm:tm + 8]
            buf[half, 8:8 + tm, :] = u
            cw = cw_ref[half, 0]
            c.append(cb_ref[half, 0] + cw[0:1] * buf[half, pl.ds(6, tm), :] + cw[1:2] * buf[half, pl.ds(7, tm), :] + cw[2:3] * u)
        for half in range(2):
            c_ref[half, 0] = c[half]
        hid_ref[0] = (c[0] * _sigmoid(c[0]) * c[1]).astype(BF16)

    return pl.pallas_call(
        body, name="up_fwd", grid=(NJ, T // tm),
        out_shape=[_sds((2, NJ, T, FB), F32), _sds((2, NJ, T, FB), F32), _sds((NJ, T, FB), BF16)],
        in_specs=[pl.BlockSpec((tm, D), lambda j, i: (i, 0)),
                  pl.BlockSpec((8, D), lambda j, i: (jnp.maximum(i * (tm // 8) - 1, 0), 0)),
                  pl.BlockSpec((2, 1, D, FB), lambda j, i: (0, j, 0, 0)),
                  pl.BlockSpec((2, 1, 3, FB), lambda j, i: (0, j, 0, 0)),
                  pl.BlockSpec((2, 1, 1, FB), lambda j, i: (0, j, 0, 0))],
        out_specs=[pl.BlockSpec((2, 1, tm, FB), lambda j, i: (0, j, i, 0)), pl.BlockSpec((2, 1, tm, FB), lambda j, i: (0, j, i, 0)),
                   pl.BlockSpec((1, tm, FB), lambda j, i: (j, i, 0))],
        scratch_shapes=[pltpu.VMEM((2, tm + 8, FB), F32), pltpu.VMEM((tm + 8, D), BF16)],
        compiler_params=_cp("parallel", "parallel"))(h2, h2, w_up, fcw, fcb)


def _final_loss(x, tgt, g):
    T = x.shape[0]
    tm = min(512, T)

    def body(x_ref, t_ref, g_ref, dx_ref, loss_ref, dg_ref):
        @pl.when(pl.program_id(0) == 0)
        def _():
            loss_ref[...] = jnp.zeros_like(loss_ref)
            dg_ref[...] = jnp.zeros_like(dg_ref)

        xv, gv = x_ref[...], g_ref[...]
        r = lax.rsqrt(jnp.mean(xv * xv, axis=-1, keepdims=True) + EPS)
        e = xv * r * gv - t_ref[...]
        loss_ref[...] += 0.5 * jnp.sum(jnp.mean(e * e, axis=-1, keepdims=True), axis=0, keepdims=True)
        dx, dg = _rms_bwd(e * (1.0 / D), xv, gv)
        dx_ref[...] = dx
        dg_ref[...] += dg

    return pl.pallas_call(
        body, name="final_loss", grid=(T // tm,), out_shape=[_sds((T, D), F32), _sds((1, 128), F32), _sds((1, D), F32)],
        in_specs=[pl.BlockSpec((tm, D), lambda i: (i, 0)), pl.BlockSpec((tm, D), lambda i: (i, 0)), pl.BlockSpec((1, D), lambda i: (0, 0))],
        out_specs=[pl.BlockSpec((tm, D), lambda i: (i, 0)), pl.BlockSpec((1, 128), lambda i: (0, 0)), pl.BlockSpec((1, D), lambda i: (0, 0))],
        compiler_params=_cp("arbitrary"))(x, tgt, g)


def _down_bwd(dx2, w_down, u, c, fcw, S):
    T = dx2.shape[0]
    tm = min(512, S)
    nps = S // tm
    nblk8 = T // 8

    def body(dx_ref, dxn_ref, wd_ref, ut_ref, ct_ref, cn_ref, cw_ref, du_ref, dcw_ref, dxb, dcbuf, dsh):
        i = pl.program_id(1)
        last = (i % nps) == nps - 1

        @pl.when(i == 0)
        def _():
            dcw_ref[...] = jnp.zeros_like(dcw_ref)

        dxb[0:tm, :] = dx_ref[...].astype(BF16)
        dxb[tm:tm + 8, :] = jnp.where(last, 0.0, dxn_ref[...]).astype(BF16)
        dh = _dot_nt(dxb[...], wd_ref[0])
        ce = [jnp.concatenate([ct_ref[half, 0], cn_ref[half, 0]], axis=0) for half in range(2)]
        sg = _sigmoid(ce[0])
        dc = [dh * ce[1] * (sg * (1.0 + ce[0] * (1.0 - sg))), dh * (ce[0] * sg)]
        for half in range(2):
            cw = cw_ref[half, 0]
            dcbuf[...] = dc[half]
            for k in range(2):
                dsh[k] = dcbuf[pl.ds(k + 1, tm), :]
            ahead = [dc[half][0:tm], dsh[0], dsh[1]]
            du_ref[half, 0] = (cw[2:3] * ahead[0] + cw[1:2] * ahead[1] + cw[0:1] * ahead[2]).astype(BF16)
            uv = ut_ref[half, 0]
            for k in range(3):
                dcw_ref[half, 0, k:k + 1, :] += jnp.sum(ahead[2 - k] * uv, axis=0, keepdims=True)
            dcw_ref[half, 0, 3:4, :] += jnp.sum(ahead[0], axis=0, keepdims=True)

    ublk = lambda rows, imap: pl.BlockSpec((2, 1, rows, FB), imap)
    return pl.pallas_call(
        body, name="down_bwd", grid=(NJ, T // tm), out_shape=[_sds((2, NJ, T, FB), BF16), _sds((2, NJ, 8, FB), F32)],
        in_specs=[pl.BlockSpec((tm, D), lambda j, i: (i, 0)),
                  pl.BlockSpec((8, D), lambda j, i: (jnp.minimum((i + 1) * (tm // 8), nblk8 - 1), 0)),
                  pl.BlockSpec((1, FB, D), lambda j, i: (j, 0, 0)),
                  ublk(tm, lambda j, i: (0, j, i, 0)),
                  ublk(tm, lambda j, i: (0, j, i, 0)),
                  ublk(8, lambda j, i: (0, j, jnp.minimum((i + 1) * (tm // 8), nblk8 - 1), 0)),
                  pl.BlockSpec((2, 1, 3, FB), lambda j, i: (0, j, 0, 0))],
        out_specs=[ublk(tm, lambda j, i: (0, j, i, 0)), pl.BlockSpec((2, 1, 8, FB), lambda j, i: (0, j, 0, 0))],
        scratch_shapes=[pltpu.VMEM((tm + 8, D), BF16), pltpu.VMEM((tm + 8, FB), F32), pltpu.VMEM((2, tm, FB), F32)],
        compiler_params=_cp("parallel", "arbitrary"))(dx2, dx2, w_down, u, c, c, fcw)


def _wgrad(a, b, tn, name, ga=1, gb=1):
    na, T, ka = a.shape
    nb, _, kb = b.shape
    tk = min(1024, T)
    nt = T // tk

    def body(a_ref, b_ref, o_ref, acc):
        t = pl.program_id(3)

        @pl.when(t == 0)
        def _():
            acc[...] = jnp.zeros_like(acc)

        bs = [b_ref[ib].astype(BF16) for ib in range(gb)]
        for ia in range(ga):
            av = a_ref[ia]
            for ib in range(gb):
                acc[ia, ib] += _dot_tn(av, bs[ib])

        @pl.when(t == nt - 1)
        def _():
            o_ref[...] = acc[...].astype(GRAD_DTYPE)

    return pl.pallas_call(
        body, name=name, grid=(na // ga, nb // gb, kb // tn, nt), out_shape=_sds((na, nb, ka, kb), GRAD_DTYPE),
        in_specs=[pl.BlockSpec((ga, tk, ka), lambda ia, ib, n, t: (ia, t, 0)), pl.BlockSpec((gb, tk, tn), lambda ia, ib, n, t: (ib, t, n))],
        out_specs=pl.BlockSpec((ga, gb, ka, tn), lambda ia, ib, n, t: (ia, ib, 0, n)),
        scratch_shapes=[pltpu.VMEM((ga, gb, ka, tn), F32)],
        compiler_params=_cp("parallel", "parallel", "parallel", "arbitrary"))(a, b)


def _dx_rms(dy, w, x, g, dres, blocked, name):
    T = x.shape[0]
    tm = min(256 if blocked else 512, T)

    def body(dy_ref, w_ref, x_ref, g_ref, r_ref, dx_ref, dg_ref):
        @pl.when(pl.program_id(0) == 0)
        def _():
            dg_ref[...] = jnp.zeros_like(dg_ref)

        if blocked:
            dh = _dot_nt(dy_ref[0, 0], w_ref[0, 0])
            for half, k in [(h, k) for k in range(NJ) for h in range(2)][1:]:
                dh = dh + _dot_nt(dy_ref[half, k], w_ref[half, k])
        else:
            dh = _dot_nt(dy_ref[...], w_ref[...])
        dx, dg = _rms_bwd(dh, x_ref[...], g_ref[...])
        dx_ref[...] = r_ref[...] + dx
        dg_ref[...] += dg

    if blocked:
        dy_spec = pl.BlockSpec((2, NJ, tm, FB), lambda i: (0, 0, i, 0))
        w_spec = pl.BlockSpec((2, NJ, D, FB), lambda i: (0, 0, 0, 0))
    else:
        dy_spec = pl.BlockSpec((tm, dy.shape[1]), lambda i: (i, 0))
        w_spec = pl.BlockSpec(w.shape, lambda i: (0, 0))
    tile = pl.BlockSpec((tm, D), lambda i: (i, 0))
    return pl.pallas_call(
        body, name=name, grid=(T // tm,), out_shape=[_sds((T, D), F32), _sds((1, D), F32)],
        in_specs=[dy_spec, w_spec, tile, pl.BlockSpec((1, D), lambda i: (0, 0)), tile],
        out_specs=[tile, pl.BlockSpec((1, D), lambda i: (0, 0))],
        compiler_params=pltpu.CompilerParams(dimension_semantics=("arbitrary",), vmem_limit_bytes=VMEM_LIMIT_BIG))(dy, w, x, g, dres)


def _mm_nt(a, w, name):
    T = a.shape[0]
    tm = min(512, T)

    def body(a_ref, w_ref, o_ref):
        o_ref[...] = _dot_nt(a_ref[...].astype(BF16), w_ref[...])

    return pl.pallas_call(
        body, name=name, grid=(T // tm,), out_shape=_sds((T, D), F32),
        in_specs=[pl.BlockSpec((tm, D), lambda i: (i, 0)), pl.BlockSpec((D, D), lambda i: (0, 0))],
        out_specs=pl.BlockSpec((tm, D), lambda i: (i, 0)), compiler_params=_cp("parallel"))(a, w)


def _mix_bwd(dy, o_a, o_b, p_rest, cv, g_a, g_b, cw, ng, nbias, S):
    T = dy.shape[0]
    tm = min(256, S)
    nps = S // tm
    HALO = 32
    nblkh = T // HALO
    RE = tm + HALO

    def body(dy_ref, dyn_ref, oa_ref, ob_ref, pt_ref, cvt_ref, cvn_ref, ga_ref, gb_ref, cw_ref, ng_ref, nb_ref,
             doa_ref, dda_ref, dob_ref, ddb_ref, dg_ref, dgn_ref, dcw_ref, dcn_ref, dvbuf, dph):
        i = pl.program_id(0)
        last = (i % nps) == nps - 1

        @pl.when(i == 0)
        def _():
            dgn_ref[...] = jnp.zeros_like(dgn_ref)
            dcw_ref[...] = jnp.zeros_like(dcw_ref)
            dcn_ref[...] = jnp.zeros_like(dcn_ref)

        head_of = (lax.broadcasted_iota(jnp.int32, (8, WA), 1) // HD == lax.broadcasted_iota(jnp.int32, (8, WA), 0)).astype(F32)
        for n, (o_ref, g_ref, do_ref, dd_ref) in enumerate(((oa_ref, ga_ref, doa_ref, dda_ref), (ob_ref, gb_ref, dob_ref, ddb_ref))):
            o = o_ref[...]
            do, dg = _rms_bwd(dy_ref[:, n * WA:(n + 1) * WA], o, g_ref[...])
            dob = do.astype(BF16)
            do_ref[...] = dob
            ddt = _dot_nt(head_of, dob.astype(F32) * o, HIGHEST)
            for hd in range(8):
                dd_ref[0, hd] = ddt[hd:hd + 1, :]
            dgn_ref[n:n + 1, :] += dg

        pt = pt_ref[...]
        gv, sgg = pt[:, :CC], _sigmoid(pt[:, CC:])
        glu = gv * sgg
        cv = jnp.concatenate([cvt_ref[...], cvn_ref[...]], axis=0)
        xc = cv - jnp.mean(cv, axis=-1, keepdims=True)
        r = lax.rsqrt(jnp.mean(xc * xc, axis=-1, keepdims=True) + EPS)
        xh = xc * r
        lo = xh * ng_ref[...] + nb_ref[...]
        sg = _sigmoid(lo)
        dyc = jnp.concatenate([dy_ref[:, 2 * WA:], jnp.where(last, 0.0, dyn_ref[...])], axis=0)
        dlo = dyc * (sg * (1.0 + lo * (1.0 - sg)))
        dcn_ref[0:1, :] += jnp.sum(dlo[0:tm] * xh[0:tm], axis=0, keepdims=True)
        dcn_ref[1:2, :] += jnp.sum(dlo[0:tm], axis=0, keepdims=True)
        dxh = dlo * ng_ref[...]
        dcv = r * (dxh - jnp.mean(dxh, axis=-1, keepdims=True) - xh * jnp.mean(dxh * xh, axis=-1, keepdims=True))
        dvbuf[...] = dcv
        _phase_copies(dvbuf, dph, RE)
        dct = dcv[0:tm]
        dcw_ref[CONV_K:CONV_K + 1, :] += jnp.sum(dct, axis=0, keepdims=True)
        dglu = jnp.zeros((tm, CC), F32)
        for k in range(CONV_K):
            ahead = _tap(dvbuf, dph, CONV_K - 1 - k, tm)
            dcw_ref[k:k + 1, :] += jnp.sum(ahead * glu, axis=0, keepdims=True)
            dglu = dglu + cw_ref[k:k + 1, :] * ahead
        dg_ref[:, :CC] = (dglu * sgg).astype(BF16)
        dg_ref[:, CC:] = (dglu * gv * sgg * (1.0 - sgg)).astype(BF16)

    row = lambda w: pl.BlockSpec((1, w), lambda i: (0, 0))
    tile = lambda w: pl.BlockSpec((tm, w), lambda i: (i, 0))
    nxt = lambda i: jnp.minimum((i + 1) * (tm // HALO), nblkh - 1)
    const = lambda r, w: pl.BlockSpec((r, w), lambda i: (0, 0))
    ddrow = pl.BlockSpec((1, 8, 1, tm), lambda i: (i, 0, 0, 0))
    return pl.pallas_call(
        body, name="mix_bwd", grid=(T // tm,),
        out_shape=[_sds((T, WA), BF16), _sds((T // tm, 8, 1, tm), F32), _sds((T, WA), BF16), _sds((T // tm, 8, 1, tm), F32),
                   _sds((T, 2 * CC), BF16), _sds((8, WA), F32), _sds((32, CC), F32), _sds((8, CC), F32)],
        in_specs=[tile(D), pl.BlockSpec((HALO, CC), lambda i: (nxt(i), 3)), tile(WA), tile(WA), tile(2 * CC),
                  tile(CC), pl.BlockSpec((HALO, CC), lambda i: (nxt(i), 0)),
                  row(WA), row(WA), const(32, CC), row(CC), row(CC)],
        out_specs=[tile(WA), ddrow, tile(WA), ddrow, tile(2 * CC), const(8, WA), const(32, CC), const(8, CC)],
        scratch_shapes=[pltpu.VMEM((RE, CC), F32), pltpu.VMEM((8, RE, CC), F32)],
        compiler_params=_cp("arbitrary"))(dy, dy, o_a, o_b, p_rest, cv, cv, g_a, g_b, cw, ng, nbias)


def _attn_bwd(p_qkv, col0, tab, do, lse, dd, kat, S, aq=None, ak=None, name="attn_bwd"):
    T = p_qkv.shape[0]
    n_tab, TK, TB = tab.shape
    nb, nkb, r = S // TB, S // TK, TK // TB
    B = T // S
    fox = aq is not None

    def body(*refs):
        if fox:
            (q_ref, k_ref, v_ref, tab_ref, do_ref, l_ref, dd_ref, kat_ref, aq_ref, ak_ref,
             dq_ref, dk_ref, dv_ref, dqa_ref, dka_ref, dqt_scr, dk_scr, dv_scr, sd_scr) = refs
        else:
            (q_ref, k_ref, v_ref, tab_ref, do_ref, l_ref, dd_ref, kat_ref,
             dq_ref, dk_ref, dv_ref, dqt_scr, dk_scr, dv_scr, sd_scr) = refs
        j = pl.program_id(2)
        hms = _head_masks()
        first_q_block = r * j

        @pl.when(j == 0)
        def _():
            dqt_scr[...] = jnp.zeros_like(dqt_scr)

        dk_scr[...] = jnp.zeros_like(dk_scr)
        dv_scr[...] = jnp.zeros_like(dv_scr)
        k2, v2 = k_ref[...], v_ref[...]
        ka = [jnp.where(hm, k2, ak_ref[0] if fox else jnp.zeros_like(k2)) for hm in hms]
        kat = [jnp.concatenate([kat_ref[0, 0, hh, t] for t in range(r)], axis=1) if r > 1 else kat_ref[0, 0, hh, 0]
               for hh in range(2)]

        def operands(i):
            ic = jnp.minimum(i, nb - 1)
            rows = pl.ds(pl.multiple_of(ic * TB, TB), TB)
            q2 = q_ref[rows, :] * 0.125
            do2 = do_ref[rows, :]
            qas = [jnp.where(hms[hh], q2, aq_ref[0, rows, :] if fox else jnp.zeros_like(q2)) for hh in range(2)]
            dohs = [jnp.where(hms[hh], do2, jnp.zeros_like(do2)) for hh in range(2)]
            return ic, qas, dohs

        def scores(i):
            ic, qas, dohs = operands(i)
            bias = tab_ref[jnp.where(i < nb, i - first_q_block, n_tab - 1)]
            return [(_dot_nt(ka[hh], qas[hh]) + bias, _dot_nt(v2, dohs[hh])) for hh in range(2)]

        def absorb(i, sd):
            ic, qas, dohs = operands(i)
            for hh in range(2):
                st, dpt = sd[hh]
                pt = jnp.exp(st - l_ref[ic, hh])
                dsb = (pt * (dpt - dd_ref[ic, hh])).astype(BF16)
                dv_scr[...] += _dot(pt.astype(BF16), dohs[hh])
                dk_scr[hh] += _dot(dsb, qas[hh])
                dqt_scr[hh, ic] += _dot(kat[hh], dsb)

        def q_pair(ii, carry):
            i0 = first_q_block + 2 * ii
            sd0 = [(sd_scr[hh, 0], sd_scr[hh, 1]) for hh in range(2)]
            sd1 = scores(i0 + 1)
            absorb(i0, sd0)
            nxt = scores(i0 + 2)
            for hh in range(2):
                sd_scr[hh, 0], sd_scr[hh, 1] = nxt[hh]
            absorb(i0 + 1, sd1)
            return carry

        first = scores(first_q_block)
        for hh in range(2):
            sd_scr[hh, 0], sd_scr[hh, 1] = first[hh]
        lax.fori_loop(0, (nb - first_q_block + 1) // 2, q_pair, 0)
        dk_ref[...] = jnp.where(hms[0], dk_scr[0], dk_scr[1]).astype(BF16)
        dv_ref[...] = dv_scr[...].astype(BF16)
        if fox:
            dka_ref[...] = jnp.where(hms[0], dk_scr[1], dk_scr[0])

        @pl.when(j == nkb - 1)
        def _():
            for ii in range(nb):
                t0, t1 = dqt_scr[0, ii].T, dqt_scr[1, ii].T
                dq_ref[ii * TB:(ii + 1) * TB, :] = jnp.where(hms[0], t0, t1) * 0.125
                if fox:
                    dqa_ref[ii * TB:(ii + 1) * TB, :] = jnp.where(hms[0], t1, t0)

    seq = lambda c: pl.BlockSpec((S, 128), lambda b, p, j: (b, c + p))
    kvb = lambda c: pl.BlockSpec((TK, 128), lambda b, p, j: (b * nkb + j, c + p))
    stat = pl.BlockSpec((nb, 2, 1, TB), lambda b, p, j: (b, p, 0, 0))
    in_specs = [seq(col0), kvb(col0 + NPAIR), kvb(col0 + 2 * NPAIR), pl.BlockSpec(tab.shape, lambda b, p, j: (0, 0, 0)),
                seq(0), stat, stat, pl.BlockSpec((1, 1, 2, r, 128, TB), lambda b, p, j: (b, p, 0, j, 0, 0))]
    args = [p_qkv, p_qkv, p_qkv, tab, do, lse, dd, kat]
    out_shape = [_sds((T, WA), F32), _sds((T, WA), BF16), _sds((T, WA), BF16)]
    out_specs = [seq(0), kvb(0), kvb(0)]
    if fox:
        in_specs += [pl.BlockSpec((1, S, 128), lambda b, p, j: (p, b, 0)), pl.BlockSpec((1, TK, 128), lambda b, p, j: (p, b * nkb + j, 0))]
        args += [aq, ak]
        out_shape += [_sds((T, WA), F32), _sds((T, WA), F32)]
        out_specs += [seq(0), kvb(0)]
    return pl.pallas_call(
        body, name=name, grid=(B, NPAIR, nkb), out_shape=out_shape, in_specs=in_specs, out_specs=out_specs,
        scratch_shapes=[pltpu.VMEM((2, nb, 128, TB), F32), pltpu.VMEM((2, TK, 128), F32), pltpu.VMEM((TK, 128), F32),
                        pltpu.VMEM((2, 2, TK, TB), F32)],
        compiler_params=_cp("parallel", "parallel", "arbitrary"))(*args)


def _gate_bwd(dqa, dka, p_rest, bf, S, TB):
    T = p_rest.shape[0]
    B, nb = T // S, S // TB

    def body(dqa_ref, dka_ref, z_ref, b_ref, dz_ref, db_ref):
        @pl.when(pl.program_id(0) == 0)
        def _():
            db_ref[...] = jnp.zeros_like(db_ref)

        later = (lax.broadcasted_iota(jnp.int32, (TB, TB), 1) >= lax.broadcasted_iota(jnp.int32, (TB, TB), 0)).astype(F32)
        lane = lax.broadcasted_iota(jnp.int32, (1, 128), 1)
        src, head = lax.broadcasted_iota(jnp.int32, (WA, 128), 0), lax.broadcasted_iota(jnp.int32, (WA, 128), 1)
        spare0 = 128 * (head // 2) + HD * (1 - head % 2)
        pick_q = (src == spare0 + AUG_K1).astype(F32)
        pick_k = (src == spare0 + AUG_Q1).astype(F32)
        carry = jnp.zeros((1, 128), F32)
        dbs = jnp.zeros((1, 128), F32)
        for j in reversed(range(nb)):
            rows = slice(j * TB, (j + 1) * TB)
            dc = _dot(dqa_ref[rows, :], pick_q, HIGHEST) - _dot(dka_ref[rows, :], pick_k, HIGHEST)
            part = _dot(later, dc, HIGHEST)
            tot = part + carry
            carry = carry + part[0:1, :]
            z = z_ref[j * TB:(j + 1) * TB, :] + b_ref[...]
            dz = jnp.where(lane < 6, tot * _sigmoid(-z), 0.0)
            dz_ref[j * TB:(j + 1) * TB, :] = dz.astype(BF16)
            dbs = dbs + jnp.sum(dz, axis=0, keepdims=True)
        db_ref[...] += dbs

    return pl.pallas_call(
        body, name="gate_bwd", grid=(B,), out_shape=[_sds((T, 128), BF16), _sds((1, 128), F32)],
        in_specs=[pl.BlockSpec((S, WA), lambda b: (b, 0)), pl.BlockSpec((S, WA), lambda b: (b, 0)),
                  pl.BlockSpec((S, 128), lambda b: (b, 4)), pl.BlockSpec((1, 128), lambda b: (0, 0))],
        out_specs=[pl.BlockSpec((S, 128), lambda b: (b, 0)), pl.BlockSpec((1, 128), lambda b: (0, 0))],
        compiler_params=_cp("arbitrary"))(dqa, dka, p_rest, bf)


def _adamw(parts, w, m, v, name, row0=0, prev=None):
    npart, Rp, C = parts.shape
    R = w.shape[0]
    tr = Rp
    for cand in (256, 128, 64, 32, 16, 8):
        if Rp % cand == 0 and row0 % cand == 0 and cand * C * 4 <= (1 << 20):
            tr = cand
            break
    if Rp == R and R * C * 4 <= (1 << 20):
        tr = R
    assert Rp % tr == 0 and row0 % tr == 0, (name, Rp, row0, tr)
    b0 = row0 // tr

    def body(p_ref, w_ref, m_ref, v_ref, *rest):
        g_ref, d_ref, mo_ref, vo_ref = rest[-4:]
        g = p_ref[0].astype(F32)
        for k in range(1, npart):
            g = g + p_ref[k].astype(F32)
        mn = ADAM_B1 * m_ref[...] + (1.0 - ADAM_B1) * g
        vn = ADAM_B2 * v_ref[...] + (1.0 - ADAM_B2) * (g * g)
        m_hat = mn / (1.0 - ADAM_B1 ** ADAM_STEP)
        v_hat = vn / (1.0 - ADAM_B2 ** ADAM_STEP)
        g_ref[...] = g
        d_ref[...] = -ADAM_LR * (m_hat / (jnp.sqrt(v_hat) + ADAM_EPS) + ADAM_WD * w_ref[...])
        mo_ref[...] = mn
        vo_ref[...] = vn

    blk = pl.BlockSpec((tr, C), lambda i: (b0 + i, 0))
    prev = list(prev) if prev is not None else []
    return pl.pallas_call(
        body, name=name, grid=(Rp // tr,), out_shape=[_sds((R, C), F32)] * 4,
        in_specs=[pl.BlockSpec((npart, tr, C), lambda i: (0, i, 0)), blk, blk, blk] + [_ANY] * len(prev), out_specs=[blk] * 4,
        input_output_aliases={4 + i: i for i in range(len(prev))},
        compiler_params=_cp("parallel"))(parts, w, m, v, *prev)


def _pad_w_in(w):
    z = jnp.zeros(w.shape[:-1] + (NPAD - N_IN,), w.dtype)
    return jnp.concatenate([w[..., :O_FA], w[..., O_QB:], w[..., O_FA:O_QB], z], axis=-1)


def _unpad_w_in(w):
    n_mid = N_IN - O_QB
    return jnp.concatenate([w[..., :O_FA], w[..., O_FA + n_mid:O_FA + n_mid + 6], w[..., O_FA:O_FA + n_mid]], axis=-1)


def _local_step(x, tgt, W, S, late_weights=None, on_layer_grads=None, on_ffn_grads=None):
    T = x.shape[0]
    TB = min(256, S)
    TW = min(2 * TB, S)
    tab_fox, tab_dil = _bias_table(S, TW, TB, True), _bias_table(S, TW, TB, False)
    tabq_fox, tabq_dil = _bias_table(S, TB, TW, True), _bias_table(S, TB, TW, False)
    saved = []
    h = _rms_fwd(x, W["ln1_g"][0])
    for l in range(DEPTH):
        p_qkv, p_rest = _mm_in(h, W["w_in"][l])
        aq, ak = _gate_fwd(p_rest, W["b_forget"][l], S, TB)
        o_a, lse_a, kat_a = _attn_fwd(p_qkv, 0, tabq_fox, S, aq, ak, name="fox_fwd")
        o_b, lse_b, kat_b = _attn_fwd(p_qkv, 3 * NPAIR, tabq_dil, S, name="dil_fwd")
        y, cv = _mix_fwd(o_a, o_b, p_rest, W["g_out_fox"][l], W["g_out_dil"][l], W["conv_w"][l], W["conv_b"][l],
                     W["cnorm_g"][l], W["cnorm_b"][l], S)
        if l == 0 and late_weights is not None:
            W = late_weights(W, y)
        x1, h2 = _mm_res(y[None], W["w_o"][l][None], x, W["ln2_g"][l], name="mm_o")
        u, c, hid = _up_fwd(h2, W["w_up"][l], W["ffn_conv_w"][l], W["ffn_conv_b"][l], S)
        g_next = W["ln1_g"][l + 1] if l + 1 < DEPTH else W["g_final"]
        x2, h_next = _mm_res(hid, W["w_down"][l], x1, g_next, name="mm_down")
        saved.append(dict(x=x, h=h, p_qkv=p_qkv, p_rest=p_rest, aq=aq, ak=ak, o_a=o_a, lse_a=lse_a, o_b=o_b,
                          lse_b=lse_b, kat_a=kat_a, kat_b=kat_b, y=y, cv=cv, x1=x1, h2=h2, u=u, c=c, hid=hid))
        x, h = x2, h_next
    dx, loss, dg_final = _final_loss(x, tgt, W["g_final"])
    G = {k: [None] * DEPTH for k in ("ln1_g", "w_in", "b_forget", "g_out_fox", "g_out_dil", "conv_w", "conv_b", "cnorm_g",
                                     "cnorm_b", "w_o", "ln2_g", "w_up", "ffn_conv_w", "ffn_conv_b", "w_down")}
    G["g_final"] = dg_final
    for l in reversed(range(DEPTH)):
        sv = saved[l]
        du, dcw = _down_bwd(dx, W["w_down"][l], sv["u"], sv["c"], W["ffn_conv_w"][l], S)
        G["w_down"][l] = _wgrad(sv["hid"], dx[None], D, "wg_down", ga=NJ)[:, 0]
        G["ffn_conv_w"][l] = dcw[:, :, 0:3, :]
        G["ffn_conv_b"][l] = dcw[:, :, 3, :]
        G["w_up"][l] = _wgrad(du.reshape(2 * NJ, T, FB), sv["h2"][None], D, "wg_up", ga=NJ)[:, 0]
        token = on_ffn_grads(l, G) if on_ffn_grads is not None else None
        ln2 = W["ln2_g"][l] if token is None else W["ln2_g"][l] + token[0, 0]
        dx1, G["ln2_g"][l] = _dx_rms(du, W["w_up"][l], sv["x1"], ln2, dx, True, "dx_up")
        G["w_o"][l] = _wgrad(sv["y"][None], dx1[None], D, "wg_o")[0, 0]
        dy = _mm_nt(dx1, W["w_o"][l], "mm_dy")
        do_a, dd_a, do_b, dd_b, dgvgg, dgn, dcvw, dcn = _mix_bwd(
            dy, sv["o_a"], sv["o_b"], sv["p_rest"], sv["cv"], W["g_out_fox"][l], W["g_out_dil"][l], W["conv_w"][l],
            W["cnorm_g"][l], W["cnorm_b"][l], S)
        G["g_out_fox"][l], G["g_out_dil"][l] = dgn[0], dgn[1]
        G["conv_w"][l], G["conv_b"][l] = dcvw[:CONV_K], dcvw[CONV_K]
        G["cnorm_g"][l], G["cnorm_b"][l] = dcn[0], dcn[1]
        dq_a, dk_a, dv_a, dqa, dka = _attn_bwd(sv["p_qkv"], 0, tab_fox, do_a, sv["lse_a"], dd_a, sv["kat_a"], S, sv["aq"], sv["ak"],
                                               name="fox_bwd")
        dq_b, dk_b, dv_b = _attn_bwd(sv["p_qkv"], 3 * NPAIR, tab_dil, do_b, sv["lse_b"], dd_b, sv["kat_b"], S, name="dil_bwd")
        dfa, db = _gate_bwd(dqa, dka, sv["p_rest"], W["b_forget"][l], S, TB)
        G["b_forget"][l] = db[0, :6]
        dp = jnp.concatenate([dq_a.astype(BF16), dk_a, dv_a, dq_b.astype(BF16), dk_b, dv_b, dgvgg, dfa,
                              jnp.zeros((T, 128), BF16)], axis=1)
        G["w_in"][l] = _wgrad(sv["h"][None], dp[None], NPAD, "wg_in")[0, 0]
        token = on_layer_grads(l, G) if on_layer_grads is not None else None
        ln1 = W["ln1_g"][l] if token is None else W["ln1_g"][l] + token[0, 0]
        dx, G["ln1_g"][l] = _dx_rms(dp, W["w_in"][l], sv["x"], ln1, dx1, False, "dx_in")
    return loss, dx, G


_MATMUL_W = ("w_in", "w_o", "w_up", "w_down")
_SHARDED = _MATMUL_W + ("conv_w", "ffn_conv_w")
_SMALL = ("ln1_g", "b_forget", "g_out_fox", "g_out_dil", "conv_b", "cnorm_g", "cnorm_b", "ln2_g", "ffn_conv_b", "g_final")
_ORDER = ("ln1_g", "w_in", "b_forget", "g_out_fox", "g_out_dil", "conv_w", "conv_b", "cnorm_g", "cnorm_b", "w_o", "ln2_g",
          "w_up", "ffn_conv_w", "ffn_conv_b", "w_down", "g_final")


def _kernel_ready(k, zone):
    if k == "w_in":
        return zone.reshape(D, NPAD)
    if k == "w_o":
        return zone.reshape(D, D)
    if k == "w_up":
        return zone.reshape(2, NJ, D, FB)
    if k == "w_down":
        return zone.reshape(NJ, FB, D)
    if k == "conv_w":
        cw = jnp.transpose(zone.reshape(NDEV, CONV_K, CC // NDEV), (1, 0, 2)).reshape(CONV_K, CC)
        return jnp.concatenate([cw, jnp.zeros((1, CC), F32)], axis=0)
    return zone.reshape(2, NJ, 3, FB)


def _full_weights(P, gathered):
    W = {}
    row = lambda a: [a[l][None, :] for l in range(DEPTH)]
    for k in _SHARDED:
        W[k] = [None if gathered[k][l] is None else _kernel_ready(k, gathered[k][l]) for l in range(DEPTH)]
    W["ffn_conv_b"] = [P["ffn_conv_b"][l].reshape(2, NJ, 1, FB) for l in range(DEPTH)]
    W["b_forget"] = [jnp.concatenate([P["b_forget"][l], jnp.zeros((128 - 6,), F32)])[None, :] for l in range(DEPTH)]
    for k in ("ln1_g", "ln2_g", "g_out_fox", "g_out_dil", "conv_b", "cnorm_g", "cnorm_b"):
        W[k] = row(P[k])
    W["g_final"] = P["g_final"][None, :]
    return W


def kernel(x, ln1_g, w_in, b_forget, g_out_fox, g_out_dil, conv_w, conv_b, cnorm_g, cnorm_b, w_o, ln2_g, w_up, ffn_conv_w, ffn_conv_b, w_down, g_final, loss_target, m_ln1_g, m_w_in, m_b_forget, m_g_out_fox, m_g_out_dil, m_conv_w, m_conv_b, m_cnorm_g, m_cnorm_b, m_w_o, m_ln2_g, m_w_up, m_ffn_conv_w, m_ffn_conv_b, m_w_down, m_g_final, v_ln1_g, v_w_in, v_b_forget, v_g_out_fox, v_g_out_dil, v_conv_w, v_conv_b, v_cnorm_g, v_cnorm_b, v_w_o, v_ln2_g, v_w_up, v_ffn_conv_w, v_ffn_conv_b, v_w_down, v_g_final):
    P = dict(ln1_g=ln1_g, w_in=w_in, b_forget=b_forget, g_out_fox=g_out_fox, g_out_dil=g_out_dil, conv_w=conv_w, conv_b=conv_b,
             cnorm_g=cnorm_g, cnorm_b=cnorm_b, w_o=w_o, ln2_g=ln2_g, w_up=w_up, ffn_conv_w=ffn_conv_w, ffn_conv_b=ffn_conv_b,
             w_down=w_down, g_final=g_final)
    M = dict(ln1_g=m_ln1_g, w_in=m_w_in, b_forget=m_b_forget, g_out_fox=m_g_out_fox, g_out_dil=m_g_out_dil, conv_w=m_conv_w,
             conv_b=m_conv_b, cnorm_g=m_cnorm_g, cnorm_b=m_cnorm_b, w_o=m_w_o, ln2_g=m_ln2_g, w_up=m_w_up, ffn_conv_w=m_ffn_conv_w,
             ffn_conv_b=m_ffn_conv_b, w_down=m_w_down, g_final=m_g_final)
    V = dict(ln1_g=v_ln1_g, w_in=v_w_in, b_forget=v_b_forget, g_out_fox=v_g_out_fox, g_out_dil=v_g_out_dil, conv_w=v_conv_w,
             conv_b=v_conv_b, cnorm_g=v_cnorm_g, cnorm_b=v_cnorm_b, w_o=v_w_o, ln2_g=v_ln2_g, w_up=v_w_up, ffn_conv_w=v_ffn_conv_w,
             ffn_conv_b=v_ffn_conv_b, w_down=v_w_down, g_final=v_g_final)
    Bl, S, _ = x.shape
    T = Bl * S

    def shard(k, l):
        if k == "w_in":
            return _pad_w_in(P[k][l].astype(BF16))
        return P[k][l].astype(BF16) if k in _MATMUL_W else P[k][l]

    early = [("w_in", 0)] + [(k, l) for k in ("conv_w", "ffn_conv_w") for l in range(DEPTH)]
    late = [(k, 0) for k in ("w_o", "w_up", "w_down")] + [(k, 1) for k in _MATMUL_W]
    gathered = {k: [None] * DEPTH for k in _SHARDED}
    plan_early = _Plan([[shard(k, l)] for k, l in early], True)
    early_zones = _exchange(plan_early, "gather_early")
    for (k, l), zone in zip(early, early_zones):
        gathered[k][l] = zone
    late_shards, _ = lax.optimization_barrier(([shard(k, l) for k, l in late], early_zones[0]))
    plan_late = _Plan([[a] for a in late_shards], True)
    late_handle, late_token = _exchange_start(plan_late, "gather_late_start", plan_late.zones_with_own())
    W = _full_weights(P, gathered)
    W["ln1_g"][0] = W["ln1_g"][0] + late_token[0, 0]

    def late_weights(W, after):
        zones = _exchange_wait(plan_late, "gather_late_wait", late_handle, after)
        W = dict(W)
        for (k, l), zone in zip(late, zones):
            W[k] = list(W[k])
            W[k][l] = _kernel_ready(k, zone)
        return W

    def owner_block(G, l, k):
        if k == "w_in":
            blk = _unpad_w_in(G[k][l]).reshape(NDEV, D // NDEV, N_IN)
        elif k == "w_o":
            blk = G[k][l].reshape(NDEV, D // NDEV, D)
        elif k == "w_up":
            blk = G[k][l]
        elif k == "w_down":
            blk = G[k][l].reshape(NDEV, DFF // NDEV, D)
        elif k == "conv_w":
            blk = jnp.transpose(G[k][l].reshape(CONV_K, NDEV, CC // NDEV), (1, 0, 2))
        else:
            blk = G[k][l].reshape(NDEV, 3, FB)
        return blk.astype(GRAD_DTYPE)

    flying = []

    def start_scatter(tag, G, l, keys):
        plan = _Plan([[owner_block(G, l, k)] for k in keys], False)
        handle, token = _exchange_start(plan, "scatter_%s_start" % tag, plan.zones_with_own())
        flying.append((tag, l, keys, plan, handle))
        return token

    def on_layer_grads(l, G):
        return start_scatter("l1", G, l, _MATMUL_W) if l == 1 else start_scatter("l0_rest", G, l, ("w_in", "w_o"))

    def on_ffn_grads(l, G):
        return start_scatter("l0_ffn", G, l, ("w_up", "w_down")) if l == 0 else None

    loss, dx, G = _local_step(x.reshape(T, D), loss_target.reshape(T, D), W, S, late_weights, on_layer_grads, on_ffn_grads)

    parts = {}
    for tag, l, keys, plan, handle in flying:
        for k, zone in zip(keys, _exchange_wait(plan, "scatter_%s_wait" % tag, handle, dx)):
            parts[(k, l)] = zone
    plan_last = _Plan([[owner_block(G, l, k) for l in range(DEPTH)] for k in ("conv_w", "ffn_conv_w")], False)
    last = _exchange(plan_last, "scatter_last")
    small_parts_sharded = {"conv_w": last[0], "ffn_conv_w": last[1]}

    small = [jnp.stack(G[k]).reshape(-1) for k in _SMALL[:-1]] + [G["g_final"].reshape(-1), loss[0, :1]]
    sizes = [int(s.shape[0]) for s in small]
    flat = jnp.concatenate(small)
    n_small = -(-flat.shape[0] // 1024) * 1024
    flat = jnp.concatenate([flat, jnp.zeros((n_small - flat.shape[0],), F32)]).reshape(n_small // 128, 128)
    small_parts = _exchange(_Plan([[flat]], True), "gather_small")[0]

    out_g, out_d, out_m, out_v = {}, {}, {}, {}
    sg = _sum_parts(small_parts.reshape(NDEV, n_small // 128, 128), "sum_small").reshape(-1)
    off = 0
    summed = {}
    for k, n in zip(_SMALL + ("loss",), sizes):
        summed[k] = sg[off:off + n]
        off += n
    for k in _ORDER:
        shp = P[k].shape
        C = shp[-1]
        if k in _MATMUL_W:
            swap = k == "w_up"
            R0, Cw = (shp[2], shp[1]) if swap else (shp[1], shp[2])
            flat2 = lambda t: (jnp.swapaxes(t, 1, 2) if swap else t).reshape(DEPTH * R0, Cw)
            res = None
            for l in reversed(range(DEPTH)):
                res = _adamw(parts[(k, l)].reshape(NDEV, R0, Cw), flat2(P[k]), flat2(M[k]), flat2(V[k]),
                             "adamw_%s_l%d" % (k, l), row0=l * R0, prev=res)
            if swap:
                res = [jnp.swapaxes(t.reshape(DEPTH, R0, Cw), 1, 2) for t in res]
        else:
            pr = small_parts_sharded[k].reshape((NDEV, -1, C)) if k in _SHARDED else summed[k].reshape((1, -1, C))
            R = pr.shape[1]
            res = _adamw(pr, P[k].reshape(R, -1), M[k].reshape(R, -1), V[k].reshape(R, -1), "adamw_" + k)
        out_g[k], out_d[k], out_m[k], out_v[k] = (t.reshape(shp) for t in res)

    loss_out = summed["loss"].reshape(())
    grad_x = dx.reshape(Bl, S, D)
    return (loss_out, grad_x, *[out_g[k] for k in _ORDER], *[out_d[k] for k in _ORDER], *[out_m[k] for k in _ORDER],
            *[out_v[k] for k in _ORDER])
```

```python
import functools

import numpy as np
import jax
import jax.numpy as jnp
from jax import lax
from jax.experimental import pallas as pl
from jax.experimental.pallas import tpu as pltpu

F32 = jnp.float32
BF16 = jnp.bfloat16
GRAD_DTYPE = BF16
HIGHEST = lax.Precision.HIGHEST

D = 1024
DEPTH = 2
HD = 64
WA = 384
NPAIR = 3
CC = 256
CONV_K = 31
DFF = 2816
FB = 704
NJ = 4
NDEV = 8
EPS = 1e-6
NQKV = 2304
NREST = 768
NPAD = NQKV + NREST
O_FA, O_QB, N_IN = 1152, 1158, 2822
DILATION_PAIRS = ((128, 1), (512, 4), (2048, 16))
NEG = -1e30

ADAM_LR, ADAM_B1, ADAM_B2, ADAM_EPS, ADAM_WD, ADAM_STEP = 0.001, 0.9, 0.999, 1e-08, 0.01, 10

VMEM_LIMIT = 48 * 1024 * 1024
VMEM_LIMIT_BIG = 56 * 1024 * 1024


def _cp(*sem):
    return pltpu.CompilerParams(dimension_semantics=sem, vmem_limit_bytes=VMEM_LIMIT)


def _sds(shape, dtype):
    return jax.ShapeDtypeStruct(shape, dtype)


def _dot(a, b, prec=None):
    return jnp.dot(a, b, preferred_element_type=F32, precision=prec)


def _dot_nt(a, b, prec=None):
    return lax.dot_general(a, b, (((1,), (1,)), ((), ())), preferred_element_type=F32, precision=prec)


def _dot_tn(a, b):
    return lax.dot_general(a, b, (((0,), (0,)), ((), ())), preferred_element_type=F32)


def _sigmoid(z):
    return 0.5 * jnp.tanh(0.5 * z) + 0.5


def _rms_bwd(dh, x, g):
    r = lax.rsqrt(jnp.mean(x * x, axis=-1, keepdims=True) + EPS)
    xh = x * r
    dg = jnp.sum(dh * xh, axis=0, keepdims=True)
    dxh = dh * g
    dx = r * (dxh - xh * jnp.mean(dxh * xh, axis=-1, keepdims=True))
    return dx, dg


class _Plan:
    def __init__(self, groups, gather, slots=None, n_slots=None):
        self.groups, self.gather = groups, gather
        self.slots = slots or [list(range(len(g))) for g in groups]
        self.n_slots = n_slots or [len(g) for g in groups]
        self.flat = [a for g in groups for a in g]
        self.where = [(gi, s) for gi, g in enumerate(groups) for s in self.slots[gi]]
        self.zone_shapes = [_sds((NDEV, ns) + (g[0].shape if gather else g[0].shape[1:]), g[0].dtype)
                            for g, ns in zip(groups, self.n_slots)]

    def new_zones(self):
        return [lax.empty(z.shape, z.dtype) for z in self.zone_shapes]

    def me(self):
        x, y, c = lax.axis_index("x"), lax.axis_index("y"), lax.axis_index("c")
        return 4 * x + 2 * y + c

    def zones_with_own(self):
        me = self.me()
        zones = self.new_zones()
        for a, (gi, s) in enumerate(self.where):
            src = self.flat[a] if self.gather else lax.dynamic_index_in_dim(self.flat[a], me, 0, keepdims=False)
            zones[gi] = lax.dynamic_update_slice(zones[gi], src[None, None], (me, s) + (0,) * src.ndim)
        return zones

    def own_copies(self, ins, zones, sems):
        me = self.me()
        return [pltpu.make_async_copy(ins[a] if self.gather else ins[a].at[me], zones[gi].at[me, s], sems.at[a])
                for a, (gi, s) in enumerate(self.where)]

    def remote_copies(self, ins, zones, send_sems, recv_sems):
        x, y, c = lax.axis_index("x"), lax.axis_index("y"), lax.axis_index("c")
        me = 4 * x + 2 * y + c
        out = []
        for a, (gi, s) in enumerate(self.where):
            for k in range(1, NDEV):
                px, py, pc = x ^ ((k >> 2) & 1), y ^ ((k >> 1) & 1), c ^ (k & 1)
                out.append(pltpu.make_async_remote_copy(
                    src_ref=ins[a] if self.gather else ins[a].at[4 * px + 2 * py + pc], dst_ref=zones[gi].at[me, s],
                    send_sem=send_sems.at[a * (NDEV - 1) + k - 1], recv_sem=recv_sems.at[a * (NDEV - 1) + k - 1],
                    device_id=(px, py, pc), device_id_type=pl.DeviceIdType.MESH))
        return out


_ANY = pl.BlockSpec(memory_space=pl.ANY)
_HBM = pl.BlockSpec(memory_space=pltpu.HBM)
_SEM = pl.BlockSpec(memory_space=pltpu.SEMAPHORE)


def _exchange(plan, name, zones=None):
    n, nz = len(plan.flat), len(plan.groups)
    zones = zones or plan.new_zones()

    def body(*refs):
        ins, zin = refs[:n], refs[n:n + nz]
        send_sems, recv_sems, local_sems = refs[n + 2 * nz:]
        copies = plan.own_copies(ins, zin, local_sems) + plan.remote_copies(ins, zin, send_sems, recv_sems)
        for cp in copies:
            cp.start()
        for cp in copies:
            cp.wait()

    return pl.pallas_call(
        body, name=name, out_shape=plan.zone_shapes, in_specs=[_ANY] * (n + nz), out_specs=[_ANY] * nz,
        input_output_aliases={n + g: g for g in range(nz)},
        scratch_shapes=[pltpu.SemaphoreType.DMA((n * (NDEV - 1),)), pltpu.SemaphoreType.DMA((n * (NDEV - 1),)),
                        pltpu.SemaphoreType.DMA((n,))],
        compiler_params=pltpu.CompilerParams(has_side_effects=True),
    )(*plan.flat, *zones)


def _exchange_start(plan, name, zones):
    n, nz = len(plan.flat), len(plan.groups)
    hbm = lambda a: pltpu.HBM(a.shape, a.dtype)

    def body(*refs):
        ins, zin = refs[:n], refs[n:n + nz]
        send_sems, recv_sems = refs[n + nz], refs[n + nz + 1]
        token = refs[-1]
        for cp in plan.remote_copies(ins, zin, send_sems, recv_sems):
            cp.start()
        token[...] = jnp.zeros_like(token)

    outs = pl.pallas_call(
        body, name=name,
        out_shape=(pltpu.SemaphoreType.DMA((n * (NDEV - 1),)), pltpu.SemaphoreType.DMA((n * (NDEV - 1),)),
                   *[hbm(a) for a in plan.flat], *[hbm(z) for z in plan.zone_shapes], _sds((8, 128), F32)),
        in_specs=[_HBM] * (n + nz),
        out_specs=(_SEM, _SEM, *[_HBM] * (n + nz), pl.BlockSpec(memory_space=pltpu.VMEM)),
        input_output_aliases={i: 2 + i for i in range(n + nz)},
        compiler_params=pltpu.CompilerParams(has_side_effects=pltpu.SideEffectType.DATAFLOW_SIDE_EFFECTING),
    )(*[pltpu.with_memory_space_constraint(a, pltpu.HBM) for a in plan.flat],
      *[pltpu.with_memory_space_constraint(z, pltpu.HBM) for z in zones])
    return outs[:-1], outs[-1]


def _exchange_wait(plan, name, handle, after):
    n, nz = len(plan.flat), len(plan.groups)
    send_sems, recv_sems = handle[0], handle[1]
    thru = handle[2:]
    hbm = lambda a: pltpu.HBM(a.shape, a.dtype)

    def body(*refs):
        ins, zin = refs[:n], refs[n:n + nz]
        ssem, rsem = refs[n + nz], refs[n + nz + 1]
        for cp in plan.remote_copies(ins, zin, ssem, rsem):
            cp.wait_send()
            cp.wait_recv()

    outs = pl.pallas_call(
        body, name=name, out_shape=tuple(hbm(a) for a in thru),
        in_specs=[_HBM] * (n + nz) + [_SEM, _SEM, _ANY], out_specs=tuple([_HBM] * (n + nz)),
        input_output_aliases={i: i for i in range(n + nz)},
        compiler_params=pltpu.CompilerParams(has_side_effects=pltpu.SideEffectType.DATAFLOW_SIDE_EFFECTING),
    )(*thru, send_sems, recv_sems, after)
    return list(outs[n:])


def _sum_parts(parts, name):
    npart, R, C = parts.shape

    def body(p_ref, o_ref):
        s = p_ref[0]
        for k in range(1, npart):
            s = s + p_ref[k]
        o_ref[...] = s

    return pl.pallas_call(body, name=name, out_shape=_sds((R, C), F32))(parts)


def _rms_fwd(x, g):
    T = x.shape[0]
    tm = min(512, T)

    def body(x_ref, g_ref, o_ref):
        xv = x_ref[...]
        r = lax.rsqrt(jnp.mean(xv * xv, axis=-1, keepdims=True) + EPS)
        o_ref[...] = (xv * r * g_ref[...]).astype(BF16)

    return pl.pallas_call(
        body, name="rms_fwd", grid=(T // tm,), out_shape=_sds((T, D), BF16),
        in_specs=[pl.BlockSpec((tm, D), lambda i: (i, 0)), pl.BlockSpec((1, D), lambda i: (0, 0))],
        out_specs=pl.BlockSpec((tm, D), lambda i: (i, 0)), compiler_params=_cp("parallel"))(x, g)


def _mm_in(h, w):
    T = h.shape[0]
    tm = min(512, T)

    def body(h_ref, w_ref, q_ref, r_ref):
        hv = h_ref[...]
        q_ref[...] = _dot(hv, w_ref[:, :NQKV]).astype(BF16)
        r_ref[...] = _dot(hv, w_ref[:, NQKV:])

    return pl.pallas_call(
        body, name="mm_in", grid=(T // tm,), out_shape=[_sds((T, NQKV), BF16), _sds((T, NREST), F32)],
        in_specs=[pl.BlockSpec((tm, D), lambda i: (i, 0)), pl.BlockSpec((D, NPAD), lambda i: (0, 0))],
        out_specs=[pl.BlockSpec((tm, NQKV), lambda i: (i, 0)), pl.BlockSpec((tm, NREST), lambda i: (i, 0))],
        compiler_params=_cp("parallel"))(h, w)


AUG_Q1, AUG_K1 = 3, 0


def _aug_lane0(h):
    return HD * (1 - h % 2)


def _aug_tables():
    selq = np.zeros((NPAIR, 3 * 128, 128), np.float32)
    selk = np.zeros((NPAIR, 3 * 128, 128), np.float32)
    oneq = np.zeros((NPAIR, 1, 128), np.float32)
    onek = np.zeros((NPAIR, 1, 128), np.float32)
    for h in range(2 * NPAIR):
        p, l0 = h // 2, _aug_lane0(h)
        for piece in range(3):
            selq[p, 128 * piece + h, l0 + piece] = 1.0
            selk[p, 128 * piece + h, l0 + 3 + piece] = -1.0
            oneq[p, 0, l0 + AUG_Q1 + piece] = 1.0
            onek[p, 0, l0 + AUG_K1 + piece] = 1.0
    return [jnp.asarray(a, BF16) for a in (selq, selk, oneq, onek)]


def _gate_fwd(p_rest, bf, S, TB):
    T = p_rest.shape[0]
    B, nb = T // S, S // TB
    selq, selk, oneq, onek = _aug_tables()

    def body(z_ref, b_ref, sq_ref, sk_ref, oq_ref, ok_ref, aq_ref, ak_ref):
        tri = (lax.broadcasted_iota(jnp.int32, (TB, TB), 0) >= lax.broadcasted_iota(jnp.int32, (TB, TB), 1)).astype(F32)
        carry = jnp.zeros((1, 128), F32)
        for j in range(nb):
            rows = slice(j * TB, (j + 1) * TB)
            z = z_ref[rows, :] + b_ref[...]
            lf = jnp.minimum(z, 0.0) - jnp.log(1.0 + jnp.exp(-jnp.abs(z)))
            cb = _dot(tri, lf, HIGHEST) + carry
            carry = cb[TB - 1:TB, :]
            hi = cb.astype(BF16)
            mid = (cb - hi.astype(F32)).astype(BF16)
            lo = (cb - hi.astype(F32) - mid.astype(F32)).astype(BF16)
            pieces = jnp.concatenate([hi, mid, lo], axis=1)
            for p in range(NPAIR):
                aq_ref[p, rows, :] = (_dot(pieces, sq_ref[p]) + oq_ref[p].astype(F32)).astype(BF16)
                ak_ref[p, rows, :] = (_dot(pieces, sk_ref[p]) + ok_ref[p].astype(F32)).astype(BF16)

    full = lambda a: pl.BlockSpec(a.shape, lambda b: (0,) * a.ndim)
    return pl.pallas_call(
        body, name="gate_fwd", grid=(B,), out_shape=[_sds((NPAIR, T, 128), BF16), _sds((NPAIR, T, 128), BF16)],
        in_specs=[pl.BlockSpec((S, 128), lambda b: (b, 4)), pl.BlockSpec((1, 128), lambda b: (0, 0)),
                  full(selq), full(selk), full(oneq), full(onek)],
        out_specs=[pl.BlockSpec((NPAIR, S, 128), lambda b: (0, b, 0)), pl.BlockSpec((NPAIR, S, 128), lambda b: (0, b, 0))],
        compiler_params=_cp("parallel"))(p_rest, bf, selq, selk, oneq, onek)


def _bias_table(S, n_key, n_query, fox):
    unit = min(n_key, n_query)
    d_min = -(n_query // unit - 1)
    d = np.arange(d_min, S // unit)[:, None, None]
    delta = d * unit + np.arange(n_query)[None, None, :] - np.arange(n_key)[None, :, None]
    if fox:
        mult = (delta >= 0).astype(np.float64)
    else:
        mult = np.zeros(delta.shape)
        for w, dil in DILATION_PAIRS:
            mult += (delta >= 0) & (delta % dil == 0) & (delta <= w)
    with np.errstate(divide="ignore"):
        tab = np.where(mult > 0, np.log(np.maximum(mult, 1e-30)), NEG)
    tab = np.concatenate([tab, np.full((1, n_key, n_query), NEG)], axis=0)
    return jnp.asarray(tab, F32)


def _head_masks():
    lane = lax.broadcasted_iota(jnp.int32, (1, 128), 1)
    return [lane < HD, lane >= HD]


def _transpose_bf16(a):
    return a.astype(F32).T.astype(BF16)


def _col_reduce(x, op):
    while x.shape[0] > 8:
        half = x.shape[0] // 2
        x = op(x[:half], x[half:])
    return jnp.max(x, axis=0, keepdims=True) if op is jnp.maximum else jnp.sum(x, axis=0, keepdims=True)


def _attn_fwd(p_qkv, col0, tab, S, aq=None, ak=None, name="attn"):
    T = p_qkv.shape[0]
    n_tab, TB, TQ = tab.shape
    nb, nq, r = S // TB, S // TQ, TQ // TB
    B = T // S
    fox = aq is not None

    def body(*refs):
        if fox:
            q_ref, k_ref, v_ref, tab_ref, aq_ref, ak_ref, o_ref, l_ref, kat_ref, vta_scr, acc_scr, st_scr = refs
        else:
            q_ref, k_ref, v_ref, tab_ref, o_ref, l_ref, kat_ref, vta_scr, acc_scr, st_scr = refs
        i = pl.program_id(2)
        hms = _head_masks()
        top = lax.broadcasted_iota(jnp.int32, (128, 1), 0) < HD
        last_key_block = r * i + r - 1

        @pl.when(i == 0)
        def _():
            for jj in range(nb):
                rows = slice(jj * TB, (jj + 1) * TB)
                vt = _transpose_bf16(v_ref[rows, :])
                for hh in range(2):
                    vta_scr[hh, jj] = jnp.where(top == (hh == 0), vt, jnp.ones_like(vt))
                    spare_k = ak_ref[0, rows, :] if fox else jnp.zeros((TB, 128), BF16)
                    kat_ref[0, 0, hh, jj] = _transpose_bf16(jnp.where(hms[hh], k_ref[rows, :], spare_k))

        q2 = q_ref[...] * 0.125
        spare_q = aq_ref[0] if fox else jnp.zeros_like(q2)
        qa = [jnp.where(hm, q2, spare_q) for hm in hms]
        acc_scr[...] = jnp.zeros_like(acc_scr)

        def scores(j):
            jc = jnp.minimum(j, nb - 1)
            rows = pl.ds(pl.multiple_of(jc * TB, TB), TB)
            k2 = k_ref[rows, :]
            bias = tab_ref[jnp.where(j <= last_key_block, r * i - j + (r - 1), n_tab - 1)]
            return [_dot_nt(jnp.where(hms[hh], k2, ak_ref[0, rows, :]) if fox else k2, qa[hh]) + bias for hh in range(2)]

        def absorb(j, sts, stats):
            jc = jnp.minimum(j, nb - 1)
            alphas, pts = [], []
            for hh in range(2):
                m_old = stats[2 * hh]
                m_new = jnp.maximum(m_old, _col_reduce(sts[hh], jnp.maximum))
                pts.append(jnp.exp(sts[hh] - m_new).astype(BF16))
                alphas.append(jnp.exp(m_old - m_new))
                stats[2 * hh] = m_new
            pvs = [_dot(vta_scr[hh, jc], pts[hh]) for hh in range(2)]
            for hh in range(2):
                half = slice(HD * hh, HD * (hh + 1))
                ones_row = HD * (1 - hh)
                acc_scr[half, :] = alphas[hh] * acc_scr[half, :] + pvs[hh][half]
                stats[2 * hh + 1] = alphas[hh] * stats[2 * hh + 1] + pvs[hh][ones_row:ones_row + 1]

        def kv_pair(jj, carry):
            stats = list(carry)
            j0 = 2 * jj
            st0 = [st_scr[hh] for hh in range(2)]
            st1 = scores(j0 + 1)
            absorb(j0, st0, stats)
            nxt = scores(j0 + 2)
            for hh in range(2):
                st_scr[hh] = nxt[hh]
            absorb(j0 + 1, st1, stats)
            return tuple(stats)

        first = scores(0)
        for hh in range(2):
            st_scr[hh] = first[hh]
        row = lambda v: jnp.full((1, TQ), v, F32)
        m0, l0, m1, l1 = lax.fori_loop(0, (last_key_block + 2) // 2, kv_pair, (row(NEG), row(0.0), row(NEG), row(0.0)))
        o_ref[...] = (acc_scr[...] * jnp.where(top, 1.0 / l0, 1.0 / l1)).T
        for hh, lse in enumerate((m0 + jnp.log(l0), m1 + jnp.log(l1))):
            for t in range(r):
                l_ref[t, hh] = lse[:, t * TB:(t + 1) * TB]

    in_specs = [pl.BlockSpec((TQ, 128), lambda b, p, i: (b * nq + i, col0 + p)),
                pl.BlockSpec((S, 128), lambda b, p, i: (b, col0 + NPAIR + p)),
                pl.BlockSpec((S, 128), lambda b, p, i: (b, col0 + 2 * NPAIR + p)),
                pl.BlockSpec(tab.shape, lambda b, p, i: (0, 0, 0))]
    args = [p_qkv, p_qkv, p_qkv, tab]
    if fox:
        in_specs += [pl.BlockSpec((1, TQ, 128), lambda b, p, i: (p, b * nq + i, 0)),
                     pl.BlockSpec((1, S, 128), lambda b, p, i: (p, b, 0))]
        args += [aq, ak]
    return pl.pallas_call(
        body, name=name, grid=(B, NPAIR, nq),
        out_shape=[_sds((T, WA), F32), _sds((T // TB, 2 * NPAIR, 1, TB), F32), _sds((B, NPAIR, 2, nb, 128, TB), BF16)],
        in_specs=in_specs,
        out_specs=[pl.BlockSpec((TQ, 128), lambda b, p, i: (b * nq + i, p)),
                   pl.BlockSpec((r, 2, 1, TB), lambda b, p, i: (b * nq + i, p, 0, 0)),
                   pl.BlockSpec((1, 1, 2, nb, 128, TB), lambda b, p, i: (b, p, 0, 0, 0, 0))],
        scratch_shapes=[pltpu.VMEM((2, nb, 128, TB), BF16), pltpu.VMEM((128, TQ), F32), pltpu.VMEM((2, TB, TQ), F32)],
        compiler_params=_cp("parallel", "parallel", "arbitrary"))(*args)


def _phase_copies(src, phases, length):
    for r in range(1, 8):
        phases[r, 0:length - 8, :] = src[pl.ds(r, length - 8), :]


def _tap(src, phases, offset, rows):
    r = offset % 8
    return src[pl.ds(offset, rows), :] if r == 0 else phases[r, pl.ds(offset - r, rows), :]


def _conv_taps(src, phases, w_ref, first_row, rows):
    acc = w_ref[0:1, :] * _tap(src, phases, first_row, rows)
    for k in range(1, CONV_K):
        acc = acc + w_ref[k:k + 1, :] * _tap(src, phases, first_row + k, rows)
    return acc


def _mix_fwd(o_a, o_b, p_rest, g_a, g_b, cw, cb, ng, nbias, S):
    T = o_a.shape[0]
    tm = min(256, S)
    nps = S // tm
    HALO = 32

    def body(oa_ref, ob_ref, pt_ref, ph_ref, ga_ref, gb_ref, cw_ref, cb_ref, ng_ref, nb_ref, y_ref, cv_ref, buf, phases):
        i = pl.program_id(0)
        first = (i % nps) == 0
        for o_ref, g_ref, c0 in ((oa_ref, ga_ref, 0), (ob_ref, gb_ref, WA)):
            o = o_ref[...]
            r = lax.rsqrt(jnp.mean(o * o, axis=-1, keepdims=True) + EPS)
            y_ref[:, c0:c0 + WA] = (o * r * g_ref[...]).astype(BF16)
        ph = jnp.where(first, 0.0, ph_ref[...])
        buf[0:HALO, :] = ph[:, :CC] * _sigmoid(ph[:, CC:])
        pt = pt_ref[...]
        buf[HALO:HALO + tm, :] = pt[:, :CC] * _sigmoid(pt[:, CC:])
        _phase_copies(buf, phases, tm + HALO)
        cv = _conv_taps(buf, phases, cw_ref, HALO - CONV_K + 1, tm) + cb_ref[...]
        cv_ref[...] = cv
        xc = cv - jnp.mean(cv, axis=-1, keepdims=True)
        lo = xc * lax.rsqrt(jnp.mean(xc * xc, axis=-1, keepdims=True) + EPS) * ng_ref[...] + nb_ref[...]
        y_ref[:, 2 * WA:] = (lo * _sigmoid(lo)).astype(BF16)

    row = lambda w: pl.BlockSpec((1, w), lambda i: (0, 0))
    return pl.pallas_call(
        body, name="mix_fwd", grid=(T // tm,), out_shape=[_sds((T, D), BF16), _sds((T, CC), F32)],
        in_specs=[pl.BlockSpec((tm, WA), lambda i: (i, 0)), pl.BlockSpec((tm, WA), lambda i: (i, 0)),
                  pl.BlockSpec((tm, 2 * CC), lambda i: (i, 0)),
                  pl.BlockSpec((HALO, 2 * CC), lambda i: (jnp.maximum(i * (tm // HALO) - 1, 0), 0)),
                  row(WA), row(WA), pl.BlockSpec((32, CC), lambda i: (0, 0)), row(CC), row(CC), row(CC)],
        out_specs=[pl.BlockSpec((tm, D), lambda i: (i, 0)), pl.BlockSpec((tm, CC), lambda i: (i, 0))],
        scratch_shapes=[pltpu.VMEM((tm + HALO, CC), F32), pltpu.VMEM((8, tm + HALO, CC), F32)],
        compiler_params=_cp("parallel"))(o_a, o_b, p_rest, p_rest, g_a, g_b, cw, cb, ng, nbias)


def _mm_res(a, w, resid, g, name):
    nk, T, kb = a.shape
    tm = min(512, T)

    def body(a_ref, w_ref, r_ref, g_ref, x_ref, h_ref):
        xv = r_ref[...]
        for k in range(nk):
            xv = xv + _dot(a_ref[k], w_ref[k])
        x_ref[...] = xv
        r = lax.rsqrt(jnp.mean(xv * xv, axis=-1, keepdims=True) + EPS)
        h_ref[...] = (xv * r * g_ref[...]).astype(BF16)

    return pl.pallas_call(
        body, name=name, grid=(T // tm,), out_shape=[_sds((T, D), F32), _sds((T, D), BF16)],
        in_specs=[pl.BlockSpec((nk, tm, kb), lambda i: (0, i, 0)), pl.BlockSpec((nk, kb, D), lambda i: (0, 0, 0)),
                  pl.BlockSpec((tm, D), lambda i: (i, 0)), pl.BlockSpec((1, D), lambda i: (0, 0))],
        out_specs=[pl.BlockSpec((tm, D), lambda i: (i, 0)), pl.BlockSpec((tm, D), lambda i: (i, 0))],
        compiler_params=_cp("parallel"))(a, w, resid, g)


def _up_fwd(h2, w_up, fcw, fcb, S):
    T = h2.shape[0]
    tm = min(512, S)
    nps = S // tm

    def body(h_ref, hh_ref, w_ref, cw_ref, cb_ref, u_ref, c_ref, hid_ref, buf, hbuf):
        i = pl.program_id(1)
        first = (i % nps) == 0
        hbuf[0:tm, :] = h_ref[...]
        hbuf[tm:tm + 8, :] = jnp.where(first, jnp.zeros_like(hh_ref[...]), hh_ref[...])
        c = []
        for half in range(2):
            ua = _dot(hbuf[...], w_ref[half, 0])
            u = ua[0:tm]
            u_ref[half, 0] = u
            buf[half, 0:8, :] = ua[tm:tm + 8]
            buf[half, 8:8 + tm, :] = u
            cw = cw_ref[half, 0]
            c.append(cb_ref[half, 0] + cw[0:1] * buf[half, pl.ds(6, tm), :] + cw[1:2] * buf[half, pl.ds(7, tm), :] + cw[2:3] * u)
        for half in range(2):
            c_ref[half, 0] = c[half]
        hid_ref[0] = (c[0] * _sigmoid(c[0]) * c[1]).astype(BF16)

    return pl.pallas_call(
        body, name="up_fwd", grid=(NJ, T // tm),
        out_shape=[_sds((2, NJ, T, FB), F32), _sds((2, NJ, T, FB), F32), _sds((NJ, T, FB), BF16)],
        in_specs=[pl.BlockSpec((tm, D), lambda j, i: (i, 0)),
                  pl.BlockSpec((8, D), lambda j, i: (jnp.maximum(i * (tm // 8) - 1, 0), 0)),
                  pl.BlockSpec((2, 1, D, FB), lambda j, i: (0, j, 0, 0)),
                  pl.BlockSpec((2, 1, 3, FB), lambda j, i: (0, j, 0, 0)),
                  pl.BlockSpec((2, 1, 1, FB), lambda j, i: (0, j, 0, 0))],
        out_specs=[pl.BlockSpec((2, 1, tm, FB), lambda j, i: (0, j, i, 0)), pl.BlockSpec((2, 1, tm, FB), lambda j, i: (0, j, i, 0)),
                   pl.BlockSpec((1, tm, FB), lambda j, i: (j, i, 0))],
        scratch_shapes=[pltpu.VMEM((2, tm + 8, FB), F32), pltpu.VMEM((tm + 8, D), BF16)],
        compiler_params=_cp("parallel", "parallel"))(h2, h2, w_up, fcw, fcb)


def _final_loss(x, tgt, g):
    T = x.shape[0]
    tm = min(512, T)

    def body(x_ref, t_ref, g_ref, dx_ref, loss_ref, dg_ref):
        @pl.when(pl.program_id(0) == 0)
        def _():
            loss_ref[...] = jnp.zeros_like(loss_ref)
            dg_ref[...] = jnp.zeros_like(dg_ref)

        xv, gv = x_ref[...], g_ref[...]
        r = lax.rsqrt(jnp.mean(xv * xv, axis=-1, keepdims=True) + EPS)
        e = xv * r * gv - t_ref[...]
        loss_ref[...] += 0.5 * jnp.sum(jnp.mean(e * e, axis=-1, keepdims=True), axis=0, keepdims=True)
        dx, dg = _rms_bwd(e * (1.0 / D), xv, gv)
        dx_ref[...] = dx
        dg_ref[...] += dg

    return pl.pallas_call(
        body, name="final_loss", grid=(T // tm,), out_shape=[_sds((T, D), F32), _sds((1, 128), F32), _sds((1, D), F32)],
        in_specs=[pl.BlockSpec((tm, D), lambda i: (i, 0)), pl.BlockSpec((tm, D), lambda i: (i, 0)), pl.BlockSpec((1, D), lambda i: (0, 0))],
        out_specs=[pl.BlockSpec((tm, D), lambda i: (i, 0)), pl.BlockSpec((1, 128), lambda i: (0, 0)), pl.BlockSpec((1, D), lambda i: (0, 0))],
        compiler_params=_cp("arbitrary"))(x, tgt, g)


def _down_bwd(dx2, w_down, u, c, fcw, S):
    T = dx2.shape[0]
    tm = min(512, S)
    nps = S // tm
    nblk8 = T // 8

    def body(dx_ref, dxn_ref, wd_ref, ut_ref, ct_ref, cn_ref, cw_ref, du_ref, dcw_ref, dxb, dcbuf, dsh):
        i = pl.program_id(1)
        last = (i % nps) == nps - 1

        @pl.when(i == 0)
        def _():
            dcw_ref[...] = jnp.zeros_like(dcw_ref)

        dxb[0:tm, :] = dx_ref[...].astype(BF16)
        dxb[tm:tm + 8, :] = jnp.where(last, 0.0, dxn_ref[...]).astype(BF16)
        dh = _dot_nt(dxb[...], wd_ref[0])
        ce = [jnp.concatenate([ct_ref[half, 0], cn_ref[half, 0]], axis=0) for half in range(2)]
        sg = _sigmoid(ce[0])
        dc = [dh * ce[1] * (sg * (1.0 + ce[0] * (1.0 - sg))), dh * (ce[0] * sg)]
        for half in range(2):
            cw = cw_ref[half, 0]
            dcbuf[...] = dc[half]
            for k in range(2):
                dsh[k] = dcbuf[pl.ds(k + 1, tm), :]
            ahead = [dc[half][0:tm], dsh[0], dsh[1]]
            du_ref[half, 0] = (cw[2:3] * ahead[0] + cw[1:2] * ahead[1] + cw[0:1] * ahead[2]).astype(BF16)
            uv = ut_ref[half, 0]
            for k in range(3):
                dcw_ref[half, 0, k:k + 1, :] += jnp.sum(ahead[2 - k] * uv, axis=0, keepdims=True)
            dcw_ref[half, 0, 3:4, :] += jnp.sum(ahead[0], axis=0, keepdims=True)

    ublk = lambda rows, imap: pl.BlockSpec((2, 1, rows, FB), imap)
    return pl.pallas_call(
        body, name="down_bwd", grid=(NJ, T // tm), out_shape=[_sds((2, NJ, T, FB), BF16), _sds((2, NJ, 8, FB), F32)],
        in_specs=[pl.BlockSpec((tm, D), lambda j, i: (i, 0)),
                  pl.BlockSpec((8, D), lambda j, i: (jnp.minimum((i + 1) * (tm // 8), nblk8 - 1), 0)),
                  pl.BlockSpec((1, FB, D), lambda j, i: (j, 0, 0)),
                  ublk(tm, lambda j, i: (0, j, i, 0)),
                  ublk(tm, lambda j, i: (0, j, i, 0)),
                  ublk(8, lambda j, i: (0, j, jnp.minimum((i + 1) * (tm // 8), nblk8 - 1), 0)),
                  pl.BlockSpec((2, 1, 3, FB), lambda j, i: (0, j, 0, 0))],
        out_specs=[ublk(tm, lambda j, i: (0, j, i, 0)), pl.BlockSpec((2, 1, 8, FB), lambda j, i: (0, j, 0, 0))],
        scratch_shapes=[pltpu.VMEM((tm + 8, D), BF16), pltpu.VMEM((tm + 8, FB), F32), pltpu.VMEM((2, tm, FB), F32)],
        compiler_params=_cp("parallel", "arbitrary"))(dx2, dx2, w_down, u, c, c, fcw)


def _wgrad(a, b, tn, name, ga=1, gb=1):
    na, T, ka = a.shape
    nb, _, kb = b.shape
    tk = min(1024, T)
    nt = T // tk

    def body(a_ref, b_ref, o_ref, acc):
        t = pl.program_id(3)

        @pl.when(t == 0)
        def _():
            acc[...] = jnp.zeros_like(acc)

        bs = [b_ref[ib].astype(BF16) for ib in range(gb)]
        for ia in range(ga):
            av = a_ref[ia]
            for ib in range(gb):
                acc[ia, ib] += _dot_tn(av, bs[ib])

        @pl.when(t == nt - 1)
        def _():
            o_ref[...] = acc[...].astype(GRAD_DTYPE)

    return pl.pallas_call(
        body, name=name, grid=(na // ga, nb // gb, kb // tn, nt), out_shape=_sds((na, nb, ka, kb), GRAD_DTYPE),
        in_specs=[pl.BlockSpec((ga, tk, ka), lambda ia, ib, n, t: (ia, t, 0)), pl.BlockSpec((gb, tk, tn), lambda ia, ib, n, t: (ib, t, n))],
        out_specs=pl.BlockSpec((ga, gb, ka, tn), lambda ia, ib, n, t: (ia, ib, 0, n)),
        scratch_shapes=[pltpu.VMEM((ga, gb, ka, tn), F32)],
        compiler_params=_cp("parallel", "parallel", "parallel", "arbitrary"))(a, b)


def _dx_rms(dy, w, x, g, dres, blocked, name):
    T = x.shape[0]
    tm = min(256 if blocked else 512, T)

    def body(dy_ref, w_ref, x_ref, g_ref, r_ref, dx_ref, dg_ref):
        @pl.when(pl.program_id(0) == 0)
        def _():
            dg_ref[...] = jnp.zeros_like(dg_ref)

        if blocked:
            dh = _dot_nt(dy_ref[0, 0], w_ref[0, 0])
            for half, k in [(h, k) for k in range(NJ) for h in range(2)][1:]:
                dh = dh + _dot_nt(dy_ref[half, k], w_ref[half, k])
        else:
            dh = _dot_nt(dy_ref[...], w_ref[...])
        dx, dg = _rms_bwd(dh, x_ref[...], g_ref[...])
        dx_ref[...] = r_ref[...] + dx
        dg_ref[...] += dg

    if blocked:
        dy_spec = pl.BlockSpec((2, NJ, tm, FB), lambda i: (0, 0, i, 0))
        w_spec = pl.BlockSpec((2, NJ, D, FB), lambda i: (0, 0, 0, 0))
    else:
        dy_spec = pl.BlockSpec((tm, dy.shape[1]), lambda i: (i, 0))
        w_spec = pl.BlockSpec(w.shape, lambda i: (0, 0))
    tile = pl.BlockSpec((tm, D), lambda i: (i, 0))
    return pl.pallas_call(
        body, name=name, grid=(T // tm,), out_shape=[_sds((T, D), F32), _sds((1, D), F32)],
        in_specs=[dy_spec, w_spec, tile, pl.BlockSpec((1, D), lambda i: (0, 0)), tile],
        out_specs=[tile, pl.BlockSpec((1, D), lambda i: (0, 0))],
        compiler_params=pltpu.CompilerParams(dimension_semantics=("arbitrary",), vmem_limit_bytes=VMEM_LIMIT_BIG))(dy, w, x, g, dres)


def _mm_nt(a, w, name):
    T = a.shape[0]
    tm = min(512, T)

    def body(a_ref, w_ref, o_ref):
        o_ref[...] = _dot_nt(a_ref[...].astype(BF16), w_ref[...])

    return pl.pallas_call(
        body, name=name, grid=(T // tm,), out_shape=_sds((T, D), F32),
        in_specs=[pl.BlockSpec((tm, D), lambda i: (i, 0)), pl.BlockSpec((D, D), lambda i: (0, 0))],
        out_specs=pl.BlockSpec((tm, D), lambda i: (i, 0)), compiler_params=_cp("parallel"))(a, w)


def _mix_bwd(dy, o_a, o_b, p_rest, cv, g_a, g_b, cw, ng, nbias, S):
    T = dy.shape[0]
    tm = min(256, S)
    nps = S // tm
    HALO = 32
    nblkh = T // HALO
    RE = tm + HALO

    def body(dy_ref, dyn_ref, oa_ref, ob_ref, pt_ref, cvt_ref, cvn_ref, ga_ref, gb_ref, cw_ref, ng_ref, nb_ref,
             doa_ref, dda_ref, dob_ref, ddb_ref, dg_ref, dgn_ref, dcw_ref, dcn_ref, dvbuf, dph):
        i = pl.program_id(0)
        last = (i % nps) == nps - 1

        @pl.when(i == 0)
        def _():
            dgn_ref[...] = jnp.zeros_like(dgn_ref)
            dcw_ref[...] = jnp.zeros_like(dcw_ref)
            dcn_ref[...] = jnp.zeros_like(dcn_ref)

        head_of = (lax.broadcasted_iota(jnp.int32, (8, WA), 1) // HD == lax.broadcasted_iota(jnp.int32, (8, WA), 0)).astype(F32)
        for n, (o_ref, g_ref, do_ref, dd_ref) in enumerate(((oa_ref, ga_ref, doa_ref, dda_ref), (ob_ref, gb_ref, dob_ref, ddb_ref))):
            o = o_ref[...]
            do, dg = _rms_bwd(dy_ref[:, n * WA:(n + 1) * WA], o, g_ref[...])
            dob = do.astype(BF16)
            do_ref[...] = dob
            ddt = _dot_nt(head_of, dob.astype(F32) * o, HIGHEST)
            for hd in range(8):
                dd_ref[0, hd] = ddt[hd:hd + 1, :]
            dgn_ref[n:n + 1, :] += dg

        pt = pt_ref[...]
        gv, sgg = pt[:, :CC], _sigmoid(pt[:, CC:])
        glu = gv * sgg
        cv = jnp.concatenate([cvt_ref[...], cvn_ref[...]], axis=0)
        xc = cv - jnp.mean(cv, axis=-1, keepdims=True)
        r = lax.rsqrt(jnp.mean(xc * xc, axis=-1, keepdims=True) + EPS)
        xh = xc * r
        lo = xh * ng_ref[...] + nb_ref[...]
        sg = _sigmoid(lo)
        dyc = jnp.concatenate([dy_ref[:, 2 * WA:], jnp.where(last, 0.0, dyn_ref[...])], axis=0)
        dlo = dyc * (sg * (1.0 + lo * (1.0 - sg)))
        dcn_ref[0:1, :] += jnp.sum(dlo[0:tm] * xh[0:tm], axis=0, keepdims=True)
        dcn_ref[1:2, :] += jnp.sum(dlo[0:tm], axis=0, keepdims=True)
        dxh = dlo * ng_ref[...]
        dcv = r * (dxh - jnp.mean(dxh, axis=-1, keepdims=True) - xh * jnp.mean(dxh * xh, axis=-1, keepdims=True))
        dvbuf[...] = dcv
        _phase_copies(dvbuf, dph, RE)
        dct = dcv[0:tm]
        dcw_ref[CONV_K:CONV_K + 1, :] += jnp.sum(dct, axis=0, keepdims=True)
        dglu = jnp.zeros((tm, CC), F32)
        for k in range(CONV_K):
            ahead = _tap(dvbuf, dph, CONV_K - 1 - k, tm)
            dcw_ref[k:k + 1, :] += jnp.sum(ahead * glu, axis=0, keepdims=True)
            dglu = dglu + cw_ref[k:k + 1, :] * ahead
        dg_ref[:, :CC] = (dglu * sgg).astype(BF16)
        dg_ref[:, CC:] = (dglu * gv * sgg * (1.0 - sgg)).astype(BF16)

    row = lambda w: pl.BlockSpec((1, w), lambda i: (0, 0))
    tile = lambda w: pl.BlockSpec((tm, w), lambda i: (i, 0))
    nxt = lambda i: jnp.minimum((i + 1) * (tm // HALO), nblkh - 1)
    const = lambda r, w: pl.BlockSpec((r, w), lambda i: (0, 0))
    ddrow = pl.BlockSpec((1, 8, 1, tm), lambda i: (i, 0, 0, 0))
    return pl.pallas_call(
        body, name="mix_bwd", grid=(T // tm,),
        out_shape=[_sds((T, WA), BF16), _sds((T // tm, 8, 1, tm), F32), _sds((T, WA), BF16), _sds((T // tm, 8, 1, tm), F32),
                   _sds((T, 2 * CC), BF16), _sds((8, WA), F32), _sds((32, CC), F32), _sds((8, CC), F32)],
        in_specs=[tile(D), pl.BlockSpec((HALO, CC), lambda i: (nxt(i), 3)), tile(WA), tile(WA), tile(2 * CC),
                  tile(CC), pl.BlockSpec((HALO, CC), lambda i: (nxt(i), 0)),
                  row(WA), row(WA), const(32, CC), row(CC), row(CC)],
        out_specs=[tile(WA), ddrow, tile(WA), ddrow, tile(2 * CC), const(8, WA), const(32, CC), const(8, CC)],
        scratch_shapes=[pltpu.VMEM((RE, CC), F32), pltpu.VMEM((8, RE, CC), F32)],
        compiler_params=_cp("arbitrary"))(dy, dy, o_a, o_b, p_rest, cv, cv, g_a, g_b, cw, ng, nbias)


def _attn_bwd(p_qkv, col0, tab, do, lse, dd, kat, S, aq=None, ak=None, name="attn_bwd"):
    T = p_qkv.shape[0]
    n_tab, TK, TB = tab.shape
    nb, nkb, r = S // TB, S // TK, TK // TB
    B = T // S
    fox = aq is not None

    def body(*refs):
        if fox:
            (q_ref, k_ref, v_ref, tab_ref, do_ref, l_ref, dd_ref, kat_ref, aq_ref, ak_ref,
             dq_ref, dk_ref, dv_ref, dqa_ref, dka_ref, dqt_scr, dk_scr, dv_scr, sd_scr) = refs
        else:
            (q_ref, k_ref, v_ref, tab_ref, do_ref, l_ref, dd_ref, kat_ref,
             dq_ref, dk_ref, dv_ref, dqt_scr, dk_scr, dv_scr, sd_scr) = refs
        j = pl.program_id(2)
        hms = _head_masks()
        first_q_block = r * j

        @pl.when(j == 0)
        def _():
            dqt_scr[...] = jnp.zeros_like(dqt_scr)

        dk_scr[...] = jnp.zeros_like(dk_scr)
        dv_scr[...] = jnp.zeros_like(dv_scr)
        k2, v2 = k_ref[...], v_ref[...]
        ka = [jnp.where(hm, k2, ak_ref[0] if fox else jnp.zeros_like(k2)) for hm in hms]
        kat = [jnp.concatenate([kat_ref[0, 0, hh, t] for t in range(r)], axis=1) if r > 1 else kat_ref[0, 0, hh, 0]
               for hh in range(2)]

        def operands(i):
            ic = jnp.minimum(i, nb - 1)
            rows = pl.ds(pl.multiple_of(ic * TB, TB), TB)
            q2 = q_ref[rows, :] * 0.125
            do2 = do_ref[rows, :]
            qas = [jnp.where(hms[hh], q2, aq_ref[0, rows, :] if fox else jnp.zeros_like(q2)) for hh in range(2)]
            dohs = [jnp.where(hms[hh], do2, jnp.zeros_like(do2)) for hh in range(2)]
            return ic, qas, dohs

        def scores(i):
            ic, qas, dohs = operands(i)
            bias = tab_ref[jnp.where(i < nb, i - first_q_block, n_tab - 1)]
            return [(_dot_nt(ka[hh], qas[hh]) + bias, _dot_nt(v2, dohs[hh])) for hh in range(2)]

        def absorb(i, sd):
            ic, qas, dohs = operands(i)
            for hh in range(2):
                st, dpt = sd[hh]
                pt = jnp.exp(st - l_ref[ic, hh])
                dsb = (pt * (dpt - dd_ref[ic, hh])).astype(BF16)
                dv_scr[...] += _dot(pt.astype(BF16), dohs[hh])
                dk_scr[hh] += _dot(dsb, qas[hh])
                dqt_scr[hh, ic] += _dot(kat[hh], dsb)

        def q_pair(ii, carry):
            i0 = first_q_block + 2 * ii
            sd0 = [(sd_scr[hh, 0], sd_scr[hh, 1]) for hh in range(2)]
            sd1 = scores(i0 + 1)
            absorb(i0, sd0)
            nxt = scores(i0 + 2)
            for hh in range(2):
                sd_scr[hh, 0], sd_scr[hh, 1] = nxt[hh]
            absorb(i0 + 1, sd1)
            return carry

        first = scores(first_q_block)
        for hh in range(2):
            sd_scr[hh, 0], sd_scr[hh, 1] = first[hh]
        lax.fori_loop(0, (nb - first_q_block + 1) // 2, q_pair, 0)
        dk_ref[...] = jnp.where(hms[0], dk_scr[0], dk_scr[1]).astype(BF16)
        dv_ref[...] = dv_scr[...].astype(BF16)
        if fox:
            dka_ref[...] = jnp.where(hms[0], dk_scr[1], dk_scr[0])

        @pl.when(j == nkb - 1)
        def _():
            for ii in range(nb):
                t0, t1 = dqt_scr[0, ii].T, dqt_scr[1, ii].T
                dq_ref[ii * TB:(ii + 1) * TB, :] = jnp.where(hms[0], t0, t1) * 0.125
                if fox:
                    dqa_ref[ii * TB:(ii + 1) * TB, :] = jnp.where(hms[0], t1, t0)

    seq = lambda c: pl.BlockSpec((S, 128), lambda b, p, j: (b, c + p))
    kvb = lambda c: pl.BlockSpec((TK, 128), lambda b, p, j: (b * nkb + j, c + p))
    stat = pl.BlockSpec((nb, 2, 1, TB), lambda b, p, j: (b, p, 0, 0))
    in_specs = [seq(col0), kvb(col0 + NPAIR), kvb(col0 + 2 * NPAIR), pl.BlockSpec(tab.shape, lambda b, p, j: (0, 0, 0)),
                seq(0), stat, stat, pl.BlockSpec((1, 1, 2, r, 128, TB), lambda b, p, j: (b, p, 0, j, 0, 0))]
    args = [p_qkv, p_qkv, p_qkv, tab, do, lse, dd, kat]
    out_shape = [_sds((T, WA), F32), _sds((T, WA), BF16), _sds((T, WA), BF16)]
    out_specs = [seq(0), kvb(0), kvb(0)]
    if fox:
        in_specs += [pl.BlockSpec((1, S, 128), lambda b, p, j: (p, b, 0)), pl.BlockSpec((1, TK, 128), lambda b, p, j: (p, b * nkb + j, 0))]
        args += [aq, ak]
        out_shape += [_sds((T, WA), F32), _sds((T, WA), F32)]
        out_specs += [seq(0), kvb(0)]
    return pl.pallas_call(
        body, name=name, grid=(B, NPAIR, nkb), out_shape=out_shape, in_specs=in_specs, out_specs=out_specs,
        scratch_shapes=[pltpu.VMEM((2, nb, 128, TB), F32), pltpu.VMEM((2, TK, 128), F32), pltpu.VMEM((TK, 128), F32),
                        pltpu.VMEM((2, 2, TK, TB), F32)],
        compiler_params=_cp("parallel", "parallel", "arbitrary"))(*args)


def _gate_bwd(dqa, dka, p_rest, bf, S, TB):
    T = p_rest.shape[0]
    B, nb = T // S, S // TB

    def body(dqa_ref, dka_ref, z_ref, b_ref, dz_ref, db_ref):
        @pl.when(pl.program_id(0) == 0)
        def _():
            db_ref[...] = jnp.zeros_like(db_ref)

        later = (lax.broadcasted_iota(jnp.int32, (TB, TB), 1) >= lax.broadcasted_iota(jnp.int32, (TB, TB), 0)).astype(F32)
        lane = lax.broadcasted_iota(jnp.int32, (1, 128), 1)
        src, head = lax.broadcasted_iota(jnp.int32, (WA, 128), 0), lax.broadcasted_iota(jnp.int32, (WA, 128), 1)
        spare0 = 128 * (head // 2) + HD * (1 - head % 2)
        pick_q = (src == spare0 + AUG_K1).astype(F32)
        pick_k = (src == spare0 + AUG_Q1).astype(F32)
        carry = jnp.zeros((1, 128), F32)
        dbs = jnp.zeros((1, 128), F32)
        for j in reversed(range(nb)):
            rows = slice(j * TB, (j + 1) * TB)
            dc = _dot(dqa_ref[rows, :], pick_q, HIGHEST) - _dot(dka_ref[rows, :], pick_k, HIGHEST)
            part = _dot(later, dc, HIGHEST)
            tot = part + carry
            carry = carry + part[0:1, :]
            z = z_ref[j * TB:(j + 1) * TB, :] + b_ref[...]
            dz = jnp.where(lane < 6, tot * _sigmoid(-z), 0.0)
            dz_ref[j * TB:(j + 1) * TB, :] = dz.astype(BF16)
            dbs = dbs + jnp.sum(dz, axis=0, keepdims=True)
        db_ref[...] += dbs

    return pl.pallas_call(
        body, name="gate_bwd", grid=(B,), out_shape=[_sds((T, 128), BF16), _sds((1, 128), F32)],
        in_specs=[pl.BlockSpec((S, WA), lambda b: (b, 0)), pl.BlockSpec((S, WA), lambda b: (b, 0)),
                  pl.BlockSpec((S, 128), lambda b: (b, 4)), pl.BlockSpec((1, 128), lambda b: (0, 0))],
        out_specs=[pl.BlockSpec((S, 128), lambda b: (b, 0)), pl.BlockSpec((1, 128), lambda b: (0, 0))],
        compiler_params=_cp("arbitrary"))(dqa, dka, p_rest, bf)


def _adamw(parts, w, m, v, name, row0=0, prev=None):
    npart, Rp, C = parts.shape
    R = w.shape[0]
    tr = Rp
    for cand in (256, 128, 64, 32, 16, 8):
        if Rp % cand == 0 and row0 % cand == 0 and cand * C * 4 <= (1 << 20):
            tr = cand
            break
    if Rp == R and R * C * 4 <= (1 << 20):
        tr = R
    assert Rp % tr == 0 and row0 % tr == 0, (name, Rp, row0, tr)
    b0 = row0 // tr

    def body(p_ref, w_ref, m_ref, v_ref, *rest):
        g_ref, d_ref, mo_ref, vo_ref = rest[-4:]
        g = p_ref[0].astype(F32)
        for k in range(1, npart):
            g = g + p_ref[k].astype(F32)
        mn = ADAM_B1 * m_ref[...] + (1.0 - ADAM_B1) * g
        vn = ADAM_B2 * v_ref[...] + (1.0 - ADAM_B2) * (g * g)
        m_hat = mn / (1.0 - ADAM_B1 ** ADAM_STEP)
        v_hat = vn / (1.0 - ADAM_B2 ** ADAM_STEP)
        g_ref[...] = g
        d_ref[...] = -ADAM_LR * (m_hat / (jnp.sqrt(v_hat) + ADAM_EPS) + ADAM_WD * w_ref[...])
        mo_ref[...] = mn
        vo_ref[...] = vn

    blk = pl.BlockSpec((tr, C), lambda i: (b0 + i, 0))
    prev = list(prev) if prev is not None else []
    return pl.pallas_call(
        body, name=name, grid=(Rp // tr,), out_shape=[_sds((R, C), F32)] * 4,
        in_specs=[pl.BlockSpec((npart, tr, C), lambda i: (0, i, 0)), blk, blk, blk] + [_ANY] * len(prev), out_specs=[blk] * 4,
        input_output_aliases={4 + i: i for i in range(len(prev))},
        compiler_params=_cp("parallel"))(parts, w, m, v, *prev)


def _pad_w_in(w):
    z = jnp.zeros(w.shape[:-1] + (NPAD - N_IN,), w.dtype)
    return jnp.concatenate([w[..., :O_FA], w[..., O_QB:], w[..., O_FA:O_QB], z], axis=-1)


def _unpad_w_in(w):
    n_mid = N_IN - O_QB
    return jnp.concatenate([w[..., :O_FA], w[..., O_FA + n_mid:O_FA + n_mid + 6], w[..., O_FA:O_FA + n_mid]], axis=-1)


def _local_step(x, tgt, W, S, late_weights=None, on_layer_grads=None, on_ffn_grads=None):
    T = x.shape[0]
    TB = min(256, S)
    TW = min(2 * TB, S)
    tab_fox, tab_dil = _bias_table(S, TW, TB, True), _bias_table(S, TW, TB, False)
    tabq_fox, tabq_dil = _bias_table(S, TB, TW, True), _bias_table(S, TB, TW, False)
    saved = []
    h = _rms_fwd(x, W["ln1_g"][0])
    for l in range(DEPTH):
        if l > 0 and late_weights is not None:
            W = late_weights(l, W, h)
        p_qkv, p_rest = _mm_in(h, W["w_in"][l])
        aq, ak = _gate_fwd(p_rest, W["b_forget"][l], S, TB)
        o_a, lse_a, kat_a = _attn_fwd(p_qkv, 0, tabq_fox, S, aq, ak, name="fox_fwd")
        o_b, lse_b, kat_b = _attn_fwd(p_qkv, 3 * NPAIR, tabq_dil, S, name="dil_fwd")
        y, cv = _mix_fwd(o_a, o_b, p_rest, W["g_out_fox"][l], W["g_out_dil"][l], W["conv_w"][l], W["conv_b"][l],
                     W["cnorm_g"][l], W["cnorm_b"][l], S)
        if l == 0 and late_weights is not None:
            W = late_weights(0, W, y)
        x1, h2 = _mm_res(y[None], W["w_o"][l][None], x, W["ln2_g"][l], name="mm_o")
        u, c, hid = _up_fwd(h2, W["w_up"][l], W["ffn_conv_w"][l], W["ffn_conv_b"][l], S)
        g_next = W["ln1_g"][l + 1] if l + 1 < DEPTH else W["g_final"]
        x2, h_next = _mm_res(hid, W["w_down"][l], x1, g_next, name="mm_down")
        saved.append(dict(x=x, h=h, p_qkv=p_qkv, p_rest=p_rest, aq=aq, ak=ak, o_a=o_a, lse_a=lse_a, o_b=o_b,
                          lse_b=lse_b, kat_a=kat_a, kat_b=kat_b, y=y, cv=cv, x1=x1, h2=h2, u=u, c=c, hid=hid))
        x, h = x2, h_next
    dx, loss, dg_final = _final_loss(x, tgt, W["g_final"])
    G = {k: [None] * DEPTH for k in ("ln1_g", "w_in", "b_forget", "g_out_fox", "g_out_dil", "conv_w", "conv_b", "cnorm_g",
                                     "cnorm_b", "w_o", "ln2_g", "w_up", "ffn_conv_w", "ffn_conv_b", "w_down")}
    G["g_final"] = dg_final
    for l in reversed(range(DEPTH)):
        sv = saved[l]
        du, dcw = _down_bwd(dx, W["w_down"][l], sv["u"], sv["c"], W["ffn_conv_w"][l], S)
        G["w_down"][l] = _wgrad(sv["hid"], dx[None], D, "wg_down", ga=NJ)[:, 0]
        G["ffn_conv_w"][l] = dcw[:, :, 0:3, :]
        G["ffn_conv_b"][l] = dcw[:, :, 3, :]
        G["w_up"][l] = _wgrad(du.reshape(2 * NJ, T, FB), sv["h2"][None], D, "wg_up", ga=NJ)[:, 0]
        token = on_ffn_grads(l, G) if on_ffn_grads is not None else None
        ln2 = W["ln2_g"][l] if token is None else W["ln2_g"][l] + token[0, 0]
        dx1, G["ln2_g"][l] = _dx_rms(du, W["w_up"][l], sv["x1"], ln2, dx, True, "dx_up")
        G["w_o"][l] = _wgrad(sv["y"][None], dx1[None], D, "wg_o")[0, 0]
        dy = _mm_nt(dx1, W["w_o"][l], "mm_dy")
        do_a, dd_a, do_b, dd_b, dgvgg, dgn, dcvw, dcn = _mix_bwd(
            dy, sv["o_a"], sv["o_b"], sv["p_rest"], sv["cv"], W["g_out_fox"][l], W["g_out_dil"][l], W["conv_w"][l],
            W["cnorm_g"][l], W["cnorm_b"][l], S)
        G["g_out_fox"][l], G["g_out_dil"][l] = dgn[0], dgn[1]
        G["conv_w"][l], G["conv_b"][l] = dcvw[:CONV_K], dcvw[CONV_K]
        G["cnorm_g"][l], G["cnorm_b"][l] = dcn[0], dcn[1]
        dq_a, dk_a, dv_a, dqa, dka = _attn_bwd(sv["p_qkv"], 0, tab_fox, do_a, sv["lse_a"], dd_a, sv["kat_a"], S, sv["aq"], sv["ak"],
                                               name="fox_bwd")
        dq_b, dk_b, dv_b = _attn_bwd(sv["p_qkv"], 3 * NPAIR, tab_dil, do_b, sv["lse_b"], dd_b, sv["kat_b"], S, name="dil_bwd")
        dfa, db = _gate_bwd(dqa, dka, sv["p_rest"], W["b_forget"][l], S, TB)
        G["b_forget"][l] = db[0, :6]
        dp = jnp.concatenate([dq_a.astype(BF16), dk_a, dv_a, dq_b.astype(BF16), dk_b, dv_b, dgvgg, dfa,
                              jnp.zeros((T, 128), BF16)], axis=1)
        G["w_in"][l] = _wgrad(sv["h"][None], dp[None], NPAD, "wg_in")[0, 0]
        token = on_layer_grads(l, G) if on_layer_grads is not None else None
        ln1 = W["ln1_g"][l] if token is None else W["ln1_g"][l] + token[0, 0]
        dx, G["ln1_g"][l] = _dx_rms(dp, W["w_in"][l], sv["x"], ln1, dx1, False, "dx_in")
    return loss, dx, G


_MATMUL_W = ("w_in", "w_o", "w_up", "w_down")
_SHARDED = _MATMUL_W + ("conv_w", "ffn_conv_w")
_SMALL = ("ln1_g", "b_forget", "g_out_fox", "g_out_dil", "conv_b", "cnorm_g", "cnorm_b", "ln2_g", "ffn_conv_b", "g_final")
_ORDER = ("ln1_g", "w_in", "b_forget", "g_out_fox", "g_out_dil", "conv_w", "conv_b", "cnorm_g", "cnorm_b", "w_o", "ln2_g",
          "w_up", "ffn_conv_w", "ffn_conv_b", "w_down", "g_final")


def _kernel_ready(k, zone):
    if k == "w_in":
        return zone.reshape(D, NPAD)
    if k == "w_o":
        return zone.reshape(D, D)
    if k == "w_up":
        return zone.reshape(2, NJ, D, FB)
    if k == "w_down":
        return zone.reshape(NJ, FB, D)
    if k == "conv_w":
        cw = jnp.transpose(zone.reshape(NDEV, CONV_K, CC // NDEV), (1, 0, 2)).reshape(CONV_K, CC)
        return jnp.concatenate([cw, jnp.zeros((1, CC), F32)], axis=0)
    return zone.reshape(2, NJ, 3, FB)


def _full_weights(P, gathered):
    W = {}
    row = lambda a: [a[l][None, :] for l in range(DEPTH)]
    for k in _SHARDED:
        W[k] = [None if gathered[k][l] is None else _kernel_ready(k, gathered[k][l]) for l in range(DEPTH)]
    W["ffn_conv_b"] = [P["ffn_conv_b"][l].reshape(2, NJ, 1, FB) for l in range(DEPTH)]
    W["b_forget"] = [jnp.concatenate([P["b_forget"][l], jnp.zeros((128 - 6,), F32)])[None, :] for l in range(DEPTH)]
    for k in ("ln1_g", "ln2_g", "g_out_fox", "g_out_dil", "conv_b", "cnorm_g", "cnorm_b"):
        W[k] = row(P[k])
    W["g_final"] = P["g_final"][None, :]
    return W


def kernel(x, ln1_g, w_in, b_forget, g_out_fox, g_out_dil, conv_w, conv_b, cnorm_g, cnorm_b, w_o, ln2_g, w_up, ffn_conv_w, ffn_conv_b, w_down, g_final, loss_target, m_ln1_g, m_w_in, m_b_forget, m_g_out_fox, m_g_out_dil, m_conv_w, m_conv_b, m_cnorm_g, m_cnorm_b, m_w_o, m_ln2_g, m_w_up, m_ffn_conv_w, m_ffn_conv_b, m_w_down, m_g_final, v_ln1_g, v_w_in, v_b_forget, v_g_out_fox, v_g_out_dil, v_conv_w, v_conv_b, v_cnorm_g, v_cnorm_b, v_w_o, v_ln2_g, v_w_up, v_ffn_conv_w, v_ffn_conv_b, v_w_down, v_g_final):
    P = dict(ln1_g=ln1_g, w_in=w_in, b_forget=b_forget, g_out_fox=g_out_fox, g_out_dil=g_out_dil, conv_w=conv_w, conv_b=conv_b,
             cnorm_g=cnorm_g, cnorm_b=cnorm_b, w_o=w_o, ln2_g=ln2_g, w_up=w_up, ffn_conv_w=ffn_conv_w, ffn_conv_b=ffn_conv_b,
             w_down=w_down, g_final=g_final)
    M = dict(ln1_g=m_ln1_g, w_in=m_w_in, b_forget=m_b_forget, g_out_fox=m_g_out_fox, g_out_dil=m_g_out_dil, conv_w=m_conv_w,
             conv_b=m_conv_b, cnorm_g=m_cnorm_g, cnorm_b=m_cnorm_b, w_o=m_w_o, ln2_g=m_ln2_g, w_up=m_w_up, ffn_conv_w=m_ffn_conv_w,
             ffn_conv_b=m_ffn_conv_b, w_down=m_w_down, g_final=m_g_final)
    V = dict(ln1_g=v_ln1_g, w_in=v_w_in, b_forget=v_b_forget, g_out_fox=v_g_out_fox, g_out_dil=v_g_out_dil, conv_w=v_conv_w,
             conv_b=v_conv_b, cnorm_g=v_cnorm_g, cnorm_b=v_cnorm_b, w_o=v_w_o, ln2_g=v_ln2_g, w_up=v_w_up, ffn_conv_w=v_ffn_conv_w,
             ffn_conv_b=v_ffn_conv_b, w_down=v_w_down, g_final=v_g_final)
    Bl, S, _ = x.shape
    T = Bl * S

    def shard(k, l):
        if k == "w_in":
            return _pad_w_in(P[k][l].astype(BF16))
        return P[k][l].astype(BF16) if k in _MATMUL_W else P[k][l]

    early = [("w_in", 0)] + [(k, l) for k in ("conv_w", "ffn_conv_w") for l in range(DEPTH)]
    late = [[(k, 0) for k in ("w_o", "w_up", "w_down")], [(k, 1) for k in _MATMUL_W]]
    gathered = {k: [None] * DEPTH for k in _SHARDED}
    plan_early = _Plan([[shard(k, l)] for k, l in early], True)
    early_zones = _exchange(plan_early, "gather_early")
    for (k, l), zone in zip(early, early_zones):
        gathered[k][l] = zone
    flights, after_prev = [], early_zones[0]
    for stage, group in enumerate(late):
        shards, _ = lax.optimization_barrier(([shard(k, l) for k, l in group], after_prev))
        plan = _Plan([[a] for a in shards], True)
        handle, after_prev = _exchange_start(plan, "gather_late%d_start" % stage, plan.zones_with_own())
        flights.append((group, plan, handle))
    W = _full_weights(P, gathered)
    W["ln1_g"][0] = W["ln1_g"][0] + after_prev[0, 0]

    def late_weights(stage, W, after):
        group, plan, handle = flights[stage]
        zones = _exchange_wait(plan, "gather_late%d_wait" % stage, handle, after)
        W = dict(W)
        for (k, l), zone in zip(group, zones):
            W[k] = list(W[k])
            W[k][l] = _kernel_ready(k, zone)
        return W

    def owner_block(G, l, k):
        if k == "w_in":
            blk = _unpad_w_in(G[k][l]).reshape(NDEV, D // NDEV, N_IN)
        elif k == "w_o":
            blk = G[k][l].reshape(NDEV, D // NDEV, D)
        elif k == "w_up":
            blk = G[k][l]
        elif k == "w_down":
            blk = G[k][l].reshape(NDEV, DFF // NDEV, D)
        elif k == "conv_w":
            blk = jnp.transpose(G[k][l].reshape(CONV_K, NDEV, CC // NDEV), (1, 0, 2))
        else:
            blk = G[k][l].reshape(NDEV, 3, FB)
        return blk.astype(GRAD_DTYPE)

    flying = []

    def start_scatter(tag, G, l, keys):
        plan = _Plan([[owner_block(G, l, k)] for k in keys], False)
        handle, token = _exchange_start(plan, "scatter_%s_start" % tag, plan.zones_with_own())
        flying.append((tag, l, keys, plan, handle))
        return token

    def on_layer_grads(l, G):
        return start_scatter("l1", G, l, _MATMUL_W) if l == 1 else start_scatter("l0_rest", G, l, ("w_in", "w_o"))

    def on_ffn_grads(l, G):
        return start_scatter("l0_ffn", G, l, ("w_up", "w_down")) if l == 0 else None

    loss, dx, G = _local_step(x.reshape(T, D), loss_target.reshape(T, D), W, S, late_weights, on_layer_grads, on_ffn_grads)

    parts = {}
    for tag, l, keys, plan, handle in flying:
        for k, zone in zip(keys, _exchange_wait(plan, "scatter_%s_wait" % tag, handle, dx)):
            parts[(k, l)] = zone
    plan_last = _Plan([[owner_block(G, l, k) for l in range(DEPTH)] for k in ("conv_w", "ffn_conv_w")], False)
    last = _exchange(plan_last, "scatter_last")
    small_parts_sharded = {"conv_w": last[0], "ffn_conv_w": last[1]}

    small = [jnp.stack(G[k]).reshape(-1) for k in _SMALL[:-1]] + [G["g_final"].reshape(-1), loss[0, :1]]
    sizes = [int(s.shape[0]) for s in small]
    flat = jnp.concatenate(small)
    n_small = -(-flat.shape[0] // 1024) * 1024
    flat = jnp.concatenate([flat, jnp.zeros((n_small - flat.shape[0],), F32)]).reshape(n_small // 128, 128)
    small_parts = _exchange(_Plan([[flat]], True), "gather_small")[0]

    out_g, out_d, out_m, out_v = {}, {}, {}, {}
    sg = _sum_parts(small_parts.reshape(NDEV, n_small // 128, 128), "sum_small").reshape(-1)
    off = 0
    summed = {}
    for k, n in zip(_SMALL + ("loss",), sizes):
        summed[k] = sg[off:off + n]
        off += n
    for k in _ORDER:
        shp = P[k].shape
        C = shp[-1]
        if k in _MATMUL_W:
            swap = k == "w_up"
            R0, Cw = (shp[2], shp[1]) if swap else (shp[1], shp[2])
            flat2 = lambda t: (jnp.swapaxes(t, 1, 2) if swap else t).reshape(DEPTH * R0, Cw)
            res = None
            for l in reversed(range(DEPTH)):
                res = _adamw(parts[(k, l)].reshape(NDEV, R0, Cw), flat2(P[k]), flat2(M[k]), flat2(V[k]),
                             "adamw_%s_l%d" % (k, l), row0=l * R0, prev=res)
            if swap:
                res = [jnp.swapaxes(t.reshape(DEPTH, R0, Cw), 1, 2) for t in res]
        else:
            pr = small_parts_sharded[k].reshape((NDEV, -1, C)) if k in _SHARDED else summed[k].reshape((1, -1, C))
            R = pr.shape[1]
            res = _adamw(pr, P[k].reshape(R, -1), M[k].reshape(R, -1), V[k].reshape(R, -1), "adamw_" + k)
        out_g[k], out_d[k], out_m[k], out_v[k] = (t.reshape(shp) for t in res)

    loss_out = summed["loss"].reshape(())
    grad_x = dx.reshape(Bl, S, D)
    return (loss_out, grad_x, *[out_g[k] for k in _ORDER], *[out_d[k] for k in _ORDER], *[out_m[k] for k in _ORDER],
            *[out_v[k] for k in _ORDER])
```

```python
import functools

import numpy as np
import jax
import jax.numpy as jnp
from jax import lax
from jax.experimental import pallas as pl
from jax.experimental.pallas import tpu as pltpu

F32 = jnp.float32
BF16 = jnp.bfloat16
GRAD_DTYPE = BF16
HIGHEST = lax.Precision.HIGHEST

D = 1024
DEPTH = 2
HD = 64
WA = 384
NPAIR = 3
CC = 256
CONV_K = 31
DFF = 2816
FB = 704
NJ = 4
NDEV = 8
EPS = 1e-6
NQKV = 2304
NREST = 768
NPAD = NQKV + NREST
O_FA, O_QB, N_IN = 1152, 1158, 2822
DILATION_PAIRS = ((128, 1), (512, 4), (2048, 16))
NEG = -1e30

ADAM_LR, ADAM_B1, ADAM_B2, ADAM_EPS, ADAM_WD, ADAM_STEP = 0.001, 0.9, 0.999, 1e-08, 0.01, 10

VMEM_LIMIT = 48 * 1024 * 1024
VMEM_LIMIT_BIG = 56 * 1024 * 1024


def _cp(*sem):
    return pltpu.CompilerParams(dimension_semantics=sem, vmem_limit_bytes=VMEM_LIMIT)


def _sds(shape, dtype):
    return jax.ShapeDtypeStruct(shape, dtype)


def _dot(a, b, prec=None):
    return jnp.dot(a, b, preferred_element_type=F32, precision=prec)


def _dot_nt(a, b, prec=None):
    return lax.dot_general(a, b, (((1,), (1,)), ((), ())), preferred_element_type=F32, precision=prec)


def _dot_tn(a, b):
    return lax.dot_general(a, b, (((0,), (0,)), ((), ())), preferred_element_type=F32)


def _sigmoid(z):
    return 0.5 * jnp.tanh(0.5 * z) + 0.5


def _rms_bwd(dh, x, g):
    r = lax.rsqrt(jnp.mean(x * x, axis=-1, keepdims=True) + EPS)
    xh = x * r
    dg = jnp.sum(dh * xh, axis=0, keepdims=True)
    dxh = dh * g
    dx = r * (dxh - xh * jnp.mean(dxh * xh, axis=-1, keepdims=True))
    return dx, dg


class _Plan:
    def __init__(self, groups, gather, slots=None, n_slots=None):
        self.groups, self.gather = groups, gather
        self.slots = slots or [list(range(len(g))) for g in groups]
        self.n_slots = n_slots or [len(g) for g in groups]
        self.flat = [a for g in groups for a in g]
        self.where = [(gi, s) for gi, g in enumerate(groups) for s in self.slots[gi]]
        self.zone_shapes = [_sds((NDEV, ns) + (g[0].shape if gather else g[0].shape[1:]), g[0].dtype)
                            for g, ns in zip(groups, self.n_slots)]

    def new_zones(self):
        return [lax.empty(z.shape, z.dtype) for z in self.zone_shapes]

    def me(self):
        x, y, c = lax.axis_index("x"), lax.axis_index("y"), lax.axis_index("c")
        return 4 * x + 2 * y + c

    def zones_with_own(self):
        me = self.me()
        zones = self.new_zones()
        for a, (gi, s) in enumerate(self.where):
            src = self.flat[a] if self.gather else lax.dynamic_index_in_dim(self.flat[a], me, 0, keepdims=False)
            zones[gi] = lax.dynamic_update_slice(zones[gi], src[None, None], (me, s) + (0,) * src.ndim)
        return zones

    def own_copies(self, ins, zones, sems):
        me = self.me()
        return [pltpu.make_async_copy(ins[a] if self.gather else ins[a].at[me], zones[gi].at[me, s], sems.at[a])
                for a, (gi, s) in enumerate(self.where)]

    def remote_copies(self, ins, zones, send_sems, recv_sems):
        x, y, c = lax.axis_index("x"), lax.axis_index("y"), lax.axis_index("c")
        me = 4 * x + 2 * y + c
        out = []
        for a, (gi, s) in enumerate(self.where):
            for k in range(1, NDEV):
                px, py, pc = x ^ ((k >> 2) & 1), y ^ ((k >> 1) & 1), c ^ (k & 1)
                out.append(pltpu.make_async_remote_copy(
                    src_ref=ins[a] if self.gather else ins[a].at[4 * px + 2 * py + pc], dst_ref=zones[gi].at[me, s],
                    send_sem=send_sems.at[a * (NDEV - 1) + k - 1], recv_sem=recv_sems.at[a * (NDEV - 1) + k - 1],
                    device_id=(px, py, pc), device_id_type=pl.DeviceIdType.MESH))
        return out


_ANY = pl.BlockSpec(memory_space=pl.ANY)
_HBM = pl.BlockSpec(memory_space=pltpu.HBM)
_SEM = pl.BlockSpec(memory_space=pltpu.SEMAPHORE)


def _exchange(plan, name, zones=None):
    n, nz = len(plan.flat), len(plan.groups)
    zones = zones or plan.new_zones()

    def body(*refs):
        ins, zin = refs[:n], refs[n:n + nz]
        send_sems, recv_sems, local_sems = refs[n + 2 * nz:]
        copies = plan.own_copies(ins, zin, local_sems) + plan.remote_copies(ins, zin, send_sems, recv_sems)
        for cp in copies:
            cp.start()
        for cp in copies:
            cp.wait()

    return pl.pallas_call(
        body, name=name, out_shape=plan.zone_shapes, in_specs=[_ANY] * (n + nz), out_specs=[_ANY] * nz,
        input_output_aliases={n + g: g for g in range(nz)},
        scratch_shapes=[pltpu.SemaphoreType.DMA((n * (NDEV - 1),)), pltpu.SemaphoreType.DMA((n * (NDEV - 1),)),
                        pltpu.SemaphoreType.DMA((n,))],
        compiler_params=pltpu.CompilerParams(has_side_effects=True),
    )(*plan.flat, *zones)


def _gather_two_level(plan, name):
    n, nz = len(plan.flat), len(plan.groups)
    assert plan.gather
    zones = plan.new_zones()

    def body(*refs):
        ins, zin = refs[:n], refs[n:n + nz]
        send_sems, recv_sems, local_sems = refs[n + 2 * nz:]
        x, y, c = lax.axis_index("x"), lax.axis_index("y"), lax.axis_index("c")
        me, sibling = (x, y, c), (x, y, 1 - c)
        chips = [(1 - x, y), (x, 1 - y), (1 - x, 1 - y)]

        def copy(a, k, block, to, src=None):
            gi, s = plan.where[a]
            dst = zin[gi].at[4 * block[0] + 2 * block[1] + block[2], s]
            return pltpu.make_async_remote_copy(
                src_ref=dst if src is None else src, dst_ref=dst,
                send_sem=send_sems.at[a * (NDEV - 1) + k], recv_sem=recv_sems.at[a * (NDEV - 1) + k],
                device_id=to, device_id_type=pl.DeviceIdType.MESH)

        owns = plan.own_copies(ins, zin, local_sems)
        first = []
        for a in range(n):
            first.append(copy(a, 0, me, sibling, src=ins[a]))
            first += [copy(a, 1 + j, me, (*chip, c), src=ins[a]) for j, chip in enumerate(chips)]
        for cp in owns + first:
            cp.start()
        passed = []
        for a in range(n):
            for j, chip in enumerate(chips):
                copy(a, 1 + j, (*chip, c), me).wait_recv()
                passed.append(copy(a, 4 + j, (*chip, c), sibling))
                passed[-1].start()
        for a in range(n):
            copy(a, 0, sibling, me).wait_recv()
            for j, chip in enumerate(chips):
                copy(a, 4 + j, (*chip, 1 - c), me).wait_recv()
        for cp in first + passed:
            cp.wait_send()
        for cp in owns:
            cp.wait()

    return pl.pallas_call(
        body, name=name, out_shape=plan.zone_shapes, in_specs=[_ANY] * (n + nz), out_specs=[_ANY] * nz,
        input_output_aliases={n + g: g for g in range(nz)},
        scratch_shapes=[pltpu.SemaphoreType.DMA((n * (NDEV - 1),)), pltpu.SemaphoreType.DMA((n * (NDEV - 1),)),
                        pltpu.SemaphoreType.DMA((n,))],
        compiler_params=pltpu.CompilerParams(has_side_effects=True),
    )(*plan.flat, *zones)


def _exchange_start(plan, name, zones):
    n, nz = len(plan.flat), len(plan.groups)
    hbm = lambda a: pltpu.HBM(a.shape, a.dtype)

    def body(*refs):
        ins, zin = refs[:n], refs[n:n + nz]
        send_sems, recv_sems = refs[n + nz], refs[n + nz + 1]
        token = refs[-1]
        for cp in plan.remote_copies(ins, zin, send_sems, recv_sems):
            cp.start()
        token[...] = jnp.zeros_like(token)

    outs = pl.pallas_call(
        body, name=name,
        out_shape=(pltpu.SemaphoreType.DMA((n * (NDEV - 1),)), pltpu.SemaphoreType.DMA((n * (NDEV - 1),)),
                   *[hbm(a) for a in plan.flat], *[hbm(z) for z in plan.zone_shapes], _sds((8, 128), F32)),
        in_specs=[_HBM] * (n + nz),
        out_specs=(_SEM, _SEM, *[_HBM] * (n + nz), pl.BlockSpec(memory_space=pltpu.VMEM)),
        input_output_aliases={i: 2 + i for i in range(n + nz)},
        compiler_params=pltpu.CompilerParams(has_side_effects=pltpu.SideEffectType.DATAFLOW_SIDE_EFFECTING),
    )(*[pltpu.with_memory_space_constraint(a, pltpu.HBM) for a in plan.flat],
      *[pltpu.with_memory_space_constraint(z, pltpu.HBM) for z in zones])
    return outs[:-1], outs[-1]


def _exchange_wait(plan, name, handle, after):
    n, nz = len(plan.flat), len(plan.groups)
    send_sems, recv_sems = handle[0], handle[1]
    thru = handle[2:]
    hbm = lambda a: pltpu.HBM(a.shape, a.dtype)

    def body(*refs):
        ins, zin = refs[:n], refs[n:n + nz]
        ssem, rsem = refs[n + nz], refs[n + nz + 1]
        for cp in plan.remote_copies(ins, zin, ssem, rsem):
            cp.wait_send()
            cp.wait_recv()

    outs = pl.pallas_call(
        body, name=name, out_shape=tuple(hbm(a) for a in thru),
        in_specs=[_HBM] * (n + nz) + [_SEM, _SEM, _ANY], out_specs=tuple([_HBM] * (n + nz)),
        input_output_aliases={i: i for i in range(n + nz)},
        compiler_params=pltpu.CompilerParams(has_side_effects=pltpu.SideEffectType.DATAFLOW_SIDE_EFFECTING),
    )(*thru, send_sems, recv_sems, after)
    return list(outs[n:])


def _sum_parts(parts, name):
    npart, R, C = parts.shape

    def body(p_ref, o_ref):
        s = p_ref[0]
        for k in range(1, npart):
            s = s + p_ref[k]
        o_ref[...] = s

    return pl.pallas_call(body, name=name, out_shape=_sds((R, C), F32))(parts)


def _rms_fwd(x, g):
    T = x.shape[0]
    tm = min(512, T)

    def body(x_ref, g_ref, o_ref):
        xv = x_ref[...]
        r = lax.rsqrt(jnp.mean(xv * xv, axis=-1, keepdims=True) + EPS)
        o_ref[...] = (xv * r * g_ref[...]).astype(BF16)

    return pl.pallas_call(
        body, name="rms_fwd", grid=(T // tm,), out_shape=_sds((T, D), BF16),
        in_specs=[pl.BlockSpec((tm, D), lambda i: (i, 0)), pl.BlockSpec((1, D), lambda i: (0, 0))],
        out_specs=pl.BlockSpec((tm, D), lambda i: (i, 0)), compiler_params=_cp("parallel"))(x, g)


def _mm_in(h, w):
    T = h.shape[0]
    tm = min(512, T)

    def body(h_ref, w_ref, q_ref, r_ref):
        hv = h_ref[...]
        q_ref[...] = _dot(hv, w_ref[:, :NQKV]).astype(BF16)
        r_ref[...] = _dot(hv, w_ref[:, NQKV:])

    return pl.pallas_call(
        body, name="mm_in", grid=(T // tm,), out_shape=[_sds((T, NQKV), BF16), _sds((T, NREST), F32)],
        in_specs=[pl.BlockSpec((tm, D), lambda i: (i, 0)), pl.BlockSpec((D, NPAD), lambda i: (0, 0))],
        out_specs=[pl.BlockSpec((tm, NQKV), lambda i: (i, 0)), pl.BlockSpec((tm, NREST), lambda i: (i, 0))],
        compiler_params=_cp("parallel"))(h, w)


AUG_Q1, AUG_K1 = 3, 0


def _aug_lane0(h):
    return HD * (1 - h % 2)


def _aug_tables():
    selq = np.zeros((NPAIR, 3 * 128, 128), np.float32)
    selk = np.zeros((NPAIR, 3 * 128, 128), np.float32)
    oneq = np.zeros((NPAIR, 1, 128), np.float32)
    onek = np.zeros((NPAIR, 1, 128), np.float32)
    for h in range(2 * NPAIR):
        p, l0 = h // 2, _aug_lane0(h)
        for piece in range(3):
            selq[p, 128 * piece + h, l0 + piece] = 1.0
            selk[p, 128 * piece + h, l0 + 3 + piece] = -1.0
            oneq[p, 0, l0 + AUG_Q1 + piece] = 1.0
            onek[p, 0, l0 + AUG_K1 + piece] = 1.0
    return [jnp.asarray(a, BF16) for a in (selq, selk, oneq, onek)]


def _gate_fwd(p_rest, bf, S, TB):
    T = p_rest.shape[0]
    B, nb = T // S, S // TB
    selq, selk, oneq, onek = _aug_tables()

    def body(z_ref, b_ref, sq_ref, sk_ref, oq_ref, ok_ref, aq_ref, ak_ref):
        tri = (lax.broadcasted_iota(jnp.int32, (TB, TB), 0) >= lax.broadcasted_iota(jnp.int32, (TB, TB), 1)).astype(F32)
        carry = jnp.zeros((1, 128), F32)
        for j in range(nb):
            rows = slice(j * TB, (j + 1) * TB)
            z = z_ref[rows, :] + b_ref[...]
            lf = jnp.minimum(z, 0.0) - jnp.log(1.0 + jnp.exp(-jnp.abs(z)))
            cb = _dot(tri, lf, HIGHEST) + carry
            carry = cb[TB - 1:TB, :]
            hi = cb.astype(BF16)
            mid = (cb - hi.astype(F32)).astype(BF16)
            lo = (cb - hi.astype(F32) - mid.astype(F32)).astype(BF16)
            pieces = jnp.concatenate([hi, mid, lo], axis=1)
            for p in range(NPAIR):
                aq_ref[p, rows, :] = (_dot(pieces, sq_ref[p]) + oq_ref[p].astype(F32)).astype(BF16)
                ak_ref[p, rows, :] = (_dot(pieces, sk_ref[p]) + ok_ref[p].astype(F32)).astype(BF16)

    full = lambda a: pl.BlockSpec(a.shape, lambda b: (0,) * a.ndim)
    return pl.pallas_call(
        body, name="gate_fwd", grid=(B,), out_shape=[_sds((NPAIR, T, 128), BF16), _sds((NPAIR, T, 128), BF16)],
        in_specs=[pl.BlockSpec((S, 128), lambda b: (b, 4)), pl.BlockSpec((1, 128), lambda b: (0, 0)),
                  full(selq), full(selk), full(oneq), full(onek)],
        out_specs=[pl.BlockSpec((NPAIR, S, 128), lambda b: (0, b, 0)), pl.BlockSpec((NPAIR, S, 128), lambda b: (0, b, 0))],
        compiler_params=_cp("parallel"))(p_rest, bf, selq, selk, oneq, onek)


def _bias_table(S, n_key, n_query, fox):
    unit = min(n_key, n_query)
    d_min = -(n_query // unit - 1)
    d = np.arange(d_min, S // unit)[:, None, None]
    delta = d * unit + np.arange(n_query)[None, None, :] - np.arange(n_key)[None, :, None]
    if fox:
        mult = (delta >= 0).astype(np.float64)
    else:
        mult = np.zeros(delta.shape)
        for w, dil in DILATION_PAIRS:
            mult += (delta >= 0) & (delta % dil == 0) & (delta <= w)
    with np.errstate(divide="ignore"):
        tab = np.where(mult > 0, np.log(np.maximum(mult, 1e-30)), NEG)
    tab = np.concatenate([tab, np.full((1, n_key, n_query), NEG)], axis=0)
    return jnp.asarray(tab, F32)


def _head_masks():
    lane = lax.broadcasted_iota(jnp.int32, (1, 128), 1)
    return [lane < HD, lane >= HD]


def _transpose_bf16(a):
    return a.astype(F32).T.astype(BF16)


def _col_reduce(x, op):
    while x.shape[0] > 8:
        half = x.shape[0] // 2
        x = op(x[:half], x[half:])
    return jnp.max(x, axis=0, keepdims=True) if op is jnp.maximum else jnp.sum(x, axis=0, keepdims=True)


def _attn_fwd(p_qkv, col0, tab, S, aq=None, ak=None, name="attn"):
    T = p_qkv.shape[0]
    n_tab, TB, TQ = tab.shape
    nb, nq, r = S // TB, S // TQ, TQ // TB
    B = T // S
    fox = aq is not None

    def body(*refs):
        if fox:
            q_ref, k_ref, v_ref, tab_ref, aq_ref, ak_ref, o_ref, l_ref, kat_ref, vta_scr, acc_scr, st_scr = refs
        else:
            q_ref, k_ref, v_ref, tab_ref, o_ref, l_ref, kat_ref, vta_scr, acc_scr, st_scr = refs
        i = pl.program_id(2)
        hms = _head_masks()
        top = lax.broadcasted_iota(jnp.int32, (128, 1), 0) < HD
        last_key_block = r * i + r - 1

        @pl.when(i == 0)
        def _():
            for jj in range(nb):
                rows = slice(jj * TB, (jj + 1) * TB)
                vt = _transpose_bf16(v_ref[rows, :])
                for hh in range(2):
                    vta_scr[hh, jj] = jnp.where(top == (hh == 0), vt, jnp.ones_like(vt))
                    spare_k = ak_ref[0, rows, :] if fox else jnp.zeros((TB, 128), BF16)
                    kat_ref[0, 0, hh, jj] = _transpose_bf16(jnp.where(hms[hh], k_ref[rows, :], spare_k))

        q2 = q_ref[...] * 0.125
        spare_q = aq_ref[0] if fox else jnp.zeros_like(q2)
        qa = [jnp.where(hm, q2, spare_q) for hm in hms]
        acc_scr[...] = jnp.zeros_like(acc_scr)

        def scores(j):
            jc = jnp.minimum(j, nb - 1)
            rows = pl.ds(pl.multiple_of(jc * TB, TB), TB)
            k2 = k_ref[rows, :]
            bias = tab_ref[jnp.where(j <= last_key_block, r * i - j + (r - 1), n_tab - 1)]
            return [_dot_nt(jnp.where(hms[hh], k2, ak_ref[0, rows, :]) if fox else k2, qa[hh]) + bias for hh in range(2)]

        def absorb(j, sts, stats):
            jc = jnp.minimum(j, nb - 1)
            alphas, pts = [], []
            for hh in range(2):
                m_old = stats[2 * hh]
                m_new = jnp.maximum(m_old, _col_reduce(sts[hh], jnp.maximum))
                pts.append(jnp.exp(sts[hh] - m_new).astype(BF16))
                alphas.append(jnp.exp(m_old - m_new))
                stats[2 * hh] = m_new
            pvs = [_dot(vta_scr[hh, jc], pts[hh]) for hh in range(2)]
            for hh in range(2):
                half = slice(HD * hh, HD * (hh + 1))
                ones_row = HD * (1 - hh)
                acc_scr[half, :] = alphas[hh] * acc_scr[half, :] + pvs[hh][half]
                stats[2 * hh + 1] = alphas[hh] * stats[2 * hh + 1] + pvs[hh][ones_row:ones_row + 1]

        def kv_pair(jj, carry):
            stats = list(carry)
            j0 = 2 * jj
            st0 = [st_scr[hh] for hh in range(2)]
            st1 = scores(j0 + 1)
            absorb(j0, st0, stats)
            nxt = scores(j0 + 2)
            for hh in range(2):
                st_scr[hh] = nxt[hh]
            absorb(j0 + 1, st1, stats)
            return tuple(stats)

        first = scores(0)
        for hh in range(2):
            st_scr[hh] = first[hh]
        row = lambda v: jnp.full((1, TQ), v, F32)
        m0, l0, m1, l1 = lax.fori_loop(0, (last_key_block + 2) // 2, kv_pair, (row(NEG), row(0.0), row(NEG), row(0.0)))
        o_ref[...] = (acc_scr[...] * jnp.where(top, 1.0 / l0, 1.0 / l1)).T
        for hh, lse in enumerate((m0 + jnp.log(l0), m1 + jnp.log(l1))):
            for t in range(r):
                l_ref[t, hh] = lse[:, t * TB:(t + 1) * TB]

    in_specs = [pl.BlockSpec((TQ, 128), lambda b, p, i: (b * nq + i, col0 + p)),
                pl.BlockSpec((S, 128), lambda b, p, i: (b, col0 + NPAIR + p)),
                pl.BlockSpec((S, 128), lambda b, p, i: (b, col0 + 2 * NPAIR + p)),
                pl.BlockSpec(tab.shape, lambda b, p, i: (0, 0, 0))]
    args = [p_qkv, p_qkv, p_qkv, tab]
    if fox:
        in_specs += [pl.BlockSpec((1, TQ, 128), lambda b, p, i: (p, b * nq + i, 0)),
                     pl.BlockSpec((1, S, 128), lambda b, p, i: (p, b, 0))]
        args += [aq, ak]
    return pl.pallas_call(
        body, name=name, grid=(B, NPAIR, nq),
        out_shape=[_sds((T, WA), F32), _sds((T // TB, 2 * NPAIR, 1, TB), F32), _sds((B, NPAIR, 2, nb, 128, TB), BF16)],
        in_specs=in_specs,
        out_specs=[pl.BlockSpec((TQ, 128), lambda b, p, i: (b * nq + i, p)),
                   pl.BlockSpec((r, 2, 1, TB), lambda b, p, i: (b * nq + i, p, 0, 0)),
                   pl.BlockSpec((1, 1, 2, nb, 128, TB), lambda b, p, i: (b, p, 0, 0, 0, 0))],
        scratch_shapes=[pltpu.VMEM((2, nb, 128, TB), BF16), pltpu.VMEM((128, TQ), F32), pltpu.VMEM((2, TB, TQ), F32)],
        compiler_params=_cp("parallel", "parallel", "arbitrary"))(*args)


def _phase_copies(src, phases, length):
    for r in range(1, 8):
        phases[r, 0:length - 8, :] = src[pl.ds(r, length - 8), :]


def _tap(src, phases, offset, rows):
    r = offset % 8
    return src[pl.ds(offset, rows), :] if r == 0 else phases[r, pl.ds(offset - r, rows), :]


def _conv_taps(src, phases, w_ref, first_row, rows):
    acc = w_ref[0:1, :] * _tap(src, phases, first_row, rows)
    for k in range(1, CONV_K):
        acc = acc + w_ref[k:k + 1, :] * _tap(src, phases, first_row + k, rows)
    return acc


def _mix_fwd(o_a, o_b, p_rest, g_a, g_b, cw, cb, ng, nbias, S):
    T = o_a.shape[0]
    tm = min(256, S)
    nps = S // tm
    HALO = 32

    def body(oa_ref, ob_ref, pt_ref, ph_ref, ga_ref, gb_ref, cw_ref, cb_ref, ng_ref, nb_ref, y_ref, cv_ref, buf, phases):
        i = pl.program_id(0)
        first = (i % nps) == 0
        for o_ref, g_ref, c0 in ((oa_ref, ga_ref, 0), (ob_ref, gb_ref, WA)):
            o = o_ref[...]
            r = lax.rsqrt(jnp.mean(o * o, axis=-1, keepdims=True) + EPS)
            y_ref[:, c0:c0 + WA] = (o * r * g_ref[...]).astype(BF16)
        ph = jnp.where(first, 0.0, ph_ref[...])
        buf[0:HALO, :] = ph[:, :CC] * _sigmoid(ph[:, CC:])
        pt = pt_ref[...]
        buf[HALO:HALO + tm, :] = pt[:, :CC] * _sigmoid(pt[:, CC:])
        _phase_copies(buf, phases, tm + HALO)
        cv = _conv_taps(buf, phases, cw_ref, HALO - CONV_K + 1, tm) + cb_ref[...]
        cv_ref[...] = cv
        xc = cv - jnp.mean(cv, axis=-1, keepdims=True)
        lo = xc * lax.rsqrt(jnp.mean(xc * xc, axis=-1, keepdims=True) + EPS) * ng_ref[...] + nb_ref[...]
        y_ref[:, 2 * WA:] = (lo * _sigmoid(lo)).astype(BF16)

    row = lambda w: pl.BlockSpec((1, w), lambda i: (0, 0))
    return pl.pallas_call(
        body, name="mix_fwd", grid=(T // tm,), out_shape=[_sds((T, D), BF16), _sds((T, CC), F32)],
        in_specs=[pl.BlockSpec((tm, WA), lambda i: (i, 0)), pl.BlockSpec((tm, WA), lambda i: (i, 0)),
                  pl.BlockSpec((tm, 2 * CC), lambda i: (i, 0)),
                  pl.BlockSpec((HALO, 2 * CC), lambda i: (jnp.maximum(i * (tm // HALO) - 1, 0), 0)),
                  row(WA), row(WA), pl.BlockSpec((32, CC), lambda i: (0, 0)), row(CC), row(CC), row(CC)],
        out_specs=[pl.BlockSpec((tm, D), lambda i: (i, 0)), pl.BlockSpec((tm, CC), lambda i: (i, 0))],
        scratch_shapes=[pltpu.VMEM((tm + HALO, CC), F32), pltpu.VMEM((8, tm + HALO, CC), F32)],
        compiler_params=_cp("parallel"))(o_a, o_b, p_rest, p_rest, g_a, g_b, cw, cb, ng, nbias)


def _mm_res(a, w, resid, g, name):
    nk, T, kb = a.shape
    tm = min(512, T)

    def body(a_ref, w_ref, r_ref, g_ref, x_ref, h_ref):
        xv = r_ref[...]
        for k in range(nk):
            xv = xv + _dot(a_ref[k], w_ref[k])
        x_ref[...] = xv
        r = lax.rsqrt(jnp.mean(xv * xv, axis=-1, keepdims=True) + EPS)
        h_ref[...] = (xv * r * g_ref[...]).astype(BF16)

    return pl.pallas_call(
        body, name=name, grid=(T // tm,), out_shape=[_sds((T, D), F32), _sds((T, D), BF16)],
        in_specs=[pl.BlockSpec((nk, tm, kb), lambda i: (0, i, 0)), pl.BlockSpec((nk, kb, D), lambda i: (0, 0, 0)),
                  pl.BlockSpec((tm, D), lambda i: (i, 0)), pl.BlockSpec((1, D), lambda i: (0, 0))],
        out_specs=[pl.BlockSpec((tm, D), lambda i: (i, 0)), pl.BlockSpec((tm, D), lambda i: (i, 0))],
        compiler_params=_cp("parallel"))(a, w, resid, g)


def _up_fwd(h2, w_up, fcw, fcb, S):
    T = h2.shape[0]
    tm = min(512, S)
    nps = S // tm

    def body(h_ref, hh_ref, w_ref, cw_ref, cb_ref, u_ref, c_ref, hid_ref, buf, hbuf):
        i = pl.program_id(1)
        first = (i % nps) == 0
        hbuf[0:tm, :] = h_ref[...]
        hbuf[tm:tm + 8, :] = jnp.where(first, jnp.zeros_like(hh_ref[...]), hh_ref[...])
        c = []
        for half in range(2):
            ua = _dot(hbuf[...], w_ref[half, 0])
            u = ua[0:tm]
            u_ref[half, 0] = u
            buf[half, 0:8, :] = ua[tm:tm + 8]
            buf[half, 8:8 + tm, :] = u
            cw = cw_ref[half, 0]
            c.append(cb_ref[half, 0] + cw[0:1] * buf[half, pl.ds(6, tm), :] + cw[1:2] * buf[half, pl.ds(7, tm), :] + cw[2:3] * u)
        for half in range(2):
            c_ref[half, 0] = c[half]
        hid_ref[0] = (c[0] * _sigmoid(c[0]) * c[1]).astype(BF16)

    return pl.pallas_call(
        body, name="up_fwd", grid=(NJ, T // tm),
        out_shape=[_sds((2, NJ, T, FB), F32), _sds((2, NJ, T, FB), F32), _sds((NJ, T, FB), BF16)],
        in_specs=[pl.BlockSpec((tm, D), lambda j, i: (i, 0)),
                  pl.BlockSpec((8, D), lambda j, i: (jnp.maximum(i * (tm // 8) - 1, 0), 0)),
                  pl.BlockSpec((2, 1, D, FB), lambda j, i: (0, j, 0, 0)),
                  pl.BlockSpec((2, 1, 3, FB), lambda j, i: (0, j, 0, 0)),
                  pl.BlockSpec((2, 1, 1, FB), lambda j, i: (0, j, 0, 0))],
        out_specs=[pl.BlockSpec((2, 1, tm, FB), lambda j, i: (0, j, i, 0)), pl.BlockSpec((2, 1, tm, FB), lambda j, i: (0, j, i, 0)),
                   pl.BlockSpec((1, tm, FB), lambda j, i: (j, i, 0))],
        scratch_shapes=[pltpu.VMEM((2, tm + 8, FB), F32), pltpu.VMEM((tm + 8, D), BF16)],
        compiler_params=_cp("parallel", "parallel"))(h2, h2, w_up, fcw, fcb)


def _final_loss(x, tgt, g):
    T = x.shape[0]
    tm = min(512, T)

    def body(x_ref, t_ref, g_ref, dx_ref, loss_ref, dg_ref):
        @pl.when(pl.program_id(0) == 0)
        def _():
            loss_ref[...] = jnp.zeros_like(loss_ref)
            dg_ref[...] = jnp.zeros_like(dg_ref)

        xv, gv = x_ref[...], g_ref[...]
        r = lax.rsqrt(jnp.mean(xv * xv, axis=-1, keepdims=True) + EPS)
        e = xv * r * gv - t_ref[...]
        loss_ref[...] += 0.5 * jnp.sum(jnp.mean(e * e, axis=-1, keepdims=True), axis=0, keepdims=True)
        dx, dg = _rms_bwd(e * (1.0 / D), xv, gv)
        dx_ref[...] = dx
        dg_ref[...] += dg

    return pl.pallas_call(
        body, name="final_loss", grid=(T // tm,), out_shape=[_sds((T, D), F32), _sds((1, 128), F32), _sds((1, D), F32)],
        in_specs=[pl.BlockSpec((tm, D), lambda i: (i, 0)), pl.BlockSpec((tm, D), lambda i: (i, 0)), pl.BlockSpec((1, D), lambda i: (0, 0))],
        out_specs=[pl.BlockSpec((tm, D), lambda i: (i, 0)), pl.BlockSpec((1, 128), lambda i: (0, 0)), pl.BlockSpec((1, D), lambda i: (0, 0))],
        compiler_params=_cp("arbitrary"))(x, tgt, g)


def _down_bwd(dx2, w_down, u, c, fcw, S):
    T = dx2.shape[0]
    tm = min(512, S)
    nps = S // tm
    nblk8 = T // 8

    def body(dx_ref, dxn_ref, wd_ref, ut_ref, ct_ref, cn_ref, cw_ref, du_ref, dcw_ref, dxb, dcbuf, dsh):
        i = pl.program_id(1)
        last = (i % nps) == nps - 1

        @pl.when(i == 0)
        def _():
            dcw_ref[...] = jnp.zeros_like(dcw_ref)

        dxb[0:tm, :] = dx_ref[...].astype(BF16)
        dxb[tm:tm + 8, :] = jnp.where(last, 0.0, dxn_ref[...]).astype(BF16)
        dh = _dot_nt(dxb[...], wd_ref[0])
        ce = [jnp.concatenate([ct_ref[half, 0], cn_ref[half, 0]], axis=0) for half in range(2)]
        sg = _sigmoid(ce[0])
        dc = [dh * ce[1] * (sg * (1.0 + ce[0] * (1.0 - sg))), dh * (ce[0] * sg)]
        for half in range(2):
            cw = cw_ref[half, 0]
            dcbuf[...] = dc[half]
            for k in range(2):
                dsh[k] = dcbuf[pl.ds(k + 1, tm), :]
            ahead = [dc[half][0:tm], dsh[0], dsh[1]]
            du_ref[half, 0] = (cw[2:3] * ahead[0] + cw[1:2] * ahead[1] + cw[0:1] * ahead[2]).astype(BF16)
            uv = ut_ref[half, 0]
            for k in range(3):
                dcw_ref[half, 0, k:k + 1, :] += jnp.sum(ahead[2 - k] * uv, axis=0, keepdims=True)
            dcw_ref[half, 0, 3:4, :] += jnp.sum(ahead[0], axis=0, keepdims=True)

    ublk = lambda rows, imap: pl.BlockSpec((2, 1, rows, FB), imap)
    return pl.pallas_call(
        body, name="down_bwd", grid=(NJ, T // tm), out_shape=[_sds((2, NJ, T, FB), BF16), _sds((2, NJ, 8, FB), F32)],
        in_specs=[pl.BlockSpec((tm, D), lambda j, i: (i, 0)),
                  pl.BlockSpec((8, D), lambda j, i: (jnp.minimum((i + 1) * (tm // 8), nblk8 - 1), 0)),
                  pl.BlockSpec((1, FB, D), lambda j, i: (j, 0, 0)),
                  ublk(tm, lambda j, i: (0, j, i, 0)),
                  ublk(tm, lambda j, i: (0, j, i, 0)),
                  ublk(8, lambda j, i: (0, j, jnp.minimum((i + 1) * (tm // 8), nblk8 - 1), 0)),
                  pl.BlockSpec((2, 1, 3, FB), lambda j, i: (0, j, 0, 0))],
        out_specs=[ublk(tm, lambda j, i: (0, j, i, 0)), pl.BlockSpec((2, 1, 8, FB), lambda j, i: (0, j, 0, 0))],
        scratch_shapes=[pltpu.VMEM((tm + 8, D), BF16), pltpu.VMEM((tm + 8, FB), F32), pltpu.VMEM((2, tm, FB), F32)],
        compiler_params=_cp("parallel", "arbitrary"))(dx2, dx2, w_down, u, c, c, fcw)


def _wgrad(a, b, tn, name, ga=1, gb=1):
    na, T, ka = a.shape
    nb, _, kb = b.shape
    tk = min(1024, T)
    nt = T // tk

    def body(a_ref, b_ref, o_ref, acc):
        t = pl.program_id(3)

        @pl.when(t == 0)
        def _():
            acc[...] = jnp.zeros_like(acc)

        bs = [b_ref[ib].astype(BF16) for ib in range(gb)]
        for ia in range(ga):
            av = a_ref[ia]
            for ib in range(gb):
                acc[ia, ib] += _dot_tn(av, bs[ib])

        @pl.when(t == nt - 1)
        def _():
            o_ref[...] = acc[...].astype(GRAD_DTYPE)

    return pl.pallas_call(
        body, name=name, grid=(na // ga, nb // gb, kb // tn, nt), out_shape=_sds((na, nb, ka, kb), GRAD_DTYPE),
        in_specs=[pl.BlockSpec((ga, tk, ka), lambda ia, ib, n, t: (ia, t, 0)), pl.BlockSpec((gb, tk, tn), lambda ia, ib, n, t: (ib, t, n))],
        out_specs=pl.BlockSpec((ga, gb, ka, tn), lambda ia, ib, n, t: (ia, ib, 0, n)),
        scratch_shapes=[pltpu.VMEM((ga, gb, ka, tn), F32)],
        compiler_params=_cp("parallel", "parallel", "parallel", "arbitrary"))(a, b)


def _dx_rms(dy, w, x, g, dres, blocked, name):
    T = x.shape[0]
    tm = min(256 if blocked else 512, T)

    def body(dy_ref, w_ref, x_ref, g_ref, r_ref, dx_ref, dg_ref):
        @pl.when(pl.program_id(0) == 0)
        def _():
            dg_ref[...] = jnp.zeros_like(dg_ref)

        if blocked:
            dh = _dot_nt(dy_ref[0, 0], w_ref[0, 0])
            for half, k in [(h, k) for k in range(NJ) for h in range(2)][1:]:
                dh = dh + _dot_nt(dy_ref[half, k], w_ref[half, k])
        else:
            dh = _dot_nt(dy_ref[...], w_ref[...])
        dx, dg = _rms_bwd(dh, x_ref[...], g_ref[...])
        dx_ref[...] = r_ref[...] + dx
        dg_ref[...] += dg

    if blocked:
        dy_spec = pl.BlockSpec((2, NJ, tm, FB), lambda i: (0, 0, i, 0))
        w_spec = pl.BlockSpec((2, NJ, D, FB), lambda i: (0, 0, 0, 0))
    else:
        dy_spec = pl.BlockSpec((tm, dy.shape[1]), lambda i: (i, 0))
        w_spec = pl.BlockSpec(w.shape, lambda i: (0, 0))
    tile = pl.BlockSpec((tm, D), lambda i: (i, 0))
    return pl.pallas_call(
        body, name=name, grid=(T // tm,), out_shape=[_sds((T, D), F32), _sds((1, D), F32)],
        in_specs=[dy_spec, w_spec, tile, pl.BlockSpec((1, D), lambda i: (0, 0)), tile],
        out_specs=[tile, pl.BlockSpec((1, D), lambda i: (0, 0))],
        compiler_params=pltpu.CompilerParams(dimension_semantics=("arbitrary",), vmem_limit_bytes=VMEM_LIMIT_BIG))(dy, w, x, g, dres)


def _mm_nt(a, w, name):
    T = a.shape[0]
    tm = min(512, T)

    def body(a_ref, w_ref, o_ref):
        o_ref[...] = _dot_nt(a_ref[...].astype(BF16), w_ref[...])

    return pl.pallas_call(
        body, name=name, grid=(T // tm,), out_shape=_sds((T, D), F32),
        in_specs=[pl.BlockSpec((tm, D), lambda i: (i, 0)), pl.BlockSpec((D, D), lambda i: (0, 0))],
        out_specs=pl.BlockSpec((tm, D), lambda i: (i, 0)), compiler_params=_cp("parallel"))(a, w)


def _mix_bwd(dy, o_a, o_b, p_rest, cv, g_a, g_b, cw, ng, nbias, S):
    T = dy.shape[0]
    tm = min(256, S)
    nps = S // tm
    HALO = 32
    nblkh = T // HALO
    RE = tm + HALO

    def body(dy_ref, dyn_ref, oa_ref, ob_ref, pt_ref, cvt_ref, cvn_ref, ga_ref, gb_ref, cw_ref, ng_ref, nb_ref,
             doa_ref, dda_ref, dob_ref, ddb_ref, dg_ref, dgn_ref, dcw_ref, dcn_ref, dvbuf, dph):
        i = pl.program_id(0)
        last = (i % nps) == nps - 1

        @pl.when(i == 0)
        def _():
            dgn_ref[...] = jnp.zeros_like(dgn_ref)
            dcw_ref[...] = jnp.zeros_like(dcw_ref)
            dcn_ref[...] = jnp.zeros_like(dcn_ref)

        head_of = (lax.broadcasted_iota(jnp.int32, (8, WA), 1) // HD == lax.broadcasted_iota(jnp.int32, (8, WA), 0)).astype(F32)
        for n, (o_ref, g_ref, do_ref, dd_ref) in enumerate(((oa_ref, ga_ref, doa_ref, dda_ref), (ob_ref, gb_ref, dob_ref, ddb_ref))):
            o = o_ref[...]
            do, dg = _rms_bwd(dy_ref[:, n * WA:(n + 1) * WA], o, g_ref[...])
            dob = do.astype(BF16)
            do_ref[...] = dob
            ddt = _dot_nt(head_of, dob.astype(F32) * o, HIGHEST)
            for hd in range(8):
                dd_ref[0, hd] = ddt[hd:hd + 1, :]
            dgn_ref[n:n + 1, :] += dg

        pt = pt_ref[...]
        gv, sgg = pt[:, :CC], _sigmoid(pt[:, CC:])
        glu = gv * sgg
        cv = jnp.concatenate([cvt_ref[...], cvn_ref[...]], axis=0)
        xc = cv - jnp.mean(cv, axis=-1, keepdims=True)
        r = lax.rsqrt(jnp.mean(xc * xc, axis=-1, keepdims=True) + EPS)
        xh = xc * r
        lo = xh * ng_ref[...] + nb_ref[...]
        sg = _sigmoid(lo)
        dyc = jnp.concatenate([dy_ref[:, 2 * WA:], jnp.where(last, 0.0, dyn_ref[...])], axis=0)
        dlo = dyc * (sg * (1.0 + lo * (1.0 - sg)))
        dcn_ref[0:1, :] += jnp.sum(dlo[0:tm] * xh[0:tm], axis=0, keepdims=True)
        dcn_ref[1:2, :] += jnp.sum(dlo[0:tm], axis=0, keepdims=True)
        dxh = dlo * ng_ref[...]
        dcv = r * (dxh - jnp.mean(dxh, axis=-1, keepdims=True) - xh * jnp.mean(dxh * xh, axis=-1, keepdims=True))
        dvbuf[...] = dcv
        _phase_copies(dvbuf, dph, RE)
        dct = dcv[0:tm]
        dcw_ref[CONV_K:CONV_K + 1, :] += jnp.sum(dct, axis=0, keepdims=True)
        dglu = jnp.zeros((tm, CC), F32)
        for k in range(CONV_K):
            ahead = _tap(dvbuf, dph, CONV_K - 1 - k, tm)
            dcw_ref[k:k + 1, :] += jnp.sum(ahead * glu, axis=0, keepdims=True)
            dglu = dglu + cw_ref[k:k + 1, :] * ahead
        dg_ref[:, :CC] = (dglu * sgg).astype(BF16)
        dg_ref[:, CC:] = (dglu * gv * sgg * (1.0 - sgg)).astype(BF16)

    row = lambda w: pl.BlockSpec((1, w), lambda i: (0, 0))
    tile = lambda w: pl.BlockSpec((tm, w), lambda i: (i, 0))
    nxt = lambda i: jnp.minimum((i + 1) * (tm // HALO), nblkh - 1)
    const = lambda r, w: pl.BlockSpec((r, w), lambda i: (0, 0))
    ddrow = pl.BlockSpec((1, 8, 1, tm), lambda i: (i, 0, 0, 0))
    return pl.pallas_call(
        body, name="mix_bwd", grid=(T // tm,),
        out_shape=[_sds((T, WA), BF16), _sds((T // tm, 8, 1, tm), F32), _sds((T, WA), BF16), _sds((T // tm, 8, 1, tm), F32),
                   _sds((T, 2 * CC), BF16), _sds((8, WA), F32), _sds((32, CC), F32), _sds((8, CC), F32)],
        in_specs=[tile(D), pl.BlockSpec((HALO, CC), lambda i: (nxt(i), 3)), tile(WA), tile(WA), tile(2 * CC),
                  tile(CC), pl.BlockSpec((HALO, CC), lambda i: (nxt(i), 0)),
                  row(WA), row(WA), const(32, CC), row(CC), row(CC)],
        out_specs=[tile(WA), ddrow, tile(WA), ddrow, tile(2 * CC), const(8, WA), const(32, CC), const(8, CC)],
        scratch_shapes=[pltpu.VMEM((RE, CC), F32), pltpu.VMEM((8, RE, CC), F32)],
        compiler_params=_cp("arbitrary"))(dy, dy, o_a, o_b, p_rest, cv, cv, g_a, g_b, cw, ng, nbias)


def _attn_bwd(p_qkv, col0, tab, do, lse, dd, kat, S, aq=None, ak=None, name="attn_bwd"):
    T = p_qkv.shape[0]
    n_tab, TK, TB = tab.shape
    nb, nkb, r = S // TB, S // TK, TK // TB
    B = T // S
    fox = aq is not None

    def body(*refs):
        if fox:
            (q_ref, k_ref, v_ref, tab_ref, do_ref, l_ref, dd_ref, kat_ref, aq_ref, ak_ref,
             dq_ref, dk_ref, dv_ref, dqa_ref, dka_ref, dqt_scr, dk_scr, dv_scr, sd_scr) = refs
        else:
            (q_ref, k_ref, v_ref, tab_ref, do_ref, l_ref, dd_ref, kat_ref,
             dq_ref, dk_ref, dv_ref, dqt_scr, dk_scr, dv_scr, sd_scr) = refs
        j = pl.program_id(2)
        hms = _head_masks()
        first_q_block = r * j

        @pl.when(j == 0)
        def _():
            dqt_scr[...] = jnp.zeros_like(dqt_scr)

        dk_scr[...] = jnp.zeros_like(dk_scr)
        dv_scr[...] = jnp.zeros_like(dv_scr)
        k2, v2 = k_ref[...], v_ref[...]
        ka = [jnp.where(hm, k2, ak_ref[0] if fox else jnp.zeros_like(k2)) for hm in hms]
        kat = [jnp.concatenate([kat_ref[0, 0, hh, t] for t in range(r)], axis=1) if r > 1 else kat_ref[0, 0, hh, 0]
               for hh in range(2)]

        def operands(i):
            ic = jnp.minimum(i, nb - 1)
            rows = pl.ds(pl.multiple_of(ic * TB, TB), TB)
            q2 = q_ref[rows, :] * 0.125
            do2 = do_ref[rows, :]
            qas = [jnp.where(hms[hh], q2, aq_ref[0, rows, :] if fox else jnp.zeros_like(q2)) for hh in range(2)]
            dohs = [jnp.where(hms[hh], do2, jnp.zeros_like(do2)) for hh in range(2)]
            return ic, qas, dohs

        def scores(i):
            ic, qas, dohs = operands(i)
            bias = tab_ref[jnp.where(i < nb, i - first_q_block, n_tab - 1)]
            return [(_dot_nt(ka[hh], qas[hh]) + bias, _dot_nt(v2, dohs[hh])) for hh in range(2)]

        def absorb(i, sd):
            ic, qas, dohs = operands(i)
            for hh in range(2):
                st, dpt = sd[hh]
                pt = jnp.exp(st - l_ref[ic, hh])
                dsb = (pt * (dpt - dd_ref[ic, hh])).astype(BF16)
                dv_scr[...] += _dot(pt.astype(BF16), dohs[hh])
                dk_scr[hh] += _dot(dsb, qas[hh])
                dqt_scr[hh, ic] += _dot(kat[hh], dsb)

        def q_pair(ii, carry):
            i0 = first_q_block + 2 * ii
            sd0 = [(sd_scr[hh, 0], sd_scr[hh, 1]) for hh in range(2)]
            sd1 = scores(i0 + 1)
            absorb(i0, sd0)
            nxt = scores(i0 + 2)
            for hh in range(2):
                sd_scr[hh, 0], sd_scr[hh, 1] = nxt[hh]
            absorb(i0 + 1, sd1)
            return carry

        first = scores(first_q_block)
        for hh in range(2):
            sd_scr[hh, 0], sd_scr[hh, 1] = first[hh]
        lax.fori_loop(0, (nb - first_q_block + 1) // 2, q_pair, 0)
        dk_ref[...] = jnp.where(hms[0], dk_scr[0], dk_scr[1]).astype(BF16)
        dv_ref[...] = dv_scr[...].astype(BF16)
        if fox:
            dka_ref[...] = jnp.where(hms[0], dk_scr[1], dk_scr[0])

        @pl.when(j == nkb - 1)
        def _():
            for ii in range(nb):
                t0, t1 = dqt_scr[0, ii].T, dqt_scr[1, ii].T
                dq_ref[ii * TB:(ii + 1) * TB, :] = jnp.where(hms[0], t0, t1) * 0.125
                if fox:
                    dqa_ref[ii * TB:(ii + 1) * TB, :] = jnp.where(hms[0], t1, t0)

    seq = lambda c: pl.BlockSpec((S, 128), lambda b, p, j: (b, c + p))
    kvb = lambda c: pl.BlockSpec((TK, 128), lambda b, p, j: (b * nkb + j, c + p))
    stat = pl.BlockSpec((nb, 2, 1, TB), lambda b, p, j: (b, p, 0, 0))
    in_specs = [seq(col0), kvb(col0 + NPAIR), kvb(col0 + 2 * NPAIR), pl.BlockSpec(tab.shape, lambda b, p, j: (0, 0, 0)),
                seq(0), stat, stat, pl.BlockSpec((1, 1, 2, r, 128, TB), lambda b, p, j: (b, p, 0, j, 0, 0))]
    args = [p_qkv, p_qkv, p_qkv, tab, do, lse, dd, kat]
    out_shape = [_sds((T, WA), F32), _sds((T, WA), BF16), _sds((T, WA), BF16)]
    out_specs = [seq(0), kvb(0), kvb(0)]
    if fox:
        in_specs += [pl.BlockSpec((1, S, 128), lambda b, p, j: (p, b, 0)), pl.BlockSpec((1, TK, 128), lambda b, p, j: (p, b * nkb + j, 0))]
        args += [aq, ak]
        out_shape += [_sds((T, WA), F32), _sds((T, WA), F32)]
        out_specs += [seq(0), kvb(0)]
    return pl.pallas_call(
        body, name=name, grid=(B, NPAIR, nkb), out_shape=out_shape, in_specs=in_specs, out_specs=out_specs,
        scratch_shapes=[pltpu.VMEM((2, nb, 128, TB), F32), pltpu.VMEM((2, TK, 128), F32), pltpu.VMEM((TK, 128), F32),
                        pltpu.VMEM((2, 2, TK, TB), F32)],
        compiler_params=_cp("parallel", "parallel", "arbitrary"))(*args)


def _gate_bwd(dqa, dka, p_rest, bf, S, TB):
    T = p_rest.shape[0]
    B, nb = T // S, S // TB

    def body(dqa_ref, dka_ref, z_ref, b_ref, dz_ref, db_ref):
        @pl.when(pl.program_id(0) == 0)
        def _():
            db_ref[...] = jnp.zeros_like(db_ref)

        later = (lax.broadcasted_iota(jnp.int32, (TB, TB), 1) >= lax.broadcasted_iota(jnp.int32, (TB, TB), 0)).astype(F32)
        lane = lax.broadcasted_iota(jnp.int32, (1, 128), 1)
        src, head = lax.broadcasted_iota(jnp.int32, (WA, 128), 0), lax.broadcasted_iota(jnp.int32, (WA, 128), 1)
        spare0 = 128 * (head // 2) + HD * (1 - head % 2)
        pick_q = (src == spare0 + AUG_K1).astype(F32)
        pick_k = (src == spare0 + AUG_Q1).astype(F32)
        carry = jnp.zeros((1, 128), F32)
        dbs = jnp.zeros((1, 128), F32)
        for j in reversed(range(nb)):
            rows = slice(j * TB, (j + 1) * TB)
            dc = _dot(dqa_ref[rows, :], pick_q, HIGHEST) - _dot(dka_ref[rows, :], pick_k, HIGHEST)
            part = _dot(later, dc, HIGHEST)
            tot = part + carry
            carry = carry + part[0:1, :]
            z = z_ref[j * TB:(j + 1) * TB, :] + b_ref[...]
            dz = jnp.where(lane < 6, tot * _sigmoid(-z), 0.0)
            dz_ref[j * TB:(j + 1) * TB, :] = dz.astype(BF16)
            dbs = dbs + jnp.sum(dz, axis=0, keepdims=True)
        db_ref[...] += dbs

    return pl.pallas_call(
        body, name="gate_bwd", grid=(B,), out_shape=[_sds((T, 128), BF16), _sds((1, 128), F32)],
        in_specs=[pl.BlockSpec((S, WA), lambda b: (b, 0)), pl.BlockSpec((S, WA), lambda b: (b, 0)),
                  pl.BlockSpec((S, 128), lambda b: (b, 4)), pl.BlockSpec((1, 128), lambda b: (0, 0))],
        out_specs=[pl.BlockSpec((S, 128), lambda b: (b, 0)), pl.BlockSpec((1, 128), lambda b: (0, 0))],
        compiler_params=_cp("arbitrary"))(dqa, dka, p_rest, bf)


def _adamw(parts, w, m, v, name, row0=0, prev=None):
    npart, Rp, C = parts.shape
    R = w.shape[0]
    tr = Rp
    for cand in (256, 128, 64, 32, 16, 8):
        if Rp % cand == 0 and row0 % cand == 0 and cand * C * 4 <= (1 << 20):
            tr = cand
            break
    if Rp == R and R * C * 4 <= (1 << 20):
        tr = R
    assert Rp % tr == 0 and row0 % tr == 0, (name, Rp, row0, tr)
    b0 = row0 // tr

    def body(p_ref, w_ref, m_ref, v_ref, *rest):
        g_ref, d_ref, mo_ref, vo_ref = rest[-4:]
        g = p_ref[0].astype(F32)
        for k in range(1, npart):
            g = g + p_ref[k].astype(F32)
        mn = ADAM_B1 * m_ref[...] + (1.0 - ADAM_B1) * g
        vn = ADAM_B2 * v_ref[...] + (1.0 - ADAM_B2) * (g * g)
        m_hat = mn / (1.0 - ADAM_B1 ** ADAM_STEP)
        v_hat = vn / (1.0 - ADAM_B2 ** ADAM_STEP)
        g_ref[...] = g
        d_ref[...] = -ADAM_LR * (m_hat / (jnp.sqrt(v_hat) + ADAM_EPS) + ADAM_WD * w_ref[...])
        mo_ref[...] = mn
        vo_ref[...] = vn

    blk = pl.BlockSpec((tr, C), lambda i: (b0 + i, 0))
    prev = list(prev) if prev is not None else []
    return pl.pallas_call(
        body, name=name, grid=(Rp // tr,), out_shape=[_sds((R, C), F32)] * 4,
        in_specs=[pl.BlockSpec((npart, tr, C), lambda i: (0, i, 0)), blk, blk, blk] + [_ANY] * len(prev), out_specs=[blk] * 4,
        input_output_aliases={4 + i: i for i in range(len(prev))},
        compiler_params=_cp("parallel"))(parts, w, m, v, *prev)


def _pad_w_in(w):
    z = jnp.zeros(w.shape[:-1] + (NPAD - N_IN,), w.dtype)
    return jnp.concatenate([w[..., :O_FA], w[..., O_QB:], w[..., O_FA:O_QB], z], axis=-1)


def _unpad_w_in(w):
    n_mid = N_IN - O_QB
    return jnp.concatenate([w[..., :O_FA], w[..., O_FA + n_mid:O_FA + n_mid + 6], w[..., O_FA:O_FA + n_mid]], axis=-1)


def _local_step(x, tgt, W, S, late_weights=None, on_layer_grads=None, on_ffn_grads=None):
    T = x.shape[0]
    TB = min(256, S)
    TW = min(2 * TB, S)
    tab_fox, tab_dil = _bias_table(S, TW, TB, True), _bias_table(S, TW, TB, False)
    tabq_fox, tabq_dil = _bias_table(S, TB, TW, True), _bias_table(S, TB, TW, False)
    saved = []
    h = _rms_fwd(x, W["ln1_g"][0])
    for l in range(DEPTH):
        p_qkv, p_rest = _mm_in(h, W["w_in"][l])
        aq, ak = _gate_fwd(p_rest, W["b_forget"][l], S, TB)
        o_a, lse_a, kat_a = _attn_fwd(p_qkv, 0, tabq_fox, S, aq, ak, name="fox_fwd")
        o_b, lse_b, kat_b = _attn_fwd(p_qkv, 3 * NPAIR, tabq_dil, S, name="dil_fwd")
        y, cv = _mix_fwd(o_a, o_b, p_rest, W["g_out_fox"][l], W["g_out_dil"][l], W["conv_w"][l], W["conv_b"][l],
                     W["cnorm_g"][l], W["cnorm_b"][l], S)
        if l == 0 and late_weights is not None:
            W = late_weights(W, y)
        x1, h2 = _mm_res(y[None], W["w_o"][l][None], x, W["ln2_g"][l], name="mm_o")
        u, c, hid = _up_fwd(h2, W["w_up"][l], W["ffn_conv_w"][l], W["ffn_conv_b"][l], S)
        g_next = W["ln1_g"][l + 1] if l + 1 < DEPTH else W["g_final"]
        x2, h_next = _mm_res(hid, W["w_down"][l], x1, g_next, name="mm_down")
        saved.append(dict(x=x, h=h, p_qkv=p_qkv, p_rest=p_rest, aq=aq, ak=ak, o_a=o_a, lse_a=lse_a, o_b=o_b,
                          lse_b=lse_b, kat_a=kat_a, kat_b=kat_b, y=y, cv=cv, x1=x1, h2=h2, u=u, c=c, hid=hid))
        x, h = x2, h_next
    dx, loss, dg_final = _final_loss(x, tgt, W["g_final"])
    G = {k: [None] * DEPTH for k in ("ln1_g", "w_in", "b_forget", "g_out_fox", "g_out_dil", "conv_w", "conv_b", "cnorm_g",
                                     "cnorm_b", "w_o", "ln2_g", "w_up", "ffn_conv_w", "ffn_conv_b", "w_down")}
    G["g_final"] = dg_final
    for l in reversed(range(DEPTH)):
        sv = saved[l]
        du, dcw = _down_bwd(dx, W["w_down"][l], sv["u"], sv["c"], W["ffn_conv_w"][l], S)
        G["w_down"][l] = _wgrad(sv["hid"], dx[None], D, "wg_down", ga=NJ)[:, 0]
        G["ffn_conv_w"][l] = dcw[:, :, 0:3, :]
        G["ffn_conv_b"][l] = dcw[:, :, 3, :]
        G["w_up"][l] = _wgrad(du.reshape(2 * NJ, T, FB), sv["h2"][None], D, "wg_up", ga=NJ)[:, 0]
        token = on_ffn_grads(l, G) if on_ffn_grads is not None else None
        ln2 = W["ln2_g"][l] if token is None else W["ln2_g"][l] + token[0, 0]
        dx1, G["ln2_g"][l] = _dx_rms(du, W["w_up"][l], sv["x1"], ln2, dx, True, "dx_up")
        G["w_o"][l] = _wgrad(sv["y"][None], dx1[None], D, "wg_o")[0, 0]
        dy = _mm_nt(dx1, W["w_o"][l], "mm_dy")
        do_a, dd_a, do_b, dd_b, dgvgg, dgn, dcvw, dcn = _mix_bwd(
            dy, sv["o_a"], sv["o_b"], sv["p_rest"], sv["cv"], W["g_out_fox"][l], W["g_out_dil"][l], W["conv_w"][l],
            W["cnorm_g"][l], W["cnorm_b"][l], S)
        G["g_out_fox"][l], G["g_out_dil"][l] = dgn[0], dgn[1]
        G["conv_w"][l], G["conv_b"][l] = dcvw[:CONV_K], dcvw[CONV_K]
        G["cnorm_g"][l], G["cnorm_b"][l] = dcn[0], dcn[1]
        dq_a, dk_a, dv_a, dqa, dka = _attn_bwd(sv["p_qkv"], 0, tab_fox, do_a, sv["lse_a"], dd_a, sv["kat_a"], S, sv["aq"], sv["ak"],
                                               name="fox_bwd")
        dq_b, dk_b, dv_b = _attn_bwd(sv["p_qkv"], 3 * NPAIR, tab_dil, do_b, sv["lse_b"], dd_b, sv["kat_b"], S, name="dil_bwd")
        dfa, db = _gate_bwd(dqa, dka, sv["p_rest"], W["b_forget"][l], S, TB)
        G["b_forget"][l] = db[0, :6]
        dp = jnp.concatenate([dq_a.astype(BF16), dk_a, dv_a, dq_b.astype(BF16), dk_b, dv_b, dgvgg, dfa,
                              jnp.zeros((T, 128), BF16)], axis=1)
        G["w_in"][l] = _wgrad(sv["h"][None], dp[None], NPAD, "wg_in")[0, 0]
        token = on_layer_grads(l, G) if on_layer_grads is not None else None
        ln1 = W["ln1_g"][l] if token is None else W["ln1_g"][l] + token[0, 0]
        dx, G["ln1_g"][l] = _dx_rms(dp, W["w_in"][l], sv["x"], ln1, dx1, False, "dx_in")
    return loss, dx, G


_MATMUL_W = ("w_in", "w_o", "w_up", "w_down")
_SHARDED = _MATMUL_W + ("conv_w", "ffn_conv_w")
_SMALL = ("ln1_g", "b_forget", "g_out_fox", "g_out_dil", "conv_b", "cnorm_g", "cnorm_b", "ln2_g", "ffn_conv_b", "g_final")
_ORDER = ("ln1_g", "w_in", "b_forget", "g_out_fox", "g_out_dil", "conv_w", "conv_b", "cnorm_g", "cnorm_b", "w_o", "ln2_g",
          "w_up", "ffn_conv_w", "ffn_conv_b", "w_down", "g_final")


def _kernel_ready(k, zone):
    if k == "w_in":
        return zone.reshape(D, NPAD)
    if k == "w_o":
        return zone.reshape(D, D)
    if k == "w_up":
        return zone.reshape(2, NJ, D, FB)
    if k == "w_down":
        return zone.reshape(NJ, FB, D)
    if k == "conv_w":
        cw = jnp.transpose(zone.reshape(NDEV, CONV_K, CC // NDEV), (1, 0, 2)).reshape(CONV_K, CC)
        return jnp.concatenate([cw, jnp.zeros((1, CC), F32)], axis=0)
    return zone.reshape(2, NJ, 3, FB)


def _full_weights(P, gathered):
    W = {}
    row = lambda a: [a[l][None, :] for l in range(DEPTH)]
    for k in _SHARDED:
        W[k] = [None if gathered[k][l] is None else _kernel_ready(k, gathered[k][l]) for l in range(DEPTH)]
    W["ffn_conv_b"] = [P["ffn_conv_b"][l].reshape(2, NJ, 1, FB) for l in range(DEPTH)]
    W["b_forget"] = [jnp.concatenate([P["b_forget"][l], jnp.zeros((128 - 6,), F32)])[None, :] for l in range(DEPTH)]
    for k in ("ln1_g", "ln2_g", "g_out_fox", "g_out_dil", "conv_b", "cnorm_g", "cnorm_b"):
        W[k] = row(P[k])
    W["g_final"] = P["g_final"][None, :]
    return W


def kernel(x, ln1_g, w_in, b_forget, g_out_fox, g_out_dil, conv_w, conv_b, cnorm_g, cnorm_b, w_o, ln2_g, w_up, ffn_conv_w, ffn_conv_b, w_down, g_final, loss_target, m_ln1_g, m_w_in, m_b_forget, m_g_out_fox, m_g_out_dil, m_conv_w, m_conv_b, m_cnorm_g, m_cnorm_b, m_w_o, m_ln2_g, m_w_up, m_ffn_conv_w, m_ffn_conv_b, m_w_down, m_g_final, v_ln1_g, v_w_in, v_b_forget, v_g_out_fox, v_g_out_dil, v_conv_w, v_conv_b, v_cnorm_g, v_cnorm_b, v_w_o, v_ln2_g, v_w_up, v_ffn_conv_w, v_ffn_conv_b, v_w_down, v_g_final):
    P = dict(ln1_g=ln1_g, w_in=w_in, b_forget=b_forget, g_out_fox=g_out_fox, g_out_dil=g_out_dil, conv_w=conv_w, conv_b=conv_b,
             cnorm_g=cnorm_g, cnorm_b=cnorm_b, w_o=w_o, ln2_g=ln2_g, w_up=w_up, ffn_conv_w=ffn_conv_w, ffn_conv_b=ffn_conv_b,
             w_down=w_down, g_final=g_final)
    M = dict(ln1_g=m_ln1_g, w_in=m_w_in, b_forget=m_b_forget, g_out_fox=m_g_out_fox, g_out_dil=m_g_out_dil, conv_w=m_conv_w,
             conv_b=m_conv_b, cnorm_g=m_cnorm_g, cnorm_b=m_cnorm_b, w_o=m_w_o, ln2_g=m_ln2_g, w_up=m_w_up, ffn_conv_w=m_ffn_conv_w,
             ffn_conv_b=m_ffn_conv_b, w_down=m_w_down, g_final=m_g_final)
    V = dict(ln1_g=v_ln1_g, w_in=v_w_in, b_forget=v_b_forget, g_out_fox=v_g_out_fox, g_out_dil=v_g_out_dil, conv_w=v_conv_w,
             conv_b=v_conv_b, cnorm_g=v_cnorm_g, cnorm_b=v_cnorm_b, w_o=v_w_o, ln2_g=v_ln2_g, w_up=v_w_up, ffn_conv_w=v_ffn_conv_w,
             ffn_conv_b=v_ffn_conv_b, w_down=v_w_down, g_final=v_g_final)
    Bl, S, _ = x.shape
    T = Bl * S

    def shard(k, l):
        if k == "w_in":
            return _pad_w_in(P[k][l].astype(BF16))
        return P[k][l].astype(BF16) if k in _MATMUL_W else P[k][l]

    early = [("w_in", 0)] + [(k, l) for k in ("conv_w", "ffn_conv_w") for l in range(DEPTH)]
    late = [(k, 0) for k in ("w_o", "w_up", "w_down")] + [(k, 1) for k in _MATMUL_W]
    gathered = {k: [None] * DEPTH for k in _SHARDED}
    plan_early = _Plan([[shard(k, l)] for k, l in early], True)
    early_zones = _gather_two_level(plan_early, "gather_early")
    for (k, l), zone in zip(early, early_zones):
        gathered[k][l] = zone
    late_shards, _ = lax.optimization_barrier(([shard(k, l) for k, l in late], early_zones[0]))
    plan_late = _Plan([[a] for a in late_shards], True)
    late_handle, late_token = _exchange_start(plan_late, "gather_late_start", plan_late.zones_with_own())
    W = _full_weights(P, gathered)
    W["ln1_g"][0] = W["ln1_g"][0] + late_token[0, 0]

    def late_weights(W, after):
        zones = _exchange_wait(plan_late, "gather_late_wait", late_handle, after)
        W = dict(W)
        for (k, l), zone in zip(late, zones):
            W[k] = list(W[k])
            W[k][l] = _kernel_ready(k, zone)
        return W

    def owner_block(G, l, k):
        if k == "w_in":
            blk = _unpad_w_in(G[k][l]).reshape(NDEV, D // NDEV, N_IN)
        elif k == "w_o":
            blk = G[k][l].reshape(NDEV, D // NDEV, D)
        elif k == "w_up":
            blk = G[k][l]
        elif k == "w_down":
            blk = G[k][l].reshape(NDEV, DFF // NDEV, D)
        elif k == "conv_w":
            blk = jnp.transpose(G[k][l].reshape(CONV_K, NDEV, CC // NDEV), (1, 0, 2))
        else:
            blk = G[k][l].reshape(NDEV, 3, FB)
        return blk.astype(GRAD_DTYPE)

    flying = []

    def start_scatter(tag, G, l, keys):
        plan = _Plan([[owner_block(G, l, k)] for k in keys], False)
        handle, token = _exchange_start(plan, "scatter_%s_start" % tag, plan.zones_with_own())
        flying.append((tag, l, keys, plan, handle))
        return token

    def on_layer_grads(l, G):
        return start_scatter("l1", G, l, _MATMUL_W) if l == 1 else start_scatter("l0_rest", G, l, ("w_in", "w_o"))

    def on_ffn_grads(l, G):
        return start_scatter("l0_ffn", G, l, ("w_up", "w_down")) if l == 0 else None

    loss, dx, G = _local_step(x.reshape(T, D), loss_target.reshape(T, D), W, S, late_weights, on_layer_grads, on_ffn_grads)

    parts = {}
    for tag, l, keys, plan, handle in flying:
        for k, zone in zip(keys, _exchange_wait(plan, "scatter_%s_wait" % tag, handle, dx)):
            parts[(k, l)] = zone
    plan_last = _Plan([[owner_block(G, l, k) for l in range(DEPTH)] for k in ("conv_w", "ffn_conv_w")], False)
    last = _exchange(plan_last, "scatter_last")
    small_parts_sharded = {"conv_w": last[0], "ffn_conv_w": last[1]}

    small = [jnp.stack(G[k]).reshape(-1) for k in _SMALL[:-1]] + [G["g_final"].reshape(-1), loss[0, :1]]
    sizes = [int(s.shape[0]) for s in small]
    flat = jnp.concatenate(small)
    n_small = -(-flat.shape[0] // 1024) * 1024
    flat = jnp.concatenate([flat, jnp.zeros((n_small - flat.shape[0],), F32)]).reshape(n_small // 128, 128)
    small_parts = _exchange(_Plan([[flat]], True), "gather_small")[0]

    out_g, out_d, out_m, out_v = {}, {}, {}, {}
    sg = _sum_parts(small_parts.reshape(NDEV, n_small // 128, 128), "sum_small").reshape(-1)
    off = 0
    summed = {}
    for k, n in zip(_SMALL + ("loss",), sizes):
        summed[k] = sg[off:off + n]
        off += n
    for k in _ORDER:
        shp = P[k].shape
        C = shp[-1]
        if k in _MATMUL_W:
            swap = k == "w_up"
            R0, Cw = (shp[2], shp[1]) if swap else (shp[1], shp[2])
            flat2 = lambda t: (jnp.swapaxes(t, 1, 2) if swap else t).reshape(DEPTH * R0, Cw)
            res = None
            for l in reversed(range(DEPTH)):
                res = _adamw(parts[(k, l)].reshape(NDEV, R0, Cw), flat2(P[k]), flat2(M[k]), flat2(V[k]),
                             "adamw_%s_l%d" % (k, l), row0=l * R0, prev=res)
            if swap:
                res = [jnp.swapaxes(t.reshape(DEPTH, R0, Cw), 1, 2) for t in res]
        else:
            pr = small_parts_sharded[k].reshape((NDEV, -1, C)) if k in _SHARDED else summed[k].reshape((1, -1, C))
            R = pr.shape[1]
            res = _adamw(pr, P[k].reshape(R, -1), M[k].reshape(R, -1), V[k].reshape(R, -1), "adamw_" + k)
        out_g[k], out_d[k], out_m[k], out_v[k] = (t.reshape(shp) for t in res)

    loss_out = summed["loss"].reshape(())
    grad_x = dx.reshape(Bl, S, D)
    return (loss_out, grad_x, *[out_g[k] for k in _ORDER], *[out_d[k] for k in _ORDER], *[out_m[k] for k in _ORDER],
            *[out_v[k] for k in _ORDER])
```

```python
import functools

import numpy as np
import jax
import jax.numpy as jnp
from jax import lax
from jax.experimental import pallas as pl
from jax.experimental.pallas import tpu as pltpu

F32 = jnp.float32
BF16 = jnp.bfloat16
GRAD_DTYPE = BF16
HIGHEST = lax.Precision.HIGHEST

D = 1024
DEPTH = 2
HD = 64
WA = 384
NPAIR = 3
CC = 256
CONV_K = 31
DFF = 2816
FB = 704
NJ = 4
NDEV = 8
EPS = 1e-6
NQKV = 2304
NREST = 768
NPAD = NQKV + NREST
O_FA, O_QB, N_IN = 1152, 1158, 2822
DILATION_PAIRS = ((128, 1), (512, 4), (2048, 16))
NEG = -1e30

ADAM_LR, ADAM_B1, ADAM_B2, ADAM_EPS, ADAM_WD, ADAM_STEP = 0.001, 0.9, 0.999, 1e-08, 0.01, 10

VMEM_LIMIT = 48 * 1024 * 1024
VMEM_LIMIT_BIG = 56 * 1024 * 1024


def _cp(*sem):
    return pltpu.CompilerParams(dimension_semantics=sem, vmem_limit_bytes=VMEM_LIMIT)


def _sds(shape, dtype):
    return jax.ShapeDtypeStruct(shape, dtype)


def _dot(a, b, prec=None):
    return jnp.dot(a, b, preferred_element_type=F32, precision=prec)


def _dot_nt(a, b, prec=None):
    return lax.dot_general(a, b, (((1,), (1,)), ((), ())), preferred_element_type=F32, precision=prec)


def _dot_tn(a, b):
    return lax.dot_general(a, b, (((0,), (0,)), ((), ())), preferred_element_type=F32)


def _sigmoid(z):
    return 0.5 * jnp.tanh(0.5 * z) + 0.5


def _rms_bwd(dh, x, g):
    r = lax.rsqrt(jnp.mean(x * x, axis=-1, keepdims=True) + EPS)
    xh = x * r
    dg = jnp.sum(dh * xh, axis=0, keepdims=True)
    dxh = dh * g
    dx = r * (dxh - xh * jnp.mean(dxh * xh, axis=-1, keepdims=True))
    return dx, dg


class _Plan:
    def __init__(self, groups, gather, slots=None, n_slots=None):
        self.groups, self.gather = groups, gather
        self.slots = slots or [list(range(len(g))) for g in groups]
        self.n_slots = n_slots or [len(g) for g in groups]
        self.flat = [a for g in groups for a in g]
        self.where = [(gi, s) for gi, g in enumerate(groups) for s in self.slots[gi]]
        self.zone_shapes = [_sds((NDEV, ns) + (g[0].shape if gather else g[0].shape[1:]), g[0].dtype)
                            for g, ns in zip(groups, self.n_slots)]

    def new_zones(self):
        return [lax.empty(z.shape, z.dtype) for z in self.zone_shapes]

    def me(self):
        x, y, c = lax.axis_index("x"), lax.axis_index("y"), lax.axis_index("c")
        return 4 * x + 2 * y + c

    def zones_with_own(self):
        me = self.me()
        zones = self.new_zones()
        for a, (gi, s) in enumerate(self.where):
            src = self.flat[a] if self.gather else lax.dynamic_index_in_dim(self.flat[a], me, 0, keepdims=False)
            zones[gi] = lax.dynamic_update_slice(zones[gi], src[None, None], (me, s) + (0,) * src.ndim)
        return zones

    def own_copies(self, ins, zones, sems):
        me = self.me()
        return [pltpu.make_async_copy(ins[a] if self.gather else ins[a].at[me], zones[gi].at[me, s], sems.at[a])
                for a, (gi, s) in enumerate(self.where)]

    def remote_copies(self, ins, zones, send_sems, recv_sems):
        x, y, c = lax.axis_index("x"), lax.axis_index("y"), lax.axis_index("c")
        me = 4 * x + 2 * y + c
        out = []
        for a, (gi, s) in enumerate(self.where):
            for k in range(1, NDEV):
                px, py, pc = x ^ ((k >> 2) & 1), y ^ ((k >> 1) & 1), c ^ (k & 1)
                out.append(pltpu.make_async_remote_copy(
                    src_ref=ins[a] if self.gather else ins[a].at[4 * px + 2 * py + pc], dst_ref=zones[gi].at[me, s],
                    send_sem=send_sems.at[a * (NDEV - 1) + k - 1], recv_sem=recv_sems.at[a * (NDEV - 1) + k - 1],
                    device_id=(px, py, pc), device_id_type=pl.DeviceIdType.MESH))
        return out


_ANY = pl.BlockSpec(memory_space=pl.ANY)
_HBM = pl.BlockSpec(memory_space=pltpu.HBM)
_SEM = pl.BlockSpec(memory_space=pltpu.SEMAPHORE)


def _exchange(plan, name, zones=None):
    n, nz = len(plan.flat), len(plan.groups)
    zones = zones or plan.new_zones()

    def body(*refs):
        ins, zin = refs[:n], refs[n:n + nz]
        send_sems, recv_sems, local_sems = refs[n + 2 * nz:]
        copies = plan.own_copies(ins, zin, local_sems) + plan.remote_copies(ins, zin, send_sems, recv_sems)
        for cp in copies:
            cp.start()
        for cp in copies:
            cp.wait()

    return pl.pallas_call(
        body, name=name, out_shape=plan.zone_shapes, in_specs=[_ANY] * (n + nz), out_specs=[_ANY] * nz,
        input_output_aliases={n + g: g for g in range(nz)},
        scratch_shapes=[pltpu.SemaphoreType.DMA((n * (NDEV - 1),)), pltpu.SemaphoreType.DMA((n * (NDEV - 1),)),
                        pltpu.SemaphoreType.DMA((n,))],
        compiler_params=pltpu.CompilerParams(has_side_effects=True),
    )(*plan.flat, *zones)


def _gather_two_level(plan, name):
    n, nz = len(plan.flat), len(plan.groups)
    assert plan.gather
    zones = plan.new_zones()

    def body(*refs):
        ins, zin = refs[:n], refs[n:n + nz]
        send_sems, recv_sems, local_sems = refs[n + 2 * nz:]
        x, y, c = lax.axis_index("x"), lax.axis_index("y"), lax.axis_index("c")
        me, sibling = (x, y, c), (x, y, 1 - c)
        chips = [(1 - x, y), (x, 1 - y), (1 - x, 1 - y)]

        def copy(a, k, block, to, src=None):
            gi, s = plan.where[a]
            dst = zin[gi].at[4 * block[0] + 2 * block[1] + block[2], s]
            return pltpu.make_async_remote_copy(
                src_ref=dst if src is None else src, dst_ref=dst,
                send_sem=send_sems.at[a * (NDEV - 1) + k], recv_sem=recv_sems.at[a * (NDEV - 1) + k],
                device_id=to, device_id_type=pl.DeviceIdType.MESH)

        owns = plan.own_copies(ins, zin, local_sems)
        first = []
        for a in range(n):
            first.append(copy(a, 0, me, sibling, src=ins[a]))
            first += [copy(a, 1 + j, me, (*chip, c), src=ins[a]) for j, chip in enumerate(chips)]
        for cp in owns + first:
            cp.start()
        passed = []
        for a in range(n):
            for j, chip in enumerate(chips):
                copy(a, 1 + j, (*chip, c), me).wait_recv()
                passed.append(copy(a, 4 + j, (*chip, c), sibling))
                passed[-1].start()
        for a in range(n):
            copy(a, 0, sibling, me).wait_recv()
            for j, chip in enumerate(chips):
                copy(a, 4 + j, (*chip, 1 - c), me).wait_recv()
        for cp in first + passed:
            cp.wait_send()
        for cp in owns:
            cp.wait()

    return pl.pallas_call(
        body, name=name, out_shape=plan.zone_shapes, in_specs=[_ANY] * (n + nz), out_specs=[_ANY] * nz,
        input_output_aliases={n + g: g for g in range(nz)},
        scratch_shapes=[pltpu.SemaphoreType.DMA((n * (NDEV - 1),)), pltpu.SemaphoreType.DMA((n * (NDEV - 1),)),
                        pltpu.SemaphoreType.DMA((n,))],
        compiler_params=pltpu.CompilerParams(has_side_effects=True),
    )(*plan.flat, *zones)


def _exchange_start(plan, name, zones):
    n, nz = len(plan.flat), len(plan.groups)
    hbm = lambda a: pltpu.HBM(a.shape, a.dtype)

    def body(*refs):
        ins, zin = refs[:n], refs[n:n + nz]
        send_sems, recv_sems = refs[n + nz], refs[n + nz + 1]
        token = refs[-1]
        for cp in plan.remote_copies(ins, zin, send_sems, recv_sems):
            cp.start()
        token[...] = jnp.zeros_like(token)

    outs = pl.pallas_call(
        body, name=name,
        out_shape=(pltpu.SemaphoreType.DMA((n * (NDEV - 1),)), pltpu.SemaphoreType.DMA((n * (NDEV - 1),)),
                   *[hbm(a) for a in plan.flat], *[hbm(z) for z in plan.zone_shapes], _sds((8, 128), F32)),
        in_specs=[_HBM] * (n + nz),
        out_specs=(_SEM, _SEM, *[_HBM] * (n + nz), pl.BlockSpec(memory_space=pltpu.VMEM)),
        input_output_aliases={i: 2 + i for i in range(n + nz)},
        compiler_params=pltpu.CompilerParams(has_side_effects=pltpu.SideEffectType.DATAFLOW_SIDE_EFFECTING),
    )(*[pltpu.with_memory_space_constraint(a, pltpu.HBM) for a in plan.flat],
      *[pltpu.with_memory_space_constraint(z, pltpu.HBM) for z in zones])
    return outs[:-1], outs[-1]


def _exchange_wait(plan, name, handle, after):
    n, nz = len(plan.flat), len(plan.groups)
    send_sems, recv_sems = handle[0], handle[1]
    thru = handle[2:]
    hbm = lambda a: pltpu.HBM(a.shape, a.dtype)

    def body(*refs):
        ins, zin = refs[:n], refs[n:n + nz]
        ssem, rsem = refs[n + nz], refs[n + nz + 1]
        for cp in plan.remote_copies(ins, zin, ssem, rsem):
            cp.wait_send()
            cp.wait_recv()

    outs = pl.pallas_call(
        body, name=name, out_shape=tuple(hbm(a) for a in thru),
        in_specs=[_HBM] * (n + nz) + [_SEM, _SEM, _ANY], out_specs=tuple([_HBM] * (n + nz)),
        input_output_aliases={i: i for i in range(n + nz)},
        compiler_params=pltpu.CompilerParams(has_side_effects=pltpu.SideEffectType.DATAFLOW_SIDE_EFFECTING),
    )(*thru, send_sems, recv_sems, after)
    return list(outs[n:])


def _sum_parts(parts, name):
    npart, R, C = parts.shape

    def body(p_ref, o_ref):
        s = p_ref[0]
        for k in range(1, npart):
            s = s + p_ref[k]
        o_ref[...] = s

    return pl.pallas_call(body, name=name, out_shape=_sds((R, C), F32))(parts)


def _rms_fwd(x, g):
    T = x.shape[0]
    tm = min(512, T)

    def body(x_ref, g_ref, o_ref):
        xv = x_ref[...]
        r = lax.rsqrt(jnp.mean(xv * xv, axis=-1, keepdims=True) + EPS)
        o_ref[...] = (xv * r * g_ref[...]).astype(BF16)

    return pl.pallas_call(
        body, name="rms_fwd", grid=(T // tm,), out_shape=_sds((T, D), BF16),
        in_specs=[pl.BlockSpec((tm, D), lambda i: (i, 0)), pl.BlockSpec((1, D), lambda i: (0, 0))],
        out_specs=pl.BlockSpec((tm, D), lambda i: (i, 0)), compiler_params=_cp("parallel"))(x, g)


def _mm_in(h, w):
    T = h.shape[0]
    tm = min(512, T)

    def body(h_ref, w_ref, q_ref, r_ref):
        hv = h_ref[...]
        q_ref[...] = _dot(hv, w_ref[:, :NQKV]).astype(BF16)
        r_ref[...] = _dot(hv, w_ref[:, NQKV:])

    return pl.pallas_call(
        body, name="mm_in", grid=(T // tm,), out_shape=[_sds((T, NQKV), BF16), _sds((T, NREST), F32)],
        in_specs=[pl.BlockSpec((tm, D), lambda i: (i, 0)), pl.BlockSpec((D, NPAD), lambda i: (0, 0))],
        out_specs=[pl.BlockSpec((tm, NQKV), lambda i: (i, 0)), pl.BlockSpec((tm, NREST), lambda i: (i, 0))],
        compiler_params=_cp("parallel"))(h, w)


AUG_Q1, AUG_K1 = 3, 0


def _aug_lane0(h):
    return HD * (1 - h % 2)


def _aug_tables():
    selq = np.zeros((NPAIR, 3 * 128, 128), np.float32)
    selk = np.zeros((NPAIR, 3 * 128, 128), np.float32)
    oneq = np.zeros((NPAIR, 1, 128), np.float32)
    onek = np.zeros((NPAIR, 1, 128), np.float32)
    for h in range(2 * NPAIR):
        p, l0 = h // 2, _aug_lane0(h)
        for piece in range(3):
            selq[p, 128 * piece + h, l0 + piece] = 1.0
            selk[p, 128 * piece + h, l0 + 3 + piece] = -1.0
            oneq[p, 0, l0 + AUG_Q1 + piece] = 1.0
            onek[p, 0, l0 + AUG_K1 + piece] = 1.0
    return [jnp.asarray(a, BF16) for a in (selq, selk, oneq, onek)]


def _gate_fwd(p_rest, bf, S, TB):
    T = p_rest.shape[0]
    B, nb = T // S, S // TB
    selq, selk, oneq, onek = _aug_tables()

    def body(z_ref, b_ref, sq_ref, sk_ref, oq_ref, ok_ref, aq_ref, ak_ref):
        tri = (lax.broadcasted_iota(jnp.int32, (TB, TB), 0) >= lax.broadcasted_iota(jnp.int32, (TB, TB), 1)).astype(F32)
        carry = jnp.zeros((1, 128), F32)
        for j in range(nb):
            rows = slice(j * TB, (j + 1) * TB)
            z = z_ref[rows, :] + b_ref[...]
            lf = jnp.minimum(z, 0.0) - jnp.log(1.0 + jnp.exp(-jnp.abs(z)))
            cb = _dot(tri, lf, HIGHEST) + carry
            carry = cb[TB - 1:TB, :]
            hi = cb.astype(BF16)
            mid = (cb - hi.astype(F32)).astype(BF16)
            lo = (cb - hi.astype(F32) - mid.astype(F32)).astype(BF16)
            pieces = jnp.concatenate([hi, mid, lo], axis=1)
            for p in range(NPAIR):
                aq_ref[p, rows, :] = (_dot(pieces, sq_ref[p]) + oq_ref[p].astype(F32)).astype(BF16)
                ak_ref[p, rows, :] = (_dot(pieces, sk_ref[p]) + ok_ref[p].astype(F32)).astype(BF16)

    full = lambda a: pl.BlockSpec(a.shape, lambda b: (0,) * a.ndim)
    return pl.pallas_call(
        body, name="gate_fwd", grid=(B,), out_shape=[_sds((NPAIR, T, 128), BF16), _sds((NPAIR, T, 128), BF16)],
        in_specs=[pl.BlockSpec((S, 128), lambda b: (b, 4)), pl.BlockSpec((1, 128), lambda b: (0, 0)),
                  full(selq), full(selk), full(oneq), full(onek)],
        out_specs=[pl.BlockSpec((NPAIR, S, 128), lambda b: (0, b, 0)), pl.BlockSpec((NPAIR, S, 128), lambda b: (0, b, 0))],
        compiler_params=_cp("parallel"))(p_rest, bf, selq, selk, oneq, onek)


def _bias_table(S, n_key, n_query, fox):
    unit = min(n_key, n_query)
    d_min = -(n_query // unit - 1)
    d = np.arange(d_min, S // unit)[:, None, None]
    delta = d * unit + np.arange(n_query)[None, None, :] - np.arange(n_key)[None, :, None]
    if fox:
        mult = (delta >= 0).astype(np.float64)
    else:
        mult = np.zeros(delta.shape)
        for w, dil in DILATION_PAIRS:
            mult += (delta >= 0) & (delta % dil == 0) & (delta <= w)
    with np.errstate(divide="ignore"):
        tab = np.where(mult > 0, np.log(np.maximum(mult, 1e-30)), NEG)
    tab = np.concatenate([tab, np.full((1, n_key, n_query), NEG)], axis=0)
    return jnp.asarray(tab, F32)


def _head_masks():
    lane = lax.broadcasted_iota(jnp.int32, (1, 128), 1)
    return [lane < HD, lane >= HD]


def _transpose_bf16(a):
    return a.astype(F32).T.astype(BF16)


def _col_reduce(x, op):
    while x.shape[0] > 8:
        half = x.shape[0] // 2
        x = op(x[:half], x[half:])
    return jnp.max(x, axis=0, keepdims=True) if op is jnp.maximum else jnp.sum(x, axis=0, keepdims=True)


def _attn_fwd(p_qkv, col0, tab, S, aq=None, ak=None, name="attn"):
    T = p_qkv.shape[0]
    n_tab, TB, TQ = tab.shape
    nb, nq, r = S // TB, S // TQ, TQ // TB
    B = T // S
    fox = aq is not None

    def body(*refs):
        if fox:
            q_ref, k_ref, v_ref, tab_ref, aq_ref, ak_ref, o_ref, l_ref, kat_ref, vta_scr, acc_scr, st_scr = refs
        else:
            q_ref, k_ref, v_ref, tab_ref, o_ref, l_ref, kat_ref, vta_scr, acc_scr, st_scr = refs
        i = pl.program_id(2)
        hms = _head_masks()
        top = lax.broadcasted_iota(jnp.int32, (128, 1), 0) < HD
        last_key_block = r * i + r - 1

        @pl.when(i == 0)
        def _():
            for jj in range(nb):
                rows = slice(jj * TB, (jj + 1) * TB)
                vt = _transpose_bf16(v_ref[rows, :])
                for hh in range(2):
                    vta_scr[hh, jj] = jnp.where(top == (hh == 0), vt, jnp.ones_like(vt))
                    spare_k = ak_ref[0, rows, :] if fox else jnp.zeros((TB, 128), BF16)
                    kat_ref[0, 0, hh, jj] = _transpose_bf16(jnp.where(hms[hh], k_ref[rows, :], spare_k))

        q2 = q_ref[...] * 0.125
        spare_q = aq_ref[0] if fox else jnp.zeros_like(q2)
        qa = [jnp.where(hm, q2, spare_q) for hm in hms]
        acc_scr[...] = jnp.zeros_like(acc_scr)

        def scores(j):
            jc = jnp.minimum(j, nb - 1)
            rows = pl.ds(pl.multiple_of(jc * TB, TB), TB)
            k2 = k_ref[rows, :]
            bias = tab_ref[jnp.where(j <= last_key_block, r * i - j + (r - 1), n_tab - 1)]
            return [_dot_nt(jnp.where(hms[hh], k2, ak_ref[0, rows, :]) if fox else k2, qa[hh]) + bias for hh in range(2)]

        def absorb(j, sts, stats):
            jc = jnp.minimum(j, nb - 1)
            alphas, pts = [], []
            for hh in range(2):
                m_old = stats[2 * hh]
                m_new = jnp.maximum(m_old, _col_reduce(sts[hh], jnp.maximum))
                pts.append(jnp.exp(sts[hh] - m_new).astype(BF16))
                alphas.append(jnp.exp(m_old - m_new))
                stats[2 * hh] = m_new
            pvs = [_dot(vta_scr[hh, jc], pts[hh]) for hh in range(2)]
            for hh in range(2):
                half = slice(HD * hh, HD * (hh + 1))
                ones_row = HD * (1 - hh)
                acc_scr[half, :] = alphas[hh] * acc_scr[half, :] + pvs[hh][half]
                stats[2 * hh + 1] = alphas[hh] * stats[2 * hh + 1] + pvs[hh][ones_row:ones_row + 1]

        def kv_pair(jj, carry):
            stats = list(carry)
            j0 = 2 * jj
            st0 = [st_scr[hh] for hh in range(2)]
            st1 = scores(j0 + 1)
            absorb(j0, st0, stats)
            nxt = scores(j0 + 2)
            for hh in range(2):
                st_scr[hh] = nxt[hh]
            absorb(j0 + 1, st1, stats)
            return tuple(stats)

        first = scores(0)
        for hh in range(2):
            st_scr[hh] = first[hh]
        row = lambda v: jnp.full((1, TQ), v, F32)
        m0, l0, m1, l1 = lax.fori_loop(0, (last_key_block + 2) // 2, kv_pair, (row(NEG), row(0.0), row(NEG), row(0.0)))
        o_ref[...] = (acc_scr[...] * jnp.where(top, 1.0 / l0, 1.0 / l1)).T
        for hh, lse in enumerate((m0 + jnp.log(l0), m1 + jnp.log(l1))):
            for t in range(r):
                l_ref[t, hh] = lse[:, t * TB:(t + 1) * TB]

    in_specs = [pl.BlockSpec((TQ, 128), lambda b, p, i: (b * nq + i, col0 + p)),
                pl.BlockSpec((S, 128), lambda b, p, i: (b, col0 + NPAIR + p)),
                pl.BlockSpec((S, 128), lambda b, p, i: (b, col0 + 2 * NPAIR + p)),
                pl.BlockSpec(tab.shape, lambda b, p, i: (0, 0, 0))]
    args = [p_qkv, p_qkv, p_qkv, tab]
    if fox:
        in_specs += [pl.BlockSpec((1, TQ, 128), lambda b, p, i: (p, b * nq + i, 0)),
                     pl.BlockSpec((1, S, 128), lambda b, p, i: (p, b, 0))]
        args += [aq, ak]
    return pl.pallas_call(
        body, name=name, grid=(B, NPAIR, nq),
        out_shape=[_sds((T, WA), F32), _sds((T // TB, 2 * NPAIR, 1, TB), F32), _sds((B, NPAIR, 2, nb, 128, TB), BF16)],
        in_specs=in_specs,
        out_specs=[pl.BlockSpec((TQ, 128), lambda b, p, i: (b * nq + i, p)),
                   pl.BlockSpec((r, 2, 1, TB), lambda b, p, i: (b * nq + i, p, 0, 0)),
                   pl.BlockSpec((1, 1, 2, nb, 128, TB), lambda b, p, i: (b, p, 0, 0, 0, 0))],
        scratch_shapes=[pltpu.VMEM((2, nb, 128, TB), BF16), pltpu.VMEM((128, TQ), F32), pltpu.VMEM((2, TB, TQ), F32)],
        compiler_params=_cp("parallel", "parallel", "arbitrary"))(*args)


def _phase_copies(src, phases, length):
    for r in range(1, 8):
        phases[r, 0:length - 8, :] = src[pl.ds(r, length - 8), :]


def _tap(src, phases, offset, rows):
    r = offset % 8
    return src[pl.ds(offset, rows), :] if r == 0 else phases[r, pl.ds(offset - r, rows), :]


def _conv_taps(src, phases, w_ref, first_row, rows):
    acc = w_ref[0:1, :] * _tap(src, phases, first_row, rows)
    for k in range(1, CONV_K):
        acc = acc + w_ref[k:k + 1, :] * _tap(src, phases, first_row + k, rows)
    return acc


def _mix_fwd(o_a, o_b, p_rest, g_a, g_b, cw, cb, ng, nbias, S):
    T = o_a.shape[0]
    tm = min(256, S)
    nps = S // tm
    HALO = 32

    def body(oa_ref, ob_ref, pt_ref, ph_ref, ga_ref, gb_ref, cw_ref, cb_ref, ng_ref, nb_ref, y_ref, cv_ref, buf, phases):
        i = pl.program_id(0)
        first = (i % nps) == 0
        for o_ref, g_ref, c0 in ((oa_ref, ga_ref, 0), (ob_ref, gb_ref, WA)):
            o = o_ref[...]
            r = lax.rsqrt(jnp.mean(o * o, axis=-1, keepdims=True) + EPS)
            y_ref[:, c0:c0 + WA] = (o * r * g_ref[...]).astype(BF16)
        ph = jnp.where(first, 0.0, ph_ref[...])
        buf[0:HALO, :] = ph[:, :CC] * _sigmoid(ph[:, CC:])
        pt = pt_ref[...]
        buf[HALO:HALO + tm, :] = pt[:, :CC] * _sigmoid(pt[:, CC:])
        _phase_copies(buf, phases, tm + HALO)
        cv = _conv_taps(buf, phases, cw_ref, HALO - CONV_K + 1, tm) + cb_ref[...]
        cv_ref[...] = cv
        xc = cv - jnp.mean(cv, axis=-1, keepdims=True)
        lo = xc * lax.rsqrt(jnp.mean(xc * xc, axis=-1, keepdims=True) + EPS) * ng_ref[...] + nb_ref[...]
        y_ref[:, 2 * WA:] = (lo * _sigmoid(lo)).astype(BF16)

    row = lambda w: pl.BlockSpec((1, w), lambda i: (0, 0))
    return pl.pallas_call(
        body, name="mix_fwd", grid=(T // tm,), out_shape=[_sds((T, D), BF16), _sds((T, CC), F32)],
        in_specs=[pl.BlockSpec((tm, WA), lambda i: (i, 0)), pl.BlockSpec((tm, WA), lambda i: (i, 0)),
                  pl.BlockSpec((tm, 2 * CC), lambda i: (i, 0)),
                  pl.BlockSpec((HALO, 2 * CC), lambda i: (jnp.maximum(i * (tm // HALO) - 1, 0), 0)),
                  row(WA), row(WA), pl.BlockSpec((32, CC), lambda i: (0, 0)), row(CC), row(CC), row(CC)],
        out_specs=[pl.BlockSpec((tm, D), lambda i: (i, 0)), pl.BlockSpec((tm, CC), lambda i: (i, 0))],
        scratch_shapes=[pltpu.VMEM((tm + HALO, CC), F32), pltpu.VMEM((8, tm + HALO, CC), F32)],
        compiler_params=_cp("parallel"))(o_a, o_b, p_rest, p_rest, g_a, g_b, cw, cb, ng, nbias)


def _mm_res(a, w, resid, g, name):
    nk, T, kb = a.shape
    tm = min(512, T)

    def body(a_ref, w_ref, r_ref, g_ref, x_ref, h_ref):
        xv = r_ref[...]
        for k in range(nk):
            xv = xv + _dot(a_ref[k], w_ref[k])
        x_ref[...] = xv
        r = lax.rsqrt(jnp.mean(xv * xv, axis=-1, keepdims=True) + EPS)
        h_ref[...] = (xv * r * g_ref[...]).astype(BF16)

    return pl.pallas_call(
        body, name=name, grid=(T // tm,), out_shape=[_sds((T, D), F32), _sds((T, D), BF16)],
        in_specs=[pl.BlockSpec((nk, tm, kb), lambda i: (0, i, 0)), pl.BlockSpec((nk, kb, D), lambda i: (0, 0, 0)),
                  pl.BlockSpec((tm, D), lambda i: (i, 0)), pl.BlockSpec((1, D), lambda i: (0, 0))],
        out_specs=[pl.BlockSpec((tm, D), lambda i: (i, 0)), pl.BlockSpec((tm, D), lambda i: (i, 0))],
        compiler_params=_cp("parallel"))(a, w, resid, g)


def _up_fwd(h2, w_up, fcw, fcb, S):
    T = h2.shape[0]
    tm = min(512, S)
    nps = S // tm

    def body(h_ref, hh_ref, w_ref, cw_ref, cb_ref, u_ref, c_ref, hid_ref, buf, hbuf):
        i = pl.program_id(1)
        first = (i % nps) == 0
        hbuf[0:tm, :] = h_ref[...]
        hbuf[tm:tm + 8, :] = jnp.where(first, jnp.zeros_like(hh_ref[...]), hh_ref[...])
        c = []
        for half in range(2):
            ua = _dot(hbuf[...], w_ref[half, 0])
            u = ua[0:tm]
            u_ref[half, 0] = u
            buf[half, 0:8, :] = ua[tm:tm + 8]
            buf[half, 8:8 + tm, :] = u
            cw = cw_ref[half, 0]
            c.append(cb_ref[half, 0] + cw[0:1] * buf[half, pl.ds(6, tm), :] + cw[1:2] * buf[half, pl.ds(7, tm), :] + cw[2:3] * u)
        for half in range(2):
            c_ref[half, 0] = c[half]
        hid_ref[0] = (c[0] * _sigmoid(c[0]) * c[1]).astype(BF16)

    return pl.pallas_call(
        body, name="up_fwd", grid=(NJ, T // tm),
        out_shape=[_sds((2, NJ, T, FB), F32), _sds((2, NJ, T, FB), F32), _sds((NJ, T, FB), BF16)],
        in_specs=[pl.BlockSpec((tm, D), lambda j, i: (i, 0)),
                  pl.BlockSpec((8, D), lambda j, i: (jnp.maximum(i * (tm // 8) - 1, 0), 0)),
                  pl.BlockSpec((2, 1, D, FB), lambda j, i: (0, j, 0, 0)),
                  pl.BlockSpec((2, 1, 3, FB), lambda j, i: (0, j, 0, 0)),
                  pl.BlockSpec((2, 1, 1, FB), lambda j, i: (0, j, 0, 0))],
        out_specs=[pl.BlockSpec((2, 1, tm, FB), lambda j, i: (0, j, i, 0)), pl.BlockSpec((2, 1, tm, FB), lambda j, i: (0, j, i, 0)),
                   pl.BlockSpec((1, tm, FB), lambda j, i: (j, i, 0))],
        scratch_shapes=[pltpu.VMEM((2, tm + 8, FB), F32), pltpu.VMEM((tm + 8, D), BF16)],
        compiler_params=_cp("parallel", "parallel"))(h2, h2, w_up, fcw, fcb)


def _final_loss(x, tgt, g):
    T = x.shape[0]
    tm = min(512, T)

    def body(x_ref, t_ref, g_ref, dx_ref, loss_ref, dg_ref):
        @pl.when(pl.program_id(0) == 0)
        def _():
            loss_ref[...] = jnp.zeros_like(loss_ref)
            dg_ref[...] = jnp.zeros_like(dg_ref)

        xv, gv = x_ref[...], g_ref[...]
        r = lax.rsqrt(jnp.mean(xv * xv, axis=-1, keepdims=True) + EPS)
        e = xv * r * gv - t_ref[...]
        loss_ref[...] += 0.5 * jnp.sum(jnp.mean(e * e, axis=-1, keepdims=True), axis=0, keepdims=True)
        dx, dg = _rms_bwd(e * (1.0 / D), xv, gv)
        dx_ref[...] = dx
        dg_ref[...] += dg

    return pl.pallas_call(
        body, name="final_loss", grid=(T // tm,), out_shape=[_sds((T, D), F32), _sds((1, 128), F32), _sds((1, D), F32)],
        in_specs=[pl.BlockSpec((tm, D), lambda i: (i, 0)), pl.BlockSpec((tm, D), lambda i: (i, 0)), pl.BlockSpec((1, D), lambda i: (0, 0))],
        out_specs=[pl.BlockSpec((tm, D), lambda i: (i, 0)), pl.BlockSpec((1, 128), lambda i: (0, 0)), pl.BlockSpec((1, D), lambda i: (0, 0))],
        compiler_params=_cp("arbitrary"))(x, tgt, g)


def _down_bwd(dx2, w_down, u, c, fcw, S):
    T = dx2.shape[0]
    tm = min(512, S)
    nps = S // tm
    nblk8 = T // 8

    def body(dx_ref, dxn_ref, wd_ref, ut_ref, ct_ref, cn_ref, cw_ref, du_ref, dcw_ref, dxb, dcbuf, dsh):
        i = pl.program_id(1)
        last = (i % nps) == nps - 1

        @pl.when(i == 0)
        def _():
            dcw_ref[...] = jnp.zeros_like(dcw_ref)

        dxb[0:tm, :] = dx_ref[...].astype(BF16)
        dxb[tm:tm + 8, :] = jnp.where(last, 0.0, dxn_ref[...]).astype(BF16)
        dh = _dot_nt(dxb[...], wd_ref[0])
        ce = [jnp.concatenate([ct_ref[half, 0], cn_ref[half, 0]], axis=0) for half in range(2)]
        sg = _sigmoid(ce[0])
        dc = [dh * ce[1] * (sg * (1.0 + ce[0] * (1.0 - sg))), dh * (ce[0] * sg)]
        for half in range(2):
            cw = cw_ref[half, 0]
            dcbuf[...] = dc[half]
            for k in range(2):
                dsh[k] = dcbuf[pl.ds(k + 1, tm), :]
            ahead = [dc[half][0:tm], dsh[0], dsh[1]]
            du_ref[half, 0] = (cw[2:3] * ahead[0] + cw[1:2] * ahead[1] + cw[0:1] * ahead[2]).astype(BF16)
            uv = ut_ref[half, 0]
            for k in range(3):
                dcw_ref[half, 0, k:k + 1, :] += jnp.sum(ahead[2 - k] * uv, axis=0, keepdims=True)
            dcw_ref[half, 0, 3:4, :] += jnp.sum(ahead[0], axis=0, keepdims=True)

    ublk = lambda rows, imap: pl.BlockSpec((2, 1, rows, FB), imap)
    return pl.pallas_call(
        body, name="down_bwd", grid=(NJ, T // tm), out_shape=[_sds((2, NJ, T, FB), BF16), _sds((2, NJ, 8, FB), F32)],
        in_specs=[pl.BlockSpec((tm, D), lambda j, i: (i, 0)),
                  pl.BlockSpec((8, D), lambda j, i: (jnp.minimum((i + 1) * (tm // 8), nblk8 - 1), 0)),
                  pl.BlockSpec((1, FB, D), lambda j, i: (j, 0, 0)),
                  ublk(tm, lambda j, i: (0, j, i, 0)),
                  ublk(tm, lambda j, i: (0, j, i, 0)),
                  ublk(8, lambda j, i: (0, j, jnp.minimum((i + 1) * (tm // 8), nblk8 - 1), 0)),
                  pl.BlockSpec((2, 1, 3, FB), lambda j, i: (0, j, 0, 0))],
        out_specs=[ublk(tm, lambda j, i: (0, j, i, 0)), pl.BlockSpec((2, 1, 8, FB), lambda j, i: (0, j, 0, 0))],
        scratch_shapes=[pltpu.VMEM((tm + 8, D), BF16), pltpu.VMEM((tm + 8, FB), F32), pltpu.VMEM((2, tm, FB), F32)],
        compiler_params=_cp("parallel", "arbitrary"))(dx2, dx2, w_down, u, c, c, fcw)


def _wgrad(a, b, tn, name, ga=1, gb=1):
    na, T, ka = a.shape
    nb, _, kb = b.shape
    tk = min(1024, T)
    nt = T // tk

    def body(a_ref, b_ref, o_ref, acc):
        t = pl.program_id(3)

        @pl.when(t == 0)
        def _():
            acc[...] = jnp.zeros_like(acc)

        bs = [b_ref[ib].astype(BF16) for ib in range(gb)]
        for ia in range(ga):
            av = a_ref[ia]
            for ib in range(gb):
                acc[ia, ib] += _dot_tn(av, bs[ib])

        @pl.when(t == nt - 1)
        def _():
            o_ref[...] = acc[...].astype(GRAD_DTYPE)

    return pl.pallas_call(
        body, name=name, grid=(na // ga, nb // gb, kb // tn, nt), out_shape=_sds((na, nb, ka, kb), GRAD_DTYPE),
        in_specs=[pl.BlockSpec((ga, tk, ka), lambda ia, ib, n, t: (ia, t, 0)), pl.BlockSpec((gb, tk, tn), lambda ia, ib, n, t: (ib, t, n))],
        out_specs=pl.BlockSpec((ga, gb, ka, tn), lambda ia, ib, n, t: (ia, ib, 0, n)),
        scratch_shapes=[pltpu.VMEM((ga, gb, ka, tn), F32)],
        compiler_params=_cp("parallel", "parallel", "parallel", "arbitrary"))(a, b)


def _dx_rms(dy, w, x, g, dres, blocked, name):
    T = x.shape[0]
    tm = min(256 if blocked else 512, T)

    def body(dy_ref, w_ref, x_ref, g_ref, r_ref, dx_ref, dg_ref):
        @pl.when(pl.program_id(0) == 0)
        def _():
            dg_ref[...] = jnp.zeros_like(dg_ref)

        if blocked:
            dh = _dot_nt(dy_ref[0, 0], w_ref[0, 0])
            for half, k in [(h, k) for k in range(NJ) for h in range(2)][1:]:
                dh = dh + _dot_nt(dy_ref[half, k], w_ref[half, k])
        else:
            dh = _dot_nt(dy_ref[...], w_ref[...])
        dx, dg = _rms_bwd(dh, x_ref[...], g_ref[...])
        dx_ref[...] = r_ref[...] + dx
        dg_ref[...] += dg

    if blocked:
        dy_spec = pl.BlockSpec((2, NJ, tm, FB), lambda i: (0, 0, i, 0))
        w_spec = pl.BlockSpec((2, NJ, D, FB), lambda i: (0, 0, 0, 0))
    else:
        dy_spec = pl.BlockSpec((tm, dy.shape[1]), lambda i: (i, 0))
        w_spec = pl.BlockSpec(w.shape, lambda i: (0, 0))
    tile = pl.BlockSpec((tm, D), lambda i: (i, 0))
    return pl.pallas_call(
        body, name=name, grid=(T // tm,), out_shape=[_sds((T, D), F32), _sds((1, D), F32)],
        in_specs=[dy_spec, w_spec, tile, pl.BlockSpec((1, D), lambda i: (0, 0)), tile],
        out_specs=[tile, pl.BlockSpec((1, D), lambda i: (0, 0))],
        compiler_params=pltpu.CompilerParams(dimension_semantics=("arbitrary",), vmem_limit_bytes=VMEM_LIMIT_BIG))(dy, w, x, g, dres)


def _mm_nt(a, w, name):
    T = a.shape[0]
    tm = min(512, T)

    def body(a_ref, w_ref, o_ref):
        o_ref[...] = _dot_nt(a_ref[...].astype(BF16), w_ref[...])

    return pl.pallas_call(
        body, name=name, grid=(T // tm,), out_shape=_sds((T, D), F32),
        in_specs=[pl.BlockSpec((tm, D), lambda i: (i, 0)), pl.BlockSpec((D, D), lambda i: (0, 0))],
        out_specs=pl.BlockSpec((tm, D), lambda i: (i, 0)), compiler_params=_cp("parallel"))(a, w)


def _mix_bwd(dy, o_a, o_b, p_rest, cv, g_a, g_b, cw, ng, nbias, S):
    T = dy.shape[0]
    tm = min(256, S)
    nps = S // tm
    HALO = 32
    nblkh = T // HALO
    RE = tm + HALO

    def body(dy_ref, dyn_ref, oa_ref, ob_ref, pt_ref, cvt_ref, cvn_ref, ga_ref, gb_ref, cw_ref, ng_ref, nb_ref,
             doa_ref, dda_ref, dob_ref, ddb_ref, dg_ref, dgn_ref, dcw_ref, dcn_ref, dvbuf, dph):
        i = pl.program_id(0)
        last = (i % nps) == nps - 1

        @pl.when(i == 0)
        def _():
            dgn_ref[...] = jnp.zeros_like(dgn_ref)
            dcw_ref[...] = jnp.zeros_like(dcw_ref)
            dcn_ref[...] = jnp.zeros_like(dcn_ref)

        head_of = (lax.broadcasted_iota(jnp.int32, (8, WA), 1) // HD == lax.broadcasted_iota(jnp.int32, (8, WA), 0)).astype(F32)
        for n, (o_ref, g_ref, do_ref, dd_ref) in enumerate(((oa_ref, ga_ref, doa_ref, dda_ref), (ob_ref, gb_ref, dob_ref, ddb_ref))):
            o = o_ref[...]
            do, dg = _rms_bwd(dy_ref[:, n * WA:(n + 1) * WA], o, g_ref[...])
            dob = do.astype(BF16)
            do_ref[...] = dob
            ddt = _dot_nt(head_of, dob.astype(F32) * o, HIGHEST)
            for hd in range(8):
                dd_ref[0, hd] = ddt[hd:hd + 1, :]
            dgn_ref[n:n + 1, :] += dg

        pt = pt_ref[...]
        gv, sgg = pt[:, :CC], _sigmoid(pt[:, CC:])
        glu = gv * sgg
        cv = jnp.concatenate([cvt_ref[...], cvn_ref[...]], axis=0)
        xc = cv - jnp.mean(cv, axis=-1, keepdims=True)
        r = lax.rsqrt(jnp.mean(xc * xc, axis=-1, keepdims=True) + EPS)
        xh = xc * r
        lo = xh * ng_ref[...] + nb_ref[...]
        sg = _sigmoid(lo)
        dyc = jnp.concatenate([dy_ref[:, 2 * WA:], jnp.where(last, 0.0, dyn_ref[...])], axis=0)
        dlo = dyc * (sg * (1.0 + lo * (1.0 - sg)))
        dcn_ref[0:1, :] += jnp.sum(dlo[0:tm] * xh[0:tm], axis=0, keepdims=True)
        dcn_ref[1:2, :] += jnp.sum(dlo[0:tm], axis=0, keepdims=True)
        dxh = dlo * ng_ref[...]
        dcv = r * (dxh - jnp.mean(dxh, axis=-1, keepdims=True) - xh * jnp.mean(dxh * xh, axis=-1, keepdims=True))
        dvbuf[...] = dcv
        _phase_copies(dvbuf, dph, RE)
        dct = dcv[0:tm]
        dcw_ref[CONV_K:CONV_K + 1, :] += jnp.sum(dct, axis=0, keepdims=True)
        dglu = jnp.zeros((tm, CC), F32)
        for k in range(CONV_K):
            ahead = _tap(dvbuf, dph, CONV_K - 1 - k, tm)
            dcw_ref[k:k + 1, :] += jnp.sum(ahead * glu, axis=0, keepdims=True)
            dglu = dglu + cw_ref[k:k + 1, :] * ahead
        dg_ref[:, :CC] = (dglu * sgg).astype(BF16)
        dg_ref[:, CC:] = (dglu * gv * sgg * (1.0 - sgg)).astype(BF16)

    row = lambda w: pl.BlockSpec((1, w), lambda i: (0, 0))
    tile = lambda w: pl.BlockSpec((tm, w), lambda i: (i, 0))
    nxt = lambda i: jnp.minimum((i + 1) * (tm // HALO), nblkh - 1)
    const = lambda r, w: pl.BlockSpec((r, w), lambda i: (0, 0))
    ddrow = pl.BlockSpec((1, 8, 1, tm), lambda i: (i, 0, 0, 0))
    return pl.pallas_call(
        body, name="mix_bwd", grid=(T // tm,),
        out_shape=[_sds((T, WA), BF16), _sds((T // tm, 8, 1, tm), F32), _sds((T, WA), BF16), _sds((T // tm, 8, 1, tm), F32),
                   _sds((T, 2 * CC), BF16), _sds((8, WA), F32), _sds((32, CC), F32), _sds((8, CC), F32)],
        in_specs=[tile(D), pl.BlockSpec((HALO, CC), lambda i: (nxt(i), 3)), tile(WA), tile(WA), tile(2 * CC),
                  tile(CC), pl.BlockSpec((HALO, CC), lambda i: (nxt(i), 0)),
                  row(WA), row(WA), const(32, CC), row(CC), row(CC)],
        out_specs=[tile(WA), ddrow, tile(WA), ddrow, tile(2 * CC), const(8, WA), const(32, CC), const(8, CC)],
        scratch_shapes=[pltpu.VMEM((RE, CC), F32), pltpu.VMEM((8, RE, CC), F32)],
        compiler_params=_cp("arbitrary"))(dy, dy, o_a, o_b, p_rest, cv, cv, g_a, g_b, cw, ng, nbias)


def _attn_bwd(p_qkv, col0, tab, do, lse, dd, kat, S, aq=None, ak=None, name="attn_bwd"):
    T = p_qkv.shape[0]
    n_tab, TK, TB = tab.shape
    nb, nkb, r = S // TB, S // TK, TK // TB
    B = T // S
    fox = aq is not None

    def body(*refs):
        if fox:
            (q_ref, k_ref, v_ref, tab_ref, do_ref, l_ref, dd_ref, kat_ref, aq_ref, ak_ref,
             dq_ref, dk_ref, dv_ref, dqa_ref, dka_ref, dqt_scr, dk_scr, dv_scr, sd_scr) = refs
        else:
            (q_ref, k_ref, v_ref, tab_ref, do_ref, l_ref, dd_ref, kat_ref,
             dq_ref, dk_ref, dv_ref, dqt_scr, dk_scr, dv_scr, sd_scr) = refs
        j = pl.program_id(2)
        hms = _head_masks()
        first_q_block = r * j

        @pl.when(j == 0)
        def _():
            dqt_scr[...] = jnp.zeros_like(dqt_scr)

        dk_scr[...] = jnp.zeros_like(dk_scr)
        dv_scr[...] = jnp.zeros_like(dv_scr)
        k2, v2 = k_ref[...], v_ref[...]
        ka = [jnp.where(hm, k2, ak_ref[0] if fox else jnp.zeros_like(k2)) for hm in hms]
        kat = [jnp.concatenate([kat_ref[0, 0, hh, t] for t in range(r)], axis=1) if r > 1 else kat_ref[0, 0, hh, 0]
               for hh in range(2)]

        def operands(i):
            ic = jnp.minimum(i, nb - 1)
            rows = pl.ds(pl.multiple_of(ic * TB, TB), TB)
            q2 = q_ref[rows, :] * 0.125
            do2 = do_ref[rows, :]
            qas = [jnp.where(hms[hh], q2, aq_ref[0, rows, :] if fox else jnp.zeros_like(q2)) for hh in range(2)]
            dohs = [jnp.where(hms[hh], do2, jnp.zeros_like(do2)) for hh in range(2)]
            return ic, qas, dohs

        def scores(i):
            ic, qas, dohs = operands(i)
            bias = tab_ref[jnp.where(i < nb, i - first_q_block, n_tab - 1)]
            return [(_dot_nt(ka[hh], qas[hh]) + bias, _dot_nt(v2, dohs[hh])) for hh in range(2)]

        def absorb(i, sd):
            ic, qas, dohs = operands(i)
            for hh in range(2):
                st, dpt = sd[hh]
                pt = jnp.exp(st - l_ref[ic, hh])
                dsb = (pt * (dpt - dd_ref[ic, hh])).astype(BF16)
                dv_scr[...] += _dot(pt.astype(BF16), dohs[hh])
                dk_scr[hh] += _dot(dsb, qas[hh])
                dqt_scr[hh, ic] += _dot(kat[hh], dsb)

        def q_pair(ii, carry):
            i0 = first_q_block + 2 * ii
            sd0 = [(sd_scr[hh, 0], sd_scr[hh, 1]) for hh in range(2)]
            sd1 = scores(i0 + 1)
            absorb(i0, sd0)
            nxt = scores(i0 + 2)
            for hh in range(2):
                sd_scr[hh, 0], sd_scr[hh, 1] = nxt[hh]
            absorb(i0 + 1, sd1)
            return carry

        first = scores(first_q_block)
        for hh in range(2):
            sd_scr[hh, 0], sd_scr[hh, 1] = first[hh]
        lax.fori_loop(0, (nb - first_q_block + 1) // 2, q_pair, 0)
        dk_ref[...] = jnp.where(hms[0], dk_scr[0], dk_scr[1]).astype(BF16)
        dv_ref[...] = dv_scr[...].astype(BF16)
        if fox:
            dka_ref[...] = jnp.where(hms[0], dk_scr[1], dk_scr[0])

        @pl.when(j == nkb - 1)
        def _():
            for ii in range(nb):
                t0, t1 = dqt_scr[0, ii].T, dqt_scr[1, ii].T
                dq_ref[ii * TB:(ii + 1) * TB, :] = jnp.where(hms[0], t0, t1) * 0.125
                if fox:
                    dqa_ref[ii * TB:(ii + 1) * TB, :] = jnp.where(hms[0], t1, t0)

    seq = lambda c: pl.BlockSpec((S, 128), lambda b, p, j: (b, c + p))
    kvb = lambda c: pl.BlockSpec((TK, 128), lambda b, p, j: (b * nkb + j, c + p))
    stat = pl.BlockSpec((nb, 2, 1, TB), lambda b, p, j: (b, p, 0, 0))
    in_specs = [seq(col0), kvb(col0 + NPAIR), kvb(col0 + 2 * NPAIR), pl.BlockSpec(tab.shape, lambda b, p, j: (0, 0, 0)),
                seq(0), stat, stat, pl.BlockSpec((1, 1, 2, r, 128, TB), lambda b, p, j: (b, p, 0, j, 0, 0))]
    args = [p_qkv, p_qkv, p_qkv, tab, do, lse, dd, kat]
    out_shape = [_sds((T, WA), F32), _sds((T, WA), BF16), _sds((T, WA), BF16)]
    out_specs = [seq(0), kvb(0), kvb(0)]
    if fox:
        in_specs += [pl.BlockSpec((1, S, 128), lambda b, p, j: (p, b, 0)), pl.BlockSpec((1, TK, 128), lambda b, p, j: (p, b * nkb + j, 0))]
        args += [aq, ak]
        out_shape += [_sds((T, WA), F32), _sds((T, WA), F32)]
        out_specs += [seq(0), kvb(0)]
    return pl.pallas_call(
        body, name=name, grid=(B, NPAIR, nkb), out_shape=out_shape, in_specs=in_specs, out_specs=out_specs,
        scratch_shapes=[pltpu.VMEM((2, nb, 128, TB), F32), pltpu.VMEM((2, TK, 128), F32), pltpu.VMEM((TK, 128), F32),
                        pltpu.VMEM((2, 2, TK, TB), F32)],
        compiler_params=_cp("parallel", "parallel", "arbitrary"))(*args)


def _gate_bwd(dqa, dka, p_rest, bf, S, TB):
    T = p_rest.shape[0]
    B, nb = T // S, S // TB

    def body(dqa_ref, dka_ref, z_ref, b_ref, dz_ref, db_ref):
        @pl.when(pl.program_id(0) == 0)
        def _():
            db_ref[...] = jnp.zeros_like(db_ref)

        later = (lax.broadcasted_iota(jnp.int32, (TB, TB), 1) >= lax.broadcasted_iota(jnp.int32, (TB, TB), 0)).astype(F32)
        lane = lax.broadcasted_iota(jnp.int32, (1, 128), 1)
        src, head = lax.broadcasted_iota(jnp.int32, (WA, 128), 0), lax.broadcasted_iota(jnp.int32, (WA, 128), 1)
        spare0 = 128 * (head // 2) + HD * (1 - head % 2)
        pick_q = (src == spare0 + AUG_K1).astype(F32)
        pick_k = (src == spare0 + AUG_Q1).astype(F32)
        carry = jnp.zeros((1, 128), F32)
        dbs = jnp.zeros((1, 128), F32)
        for j in reversed(range(nb)):
            rows = slice(j * TB, (j + 1) * TB)
            dc = _dot(dqa_ref[rows, :], pick_q, HIGHEST) - _dot(dka_ref[rows, :], pick_k, HIGHEST)
            part = _dot(later, dc, HIGHEST)
            tot = part + carry
            carry = carry + part[0:1, :]
            z = z_ref[j * TB:(j + 1) * TB, :] + b_ref[...]
            dz = jnp.where(lane < 6, tot * _sigmoid(-z), 0.0)
            dz_ref[j * TB:(j + 1) * TB, :] = dz.astype(BF16)
            dbs = dbs + jnp.sum(dz, axis=0, keepdims=True)
        db_ref[...] += dbs

    return pl.pallas_call(
        body, name="gate_bwd", grid=(B,), out_shape=[_sds((T, 128), BF16), _sds((1, 128), F32)],
        in_specs=[pl.BlockSpec((S, WA), lambda b: (b, 0)), pl.BlockSpec((S, WA), lambda b: (b, 0)),
                  pl.BlockSpec((S, 128), lambda b: (b, 4)), pl.BlockSpec((1, 128), lambda b: (0, 0))],
        out_specs=[pl.BlockSpec((S, 128), lambda b: (b, 0)), pl.BlockSpec((1, 128), lambda b: (0, 0))],
        compiler_params=_cp("arbitrary"))(dqa, dka, p_rest, bf)


def _adamw(parts, w, m, v, name, row0=0, prev=None):
    npart, Rp, C = parts.shape
    R = w.shape[0]
    tr = Rp
    for cand in (256, 128, 64, 32, 16, 8):
        if Rp % cand == 0 and row0 % cand == 0 and cand * C * 4 <= (1 << 20):
            tr = cand
            break
    if Rp == R and R * C * 4 <= (1 << 20):
        tr = R
    assert Rp % tr == 0 and row0 % tr == 0, (name, Rp, row0, tr)
    b0 = row0 // tr

    def body(p_ref, w_ref, m_ref, v_ref, *rest):
        g_ref, d_ref, mo_ref, vo_ref = rest[-4:]
        g = p_ref[0].astype(F32)
        for k in range(1, npart):
            g = g + p_ref[k].astype(F32)
        mn = ADAM_B1 * m_ref[...] + (1.0 - ADAM_B1) * g
        vn = ADAM_B2 * v_ref[...] + (1.0 - ADAM_B2) * (g * g)
        m_hat = mn / (1.0 - ADAM_B1 ** ADAM_STEP)
        v_hat = vn / (1.0 - ADAM_B2 ** ADAM_STEP)
        g_ref[...] = g
        d_ref[...] = -ADAM_LR * (m_hat / (jnp.sqrt(v_hat) + ADAM_EPS) + ADAM_WD * w_ref[...])
        mo_ref[...] = mn
        vo_ref[...] = vn

    blk = pl.BlockSpec((tr, C), lambda i: (b0 + i, 0))
    prev = list(prev) if prev is not None else []
    return pl.pallas_call(
        body, name=name, grid=(Rp // tr,), out_shape=[_sds((R, C), F32)] * 4,
        in_specs=[pl.BlockSpec((npart, tr, C), lambda i: (0, i, 0)), blk, blk, blk] + [_ANY] * len(prev), out_specs=[blk] * 4,
        input_output_aliases={4 + i: i for i in range(len(prev))},
        compiler_params=_cp("parallel"))(parts, w, m, v, *prev)


def _pad_w_in(w):
    z = jnp.zeros(w.shape[:-1] + (NPAD - N_IN,), w.dtype)
    return jnp.concatenate([w[..., :O_FA], w[..., O_QB:], w[..., O_FA:O_QB], z], axis=-1)


def _unpad_w_in(w):
    n_mid = N_IN - O_QB
    return jnp.concatenate([w[..., :O_FA], w[..., O_FA + n_mid:O_FA + n_mid + 6], w[..., O_FA:O_FA + n_mid]], axis=-1)


def _local_step(x, tgt, W, S, late_weights=None, on_layer_grads=None, on_ffn_grads=None):
    T = x.shape[0]
    TB = min(256, S)
    TW = min(2 * TB, S)
    tab_fox, tab_dil = _bias_table(S, TW, TB, True), _bias_table(S, TW, TB, False)
    tabq_fox, tabq_dil = _bias_table(S, TB, TW, True), _bias_table(S, TB, TW, False)
    saved = []
    h = _rms_fwd(x, W["ln1_g"][0])
    for l in range(DEPTH):
        p_qkv, p_rest = _mm_in(h, W["w_in"][l])
        aq, ak = _gate_fwd(p_rest, W["b_forget"][l], S, TB)
        o_a, lse_a, kat_a = _attn_fwd(p_qkv, 0, tabq_fox, S, aq, ak, name="fox_fwd")
        o_b, lse_b, kat_b = _attn_fwd(p_qkv, 3 * NPAIR, tabq_dil, S, name="dil_fwd")
        y, cv = _mix_fwd(o_a, o_b, p_rest, W["g_out_fox"][l], W["g_out_dil"][l], W["conv_w"][l], W["conv_b"][l],
                     W["cnorm_g"][l], W["cnorm_b"][l], S)
        x1, h2 = _mm_res(y[None], W["w_o"][l][None], x, W["ln2_g"][l], name="mm_o")
        if l == 0 and late_weights is not None:
            W = late_weights(W, h2)
        u, c, hid = _up_fwd(h2, W["w_up"][l], W["ffn_conv_w"][l], W["ffn_conv_b"][l], S)
        g_next = W["ln1_g"][l + 1] if l + 1 < DEPTH else W["g_final"]
        x2, h_next = _mm_res(hid, W["w_down"][l], x1, g_next, name="mm_down")
        saved.append(dict(x=x, h=h, p_qkv=p_qkv, p_rest=p_rest, aq=aq, ak=ak, o_a=o_a, lse_a=lse_a, o_b=o_b,
                          lse_b=lse_b, kat_a=kat_a, kat_b=kat_b, y=y, cv=cv, x1=x1, h2=h2, u=u, c=c, hid=hid))
        x, h = x2, h_next
    dx, loss, dg_final = _final_loss(x, tgt, W["g_final"])
    G = {k: [None] * DEPTH for k in ("ln1_g", "w_in", "b_forget", "g_out_fox", "g_out_dil", "conv_w", "conv_b", "cnorm_g",
                                     "cnorm_b", "w_o", "ln2_g", "w_up", "ffn_conv_w", "ffn_conv_b", "w_down")}
    G["g_final"] = dg_final
    for l in reversed(range(DEPTH)):
        sv = saved[l]
        du, dcw = _down_bwd(dx, W["w_down"][l], sv["u"], sv["c"], W["ffn_conv_w"][l], S)
        G["w_down"][l] = _wgrad(sv["hid"], dx[None], D, "wg_down", ga=NJ)[:, 0]
        G["ffn_conv_w"][l] = dcw[:, :, 0:3, :]
        G["ffn_conv_b"][l] = dcw[:, :, 3, :]
        G["w_up"][l] = _wgrad(du.reshape(2 * NJ, T, FB), sv["h2"][None], D, "wg_up", ga=NJ)[:, 0]
        token = on_ffn_grads(l, G) if on_ffn_grads is not None else None
        ln2 = W["ln2_g"][l] if token is None else W["ln2_g"][l] + token[0, 0]
        dx1, G["ln2_g"][l] = _dx_rms(du, W["w_up"][l], sv["x1"], ln2, dx, True, "dx_up")
        G["w_o"][l] = _wgrad(sv["y"][None], dx1[None], D, "wg_o")[0, 0]
        dy = _mm_nt(dx1, W["w_o"][l], "mm_dy")
        do_a, dd_a, do_b, dd_b, dgvgg, dgn, dcvw, dcn = _mix_bwd(
            dy, sv["o_a"], sv["o_b"], sv["p_rest"], sv["cv"], W["g_out_fox"][l], W["g_out_dil"][l], W["conv_w"][l],
            W["cnorm_g"][l], W["cnorm_b"][l], S)
        G["g_out_fox"][l], G["g_out_dil"][l] = dgn[0], dgn[1]
        G["conv_w"][l], G["conv_b"][l] = dcvw[:CONV_K], dcvw[CONV_K]
        G["cnorm_g"][l], G["cnorm_b"][l] = dcn[0], dcn[1]
        dq_a, dk_a, dv_a, dqa, dka = _attn_bwd(sv["p_qkv"], 0, tab_fox, do_a, sv["lse_a"], dd_a, sv["kat_a"], S, sv["aq"], sv["ak"],
                                               name="fox_bwd")
        dq_b, dk_b, dv_b = _attn_bwd(sv["p_qkv"], 3 * NPAIR, tab_dil, do_b, sv["lse_b"], dd_b, sv["kat_b"], S, name="dil_bwd")
        dfa, db = _gate_bwd(dqa, dka, sv["p_rest"], W["b_forget"][l], S, TB)
        G["b_forget"][l] = db[0, :6]
        dp = jnp.concatenate([dq_a.astype(BF16), dk_a, dv_a, dq_b.astype(BF16), dk_b, dv_b, dgvgg, dfa,
                              jnp.zeros((T, 128), BF16)], axis=1)
        G["w_in"][l] = _wgrad(sv["h"][None], dp[None], NPAD, "wg_in")[0, 0]
        token = on_layer_grads(l, G) if on_layer_grads is not None else None
        ln1 = W["ln1_g"][l] if token is None else W["ln1_g"][l] + token[0, 0]
        dx, G["ln1_g"][l] = _dx_rms(dp, W["w_in"][l], sv["x"], ln1, dx1, False, "dx_in")
    return loss, dx, G


_MATMUL_W = ("w_in", "w_o", "w_up", "w_down")
_SHARDED = _MATMUL_W + ("conv_w", "ffn_conv_w")
_SMALL = ("ln1_g", "b_forget", "g_out_fox", "g_out_dil", "conv_b", "cnorm_g", "cnorm_b", "ln2_g", "ffn_conv_b", "g_final")
_ORDER = ("ln1_g", "w_in", "b_forget", "g_out_fox", "g_out_dil", "conv_w", "conv_b", "cnorm_g", "cnorm_b", "w_o", "ln2_g",
          "w_up", "ffn_conv_w", "ffn_conv_b", "w_down", "g_final")


def _kernel_ready(k, zone):
    if k == "w_in":
        return zone.reshape(D, NPAD)
    if k == "w_o":
        return zone.reshape(D, D)
    if k == "w_up":
        return zone.reshape(2, NJ, D, FB)
    if k == "w_down":
        return zone.reshape(NJ, FB, D)
    if k == "conv_w":
        cw = jnp.transpose(zone.reshape(NDEV, CONV_K, CC // NDEV), (1, 0, 2)).reshape(CONV_K, CC)
        return jnp.concatenate([cw, jnp.zeros((1, CC), F32)], axis=0)
    return zone.reshape(2, NJ, 3, FB)


def _full_weights(P, gathered):
    W = {}
    row = lambda a: [a[l][None, :] for l in range(DEPTH)]
    for k in _SHARDED:
        W[k] = [None if gathered[k][l] is None else _kernel_ready(k, gathered[k][l]) for l in range(DEPTH)]
    W["ffn_conv_b"] = [P["ffn_conv_b"][l].reshape(2, NJ, 1, FB) for l in range(DEPTH)]
    W["b_forget"] = [jnp.concatenate([P["b_forget"][l], jnp.zeros((128 - 6,), F32)])[None, :] for l in range(DEPTH)]
    for k in ("ln1_g", "ln2_g", "g_out_fox", "g_out_dil", "conv_b", "cnorm_g", "cnorm_b"):
        W[k] = row(P[k])
    W["g_final"] = P["g_final"][None, :]
    return W


def kernel(x, ln1_g, w_in, b_forget, g_out_fox, g_out_dil, conv_w, conv_b, cnorm_g, cnorm_b, w_o, ln2_g, w_up, ffn_conv_w, ffn_conv_b, w_down, g_final, loss_target, m_ln1_g, m_w_in, m_b_forget, m_g_out_fox, m_g_out_dil, m_conv_w, m_conv_b, m_cnorm_g, m_cnorm_b, m_w_o, m_ln2_g, m_w_up, m_ffn_conv_w, m_ffn_conv_b, m_w_down, m_g_final, v_ln1_g, v_w_in, v_b_forget, v_g_out_fox, v_g_out_dil, v_conv_w, v_conv_b, v_cnorm_g, v_cnorm_b, v_w_o, v_ln2_g, v_w_up, v_ffn_conv_w, v_ffn_conv_b, v_w_down, v_g_final):
    P = dict(ln1_g=ln1_g, w_in=w_in, b_forget=b_forget, g_out_fox=g_out_fox, g_out_dil=g_out_dil, conv_w=conv_w, conv_b=conv_b,
             cnorm_g=cnorm_g, cnorm_b=cnorm_b, w_o=w_o, ln2_g=ln2_g, w_up=w_up, ffn_conv_w=ffn_conv_w, ffn_conv_b=ffn_conv_b,
             w_down=w_down, g_final=g_final)
    M = dict(ln1_g=m_ln1_g, w_in=m_w_in, b_forget=m_b_forget, g_out_fox=m_g_out_fox, g_out_dil=m_g_out_dil, conv_w=m_conv_w,
             conv_b=m_conv_b, cnorm_g=m_cnorm_g, cnorm_b=m_cnorm_b, w_o=m_w_o, ln2_g=m_ln2_g, w_up=m_w_up, ffn_conv_w=m_ffn_conv_w,
             ffn_conv_b=m_ffn_conv_b, w_down=m_w_down, g_final=m_g_final)
    V = dict(ln1_g=v_ln1_g, w_in=v_w_in, b_forget=v_b_forget, g_out_fox=v_g_out_fox, g_out_dil=v_g_out_dil, conv_w=v_conv_w,
             conv_b=v_conv_b, cnorm_g=v_cnorm_g, cnorm_b=v_cnorm_b, w_o=v_w_o, ln2_g=v_ln2_g, w_up=v_w_up, ffn_conv_w=v_ffn_conv_w,
             ffn_conv_b=v_ffn_conv_b, w_down=v_w_down, g_final=v_g_final)
    Bl, S, _ = x.shape
    T = Bl * S

    def shard(k, l):
        if k == "w_in":
            return _pad_w_in(P[k][l].astype(BF16))
        return P[k][l].astype(BF16) if k in _MATMUL_W else P[k][l]

    early = [("w_in", 0), ("w_o", 0)] + [(k, l) for k in ("conv_w", "ffn_conv_w") for l in range(DEPTH)]
    late = [(k, 0) for k in ("w_up", "w_down")] + [(k, 1) for k in _MATMUL_W]
    gathered = {k: [None] * DEPTH for k in _SHARDED}
    plan_early = _Plan([[shard(k, l)] for k, l in early], True)
    early_zones = _gather_two_level(plan_early, "gather_early")
    for (k, l), zone in zip(early, early_zones):
        gathered[k][l] = zone
    late_shards, _ = lax.optimization_barrier(([shard(k, l) for k, l in late], early_zones[0]))
    plan_late = _Plan([[a] for a in late_shards], True)
    late_handle, late_token = _exchange_start(plan_late, "gather_late_start", plan_late.zones_with_own())
    W = _full_weights(P, gathered)
    W["ln1_g"][0] = W["ln1_g"][0] + late_token[0, 0]

    def late_weights(W, after):
        zones = _exchange_wait(plan_late, "gather_late_wait", late_handle, after)
        W = dict(W)
        for (k, l), zone in zip(late, zones):
            W[k] = list(W[k])
            W[k][l] = _kernel_ready(k, zone)
        return W

    def owner_block(G, l, k):
        if k == "w_in":
            blk = _unpad_w_in(G[k][l]).reshape(NDEV, D // NDEV, N_IN)
        elif k == "w_o":
            blk = G[k][l].reshape(NDEV, D // NDEV, D)
        elif k == "w_up":
            blk = G[k][l]
        elif k == "w_down":
            blk = G[k][l].reshape(NDEV, DFF // NDEV, D)
        elif k == "conv_w":
            blk = jnp.transpose(G[k][l].reshape(CONV_K, NDEV, CC // NDEV), (1, 0, 2))
        else:
            blk = G[k][l].reshape(NDEV, 3, FB)
        return blk.astype(GRAD_DTYPE)

    flying = []

    def start_scatter(tag, G, l, keys):
        plan = _Plan([[owner_block(G, l, k)] for k in keys], False)
        handle, token = _exchange_start(plan, "scatter_%s_start" % tag, plan.zones_with_own())
        flying.append((tag, l, keys, plan, handle))
        return token

    def on_layer_grads(l, G):
        return start_scatter("l1", G, l, _MATMUL_W) if l == 1 else start_scatter("l0_rest", G, l, ("w_in", "w_o"))

    def on_ffn_grads(l, G):
        return start_scatter("l0_ffn", G, l, ("w_up", "w_down")) if l == 0 else None

    loss, dx, G = _local_step(x.reshape(T, D), loss_target.reshape(T, D), W, S, late_weights, on_layer_grads, on_ffn_grads)

    parts = {}
    for tag, l, keys, plan, handle in flying:
        for k, zone in zip(keys, _exchange_wait(plan, "scatter_%s_wait" % tag, handle, dx)):
            parts[(k, l)] = zone
    plan_last = _Plan([[owner_block(G, l, k) for l in range(DEPTH)] for k in ("conv_w", "ffn_conv_w")], False)
    last = _exchange(plan_last, "scatter_last")
    small_parts_sharded = {"conv_w": last[0], "ffn_conv_w": last[1]}

    small = [jnp.stack(G[k]).reshape(-1) for k in _SMALL[:-1]] + [G["g_final"].reshape(-1), loss[0, :1]]
    sizes = [int(s.shape[0]) for s in small]
    flat = jnp.concatenate(small)
    n_small = -(-flat.shape[0] // 1024) * 1024
    flat = jnp.concatenate([flat, jnp.zeros((n_small - flat.shape[0],), F32)]).reshape(n_small // 128, 128)
    small_parts = _exchange(_Plan([[flat]], True), "gather_small")[0]

    out_g, out_d, out_m, out_v = {}, {}, {}, {}
    sg = _sum_parts(small_parts.reshape(NDEV, n_small // 128, 128), "sum_small").reshape(-1)
    off = 0
    summed = {}
    for k, n in zip(_SMALL + ("loss",), sizes):
        summed[k] = sg[off:off + n]
        off += n
    for k in _ORDER:
        shp = P[k].shape
        C = shp[-1]
        if k in _MATMUL_W:
            swap = k == "w_up"
            R0, Cw = (shp[2], shp[1]) if swap else (shp[1], shp[2])
            flat2 = lambda t: (jnp.swapaxes(t, 1, 2) if swap else t).reshape(DEPTH * R0, Cw)
            res = None
            for l in reversed(range(DEPTH)):
                res = _adamw(parts[(k, l)].reshape(NDEV, R0, Cw), flat2(P[k]), flat2(M[k]), flat2(V[k]),
                             "adamw_%s_l%d" % (k, l), row0=l * R0, prev=res)
            if swap:
                res = [jnp.swapaxes(t.reshape(DEPTH, R0, Cw), 1, 2) for t in res]
        else:
            pr = small_parts_sharded[k].reshape((NDEV, -1, C)) if k in _SHARDED else summed[k].reshape((1, -1, C))
            R = pr.shape[1]
            res = _adamw(pr, P[k].reshape(R, -1), M[k].reshape(R, -1), V[k].reshape(R, -1), "adamw_" + k)
        out_g[k], out_d[k], out_m[k], out_v[k] = (t.reshape(shp) for t in res)

    loss_out = summed["loss"].reshape(())
    grad_x = dx.reshape(Bl, S, D)
    return (loss_out, grad_x, *[out_g[k] for k in _ORDER], *[out_d[k] for k in _ORDER], *[out_m[k] for k in _ORDER],
            *[out_v[k] for k in _ORDER])
```
